```python
import jax
import jax.numpy as jnp
from jax import lax
import numpy as np

D_MODEL = 2048
BATCH = 8
SEQ = 8192
DEPTH = 2

GRID_W = 64
CTX_LEN = 256
EPS = 1e-6

N_EVEN = (DEPTH + 1) // 2
N_ODD = DEPTH // 2

A_WIDTH = D_MODEL // 2
A_HEAD_DIM = 128
A_HEADS = A_WIDTH // A_HEAD_DIM
A_CHUNK = 32
B_WIDTH = D_MODEL // 2
POOL_WINDOWS = (2, 4, 8, 16)
B_GROUPS = len(POOL_WINDOWS)
B_GROUP_DIM = B_WIDTH // B_GROUPS
EVEN_IN = 5 * A_WIDTH + B_WIDTH
EVEN_MIX = A_WIDTH + B_WIDTH

C_HEAD_DIM = 128
C_KEY_HEADS = D_MODEL // C_HEAD_DIM
C_VALUE_HEADS = 2 * C_KEY_HEADS
C_KEY_WIDTH = C_KEY_HEADS * C_HEAD_DIM
C_VALUE_WIDTH = C_VALUE_HEADS * C_HEAD_DIM
C_QKV = 2 * C_KEY_WIDTH + C_VALUE_WIDTH
C_CONV = 4
C_CHUNK = 64
ODD_IN = C_QKV + C_VALUE_WIDTH + 4 * C_VALUE_HEADS

FFN_HIDDEN = -(-8 * D_MODEL // (3 * 256)) * 256

kernel_name = "hybrid_hgrn2_pool_gdn_diffusion_trunk"


def rmsnorm(x, w):
    xf = x.astype(jnp.float32)
    y = xf * lax.rsqrt(jnp.mean(xf * xf, axis=-1, keepdims=True) + EPS)
    return (y * w.astype(jnp.float32)).astype(x.dtype)


def l2norm(x):
    return x * lax.rsqrt(jnp.sum(x * x, axis=-1, keepdims=True) + EPS)


def modulate(h, shift, scale):
    return h * (1.0 + scale) + shift


def split_heads(t, d):
    return t.reshape(t.shape[:-1] + (t.shape[-1] // d, d))


def swiglu(h, w13, w2):
    gate, up = jnp.split(h @ w13, 2, axis=-1)
    return (jax.nn.silu(gate) * up) @ w2


def centred_short_conv(u, w):
    length = u.shape[1]
    left = C_CONV // 2
    up = jnp.pad(u, ((0, 0), (left, C_CONV - 1 - left), (0, 0)))
    out = up[:, 0:length] * w[0]
    for j in range(1, C_CONV):
        out = out + up[:, j:j + length] * w[j]
    return out


def hgrn2_chunk_scan(q, k, v, log_f, s0):
    bsz, length, heads, _ = q.shape
    dv = v.shape[-1]
    n = length // A_CHUNK
    rs = lambda t: t.reshape(bsz, n, A_CHUNK, heads, t.shape[-1]).transpose(1, 0, 3, 2, 4)
    q, k, v, log_f = rs(q), rs(k), rs(v), rs(log_f)
    b = jnp.cumsum(log_f, axis=-2)
    b_last = b[..., -1:, :]
    b_mid = b[..., A_CHUNK // 2 - 1:A_CHUNK // 2, :]
    causal = jnp.tril(jnp.ones((A_CHUNK, A_CHUNK), bool))
    scores = jnp.einsum('nbhtd,nbhsd->nbhts', q * jnp.exp(b - b_mid), k * jnp.exp(b_mid - b))
    o_intra = jnp.einsum('nbhts,nbhsv->nbhtv', jnp.where(causal, scores, 0.0), v)
    q_in = q * jnp.exp(b)
    k_out = k * jnp.exp(b_last - b)
    decay = jnp.exp(b_last)

    def step(state, inp):
        qi, ki, vi, dl = inp
        o = jnp.einsum('bhtd,bhdv->bhtv', qi, state)
        state = state * dl[..., 0, :, None] + jnp.einsum('bhsd,bhsv->bhdv', ki, vi)
        return state, o

    s_final, o_inter = lax.scan(step, s0, (q_in, k_out, v, decay))
    o = (o_intra + o_inter).transpose(1, 0, 3, 2, 4).reshape(bsz, length, heads, dv)
    return o, s_final


def gated_delta_chunk_scan(q, k, v, g, beta, s0):
    bsz, length, heads, _ = q.shape
    dv = v.shape[-1]
    n = length // C_CHUNK
    rs = lambda t: t.reshape(bsz, n, C_CHUNK, heads, t.shape[-1]).transpose(1, 0, 3, 2, 4)
    q, k, v = rs(q), rs(k), rs(v)
    g = rs(g[..., None])[..., 0]
    beta = rs(beta[..., None])[..., 0]
    gc = jnp.cumsum(g, axis=-1)
    causal = jnp.tril(jnp.ones((C_CHUNK, C_CHUNK), bool))
    strict = jnp.tril(jnp.ones((C_CHUNK, C_CHUNK), bool), -1)
    diff = gc[..., :, None] - gc[..., None, :]
    decay = jnp.where(causal, jnp.exp(jnp.where(causal, diff, 0.0)), 0.0)
    k_beta = k * beta[..., None]
    v_beta = v * beta[..., None]
    a_low = jnp.where(strict, jnp.einsum('nbhtd,nbhsd->nbhts', k_beta, k) * decay, 0.0)
    t_mat = jnp.eye(C_CHUNK, dtype=a_low.dtype) + a_low
    u = lax.linalg.triangular_solve(t_mat, v_beta, left_side=True, lower=True, unit_diagonal=True)
    w = lax.linalg.triangular_solve(t_mat, k_beta * jnp.exp(gc)[..., None], left_side=True,
                                    lower=True, unit_diagonal=True)
    qk = jnp.where(causal, jnp.einsum('nbhtd,nbhsd->nbhts', q, k) * decay, 0.0)
    q_dec = q * jnp.exp(gc)[..., None]
    k_dec = k * jnp.exp(gc[..., -1:] - gc)[..., None]
    g_last = jnp.exp(gc[..., -1])

    def step(state, inp):
        u_i, w_i, qk_i, qd_i, kd_i, gl_i = inp
        v_new = u_i - jnp.einsum('bhtd,bhdv->bhtv', w_i, state)
        o = jnp.einsum('bhtd,bhdv->bhtv', qd_i, state) + jnp.einsum('bhts,bhsv->bhtv', qk_i, v_new)
        state = state * gl_i[..., None, None] + jnp.einsum('bhsd,bhsv->bhdv', kd_i, v_new)
        return state, o

    s_final, o = lax.scan(step, s0, (u, w, qk, q_dec, k_dec, g_last))
    o = o.transpose(1, 0, 3, 2, 4).reshape(bsz, length, heads, dv)
    return o, s_final


def bidirectional_prefix_scan(scan_fn, ctx_f, lat_f, ctx_b, lat_b, s0):
    flip = lambda args: tuple(jnp.flip(t, axis=1) for t in args)
    oc_f, sc_f = scan_fn(*ctx_f, s0)
    ol_f, _ = scan_fn(*lat_f, sc_f)
    oc_b, sc_b = scan_fn(*flip(ctx_b), s0)
    ol_b, _ = scan_fn(*flip(lat_b), sc_b)
    return ol_f + jnp.flip(ol_b, axis=1), oc_f + jnp.flip(oc_b, axis=1)


def multiscale_pool(u, pool_w, pool_scale):
    length = u.shape[-2]
    uf = u.astype(jnp.float32).reshape(u.shape[:-1] + (B_GROUPS, B_GROUP_DIM))
    cs = jnp.concatenate([jnp.zeros_like(uf[..., :1, :, :]), jnp.cumsum(uf, axis=-3)], axis=-3)
    pos = jnp.arange(length)
    mixed = []
    for gi, win in enumerate(POOL_WINDOWS):
        lo = jnp.clip(pos - win // 2, 0, length - 1)
        hi = jnp.clip(pos + win - 1 - win // 2, 0, length - 1)
        cnt = (hi - lo + 1).astype(jnp.float32)[:, None]
        csg = cs[..., gi, :]
        mean = (jnp.take(csg, hi + 1, axis=-2) - jnp.take(csg, lo, axis=-2)) / cnt
        mixed.append(mean - uf[..., gi, :])
    d = jnp.stack(mixed, axis=-2)
    y = jnp.einsum('...lgc,gcd->...lgd', d, pool_w.astype(jnp.float32))
    return y.reshape(u.shape) * pool_scale.astype(jnp.float32)


def hgrn2_gates(pre_f, lb):
    s = pre_f.astype(jnp.float32)
    log_f = jnp.log(lb + (1.0 - lb) * jax.nn.sigmoid(s))
    k = (1.0 - lb) * jax.nn.sigmoid(-s)
    return split_heads(k, A_HEAD_DIM), split_heads(log_f, A_HEAD_DIM)


def even_mixer(h_lat, h_ctx, rows, lb, w_in, a_norm, pool_w, pool_scale, w_out, need_ctx):
    bsz, seq_len, _ = h_lat.shape

    def project(h):
        p = h @ w_in
        q, f_f, f_b, i, g, u = jnp.split(
            p, [A_WIDTH, 2 * A_WIDTH, 3 * A_WIDTH, 4 * A_WIDTH, 5 * A_WIDTH], axis=-1)
        q = split_heads(jax.nn.silu(q.astype(jnp.float32)), A_HEAD_DIM)
        i = split_heads(i.astype(jnp.float32), A_HEAD_DIM)
        k_f, logf_f = hgrn2_gates(f_f, lb[0])
        k_b, logf_b = hgrn2_gates(f_b, lb[1])
        return (q, k_f, i, logf_f), (q, k_b, i, logf_b), g, u

    def readout(h, o, g, pooled):
        a_out = rmsnorm(o, a_norm) * jax.nn.silu(split_heads(g.astype(jnp.float32), A_HEAD_DIM))
        a_out = a_out.reshape(o.shape[:-2] + (A_WIDTH,))
        return jnp.concatenate([a_out, pooled], axis=-1).astype(h.dtype) @ w_out

    fwd_l, bwd_l, g_l, u_l = project(h_lat)
    fwd_c, bwd_c, g_c, u_c = project(h_ctx)
    s0 = jnp.zeros((bsz, A_HEADS, A_HEAD_DIM, A_HEAD_DIM), jnp.float32)
    o_l, o_c = bidirectional_prefix_scan(hgrn2_chunk_scan, fwd_c, fwd_l, bwd_c, bwd_l, s0)
    pooled_l = multiscale_pool(u_l.reshape(bsz, rows, GRID_W, B_WIDTH), pool_w, pool_scale)
    y_l = readout(h_lat, o_l, g_l, pooled_l.reshape(bsz, seq_len, B_WIDTH))
    if not need_ctx:
        return y_l, None
    y_c = readout(h_ctx, o_c, g_c, multiscale_pool(u_c, pool_w, pool_scale))
    return y_l, y_c


def odd_mixer(h_lat, h_ctx, w_in, conv_w, a_log, dt_bias, norm_w, w_out, need_ctx):
    bsz = h_lat.shape[0]
    rep = C_VALUE_HEADS // C_KEY_HEADS
    a_rate = jnp.exp(a_log.astype(jnp.float32))
    dtb = dt_bias.astype(jnp.float32)

    def project(h):
        p = h @ w_in
        qkv, z, gates = jnp.split(p, [C_QKV, C_QKV + C_VALUE_WIDTH], axis=-1)
        qkv = jax.nn.silu(centred_short_conv(qkv, conv_w).astype(jnp.float32))
        q, k, v = jnp.split(qkv, [C_KEY_WIDTH, 2 * C_KEY_WIDTH], axis=-1)
        q = jnp.repeat(l2norm(split_heads(q, C_HEAD_DIM)) * C_HEAD_DIM ** -0.5, rep, axis=2)
        k = jnp.repeat(l2norm(split_heads(k, C_HEAD_DIM)), rep, axis=2)
        v = split_heads(v, C_HEAD_DIM)
        a_f, a_b, b_f, b_b = jnp.split(gates.astype(jnp.float32), 4, axis=-1)
        g_f = -a_rate[0] * jax.nn.softplus(a_f + dtb[0])
        g_b = -a_rate[1] * jax.nn.softplus(a_b + dtb[1])
        return (q, k, v, g_f, jax.nn.sigmoid(b_f)), (q, k, v, g_b, jax.nn.sigmoid(b_b)), z

    def readout(h, o, z):
        y = rmsnorm(o, norm_w) * jax.nn.silu(split_heads(z.astype(jnp.float32), C_HEAD_DIM))
        return y.reshape(o.shape[:-2] + (C_VALUE_WIDTH,)).astype(h.dtype) @ w_out

    fwd_l, bwd_l, z_l = project(h_lat)
    fwd_c, bwd_c, z_c = project(h_ctx)
    s0 = jnp.zeros((bsz, C_VALUE_HEADS, C_HEAD_DIM, C_HEAD_DIM), jnp.float32)
    o_l, o_c = bidirectional_prefix_scan(gated_delta_chunk_scan, fwd_c, fwd_l, bwd_c, bwd_l, s0)
    y_l = readout(h_lat, o_l, z_l)
    if not need_ctx:
        return y_l, None
    return y_l, readout(h_ctx, o_c, z_c)


def _fwd_setup_inputs(seed: int = 0) -> dict:
    key = jax.random.key(seed)
    ks = jax.random.split(key, 24)
    nrm = lambda k, shape, s: jax.random.normal(k, shape, jnp.float32) * s
    dt = jnp.exp(jax.random.uniform(ks[15], (N_ODD, 2, C_VALUE_HEADS), jnp.float32,
                                    minval=float(np.log(1e-3)), maxval=float(np.log(1e-1))))
    return {
        "x": nrm(ks[0], (BATCH, SEQ, D_MODEL), 1.0),
        "c": nrm(ks[1], (BATCH, D_MODEL), 1.0),
        "ctx": nrm(ks[2], (BATCH, CTX_LEN, D_MODEL), 1.0),
        "c_ctx": nrm(ks[3], (D_MODEL,), 1.0),
        "w_ada": nrm(ks[4], (DEPTH, D_MODEL, 6 * D_MODEL), 0.5 * D_MODEL ** -0.5),
        "b_ada": nrm(ks[5], (DEPTH, 6 * D_MODEL), 0.02),
        "norm_w": 1.0 + nrm(ks[6], (DEPTH, 4, D_MODEL), 0.02),
        "ev_w_in": nrm(ks[7], (N_EVEN, D_MODEL, EVEN_IN), D_MODEL ** -0.5),
        "ev_lb": nrm(ks[8], (2, DEPTH + 1, A_WIDTH), 0.1),
        "ev_a_norm": 1.0 + nrm(ks[9], (N_EVEN, A_HEAD_DIM), 0.02),
        "ev_pool_w": nrm(ks[10], (N_EVEN, B_GROUPS, B_GROUP_DIM, B_GROUP_DIM), B_GROUP_DIM ** -0.5),
        "ev_pool_scale": 1.0 + nrm(ks[11], (N_EVEN, B_WIDTH), 0.02),
        "ev_w_out": nrm(ks[12], (N_EVEN, EVEN_MIX, D_MODEL), EVEN_MIX ** -0.5),
        "od_w_in": nrm(ks[13], (N_ODD, D_MODEL, ODD_IN), D_MODEL ** -0.5),
        "od_conv": nrm(ks[14], (N_ODD, C_CONV, C_QKV), C_CONV ** -0.5),
        "od_A_log": jnp.log(jax.random.uniform(ks[16], (N_ODD, 2, C_VALUE_HEADS), jnp.float32,
                                               minval=1.0, maxval=16.0)),
        "od_dt_bias": dt + jnp.log(-jnp.expm1(-dt)),
        "od_norm": 1.0 + nrm(ks[17], (N_ODD, C_HEAD_DIM), 0.02),
        "od_w_out": nrm(ks[18], (N_ODD, C_VALUE_WIDTH, D_MODEL), C_VALUE_WIDTH ** -0.5),
        "ffn_w13": nrm(ks[19], (DEPTH, D_MODEL, 2 * FFN_HIDDEN), D_MODEL ** -0.5),
        "ffn_w2": nrm(ks[20], (DEPTH, FFN_HIDDEN, D_MODEL), FFN_HIDDEN ** -0.5),
    }


def _fwd_reference(x, c, ctx, c_ctx, w_ada, b_ada, norm_w, ev_w_in, ev_lb, ev_a_norm, ev_pool_w,
              ev_pool_scale, ev_w_out, od_w_in, od_conv, od_A_log, od_dt_bias, od_norm, od_w_out,
              ffn_w13, ffn_w2):
    rows = x.shape[1] // GRID_W
    lb_all = jnp.cumsum(jax.nn.softmax(ev_lb.astype(jnp.float32), axis=1), axis=1)
    silu_c = jax.nn.silu(c)
    silu_cc = jax.nn.silu(c_ctx)[None, :]
    for layer in range(DEPTH):
        need_ctx = layer < DEPTH - 1
        j = layer // 2
        m_l = [t[:, None, :] for t in jnp.split(silu_c @ w_ada[layer] + b_ada[layer], 6, axis=-1)]
        m_c = [t[:, None, :] for t in jnp.split(silu_cc @ w_ada[layer] + b_ada[layer], 6, axis=-1)]
        h_l = modulate(rmsnorm(x, norm_w[layer, 0]), m_l[0], m_l[1])
        h_c = modulate(rmsnorm(ctx, norm_w[layer, 0]), m_c[0], m_c[1])
        if layer % 2 == 0:
            y_l, y_c = even_mixer(h_l, h_c, rows, lb_all[:, layer], ev_w_in[j], ev_a_norm[j],
                                  ev_pool_w[j], ev_pool_scale[j], ev_w_out[j], need_ctx)
        else:
            y_l, y_c = odd_mixer(h_l, h_c, od_w_in[j], od_conv[j], od_A_log[j], od_dt_bias[j],
                                 od_norm[j], od_w_out[j], need_ctx)
        x = x + m_l[2] * rmsnorm(y_l, norm_w[layer, 1])
        f_l = swiglu(modulate(rmsnorm(x, norm_w[layer, 2]), m_l[3], m_l[4]), ffn_w13[layer], ffn_w2[layer])
        x = x + m_l[5] * rmsnorm(f_l, norm_w[layer, 3])
        if need_ctx:
            ctx = ctx + m_c[2] * rmsnorm(y_c, norm_w[layer, 1])
            f_c = swiglu(modulate(rmsnorm(ctx, norm_w[layer, 2]), m_c[3], m_c[4]), ffn_w13[layer], ffn_w2[layer])
            ctx = ctx + m_c[5] * rmsnorm(f_c, norm_w[layer, 3])
    return x


import jax as _jax
import jax.numpy as _jnp

TWIN_FORMAT = 'train_step'
FWD_PARAMS = ['x', 'c', 'ctx', 'c_ctx', 'w_ada', 'b_ada', 'norm_w', 'ev_w_in', 'ev_lb', 'ev_a_norm', 'ev_pool_w', 'ev_pool_scale', 'ev_w_out', 'od_w_in', 'od_conv', 'od_A_log', 'od_dt_bias', 'od_norm', 'od_w_out', 'ffn_w13', 'ffn_w2']
TWIN_WEIGHTS = ['c_ctx', 'w_ada', 'b_ada', 'norm_w', 'ev_w_in', 'ev_lb', 'ev_a_norm', 'ev_pool_w', 'ev_pool_scale', 'ev_w_out', 'od_w_in', 'od_conv', 'od_A_log', 'od_dt_bias', 'od_norm', 'od_w_out', 'ffn_w13', 'ffn_w2']
TWIN_DIFF_INPUT = 'x'
TWIN_INPUTS = ['x', 'c', 'ctx', 'c_ctx', 'w_ada', 'b_ada', 'norm_w', 'ev_w_in', 'ev_lb', 'ev_a_norm', 'ev_pool_w', 'ev_pool_scale', 'ev_w_out', 'od_w_in', 'od_conv', 'od_A_log', 'od_dt_bias', 'od_norm', 'od_w_out', 'ffn_w13', 'ffn_w2', 'loss_target', 'm_c_ctx', 'm_w_ada', 'm_b_ada', 'm_norm_w', 'm_ev_w_in', 'm_ev_lb', 'm_ev_a_norm', 'm_ev_pool_w', 'm_ev_pool_scale', 'm_ev_w_out', 'm_od_w_in', 'm_od_conv', 'm_od_A_log', 'm_od_dt_bias', 'm_od_norm', 'm_od_w_out', 'm_ffn_w13', 'm_ffn_w2', 'v_c_ctx', 'v_w_ada', 'v_b_ada', 'v_norm_w', 'v_ev_w_in', 'v_ev_lb', 'v_ev_a_norm', 'v_ev_pool_w', 'v_ev_pool_scale', 'v_ev_w_out', 'v_od_w_in', 'v_od_conv', 'v_od_A_log', 'v_od_dt_bias', 'v_od_norm', 'v_od_w_out', 'v_ffn_w13', 'v_ffn_w2']
TWIN_OUTPUTS = ['loss', 'grad_x', 'grad_c_ctx', 'grad_w_ada', 'grad_b_ada', 'grad_norm_w', 'grad_ev_w_in', 'grad_ev_lb', 'grad_ev_a_norm', 'grad_ev_pool_w', 'grad_ev_pool_scale', 'grad_ev_w_out', 'grad_od_w_in', 'grad_od_conv', 'grad_od_A_log', 'grad_od_dt_bias', 'grad_od_norm', 'grad_od_w_out', 'grad_ffn_w13', 'grad_ffn_w2', 'delta_c_ctx', 'delta_w_ada', 'delta_b_ada', 'delta_norm_w', 'delta_ev_w_in', 'delta_ev_lb', 'delta_ev_a_norm', 'delta_ev_pool_w', 'delta_ev_pool_scale', 'delta_ev_w_out', 'delta_od_w_in', 'delta_od_conv', 'delta_od_A_log', 'delta_od_dt_bias', 'delta_od_norm', 'delta_od_w_out', 'delta_ffn_w13', 'delta_ffn_w2', 'new_m_c_ctx', 'new_m_w_ada', 'new_m_b_ada', 'new_m_norm_w', 'new_m_ev_w_in', 'new_m_ev_lb', 'new_m_ev_a_norm', 'new_m_ev_pool_w', 'new_m_ev_pool_scale', 'new_m_ev_w_out', 'new_m_od_w_in', 'new_m_od_conv', 'new_m_od_A_log', 'new_m_od_dt_bias', 'new_m_od_norm', 'new_m_od_w_out', 'new_m_ffn_w13', 'new_m_ffn_w2', 'new_v_c_ctx', 'new_v_w_ada', 'new_v_b_ada', 'new_v_norm_w', 'new_v_ev_w_in', 'new_v_ev_lb', 'new_v_ev_a_norm', 'new_v_ev_pool_w', 'new_v_ev_pool_scale', 'new_v_ev_w_out', 'new_v_od_w_in', 'new_v_od_conv', 'new_v_od_A_log', 'new_v_od_dt_bias', 'new_v_od_norm', 'new_v_od_w_out', 'new_v_ffn_w13', 'new_v_ffn_w2']
TWIN_LEAF_KINDS = {'loss': 'loss', 'grad_x': 'grad_x', 'grad_c_ctx': 'grad_w', 'grad_w_ada': 'grad_w', 'grad_b_ada': 'grad_w', 'grad_norm_w': 'grad_w', 'grad_ev_w_in': 'grad_w', 'grad_ev_lb': 'grad_w', 'grad_ev_a_norm': 'grad_w', 'grad_ev_pool_w': 'grad_w', 'grad_ev_pool_scale': 'grad_w', 'grad_ev_w_out': 'grad_w', 'grad_od_w_in': 'grad_w', 'grad_od_conv': 'grad_w', 'grad_od_A_log': 'grad_w', 'grad_od_dt_bias': 'grad_w', 'grad_od_norm': 'grad_w', 'grad_od_w_out': 'grad_w', 'grad_ffn_w13': 'grad_w', 'grad_ffn_w2': 'grad_w', 'delta_c_ctx': 'delta_w', 'delta_w_ada': 'delta_w', 'delta_b_ada': 'delta_w', 'delta_norm_w': 'delta_w', 'delta_ev_w_in': 'delta_w', 'delta_ev_lb': 'delta_w', 'delta_ev_a_norm': 'delta_w', 'delta_ev_pool_w': 'delta_w', 'delta_ev_pool_scale': 'delta_w', 'delta_ev_w_out': 'delta_w', 'delta_od_w_in': 'delta_w', 'delta_od_conv': 'delta_w', 'delta_od_A_log': 'delta_w', 'delta_od_dt_bias': 'delta_w', 'delta_od_norm': 'delta_w', 'delta_od_w_out': 'delta_w', 'delta_ffn_w13': 'delta_w', 'delta_ffn_w2': 'delta_w', 'new_m_c_ctx': 'new_m', 'new_m_w_ada': 'new_m', 'new_m_b_ada': 'new_m', 'new_m_norm_w': 'new_m', 'new_m_ev_w_in': 'new_m', 'new_m_ev_lb': 'new_m', 'new_m_ev_a_norm': 'new_m', 'new_m_ev_pool_w': 'new_m', 'new_m_ev_pool_scale': 'new_m', 'new_m_ev_w_out': 'new_m', 'new_m_od_w_in': 'new_m', 'new_m_od_conv': 'new_m', 'new_m_od_A_log': 'new_m', 'new_m_od_dt_bias': 'new_m', 'new_m_od_norm': 'new_m', 'new_m_od_w_out': 'new_m', 'new_m_ffn_w13': 'new_m', 'new_m_ffn_w2': 'new_m', 'new_v_c_ctx': 'new_v', 'new_v_w_ada': 'new_v', 'new_v_b_ada': 'new_v', 'new_v_norm_w': 'new_v', 'new_v_ev_w_in': 'new_v', 'new_v_ev_lb': 'new_v', 'new_v_ev_a_norm': 'new_v', 'new_v_ev_pool_w': 'new_v', 'new_v_ev_pool_scale': 'new_v', 'new_v_ev_w_out': 'new_v', 'new_v_od_w_in': 'new_v', 'new_v_od_conv': 'new_v', 'new_v_od_A_log': 'new_v', 'new_v_od_dt_bias': 'new_v', 'new_v_od_norm': 'new_v', 'new_v_od_w_out': 'new_v', 'new_v_ffn_w13': 'new_v', 'new_v_ffn_w2': 'new_v'}


def _forward(args):
    return _fwd_reference(*[args[k] for k in FWD_PARAMS])


def _output_shape():
    def fwd():
        inp = _fwd_setup_inputs(0)
        return _fwd_reference(*[inp[k] for k in FWD_PARAMS])
    out = _jax.eval_shape(fwd)
    return out.shape, out.dtype

N_MICROBATCH = 1
ADAM_LR = 0.001
ADAM_B1 = 0.9
ADAM_B2 = 0.999
ADAM_EPS = 1e-08
ADAM_WD = 0.01
ADAM_STEP = 10
PER_EXAMPLE_BATCH_AXIS = {'x': 0, 'c': 0, 'ctx': 0, 'loss_target': 0}
SHARED_INPUTS = []
_WEIGHT_DTYPES = {'c_ctx': _jnp.float32, 'w_ada': _jnp.float32, 'b_ada': _jnp.float32, 'norm_w': _jnp.float32, 'ev_w_in': _jnp.float32, 'ev_lb': _jnp.float32, 'ev_a_norm': _jnp.float32, 'ev_pool_w': _jnp.float32, 'ev_pool_scale': _jnp.float32, 'ev_w_out': _jnp.float32, 'od_w_in': _jnp.float32, 'od_conv': _jnp.float32, 'od_A_log': _jnp.float32, 'od_dt_bias': _jnp.float32, 'od_norm': _jnp.float32, 'od_w_out': _jnp.float32, 'ffn_w13': _jnp.float32, 'ffn_w2': _jnp.float32}
MOMENT_SCALE = {'c_ctx': 1.031522e-02, 'w_ada': 1.164683e+00, 'b_ada': 2.527591e+00, 'norm_w': 2.274091e+00, 'ev_w_in': 6.155291e-02, 'ev_lb': 1.371831e-03, 'ev_a_norm': 2.382065e-01, 'ev_pool_w': 1.061375e-01, 'ev_pool_scale': 1.094828e-01, 'ev_w_out': 9.777415e-02, 'od_w_in': 5.341375e-02, 'od_conv': 5.836231e-02, 'od_A_log': 1.557997e-01, 'od_dt_bias': 1.507578e-01, 'od_norm': 4.014864e-01, 'od_w_out': 1.330085e-01, 'ffn_w13': 4.381075e-02, 'ffn_w2': 7.651385e-02}


def _to_microbatches(a, axis):
    t = _jnp.moveaxis(a, axis, 0)
    t = t.reshape((N_MICROBATCH, t.shape[0] // N_MICROBATCH) + t.shape[1:])
    return _jnp.moveaxis(t, 1, axis + 1)


def setup_inputs(seed: int = 0) -> dict:
    inp = _fwd_setup_inputs(seed)
    key = _jax.random.fold_in(_jax.random.key(seed), 7919)
    shape, _ = _output_shape()
    out = dict(inp)
    out["loss_target"] = _jax.random.normal(_jax.random.fold_in(key, 0), shape, _jnp.float32)
    for i, name in enumerate(TWIN_WEIGHTS):
        w = inp[name].astype(_jnp.float32)
        if MOMENT_SCALE is None:
            s = _jnp.sqrt(_jnp.mean(_jnp.square(w)) + 1e-30)
        else:
            s = MOMENT_SCALE[name]
        km, kv = _jax.random.split(_jax.random.fold_in(key, i + 1))
        out[name] = w
        out["m_" + name] = s * _jax.random.normal(km, w.shape, _jnp.float32)
        out["v_" + name] = (s * s) * _jax.random.uniform(kv, w.shape, _jnp.float32, 0.5, 1.5)
    if N_MICROBATCH > 1:
        for name, axis in PER_EXAMPLE_BATCH_AXIS.items():
            out[name] = _to_microbatches(out[name], axis)
    return {'x': out['x'], 'c': out['c'], 'ctx': out['ctx'], 'c_ctx': out['c_ctx'], 'w_ada': out['w_ada'], 'b_ada': out['b_ada'], 'norm_w': out['norm_w'], 'ev_w_in': out['ev_w_in'], 'ev_lb': out['ev_lb'], 'ev_a_norm': out['ev_a_norm'], 'ev_pool_w': out['ev_pool_w'], 'ev_pool_scale': out['ev_pool_scale'], 'ev_w_out': out['ev_w_out'], 'od_w_in': out['od_w_in'], 'od_conv': out['od_conv'], 'od_A_log': out['od_A_log'], 'od_dt_bias': out['od_dt_bias'], 'od_norm': out['od_norm'], 'od_w_out': out['od_w_out'], 'ffn_w13': out['ffn_w13'], 'ffn_w2': out['ffn_w2'], 'loss_target': out['loss_target'], 'm_c_ctx': out['m_c_ctx'], 'm_w_ada': out['m_w_ada'], 'm_b_ada': out['m_b_ada'], 'm_norm_w': out['m_norm_w'], 'm_ev_w_in': out['m_ev_w_in'], 'm_ev_lb': out['m_ev_lb'], 'm_ev_a_norm': out['m_ev_a_norm'], 'm_ev_pool_w': out['m_ev_pool_w'], 'm_ev_pool_scale': out['m_ev_pool_scale'], 'm_ev_w_out': out['m_ev_w_out'], 'm_od_w_in': out['m_od_w_in'], 'm_od_conv': out['m_od_conv'], 'm_od_A_log': out['m_od_A_log'], 'm_od_dt_bias': out['m_od_dt_bias'], 'm_od_norm': out['m_od_norm'], 'm_od_w_out': out['m_od_w_out'], 'm_ffn_w13': out['m_ffn_w13'], 'm_ffn_w2': out['m_ffn_w2'], 'v_c_ctx': out['v_c_ctx'], 'v_w_ada': out['v_w_ada'], 'v_b_ada': out['v_b_ada'], 'v_norm_w': out['v_norm_w'], 'v_ev_w_in': out['v_ev_w_in'], 'v_ev_lb': out['v_ev_lb'], 'v_ev_a_norm': out['v_ev_a_norm'], 'v_ev_pool_w': out['v_ev_pool_w'], 'v_ev_pool_scale': out['v_ev_pool_scale'], 'v_ev_w_out': out['v_ev_w_out'], 'v_od_w_in': out['v_od_w_in'], 'v_od_conv': out['v_od_conv'], 'v_od_A_log': out['v_od_A_log'], 'v_od_dt_bias': out['v_od_dt_bias'], 'v_od_norm': out['v_od_norm'], 'v_od_w_out': out['v_od_w_out'], 'v_ffn_w13': out['v_ffn_w13'], 'v_ffn_w2': out['v_ffn_w2']}


def _loss(weights, diff, rest, loss_target):
    with _jax.named_scope("forward"):
        args = {**rest, TWIN_DIFF_INPUT: diff, **{k: w.astype(_WEIGHT_DTYPES[k]) for k, w in weights.items()}}
        y = _forward(args)
    with _jax.named_scope("loss_head"):
        err = _jnp.square(y.astype(_jnp.float32) - loss_target)
        return 0.5 * _jnp.sum(_jnp.mean(err, axis=-1)) if err.ndim else 0.5 * err


def _adamw(w, g, m, v):
    m = ADAM_B1 * m + (1.0 - ADAM_B1) * g
    v = ADAM_B2 * v + (1.0 - ADAM_B2) * _jnp.square(g)
    m_hat = m / (1.0 - ADAM_B1 ** ADAM_STEP)
    v_hat = v / (1.0 - ADAM_B2 ** ADAM_STEP)
    delta = -ADAM_LR * (m_hat / (_jnp.sqrt(v_hat) + ADAM_EPS) + ADAM_WD * w)
    return delta, m, v


def reference(x, c, ctx, c_ctx, w_ada, b_ada, norm_w, ev_w_in, ev_lb, ev_a_norm, ev_pool_w, ev_pool_scale, ev_w_out, od_w_in, od_conv, od_A_log, od_dt_bias, od_norm, od_w_out, ffn_w13, ffn_w2, loss_target, m_c_ctx, m_w_ada, m_b_ada, m_norm_w, m_ev_w_in, m_ev_lb, m_ev_a_norm, m_ev_pool_w, m_ev_pool_scale, m_ev_w_out, m_od_w_in, m_od_conv, m_od_A_log, m_od_dt_bias, m_od_norm, m_od_w_out, m_ffn_w13, m_ffn_w2, v_c_ctx, v_w_ada, v_b_ada, v_norm_w, v_ev_w_in, v_ev_lb, v_ev_a_norm, v_ev_pool_w, v_ev_pool_scale, v_ev_w_out, v_od_w_in, v_od_conv, v_od_A_log, v_od_dt_bias, v_od_norm, v_od_w_out, v_ffn_w13, v_ffn_w2):
    given = dict(x=x, c=c, ctx=ctx, c_ctx=c_ctx, w_ada=w_ada, b_ada=b_ada, norm_w=norm_w, ev_w_in=ev_w_in, ev_lb=ev_lb, ev_a_norm=ev_a_norm, ev_pool_w=ev_pool_w, ev_pool_scale=ev_pool_scale, ev_w_out=ev_w_out, od_w_in=od_w_in, od_conv=od_conv, od_A_log=od_A_log, od_dt_bias=od_dt_bias, od_norm=od_norm, od_w_out=od_w_out, ffn_w13=ffn_w13, ffn_w2=ffn_w2, loss_target=loss_target, m_c_ctx=m_c_ctx, m_w_ada=m_w_ada, m_b_ada=m_b_ada, m_norm_w=m_norm_w, m_ev_w_in=m_ev_w_in, m_ev_lb=m_ev_lb, m_ev_a_norm=m_ev_a_norm, m_ev_pool_w=m_ev_pool_w, m_ev_pool_scale=m_ev_pool_scale, m_ev_w_out=m_ev_w_out, m_od_w_in=m_od_w_in, m_od_conv=m_od_conv, m_od_A_log=m_od_A_log, m_od_dt_bias=m_od_dt_bias, m_od_norm=m_od_norm, m_od_w_out=m_od_w_out, m_ffn_w13=m_ffn_w13, m_ffn_w2=m_ffn_w2, v_c_ctx=v_c_ctx, v_w_ada=v_w_ada, v_b_ada=v_b_ada, v_norm_w=v_norm_w, v_ev_w_in=v_ev_w_in, v_ev_lb=v_ev_lb, v_ev_a_norm=v_ev_a_norm, v_ev_pool_w=v_ev_pool_w, v_ev_pool_scale=v_ev_pool_scale, v_ev_w_out=v_ev_w_out, v_od_w_in=v_od_w_in, v_od_conv=v_od_conv, v_od_A_log=v_od_A_log, v_od_dt_bias=v_od_dt_bias, v_od_norm=v_od_norm, v_od_w_out=v_od_w_out, v_ffn_w13=v_ffn_w13, v_ffn_w2=v_ffn_w2)
    weights = {n: given[n] for n in TWIN_WEIGHTS}
    shared = {n: given[n] for n in SHARED_INPUTS}
    per_example = {n: given[n] for n in ['x', 'c', 'ctx']}
    grad_fn = _jax.value_and_grad(_loss, argnums=(0, 1))

    def one_microbatch(ex, loss_target):
        ex = dict(ex)
        diff = ex.pop(TWIN_DIFF_INPUT)
        return grad_fn(weights, diff, {**shared, **ex}, loss_target)

    if N_MICROBATCH == 1:
        loss, (grad_w, grad_x) = one_microbatch(per_example, given["loss_target"])
    else:
        def body(carry, xs):
            loss_sum, grad_sum = carry
            l_k, (gw_k, gx_k) = one_microbatch(xs[0], xs[1])
            with _jax.named_scope("update"):
                return (loss_sum + l_k, _jax.tree.map(_jnp.add, grad_sum, gw_k)), gx_k

        init = (_jnp.zeros((), _jnp.float32), _jax.tree.map(_jnp.zeros_like, weights))
        (loss, grad_w), grad_x = _jax.lax.scan(body, init, (per_example, given["loss_target"]))
    with _jax.named_scope("update"):
        delta_w, new_m, new_v = {}, {}, {}
        for n in TWIN_WEIGHTS:
            delta_w[n], new_m[n], new_v[n] = _adamw(weights[n], grad_w[n], given["m_" + n], given["v_" + n])
    return (loss, grad_x, *[grad_w[n] for n in TWIN_WEIGHTS], *[delta_w[n] for n in TWIN_WEIGHTS],
            *[new_m[n] for n in TWIN_WEIGHTS], *[new_v[n] for n in TWIN_WEIGHTS])
```

```python
import functools
from typing import Any, NamedTuple

import numpy as np

import jax
import jax.numpy as jnp
from jax import lax
from jax.experimental import pallas as pl
from jax.experimental.pallas import tpu as pltpu

F32 = jnp.float32
BF16 = jnp.bfloat16
MESH = pl.DeviceIdType.MESH
N_DEV = 8

EPS = 1e-6
GRID_W = 64
HEAD = 128
A_CHUNK = 32
C_CHUNK = 64
C_CONV = 4
POOL_WINDOWS = (2, 4, 8, 16)
ADAM_LR, ADAM_B1, ADAM_B2, ADAM_EPS, ADAM_WD, ADAM_STEP = 0.001, 0.9, 0.999, 1e-08, 0.01, 10

VMEM_LIMIT_BYTES = 56 * 1024 * 1024
LANES = 128
SUBLANES = 8
ROW_TILE = 256
FLAT_W = 1024
FLAT_ROWS = 512
TM_PREFS = (1056, 768, 512, 256, 128, 64, 32, 16)
TN_PREFS = (512, 384, 256, 128)
TK_PREFS = (2048, 2816, 1408, 1024, 512, 256, 128)
TO_PREFS = (1024, 1408, 704, 512, 384, 256, 128)


def _pick(dim, prefs):
    for p in prefs:
        if p <= dim and dim % p == 0:
            return p
    return dim


def _cparams(ngrid):
    return pltpu.CompilerParams(dimension_semantics=("arbitrary",) * ngrid, vmem_limit_bytes=VMEM_LIMIT_BYTES)


def _sds(shape, dtype):
    return jax.ShapeDtypeStruct(tuple(shape), dtype)


def _dot(a, b, ca, cb, hi):
    dims = (((ca,), (cb,)), ((), ()))
    if hi:
        return lax.dot_general(a.astype(F32), b.astype(F32), dims, precision=lax.Precision.HIGHEST,
                               preferred_element_type=F32)
    return lax.dot_general(a.astype(BF16), b.astype(BF16), dims, preferred_element_type=F32)


@functools.partial(jax.custom_vjp, nondiff_argnums=(2, 3, 4))
def mm(a, b, ca=1, cb=0, hi=False):
    return _dot(a, b, ca, cb, hi)


def _mm_fwd(a, b, ca, cb, hi):
    return _dot(a, b, ca, cb, hi), (a, b)


def _mm_bwd(ca, cb, hi, res, g):
    a, b = res
    da = _dot(g, b, 1, 1 - cb, hi) if ca == 1 else _dot(b, g, 1 - cb, 1, hi)
    db = _dot(a, g, 1 - ca, 0, hi) if cb == 0 else _dot(g, a, 0, 1 - ca, hi)
    return da, db


mm.defvjp(_mm_fwd, _mm_bwd)


def _iota2(n, m, axis):
    return lax.broadcasted_iota(jnp.int32, (n, m), axis)


class TArg(NamedTuple):
    arr: Any
    block: tuple
    imap: Any
    kind: str = "row"
    acc: tuple = ()
    gdtype: Any = F32
    grad: bool = True


def _load(ref):
    v = ref[...]
    return v.astype(F32) if jnp.issubdtype(v.dtype, jnp.floating) else v


def tile_fwd(name, f, grid, args, outs):
    n_in, ng = len(args), len(grid)

    def body(*refs):
        pids = tuple(pl.program_id(k) for k in range(ng))
        res = f(pids, *[_load(r) for r in refs[:n_in]])
        for r, v in zip(refs[n_in:], res):
            r[...] = v.astype(r.dtype)

    return pl.pallas_call(
        body, grid=grid, name=name,
        in_specs=[pl.BlockSpec(a.block, a.imap) for a in args],
        out_specs=[pl.BlockSpec(b, im) for (_, _, b, im) in outs],
        out_shape=[_sds(s, d) for (s, d, _, _) in outs],
        compiler_params=_cparams(ng),
    )(*[a.arr for a in args])


def _store_grads(args, diff, pids, g_refs, d):
    for k, gr, dv in zip(diff, g_refs, d):
        a = args[k]
        if a.kind == "row" or not a.acc:
            gr[...] = dv.astype(gr.dtype)
        else:
            first = pids[a.acc[0]] == 0
            for ax in a.acc[1:]:
                first = jnp.logical_and(first, pids[ax] == 0)

            @pl.when(first)
            def _(gr=gr, dv=dv):
                gr[...] = dv.astype(gr.dtype)

            @pl.when(jnp.logical_not(first))
            def _(gr=gr, dv=dv):
                gr[...] += dv.astype(gr.dtype)


def tile_bwd(name, f, grid, args, cts):
    n_in, n_ct, ng = len(args), len(cts), len(grid)
    diff = [k for k, a in enumerate(args) if a.kind != "const" and a.grad]

    def body(*refs):
        pids = tuple(pl.program_id(k) for k in range(ng))
        vals = [_load(r) for r in refs[:n_in]]

        def g(*dv):
            full = list(vals)
            for k, v in zip(diff, dv):
                full[k] = v
            return tuple(f(pids, *full))

        _, vjp = jax.vjp(g, *[vals[k] for k in diff])
        d = vjp(tuple(_load(r) for r in refs[n_in:n_in + n_ct]))
        _store_grads(args, diff, pids, refs[n_in + n_ct:], d)

    return pl.pallas_call(
        body, grid=grid, name=name,
        in_specs=[pl.BlockSpec(a.block, a.imap) for a in args] + [pl.BlockSpec(b, im) for (_, b, im) in cts],
        out_specs=[pl.BlockSpec(args[k].block, args[k].imap) for k in diff],
        out_shape=[_sds(args[k].arr.shape, args[k].gdtype) for k in diff],
        compiler_params=_cparams(ng),
    )(*[a.arr for a in args], *[c[0] for c in cts])


def scan_fwd(name, f, n_heads, n_steps, args, outs, n_state):
    n_in, n_out = len(args), len(outs)
    sblock = (None, None, HEAD, HEAD)

    def body(*refs):
        in_refs = refs[:n_in]
        out_refs = refs[n_in:n_in + n_out]
        save_refs = refs[n_in + n_out:n_in + n_out + n_state]
        s_refs = refs[n_in + n_out + n_state:]

        @pl.when(pl.program_id(1) == 0)
        def _():
            for s in s_refs:
                s[...] = jnp.zeros_like(s)

        states = tuple(s[...] for s in s_refs)
        for sv, s in zip(save_refs, states):
            sv[...] = s
        new_states, res = f(states, *[_load(r) for r in in_refs])
        for s, v in zip(s_refs, new_states):
            s[...] = v
        for r, v in zip(out_refs, res):
            r[...] = v.astype(r.dtype)

    res = pl.pallas_call(
        body, grid=(n_heads, n_steps), name=name,
        in_specs=[pl.BlockSpec(a.block, a.imap) for a in args],
        out_specs=[pl.BlockSpec(b, im) for (_, _, b, im) in outs]
        + [pl.BlockSpec(sblock, lambda h, i: (h, i, 0, 0))] * n_state,
        out_shape=[_sds(s, d) for (s, d, _, _) in outs] + [_sds((n_heads, n_steps, HEAD, HEAD), F32)] * n_state,
        scratch_shapes=[pltpu.VMEM((HEAD, HEAD), F32)] * n_state,
        compiler_params=_cparams(2),
    )(*[a.arr for a in args])
    return res[:n_out], res[n_out:]


def scan_bwd(name, f, n_heads, n_steps, args, saves, cts):
    n_in, n_ct, n_state = len(args), len(cts), len(saves)
    diff = [k for k, a in enumerate(args) if a.kind != "const" and a.grad]
    sblock = (None, None, HEAD, HEAD)

    def rv(im):
        return lambda h, i: im(h, n_steps - 1 - i)

    def body(*refs):
        in_refs = refs[:n_in]
        save_refs = refs[n_in:n_in + n_state]
        ct_refs = refs[n_in + n_state:n_in + n_state + n_ct]
        g_refs = refs[n_in + n_state + n_ct:n_in + n_state + n_ct + len(diff)]
        ds_refs = refs[n_in + n_state + n_ct + len(diff):]
        pids = (pl.program_id(0), pl.program_id(1))

        @pl.when(pids[1] == 0)
        def _():
            for s in ds_refs:
                s[...] = jnp.zeros_like(s)

        vals = [_load(r) for r in in_refs]

        def g(states, *dv):
            full = list(vals)
            for k, v in zip(diff, dv):
                full[k] = v
            new_states, res = f(states, *full)
            return tuple(new_states), tuple(res)

        _, vjp = jax.vjp(g, tuple(s[...] for s in save_refs), *[vals[k] for k in diff])
        d = vjp((tuple(s[...] for s in ds_refs), tuple(_load(r) for r in ct_refs)))
        for s, v in zip(ds_refs, d[0]):
            s[...] = v
        _store_grads(args, diff, pids, g_refs, d[1:])

    return pl.pallas_call(
        body, grid=(n_heads, n_steps), name=name,
        in_specs=[pl.BlockSpec(a.block, rv(a.imap)) for a in args]
        + [pl.BlockSpec(sblock, rv(lambda h, i: (h, i, 0, 0)))] * n_state
        + [pl.BlockSpec(b, rv(im)) for (_, b, im) in cts],
        out_specs=[pl.BlockSpec(args[k].block, rv(args[k].imap)) for k in diff],
        out_shape=[_sds(args[k].arr.shape, args[k].gdtype) for k in diff],
        scratch_shapes=[pltpu.VMEM((HEAD, HEAD), F32)] * n_state,
        compiler_params=_cparams(2),
    )(*[a.arr for a in args], *saves, *[c[0] for c in cts])


def matmul(name, a, b, mode, add=None, out_dtype=F32):
    if mode == "nn":
        (m, k), n = a.shape, b.shape[1]
        to_m, to_n, tr = _pick(m, TM_PREFS), _pick(n, TN_PREFS), _pick(k, TK_PREFS)
        grid = (m // to_m, n // to_n, k // tr)
        a_spec = pl.BlockSpec((to_m, tr), lambda i, j, l: (i, l))
        b_spec = pl.BlockSpec((tr, to_n), lambda i, j, l: (l, j))
        dims, oshape = (1, 0), (m, n)
    elif mode == "nt":
        (m, n), k = a.shape, b.shape[0]
        to_m, to_n, tr = _pick(m, TM_PREFS), _pick(k, TO_PREFS), _pick(n, TK_PREFS)
        grid = (m // to_m, k // to_n, n // tr)
        a_spec = pl.BlockSpec((to_m, tr), lambda i, j, l: (i, l))
        b_spec = pl.BlockSpec((to_n, tr), lambda i, j, l: (j, l))
        dims, oshape = (1, 1), (m, k)
    else:
        (t, k), n = a.shape, b.shape[1]
        to_m, to_n, tr = _pick(k, TO_PREFS), _pick(n, TO_PREFS), _pick(t, TM_PREFS)
        grid = (k // to_m, n // to_n, t // tr)
        a_spec = pl.BlockSpec((tr, to_m), lambda i, j, l: (l, i))
        b_spec = pl.BlockSpec((tr, to_n), lambda i, j, l: (l, j))
        dims, oshape = (0, 0), (k, n)
    n_red = grid[2]
    o_spec = pl.BlockSpec((to_m, to_n), lambda i, j, l: (i, j))
    has_add = add is not None

    def body(a_ref, b_ref, *rest):
        add_ref = rest[0] if has_add else None
        o_ref = rest[1] if has_add else rest[0]
        part = lax.dot_general(a_ref[...].astype(BF16), b_ref[...].astype(BF16),
                               (((dims[0],), (dims[1],)), ((), ())), preferred_element_type=F32)

        def finish(v):
            if has_add:
                v = v + add_ref[...]
            o_ref[...] = v.astype(o_ref.dtype)

        if n_red == 1:
            finish(part)
        else:
            acc = rest[-1]
            step = pl.program_id(2)

            @pl.when(step == 0)
            def _():
                acc[...] = part

            @pl.when(step > 0)
            def _():
                acc[...] += part

            @pl.when(step == n_red - 1)
            def _():
                finish(acc[...])

    return pl.pallas_call(
        body, grid=grid, name=name,
        in_specs=[a_spec, b_spec] + ([o_spec] if has_add else []),
        out_specs=o_spec, out_shape=_sds(oshape, out_dtype),
        scratch_shapes=[pltpu.VMEM((to_m, to_n), F32)] if n_red > 1 else [],
        compiler_params=_cparams(3),
    )(a, b, *([add] if has_add else []))


def assemble(name, pieces, out_dtype):
    flat = [s for piece in pieces for s in piece]
    t = flat[0][0].shape[0]
    widths = [piece[0][1] for piece in pieces]

    def body(*refs):
        o_ref, k, off = refs[-1], 0, 0
        for piece, w in zip(pieces, widths):
            v = refs[k][...].astype(F32)
            k += 1
            for _ in piece[1:]:
                v = v + refs[k][...].astype(F32)
                k += 1
            o_ref[:, off:off + w] = v.astype(o_ref.dtype)
            off += w

    tr = ROW_TILE // 2
    return pl.pallas_call(
        body, grid=(t // tr,), name=name,
        in_specs=[pl.BlockSpec((tr, w), functools.partial(lambda i, cb: (i, cb), cb=cb)) for (_, w, cb) in flat],
        out_specs=pl.BlockSpec((tr, sum(widths)), lambda i: (i, 0)),
        out_shape=_sds((t, sum(widths)), out_dtype),
        compiler_params=_cparams(1),
    )(*[s[0] for s in flat])


def sum_leading(name, arr, out_dtype):
    k, rows, w = arr.shape
    tr = _pick(rows, (FLAT_ROWS, 256, 128, 64, 32, 16, 8))

    def body(a_ref, o_ref):
        v = a_ref[0].astype(F32)
        for j in range(1, k):
            v = v + a_ref[j].astype(F32)
        o_ref[...] = v.astype(o_ref.dtype)

    return pl.pallas_call(
        body, grid=(rows // tr,), name=name,
        in_specs=[pl.BlockSpec((k, tr, w), lambda i: (0, i, 0))],
        out_specs=pl.BlockSpec((tr, w), lambda i: (i, 0)),
        out_shape=_sds((rows, w), out_dtype), compiler_params=_cparams(1),
    )(arr)


def add_cast(name, a, b, out_dtype):
    k, rows, w = a.shape
    tr = _pick(rows, (FLAT_ROWS, 256, 128, 64, 32, 16, 8))

    def body(a_ref, b_ref, o_ref):
        o_ref[...] = (a_ref[...] + b_ref[...]).astype(o_ref.dtype)

    spec = pl.BlockSpec((None, tr, w), lambda j, i: (j, i, 0))
    return pl.pallas_call(
        body, grid=(k, rows // tr), name=name, in_specs=[spec, spec], out_specs=spec,
        out_shape=_sds(a.shape, out_dtype), compiler_params=_cparams(2),
    )(a, b)


def adamw(name, g, w, m, v):
    rows, wd = g.shape
    tr = _pick(rows, (FLAT_ROWS, 256, 128, 64, 32, 16, 8))

    def body(g_ref, w_ref, m_ref, v_ref, d_ref, nm_ref, nv_ref):
        gv = g_ref[...]
        mn = ADAM_B1 * m_ref[...] + (1.0 - ADAM_B1) * gv
        vn = ADAM_B2 * v_ref[...] + (1.0 - ADAM_B2) * jnp.square(gv)
        m_hat = mn / (1.0 - ADAM_B1 ** ADAM_STEP)
        v_hat = vn / (1.0 - ADAM_B2 ** ADAM_STEP)
        d_ref[...] = -ADAM_LR * (m_hat / (jnp.sqrt(v_hat) + ADAM_EPS) + ADAM_WD * w_ref[...])
        nm_ref[...] = mn
        nv_ref[...] = vn

    spec = pl.BlockSpec((tr, wd), lambda i: (i, 0))
    return pl.pallas_call(
        body, grid=(rows // tr,), name=name, in_specs=[spec] * 4, out_specs=[spec] * 3,
        out_shape=[_sds(g.shape, F32)] * 3, compiler_params=_cparams(1),
    )(g, w, m, v)


def loss_kernel(name, xs, target, n_ctx_tiles):
    t, d = xs.shape
    nt = t // ROW_TILE

    def body(x_ref, t_ref, dx_ref, l_ref):
        i = pl.program_id(0)
        is_lat = i >= n_ctx_tiles
        err = jnp.where(is_lat, x_ref[...] - t_ref[...], 0.0)
        dx_ref[...] = err / d
        part = 0.5 * jnp.sum(jnp.mean(jnp.square(err), axis=-1, keepdims=True), axis=0, keepdims=True)

        @pl.when(i == 0)
        def _():
            l_ref[...] = jnp.zeros_like(l_ref)

        l_ref[...] += jnp.broadcast_to(part, l_ref.shape)

    dx, l = pl.pallas_call(
        body, grid=(nt,), name=name,
        in_specs=[pl.BlockSpec((ROW_TILE, d), lambda i: (i, 0)),
                  pl.BlockSpec((ROW_TILE, d), lambda i: (jnp.maximum(i - n_ctx_tiles, 0), 0))],
        out_specs=[pl.BlockSpec((ROW_TILE, d), lambda i: (i, 0)), pl.BlockSpec((SUBLANES, LANES), lambda i: (0, 0))],
        out_shape=[_sds((t, d), F32), _sds((SUBLANES, LANES), F32)], compiler_params=_cparams(1),
    )(xs, target)
    return l[0, 0], dx


CONV_COLS = 512
CONV_LEFT = C_CONV // 2


def _conv_halo_specs(t, n_ctx_tiles):
    nt = t // ROW_TILE
    per = ROW_TILE // SUBLANES
    cur = pl.BlockSpec((ROW_TILE, CONV_COLS), lambda j, i: (i, j))
    prev = pl.BlockSpec((SUBLANES, CONV_COLS), lambda j, i: (jnp.maximum(i * per - 1, 0), j))
    nxt = pl.BlockSpec((SUBLANES, CONV_COLS), lambda j, i: (jnp.minimum((i + 1) * per, nt * per - 1), j))
    return cur, prev, nxt


def _fill_ext(ext, prev_ref, cur_ref, next_ref, i, nt, n_ctx_tiles):
    has_prev = jnp.logical_and(i != 0, i != n_ctx_tiles)
    has_next = jnp.logical_and(i != n_ctx_tiles - 1, i != nt - 1)
    ext[0:SUBLANES, :] = jnp.where(has_prev, prev_ref[...], 0.0)
    ext[SUBLANES:SUBLANES + ROW_TILE, :] = cur_ref[...]
    ext[SUBLANES + ROW_TILE:, :] = jnp.where(has_next, next_ref[...], 0.0)


def conv_fwd(name, p, w, width, n_ctx_tiles):
    t = p.shape[0]
    nt = t // ROW_TILE
    cur, prev, nxt = _conv_halo_specs(t, n_ctx_tiles)

    def body(c_ref, p_ref, n_ref, w_ref, o_ref, ext):
        _fill_ext(ext, p_ref, c_ref, n_ref, pl.program_id(1), nt, n_ctx_tiles)
        acc = None
        for j in range(C_CONV):
            term = ext[pl.ds(SUBLANES + j - CONV_LEFT, ROW_TILE), :] * w_ref[j:j + 1, :]
            acc = term if acc is None else acc + term
        o_ref[...] = acc

    return pl.pallas_call(
        body, grid=(width // CONV_COLS, nt), name=name,
        in_specs=[cur, prev, nxt, pl.BlockSpec((C_CONV, CONV_COLS), lambda j, i: (0, j))],
        out_specs=cur, out_shape=_sds((t, width), F32),
        scratch_shapes=[pltpu.VMEM((ROW_TILE + 2 * SUBLANES, CONV_COLS), F32)],
        compiler_params=_cparams(2),
    )(p, p, p, w)


def conv_bwd(name, p, w, dz, width, n_ctx_tiles):
    t = p.shape[0]
    nt = t // ROW_TILE
    cur, prev, nxt = _conv_halo_specs(t, n_ctx_tiles)

    def body(c_ref, p_ref, n_ref, dc_ref, dp_ref, dn_ref, w_ref, du_ref, dw_ref, ext, dext):
        i = pl.program_id(1)
        _fill_ext(ext, p_ref, c_ref, n_ref, i, nt, n_ctx_tiles)
        _fill_ext(dext, dp_ref, dc_ref, dn_ref, i, nt, n_ctx_tiles)
        dzc = dc_ref[...]
        @pl.when(i == 0)
        def _():
            dw_ref[...] = jnp.zeros_like(dw_ref)

        acc = None
        for j in range(C_CONV):
            term = dext[pl.ds(SUBLANES + CONV_LEFT - j, ROW_TILE), :] * w_ref[j:j + 1, :]
            acc = term if acc is None else acc + term
            dw_ref[j:j + 1, :] += jnp.sum(dzc * ext[pl.ds(SUBLANES + j - CONV_LEFT, ROW_TILE), :], axis=0, keepdims=True)
        du_ref[...] = acc

    wspec = pl.BlockSpec((C_CONV, CONV_COLS), lambda j, i: (0, j))
    return pl.pallas_call(
        body, grid=(width // CONV_COLS, nt), name=name,
        in_specs=[cur, prev, nxt, cur, prev, nxt, wspec],
        out_specs=[cur, wspec], out_shape=[_sds((t, width), F32), _sds((C_CONV, width), F32)],
        scratch_shapes=[pltpu.VMEM((ROW_TILE + 2 * SUBLANES, CONV_COLS), F32)] * 2,
        compiler_params=_cparams(2),
    )(p, p, p, dz, dz, dz, w)


def _rms(x, w):
    return x * lax.rsqrt(jnp.mean(x * x, axis=-1, keepdims=True) + EPS) * w


def _seg_mod(mods, is_ctx):
    return jnp.where(is_ctx, mods[0], mods[1])


def f_norm_mod(shift_i, scale_i, n_ctx_tiles, passthrough=False):
    def f(pids, x, nw, mods):
        m = _seg_mod(mods, pids[0] < n_ctx_tiles)
        h = _rms(x, nw) * (1.0 + m[scale_i:scale_i + 1]) + m[shift_i:shift_i + 1]
        return (h, x) if passthrough else (h,)
    return f


def f_gate_res(gate_i, n_ctx_tiles):
    def f(pids, x, y, nw, mods):
        m = _seg_mod(mods, pids[0] < n_ctx_tiles)
        return (x + m[gate_i:gate_i + 1] * _rms(y, nw),)
    return f


def f_swiglu(pids, gu):
    half = gu.shape[1] // 2
    return (jax.nn.silu(gu[:, :half]) * gu[:, half:],)


def f_readout(pids, o_a, o_b, gate, nw):
    return (_rms(o_a + o_b, nw) * jax.nn.silu(gate),)


def f_pool(pids, u, pmat, pw, scale):
    d = mm(pmat, u, 1, 0, True) - u
    return (mm(d, pw) * scale,)


def f_lb(layer):
    def f(pids, *slots):
        top = slots[0]
        for s in slots[1:]:
            top = jnp.maximum(top, s)
        ex = [jnp.exp(s - top) for s in slots]
        tot = ex[0]
        for e in ex[1:]:
            tot = tot + e
        part = ex[0]
        for e in ex[1:layer + 1]:
            part = part + e
        return (part / tot,)
    return f


def f_ada(pids, c_all, c_ctx, w, b):
    c16 = jnp.concatenate([c_all, jnp.broadcast_to(c_ctx, c_all.shape)], axis=0)
    return (mm(jax.nn.silu(c16), w) + b,)


def f_hgrn2(rev):
    c = A_CHUNK
    mid = c - c // 2 if rev else c // 2 - 1

    def f(states, qr, fr, ir, lb):
        (st,) = states
        ri, ci = _iota2(c, c, 0), _iota2(c, c, 1)
        incl = (ri <= ci) if rev else (ri >= ci)
        q = jax.nn.silu(qr)
        log_f = jnp.log(lb + (1.0 - lb) * jax.nn.sigmoid(fr))
        k = (1.0 - lb) * jax.nn.sigmoid(-fr)
        b = mm(incl.astype(F32), log_f, 1, 0, True)
        b_last = jnp.sum(log_f, axis=0, keepdims=True)
        b_mid = b[mid:mid + 1]
        scores = mm(q * jnp.exp(b - b_mid), k * jnp.exp(b_mid - b), 1, 1)
        o = mm(jnp.where(incl, scores, 0.0), ir) + mm(q * jnp.exp(b), st, 1, 1)
        st_new = st * jnp.exp(b_last) + mm(ir, k * jnp.exp(b_last - b), 0, 0)
        return (st_new,), (o,)
    return f


def _unit_tri_inv(a_low, eye):
    n = a_low.shape[0]
    p = -a_low
    x = eye + p
    k = 2
    while k < n:
        p = mm(p, p, 1, 0, True)
        x = x + mm(x, p, 1, 0, True)
        k *= 2
    return x


def _l2n(x):
    return x * lax.rsqrt(jnp.sum(x * x, axis=-1, keepdims=True) + EPS)


def _gdn_head(s, q, k, v, a_row, b_row, alog, dtb, rev):
    c = q.shape[0]
    ri, ci = _iota2(c, c, 0), _iota2(c, c, 1)
    causal = (ri <= ci) if rev else (ri >= ci)
    causal_t = (ri >= ci) if rev else (ri <= ci)
    strict = (ri < ci) if rev else (ri > ci)
    eye = ri == ci
    g_row = -jnp.exp(alog) * jax.nn.softplus(a_row + dtb)
    beta_row = jax.nn.sigmoid(b_row)
    g_sq = jnp.broadcast_to(g_row, (c, c))
    beta_col = jnp.sum(jnp.where(eye, jnp.broadcast_to(beta_row, (c, c)), 0.0), axis=1, keepdims=True)
    g_col = jnp.sum(jnp.where(eye, g_sq, 0.0), axis=1, keepdims=True)
    gc_col = jnp.sum(jnp.where(causal, g_sq, 0.0), axis=1, keepdims=True)
    gc_row = jnp.sum(jnp.where(causal_t, jnp.broadcast_to(g_col, (c, c)), 0.0), axis=0, keepdims=True)
    gc_last = jnp.sum(g_row, axis=1, keepdims=True)
    decay = jnp.where(causal, jnp.exp(jnp.where(causal, gc_col - gc_row, 0.0)), 0.0)
    k_beta = k * beta_col
    v_beta = v * beta_col
    a_low = jnp.where(strict, mm(k_beta, k, 1, 1) * decay, 0.0)
    x = _unit_tri_inv(a_low, eye.astype(F32))
    egc = jnp.exp(gc_col)
    u = mm(x, v_beta, 1, 0, True)
    w = mm(x, k_beta * egc, 1, 0, True)
    qk = jnp.where(causal, mm(q, k, 1, 1) * decay, 0.0)
    v_new = u - mm(w, s)
    o = mm(q * egc, s) + mm(qk, v_new)
    s_new = s * jnp.exp(gc_last) + mm(k * jnp.exp(gc_last - gc_col), v_new, 0, 0)
    return s_new, o


def f_gdn(rev):
    def f(states, qr, kr, vr, a2, b2, alog2, dtb2):
        q = _l2n(jax.nn.silu(qr)) * (HEAD ** -0.5)
        k = _l2n(jax.nn.silu(kr))
        v = jax.nn.silu(vr)
        new, outs = [], []
        for r in range(2):
            s_new, o = _gdn_head(states[r], q, k, v[:, r * HEAD:(r + 1) * HEAD], a2[r:r + 1], b2[r:r + 1],
                                 alog2[r:r + 1], dtb2[r:r + 1], rev)
            new.append(s_new)
            outs.append(o)
        return tuple(new), (jnp.concatenate(outs, axis=1),)
    return f


def _hbm_spec():
    return pl.BlockSpec(memory_space=pltpu.HBM)


def all_gather(name, xs):
    m, n = xs.shape

    def body(x_ref, out_ref, send_sems, recv_sems, local_sem):
        x, y, c = lax.axis_index("x"), lax.axis_index("y"), lax.axis_index("c")
        me, sibling = (x, y, c), (x, y, 1 - c)
        chips = [(1 - x, y), (x, 1 - y), (1 - x, 1 - y)]

        def slab(px, py, pc):
            return out_ref.at[4 * px + 2 * py + pc]

        def copy(k, block, to, src=None):
            return pltpu.make_async_remote_copy(
                src_ref=slab(*block) if src is None else src, dst_ref=slab(*block),
                send_sem=send_sems.at[k], recv_sem=recv_sems.at[k], device_id=to, device_id_type=MESH)

        mine = pltpu.make_async_copy(x_ref, slab(*me), local_sem)
        mine.start()
        first = [copy(0, me, sibling, src=x_ref)]
        first += [copy(1 + j, me, (*chip, c), src=x_ref) for j, chip in enumerate(chips)]
        for cp in first:
            cp.start()
        passed = [copy(4 + j, (*chip, c), sibling) for j, chip in enumerate(chips)]
        for j, chip in enumerate(chips):
            copy(1 + j, (*chip, c), me).wait_recv()
            passed[j].start()
        copy(0, sibling, me).wait_recv()
        for j, chip in enumerate(chips):
            copy(4 + j, (*chip, 1 - c), me).wait_recv()
        for cp in first + passed:
            cp.wait_send()
        mine.wait()

    return pl.pallas_call(
        body, name=name, out_shape=_sds((N_DEV, m, n), xs.dtype),
        in_specs=[_hbm_spec()], out_specs=_hbm_spec(),
        scratch_shapes=[pltpu.SemaphoreType.DMA((7,)), pltpu.SemaphoreType.DMA((7,)), pltpu.SemaphoreType.DMA],
    )(xs)


def sibling_exchange(name, send):
    def body(s_ref, l_ref, send_sem, recv_sem):
        x, y, c = lax.axis_index("x"), lax.axis_index("y"), lax.axis_index("c")
        cp = pltpu.make_async_remote_copy(src_ref=s_ref, dst_ref=l_ref, send_sem=send_sem, recv_sem=recv_sem,
                                          device_id=(x, y, 1 - c), device_id_type=MESH)
        cp.start()
        cp.wait()

    return pl.pallas_call(
        body, name=name, out_shape=_sds(send.shape, send.dtype),
        in_specs=[_hbm_spec()], out_specs=_hbm_spec(),
        scratch_shapes=[pltpu.SemaphoreType.DMA, pltpu.SemaphoreType.DMA],
    )(send)


def chip_exchange(name, h):
    def body(h_ref, r_ref, send_sems, recv_sems, local_sem):
        x, y, c = lax.axis_index("x"), lax.axis_index("y"), lax.axis_index("c")
        my = 2 * x + y
        chips = [(1 - x, y), (x, 1 - y), (1 - x, 1 - y)]
        mine = pltpu.make_async_copy(h_ref.at[my], r_ref.at[my], local_sem)
        mine.start()

        def copy(k, src_slot, dst_slot, to):
            return pltpu.make_async_remote_copy(
                src_ref=h_ref.at[src_slot], dst_ref=r_ref.at[dst_slot], send_sem=send_sems.at[k],
                recv_sem=recv_sems.at[k], device_id=(*to, c), device_id_type=MESH)

        sends = [copy(k, 2 * qx + qy, my, (qx, qy)) for k, (qx, qy) in enumerate(chips)]
        for cp in sends:
            cp.start()
        for k, (qx, qy) in enumerate(chips):
            copy(k, my, 2 * qx + qy, (qx, qy)).wait_recv()
        for cp in sends:
            cp.wait_send()
        mine.wait()

    return pl.pallas_call(
        body, name=name, out_shape=_sds(h.shape, h.dtype),
        in_specs=[_hbm_spec()], out_specs=_hbm_spec(),
        scratch_shapes=[pltpu.SemaphoreType.DMA((3,)), pltpu.SemaphoreType.DMA((3,)), pltpu.SemaphoreType.DMA],
    )(h)


def _pack(arrs, dtype, width, row_mult, lead=0):
    ld = arrs[0].shape[:lead]
    flat = jnp.concatenate([a.reshape(ld + (-1,)).astype(dtype) for a in arrs], axis=-1)
    n = flat.shape[-1]
    q = width * row_mult
    npad = -(-n // q) * q
    flat = jnp.pad(flat, [(0, 0)] * lead + [(0, npad - n)])
    return flat.reshape(ld + (npad // width, width))


def _unpack(buf, shapes, lead=0):
    ld = buf.shape[:lead]
    flat = buf.reshape(ld + (-1,))
    out, off = [], 0
    for s in shapes:
        n = int(np.prod(s))
        out.append(flat[..., off:off + n].reshape(ld + tuple(s)))
        off += n
    return out


def _pool_mats(seg_len):
    mats = np.zeros((len(POOL_WINDOWS), ROW_TILE, ROW_TILE), np.float32)
    for gi, win in enumerate(POOL_WINDOWS):
        for p in range(ROW_TILE):
            base = (p // seg_len) * seg_len
            q = p - base
            lo = min(max(q - win // 2, 0), seg_len - 1)
            hi = min(max(q + win - 1 - win // 2, 0), seg_len - 1)
            mats[gi, p, base + lo:base + hi + 1] = 1.0 / (hi - lo + 1)
    return mats


def kernel(x, c, ctx, c_ctx, w_ada, b_ada, norm_w, ev_w_in, ev_lb, ev_a_norm, ev_pool_w, ev_pool_scale, ev_w_out, od_w_in, od_conv, od_A_log, od_dt_bias, od_norm, od_w_out, ffn_w13, ffn_w2, loss_target, m_c_ctx, m_w_ada, m_b_ada, m_norm_w, m_ev_w_in, m_ev_lb, m_ev_a_norm, m_ev_pool_w, m_ev_pool_scale, m_ev_w_out, m_od_w_in, m_od_conv, m_od_A_log, m_od_dt_bias, m_od_norm, m_od_w_out, m_ffn_w13, m_ffn_w2, v_c_ctx, v_w_ada, v_b_ada, v_norm_w, v_ev_w_in, v_ev_lb, v_ev_a_norm, v_ev_pool_w, v_ev_pool_scale, v_ev_w_out, v_od_w_in, v_od_conv, v_od_A_log, v_od_dt_bias, v_od_norm, v_od_w_out, v_ffn_w13, v_ffn_w2):
    names = ["c_ctx", "w_ada", "b_ada", "norm_w", "ev_w_in", "ev_lb", "ev_a_norm", "ev_pool_w", "ev_pool_scale",
             "ev_w_out", "od_w_in", "od_conv", "od_A_log", "od_dt_bias", "od_norm", "od_w_out", "ffn_w13", "ffn_w2"]
    wts = dict(zip(names, [c_ctx, w_ada, b_ada, norm_w, ev_w_in, ev_lb, ev_a_norm, ev_pool_w, ev_pool_scale,
                           ev_w_out, od_w_in, od_conv, od_A_log, od_dt_bias, od_norm, od_w_out, ffn_w13, ffn_w2]))
    mom1 = dict(zip(names, [m_c_ctx, m_w_ada, m_b_ada, m_norm_w, m_ev_w_in, m_ev_lb, m_ev_a_norm, m_ev_pool_w,
                            m_ev_pool_scale, m_ev_w_out, m_od_w_in, m_od_conv, m_od_A_log, m_od_dt_bias, m_od_norm,
                            m_od_w_out, m_ffn_w13, m_ffn_w2]))
    mom2 = dict(zip(names, [v_c_ctx, v_w_ada, v_b_ada, v_norm_w, v_ev_w_in, v_ev_lb, v_ev_a_norm, v_ev_pool_w,
                            v_ev_pool_scale, v_ev_w_out, v_od_w_in, v_od_conv, v_od_A_log, v_od_dt_bias, v_od_norm,
                            v_od_w_out, v_ffn_w13, v_ffn_w2]))

    ax, ay, ac = lax.axis_index("x"), lax.axis_index("y"), lax.axis_index("c")
    me = 4 * ax + 2 * ay + ac
    my_chip = 2 * ax + ay

    seq, d = x.shape[1], x.shape[2]
    n_ctx = ctx.shape[1]
    t = n_ctx + seq
    nt = t // ROW_TILE
    nct = n_ctx // ROW_TILE
    assert n_ctx == ROW_TILE and seq % ROW_TILE == 0 and ROW_TILE % GRID_W == 0
    depth = w_ada.shape[0]
    aw = d // 2
    n_ah = aw // HEAD
    n_grp = len(POOL_WINDOWS)
    dg = aw // n_grp
    assert dg % LANES == 0
    n_kh = d // HEAD
    kw, vw = n_kh * HEAD, 2 * n_kh * HEAD
    n_gate = 8 * n_kh
    ffn_h = ffn_w2.shape[1] * N_DEV
    ada_loc = w_ada.shape[2]
    assert depth == 2

    small_shapes = [(d,), norm_w.shape, ev_lb.shape, ev_pool_w.shape[1:], od_conv.shape[1:]]
    g1 = all_gather("ag_small", _pack([c[0], norm_w, ev_lb, ev_pool_w[0], od_conv[0]], F32, LANES, SUBLANES))
    c_all, nw_g, lb_g, pw_g, cv_g = _unpack(g1, small_shapes, lead=1)
    nw_full = nw_g.transpose(1, 2, 0, 3).reshape(depth, 4, d)
    lb_full = lb_g.transpose(1, 2, 0, 3).reshape(2, depth + 1, aw)
    pw_full = pw_g.transpose(1, 0, 2, 3).reshape(n_grp, dg, dg)
    cv_full = cv_g.transpose(1, 0, 2).reshape(C_CONV, 2 * kw + vw)

    big_names = ["ev_w_in", "ev_w_out", "od_w_in", "od_w_out", "ffn_w13", "ffn_w2"]
    big_shapes = [wts[n_].shape for n_ in big_names]
    wb = all_gather("ag_weights", _pack([wts[n_] for n_ in big_names], BF16, FLAT_W, FLAT_ROWS))
    g_ev_in, g_ev_out, g_od_in, g_od_out, g_w13, g_w2 = _unpack(wb, big_shapes, lead=1)

    def cols_full(g):
        g = jnp.moveaxis(g, 0, -2)
        return g.reshape(g.shape[:-2] + (g.shape[-2] * g.shape[-1],))

    def rows_full(g):
        g = jnp.moveaxis(g, 0, -3)
        return g.reshape(g.shape[:-3] + (g.shape[-3] * g.shape[-2], g.shape[-1]))

    w_ev_in = cols_full(g_ev_in)[0]
    w_ev_out = rows_full(g_ev_out)[0]
    w_od_in = cols_full(g_od_in)[0]
    w_od_main, w_od_gate = w_od_in[:, :2 * kw + 2 * vw], w_od_in[:, 2 * kw + 2 * vw:]
    w_od_out = rows_full(g_od_out)[0]
    w13 = cols_full(g_w13)
    w2 = rows_full(g_w2)

    b_loc = lax.dynamic_slice_in_dim(b_ada, me * ada_loc, ada_loc, axis=1).reshape(depth, 1, ada_loc)
    ada_cb = _pick(ada_loc, TN_PREFS)
    ada_grid = (depth, ada_loc // ada_cb)
    ada_args = [
        TArg(c_all, (N_DEV, d), lambda l, j: (0, 0), "const"),
        TArg(c_ctx.reshape(1, d), (1, d), lambda l, j: (0, 0), "par", (0, 1)),
        TArg(w_ada, (None, d, ada_cb), lambda l, j: (l, 0, j)),
        TArg(b_loc, (None, 1, ada_cb), lambda l, j: (l, 0, j)),
    ]
    (m_loc,) = tile_fwd("ada_fwd", f_ada, ada_grid, ada_args,
                        [((depth, 2 * N_DEV, ada_loc), F32, (None, 2 * N_DEV, ada_cb), lambda l, j: (l, 0, j))])
    m_all = all_gather("ag_mod", m_loc.reshape(depth * 2 * N_DEV, ada_loc)).reshape(N_DEV, depth, 2 * N_DEV, ada_loc)
    mods = []
    for layer in range(depth):
        lat = lax.dynamic_index_in_dim(m_all[:, layer], me, axis=1, keepdims=False).reshape(6, d)
        cxt = lax.dynamic_index_in_dim(m_all[:, layer], N_DEV + me, axis=1, keepdims=False).reshape(6, d)
        mods.append(jnp.stack([cxt, lat]))

    lb_slots = [TArg(lb_full[:, j], (2, aw), lambda i: (0, 0)) for j in range(depth + 1)]
    (lb0,) = tile_fwd("lb_fwd", f_lb(0), (1,), lb_slots, [((2, aw), F32, (2, aw), lambda i: (0, 0))])
    lb0r = lb0.reshape(2, n_ah, 1, HEAD)

    full_row = lambda i: (i, 0)
    par0 = lambda i: (0, 0)

    def nm_args(xs, layer, slot):
        return [TArg(xs, (ROW_TILE, d), full_row),
                TArg(nw_full[layer, slot].reshape(1, d), (1, d), par0, "par", (0,)),
                TArg(mods[layer], (2, 6, d), lambda i: (0, 0, 0), "par", (0,))]

    def norm_mod(name, xs, layer, slot, si, ci):
        (h,) = tile_fwd(name, f_norm_mod(si, ci, nct), (nt,), nm_args(xs, layer, slot),
                        [((t, d), BF16, (ROW_TILE, d), full_row)])
        return h

    def norm_mod_bwd(name, xs, layer, slot, si, ci, dh, carry):
        return tile_bwd(name, f_norm_mod(si, ci, nct, True), (nt,), nm_args(xs, layer, slot),
                        [(dh, (ROW_TILE, d), full_row), (carry, (ROW_TILE, d), full_row)])

    def gr_args(xs, ys, layer, slot):
        return [TArg(xs, (ROW_TILE, d), full_row, grad=False), TArg(ys, (ROW_TILE, d), full_row, gdtype=BF16),
                TArg(nw_full[layer, slot].reshape(1, d), (1, d), par0, "par", (0,)),
                TArg(mods[layer], (2, 6, d), lambda i: (0, 0, 0), "par", (0,))]

    def gate_res(name, xs, ys, layer, slot, gi):
        (o,) = tile_fwd(name, f_gate_res(gi, nct), (nt,), gr_args(xs, ys, layer, slot),
                        [((t, d), F32, (ROW_TILE, d), full_row)])
        return o

    def gate_res_bwd(name, xs, ys, layer, slot, gi, dx):
        return tile_bwd(name, f_gate_res(gi, nct), (nt,), gr_args(xs, ys, layer, slot),
                        [(dx, (ROW_TILE, d), full_row)])

    sw_rows = ROW_TILE // 2

    def sw_args(gu):
        return [TArg(gu, (sw_rows, 2 * ffn_h), full_row, gdtype=BF16)]

    def ffn_fwd(tag, xs, layer):
        h2 = norm_mod(f"nm2_{tag}", xs, layer, 2, 3, 4)
        gu = matmul(f"w13_{tag}", h2, w13[layer], "nn")
        (act,) = tile_fwd(f"swiglu_{tag}", f_swiglu, (t // sw_rows,), sw_args(gu),
                          [((t, ffn_h), BF16, (sw_rows, ffn_h), full_row)])
        fo = matmul(f"w2_{tag}", act, w2[layer], "nn")
        xn = gate_res(f"gr2_{tag}", xs, fo, layer, 3, 5)
        return xn, (xs, h2, gu, act, fo)

    def ffn_bwd(tag, saved, layer, dxn, acc):
        xs, h2, gu, act, fo = saved
        dfo, dnw3, dmod_a = gate_res_bwd(f"gr2b_{tag}", xs, fo, layer, 3, 5, dxn)
        dact = matmul(f"w2d_{tag}", dfo, w2[layer], "nt")
        dw2 = matmul(f"w2w_{tag}", act, dfo, "tn")
        (dgu,) = tile_bwd(f"swiglub_{tag}", f_swiglu, (t // sw_rows,), sw_args(gu), [(dact, (sw_rows, ffn_h), full_row)])
        dh2 = matmul(f"w13d_{tag}", dgu, w13[layer], "nt")
        dw13 = matmul(f"w13w_{tag}", h2, dgu, "tn")
        dxs, dnw2, dmod_b = norm_mod_bwd(f"nm2b_{tag}", xs, layer, 2, 3, 4, dh2, dxn)
        acc["ffn_w13"][layer] = dw13
        acc["ffn_w2"][layer] = dw2
        acc["norm_w"][layer][2] = dnw2
        acc["norm_w"][layer][3] = dnw3
        acc["mods"][layer].extend([dmod_a, dmod_b])
        return dxs

    def head_cols(width):
        return (ROW_TILE, width)

    n_a = t // A_CHUNK
    nca = n_ctx // A_CHUNK

    def a_tok(rev):
        if not rev:
            return lambda i: i
        return lambda i: jnp.where(i < nca, nca - 1 - i, n_a + nca - 1 - i)

    def hg_args(p, direction):
        tok = a_tok(direction == 1)
        blk = (A_CHUNK, HEAD)
        return [TArg(p, blk, lambda h, i: (tok(i), h)),
                TArg(p, blk, lambda h, i: (tok(i), (1 + direction) * n_ah + h)),
                TArg(p, blk, lambda h, i: (tok(i), 3 * n_ah + h)),
                TArg(lb0r, (None, None, 1, HEAD), lambda h, i: (direction, h, 0, 0), "par", (1,))]

    pmats = jnp.asarray(np.stack([_pool_mats(n_ctx), _pool_mats(GRID_W)]))

    def pool_args(p):
        return [TArg(p, (ROW_TILE, dg), lambda g, i: (i, 5 * n_grp + g)),
                TArg(pmats, (None, None, ROW_TILE, ROW_TILE), lambda g, i: (jnp.where(i < nct, 0, 1), g, 0, 0), "const"),
                TArg(pw_full, (None, dg, dg), lambda g, i: (g, 0, 0), "par", (1,)),
                TArg(ev_pool_scale, (1, dg), lambda g, i: (0, g), "par", (1,))]

    def ro_args(o_f, o_b, gate_arr, gate_off, nw_arr, n_heads):
        blk = (ROW_TILE, HEAD)
        return [TArg(o_f, blk, lambda h, i: (i, h)), TArg(o_b, blk, lambda h, i: (i, h), grad=False),
                TArg(gate_arr, blk, lambda h, i: (i, gate_off + h)),
                TArg(nw_arr, (1, HEAD), lambda h, i: (0, 0), "par", (0, 1))]

    def even_fwd(tag, xs, layer):
        h = norm_mod(f"nm1_{tag}", xs, layer, 0, 0, 1)
        p = matmul(f"win_{tag}", h, w_ev_in, "nn")
        outs, saves = [], []
        for direction in (0, 1):
            (o,), sv = scan_fwd(f"hgrn_{tag}_{direction}", f_hgrn2(direction == 1), n_ah, n_a, hg_args(p, direction),
                                [((t, aw), F32, (A_CHUNK, HEAD), lambda hh, i, tok=a_tok(direction == 1): (tok(i), hh))], 1)
            outs.append(o)
            saves.append(sv)
        (a_out,) = tile_fwd(f"ro_{tag}", f_readout, (n_ah, nt), ro_args(outs[0], outs[1], p, 4 * n_ah, ev_a_norm, n_ah),
                            [((t, aw), BF16, (ROW_TILE, HEAD), lambda hh, i: (i, hh))])
        (pooled,) = tile_fwd(f"pool_{tag}", f_pool, (n_grp, nt), pool_args(p),
                             [((t, aw), BF16, (ROW_TILE, dg), lambda g, i: (i, g))])
        cat = assemble(f"cat_{tag}", [[(a_out, aw, 0)], [(pooled, aw, 0)]], BF16)
        y = matmul(f"wout_{tag}", cat, w_ev_out, "nn")
        xn = gate_res(f"gr1_{tag}", xs, y, layer, 1, 2)
        return xn, (xs, h, p, outs, saves, cat, y)

    def even_bwd(tag, saved, layer, dxn, acc):
        xs, h, p, outs, saves, cat, y = saved
        dy, dnw1, dmod_a = gate_res_bwd(f"gr1b_{tag}", xs, y, layer, 1, 2, dxn)
        dcat = matmul(f"woutd_{tag}", dy, w_ev_out, "nt")
        acc["ev_w_out"] = matmul(f"woutw_{tag}", cat, dy, "tn")
        do, dgate, d_anorm = tile_bwd(f"rob_{tag}", f_readout, (n_ah, nt),
                                      ro_args(outs[0], outs[1], p, 4 * n_ah, ev_a_norm, n_ah),
                                      [(dcat, (ROW_TILE, HEAD), lambda hh, i: (i, hh))])
        du, d_pw, d_ps = tile_bwd(f"poolb_{tag}", f_pool, (n_grp, nt), pool_args(p),
                                  [(dcat, (ROW_TILE, dg), lambda g, i: (i, n_grp + g))])
        dq, df, di, dlb = [], [], [], []
        for direction in (0, 1):
            r = scan_bwd(f"hgrnb_{tag}_{direction}", f_hgrn2(direction == 1), n_ah, n_a, hg_args(p, direction),
                         saves[direction],
                         [(do, (A_CHUNK, HEAD), lambda hh, i, tok=a_tok(direction == 1): (tok(i), hh))])
            dq.append(r[0])
            df.append(r[1])
            di.append(r[2])
            dlb.append(r[3])
        sec = lambda arr, s: (arr, aw, s)
        dp = assemble(f"dp_{tag}", [[sec(dq[0], 0), sec(dq[1], 0)], [sec(df[0], 1)], [sec(df[1], 2)],
                                    [sec(di[0], 3), sec(di[1], 3)], [sec(dgate, 4)], [sec(du, 5)]], BF16)
        dh = matmul(f"wind_{tag}", dp, w_ev_in, "nt")
        acc["ev_w_in"] = matmul(f"winw_{tag}", h, dp, "tn")
        dxs, dnw0, dmod_b = norm_mod_bwd(f"nm1b_{tag}", xs, layer, 0, 0, 1, dh, dxn)
        acc["norm_w"][layer][0] = dnw0
        acc["norm_w"][layer][1] = dnw1
        acc["mods"][layer].extend([dmod_a, dmod_b])
        acc["ev_a_norm"] = d_anorm
        acc["ev_pool_w"] = d_pw
        acc["ev_pool_scale"] = d_ps
        acc["lb0"] = jnp.stack([dlb[0][0], dlb[1][1]]).reshape(2, aw)
        return dxs

    n_c = t // C_CHUNK
    ncc = n_ctx // C_CHUNK

    def c_tok(rev):
        if not rev:
            return lambda i: i
        return lambda i: jnp.where(i < ncc, ncc - 1 - i, n_c + ncc - 1 - i)

    alog = od_A_log[0].reshape(2, n_kh, 2, 1)
    dtb = od_dt_bias[0].reshape(2, n_kh, 2, 1)

    def gd_args(z, gates, direction):
        tok = c_tok(direction == 1)
        gblk = (None, None, None, 2, C_CHUNK)
        sblk = (None, None, 2, 1)
        return [TArg(z, (C_CHUNK, HEAD), lambda kh, i: (tok(i), kh)),
                TArg(z, (C_CHUNK, HEAD), lambda kh, i: (tok(i), n_kh + kh)),
                TArg(z, (C_CHUNK, 2 * HEAD), lambda kh, i: (tok(i), n_kh + kh)),
                TArg(gates, gblk, lambda kh, i: (direction, kh, tok(i), 0, 0)),
                TArg(gates, gblk, lambda kh, i: (2 + direction, kh, tok(i), 0, 0)),
                TArg(alog, sblk, lambda kh, i: (direction, kh, 0, 0), "par", (1,)),
                TArg(dtb, sblk, lambda kh, i: (direction, kh, 0, 0), "par", (1,))]

    def odd_fwd(tag, xs, layer):
        h = norm_mod(f"nm1_{tag}", xs, layer, 0, 0, 1)
        pm = matmul(f"win_{tag}", h, w_od_main, "nn")
        pg = matmul(f"wgate_{tag}", h, w_od_gate, "nn")
        z = conv_fwd(f"conv_{tag}", pm, cv_full, 2 * kw + vw, nct)
        gates = pg.reshape(n_c, C_CHUNK, 4, n_kh, 2).transpose(2, 3, 0, 4, 1)
        outs, saves = [], []
        for direction in (0, 1):
            (o,), sv = scan_fwd(f"gdn_{tag}_{direction}", f_gdn(direction == 1), n_kh, n_c, gd_args(z, gates, direction),
                                [((t, vw), F32, (C_CHUNK, 2 * HEAD), lambda kh, i, tok=c_tok(direction == 1): (tok(i), kh))], 2)
            outs.append(o)
            saves.append(sv)
        n_vh = 2 * n_kh
        (yo,) = tile_fwd(f"ro_{tag}", f_readout, (n_vh, nt), ro_args(outs[0], outs[1], pm, 2 * n_kh + n_vh, od_norm, n_vh),
                         [((t, vw), BF16, (ROW_TILE, HEAD), lambda hh, i: (i, hh))])
        y = matmul(f"wout_{tag}", yo, w_od_out, "nn")
        xn = gate_res(f"gr1_{tag}", xs, y, layer, 1, 2)
        return xn, (xs, h, pm, z, gates, outs, saves, yo, y)

    def odd_bwd(tag, saved, layer, dxn, acc):
        xs, h, pm, z, gates, outs, saves, yo, y = saved
        n_vh = 2 * n_kh
        dy, dnw1, dmod_a = gate_res_bwd(f"gr1b_{tag}", xs, y, layer, 1, 2, dxn)
        dyo = matmul(f"woutd_{tag}", dy, w_od_out, "nt")
        acc["od_w_out"] = matmul(f"woutw_{tag}", yo, dy, "tn")
        do, dzg, d_onorm = tile_bwd(f"rob_{tag}", f_readout, (n_vh, nt),
                                    ro_args(outs[0], outs[1], pm, 2 * n_kh + n_vh, od_norm, n_vh),
                                    [(dyo, (ROW_TILE, HEAD), lambda hh, i: (i, hh))])
        dq, dk, dv, dga, dgb, dal, ddt = [], [], [], [], [], [], []
        for direction in (0, 1):
            r = scan_bwd(f"gdnb_{tag}_{direction}", f_gdn(direction == 1), n_kh, n_c, gd_args(z, gates, direction),
                         saves[direction],
                         [(do, (C_CHUNK, 2 * HEAD), lambda kh, i, tok=c_tok(direction == 1): (tok(i), kh))])
            for lst, v_ in zip((dq, dk, dv, dga, dgb, dal, ddt), r):
                lst.append(v_)
        dz = assemble(f"dz_{tag}", [[(dq[0], kw, 0), (dq[1], kw, 0)], [(dk[0], kw, 1), (dk[1], kw, 1)],
                                    [(dv[0], vw, 1), (dv[1], vw, 1)]], F32)
        du, d_conv = conv_bwd(f"convb_{tag}", pm, cv_full, dz, 2 * kw + vw, nct)
        dpm = assemble(f"dpm_{tag}", [[(du, 2 * kw + vw, 0)], [(dzg, vw, 2)]], BF16)
        dgates = jnp.stack([dga[0][0], dga[1][1], dgb[0][2], dgb[1][3]])
        dpg = dgates.transpose(2, 4, 0, 1, 3).reshape(t, n_gate).astype(BF16)
        dh = matmul(f"wgated_{tag}", dpg, w_od_gate, "nt")
        dh = matmul(f"wind_{tag}", dpm, w_od_main, "nt", add=dh)
        acc["od_w_in"] = jnp.concatenate([matmul(f"winw_{tag}", h, dpm, "tn"), matmul(f"wgatew_{tag}", h, dpg, "tn")], axis=1)
        dxs, dnw0, dmod_b = norm_mod_bwd(f"nm1b_{tag}", xs, layer, 0, 0, 1, dh, dxn)
        acc["norm_w"][layer][0] = dnw0
        acc["norm_w"][layer][1] = dnw1
        acc["mods"][layer].extend([dmod_a, dmod_b])
        acc["od_norm"] = d_onorm
        acc["od_conv"] = d_conv
        acc["od_A_log"] = jnp.stack([dal[0][0], dal[1][1]]).reshape(1, 2, n_vh)
        acc["od_dt_bias"] = jnp.stack([ddt[0][0], ddt[1][1]]).reshape(1, 2, n_vh)
        return dxs

    xs0 = jnp.concatenate([ctx[0], x[0]], axis=0)
    xs1, sv_e = even_fwd("l0", xs0, 0)
    xs2, sv_f0 = ffn_fwd("l0", xs1, 0)
    xs3, sv_o = odd_fwd("l1", xs2, 1)
    xs4, sv_f1 = ffn_fwd("l1", xs3, 1)
    loss_loc, dxs = loss_kernel("loss", xs4, loss_target[0], nct)
    loss = lax.psum(loss_loc, ("x", "y", "c"))

    acc = {"norm_w": [[None] * 4 for _ in range(depth)], "mods": [[] for _ in range(depth)],
           "ffn_w13": [None] * depth, "ffn_w2": [None] * depth}
    dxs = ffn_bwd("l1", sv_f1, 1, dxs, acc)
    dxs = odd_bwd("l1", sv_o, 1, dxs, acc)
    dxs = ffn_bwd("l0", sv_f0, 0, dxs, acc)
    dxs = even_bwd("l0", sv_e, 0, dxs, acc)
    grad_x = dxs[n_ctx:].reshape(1, seq, d)

    (d_lb_slots) = tile_bwd("lb_bwd", f_lb(0), (1,), lb_slots, [(acc["lb0"], (2, aw), lambda i: (0, 0))])
    d_ev_lb = jnp.stack(d_lb_slots, axis=1)

    dmods = jnp.stack([functools.reduce(jnp.add, acc["mods"][layer]) for layer in range(depth)])
    dm_all = all_gather("ag_dmod", dmods.reshape(depth * 2 * 6, d)).reshape(N_DEV, depth, 2, 6 * d)
    dm_cols = lax.dynamic_slice_in_dim(dm_all, me * ada_loc, ada_loc, axis=3)
    dm_loc = jnp.concatenate([dm_cols[:, :, 1].transpose(1, 0, 2), dm_cols[:, :, 0].transpose(1, 0, 2)], axis=1)
    d_cctx_part, d_w_ada, d_b_loc = tile_bwd("ada_bwd", f_ada, ada_grid, ada_args,
                                             [(dm_loc, (None, 2 * N_DEV, ada_cb), lambda l, j: (l, 0, j))])

    d_b_full = lax.dynamic_update_slice_in_dim(jnp.zeros_like(b_ada), d_b_loc.reshape(depth, ada_loc), me * ada_loc, axis=1)
    d_nw = jnp.stack([jnp.stack([acc["norm_w"][layer][s].reshape(d) for s in range(4)]) for layer in range(depth)])
    small_grads = [d_cctx_part.reshape(d), d_b_full, d_nw, d_ev_lb, acc["ev_a_norm"], acc["ev_pool_w"],
                   acc["ev_pool_scale"], acc["od_conv"], acc["od_A_log"], acc["od_dt_bias"], acc["od_norm"]]
    sg_shapes = [a.shape for a in small_grads]
    sg = all_gather("ag_smallgrads", _pack(small_grads, F32, FLAT_W, SUBLANES))
    sg_sum = sum_leading("sum_smallgrads", sg, F32)
    (g_cctx, g_bada, g_nw, g_lb, g_anorm, g_pw, g_ps, g_conv, g_alog, g_dtb, g_onorm) = _unpack(sg_sum, sg_shapes)

    def my_cols(full, axis):
        loc = full.shape[axis] // N_DEV
        return lax.dynamic_slice_in_dim(full, me * loc, loc, axis=axis)

    grads = {
        "c_ctx": g_cctx, "w_ada": d_w_ada, "b_ada": g_bada, "norm_w": my_cols(g_nw, 2), "ev_lb": my_cols(g_lb, 2),
        "ev_a_norm": g_anorm, "ev_pool_w": my_cols(g_pw, 1)[None], "ev_pool_scale": g_ps,
        "od_conv": my_cols(g_conv, 1)[None], "od_A_log": g_alog, "od_dt_bias": g_dtb, "od_norm": g_onorm,
    }

    def col_dest(g):
        g = g.reshape(g.shape[:-1] + (N_DEV, g.shape[-1] // N_DEV))
        return jnp.moveaxis(g, -2, 0)

    def row_dest(g):
        g = g.reshape(g.shape[:-2] + (N_DEV, g.shape[-2] // N_DEV, g.shape[-1]))
        return jnp.moveaxis(g, -3, 0)

    big_grads = [col_dest(acc["ev_w_in"][None]), row_dest(acc["ev_w_out"][None]), col_dest(acc["od_w_in"][None]),
                 row_dest(acc["od_w_out"][None]), col_dest(jnp.stack(acc["ffn_w13"])), row_dest(jnp.stack(acc["ffn_w2"]))]
    gp = _pack(big_grads, F32, FLAT_W, FLAT_ROWS, lead=1)
    rows_b = gp.shape[1]
    gp = gp.reshape(4, 2, rows_b, FLAT_W)
    own = lax.dynamic_index_in_dim(gp, ac, axis=1, keepdims=False)
    send = lax.dynamic_index_in_dim(gp, 1 - ac, axis=1, keepdims=False)
    got = sibling_exchange("rs_sibling", send)
    chip_sum = add_cast("rs_add", own, got, BF16)
    landed = chip_exchange("rs_chips", chip_sum)
    g_big = sum_leading("rs_sum", landed, F32)
    for n_, g_ in zip(big_names, _unpack(g_big, big_shapes)):
        grads[n_] = g_

    gl = [grads[n_].reshape(wts[n_].shape) for n_ in names]
    shapes = [wts[n_].shape for n_ in names]
    pk = lambda lst: _pack(lst, F32, FLAT_W, FLAT_ROWS)
    delta, new_m, new_v = adamw("adamw", pk(gl), pk([wts[n_] for n_ in names]), pk([mom1[n_] for n_ in names]),
                                pk([mom2[n_] for n_ in names]))
    return (loss, grad_x, *gl, *_unpack(delta, shapes), *_unpack(new_m, shapes), *_unpack(new_v, shapes))
```

```python
import functools
from typing import Any, NamedTuple

import numpy as np

import jax
import jax.numpy as jnp
from jax import lax
from jax.experimental import pallas as pl
from jax.experimental.pallas import tpu as pltpu

F32 = jnp.float32
BF16 = jnp.bfloat16
MESH = pl.DeviceIdType.MESH
N_DEV = 8

EPS = 1e-6
GRID_W = 64
HEAD = 128
A_CHUNK = 32
C_CHUNK = 64
C_CONV = 4
POOL_WINDOWS = (2, 4, 8, 16)
ADAM_LR, ADAM_B1, ADAM_B2, ADAM_EPS, ADAM_WD, ADAM_STEP = 0.001, 0.9, 0.999, 1e-08, 0.01, 10

VMEM_LIMIT_BYTES = 56 * 1024 * 1024
LANES = 128
SUBLANES = 8
ROW_TILE = 256
FLAT_W = 1024
FLAT_ROWS = 512
TM_PREFS = (1056, 768, 512, 256, 128, 64, 32, 16)
TN_PREFS = (512, 384, 1408, 256, 128)
TK_PREFS = (2048, 2816, 1408, 1024, 768, 512, 384, 256, 128)
TO_PREFS = (1024, 1408, 768, 704, 512, 384, 256, 128)
HEADS_PER_STEP = 4


def _pick(dim, prefs):
    for p in prefs:
        if p <= dim and dim % p == 0:
            return p
    return dim


def _cparams(ngrid):
    return pltpu.CompilerParams(dimension_semantics=("arbitrary",) * ngrid, vmem_limit_bytes=VMEM_LIMIT_BYTES)


def _sds(shape, dtype):
    return jax.ShapeDtypeStruct(tuple(shape), dtype)


def _dot(a, b, ca, cb, hi):
    dims = (((ca,), (cb,)), ((), ()))
    if hi:
        return lax.dot_general(a.astype(F32), b.astype(F32), dims, precision=lax.Precision.HIGHEST,
                               preferred_element_type=F32)
    return lax.dot_general(a.astype(BF16), b.astype(BF16), dims, preferred_element_type=F32)


@functools.partial(jax.custom_vjp, nondiff_argnums=(2, 3, 4))
def mm(a, b, ca=1, cb=0, hi=False):
    return _dot(a, b, ca, cb, hi)


def _mm_fwd(a, b, ca, cb, hi):
    return _dot(a, b, ca, cb, hi), (a, b)


def _mm_bwd(ca, cb, hi, res, g):
    a, b = res
    da = _dot(g, b, 1, 1 - cb, hi) if ca == 1 else _dot(b, g, 1 - cb, 1, hi)
    db = _dot(a, g, 1 - ca, 0, hi) if cb == 0 else _dot(g, a, 0, 1 - ca, hi)
    return da, db


mm.defvjp(_mm_fwd, _mm_bwd)


def _iota2(n, m, axis):
    return lax.broadcasted_iota(jnp.int32, (n, m), axis)


class TArg(NamedTuple):
    arr: Any
    block: tuple
    imap: Any
    kind: str = "row"
    acc: tuple = ()
    gdtype: Any = F32
    grad: bool = True


def _load(ref):
    v = ref[...]
    return v.astype(F32) if jnp.issubdtype(v.dtype, jnp.floating) else v


def tile_fwd(name, f, grid, args, outs):
    n_in, ng = len(args), len(grid)

    def body(*refs):
        pids = tuple(pl.program_id(k) for k in range(ng))
        res = f(pids, *[_load(r) for r in refs[:n_in]])
        for r, v in zip(refs[n_in:], res):
            r[...] = v.astype(r.dtype)

    return pl.pallas_call(
        body, grid=grid, name=name,
        in_specs=[pl.BlockSpec(a.block, a.imap) for a in args],
        out_specs=[pl.BlockSpec(b, im) for (_, _, b, im) in outs],
        out_shape=[_sds(s, d) for (s, d, _, _) in outs],
        compiler_params=_cparams(ng),
    )(*[a.arr for a in args])


def _store_grads(args, diff, pids, g_refs, d):
    for k, gr, dv in zip(diff, g_refs, d):
        a = args[k]
        if a.kind == "row" or not a.acc:
            gr[...] = dv.astype(gr.dtype)
        else:
            first = pids[a.acc[0]] == 0
            for ax in a.acc[1:]:
                first = jnp.logical_and(first, pids[ax] == 0)

            @pl.when(first)
            def _(gr=gr, dv=dv):
                gr[...] = dv.astype(gr.dtype)

            @pl.when(jnp.logical_not(first))
            def _(gr=gr, dv=dv):
                gr[...] += dv.astype(gr.dtype)


def tile_bwd(name, f, grid, args, cts):
    n_in, n_ct, ng = len(args), len(cts), len(grid)
    diff = [k for k, a in enumerate(args) if a.kind != "const" and a.grad]

    def body(*refs):
        pids = tuple(pl.program_id(k) for k in range(ng))
        vals = [_load(r) for r in refs[:n_in]]

        def g(*dv):
            full = list(vals)
            for k, v in zip(diff, dv):
                full[k] = v
            return tuple(f(pids, *full))

        _, vjp = jax.vjp(g, *[vals[k] for k in diff])
        d = vjp(tuple(_load(r) for r in refs[n_in:n_in + n_ct]))
        _store_grads(args, diff, pids, refs[n_in + n_ct:], d)

    return pl.pallas_call(
        body, grid=grid, name=name,
        in_specs=[pl.BlockSpec(a.block, a.imap) for a in args] + [pl.BlockSpec(b, im) for (_, b, im) in cts],
        out_specs=[pl.BlockSpec(args[k].block, args[k].imap) for k in diff],
        out_shape=[_sds(args[k].arr.shape, args[k].gdtype) for k in diff],
        compiler_params=_cparams(ng),
    )(*[a.arr for a in args], *[c[0] for c in cts])


def scan_fwd(name, f, n_heads, n_steps, args, outs, n_state):
    n_in, n_out = len(args), len(outs)
    sblock = (None, None, HEAD, HEAD)

    def body(*refs):
        in_refs = refs[:n_in]
        out_refs = refs[n_in:n_in + n_out]
        save_refs = refs[n_in + n_out:n_in + n_out + n_state]
        s_refs = refs[n_in + n_out + n_state:]

        @pl.when(pl.program_id(1) == 0)
        def _():
            for s in s_refs:
                s[...] = jnp.zeros_like(s)

        states = tuple(s[...] for s in s_refs)
        for sv, s in zip(save_refs, states):
            sv[...] = s
        new_states, res = f(states, *[_load(r) for r in in_refs])
        for s, v in zip(s_refs, new_states):
            s[...] = v
        for r, v in zip(out_refs, res):
            r[...] = v.astype(r.dtype)

    res = pl.pallas_call(
        body, grid=(n_heads, n_steps), name=name,
        in_specs=[pl.BlockSpec(a.block, a.imap) for a in args],
        out_specs=[pl.BlockSpec(b, im) for (_, _, b, im) in outs]
        + [pl.BlockSpec(sblock, lambda h, i: (h, i, 0, 0))] * n_state,
        out_shape=[_sds(s, d) for (s, d, _, _) in outs] + [_sds((n_heads, n_steps, HEAD, HEAD), F32)] * n_state,
        scratch_shapes=[pltpu.VMEM((HEAD, HEAD), F32)] * n_state,
        compiler_params=_cparams(2),
    )(*[a.arr for a in args])
    return res[:n_out], res[n_out:]


def scan_bwd(name, f, n_heads, n_steps, args, saves, cts):
    n_in, n_ct, n_state = len(args), len(cts), len(saves)
    diff = [k for k, a in enumerate(args) if a.kind != "const" and a.grad]
    sblock = (None, None, HEAD, HEAD)

    def rv(im):
        return lambda h, i: im(h, n_steps - 1 - i)

    def body(*refs):
        in_refs = refs[:n_in]
        save_refs = refs[n_in:n_in + n_state]
        ct_refs = refs[n_in + n_state:n_in + n_state + n_ct]
        g_refs = refs[n_in + n_state + n_ct:n_in + n_state + n_ct + len(diff)]
        ds_refs = refs[n_in + n_state + n_ct + len(diff):]
        pids = (pl.program_id(0), pl.program_id(1))

        @pl.when(pids[1] == 0)
        def _():
            for s in ds_refs:
                s[...] = jnp.zeros_like(s)

        vals = [_load(r) for r in in_refs]

        def g(states, *dv):
            full = list(vals)
            for k, v in zip(diff, dv):
                full[k] = v
            new_states, res = f(states, *full)
            return tuple(new_states), tuple(res)

        _, vjp = jax.vjp(g, tuple(s[...] for s in save_refs), *[vals[k] for k in diff])
        d = vjp((tuple(s[...] for s in ds_refs), tuple(_load(r) for r in ct_refs)))
        for s, v in zip(ds_refs, d[0]):
            s[...] = v
        _store_grads(args, diff, pids, g_refs, d[1:])

    return pl.pallas_call(
        body, grid=(n_heads, n_steps), name=name,
        in_specs=[pl.BlockSpec(a.block, rv(a.imap)) for a in args]
        + [pl.BlockSpec(sblock, rv(lambda h, i: (h, i, 0, 0)))] * n_state
        + [pl.BlockSpec(b, rv(im)) for (_, b, im) in cts],
        out_specs=[pl.BlockSpec(args[k].block, rv(args[k].imap)) for k in diff],
        out_shape=[_sds(args[k].arr.shape, args[k].gdtype) for k in diff],
        scratch_shapes=[pltpu.VMEM((HEAD, HEAD), F32)] * n_state,
        compiler_params=_cparams(2),
    )(*[a.arr for a in args], *saves, *[c[0] for c in cts])


def matmul(name, a, b, mode, add=None, out_dtype=F32, slabs=False):
    o_spec = None
    if mode == "nn":
        m, k = a.shape
        ns = b.shape[2] if slabs else b.shape[1]
        n = N_DEV * ns if slabs else ns
        to_m, to_n, tr = _pick(m, TM_PREFS), _pick(ns, TN_PREFS), _pick(k, TK_PREFS)
        nb = ns // to_n
        grid = (m // to_m, n // to_n, k // tr)
        a_spec = pl.BlockSpec((to_m, tr), lambda i, j, l: (i, l))
        if slabs:
            b_spec = pl.BlockSpec((None, tr, to_n), lambda i, j, l: (j // nb, l, j % nb))
        else:
            b_spec = pl.BlockSpec((tr, to_n), lambda i, j, l: (l, j))
        dims, oshape = (1, 0), (m, n)
    elif mode == "nt":
        m, n = a.shape
        k = b.shape[1] if slabs else b.shape[0]
        ns = n // N_DEV if slabs else n
        to_m, to_n, tr = _pick(m, TM_PREFS), _pick(k, TO_PREFS), _pick(ns, TK_PREFS)
        nb = ns // tr
        grid = (m // to_m, k // to_n, n // tr)
        a_spec = pl.BlockSpec((to_m, tr), lambda i, j, l: (i, l))
        if slabs:
            b_spec = pl.BlockSpec((None, to_n, tr), lambda i, j, l: (l // nb, j, l % nb))
        else:
            b_spec = pl.BlockSpec((to_n, tr), lambda i, j, l: (j, l))
        dims, oshape = (1, 1), (m, k)
    else:
        (t, k), n = a.shape, b.shape[1]
        ns = n // N_DEV if slabs else n
        to_m, to_n, tr = _pick(k, TO_PREFS), _pick(ns, TO_PREFS), _pick(t, TM_PREFS)
        nb = ns // to_n
        grid = (k // to_m, n // to_n, t // tr)
        a_spec = pl.BlockSpec((tr, to_m), lambda i, j, l: (l, i))
        b_spec = pl.BlockSpec((tr, to_n), lambda i, j, l: (l, j))
        dims, oshape = (0, 0), (k, n)
        if slabs:
            o_spec = pl.BlockSpec((None, to_m, to_n), lambda i, j, l: (j // nb, i, j % nb))
            oshape = (N_DEV, k, ns)
    n_red = grid[2]
    if o_spec is None:
        o_spec = pl.BlockSpec((to_m, to_n), lambda i, j, l: (i, j))
    has_add = add is not None

    def body(a_ref, b_ref, *rest):
        add_ref = rest[0] if has_add else None
        o_ref = rest[1] if has_add else rest[0]
        part = lax.dot_general(a_ref[...].astype(BF16), b_ref[...].astype(BF16),
                               (((dims[0],), (dims[1],)), ((), ())), preferred_element_type=F32)

        def finish(v):
            if has_add:
                v = v + add_ref[...]
            o_ref[...] = v.astype(o_ref.dtype)

        if n_red == 1:
            finish(part)
        else:
            acc = rest[-1]
            step = pl.program_id(2)

            @pl.when(step == 0)
            def _():
                acc[...] = part

            @pl.when(step > 0)
            def _():
                acc[...] += part

            @pl.when(step == n_red - 1)
            def _():
                finish(acc[...])

    return pl.pallas_call(
        body, grid=grid, name=name,
        in_specs=[a_spec, b_spec] + ([o_spec] if has_add else []),
        out_specs=o_spec, out_shape=_sds(oshape, out_dtype),
        scratch_shapes=[pltpu.VMEM((to_m, to_n), F32)] if n_red > 1 else [],
        compiler_params=_cparams(3),
    )(a, b, *([add] if has_add else []))


def assemble(name, pieces, out_dtype):
    flat = [s for piece in pieces for s in piece]
    t = flat[0][0].shape[0]
    widths = [piece[0][1] for piece in pieces]

    def body(*refs):
        o_ref, k, off = refs[-1], 0, 0
        for piece, w in zip(pieces, widths):
            v = refs[k][...].astype(F32)
            k += 1
            for _ in piece[1:]:
                v = v + refs[k][...].astype(F32)
                k += 1
            o_ref[:, off:off + w] = v.astype(o_ref.dtype)
            off += w

    tr = ROW_TILE // 2
    return pl.pallas_call(
        body, grid=(t // tr,), name=name,
        in_specs=[pl.BlockSpec((tr, w), functools.partial(lambda i, cb: (i, cb), cb=cb)) for (_, w, cb) in flat],
        out_specs=pl.BlockSpec((tr, sum(widths)), lambda i: (i, 0)),
        out_shape=_sds((t, sum(widths)), out_dtype),
        compiler_params=_cparams(1),
    )(*[s[0] for s in flat])


ELEM_ROWS = (128, 64, 32, 16, 8)


def _rows2d(a, lead=0):
    return a.reshape(a.shape[:lead] + (-1, a.shape[-1]))


def sum_leading(name, arr, out_dtype):
    k, rows, w = arr.shape
    tr = _pick(rows, ELEM_ROWS)

    def body(a_ref, o_ref):
        v = a_ref[0].astype(F32)
        for j in range(1, k):
            v = v + a_ref[j].astype(F32)
        o_ref[...] = v.astype(o_ref.dtype)

    return pl.pallas_call(
        body, grid=(rows // tr,), name=name,
        in_specs=[pl.BlockSpec((k, tr, w), lambda i: (0, i, 0))],
        out_specs=pl.BlockSpec((tr, w), lambda i: (i, 0)),
        out_shape=_sds((rows, w), out_dtype), compiler_params=_cparams(1),
    )(arr)


def add_own(name, g8, got, core, out_dtype):
    _, rows, w = g8.shape
    tr = _pick(rows, ELEM_ROWS)

    def body(core_ref, a_ref, b_ref, o_ref):
        o_ref[...] = (a_ref[...] + b_ref[...]).astype(o_ref.dtype)

    spec = pl.BlockSpec((None, tr, w), lambda q, i, core_ref: (q, i, 0))
    return pl.pallas_call(
        body, name=name,
        grid_spec=pltpu.PrefetchScalarGridSpec(
            num_scalar_prefetch=1, grid=(4, rows // tr),
            in_specs=[pl.BlockSpec((None, tr, w), lambda q, i, core_ref: (2 * q + core_ref[0], i, 0)), spec],
            out_specs=spec),
        out_shape=_sds(got.shape, out_dtype), compiler_params=_cparams(2),
    )(core, g8, got)


def adamw(name, g, w, m, v):
    rows, wd = g.shape
    tr = _pick(rows, ELEM_ROWS)

    def body(g_ref, w_ref, m_ref, v_ref, d_ref, nm_ref, nv_ref):
        gv = g_ref[...]
        mn = ADAM_B1 * m_ref[...] + (1.0 - ADAM_B1) * gv
        vn = ADAM_B2 * v_ref[...] + (1.0 - ADAM_B2) * jnp.square(gv)
        m_hat = mn / (1.0 - ADAM_B1 ** ADAM_STEP)
        v_hat = vn / (1.0 - ADAM_B2 ** ADAM_STEP)
        d_ref[...] = -ADAM_LR * (m_hat / (jnp.sqrt(v_hat) + ADAM_EPS) + ADAM_WD * w_ref[...])
        nm_ref[...] = mn
        nv_ref[...] = vn

    spec = pl.BlockSpec((tr, wd), lambda i: (i, 0))
    return pl.pallas_call(
        body, grid=(rows // tr,), name=name, in_specs=[spec] * 4, out_specs=[spec] * 3,
        out_shape=[_sds(g.shape, F32)] * 3, compiler_params=_cparams(1),
    )(g, w, m, v)


def loss_kernel(name, xs, target, n_ctx_tiles):
    t, d = xs.shape
    nt = t // ROW_TILE

    def body(x_ref, t_ref, dx_ref, l_ref):
        i = pl.program_id(0)
        is_lat = i >= n_ctx_tiles
        err = jnp.where(is_lat, x_ref[...] - t_ref[...], 0.0)
        dx_ref[...] = err / d
        part = 0.5 * jnp.sum(jnp.mean(jnp.square(err), axis=-1, keepdims=True), axis=0, keepdims=True)

        @pl.when(i == 0)
        def _():
            l_ref[...] = jnp.zeros_like(l_ref)

        l_ref[...] += jnp.broadcast_to(part, l_ref.shape)

    dx, l = pl.pallas_call(
        body, grid=(nt,), name=name,
        in_specs=[pl.BlockSpec((ROW_TILE, d), lambda i: (i, 0)),
                  pl.BlockSpec((ROW_TILE, d), lambda i: (jnp.maximum(i - n_ctx_tiles, 0), 0))],
        out_specs=[pl.BlockSpec((ROW_TILE, d), lambda i: (i, 0)), pl.BlockSpec((SUBLANES, LANES), lambda i: (0, 0))],
        out_shape=[_sds((t, d), F32), _sds((SUBLANES, LANES), F32)], compiler_params=_cparams(1),
    )(xs, target)
    return l[0, 0], dx


CONV_COLS = 512
CONV_LEFT = C_CONV // 2


def _conv_halo_specs(t, n_ctx_tiles):
    nt = t // ROW_TILE
    per = ROW_TILE // SUBLANES
    cur = pl.BlockSpec((ROW_TILE, CONV_COLS), lambda j, i: (i, j))
    prev = pl.BlockSpec((SUBLANES, CONV_COLS), lambda j, i: (jnp.maximum(i * per - 1, 0), j))
    nxt = pl.BlockSpec((SUBLANES, CONV_COLS), lambda j, i: (jnp.minimum((i + 1) * per, nt * per - 1), j))
    return cur, prev, nxt


def _fill_ext(ext, prev_ref, cur_ref, next_ref, i, nt, n_ctx_tiles):
    has_prev = jnp.logical_and(i != 0, i != n_ctx_tiles)
    has_next = jnp.logical_and(i != n_ctx_tiles - 1, i != nt - 1)
    ext[0:SUBLANES, :] = jnp.where(has_prev, prev_ref[...], 0.0)
    ext[SUBLANES:SUBLANES + ROW_TILE, :] = cur_ref[...]
    ext[SUBLANES + ROW_TILE:, :] = jnp.where(has_next, next_ref[...], 0.0)


def conv_fwd(name, p, w, width, n_ctx_tiles):
    t = p.shape[0]
    nt = t // ROW_TILE
    cur, prev, nxt = _conv_halo_specs(t, n_ctx_tiles)

    def body(c_ref, p_ref, n_ref, w_ref, o_ref, ext):
        _fill_ext(ext, p_ref, c_ref, n_ref, pl.program_id(1), nt, n_ctx_tiles)
        acc = None
        for j in range(C_CONV):
            term = ext[pl.ds(SUBLANES + j - CONV_LEFT, ROW_TILE), :] * w_ref[j:j + 1, :]
            acc = term if acc is None else acc + term
        o_ref[...] = acc

    return pl.pallas_call(
        body, grid=(width // CONV_COLS, nt), name=name,
        in_specs=[cur, prev, nxt, pl.BlockSpec((C_CONV, CONV_COLS), lambda j, i: (0, j))],
        out_specs=cur, out_shape=_sds((t, width), F32),
        scratch_shapes=[pltpu.VMEM((ROW_TILE + 2 * SUBLANES, CONV_COLS), F32)],
        compiler_params=_cparams(2),
    )(p, p, p, w)


def conv_bwd(name, p, w, dz, width, n_ctx_tiles):
    t = p.shape[0]
    nt = t // ROW_TILE
    cur, prev, nxt = _conv_halo_specs(t, n_ctx_tiles)

    def body(c_ref, p_ref, n_ref, dc_ref, dp_ref, dn_ref, w_ref, du_ref, dw_ref, ext, dext):
        i = pl.program_id(1)
        _fill_ext(ext, p_ref, c_ref, n_ref, i, nt, n_ctx_tiles)
        _fill_ext(dext, dp_ref, dc_ref, dn_ref, i, nt, n_ctx_tiles)
        dzc = dc_ref[...]
        @pl.when(i == 0)
        def _():
            dw_ref[...] = jnp.zeros_like(dw_ref)

        acc = None
        for j in range(C_CONV):
            term = dext[pl.ds(SUBLANES + CONV_LEFT - j, ROW_TILE), :] * w_ref[j:j + 1, :]
            acc = term if acc is None else acc + term
            dw_ref[j:j + 1, :] += jnp.sum(dzc * ext[pl.ds(SUBLANES + j - CONV_LEFT, ROW_TILE), :], axis=0, keepdims=True)
        du_ref[...] = acc

    wspec = pl.BlockSpec((C_CONV, CONV_COLS), lambda j, i: (0, j))
    return pl.pallas_call(
        body, grid=(width // CONV_COLS, nt), name=name,
        in_specs=[cur, prev, nxt, cur, prev, nxt, wspec],
        out_specs=[cur, wspec], out_shape=[_sds((t, width), F32), _sds((C_CONV, width), F32)],
        scratch_shapes=[pltpu.VMEM((ROW_TILE + 2 * SUBLANES, CONV_COLS), F32)] * 2,
        compiler_params=_cparams(2),
    )(p, p, p, dz, dz, dz, w)


def _rms(x, w):
    return x * lax.rsqrt(jnp.mean(x * x, axis=-1, keepdims=True) + EPS) * w


def _seg_mod(mods, is_ctx):
    return jnp.where(is_ctx, mods[0], mods[1])


def f_norm_mod(shift_i, scale_i, n_ctx_tiles, passthrough=False):
    def f(pids, x, nw, mods):
        m = _seg_mod(mods, pids[0] < n_ctx_tiles)
        h = _rms(x, nw) * (1.0 + m[scale_i:scale_i + 1]) + m[shift_i:shift_i + 1]
        return (h, x) if passthrough else (h,)
    return f


def f_gate_res(gate_i, n_ctx_tiles):
    def f(pids, x, y, nw, mods):
        m = _seg_mod(mods, pids[0] < n_ctx_tiles)
        return (x + m[gate_i:gate_i + 1] * _rms(y, nw),)
    return f


def f_swiglu(pids, gu):
    half = gu.shape[1] // 2
    return (jax.nn.silu(gu[:, :half]) * gu[:, half:],)


def f_readout(pids, o_a, o_b, gate, nw):
    return (_rms(o_a + o_b, nw) * jax.nn.silu(gate),)


def f_pool(pids, u, pmat, pw, scale):
    d = mm(pmat, u, 1, 0, True) - u
    return (mm(d, pw) * scale,)


def f_lb(layer):
    def f(pids, *slots):
        top = slots[0]
        for s in slots[1:]:
            top = jnp.maximum(top, s)
        ex = [jnp.exp(s - top) for s in slots]
        tot = ex[0]
        for e in ex[1:]:
            tot = tot + e
        part = ex[0]
        for e in ex[1:layer + 1]:
            part = part + e
        return (part / tot,)
    return f


def f_ada(pids, c_all, c_ctx, w, b):
    c16 = jnp.concatenate([c_all, jnp.broadcast_to(c_ctx, c_all.shape)], axis=0)
    return (mm(jax.nn.silu(c16), w) + b,)


def _hgrn2_head(st, qr, fr, ir, lb, rev):
    c = A_CHUNK
    mid = c - c // 2 if rev else c // 2 - 1
    ri, ci = _iota2(c, c, 0), _iota2(c, c, 1)
    incl = (ri <= ci) if rev else (ri >= ci)
    q = jax.nn.silu(qr)
    log_f = jnp.log(lb + (1.0 - lb) * jax.nn.sigmoid(fr))
    k = (1.0 - lb) * jax.nn.sigmoid(-fr)
    b = mm(incl.astype(F32), log_f, 1, 0, True)
    b_last = jnp.sum(log_f, axis=0, keepdims=True)
    b_mid = b[mid:mid + 1]
    scores = mm(q * jnp.exp(b - b_mid), k * jnp.exp(b_mid - b), 1, 1)
    o = mm(jnp.where(incl, scores, 0.0), ir) + mm(q * jnp.exp(b), st, 1, 1)
    st_new = st * jnp.exp(b_last) + mm(ir, k * jnp.exp(b_last - b), 0, 0)
    return st_new, o


def f_hgrn2(rev, hb):
    def f(states, qr, fr, ir, lb):
        new, outs = [], []
        for j in range(hb):
            cols = slice(j * HEAD, (j + 1) * HEAD)
            st_new, o = _hgrn2_head(states[j], qr[:, cols], fr[:, cols], ir[:, cols], lb[j], rev)
            new.append(st_new)
            outs.append(o)
        return tuple(new), (jnp.concatenate(outs, axis=1) if hb > 1 else outs[0],)
    return f


def _neumann_inv(a_low):
    n = a_low.shape[0]
    eye = (_iota2(n, n, 0) == _iota2(n, n, 1)).astype(F32)
    p = -a_low
    x = eye + p
    k = 2
    while k < n:
        p = mm(p, p, 1, 0, True)
        x = x + mm(x, p, 1, 0, True)
        k *= 2
    return x


@jax.custom_vjp
def unit_tri_inv(a_low):
    return _neumann_inv(a_low)


def _uti_fwd(a_low):
    x = _neumann_inv(a_low)
    return x, x


def _uti_bwd(x, g):
    return (-mm(mm(x, g, 0, 0, True), x, 1, 1, True),)


unit_tri_inv.defvjp(_uti_fwd, _uti_bwd)


def _l2n(x):
    return x * lax.rsqrt(jnp.sum(x * x, axis=-1, keepdims=True) + EPS)


def _gdn_head(s, q, k, v, a_row, b_row, alog, dtb, rev):
    c = q.shape[0]
    ri, ci = _iota2(c, c, 0), _iota2(c, c, 1)
    causal = (ri <= ci) if rev else (ri >= ci)
    causal_t = (ri >= ci) if rev else (ri <= ci)
    strict = (ri < ci) if rev else (ri > ci)
    eye = ri == ci
    g_row = -jnp.exp(alog) * jax.nn.softplus(a_row + dtb)
    beta_row = jax.nn.sigmoid(b_row)
    g_sq = jnp.broadcast_to(g_row, (c, c))
    beta_col = jnp.sum(jnp.where(eye, jnp.broadcast_to(beta_row, (c, c)), 0.0), axis=1, keepdims=True)
    g_col = jnp.sum(jnp.where(eye, g_sq, 0.0), axis=1, keepdims=True)
    gc_col = jnp.sum(jnp.where(causal, g_sq, 0.0), axis=1, keepdims=True)
    gc_row = jnp.sum(jnp.where(causal_t, jnp.broadcast_to(g_col, (c, c)), 0.0), axis=0, keepdims=True)
    gc_last = jnp.sum(g_row, axis=1, keepdims=True)
    decay = jnp.where(causal, jnp.exp(jnp.where(causal, gc_col - gc_row, 0.0)), 0.0)
    k_beta = k * beta_col
    v_beta = v * beta_col
    a_low = jnp.where(strict, mm(k_beta, k, 1, 1) * decay, 0.0)
    x = unit_tri_inv(a_low)
    egc = jnp.exp(gc_col)
    u = mm(x, v_beta, 1, 0, True)
    w = mm(x, k_beta * egc, 1, 0, True)
    qk = jnp.where(causal, mm(q, k, 1, 1) * decay, 0.0)
    v_new = u - mm(w, s)
    o = mm(q * egc, s) + mm(qk, v_new)
    s_new = s * jnp.exp(gc_last) + mm(k * jnp.exp(gc_last - gc_col), v_new, 0, 0)
    return s_new, o


def f_gdn(rev, khb):
    def f(states, qr, kr, vr, a3, b3, alog3, dtb3):
        new, outs = [], []
        for j in range(khb):
            cols = slice(j * HEAD, (j + 1) * HEAD)
            q = _l2n(jax.nn.silu(qr[:, cols])) * (HEAD ** -0.5)
            k = _l2n(jax.nn.silu(kr[:, cols]))
            for r in range(2):
                hv = 2 * j + r
                v = jax.nn.silu(vr[:, hv * HEAD:(hv + 1) * HEAD])
                s_new, o = _gdn_head(states[hv], q, k, v, a3[j][r:r + 1], b3[j][r:r + 1],
                                     alog3[j][r:r + 1], dtb3[j][r:r + 1], rev)
                new.append(s_new)
                outs.append(o)
        return tuple(new), (jnp.concatenate(outs, axis=1),)
    return f


def _hbm_spec():
    return pl.BlockSpec(memory_space=pltpu.HBM)


def all_gather(name, xs):
    nt = len(xs)

    def body(*refs):
        x_refs, out_refs = refs[:nt], refs[nt:2 * nt]
        send_sems, recv_sems, local_sems = refs[2 * nt:]
        x, y, c = lax.axis_index("x"), lax.axis_index("y"), lax.axis_index("c")
        me, sibling = (x, y, c), (x, y, 1 - c)
        chips = [(1 - x, y), (x, 1 - y), (1 - x, 1 - y)]

        def slab(t, px, py, pc):
            return out_refs[t].at[4 * px + 2 * py + pc]

        def copy(t, k, block, to, src=None):
            return pltpu.make_async_remote_copy(
                src_ref=slab(t, *block) if src is None else src, dst_ref=slab(t, *block),
                send_sem=send_sems.at[7 * t + k], recv_sem=recv_sems.at[7 * t + k], device_id=to, device_id_type=MESH)

        mine, first, passed = [], [], []
        for t in range(nt):
            mine.append(pltpu.make_async_copy(x_refs[t], slab(t, *me), local_sems.at[t]))
            mine[-1].start()
            cps = [copy(t, 0, me, sibling, src=x_refs[t])]
            cps += [copy(t, 1 + j, me, (*chip, c), src=x_refs[t]) for j, chip in enumerate(chips)]
            for cp in cps:
                cp.start()
            first += cps
        for j, chip in enumerate(chips):
            for t in range(nt):
                copy(t, 1 + j, (*chip, c), me).wait_recv()
                fw = copy(t, 4 + j, (*chip, c), sibling)
                fw.start()
                passed.append(fw)
        for t in range(nt):
            copy(t, 0, sibling, me).wait_recv()
            for j, chip in enumerate(chips):
                copy(t, 4 + j, (*chip, 1 - c), me).wait_recv()
        for cp in first + passed:
            cp.wait_send()
        for cp in mine:
            cp.wait()

    return pl.pallas_call(
        body, name=name, out_shape=[_sds((N_DEV,) + a.shape, a.dtype) for a in xs],
        in_specs=[_hbm_spec()] * nt, out_specs=[_hbm_spec()] * nt,
        scratch_shapes=[pltpu.SemaphoreType.DMA((7 * nt,)), pltpu.SemaphoreType.DMA((7 * nt,)),
                        pltpu.SemaphoreType.DMA((nt,))],
    )(*xs)


def sibling_exchange(name, gs):
    nt = len(gs)

    def body(*refs):
        g_refs, l_refs, send_sems, recv_sems = refs[:nt], refs[nt:2 * nt], refs[2 * nt], refs[2 * nt + 1]
        x, y, c = lax.axis_index("x"), lax.axis_index("y"), lax.axis_index("c")
        cps = []
        for t in range(nt):
            for q in range(4):
                cps.append(pltpu.make_async_remote_copy(
                    src_ref=g_refs[t].at[2 * q + (1 - c)], dst_ref=l_refs[t].at[q], send_sem=send_sems.at[4 * t + q],
                    recv_sem=recv_sems.at[4 * t + q], device_id=(x, y, 1 - c), device_id_type=MESH))
        for cp in cps:
            cp.start()
        for cp in cps:
            cp.wait()

    return pl.pallas_call(
        body, name=name, out_shape=[_sds((4,) + g.shape[1:], g.dtype) for g in gs],
        in_specs=[_hbm_spec()] * nt, out_specs=[_hbm_spec()] * nt,
        scratch_shapes=[pltpu.SemaphoreType.DMA((4 * nt,)), pltpu.SemaphoreType.DMA((4 * nt,))],
    )(*gs)


def chip_exchange(name, hs):
    nt = len(hs)

    def body(*refs):
        h_refs, r_refs = refs[:nt], refs[nt:2 * nt]
        send_sems, recv_sems, local_sems = refs[2 * nt:]
        x, y, c = lax.axis_index("x"), lax.axis_index("y"), lax.axis_index("c")
        my = 2 * x + y
        chips = [(1 - x, y), (x, 1 - y), (1 - x, 1 - y)]

        def copy(t, k, src_slot, dst_slot, to):
            return pltpu.make_async_remote_copy(
                src_ref=h_refs[t].at[src_slot], dst_ref=r_refs[t].at[dst_slot], send_sem=send_sems.at[3 * t + k],
                recv_sem=recv_sems.at[3 * t + k], device_id=(*to, c), device_id_type=MESH)

        mine, sends = [], []
        for t in range(nt):
            mine.append(pltpu.make_async_copy(h_refs[t].at[my], r_refs[t].at[my], local_sems.at[t]))
            mine[-1].start()
            for k, (qx, qy) in enumerate(chips):
                sends.append(copy(t, k, 2 * qx + qy, my, (qx, qy)))
                sends[-1].start()
        for t in range(nt):
            for k, (qx, qy) in enumerate(chips):
                copy(t, k, my, 2 * qx + qy, (qx, qy)).wait_recv()
        for cp in sends:
            cp.wait_send()
        for cp in mine:
            cp.wait()

    return pl.pallas_call(
        body, name=name, out_shape=[_sds(h.shape, h.dtype) for h in hs],
        in_specs=[_hbm_spec()] * nt, out_specs=[_hbm_spec()] * nt,
        scratch_shapes=[pltpu.SemaphoreType.DMA((3 * nt,)), pltpu.SemaphoreType.DMA((3 * nt,)),
                        pltpu.SemaphoreType.DMA((nt,))],
    )(*hs)


def _pack(arrs, dtype, width, row_mult, lead=0):
    ld = arrs[0].shape[:lead]
    flat = jnp.concatenate([a.reshape(ld + (-1,)).astype(dtype) for a in arrs], axis=-1)
    n = flat.shape[-1]
    q = width * row_mult
    npad = -(-n // q) * q
    flat = jnp.pad(flat, [(0, 0)] * lead + [(0, npad - n)])
    return flat.reshape(ld + (npad // width, width))


def _unpack(buf, shapes, lead=0):
    ld = buf.shape[:lead]
    flat = buf.reshape(ld + (-1,))
    out, off = [], 0
    for s in shapes:
        n = int(np.prod(s))
        out.append(flat[..., off:off + n].reshape(ld + tuple(s)))
        off += n
    return out


def _pool_mats(seg_len):
    mats = np.zeros((len(POOL_WINDOWS), ROW_TILE, ROW_TILE), np.float32)
    for gi, win in enumerate(POOL_WINDOWS):
        for p in range(ROW_TILE):
            base = (p // seg_len) * seg_len
            q = p - base
            lo = min(max(q - win // 2, 0), seg_len - 1)
            hi = min(max(q + win - 1 - win // 2, 0), seg_len - 1)
            mats[gi, p, base + lo:base + hi + 1] = 1.0 / (hi - lo + 1)
    return mats


def kernel(x, c, ctx, c_ctx, w_ada, b_ada, norm_w, ev_w_in, ev_lb, ev_a_norm, ev_pool_w, ev_pool_scale, ev_w_out, od_w_in, od_conv, od_A_log, od_dt_bias, od_norm, od_w_out, ffn_w13, ffn_w2, loss_target, m_c_ctx, m_w_ada, m_b_ada, m_norm_w, m_ev_w_in, m_ev_lb, m_ev_a_norm, m_ev_pool_w, m_ev_pool_scale, m_ev_w_out, m_od_w_in, m_od_conv, m_od_A_log, m_od_dt_bias, m_od_norm, m_od_w_out, m_ffn_w13, m_ffn_w2, v_c_ctx, v_w_ada, v_b_ada, v_norm_w, v_ev_w_in, v_ev_lb, v_ev_a_norm, v_ev_pool_w, v_ev_pool_scale, v_ev_w_out, v_od_w_in, v_od_conv, v_od_A_log, v_od_dt_bias, v_od_norm, v_od_w_out, v_ffn_w13, v_ffn_w2):
    names = ["c_ctx", "w_ada", "b_ada", "norm_w", "ev_w_in", "ev_lb", "ev_a_norm", "ev_pool_w", "ev_pool_scale",
             "ev_w_out", "od_w_in", "od_conv", "od_A_log", "od_dt_bias", "od_norm", "od_w_out", "ffn_w13", "ffn_w2"]
    wts = dict(zip(names, [c_ctx, w_ada, b_ada, norm_w, ev_w_in, ev_lb, ev_a_norm, ev_pool_w, ev_pool_scale,
                           ev_w_out, od_w_in, od_conv, od_A_log, od_dt_bias, od_norm, od_w_out, ffn_w13, ffn_w2]))
    mom1 = dict(zip(names, [m_c_ctx, m_w_ada, m_b_ada, m_norm_w, m_ev_w_in, m_ev_lb, m_ev_a_norm, m_ev_pool_w,
                            m_ev_pool_scale, m_ev_w_out, m_od_w_in, m_od_conv, m_od_A_log, m_od_dt_bias, m_od_norm,
                            m_od_w_out, m_ffn_w13, m_ffn_w2]))
    mom2 = dict(zip(names, [v_c_ctx, v_w_ada, v_b_ada, v_norm_w, v_ev_w_in, v_ev_lb, v_ev_a_norm, v_ev_pool_w,
                            v_ev_pool_scale, v_ev_w_out, v_od_w_in, v_od_conv, v_od_A_log, v_od_dt_bias, v_od_norm,
                            v_od_w_out, v_ffn_w13, v_ffn_w2]))

    ax, ay, ac = lax.axis_index("x"), lax.axis_index("y"), lax.axis_index("c")
    me = 4 * ax + 2 * ay + ac
    my_chip = 2 * ax + ay

    seq, d = x.shape[1], x.shape[2]
    n_ctx = ctx.shape[1]
    t = n_ctx + seq
    nt = t // ROW_TILE
    nct = n_ctx // ROW_TILE
    assert n_ctx == ROW_TILE and seq % ROW_TILE == 0 and ROW_TILE % GRID_W == 0
    depth = w_ada.shape[0]
    aw = d // 2
    n_ah = aw // HEAD
    n_grp = len(POOL_WINDOWS)
    dg = aw // n_grp
    assert dg % LANES == 0
    n_kh = d // HEAD
    kw, vw = n_kh * HEAD, 2 * n_kh * HEAD
    n_gate = 8 * n_kh
    ffn_h = ffn_w2.shape[1] * N_DEV
    ada_loc = w_ada.shape[2]
    assert depth == 2

    small_shapes = [(d,), norm_w.shape, ev_lb.shape, ev_pool_w.shape[1:], od_conv.shape[1:]]
    (g1,) = all_gather("ag_small", [_pack([c[0], norm_w, ev_lb, ev_pool_w[0], od_conv[0]], F32, LANES, SUBLANES)])
    c_all, nw_g, lb_g, pw_g, cv_g = _unpack(g1, small_shapes, lead=1)
    nw_full = nw_g.transpose(1, 2, 0, 3).reshape(depth, 4, d)
    lb_full = lb_g.transpose(1, 2, 0, 3).reshape(2, depth + 1, aw)
    pw_full = pw_g.transpose(1, 0, 2, 3).reshape(n_grp, dg, dg)
    cv_full = cv_g.transpose(1, 0, 2).reshape(C_CONV, 2 * kw + vw)

    big = [ev_w_in[0], ev_w_out[0], od_w_in[0], od_w_out[0], ffn_w13[0], ffn_w13[1], ffn_w2[0], ffn_w2[1]]
    g_ev_in, g_ev_out, g_od_in, g_od_out, g_w13a, g_w13b, g_w2a, g_w2b = all_gather(
        "ag_weights", [w_.astype(BF16) for w_ in big])

    def cols_natural(g):
        return g.transpose(1, 0, 2).reshape(g.shape[1], N_DEV * g.shape[2])

    def rows_natural(g):
        return g.reshape(N_DEV * g.shape[1], g.shape[2])

    def col_weight(g):
        return (g, True) if g.shape[2] % LANES == 0 else (cols_natural(g), False)

    w_ev_in = col_weight(g_ev_in)
    w_ev_out = rows_natural(g_ev_out)
    w_od_in = cols_natural(g_od_in)
    w_od_main, w_od_gate = w_od_in[:, :2 * kw + 2 * vw], w_od_in[:, 2 * kw + 2 * vw:]
    w_od_out = rows_natural(g_od_out)
    w13 = [col_weight(g_w13a), col_weight(g_w13b)]
    w2 = [rows_natural(g_w2a), rows_natural(g_w2b)]

    b_loc = lax.dynamic_slice_in_dim(b_ada, me * ada_loc, ada_loc, axis=1).reshape(depth, 1, ada_loc)
    ada_cb = _pick(ada_loc, TN_PREFS)
    ada_grid = (depth, ada_loc // ada_cb)
    ada_args = [
        TArg(c_all, (N_DEV, d), lambda l, j: (0, 0), "const"),
        TArg(c_ctx.reshape(1, d), (1, d), lambda l, j: (0, 0), "par", (0, 1)),
        TArg(w_ada, (None, d, ada_cb), lambda l, j: (l, 0, j)),
        TArg(b_loc, (None, 1, ada_cb), lambda l, j: (l, 0, j)),
    ]
    (m_loc,) = tile_fwd("ada_fwd", f_ada, ada_grid, ada_args,
                        [((depth, 2 * N_DEV, ada_loc), F32, (None, 2 * N_DEV, ada_cb), lambda l, j: (l, 0, j))])
    (m_all,) = all_gather("ag_mod", [m_loc])
    mods = []
    for layer in range(depth):
        lat = lax.dynamic_index_in_dim(m_all[:, layer], me, axis=1, keepdims=False).reshape(6, d)
        cxt = lax.dynamic_index_in_dim(m_all[:, layer], N_DEV + me, axis=1, keepdims=False).reshape(6, d)
        mods.append(jnp.stack([cxt, lat]))

    lb_slots = [TArg(lb_full[:, j], (2, aw), lambda i: (0, 0)) for j in range(depth + 1)]
    (lb0,) = tile_fwd("lb_fwd", f_lb(0), (1,), lb_slots, [((2, aw), F32, (2, aw), lambda i: (0, 0))])
    lb0r = lb0.reshape(2, n_ah, 1, HEAD)

    full_row = lambda i: (i, 0)
    par0 = lambda i: (0, 0)

    def nm_args(xs, layer, slot):
        return [TArg(xs, (ROW_TILE, d), full_row),
                TArg(nw_full[layer, slot].reshape(1, d), (1, d), par0, "par", (0,)),
                TArg(mods[layer], (2, 6, d), lambda i: (0, 0, 0), "par", (0,))]

    def norm_mod(name, xs, layer, slot, si, ci):
        (h,) = tile_fwd(name, f_norm_mod(si, ci, nct), (nt,), nm_args(xs, layer, slot),
                        [((t, d), BF16, (ROW_TILE, d), full_row)])
        return h

    def norm_mod_bwd(name, xs, layer, slot, si, ci, dh, carry):
        return tile_bwd(name, f_norm_mod(si, ci, nct, True), (nt,), nm_args(xs, layer, slot),
                        [(dh, (ROW_TILE, d), full_row), (carry, (ROW_TILE, d), full_row)])

    def gr_args(xs, ys, layer, slot):
        return [TArg(xs, (ROW_TILE, d), full_row, grad=False), TArg(ys, (ROW_TILE, d), full_row, gdtype=BF16),
                TArg(nw_full[layer, slot].reshape(1, d), (1, d), par0, "par", (0,)),
                TArg(mods[layer], (2, 6, d), lambda i: (0, 0, 0), "par", (0,))]

    def gate_res(name, xs, ys, layer, slot, gi):
        (o,) = tile_fwd(name, f_gate_res(gi, nct), (nt,), gr_args(xs, ys, layer, slot),
                        [((t, d), F32, (ROW_TILE, d), full_row)])
        return o

    def gate_res_bwd(name, xs, ys, layer, slot, gi, dx):
        return tile_bwd(name, f_gate_res(gi, nct), (nt,), gr_args(xs, ys, layer, slot),
                        [(dx, (ROW_TILE, d), full_row)])

    sw_rows = ROW_TILE // 2

    def sw_args(gu):
        return [TArg(gu, (sw_rows, 2 * ffn_h), full_row, gdtype=BF16)]

    def ffn_fwd(tag, xs, layer):
        h2 = norm_mod(f"nm2_{tag}", xs, layer, 2, 3, 4)
        gu = matmul(f"w13_{tag}", h2, w13[layer][0], "nn", slabs=w13[layer][1])
        (act,) = tile_fwd(f"swiglu_{tag}", f_swiglu, (t // sw_rows,), sw_args(gu),
                          [((t, ffn_h), BF16, (sw_rows, ffn_h), full_row)])
        fo = matmul(f"w2_{tag}", act, w2[layer], "nn")
        xn = gate_res(f"gr2_{tag}", xs, fo, layer, 3, 5)
        return xn, (xs, h2, gu, act, fo)

    def col_grad(name, a, dy, slabs):
        g = matmul(name, a, dy, "tn", slabs=slabs)
        return g if slabs else g.reshape(g.shape[0], N_DEV, g.shape[1] // N_DEV).transpose(1, 0, 2)

    def row_grad(name, a, dy):
        g = matmul(name, a, dy, "tn")
        return g.reshape(N_DEV, g.shape[0] // N_DEV, g.shape[1])

    def ffn_bwd(tag, saved, layer, dxn, acc):
        xs, h2, gu, act, fo = saved
        dfo, dnw3, dmod_a = gate_res_bwd(f"gr2b_{tag}", xs, fo, layer, 3, 5, dxn)
        dact = matmul(f"w2d_{tag}", dfo, w2[layer], "nt")
        dw2 = row_grad(f"w2w_{tag}", act, dfo)
        (dgu,) = tile_bwd(f"swiglub_{tag}", f_swiglu, (t // sw_rows,), sw_args(gu), [(dact, (sw_rows, ffn_h), full_row)])
        dh2 = matmul(f"w13d_{tag}", dgu, w13[layer][0], "nt", slabs=w13[layer][1])
        dw13 = col_grad(f"w13w_{tag}", h2, dgu, w13[layer][1])
        dxs, dnw2, dmod_b = norm_mod_bwd(f"nm2b_{tag}", xs, layer, 2, 3, 4, dh2, dxn)
        acc["ffn_w13"][layer] = dw13
        acc["ffn_w2"][layer] = dw2
        acc["norm_w"][layer][2] = dnw2
        acc["norm_w"][layer][3] = dnw3
        acc["mods"][layer].extend([dmod_a, dmod_b])
        return dxs

    def head_cols(width):
        return (ROW_TILE, width)

    n_a = t // A_CHUNK
    nca = n_ctx // A_CHUNK

    def a_tok(rev):
        if not rev:
            return lambda i: i
        return lambda i: jnp.where(i < nca, nca - 1 - i, n_a + nca - 1 - i)

    hb = _pick(n_ah, (HEADS_PER_STEP, 2, 1))
    n_hblk = n_ah // hb

    def hg_args(p, direction):
        tok = a_tok(direction == 1)
        blk = (A_CHUNK, hb * HEAD)
        return [TArg(p, blk, lambda h, i: (tok(i), h)),
                TArg(p, blk, lambda h, i: (tok(i), (1 + direction) * n_hblk + h)),
                TArg(p, blk, lambda h, i: (tok(i), 3 * n_hblk + h)),
                TArg(lb0r, (None, hb, 1, HEAD), lambda h, i: (direction, h, 0, 0), "par", (1,))]

    pmats = jnp.asarray(np.stack([_pool_mats(n_ctx), _pool_mats(GRID_W)]))

    def pool_args(p):
        return [TArg(p, (ROW_TILE, dg), lambda g, i: (i, 5 * n_grp + g)),
                TArg(pmats, (None, None, ROW_TILE, ROW_TILE), lambda g, i: (jnp.where(i < nct, 0, 1), g, 0, 0), "const"),
                TArg(pw_full, (None, dg, dg), lambda g, i: (g, 0, 0), "par", (1,)),
                TArg(ev_pool_scale, (1, dg), lambda g, i: (0, g), "par", (1,))]

    def ro_args(o_f, o_b, gate_arr, gate_off, nw_arr, n_heads):
        blk = (ROW_TILE, HEAD)
        return [TArg(o_f, blk, lambda h, i: (i, h)), TArg(o_b, blk, lambda h, i: (i, h), grad=False),
                TArg(gate_arr, blk, lambda h, i: (i, gate_off + h)),
                TArg(nw_arr, (1, HEAD), lambda h, i: (0, 0), "par", (0, 1))]

    def even_fwd(tag, xs, layer):
        h = norm_mod(f"nm1_{tag}", xs, layer, 0, 0, 1)
        p = matmul(f"win_{tag}", h, w_ev_in[0], "nn", slabs=w_ev_in[1])
        outs, saves = [], []
        for direction in (0, 1):
            (o,), sv = scan_fwd(f"hgrn_{tag}_{direction}", f_hgrn2(direction == 1, hb), n_hblk, n_a, hg_args(p, direction),
                                [((t, aw), F32, (A_CHUNK, hb * HEAD), lambda hh, i, tok=a_tok(direction == 1): (tok(i), hh))], hb)
            outs.append(o)
            saves.append(sv)
        (a_out,) = tile_fwd(f"ro_{tag}", f_readout, (n_ah, nt), ro_args(outs[0], outs[1], p, 4 * n_ah, ev_a_norm, n_ah),
                            [((t, aw), BF16, (ROW_TILE, HEAD), lambda hh, i: (i, hh))])
        (pooled,) = tile_fwd(f"pool_{tag}", f_pool, (n_grp, nt), pool_args(p),
                             [((t, aw), BF16, (ROW_TILE, dg), lambda g, i: (i, g))])
        cat = assemble(f"cat_{tag}", [[(a_out, aw, 0)], [(pooled, aw, 0)]], BF16)
        y = matmul(f"wout_{tag}", cat, w_ev_out, "nn")
        xn = gate_res(f"gr1_{tag}", xs, y, layer, 1, 2)
        return xn, (xs, h, p, outs, saves, cat, y)

    def even_bwd(tag, saved, layer, dxn, acc):
        xs, h, p, outs, saves, cat, y = saved
        dy, dnw1, dmod_a = gate_res_bwd(f"gr1b_{tag}", xs, y, layer, 1, 2, dxn)
        dcat = matmul(f"woutd_{tag}", dy, w_ev_out, "nt")
        acc["ev_w_out"] = row_grad(f"woutw_{tag}", cat, dy)
        do, dgate, d_anorm = tile_bwd(f"rob_{tag}", f_readout, (n_ah, nt),
                                      ro_args(outs[0], outs[1], p, 4 * n_ah, ev_a_norm, n_ah),
                                      [(dcat, (ROW_TILE, HEAD), lambda hh, i: (i, hh))])
        du, d_pw, d_ps = tile_bwd(f"poolb_{tag}", f_pool, (n_grp, nt), pool_args(p),
                                  [(dcat, (ROW_TILE, dg), lambda g, i: (i, n_grp + g))])
        dq, df, di, dlb = [], [], [], []
        for direction in (0, 1):
            r = scan_bwd(f"hgrnb_{tag}_{direction}", f_hgrn2(direction == 1, hb), n_hblk, n_a, hg_args(p, direction),
                         saves[direction],
                         [(do, (A_CHUNK, hb * HEAD), lambda hh, i, tok=a_tok(direction == 1): (tok(i), hh))])
            dq.append(r[0])
            df.append(r[1])
            di.append(r[2])
            dlb.append(r[3])
        sec = lambda arr, s: (arr, aw, s)
        dp = assemble(f"dp_{tag}", [[sec(dq[0], 0), sec(dq[1], 0)], [sec(df[0], 1)], [sec(df[1], 2)],
                                    [sec(di[0], 3), sec(di[1], 3)], [sec(dgate, 4)], [sec(du, 5)]], BF16)
        dh = matmul(f"wind_{tag}", dp, w_ev_in[0], "nt", slabs=w_ev_in[1])
        acc["ev_w_in"] = col_grad(f"winw_{tag}", h, dp, w_ev_in[1])
        dxs, dnw0, dmod_b = norm_mod_bwd(f"nm1b_{tag}", xs, layer, 0, 0, 1, dh, dxn)
        acc["norm_w"][layer][0] = dnw0
        acc["norm_w"][layer][1] = dnw1
        acc["mods"][layer].extend([dmod_a, dmod_b])
        acc["ev_a_norm"] = d_anorm
        acc["ev_pool_w"] = d_pw
        acc["ev_pool_scale"] = d_ps
        acc["lb0"] = jnp.stack([dlb[0][0], dlb[1][1]]).reshape(2, aw)
        return dxs

    n_c = t // C_CHUNK
    ncc = n_ctx // C_CHUNK

    def c_tok(rev):
        if not rev:
            return lambda i: i
        return lambda i: jnp.where(i < ncc, ncc - 1 - i, n_c + ncc - 1 - i)

    alog = od_A_log[0].reshape(2, n_kh, 2, 1)
    dtb = od_dt_bias[0].reshape(2, n_kh, 2, 1)

    khb = _pick(n_kh, (HEADS_PER_STEP, 2, 1))
    n_kblk = n_kh // khb

    def gd_args(z, gates, direction):
        tok = c_tok(direction == 1)
        gblk = (None, khb, None, 2, C_CHUNK)
        sblk = (None, khb, 2, 1)
        return [TArg(z, (C_CHUNK, khb * HEAD), lambda kb, i: (tok(i), kb)),
                TArg(z, (C_CHUNK, khb * HEAD), lambda kb, i: (tok(i), n_kblk + kb)),
                TArg(z, (C_CHUNK, khb * 2 * HEAD), lambda kb, i: (tok(i), n_kblk + kb)),
                TArg(gates, gblk, lambda kb, i: (direction, kb, tok(i), 0, 0)),
                TArg(gates, gblk, lambda kb, i: (2 + direction, kb, tok(i), 0, 0)),
                TArg(alog, sblk, lambda kb, i: (direction, kb, 0, 0), "par", (1,)),
                TArg(dtb, sblk, lambda kb, i: (direction, kb, 0, 0), "par", (1,))]

    def odd_fwd(tag, xs, layer):
        h = norm_mod(f"nm1_{tag}", xs, layer, 0, 0, 1)
        pm = matmul(f"win_{tag}", h, w_od_main, "nn")
        pg = matmul(f"wgate_{tag}", h, w_od_gate, "nn")
        z = conv_fwd(f"conv_{tag}", pm, cv_full, 2 * kw + vw, nct)
        gates = pg.reshape(n_c, C_CHUNK, 4, n_kh, 2).transpose(2, 3, 0, 4, 1)
        outs, saves = [], []
        for direction in (0, 1):
            (o,), sv = scan_fwd(f"gdn_{tag}_{direction}", f_gdn(direction == 1, khb), n_kblk, n_c, gd_args(z, gates, direction),
                                [((t, vw), F32, (C_CHUNK, khb * 2 * HEAD), lambda kb, i, tok=c_tok(direction == 1): (tok(i), kb))],
                                2 * khb)
            outs.append(o)
            saves.append(sv)
        n_vh = 2 * n_kh
        (yo,) = tile_fwd(f"ro_{tag}", f_readout, (n_vh, nt), ro_args(outs[0], outs[1], pm, 2 * n_kh + n_vh, od_norm, n_vh),
                         [((t, vw), BF16, (ROW_TILE, HEAD), lambda hh, i: (i, hh))])
        y = matmul(f"wout_{tag}", yo, w_od_out, "nn")
        xn = gate_res(f"gr1_{tag}", xs, y, layer, 1, 2)
        return xn, (xs, h, pm, z, gates, outs, saves, yo, y)

    def odd_bwd(tag, saved, layer, dxn, acc):
        xs, h, pm, z, gates, outs, saves, yo, y = saved
        n_vh = 2 * n_kh
        dy, dnw1, dmod_a = gate_res_bwd(f"gr1b_{tag}", xs, y, layer, 1, 2, dxn)
        dyo = matmul(f"woutd_{tag}", dy, w_od_out, "nt")
        acc["od_w_out"] = row_grad(f"woutw_{tag}", yo, dy)
        do, dzg, d_onorm = tile_bwd(f"rob_{tag}", f_readout, (n_vh, nt),
                                    ro_args(outs[0], outs[1], pm, 2 * n_kh + n_vh, od_norm, n_vh),
                                    [(dyo, (ROW_TILE, HEAD), lambda hh, i: (i, hh))])
        dq, dk, dv, dga, dgb, dal, ddt = [], [], [], [], [], [], []
        for direction in (0, 1):
            r = scan_bwd(f"gdnb_{tag}_{direction}", f_gdn(direction == 1, khb), n_kblk, n_c, gd_args(z, gates, direction),
                         saves[direction],
                         [(do, (C_CHUNK, khb * 2 * HEAD), lambda kb, i, tok=c_tok(direction == 1): (tok(i), kb))])
            for lst, v_ in zip((dq, dk, dv, dga, dgb, dal, ddt), r):
                lst.append(v_)
        dz = assemble(f"dz_{tag}", [[(dq[0], kw, 0), (dq[1], kw, 0)], [(dk[0], kw, 1), (dk[1], kw, 1)],
                                    [(dv[0], vw, 1), (dv[1], vw, 1)]], F32)
        du, d_conv = conv_bwd(f"convb_{tag}", pm, cv_full, dz, 2 * kw + vw, nct)
        dpm = assemble(f"dpm_{tag}", [[(du, 2 * kw + vw, 0)], [(dzg, vw, 2)]], BF16)
        dgates = jnp.stack([dga[0][0], dga[1][1], dgb[0][2], dgb[1][3]])
        dpg = dgates.transpose(2, 4, 0, 1, 3).reshape(t, n_gate).astype(BF16)
        dh = matmul(f"wgated_{tag}", dpg, w_od_gate, "nt")
        dh = matmul(f"wind_{tag}", dpm, w_od_main, "nt", add=dh)
        dw_in = jnp.concatenate([matmul(f"winw_{tag}", h, dpm, "tn"), matmul(f"wgatew_{tag}", h, dpg, "tn")], axis=1)
        acc["od_w_in"] = dw_in.reshape(d, N_DEV, dw_in.shape[1] // N_DEV).transpose(1, 0, 2)
        dxs, dnw0, dmod_b = norm_mod_bwd(f"nm1b_{tag}", xs, layer, 0, 0, 1, dh, dxn)
        acc["norm_w"][layer][0] = dnw0
        acc["norm_w"][layer][1] = dnw1
        acc["mods"][layer].extend([dmod_a, dmod_b])
        acc["od_norm"] = d_onorm
        acc["od_conv"] = d_conv
        acc["od_A_log"] = jnp.stack([dal[0][0], dal[1][1]]).reshape(1, 2, n_vh)
        acc["od_dt_bias"] = jnp.stack([ddt[0][0], ddt[1][1]]).reshape(1, 2, n_vh)
        return dxs

    xs0 = jnp.concatenate([ctx[0], x[0]], axis=0)
    xs1, sv_e = even_fwd("l0", xs0, 0)
    xs2, sv_f0 = ffn_fwd("l0", xs1, 0)
    xs3, sv_o = odd_fwd("l1", xs2, 1)
    xs4, sv_f1 = ffn_fwd("l1", xs3, 1)
    loss_loc, dxs = loss_kernel("loss", xs4, loss_target[0], nct)
    loss = lax.psum(loss_loc, ("x", "y", "c"))

    acc = {"norm_w": [[None] * 4 for _ in range(depth)], "mods": [[] for _ in range(depth)],
           "ffn_w13": [None] * depth, "ffn_w2": [None] * depth}
    dxs = ffn_bwd("l1", sv_f1, 1, dxs, acc)
    dxs = odd_bwd("l1", sv_o, 1, dxs, acc)
    dxs = ffn_bwd("l0", sv_f0, 0, dxs, acc)
    dxs = even_bwd("l0", sv_e, 0, dxs, acc)
    grad_x = dxs[n_ctx:].reshape(1, seq, d)

    (d_lb_slots) = tile_bwd("lb_bwd", f_lb(0), (1,), lb_slots, [(acc["lb0"], (2, aw), lambda i: (0, 0))])
    d_ev_lb = jnp.stack(d_lb_slots, axis=1)

    dmods = jnp.stack([functools.reduce(jnp.add, acc["mods"][layer]) for layer in range(depth)])
    (dm_all,) = all_gather("ag_dmod", [dmods.reshape(depth * 2 * 6, d)])
    dm_all = dm_all.reshape(N_DEV, depth, 2, 6 * d)
    dm_cols = lax.dynamic_slice_in_dim(dm_all, me * ada_loc, ada_loc, axis=3)
    dm_loc = jnp.concatenate([dm_cols[:, :, 1].transpose(1, 0, 2), dm_cols[:, :, 0].transpose(1, 0, 2)], axis=1)
    d_cctx_part, d_w_ada, d_b_loc = tile_bwd("ada_bwd", f_ada, ada_grid, ada_args,
                                             [(dm_loc, (None, 2 * N_DEV, ada_cb), lambda l, j: (l, 0, j))])

    d_b_full = lax.dynamic_update_slice_in_dim(jnp.zeros_like(b_ada), d_b_loc.reshape(depth, ada_loc), me * ada_loc, axis=1)
    d_nw = jnp.stack([jnp.stack([acc["norm_w"][layer][s].reshape(d) for s in range(4)]) for layer in range(depth)])
    small_grads = [d_cctx_part.reshape(d), d_b_full, d_nw, d_ev_lb, acc["ev_a_norm"], acc["ev_pool_w"],
                   acc["ev_pool_scale"], acc["od_conv"], acc["od_A_log"], acc["od_dt_bias"], acc["od_norm"]]
    sg_shapes = [a.shape for a in small_grads]
    (sg,) = all_gather("ag_smallgrads", [_pack(small_grads, F32, FLAT_W, SUBLANES)])
    sg_sum = sum_leading("sum_smallgrads", sg, F32)
    (g_cctx, g_bada, g_nw, g_lb, g_anorm, g_pw, g_ps, g_conv, g_alog, g_dtb, g_onorm) = _unpack(sg_sum, sg_shapes)

    def my_cols(full, axis):
        loc = full.shape[axis] // N_DEV
        return lax.dynamic_slice_in_dim(full, me * loc, loc, axis=axis)

    grads = {
        "c_ctx": g_cctx, "w_ada": d_w_ada, "b_ada": g_bada, "norm_w": my_cols(g_nw, 2), "ev_lb": my_cols(g_lb, 2),
        "ev_a_norm": g_anorm, "ev_pool_w": my_cols(g_pw, 1)[None], "ev_pool_scale": g_ps,
        "od_conv": my_cols(g_conv, 1)[None], "od_A_log": g_alog, "od_dt_bias": g_dtb, "od_norm": g_onorm,
    }

    tags = ["ev_in", "ev_out", "od_in", "od_out", "w13a", "w13b", "w2a", "w2b"]
    g8 = [acc["ev_w_in"], acc["ev_w_out"], acc["od_w_in"], acc["od_w_out"], acc["ffn_w13"][0], acc["ffn_w13"][1],
          acc["ffn_w2"][0], acc["ffn_w2"][1]]
    core = jnp.reshape(ac, (1,)).astype(jnp.int32)
    got = sibling_exchange("rs_sibling", g8)
    chip_sums = [add_own(f"rs_add_{tg}", g_, o_, core, BF16) for tg, g_, o_ in zip(tags, g8, got)]
    landed = chip_exchange("rs_chips", chip_sums)
    gb = [sum_leading(f"rs_sum_{tg}", l_, F32) for tg, l_ in zip(tags, landed)]
    grads["ev_w_in"], grads["ev_w_out"], grads["od_w_in"], grads["od_w_out"] = gb[0][None], gb[1][None], gb[2][None], gb[3][None]
    grads["ffn_w13"] = jnp.stack([gb[4], gb[5]])
    grads["ffn_w2"] = jnp.stack([gb[6], gb[7]])

    big_names = ["w_ada", "ev_w_in", "ev_w_out", "od_w_in", "od_w_out", "ffn_w13", "ffn_w2"]
    small_names = [n_ for n_ in names if n_ not in big_names]
    gl = {n_: grads[n_].reshape(wts[n_].shape) for n_ in names}
    delta, new_m, new_v = {}, {}, {}
    for n_ in big_names:
        shp = wts[n_].shape
        res = adamw(f"adamw_{n_}", _rows2d(gl[n_]), _rows2d(wts[n_]), _rows2d(mom1[n_]), _rows2d(mom2[n_]))
        delta[n_], new_m[n_], new_v[n_] = (r_.reshape(shp) for r_ in res)
    shapes = [wts[n_].shape for n_ in small_names]
    pk = lambda dct: _pack([dct[n_] for n_ in small_names], F32, FLAT_W, SUBLANES)
    res = adamw("adamw_small", pk(gl), pk(wts), pk(mom1), pk(mom2))
    for dct, r_ in zip((delta, new_m, new_v), res):
        for n_, a_ in zip(small_names, _unpack(r_, shapes)):
            dct[n_] = a_
    return (loss, grad_x, *[gl[n_] for n_ in names], *[delta[n_] for n_ in names], *[new_m[n_] for n_ in names],
            *[new_v[n_] for n_ in names])
```

```python
import functools
from typing import Any, NamedTuple

import numpy as np

import jax
import jax.numpy as jnp
from jax import lax
from jax.experimental import pallas as pl
from jax.experimental.pallas import tpu as pltpu

F32 = jnp.float32
BF16 = jnp.bfloat16
MESH = pl.DeviceIdType.MESH
N_DEV = 8

EPS = 1e-6
GRID_W = 64
HEAD = 128
A_CHUNK = 32
C_CHUNK = 64
C_CONV = 4
POOL_WINDOWS = (2, 4, 8, 16)
ADAM_LR, ADAM_B1, ADAM_B2, ADAM_EPS, ADAM_WD, ADAM_STEP = 0.001, 0.9, 0.999, 1e-08, 0.01, 10

VMEM_LIMIT_BYTES = 56 * 1024 * 1024
LANES = 128
SUBLANES = 8
ROW_TILE = 256
FLAT_W = 1024
FLAT_ROWS = 512
TM_PREFS = (1056, 768, 512, 256, 128, 64, 32, 16)
TN_PREFS = (512, 384, 1408, 256, 128)
TK_PREFS = (2048, 2816, 1408, 1024, 768, 512, 384, 256, 128)
TO_PREFS = (1024, 1408, 768, 704, 512, 384, 256, 128)
HEADS_PER_STEP = 4


def _pick(dim, prefs):
    for p in prefs:
        if p <= dim and dim % p == 0:
            return p
    return dim


def _cparams(ngrid):
    return pltpu.CompilerParams(dimension_semantics=("arbitrary",) * ngrid, vmem_limit_bytes=VMEM_LIMIT_BYTES)


def _sds(shape, dtype):
    return jax.ShapeDtypeStruct(tuple(shape), dtype)


def _dot(a, b, ca, cb, hi):
    dims = (((ca,), (cb,)), ((), ()))
    if hi:
        return lax.dot_general(a.astype(F32), b.astype(F32), dims, precision=lax.Precision.HIGHEST,
                               preferred_element_type=F32)
    return lax.dot_general(a.astype(BF16), b.astype(BF16), dims, preferred_element_type=F32)


@functools.partial(jax.custom_vjp, nondiff_argnums=(2, 3, 4))
def mm(a, b, ca=1, cb=0, hi=False):
    return _dot(a, b, ca, cb, hi)


def _mm_fwd(a, b, ca, cb, hi):
    return _dot(a, b, ca, cb, hi), (a, b)


def _mm_bwd(ca, cb, hi, res, g):
    a, b = res
    da = _dot(g, b, 1, 1 - cb, hi) if ca == 1 else _dot(b, g, 1 - cb, 1, hi)
    db = _dot(a, g, 1 - ca, 0, hi) if cb == 0 else _dot(g, a, 0, 1 - ca, hi)
    return da, db


mm.defvjp(_mm_fwd, _mm_bwd)


def _iota2(n, m, axis):
    return lax.broadcasted_iota(jnp.int32, (n, m), axis)


class TArg(NamedTuple):
    arr: Any
    block: tuple
    imap: Any
    kind: str = "row"
    acc: tuple = ()
    gdtype: Any = F32
    grad: bool = True


def _load(ref):
    v = ref[...]
    return v.astype(F32) if jnp.issubdtype(v.dtype, jnp.floating) else v


def tile_fwd(name, f, grid, args, outs):
    n_in, ng = len(args), len(grid)

    def body(*refs):
        pids = tuple(pl.program_id(k) for k in range(ng))
        res = f(pids, *[_load(r) for r in refs[:n_in]])
        for r, v in zip(refs[n_in:], res):
            r[...] = v.astype(r.dtype)

    return pl.pallas_call(
        body, grid=grid, name=name,
        in_specs=[pl.BlockSpec(a.block, a.imap) for a in args],
        out_specs=[pl.BlockSpec(b, im) for (_, _, b, im) in outs],
        out_shape=[_sds(s, d) for (s, d, _, _) in outs],
        compiler_params=_cparams(ng),
    )(*[a.arr for a in args])


def _store_grads(args, diff, pids, g_refs, d):
    for k, gr, dv in zip(diff, g_refs, d):
        a = args[k]
        if a.kind == "row" or not a.acc:
            gr[...] = dv.astype(gr.dtype)
        else:
            first = pids[a.acc[0]] == 0
            for ax in a.acc[1:]:
                first = jnp.logical_and(first, pids[ax] == 0)

            @pl.when(first)
            def _(gr=gr, dv=dv):
                gr[...] = dv.astype(gr.dtype)

            @pl.when(jnp.logical_not(first))
            def _(gr=gr, dv=dv):
                gr[...] += dv.astype(gr.dtype)


def tile_bwd(name, f, grid, args, cts):
    n_in, n_ct, ng = len(args), len(cts), len(grid)
    diff = [k for k, a in enumerate(args) if a.kind != "const" and a.grad]

    def body(*refs):
        pids = tuple(pl.program_id(k) for k in range(ng))
        vals = [_load(r) for r in refs[:n_in]]

        def g(*dv):
            full = list(vals)
            for k, v in zip(diff, dv):
                full[k] = v
            return tuple(f(pids, *full))

        _, vjp = jax.vjp(g, *[vals[k] for k in diff])
        d = vjp(tuple(_load(r) for r in refs[n_in:n_in + n_ct]))
        _store_grads(args, diff, pids, refs[n_in + n_ct:], d)

    return pl.pallas_call(
        body, grid=grid, name=name,
        in_specs=[pl.BlockSpec(a.block, a.imap) for a in args] + [pl.BlockSpec(b, im) for (_, b, im) in cts],
        out_specs=[pl.BlockSpec(args[k].block, args[k].imap) for k in diff],
        out_shape=[_sds(args[k].arr.shape, args[k].gdtype) for k in diff],
        compiler_params=_cparams(ng),
    )(*[a.arr for a in args], *[c[0] for c in cts])


def scan_fwd(name, f, n_heads, n_steps, args, outs, n_state):
    n_in, n_out = len(args), len(outs)
    sblock = (None, None, HEAD, HEAD)

    def body(*refs):
        in_refs = refs[:n_in]
        out_refs = refs[n_in:n_in + n_out]
        save_refs = refs[n_in + n_out:n_in + n_out + n_state]
        s_refs = refs[n_in + n_out + n_state:]

        @pl.when(pl.program_id(1) == 0)
        def _():
            for s in s_refs:
                s[...] = jnp.zeros_like(s)

        states = tuple(s[...] for s in s_refs)
        for sv, s in zip(save_refs, states):
            sv[...] = s
        new_states, res = f(states, *[_load(r) for r in in_refs])
        for s, v in zip(s_refs, new_states):
            s[...] = v
        for r, v in zip(out_refs, res):
            r[...] = v.astype(r.dtype)

    res = pl.pallas_call(
        body, grid=(n_heads, n_steps), name=name,
        in_specs=[pl.BlockSpec(a.block, a.imap) for a in args],
        out_specs=[pl.BlockSpec(b, im) for (_, _, b, im) in outs]
        + [pl.BlockSpec(sblock, lambda h, i: (h, i, 0, 0))] * n_state,
        out_shape=[_sds(s, d) for (s, d, _, _) in outs] + [_sds((n_heads, n_steps, HEAD, HEAD), F32)] * n_state,
        scratch_shapes=[pltpu.VMEM((HEAD, HEAD), F32)] * n_state,
        compiler_params=_cparams(2),
    )(*[a.arr for a in args])
    return res[:n_out], res[n_out:]


def scan_bwd(name, f, n_heads, n_steps, args, saves, cts):
    n_in, n_ct, n_state = len(args), len(cts), len(saves)
    diff = [k for k, a in enumerate(args) if a.kind != "const" and a.grad]
    sblock = (None, None, HEAD, HEAD)

    def rv(im):
        return lambda h, i: im(h, n_steps - 1 - i)

    def body(*refs):
        in_refs = refs[:n_in]
        save_refs = refs[n_in:n_in + n_state]
        ct_refs = refs[n_in + n_state:n_in + n_state + n_ct]
        g_refs = refs[n_in + n_state + n_ct:n_in + n_state + n_ct + len(diff)]
        ds_refs = refs[n_in + n_state + n_ct + len(diff):]
        pids = (pl.program_id(0), pl.program_id(1))

        @pl.when(pids[1] == 0)
        def _():
            for s in ds_refs:
                s[...] = jnp.zeros_like(s)

        vals = [_load(r) for r in in_refs]

        def g(states, *dv):
            full = list(vals)
            for k, v in zip(diff, dv):
                full[k] = v
            new_states, res = f(states, *full)
            return tuple(new_states), tuple(res)

        _, vjp = jax.vjp(g, tuple(s[...] for s in save_refs), *[vals[k] for k in diff])
        d = vjp((tuple(s[...] for s in ds_refs), tuple(_load(r) for r in ct_refs)))
        for s, v in zip(ds_refs, d[0]):
            s[...] = v
        _store_grads(args, diff, pids, g_refs, d[1:])

    return pl.pallas_call(
        body, grid=(n_heads, n_steps), name=name,
        in_specs=[pl.BlockSpec(a.block, rv(a.imap)) for a in args]
        + [pl.BlockSpec(sblock, rv(lambda h, i: (h, i, 0, 0)))] * n_state
        + [pl.BlockSpec(b, rv(im)) for (_, b, im) in cts],
        out_specs=[pl.BlockSpec(args[k].block, rv(args[k].imap)) for k in diff],
        out_shape=[_sds(args[k].arr.shape, args[k].gdtype) for k in diff],
        scratch_shapes=[pltpu.VMEM((HEAD, HEAD), F32)] * n_state,
        compiler_params=_cparams(2),
    )(*[a.arr for a in args], *saves, *[c[0] for c in cts])


def matmul(name, a, b, mode, add=None, out_dtype=F32, slabs=False):
    o_spec = None
    if mode == "nn":
        m, k = a.shape
        ns = b.shape[2] if slabs else b.shape[1]
        n = N_DEV * ns if slabs else ns
        to_m, to_n, tr = _pick(m, TM_PREFS), _pick(ns, TN_PREFS), _pick(k, TK_PREFS)
        nb = ns // to_n
        grid = (m // to_m, n // to_n, k // tr)
        a_spec = pl.BlockSpec((to_m, tr), lambda i, j, l: (i, l))
        if slabs:
            b_spec = pl.BlockSpec((None, tr, to_n), lambda i, j, l: (j // nb, l, j % nb))
        else:
            b_spec = pl.BlockSpec((tr, to_n), lambda i, j, l: (l, j))
        dims, oshape = (1, 0), (m, n)
    elif mode == "nt":
        m, n = a.shape
        k = b.shape[1] if slabs else b.shape[0]
        ns = n // N_DEV if slabs else n
        to_m, to_n, tr = _pick(m, TM_PREFS), _pick(k, TO_PREFS), _pick(ns, TK_PREFS)
        nb = ns // tr
        grid = (m // to_m, k // to_n, n // tr)
        a_spec = pl.BlockSpec((to_m, tr), lambda i, j, l: (i, l))
        if slabs:
            b_spec = pl.BlockSpec((None, to_n, tr), lambda i, j, l: (l // nb, j, l % nb))
        else:
            b_spec = pl.BlockSpec((to_n, tr), lambda i, j, l: (j, l))
        dims, oshape = (1, 1), (m, k)
    else:
        (t, k), n = a.shape, b.shape[1]
        ns = n // N_DEV if slabs else n
        to_m, to_n, tr = _pick(k, TO_PREFS), _pick(ns, TO_PREFS), _pick(t, TM_PREFS)
        nb = ns // to_n
        grid = (k // to_m, n // to_n, t // tr)
        a_spec = pl.BlockSpec((tr, to_m), lambda i, j, l: (l, i))
        b_spec = pl.BlockSpec((tr, to_n), lambda i, j, l: (l, j))
        dims, oshape = (0, 0), (k, n)
        if slabs:
            o_spec = pl.BlockSpec((None, to_m, to_n), lambda i, j, l: (j // nb, i, j % nb))
            oshape = (N_DEV, k, ns)
    n_red = grid[2]
    if o_spec is None:
        o_spec = pl.BlockSpec((to_m, to_n), lambda i, j, l: (i, j))
    has_add = add is not None

    def body(a_ref, b_ref, *rest):
        add_ref = rest[0] if has_add else None
        o_ref = rest[1] if has_add else rest[0]
        part = lax.dot_general(a_ref[...].astype(BF16), b_ref[...].astype(BF16),
                               (((dims[0],), (dims[1],)), ((), ())), preferred_element_type=F32)

        def finish(v):
            if has_add:
                v = v + add_ref[...]
            o_ref[...] = v.astype(o_ref.dtype)

        if n_red == 1:
            finish(part)
        else:
            acc = rest[-1]
            step = pl.program_id(2)

            @pl.when(step == 0)
            def _():
                acc[...] = part

            @pl.when(step > 0)
            def _():
                acc[...] += part

            @pl.when(step == n_red - 1)
            def _():
                finish(acc[...])

    return pl.pallas_call(
        body, grid=grid, name=name,
        in_specs=[a_spec, b_spec] + ([o_spec] if has_add else []),
        out_specs=o_spec, out_shape=_sds(oshape, out_dtype),
        scratch_shapes=[pltpu.VMEM((to_m, to_n), F32)] if n_red > 1 else [],
        compiler_params=_cparams(3),
    )(a, b, *([add] if has_add else []))


def assemble(name, pieces, out_dtype):
    flat = [s for piece in pieces for s in piece]
    t = flat[0][0].shape[0]
    widths = [piece[0][1] for piece in pieces]

    def body(*refs):
        o_ref, k, off = refs[-1], 0, 0
        for piece, w in zip(pieces, widths):
            v = refs[k][...].astype(F32)
            k += 1
            for _ in piece[1:]:
                v = v + refs[k][...].astype(F32)
                k += 1
            o_ref[:, off:off + w] = v.astype(o_ref.dtype)
            off += w

    tr = ROW_TILE // 2
    return pl.pallas_call(
        body, grid=(t // tr,), name=name,
        in_specs=[pl.BlockSpec((tr, w), functools.partial(lambda i, cb: (i, cb), cb=cb)) for (_, w, cb) in flat],
        out_specs=pl.BlockSpec((tr, sum(widths)), lambda i: (i, 0)),
        out_shape=_sds((t, sum(widths)), out_dtype),
        compiler_params=_cparams(1),
    )(*[s[0] for s in flat])


ELEM_ROWS = (128, 64, 32, 16, 8)


def _rows2d(a, lead=0):
    return a.reshape(a.shape[:lead] + (-1, a.shape[-1]))


def sum_leading(name, arr, out_dtype):
    k, rows, w = arr.shape
    tr = _pick(rows, ELEM_ROWS)

    def body(a_ref, o_ref):
        v = a_ref[0].astype(F32)
        for j in range(1, k):
            v = v + a_ref[j].astype(F32)
        o_ref[...] = v.astype(o_ref.dtype)

    return pl.pallas_call(
        body, grid=(rows // tr,), name=name,
        in_specs=[pl.BlockSpec((k, tr, w), lambda i: (0, i, 0))],
        out_specs=pl.BlockSpec((tr, w), lambda i: (i, 0)),
        out_shape=_sds((rows, w), out_dtype), compiler_params=_cparams(1),
    )(arr)


def add_own(name, g8, got, core, out_dtype):
    _, rows, w = g8.shape
    tr = _pick(rows, ELEM_ROWS)

    def body(core_ref, a_ref, b_ref, o_ref):
        o_ref[...] = (a_ref[...] + b_ref[...]).astype(o_ref.dtype)

    spec = pl.BlockSpec((None, tr, w), lambda q, i, core_ref: (q, i, 0))
    return pl.pallas_call(
        body, name=name,
        grid_spec=pltpu.PrefetchScalarGridSpec(
            num_scalar_prefetch=1, grid=(4, rows // tr),
            in_specs=[pl.BlockSpec((None, tr, w), lambda q, i, core_ref: (2 * q + core_ref[0], i, 0)), spec],
            out_specs=spec),
        out_shape=_sds(got.shape, out_dtype), compiler_params=_cparams(2),
    )(core, g8, got)


def adamw(name, g, w, m, v):
    rows, wd = g.shape
    tr = _pick(rows, ELEM_ROWS)

    def body(g_ref, w_ref, m_ref, v_ref, d_ref, nm_ref, nv_ref):
        gv = g_ref[...]
        mn = ADAM_B1 * m_ref[...] + (1.0 - ADAM_B1) * gv
        vn = ADAM_B2 * v_ref[...] + (1.0 - ADAM_B2) * jnp.square(gv)
        m_hat = mn / (1.0 - ADAM_B1 ** ADAM_STEP)
        v_hat = vn / (1.0 - ADAM_B2 ** ADAM_STEP)
        d_ref[...] = -ADAM_LR * (m_hat / (jnp.sqrt(v_hat) + ADAM_EPS) + ADAM_WD * w_ref[...])
        nm_ref[...] = mn
        nv_ref[...] = vn

    spec = pl.BlockSpec((tr, wd), lambda i: (i, 0))
    return pl.pallas_call(
        body, grid=(rows // tr,), name=name, in_specs=[spec] * 4, out_specs=[spec] * 3,
        out_shape=[_sds(g.shape, F32)] * 3, compiler_params=_cparams(1),
    )(g, w, m, v)


def loss_kernel(name, xs, target, n_ctx_tiles):
    t, d = xs.shape
    nt = t // ROW_TILE

    def body(x_ref, t_ref, dx_ref, l_ref):
        i = pl.program_id(0)
        is_lat = i >= n_ctx_tiles
        err = jnp.where(is_lat, x_ref[...] - t_ref[...], 0.0)
        dx_ref[...] = err / d
        part = 0.5 * jnp.sum(jnp.mean(jnp.square(err), axis=-1, keepdims=True), axis=0, keepdims=True)

        @pl.when(i == 0)
        def _():
            l_ref[...] = jnp.zeros_like(l_ref)

        l_ref[...] += jnp.broadcast_to(part, l_ref.shape)

    dx, l = pl.pallas_call(
        body, grid=(nt,), name=name,
        in_specs=[pl.BlockSpec((ROW_TILE, d), lambda i: (i, 0)),
                  pl.BlockSpec((ROW_TILE, d), lambda i: (jnp.maximum(i - n_ctx_tiles, 0), 0))],
        out_specs=[pl.BlockSpec((ROW_TILE, d), lambda i: (i, 0)), pl.BlockSpec((SUBLANES, LANES), lambda i: (0, 0))],
        out_shape=[_sds((t, d), F32), _sds((SUBLANES, LANES), F32)], compiler_params=_cparams(1),
    )(xs, target)
    return l[0, 0], dx


CONV_COLS = 512
CONV_LEFT = C_CONV // 2


def _conv_halo_specs(t, n_ctx_tiles):
    nt = t // ROW_TILE
    per = ROW_TILE // SUBLANES
    cur = pl.BlockSpec((ROW_TILE, CONV_COLS), lambda j, i: (i, j))
    prev = pl.BlockSpec((SUBLANES, CONV_COLS), lambda j, i: (jnp.maximum(i * per - 1, 0), j))
    nxt = pl.BlockSpec((SUBLANES, CONV_COLS), lambda j, i: (jnp.minimum((i + 1) * per, nt * per - 1), j))
    return cur, prev, nxt


def _fill_ext(ext, prev_ref, cur_ref, next_ref, i, nt, n_ctx_tiles):
    has_prev = jnp.logical_and(i != 0, i != n_ctx_tiles)
    has_next = jnp.logical_and(i != n_ctx_tiles - 1, i != nt - 1)
    ext[0:SUBLANES, :] = jnp.where(has_prev, prev_ref[...], 0.0)
    ext[SUBLANES:SUBLANES + ROW_TILE, :] = cur_ref[...]
    ext[SUBLANES + ROW_TILE:, :] = jnp.where(has_next, next_ref[...], 0.0)


def conv_fwd(name, p, w, width, n_ctx_tiles):
    t = p.shape[0]
    nt = t // ROW_TILE
    cur, prev, nxt = _conv_halo_specs(t, n_ctx_tiles)

    def body(c_ref, p_ref, n_ref, w_ref, o_ref, ext):
        _fill_ext(ext, p_ref, c_ref, n_ref, pl.program_id(1), nt, n_ctx_tiles)
        acc = None
        for j in range(C_CONV):
            term = ext[pl.ds(SUBLANES + j - CONV_LEFT, ROW_TILE), :] * w_ref[j:j + 1, :]
            acc = term if acc is None else acc + term
        o_ref[...] = acc

    return pl.pallas_call(
        body, grid=(width // CONV_COLS, nt), name=name,
        in_specs=[cur, prev, nxt, pl.BlockSpec((C_CONV, CONV_COLS), lambda j, i: (0, j))],
        out_specs=cur, out_shape=_sds((t, width), F32),
        scratch_shapes=[pltpu.VMEM((ROW_TILE + 2 * SUBLANES, CONV_COLS), F32)],
        compiler_params=_cparams(2),
    )(p, p, p, w)


def conv_bwd(name, p, w, dz, width, n_ctx_tiles):
    t = p.shape[0]
    nt = t // ROW_TILE
    cur, prev, nxt = _conv_halo_specs(t, n_ctx_tiles)

    def body(c_ref, p_ref, n_ref, dc_ref, dp_ref, dn_ref, w_ref, du_ref, dw_ref, ext, dext):
        i = pl.program_id(1)
        _fill_ext(ext, p_ref, c_ref, n_ref, i, nt, n_ctx_tiles)
        _fill_ext(dext, dp_ref, dc_ref, dn_ref, i, nt, n_ctx_tiles)
        dzc = dc_ref[...]
        @pl.when(i == 0)
        def _():
            dw_ref[...] = jnp.zeros_like(dw_ref)

        acc = None
        for j in range(C_CONV):
            term = dext[pl.ds(SUBLANES + CONV_LEFT - j, ROW_TILE), :] * w_ref[j:j + 1, :]
            acc = term if acc is None else acc + term
            dw_ref[j:j + 1, :] += jnp.sum(dzc * ext[pl.ds(SUBLANES + j - CONV_LEFT, ROW_TILE), :], axis=0, keepdims=True)
        du_ref[...] = acc

    wspec = pl.BlockSpec((C_CONV, CONV_COLS), lambda j, i: (0, j))
    return pl.pallas_call(
        body, grid=(width // CONV_COLS, nt), name=name,
        in_specs=[cur, prev, nxt, cur, prev, nxt, wspec],
        out_specs=[cur, wspec], out_shape=[_sds((t, width), F32), _sds((C_CONV, width), F32)],
        scratch_shapes=[pltpu.VMEM((ROW_TILE + 2 * SUBLANES, CONV_COLS), F32)] * 2,
        compiler_params=_cparams(2),
    )(p, p, p, dz, dz, dz, w)


def _rms(x, w):
    return x * lax.rsqrt(jnp.mean(x * x, axis=-1, keepdims=True) + EPS) * w


def _seg_mod(mods, is_ctx):
    return jnp.where(is_ctx, mods[0], mods[1])


def f_norm_mod(shift_i, scale_i, n_ctx_tiles, passthrough=False):
    def f(pids, x, nw, mods):
        m = _seg_mod(mods, pids[0] < n_ctx_tiles)
        h = _rms(x, nw) * (1.0 + m[scale_i:scale_i + 1]) + m[shift_i:shift_i + 1]
        return (h, x) if passthrough else (h,)
    return f


def f_gate_res(gate_i, n_ctx_tiles):
    def f(pids, x, y, nw, mods):
        m = _seg_mod(mods, pids[0] < n_ctx_tiles)
        return (x + m[gate_i:gate_i + 1] * _rms(y, nw),)
    return f


def f_swiglu(pids, gu):
    half = gu.shape[1] // 2
    return (jax.nn.silu(gu[:, :half]) * gu[:, half:],)


def f_readout(pids, o_a, o_b, gate, nw):
    return (_rms(o_a + o_b, nw) * jax.nn.silu(gate),)


def f_pool(pids, u, pmat, pw, scale):
    d = mm(pmat, u, 1, 0, True) - u
    return (mm(d, pw) * scale,)


def f_lb(layer):
    def f(pids, *slots):
        top = slots[0]
        for s in slots[1:]:
            top = jnp.maximum(top, s)
        ex = [jnp.exp(s - top) for s in slots]
        tot = ex[0]
        for e in ex[1:]:
            tot = tot + e
        part = ex[0]
        for e in ex[1:layer + 1]:
            part = part + e
        return (part / tot,)
    return f


def f_ada(pids, c_all, c_ctx, w, b):
    c16 = jnp.concatenate([c_all, jnp.broadcast_to(c_ctx, c_all.shape)], axis=0)
    return (mm(jax.nn.silu(c16), w) + b,)


def _each(fn, *lists):
    return [fn(*xs) for xs in zip(*lists)]


def _hgrn2_heads(sts, qrs, frs, irs, lbs, rev):
    c = A_CHUNK
    mid = c - c // 2 if rev else c // 2 - 1
    ri, ci = _iota2(c, c, 0), _iota2(c, c, 1)
    incl = (ri <= ci) if rev else (ri >= ci)
    incl_f = incl.astype(F32)
    qs = _each(jax.nn.silu, qrs)
    log_fs = _each(lambda lb, fr: jnp.log(lb + (1.0 - lb) * jax.nn.sigmoid(fr)), lbs, frs)
    ks = _each(lambda lb, fr: (1.0 - lb) * jax.nn.sigmoid(-fr), lbs, frs)
    bs = _each(lambda lf: mm(incl_f, lf, 1, 0, True), log_fs)
    b_lasts = _each(lambda lf: jnp.sum(lf, axis=0, keepdims=True), log_fs)
    scores = _each(lambda q, k, b: mm(q * jnp.exp(b - b[mid:mid + 1]), k * jnp.exp(b[mid:mid + 1] - b), 1, 1), qs, ks, bs)
    intra = _each(lambda sc, ir: mm(jnp.where(incl, sc, 0.0), ir), scores, irs)
    inter = _each(lambda q, b, st: mm(q * jnp.exp(b), st, 1, 1), qs, bs, sts)
    upd = _each(lambda ir, k, bl, b: mm(ir, k * jnp.exp(bl - b), 0, 0), irs, ks, b_lasts, bs)
    new = _each(lambda st, bl, u: st * jnp.exp(bl) + u, sts, b_lasts, upd)
    return new, _each(jnp.add, intra, inter)


def f_hgrn2(rev, hb):
    def f(states, qr, fr, ir, lb):
        cols = [slice(j * HEAD, (j + 1) * HEAD) for j in range(hb)]
        new, outs = _hgrn2_heads(list(states), [qr[:, cs] for cs in cols], [fr[:, cs] for cs in cols],
                                 [ir[:, cs] for cs in cols], [lb[j] for j in range(hb)], rev)
        return tuple(new), (jnp.concatenate(outs, axis=1) if hb > 1 else outs[0],)
    return f


def _neumann_inv(a_lows):
    n = a_lows[0].shape[0]
    eye = (_iota2(n, n, 0) == _iota2(n, n, 1)).astype(F32)
    ps = _each(lambda a: -a, a_lows)
    xs = _each(lambda p: eye + p, ps)
    k = 2
    while k < n:
        ps = _each(lambda p: mm(p, p, 1, 0, True), ps)
        xs = _each(lambda x, p: x + mm(x, p, 1, 0, True), xs, ps)
        k *= 2
    return tuple(xs)


@jax.custom_vjp
def unit_tri_inv(a_lows):
    return _neumann_inv(a_lows)


def _uti_fwd(a_lows):
    xs = _neumann_inv(a_lows)
    return xs, xs


def _uti_bwd(xs, gs):
    ts = _each(lambda x, g: mm(x, g, 0, 0, True), xs, gs)
    return (tuple(_each(lambda t, x: -mm(t, x, 1, 1, True), ts, xs)),)


unit_tri_inv.defvjp(_uti_fwd, _uti_bwd)


def _l2n(x):
    return x * lax.rsqrt(jnp.sum(x * x, axis=-1, keepdims=True) + EPS)


def _gdn_heads(ss, qs, ks, vs, a_rows, b_rows, alogs, dtbs, rev):
    c = qs[0].shape[0]
    ri, ci = _iota2(c, c, 0), _iota2(c, c, 1)
    causal = (ri <= ci) if rev else (ri >= ci)
    causal_t = (ri >= ci) if rev else (ri <= ci)
    strict = (ri < ci) if rev else (ri > ci)
    eye = ri == ci
    sq = lambda row: jnp.broadcast_to(row, (c, c))
    to_col = lambda row: jnp.sum(jnp.where(eye, sq(row), 0.0), axis=1, keepdims=True)
    g_rows = _each(lambda al, a, dt: -jnp.exp(al) * jax.nn.softplus(a + dt), alogs, a_rows, dtbs)
    beta_cols = _each(lambda b: to_col(jax.nn.sigmoid(b)), b_rows)
    g_cols = _each(to_col, g_rows)
    gc_cols = _each(lambda g: jnp.sum(jnp.where(causal, sq(g), 0.0), axis=1, keepdims=True), g_rows)
    gc_rows = _each(lambda g: jnp.sum(jnp.where(causal_t, sq(g), 0.0), axis=0, keepdims=True), g_cols)
    gc_lasts = _each(lambda g: jnp.sum(g, axis=1, keepdims=True), g_rows)
    decays = _each(lambda gc, gr: jnp.where(causal, jnp.exp(jnp.where(causal, gc - gr, 0.0)), 0.0), gc_cols, gc_rows)
    k_betas = _each(jnp.multiply, ks, beta_cols)
    v_betas = _each(jnp.multiply, vs, beta_cols)
    a_lows = _each(lambda kb, k, dec: jnp.where(strict, mm(kb, k, 1, 1) * dec, 0.0), k_betas, ks, decays)
    xs = unit_tri_inv(tuple(a_lows))
    egcs = _each(jnp.exp, gc_cols)
    us = _each(lambda x, vb: mm(x, vb, 1, 0, True), xs, v_betas)
    ws = _each(lambda x, kb, e: mm(x, kb * e, 1, 0, True), xs, k_betas, egcs)
    qks = _each(lambda q, k, dec: jnp.where(causal, mm(q, k, 1, 1) * dec, 0.0), qs, ks, decays)
    v_news = _each(lambda u, w, s: u - mm(w, s), us, ws, ss)
    o_states = _each(lambda q, e, s: mm(q * e, s), qs, egcs, ss)
    o_locals = _each(mm, qks, v_news)
    upds = _each(lambda k, gl, gc, vn: mm(k * jnp.exp(gl - gc), vn, 0, 0), ks, gc_lasts, gc_cols, v_news)
    new = _each(lambda s, gl, u: s * jnp.exp(gl) + u, ss, gc_lasts, upds)
    return new, _each(jnp.add, o_states, o_locals)


def f_gdn(rev, khb):
    def f(states, qr, kr, vr, a3, b3, alog3, dtb3):
        heads = [(j, r) for j in range(khb) for r in range(2)]
        cols = [slice(j * HEAD, (j + 1) * HEAD) for j in range(khb)]
        qk_ = _each(lambda cs: (_l2n(jax.nn.silu(qr[:, cs])) * (HEAD ** -0.5), _l2n(jax.nn.silu(kr[:, cs]))), cols)
        vs = [jax.nn.silu(vr[:, (2 * j + r) * HEAD:(2 * j + r + 1) * HEAD]) for j, r in heads]
        row = lambda arr3: [arr3[j][r:r + 1] for j, r in heads]
        new, outs = _gdn_heads(list(states), [qk_[j][0] for j, _ in heads], [qk_[j][1] for j, _ in heads], vs,
                               row(a3), row(b3), row(alog3), row(dtb3), rev)
        return tuple(new), (jnp.concatenate(outs, axis=1),)
    return f


def _hbm_spec():
    return pl.BlockSpec(memory_space=pltpu.HBM)


def all_gather(name, xs):
    nt = len(xs)

    def body(*refs):
        x_refs, out_refs = refs[:nt], refs[nt:2 * nt]
        send_sems, recv_sems, local_sems = refs[2 * nt:]
        x, y, c = lax.axis_index("x"), lax.axis_index("y"), lax.axis_index("c")
        me, sibling = (x, y, c), (x, y, 1 - c)
        chips = [(1 - x, y), (x, 1 - y), (1 - x, 1 - y)]

        def slab(t, px, py, pc):
            return out_refs[t].at[4 * px + 2 * py + pc]

        def copy(t, k, block, to, src=None):
            return pltpu.make_async_remote_copy(
                src_ref=slab(t, *block) if src is None else src, dst_ref=slab(t, *block),
                send_sem=send_sems.at[7 * t + k], recv_sem=recv_sems.at[7 * t + k], device_id=to, device_id_type=MESH)

        mine, first, passed = [], [], []
        for t in range(nt):
            mine.append(pltpu.make_async_copy(x_refs[t], slab(t, *me), local_sems.at[t]))
            mine[-1].start()
            cps = [copy(t, 0, me, sibling, src=x_refs[t])]
            cps += [copy(t, 1 + j, me, (*chip, c), src=x_refs[t]) for j, chip in enumerate(chips)]
            for cp in cps:
                cp.start()
            first += cps
        for j, chip in enumerate(chips):
            for t in range(nt):
                copy(t, 1 + j, (*chip, c), me).wait_recv()
                fw = copy(t, 4 + j, (*chip, c), sibling)
                fw.start()
                passed.append(fw)
        for t in range(nt):
            copy(t, 0, sibling, me).wait_recv()
            for j, chip in enumerate(chips):
                copy(t, 4 + j, (*chip, 1 - c), me).wait_recv()
        for cp in first + passed:
            cp.wait_send()
        for cp in mine:
            cp.wait()

    return pl.pallas_call(
        body, name=name, out_shape=[_sds((N_DEV,) + a.shape, a.dtype) for a in xs],
        in_specs=[_hbm_spec()] * nt, out_specs=[_hbm_spec()] * nt,
        scratch_shapes=[pltpu.SemaphoreType.DMA((7 * nt,)), pltpu.SemaphoreType.DMA((7 * nt,)),
                        pltpu.SemaphoreType.DMA((nt,))],
    )(*xs)


def sibling_exchange(name, gs):
    nt = len(gs)

    def body(*refs):
        g_refs, l_refs, send_sems, recv_sems = refs[:nt], refs[nt:2 * nt], refs[2 * nt], refs[2 * nt + 1]
        x, y, c = lax.axis_index("x"), lax.axis_index("y"), lax.axis_index("c")
        cps = []
        for t in range(nt):
            for q in range(4):
                cps.append(pltpu.make_async_remote_copy(
                    src_ref=g_refs[t].at[2 * q + (1 - c)], dst_ref=l_refs[t].at[q], send_sem=send_sems.at[4 * t + q],
                    recv_sem=recv_sems.at[4 * t + q], device_id=(x, y, 1 - c), device_id_type=MESH))
        for cp in cps:
            cp.start()
        for cp in cps:
            cp.wait()

    return pl.pallas_call(
        body, name=name, out_shape=[_sds((4,) + g.shape[1:], g.dtype) for g in gs],
        in_specs=[_hbm_spec()] * nt, out_specs=[_hbm_spec()] * nt,
        scratch_shapes=[pltpu.SemaphoreType.DMA((4 * nt,)), pltpu.SemaphoreType.DMA((4 * nt,))],
    )(*gs)


def chip_exchange(name, hs):
    nt = len(hs)

    def body(*refs):
        h_refs, r_refs = refs[:nt], refs[nt:2 * nt]
        send_sems, recv_sems, local_sems = refs[2 * nt:]
        x, y, c = lax.axis_index("x"), lax.axis_index("y"), lax.axis_index("c")
        my = 2 * x + y
        chips = [(1 - x, y), (x, 1 - y), (1 - x, 1 - y)]

        def copy(t, k, src_slot, dst_slot, to):
            return pltpu.make_async_remote_copy(
                src_ref=h_refs[t].at[src_slot], dst_ref=r_refs[t].at[dst_slot], send_sem=send_sems.at[3 * t + k],
                recv_sem=recv_sems.at[3 * t + k], device_id=(*to, c), device_id_type=MESH)

        mine, sends = [], []
        for t in range(nt):
            mine.append(pltpu.make_async_copy(h_refs[t].at[my], r_refs[t].at[my], local_sems.at[t]))
            mine[-1].start()
            for k, (qx, qy) in enumerate(chips):
                sends.append(copy(t, k, 2 * qx + qy, my, (qx, qy)))
                sends[-1].start()
        for t in range(nt):
            for k, (qx, qy) in enumerate(chips):
                copy(t, k, my, 2 * qx + qy, (qx, qy)).wait_recv()
        for cp in sends:
            cp.wait_send()
        for cp in mine:
            cp.wait()

    return pl.pallas_call(
        body, name=name, out_shape=[_sds(h.shape, h.dtype) for h in hs],
        in_specs=[_hbm_spec()] * nt, out_specs=[_hbm_spec()] * nt,
        scratch_shapes=[pltpu.SemaphoreType.DMA((3 * nt,)), pltpu.SemaphoreType.DMA((3 * nt,)),
                        pltpu.SemaphoreType.DMA((nt,))],
    )(*hs)


def _pack(arrs, dtype, width, row_mult, lead=0):
    ld = arrs[0].shape[:lead]
    flat = jnp.concatenate([a.reshape(ld + (-1,)).astype(dtype) for a in arrs], axis=-1)
    n = flat.shape[-1]
    q = width * row_mult
    npad = -(-n // q) * q
    flat = jnp.pad(flat, [(0, 0)] * lead + [(0, npad - n)])
    return flat.reshape(ld + (npad // width, width))


def _unpack(buf, shapes, lead=0):
    ld = buf.shape[:lead]
    flat = buf.reshape(ld + (-1,))
    out, off = [], 0
    for s in shapes:
        n = int(np.prod(s))
        out.append(flat[..., off:off + n].reshape(ld + tuple(s)))
        off += n
    return out


def _pool_mats(seg_len):
    mats = np.zeros((len(POOL_WINDOWS), ROW_TILE, ROW_TILE), np.float32)
    for gi, win in enumerate(POOL_WINDOWS):
        for p in range(ROW_TILE):
            base = (p // seg_len) * seg_len
            q = p - base
            lo = min(max(q - win // 2, 0), seg_len - 1)
            hi = min(max(q + win - 1 - win // 2, 0), seg_len - 1)
            mats[gi, p, base + lo:base + hi + 1] = 1.0 / (hi - lo + 1)
    return mats


def kernel(x, c, ctx, c_ctx, w_ada, b_ada, norm_w, ev_w_in, ev_lb, ev_a_norm, ev_pool_w, ev_pool_scale, ev_w_out, od_w_in, od_conv, od_A_log, od_dt_bias, od_norm, od_w_out, ffn_w13, ffn_w2, loss_target, m_c_ctx, m_w_ada, m_b_ada, m_norm_w, m_ev_w_in, m_ev_lb, m_ev_a_norm, m_ev_pool_w, m_ev_pool_scale, m_ev_w_out, m_od_w_in, m_od_conv, m_od_A_log, m_od_dt_bias, m_od_norm, m_od_w_out, m_ffn_w13, m_ffn_w2, v_c_ctx, v_w_ada, v_b_ada, v_norm_w, v_ev_w_in, v_ev_lb, v_ev_a_norm, v_ev_pool_w, v_ev_pool_scale, v_ev_w_out, v_od_w_in, v_od_conv, v_od_A_log, v_od_dt_bias, v_od_norm, v_od_w_out, v_ffn_w13, v_ffn_w2):
    names = ["c_ctx", "w_ada", "b_ada", "norm_w", "ev_w_in", "ev_lb", "ev_a_norm", "ev_pool_w", "ev_pool_scale",
             "ev_w_out", "od_w_in", "od_conv", "od_A_log", "od_dt_bias", "od_norm", "od_w_out", "ffn_w13", "ffn_w2"]
    wts = dict(zip(names, [c_ctx, w_ada, b_ada, norm_w, ev_w_in, ev_lb, ev_a_norm, ev_pool_w, ev_pool_scale,
                           ev_w_out, od_w_in, od_conv, od_A_log, od_dt_bias, od_norm, od_w_out, ffn_w13, ffn_w2]))
    mom1 = dict(zip(names, [m_c_ctx, m_w_ada, m_b_ada, m_norm_w, m_ev_w_in, m_ev_lb, m_ev_a_norm, m_ev_pool_w,
                            m_ev_pool_scale, m_ev_w_out, m_od_w_in, m_od_conv, m_od_A_log, m_od_dt_bias, m_od_norm,
                            m_od_w_out, m_ffn_w13, m_ffn_w2]))
    mom2 = dict(zip(names, [v_c_ctx, v_w_ada, v_b_ada, v_norm_w, v_ev_w_in, v_ev_lb, v_ev_a_norm, v_ev_pool_w,
                            v_ev_pool_scale, v_ev_w_out, v_od_w_in, v_od_conv, v_od_A_log, v_od_dt_bias, v_od_norm,
                            v_od_w_out, v_ffn_w13, v_ffn_w2]))

    ax, ay, ac = lax.axis_index("x"), lax.axis_index("y"), lax.axis_index("c")
    me = 4 * ax + 2 * ay + ac
    my_chip = 2 * ax + ay

    seq, d = x.shape[1], x.shape[2]
    n_ctx = ctx.shape[1]
    t = n_ctx + seq
    nt = t // ROW_TILE
    nct = n_ctx // ROW_TILE
    assert n_ctx == ROW_TILE and seq % ROW_TILE == 0 and ROW_TILE % GRID_W == 0
    depth = w_ada.shape[0]
    aw = d // 2
    n_ah = aw // HEAD
    n_grp = len(POOL_WINDOWS)
    dg = aw // n_grp
    assert dg % LANES == 0
    n_kh = d // HEAD
    kw, vw = n_kh * HEAD, 2 * n_kh * HEAD
    n_gate = 8 * n_kh
    ffn_h = ffn_w2.shape[1] * N_DEV
    ada_loc = w_ada.shape[2]
    assert depth == 2

    small_shapes = [(d,), norm_w.shape, ev_lb.shape, ev_pool_w.shape[1:], od_conv.shape[1:]]
    (g1,) = all_gather("ag_small", [_pack([c[0], norm_w, ev_lb, ev_pool_w[0], od_conv[0]], F32, LANES, SUBLANES)])
    c_all, nw_g, lb_g, pw_g, cv_g = _unpack(g1, small_shapes, lead=1)
    nw_full = nw_g.transpose(1, 2, 0, 3).reshape(depth, 4, d)
    lb_full = lb_g.transpose(1, 2, 0, 3).reshape(2, depth + 1, aw)
    pw_full = pw_g.transpose(1, 0, 2, 3).reshape(n_grp, dg, dg)
    cv_full = cv_g.transpose(1, 0, 2).reshape(C_CONV, 2 * kw + vw)

    big = [ev_w_in[0], ev_w_out[0], od_w_in[0], od_w_out[0], ffn_w13[0], ffn_w13[1], ffn_w2[0], ffn_w2[1]]
    g_ev_in, g_ev_out, g_od_in, g_od_out, g_w13a, g_w13b, g_w2a, g_w2b = all_gather(
        "ag_weights", [w_.astype(BF16) for w_ in big])

    def cols_natural(g):
        return g.transpose(1, 0, 2).reshape(g.shape[1], N_DEV * g.shape[2])

    def rows_natural(g):
        return g.reshape(N_DEV * g.shape[1], g.shape[2])

    def col_weight(g):
        return (g, True) if g.shape[2] % LANES == 0 else (cols_natural(g), False)

    w_ev_in = col_weight(g_ev_in)
    w_ev_out = rows_natural(g_ev_out)
    w_od_in = cols_natural(g_od_in)
    w_od_main, w_od_gate = w_od_in[:, :2 * kw + 2 * vw], w_od_in[:, 2 * kw + 2 * vw:]
    w_od_out = rows_natural(g_od_out)
    w13 = [col_weight(g_w13a), col_weight(g_w13b)]
    w2 = [rows_natural(g_w2a), rows_natural(g_w2b)]

    b_loc = lax.dynamic_slice_in_dim(b_ada, me * ada_loc, ada_loc, axis=1).reshape(depth, 1, ada_loc)
    ada_cb = _pick(ada_loc, TN_PREFS)
    ada_grid = (depth, ada_loc // ada_cb)
    ada_args = [
        TArg(c_all, (N_DEV, d), lambda l, j: (0, 0), "const"),
        TArg(c_ctx.reshape(1, d), (1, d), lambda l, j: (0, 0), "par", (0, 1)),
        TArg(w_ada, (None, d, ada_cb), lambda l, j: (l, 0, j)),
        TArg(b_loc, (None, 1, ada_cb), lambda l, j: (l, 0, j)),
    ]
    (m_loc,) = tile_fwd("ada_fwd", f_ada, ada_grid, ada_args,
                        [((depth, 2 * N_DEV, ada_loc), F32, (None, 2 * N_DEV, ada_cb), lambda l, j: (l, 0, j))])
    (m_all,) = all_gather("ag_mod", [m_loc])
    mods = []
    for layer in range(depth):
        lat = lax.dynamic_index_in_dim(m_all[:, layer], me, axis=1, keepdims=False).reshape(6, d)
        cxt = lax.dynamic_index_in_dim(m_all[:, layer], N_DEV + me, axis=1, keepdims=False).reshape(6, d)
        mods.append(jnp.stack([cxt, lat]))

    lb_slots = [TArg(lb_full[:, j], (2, aw), lambda i: (0, 0)) for j in range(depth + 1)]
    (lb0,) = tile_fwd("lb_fwd", f_lb(0), (1,), lb_slots, [((2, aw), F32, (2, aw), lambda i: (0, 0))])
    lb0r = lb0.reshape(2, n_ah, 1, HEAD)

    full_row = lambda i: (i, 0)
    par0 = lambda i: (0, 0)

    def nm_args(xs, layer, slot):
        return [TArg(xs, (ROW_TILE, d), full_row),
                TArg(nw_full[layer, slot].reshape(1, d), (1, d), par0, "par", (0,)),
                TArg(mods[layer], (2, 6, d), lambda i: (0, 0, 0), "par", (0,))]

    def norm_mod(name, xs, layer, slot, si, ci):
        (h,) = tile_fwd(name, f_norm_mod(si, ci, nct), (nt,), nm_args(xs, layer, slot),
                        [((t, d), BF16, (ROW_TILE, d), full_row)])
        return h

    def norm_mod_bwd(name, xs, layer, slot, si, ci, dh, carry):
        return tile_bwd(name, f_norm_mod(si, ci, nct, True), (nt,), nm_args(xs, layer, slot),
                        [(dh, (ROW_TILE, d), full_row), (carry, (ROW_TILE, d), full_row)])

    def gr_args(xs, ys, layer, slot):
        return [TArg(xs, (ROW_TILE, d), full_row, grad=False), TArg(ys, (ROW_TILE, d), full_row, gdtype=BF16),
                TArg(nw_full[layer, slot].reshape(1, d), (1, d), par0, "par", (0,)),
                TArg(mods[layer], (2, 6, d), lambda i: (0, 0, 0), "par", (0,))]

    def gate_res(name, xs, ys, layer, slot, gi):
        (o,) = tile_fwd(name, f_gate_res(gi, nct), (nt,), gr_args(xs, ys, layer, slot),
                        [((t, d), F32, (ROW_TILE, d), full_row)])
        return o

    def gate_res_bwd(name, xs, ys, layer, slot, gi, dx):
        return tile_bwd(name, f_gate_res(gi, nct), (nt,), gr_args(xs, ys, layer, slot),
                        [(dx, (ROW_TILE, d), full_row)])

    sw_rows = ROW_TILE // 2

    def sw_args(gu):
        return [TArg(gu, (sw_rows, 2 * ffn_h), full_row, gdtype=BF16)]

    def ffn_fwd(tag, xs, layer):
        h2 = norm_mod(f"nm2_{tag}", xs, layer, 2, 3, 4)
        gu = matmul(f"w13_{tag}", h2, w13[layer][0], "nn", slabs=w13[layer][1])
        (act,) = tile_fwd(f"swiglu_{tag}", f_swiglu, (t // sw_rows,), sw_args(gu),
                          [((t, ffn_h), BF16, (sw_rows, ffn_h), full_row)])
        fo = matmul(f"w2_{tag}", act, w2[layer], "nn")
        xn = gate_res(f"gr2_{tag}", xs, fo, layer, 3, 5)
        return xn, (xs, h2, gu, act, fo)

    def col_grad(name, a, dy, slabs):
        g = matmul(name, a, dy, "tn", slabs=slabs)
        return g if slabs else g.reshape(g.shape[0], N_DEV, g.shape[1] // N_DEV).transpose(1, 0, 2)

    def row_grad(name, a, dy):
        g = matmul(name, a, dy, "tn")
        return g.reshape(N_DEV, g.shape[0] // N_DEV, g.shape[1])

    def ffn_bwd(tag, saved, layer, dxn, acc):
        xs, h2, gu, act, fo = saved
        dfo, dnw3, dmod_a = gate_res_bwd(f"gr2b_{tag}", xs, fo, layer, 3, 5, dxn)
        dact = matmul(f"w2d_{tag}", dfo, w2[layer], "nt")
        dw2 = row_grad(f"w2w_{tag}", act, dfo)
        (dgu,) = tile_bwd(f"swiglub_{tag}", f_swiglu, (t // sw_rows,), sw_args(gu), [(dact, (sw_rows, ffn_h), full_row)])
        dh2 = matmul(f"w13d_{tag}", dgu, w13[layer][0], "nt", slabs=w13[layer][1])
        dw13 = col_grad(f"w13w_{tag}", h2, dgu, w13[layer][1])
        dxs, dnw2, dmod_b = norm_mod_bwd(f"nm2b_{tag}", xs, layer, 2, 3, 4, dh2, dxn)
        acc["ffn_w13"][layer] = dw13
        acc["ffn_w2"][layer] = dw2
        acc["norm_w"][layer][2] = dnw2
        acc["norm_w"][layer][3] = dnw3
        acc["mods"][layer].extend([dmod_a, dmod_b])
        return dxs

    def head_cols(width):
        return (ROW_TILE, width)

    n_a = t // A_CHUNK
    nca = n_ctx // A_CHUNK

    def a_tok(rev):
        if not rev:
            return lambda i: i
        return lambda i: jnp.where(i < nca, nca - 1 - i, n_a + nca - 1 - i)

    hb = _pick(n_ah, (HEADS_PER_STEP, 2, 1))
    n_hblk = n_ah // hb

    def hg_args(p, direction):
        tok = a_tok(direction == 1)
        blk = (A_CHUNK, hb * HEAD)
        return [TArg(p, blk, lambda h, i: (tok(i), h)),
                TArg(p, blk, lambda h, i: (tok(i), (1 + direction) * n_hblk + h)),
                TArg(p, blk, lambda h, i: (tok(i), 3 * n_hblk + h)),
                TArg(lb0r, (None, hb, 1, HEAD), lambda h, i: (direction, h, 0, 0), "par", (1,))]

    pmats = jnp.asarray(np.stack([_pool_mats(n_ctx), _pool_mats(GRID_W)]))

    def pool_args(p):
        return [TArg(p, (ROW_TILE, dg), lambda g, i: (i, 5 * n_grp + g)),
                TArg(pmats, (None, None, ROW_TILE, ROW_TILE), lambda g, i: (jnp.where(i < nct, 0, 1), g, 0, 0), "const"),
                TArg(pw_full, (None, dg, dg), lambda g, i: (g, 0, 0), "par", (1,)),
                TArg(ev_pool_scale, (1, dg), lambda g, i: (0, g), "par", (1,))]

    def ro_args(o_f, o_b, gate_arr, gate_off, nw_arr, n_heads):
        blk = (ROW_TILE, HEAD)
        return [TArg(o_f, blk, lambda h, i: (i, h)), TArg(o_b, blk, lambda h, i: (i, h), grad=False),
                TArg(gate_arr, blk, lambda h, i: (i, gate_off + h)),
                TArg(nw_arr, (1, HEAD), lambda h, i: (0, 0), "par", (0, 1))]

    def even_fwd(tag, xs, layer):
        h = norm_mod(f"nm1_{tag}", xs, layer, 0, 0, 1)
        p = matmul(f"win_{tag}", h, w_ev_in[0], "nn", slabs=w_ev_in[1])
        outs, saves = [], []
        for direction in (0, 1):
            (o,), sv = scan_fwd(f"hgrn_{tag}_{direction}", f_hgrn2(direction == 1, hb), n_hblk, n_a, hg_args(p, direction),
                                [((t, aw), F32, (A_CHUNK, hb * HEAD), lambda hh, i, tok=a_tok(direction == 1): (tok(i), hh))], hb)
            outs.append(o)
            saves.append(sv)
        (a_out,) = tile_fwd(f"ro_{tag}", f_readout, (n_ah, nt), ro_args(outs[0], outs[1], p, 4 * n_ah, ev_a_norm, n_ah),
                            [((t, aw), BF16, (ROW_TILE, HEAD), lambda hh, i: (i, hh))])
        (pooled,) = tile_fwd(f"pool_{tag}", f_pool, (n_grp, nt), pool_args(p),
                             [((t, aw), BF16, (ROW_TILE, dg), lambda g, i: (i, g))])
        cat = assemble(f"cat_{tag}", [[(a_out, aw, 0)], [(pooled, aw, 0)]], BF16)
        y = matmul(f"wout_{tag}", cat, w_ev_out, "nn")
        xn = gate_res(f"gr1_{tag}", xs, y, layer, 1, 2)
        return xn, (xs, h, p, outs, saves, cat, y)

    def even_bwd(tag, saved, layer, dxn, acc):
        xs, h, p, outs, saves, cat, y = saved
        dy, dnw1, dmod_a = gate_res_bwd(f"gr1b_{tag}", xs, y, layer, 1, 2, dxn)
        dcat = matmul(f"woutd_{tag}", dy, w_ev_out, "nt")
        acc["ev_w_out"] = row_grad(f"woutw_{tag}", cat, dy)
        do, dgate, d_anorm = tile_bwd(f"rob_{tag}", f_readout, (n_ah, nt),
                                      ro_args(outs[0], outs[1], p, 4 * n_ah, ev_a_norm, n_ah),
                                      [(dcat, (ROW_TILE, HEAD), lambda hh, i: (i, hh))])
        du, d_pw, d_ps = tile_bwd(f"poolb_{tag}", f_pool, (n_grp, nt), pool_args(p),
                                  [(dcat, (ROW_TILE, dg), lambda g, i: (i, n_grp + g))])
        dq, df, di, dlb = [], [], [], []
        for direction in (0, 1):
            r = scan_bwd(f"hgrnb_{tag}_{direction}", f_hgrn2(direction == 1, hb), n_hblk, n_a, hg_args(p, direction),
                         saves[direction],
                         [(do, (A_CHUNK, hb * HEAD), lambda hh, i, tok=a_tok(direction == 1): (tok(i), hh))])
            dq.append(r[0])
            df.append(r[1])
            di.append(r[2])
            dlb.append(r[3])
        sec = lambda arr, s: (arr, aw, s)
        dp = assemble(f"dp_{tag}", [[sec(dq[0], 0), sec(dq[1], 0)], [sec(df[0], 1)], [sec(df[1], 2)],
                                    [sec(di[0], 3), sec(di[1], 3)], [sec(dgate, 4)], [sec(du, 5)]], BF16)
        dh = matmul(f"wind_{tag}", dp, w_ev_in[0], "nt", slabs=w_ev_in[1])
        acc["ev_w_in"] = col_grad(f"winw_{tag}", h, dp, w_ev_in[1])
        dxs, dnw0, dmod_b = norm_mod_bwd(f"nm1b_{tag}", xs, layer, 0, 0, 1, dh, dxn)
        acc["norm_w"][layer][0] = dnw0
        acc["norm_w"][layer][1] = dnw1
        acc["mods"][layer].extend([dmod_a, dmod_b])
        acc["ev_a_norm"] = d_anorm
        acc["ev_pool_w"] = d_pw
        acc["ev_pool_scale"] = d_ps
        acc["lb0"] = jnp.stack([dlb[0][0], dlb[1][1]]).reshape(2, aw)
        return dxs

    n_c = t // C_CHUNK
    ncc = n_ctx // C_CHUNK

    def c_tok(rev):
        if not rev:
            return lambda i: i
        return lambda i: jnp.where(i < ncc, ncc - 1 - i, n_c + ncc - 1 - i)

    alog = od_A_log[0].reshape(2, n_kh, 2, 1)
    dtb = od_dt_bias[0].reshape(2, n_kh, 2, 1)

    khb = _pick(n_kh, (HEADS_PER_STEP, 2, 1))
    n_kblk = n_kh // khb

    def gd_args(z, gates, direction):
        tok = c_tok(direction == 1)
        gblk = (None, khb, None, 2, C_CHUNK)
        sblk = (None, khb, 2, 1)
        return [TArg(z, (C_CHUNK, khb * HEAD), lambda kb, i: (tok(i), kb)),
                TArg(z, (C_CHUNK, khb * HEAD), lambda kb, i: (tok(i), n_kblk + kb)),
                TArg(z, (C_CHUNK, khb * 2 * HEAD), lambda kb, i: (tok(i), n_kblk + kb)),
                TArg(gates, gblk, lambda kb, i: (direction, kb, tok(i), 0, 0)),
                TArg(gates, gblk, lambda kb, i: (2 + direction, kb, tok(i), 0, 0)),
                TArg(alog, sblk, lambda kb, i: (direction, kb, 0, 0), "par", (1,)),
                TArg(dtb, sblk, lambda kb, i: (direction, kb, 0, 0), "par", (1,))]

    def odd_fwd(tag, xs, layer):
        h = norm_mod(f"nm1_{tag}", xs, layer, 0, 0, 1)
        pm = matmul(f"win_{tag}", h, w_od_main, "nn")
        pg = matmul(f"wgate_{tag}", h, w_od_gate, "nn")
        z = conv_fwd(f"conv_{tag}", pm, cv_full, 2 * kw + vw, nct)
        gates = pg.reshape(n_c, C_CHUNK, 4, n_kh, 2).transpose(2, 3, 0, 4, 1)
        outs, saves = [], []
        for direction in (0, 1):
            (o,), sv = scan_fwd(f"gdn_{tag}_{direction}", f_gdn(direction == 1, khb), n_kblk, n_c, gd_args(z, gates, direction),
                                [((t, vw), F32, (C_CHUNK, khb * 2 * HEAD), lambda kb, i, tok=c_tok(direction == 1): (tok(i), kb))],
                                2 * khb)
            outs.append(o)
            saves.append(sv)
        n_vh = 2 * n_kh
        (yo,) = tile_fwd(f"ro_{tag}", f_readout, (n_vh, nt), ro_args(outs[0], outs[1], pm, 2 * n_kh + n_vh, od_norm, n_vh),
                         [((t, vw), BF16, (ROW_TILE, HEAD), lambda hh, i: (i, hh))])
        y = matmul(f"wout_{tag}", yo, w_od_out, "nn")
        xn = gate_res(f"gr1_{tag}", xs, y, layer, 1, 2)
        return xn, (xs, h, pm, z, gates, outs, saves, yo, y)

    def odd_bwd(tag, saved, layer, dxn, acc):
        xs, h, pm, z, gates, outs, saves, yo, y = saved
        n_vh = 2 * n_kh
        dy, dnw1, dmod_a = gate_res_bwd(f"gr1b_{tag}", xs, y, layer, 1, 2, dxn)
        dyo = matmul(f"woutd_{tag}", dy, w_od_out, "nt")
        acc["od_w_out"] = row_grad(f"woutw_{tag}", yo, dy)
        do, dzg, d_onorm = tile_bwd(f"rob_{tag}", f_readout, (n_vh, nt),
                                    ro_args(outs[0], outs[1], pm, 2 * n_kh + n_vh, od_norm, n_vh),
                                    [(dyo, (ROW_TILE, HEAD), lambda hh, i: (i, hh))])
        dq, dk, dv, dga, dgb, dal, ddt = [], [], [], [], [], [], []
        for direction in (0, 1):
            r = scan_bwd(f"gdnb_{tag}_{direction}", f_gdn(direction == 1, khb), n_kblk, n_c, gd_args(z, gates, direction),
                         saves[direction],
                         [(do, (C_CHUNK, khb * 2 * HEAD), lambda kb, i, tok=c_tok(direction == 1): (tok(i), kb))])
            for lst, v_ in zip((dq, dk, dv, dga, dgb, dal, ddt), r):
                lst.append(v_)
        dz = assemble(f"dz_{tag}", [[(dq[0], kw, 0), (dq[1], kw, 0)], [(dk[0], kw, 1), (dk[1], kw, 1)],
                                    [(dv[0], vw, 1), (dv[1], vw, 1)]], F32)
        du, d_conv = conv_bwd(f"convb_{tag}", pm, cv_full, dz, 2 * kw + vw, nct)
        dpm = assemble(f"dpm_{tag}", [[(du, 2 * kw + vw, 0)], [(dzg, vw, 2)]], BF16)
        dgates = jnp.stack([dga[0][0], dga[1][1], dgb[0][2], dgb[1][3]])
        dpg = dgates.transpose(2, 4, 0, 1, 3).reshape(t, n_gate).astype(BF16)
        dh = matmul(f"wgated_{tag}", dpg, w_od_gate, "nt")
        dh = matmul(f"wind_{tag}", dpm, w_od_main, "nt", add=dh)
        dw_in = jnp.concatenate([matmul(f"winw_{tag}", h, dpm, "tn"), matmul(f"wgatew_{tag}", h, dpg, "tn")], axis=1)
        acc["od_w_in"] = dw_in.reshape(d, N_DEV, dw_in.shape[1] // N_DEV).transpose(1, 0, 2)
        dxs, dnw0, dmod_b = norm_mod_bwd(f"nm1b_{tag}", xs, layer, 0, 0, 1, dh, dxn)
        acc["norm_w"][layer][0] = dnw0
        acc["norm_w"][layer][1] = dnw1
        acc["mods"][layer].extend([dmod_a, dmod_b])
        acc["od_norm"] = d_onorm
        acc["od_conv"] = d_conv
        acc["od_A_log"] = jnp.stack([dal[0][0], dal[1][1]]).reshape(1, 2, n_vh)
        acc["od_dt_bias"] = jnp.stack([ddt[0][0], ddt[1][1]]).reshape(1, 2, n_vh)
        return dxs

    xs0 = jnp.concatenate([ctx[0], x[0]], axis=0)
    xs1, sv_e = even_fwd("l0", xs0, 0)
    xs2, sv_f0 = ffn_fwd("l0", xs1, 0)
    xs3, sv_o = odd_fwd("l1", xs2, 1)
    xs4, sv_f1 = ffn_fwd("l1", xs3, 1)
    loss_loc, dxs = loss_kernel("loss", xs4, loss_target[0], nct)
    loss = lax.psum(loss_loc, ("x", "y", "c"))

    acc = {"norm_w": [[None] * 4 for _ in range(depth)], "mods": [[] for _ in range(depth)],
           "ffn_w13": [None] * depth, "ffn_w2": [None] * depth}
    dxs = ffn_bwd("l1", sv_f1, 1, dxs, acc)
    dxs = odd_bwd("l1", sv_o, 1, dxs, acc)
    dxs = ffn_bwd("l0", sv_f0, 0, dxs, acc)
    dxs = even_bwd("l0", sv_e, 0, dxs, acc)
    grad_x = dxs[n_ctx:].reshape(1, seq, d)

    (d_lb_slots) = tile_bwd("lb_bwd", f_lb(0), (1,), lb_slots, [(acc["lb0"], (2, aw), lambda i: (0, 0))])
    d_ev_lb = jnp.stack(d_lb_slots, axis=1)

    dmods = jnp.stack([functools.reduce(jnp.add, acc["mods"][layer]) for layer in range(depth)])
    (dm_all,) = all_gather("ag_dmod", [dmods.reshape(depth * 2 * 6, d)])
    dm_all = dm_all.reshape(N_DEV, depth, 2, 6 * d)
    dm_cols = lax.dynamic_slice_in_dim(dm_all, me * ada_loc, ada_loc, axis=3)
    dm_loc = jnp.concatenate([dm_cols[:, :, 1].transpose(1, 0, 2), dm_cols[:, :, 0].transpose(1, 0, 2)], axis=1)
    d_cctx_part, d_w_ada, d_b_loc = tile_bwd("ada_bwd", f_ada, ada_grid, ada_args,
                                             [(dm_loc, (None, 2 * N_DEV, ada_cb), lambda l, j: (l, 0, j))])

    d_b_full = lax.dynamic_update_slice_in_dim(jnp.zeros_like(b_ada), d_b_loc.reshape(depth, ada_loc), me * ada_loc, axis=1)
    d_nw = jnp.stack([jnp.stack([acc["norm_w"][layer][s].reshape(d) for s in range(4)]) for layer in range(depth)])
    small_grads = [d_cctx_part.reshape(d), d_b_full, d_nw, d_ev_lb, acc["ev_a_norm"], acc["ev_pool_w"],
                   acc["ev_pool_scale"], acc["od_conv"], acc["od_A_log"], acc["od_dt_bias"], acc["od_norm"]]
    sg_shapes = [a.shape for a in small_grads]
    (sg,) = all_gather("ag_smallgrads", [_pack(small_grads, F32, FLAT_W, SUBLANES)])
    sg_sum = sum_leading("sum_smallgrads", sg, F32)
    (g_cctx, g_bada, g_nw, g_lb, g_anorm, g_pw, g_ps, g_conv, g_alog, g_dtb, g_onorm) = _unpack(sg_sum, sg_shapes)

    def my_cols(full, axis):
        loc = full.shape[axis] // N_DEV
        return lax.dynamic_slice_in_dim(full, me * loc, loc, axis=axis)

    grads = {
        "c_ctx": g_cctx, "w_ada": d_w_ada, "b_ada": g_bada, "norm_w": my_cols(g_nw, 2), "ev_lb": my_cols(g_lb, 2),
        "ev_a_norm": g_anorm, "ev_pool_w": my_cols(g_pw, 1)[None], "ev_pool_scale": g_ps,
        "od_conv": my_cols(g_conv, 1)[None], "od_A_log": g_alog, "od_dt_bias": g_dtb, "od_norm": g_onorm,
    }

    tags = ["ev_in", "ev_out", "od_in", "od_out", "w13a", "w13b", "w2a", "w2b"]
    g8 = [acc["ev_w_in"], acc["ev_w_out"], acc["od_w_in"], acc["od_w_out"], acc["ffn_w13"][0], acc["ffn_w13"][1],
          acc["ffn_w2"][0], acc["ffn_w2"][1]]
    core = jnp.reshape(ac, (1,)).astype(jnp.int32)
    got = sibling_exchange("rs_sibling", g8)
    chip_sums = [add_own(f"rs_add_{tg}", g_, o_, core, BF16) for tg, g_, o_ in zip(tags, g8, got)]
    landed = chip_exchange("rs_chips", chip_sums)
    gb = [sum_leading(f"rs_sum_{tg}", l_, F32) for tg, l_ in zip(tags, landed)]
    grads["ev_w_in"], grads["ev_w_out"], grads["od_w_in"], grads["od_w_out"] = gb[0][None], gb[1][None], gb[2][None], gb[3][None]
    grads["ffn_w13"] = jnp.stack([gb[4], gb[5]])
    grads["ffn_w2"] = jnp.stack([gb[6], gb[7]])

    big_names = ["w_ada", "ev_w_in", "ev_w_out", "od_w_in", "od_w_out", "ffn_w13", "ffn_w2"]
    small_names = [n_ for n_ in names if n_ not in big_names]
    gl = {n_: grads[n_].reshape(wts[n_].shape) for n_ in names}
    delta, new_m, new_v = {}, {}, {}
    for n_ in big_names:
        shp = wts[n_].shape
        res = adamw(f"adamw_{n_}", _rows2d(gl[n_]), _rows2d(wts[n_]), _rows2d(mom1[n_]), _rows2d(mom2[n_]))
        delta[n_], new_m[n_], new_v[n_] = (r_.reshape(shp) for r_ in res)
    shapes = [wts[n_].shape for n_ in small_names]
    pk = lambda dct: _pack([dct[n_] for n_ in small_names], F32, FLAT_W, SUBLANES)
    res = adamw("adamw_small", pk(gl), pk(wts), pk(mom1), pk(mom2))
    for dct, r_ in zip((delta, new_m, new_v), res):
        for n_, a_ in zip(small_names, _unpack(r_, shapes)):
            dct[n_] = a_
    return (loss, grad_x, *[gl[n_] for n_ in names], *[delta[n_] for n_ in names], *[new_m[n_] for n_ in names],
            *[new_v[n_] for n_ in names])
```

```python
import functools
from typing import Any, NamedTuple

import numpy as np

import jax
import jax.numpy as jnp
from jax import lax
from jax.experimental import pallas as pl
from jax.experimental.pallas import tpu as pltpu

F32 = jnp.float32
BF16 = jnp.bfloat16
MESH = pl.DeviceIdType.MESH
N_DEV = 8

EPS = 1e-6
GRID_W = 64
HEAD = 128
A_CHUNK = 32
C_CHUNK = 64
C_CONV = 4
POOL_WINDOWS = (2, 4, 8, 16)
ADAM_LR, ADAM_B1, ADAM_B2, ADAM_EPS, ADAM_WD, ADAM_STEP = 0.001, 0.9, 0.999, 1e-08, 0.01, 10

VMEM_LIMIT_BYTES = 56 * 1024 * 1024
LANES = 128
SUBLANES = 8
ROW_TILE = 256
FLAT_W = 1024
FLAT_ROWS = 512
TM_PREFS = (1056, 768, 512, 256, 128, 64, 32, 16)
TN_PREFS = (512, 384, 1408, 256, 128)
TK_PREFS = (2048, 2816, 1408, 1024, 768, 512, 384, 256, 128)
TO_PREFS = (1024, 1408, 768, 704, 512, 384, 256, 128)
HEADS_PER_STEP = 4


def _pick(dim, prefs):
    for p in prefs:
        if p <= dim and dim % p == 0:
            return p
    return dim


def _cparams(ngrid):
    return pltpu.CompilerParams(dimension_semantics=("arbitrary",) * ngrid, vmem_limit_bytes=VMEM_LIMIT_BYTES)


def _sds(shape, dtype):
    return jax.ShapeDtypeStruct(tuple(shape), dtype)


def _split(x):
    hi = x.astype(BF16)
    return hi, (x - hi.astype(F32)).astype(BF16)


def _dot(a, b, ca, cb, hi):
    dims = (((ca,), (cb,)), ((), ()))
    dot = lambda u, v: lax.dot_general(u, v, dims, preferred_element_type=F32)
    if hi:
        (ah, al), (bh, bl) = _split(a.astype(F32)), _split(b.astype(F32))
        return dot(ah, bh) + (dot(ah, bl) + dot(al, bh))
    return dot(a.astype(BF16), b.astype(BF16))


@functools.partial(jax.custom_vjp, nondiff_argnums=(2, 3, 4))
def mm(a, b, ca=1, cb=0, hi=False):
    return _dot(a, b, ca, cb, hi)


def _mm_fwd(a, b, ca, cb, hi):
    return _dot(a, b, ca, cb, hi), (a, b)


def _mm_bwd(ca, cb, hi, res, g):
    a, b = res
    da = _dot(g, b, 1, 1 - cb, hi) if ca == 1 else _dot(b, g, 1 - cb, 1, hi)
    db = _dot(a, g, 1 - ca, 0, hi) if cb == 0 else _dot(g, a, 0, 1 - ca, hi)
    return da, db


mm.defvjp(_mm_fwd, _mm_bwd)


def _iota2(n, m, axis):
    return lax.broadcasted_iota(jnp.int32, (n, m), axis)


class TArg(NamedTuple):
    arr: Any
    block: tuple
    imap: Any
    kind: str = "row"
    acc: tuple = ()
    gdtype: Any = F32
    grad: bool = True


def _load(ref):
    v = ref[...]
    return v.astype(F32) if jnp.issubdtype(v.dtype, jnp.floating) else v


def tile_fwd(name, f, grid, args, outs):
    n_in, ng = len(args), len(grid)

    def body(*refs):
        pids = tuple(pl.program_id(k) for k in range(ng))
        res = f(pids, *[_load(r) for r in refs[:n_in]])
        for r, v in zip(refs[n_in:], res):
            r[...] = v.astype(r.dtype)

    return pl.pallas_call(
        body, grid=grid, name=name,
        in_specs=[pl.BlockSpec(a.block, a.imap) for a in args],
        out_specs=[pl.BlockSpec(b, im) for (_, _, b, im) in outs],
        out_shape=[_sds(s, d) for (s, d, _, _) in outs],
        compiler_params=_cparams(ng),
    )(*[a.arr for a in args])


def _store_grads(args, diff, pids, g_refs, d):
    for k, gr, dv in zip(diff, g_refs, d):
        a = args[k]
        if a.kind == "row" or not a.acc:
            gr[...] = dv.astype(gr.dtype)
        else:
            first = pids[a.acc[0]] == 0
            for ax in a.acc[1:]:
                first = jnp.logical_and(first, pids[ax] == 0)

            @pl.when(first)
            def _(gr=gr, dv=dv):
                gr[...] = dv.astype(gr.dtype)

            @pl.when(jnp.logical_not(first))
            def _(gr=gr, dv=dv):
                gr[...] += dv.astype(gr.dtype)


def tile_bwd(name, f, grid, args, cts):
    n_in, n_ct, ng = len(args), len(cts), len(grid)
    diff = [k for k, a in enumerate(args) if a.kind != "const" and a.grad]

    def body(*refs):
        pids = tuple(pl.program_id(k) for k in range(ng))
        vals = [_load(r) for r in refs[:n_in]]

        def g(*dv):
            full = list(vals)
            for k, v in zip(diff, dv):
                full[k] = v
            return tuple(f(pids, *full))

        _, vjp = jax.vjp(g, *[vals[k] for k in diff])
        d = vjp(tuple(_load(r) for r in refs[n_in:n_in + n_ct]))
        _store_grads(args, diff, pids, refs[n_in + n_ct:], d)

    return pl.pallas_call(
        body, grid=grid, name=name,
        in_specs=[pl.BlockSpec(a.block, a.imap) for a in args] + [pl.BlockSpec(b, im) for (_, b, im) in cts],
        out_specs=[pl.BlockSpec(args[k].block, args[k].imap) for k in diff],
        out_shape=[_sds(args[k].arr.shape, args[k].gdtype) for k in diff],
        compiler_params=_cparams(ng),
    )(*[a.arr for a in args], *[c[0] for c in cts])


def scan_fwd(name, f, n_heads, n_steps, args, outs, n_state):
    n_in, n_out = len(args), len(outs)
    sblock = (None, None, HEAD, HEAD)

    def body(*refs):
        in_refs = refs[:n_in]
        out_refs = refs[n_in:n_in + n_out]
        save_refs = refs[n_in + n_out:n_in + n_out + n_state]
        s_refs = refs[n_in + n_out + n_state:]

        @pl.when(pl.program_id(1) == 0)
        def _():
            for s in s_refs:
                s[...] = jnp.zeros_like(s)

        states = tuple(s[...] for s in s_refs)
        for sv, s in zip(save_refs, states):
            sv[...] = s
        new_states, res = f(states, *[_load(r) for r in in_refs])
        for s, v in zip(s_refs, new_states):
            s[...] = v
        for r, v in zip(out_refs, res):
            r[...] = v.astype(r.dtype)

    res = pl.pallas_call(
        body, grid=(n_heads, n_steps), name=name,
        in_specs=[pl.BlockSpec(a.block, a.imap) for a in args],
        out_specs=[pl.BlockSpec(b, im) for (_, _, b, im) in outs]
        + [pl.BlockSpec(sblock, lambda h, i: (h, i, 0, 0))] * n_state,
        out_shape=[_sds(s, d) for (s, d, _, _) in outs] + [_sds((n_heads, n_steps, HEAD, HEAD), F32)] * n_state,
        scratch_shapes=[pltpu.VMEM((HEAD, HEAD), F32)] * n_state,
        compiler_params=_cparams(2),
    )(*[a.arr for a in args])
    return res[:n_out], res[n_out:]


def scan_bwd(name, f, n_heads, n_steps, args, saves, cts):
    n_in, n_ct, n_state = len(args), len(cts), len(saves)
    diff = [k for k, a in enumerate(args) if a.kind != "const" and a.grad]
    sblock = (None, None, HEAD, HEAD)

    def rv(im):
        return lambda h, i: im(h, n_steps - 1 - i)

    def body(*refs):
        in_refs = refs[:n_in]
        save_refs = refs[n_in:n_in + n_state]
        ct_refs = refs[n_in + n_state:n_in + n_state + n_ct]
        g_refs = refs[n_in + n_state + n_ct:n_in + n_state + n_ct + len(diff)]
        ds_refs = refs[n_in + n_state + n_ct + len(diff):]
        pids = (pl.program_id(0), pl.program_id(1))

        @pl.when(pids[1] == 0)
        def _():
            for s in ds_refs:
                s[...] = jnp.zeros_like(s)

        vals = [_load(r) for r in in_refs]

        def g(states, *dv):
            full = list(vals)
            for k, v in zip(diff, dv):
                full[k] = v
            new_states, res = f(states, *full)
            return tuple(new_states), tuple(res)

        _, vjp = jax.vjp(g, tuple(s[...] for s in save_refs), *[vals[k] for k in diff])
        d = vjp((tuple(s[...] for s in ds_refs), tuple(_load(r) for r in ct_refs)))
        for s, v in zip(ds_refs, d[0]):
            s[...] = v
        _store_grads(args, diff, pids, g_refs, d[1:])

    return pl.pallas_call(
        body, grid=(n_heads, n_steps), name=name,
        in_specs=[pl.BlockSpec(a.block, rv(a.imap)) for a in args]
        + [pl.BlockSpec(sblock, rv(lambda h, i: (h, i, 0, 0)))] * n_state
        + [pl.BlockSpec(b, rv(im)) for (_, b, im) in cts],
        out_specs=[pl.BlockSpec(args[k].block, rv(args[k].imap)) for k in diff],
        out_shape=[_sds(args[k].arr.shape, args[k].gdtype) for k in diff],
        scratch_shapes=[pltpu.VMEM((HEAD, HEAD), F32)] * n_state,
        compiler_params=_cparams(2),
    )(*[a.arr for a in args], *saves, *[c[0] for c in cts])


def matmul(name, a, b, mode, add=None, out_dtype=F32, slabs=False):
    o_spec = None
    if mode == "nn":
        m, k = a.shape
        ns = b.shape[2] if slabs else b.shape[1]
        n = N_DEV * ns if slabs else ns
        to_m, to_n, tr = _pick(m, TM_PREFS), _pick(ns, TN_PREFS), _pick(k, TK_PREFS)
        nb = ns // to_n
        grid = (m // to_m, n // to_n, k // tr)
        a_spec = pl.BlockSpec((to_m, tr), lambda i, j, l: (i, l))
        if slabs:
            b_spec = pl.BlockSpec((None, tr, to_n), lambda i, j, l: (j // nb, l, j % nb))
        else:
            b_spec = pl.BlockSpec((tr, to_n), lambda i, j, l: (l, j))
        dims, oshape = (1, 0), (m, n)
    elif mode == "nt":
        m, n = a.shape
        k = b.shape[1] if slabs else b.shape[0]
        ns = n // N_DEV if slabs else n
        to_m, to_n, tr = _pick(m, TM_PREFS), _pick(k, TO_PREFS), _pick(ns, TK_PREFS)
        nb = ns // tr
        grid = (m // to_m, k // to_n, n // tr)
        a_spec = pl.BlockSpec((to_m, tr), lambda i, j, l: (i, l))
        if slabs:
            b_spec = pl.BlockSpec((None, to_n, tr), lambda i, j, l: (l // nb, j, l % nb))
        else:
            b_spec = pl.BlockSpec((to_n, tr), lambda i, j, l: (j, l))
        dims, oshape = (1, 1), (m, k)
    else:
        (t, k), n = a.shape, b.shape[1]
        ns = n // N_DEV if slabs else n
        to_m, to_n, tr = _pick(k, TO_PREFS), _pick(ns, TO_PREFS), _pick(t, TM_PREFS)
        nb = ns // to_n
        grid = (k // to_m, n // to_n, t // tr)
        a_spec = pl.BlockSpec((tr, to_m), lambda i, j, l: (l, i))
        b_spec = pl.BlockSpec((tr, to_n), lambda i, j, l: (l, j))
        dims, oshape = (0, 0), (k, n)
        if slabs:
            o_spec = pl.BlockSpec((None, to_m, to_n), lambda i, j, l: (j // nb, i, j % nb))
            oshape = (N_DEV, k, ns)
    n_red = grid[2]
    if o_spec is None:
        o_spec = pl.BlockSpec((to_m, to_n), lambda i, j, l: (i, j))
    has_add = add is not None

    def body(a_ref, b_ref, *rest):
        add_ref = rest[0] if has_add else None
        o_ref = rest[1] if has_add else rest[0]
        part = lax.dot_general(a_ref[...].astype(BF16), b_ref[...].astype(BF16),
                               (((dims[0],), (dims[1],)), ((), ())), preferred_element_type=F32)

        def finish(v):
            if has_add:
                v = v + add_ref[...]
            o_ref[...] = v.astype(o_ref.dtype)

        if n_red == 1:
            finish(part)
        else:
            acc = rest[-1]
            step = pl.program_id(2)

            @pl.when(step == 0)
            def _():
                acc[...] = part

            @pl.when(step > 0)
            def _():
                acc[...] += part

            @pl.when(step == n_red - 1)
            def _():
                finish(acc[...])

    return pl.pallas_call(
        body, grid=grid, name=name,
        in_specs=[a_spec, b_spec] + ([o_spec] if has_add else []),
        out_specs=o_spec, out_shape=_sds(oshape, out_dtype),
        scratch_shapes=[pltpu.VMEM((to_m, to_n), F32)] if n_red > 1 else [],
        compiler_params=_cparams(3),
    )(a, b, *([add] if has_add else []))


def assemble(name, pieces, out_dtype):
    flat = [s for piece in pieces for s in piece]
    t = flat[0][0].shape[0]
    widths = [piece[0][1] for piece in pieces]

    def body(*refs):
        o_ref, k, off = refs[-1], 0, 0
        for piece, w in zip(pieces, widths):
            v = refs[k][...].astype(F32)
            k += 1
            for _ in piece[1:]:
                v = v + refs[k][...].astype(F32)
                k += 1
            o_ref[:, off:off + w] = v.astype(o_ref.dtype)
            off += w

    tr = ROW_TILE // 2
    return pl.pallas_call(
        body, grid=(t // tr,), name=name,
        in_specs=[pl.BlockSpec((tr, w), functools.partial(lambda i, cb: (i, cb), cb=cb)) for (_, w, cb) in flat],
        out_specs=pl.BlockSpec((tr, sum(widths)), lambda i: (i, 0)),
        out_shape=_sds((t, sum(widths)), out_dtype),
        compiler_params=_cparams(1),
    )(*[s[0] for s in flat])


ELEM_ROWS = (128, 64, 32, 16, 8)


def _rows2d(a, lead=0):
    return a.reshape(a.shape[:lead] + (-1, a.shape[-1]))


def sum_leading(name, arr, out_dtype):
    k, rows, w = arr.shape
    tr = _pick(rows, ELEM_ROWS)

    def body(a_ref, o_ref):
        v = a_ref[0].astype(F32)
        for j in range(1, k):
            v = v + a_ref[j].astype(F32)
        o_ref[...] = v.astype(o_ref.dtype)

    return pl.pallas_call(
        body, grid=(rows // tr,), name=name,
        in_specs=[pl.BlockSpec((k, tr, w), lambda i: (0, i, 0))],
        out_specs=pl.BlockSpec((tr, w), lambda i: (i, 0)),
        out_shape=_sds((rows, w), out_dtype), compiler_params=_cparams(1),
    )(arr)


def add_own(name, g8, got, core, out_dtype):
    _, rows, w = g8.shape
    tr = _pick(rows, ELEM_ROWS)

    def body(core_ref, a_ref, b_ref, o_ref):
        o_ref[...] = (a_ref[...] + b_ref[...]).astype(o_ref.dtype)

    spec = pl.BlockSpec((None, tr, w), lambda q, i, core_ref: (q, i, 0))
    return pl.pallas_call(
        body, name=name,
        grid_spec=pltpu.PrefetchScalarGridSpec(
            num_scalar_prefetch=1, grid=(4, rows // tr),
            in_specs=[pl.BlockSpec((None, tr, w), lambda q, i, core_ref: (2 * q + core_ref[0], i, 0)), spec],
            out_specs=spec),
        out_shape=_sds(got.shape, out_dtype), compiler_params=_cparams(2),
    )(core, g8, got)


def adamw(name, g, w, m, v):
    rows, wd = g.shape
    tr = _pick(rows, ELEM_ROWS)

    def body(g_ref, w_ref, m_ref, v_ref, d_ref, nm_ref, nv_ref):
        gv = g_ref[...]
        mn = ADAM_B1 * m_ref[...] + (1.0 - ADAM_B1) * gv
        vn = ADAM_B2 * v_ref[...] + (1.0 - ADAM_B2) * jnp.square(gv)
        m_hat = mn / (1.0 - ADAM_B1 ** ADAM_STEP)
        v_hat = vn / (1.0 - ADAM_B2 ** ADAM_STEP)
        d_ref[...] = -ADAM_LR * (m_hat / (jnp.sqrt(v_hat) + ADAM_EPS) + ADAM_WD * w_ref[...])
        nm_ref[...] = mn
        nv_ref[...] = vn

    spec = pl.BlockSpec((tr, wd), lambda i: (i, 0))
    return pl.pallas_call(
        body, grid=(rows // tr,), name=name, in_specs=[spec] * 4, out_specs=[spec] * 3,
        out_shape=[_sds(g.shape, F32)] * 3, compiler_params=_cparams(1),
    )(g, w, m, v)


def loss_kernel(name, xs, target, n_ctx_tiles):
    t, d = xs.shape
    nt = t // ROW_TILE

    def body(x_ref, t_ref, dx_ref, l_ref):
        i = pl.program_id(0)
        is_lat = i >= n_ctx_tiles
        err = jnp.where(is_lat, x_ref[...] - t_ref[...], 0.0)
        dx_ref[...] = err / d
        part = 0.5 * jnp.sum(jnp.mean(jnp.square(err), axis=-1, keepdims=True), axis=0, keepdims=True)

        @pl.when(i == 0)
        def _():
            l_ref[...] = jnp.zeros_like(l_ref)

        l_ref[...] += jnp.broadcast_to(part, l_ref.shape)

    dx, l = pl.pallas_call(
        body, grid=(nt,), name=name,
        in_specs=[pl.BlockSpec((ROW_TILE, d), lambda i: (i, 0)),
                  pl.BlockSpec((ROW_TILE, d), lambda i: (jnp.maximum(i - n_ctx_tiles, 0), 0))],
        out_specs=[pl.BlockSpec((ROW_TILE, d), lambda i: (i, 0)), pl.BlockSpec((SUBLANES, LANES), lambda i: (0, 0))],
        out_shape=[_sds((t, d), F32), _sds((SUBLANES, LANES), F32)], compiler_params=_cparams(1),
    )(xs, target)
    return l[0, 0], dx


CONV_COLS = 512
CONV_LEFT = C_CONV // 2


def _conv_halo_specs(t, n_ctx_tiles):
    nt = t // ROW_TILE
    per = ROW_TILE // SUBLANES
    cur = pl.BlockSpec((ROW_TILE, CONV_COLS), lambda j, i: (i, j))
    prev = pl.BlockSpec((SUBLANES, CONV_COLS), lambda j, i: (jnp.maximum(i * per - 1, 0), j))
    nxt = pl.BlockSpec((SUBLANES, CONV_COLS), lambda j, i: (jnp.minimum((i + 1) * per, nt * per - 1), j))
    return cur, prev, nxt


def _fill_ext(ext, prev_ref, cur_ref, next_ref, i, nt, n_ctx_tiles):
    has_prev = jnp.logical_and(i != 0, i != n_ctx_tiles)
    has_next = jnp.logical_and(i != n_ctx_tiles - 1, i != nt - 1)
    ext[0:SUBLANES, :] = jnp.where(has_prev, prev_ref[...], 0.0)
    ext[SUBLANES:SUBLANES + ROW_TILE, :] = cur_ref[...]
    ext[SUBLANES + ROW_TILE:, :] = jnp.where(has_next, next_ref[...], 0.0)


def conv_fwd(name, p, w, width, n_ctx_tiles):
    t = p.shape[0]
    nt = t // ROW_TILE
    cur, prev, nxt = _conv_halo_specs(t, n_ctx_tiles)

    def body(c_ref, p_ref, n_ref, w_ref, o_ref, ext):
        _fill_ext(ext, p_ref, c_ref, n_ref, pl.program_id(1), nt, n_ctx_tiles)
        acc = None
        for j in range(C_CONV):
            term = ext[pl.ds(SUBLANES + j - CONV_LEFT, ROW_TILE), :] * w_ref[j:j + 1, :]
            acc = term if acc is None else acc + term
        o_ref[...] = acc

    return pl.pallas_call(
        body, grid=(width // CONV_COLS, nt), name=name,
        in_specs=[cur, prev, nxt, pl.BlockSpec((C_CONV, CONV_COLS), lambda j, i: (0, j))],
        out_specs=cur, out_shape=_sds((t, width), F32),
        scratch_shapes=[pltpu.VMEM((ROW_TILE + 2 * SUBLANES, CONV_COLS), F32)],
        compiler_params=_cparams(2),
    )(p, p, p, w)


def conv_bwd(name, p, w, dz, width, n_ctx_tiles):
    t = p.shape[0]
    nt = t // ROW_TILE
    cur, prev, nxt = _conv_halo_specs(t, n_ctx_tiles)

    def body(c_ref, p_ref, n_ref, dc_ref, dp_ref, dn_ref, w_ref, du_ref, dw_ref, ext, dext):
        i = pl.program_id(1)
        _fill_ext(ext, p_ref, c_ref, n_ref, i, nt, n_ctx_tiles)
        _fill_ext(dext, dp_ref, dc_ref, dn_ref, i, nt, n_ctx_tiles)
        dzc = dc_ref[...]
        @pl.when(i == 0)
        def _():
            dw_ref[...] = jnp.zeros_like(dw_ref)

        acc = None
        for j in range(C_CONV):
            term = dext[pl.ds(SUBLANES + CONV_LEFT - j, ROW_TILE), :] * w_ref[j:j + 1, :]
            acc = term if acc is None else acc + term
            dw_ref[j:j + 1, :] += jnp.sum(dzc * ext[pl.ds(SUBLANES + j - CONV_LEFT, ROW_TILE), :], axis=0, keepdims=True)
        du_ref[...] = acc

    wspec = pl.BlockSpec((C_CONV, CONV_COLS), lambda j, i: (0, j))
    return pl.pallas_call(
        body, grid=(width // CONV_COLS, nt), name=name,
        in_specs=[cur, prev, nxt, cur, prev, nxt, wspec],
        out_specs=[cur, wspec], out_shape=[_sds((t, width), F32), _sds((C_CONV, width), F32)],
        scratch_shapes=[pltpu.VMEM((ROW_TILE + 2 * SUBLANES, CONV_COLS), F32)] * 2,
        compiler_params=_cparams(2),
    )(p, p, p, dz, dz, dz, w)


def _rms(x, w):
    return x * lax.rsqrt(jnp.mean(x * x, axis=-1, keepdims=True) + EPS) * w


def _seg_mod(mods, is_ctx):
    return jnp.where(is_ctx, mods[0], mods[1])


def f_norm_mod(shift_i, scale_i, n_ctx_tiles, passthrough=False):
    def f(pids, x, nw, mods):
        m = _seg_mod(mods, pids[0] < n_ctx_tiles)
        h = _rms(x, nw) * (1.0 + m[scale_i:scale_i + 1]) + m[shift_i:shift_i + 1]
        return (h, x) if passthrough else (h,)
    return f


def f_gate_res(gate_i, n_ctx_tiles):
    def f(pids, x, y, nw, mods):
        m = _seg_mod(mods, pids[0] < n_ctx_tiles)
        return (x + m[gate_i:gate_i + 1] * _rms(y, nw),)
    return f


def f_swiglu(pids, gu):
    half = gu.shape[1] // 2
    return (jax.nn.silu(gu[:, :half]) * gu[:, half:],)


def f_readout(n_heads):
    def f(pids, o_a, o_b, gate, nw):
        cols = [slice(j * HEAD, (j + 1) * HEAD) for j in range(n_heads)]
        outs = _each(lambda cs: _rms(o_a[:, cs] + o_b[:, cs], nw) * jax.nn.silu(gate[:, cs]), cols)
        return (jnp.concatenate(outs, axis=1) if n_heads > 1 else outs[0],)
    return f


def f_pool(pids, u, pmat, pw, scale):
    d = mm(pmat, u, 1, 0, True) - u
    return (mm(d, pw) * scale,)


def f_lb(layer):
    def f(pids, *slots):
        top = slots[0]
        for s in slots[1:]:
            top = jnp.maximum(top, s)
        ex = [jnp.exp(s - top) for s in slots]
        tot = ex[0]
        for e in ex[1:]:
            tot = tot + e
        part = ex[0]
        for e in ex[1:layer + 1]:
            part = part + e
        return (part / tot,)
    return f


def f_ada(pids, c_all, c_ctx, w, b):
    c16 = jnp.concatenate([c_all, jnp.broadcast_to(c_ctx, c_all.shape)], axis=0)
    return (mm(jax.nn.silu(c16), w) + b,)


def _each(fn, *lists):
    return [fn(*xs) for xs in zip(*lists)]


def _hgrn2_heads(sts, qrs, frs, irs, lbs, rev):
    c = A_CHUNK
    mid = c - c // 2 if rev else c // 2 - 1
    ri, ci = _iota2(c, c, 0), _iota2(c, c, 1)
    incl = (ri <= ci) if rev else (ri >= ci)
    incl_f = incl.astype(F32)
    qs = _each(jax.nn.silu, qrs)
    log_fs = _each(lambda lb, fr: jnp.log(lb + (1.0 - lb) * jax.nn.sigmoid(fr)), lbs, frs)
    ks = _each(lambda lb, fr: (1.0 - lb) * jax.nn.sigmoid(-fr), lbs, frs)
    bs = _each(lambda lf: mm(incl_f, lf, 1, 0, True), log_fs)
    b_lasts = _each(lambda lf: jnp.sum(lf, axis=0, keepdims=True), log_fs)
    scores = _each(lambda q, k, b: mm(q * jnp.exp(b - b[mid:mid + 1]), k * jnp.exp(b[mid:mid + 1] - b), 1, 1), qs, ks, bs)
    intra = _each(lambda sc, ir: mm(jnp.where(incl, sc, 0.0), ir), scores, irs)
    inter = _each(lambda q, b, st: mm(q * jnp.exp(b), st, 1, 1), qs, bs, sts)
    upd = _each(lambda ir, k, bl, b: mm(ir, k * jnp.exp(bl - b), 0, 0), irs, ks, b_lasts, bs)
    new = _each(lambda st, bl, u: st * jnp.exp(bl) + u, sts, b_lasts, upd)
    return new, _each(jnp.add, intra, inter)


def f_hgrn2(rev, hb):
    def f(states, qr, fr, ir, lb):
        cols = [slice(j * HEAD, (j + 1) * HEAD) for j in range(hb)]
        new, outs = _hgrn2_heads(list(states), [qr[:, cs] for cs in cols], [fr[:, cs] for cs in cols],
                                 [ir[:, cs] for cs in cols], [lb[j] for j in range(hb)], rev)
        return tuple(new), (jnp.concatenate(outs, axis=1) if hb > 1 else outs[0],)
    return f


def _neumann_inv(a_lows):
    n = a_lows[0].shape[0]
    eye = (_iota2(n, n, 0) == _iota2(n, n, 1)).astype(F32)
    ps = _each(lambda a: -a, a_lows)
    xs = _each(lambda p: eye + p, ps)
    k = 2
    while k < n:
        ps = _each(lambda p: mm(p, p, 1, 0, True), ps)
        xs = _each(lambda x, p: x + mm(x, p, 1, 0, True), xs, ps)
        k *= 2
    return tuple(xs)


@jax.custom_vjp
def unit_tri_inv(a_lows):
    return _neumann_inv(a_lows)


def _uti_fwd(a_lows):
    xs = _neumann_inv(a_lows)
    return xs, xs


def _uti_bwd(xs, gs):
    ts = _each(lambda x, g: mm(x, g, 0, 0, True), xs, gs)
    return (tuple(_each(lambda t, x: -mm(t, x, 1, 1, True), ts, xs)),)


unit_tri_inv.defvjp(_uti_fwd, _uti_bwd)


@jax.custom_vjp
def unit_tri_inv_saved(a_lows, xs):
    return xs


def _utis_fwd(a_lows, xs):
    return xs, xs


def _utis_bwd(xs, gs):
    return _uti_bwd(xs, gs) + (tuple(jnp.zeros_like(x) for x in xs),)


unit_tri_inv_saved.defvjp(_utis_fwd, _utis_bwd)


def _l2n(x):
    return x * lax.rsqrt(jnp.sum(x * x, axis=-1, keepdims=True) + EPS)


def _gdn_heads(ss, qs, ks, vs, a_rows, b_rows, alogs, dtbs, rev, xs_saved=None):
    c = qs[0].shape[0]
    ri, ci = _iota2(c, c, 0), _iota2(c, c, 1)
    causal = (ri <= ci) if rev else (ri >= ci)
    causal_t = (ri >= ci) if rev else (ri <= ci)
    strict = (ri < ci) if rev else (ri > ci)
    eye = ri == ci
    sq = lambda row: jnp.broadcast_to(row, (c, c))
    to_col = lambda row: jnp.sum(jnp.where(eye, sq(row), 0.0), axis=1, keepdims=True)
    g_rows = _each(lambda al, a, dt: -jnp.exp(al) * jax.nn.softplus(a + dt), alogs, a_rows, dtbs)
    beta_cols = _each(lambda b: to_col(jax.nn.sigmoid(b)), b_rows)
    g_cols = _each(to_col, g_rows)
    gc_cols = _each(lambda g: jnp.sum(jnp.where(causal, sq(g), 0.0), axis=1, keepdims=True), g_rows)
    gc_rows = _each(lambda g: jnp.sum(jnp.where(causal_t, sq(g), 0.0), axis=0, keepdims=True), g_cols)
    gc_lasts = _each(lambda g: jnp.sum(g, axis=1, keepdims=True), g_rows)
    decays = _each(lambda gc, gr: jnp.where(causal, jnp.exp(jnp.where(causal, gc - gr, 0.0)), 0.0), gc_cols, gc_rows)
    k_betas = _each(jnp.multiply, ks, beta_cols)
    v_betas = _each(jnp.multiply, vs, beta_cols)
    a_lows = _each(lambda kb, k, dec: jnp.where(strict, mm(kb, k, 1, 1) * dec, 0.0), k_betas, ks, decays)
    xs = unit_tri_inv(tuple(a_lows)) if xs_saved is None else unit_tri_inv_saved(tuple(a_lows), tuple(xs_saved))
    egcs = _each(jnp.exp, gc_cols)
    us = _each(lambda x, vb: mm(x, vb, 1, 0, True), xs, v_betas)
    ws = _each(lambda x, kb, e: mm(x, kb * e, 1, 0, True), xs, k_betas, egcs)
    qks = _each(lambda q, k, dec: jnp.where(causal, mm(q, k, 1, 1) * dec, 0.0), qs, ks, decays)
    v_news = _each(lambda u, w, s: u - mm(w, s), us, ws, ss)
    o_states = _each(lambda q, e, s: mm(q * e, s), qs, egcs, ss)
    o_locals = _each(mm, qks, v_news)
    upds = _each(lambda k, gl, gc, vn: mm(k * jnp.exp(gl - gc), vn, 0, 0), ks, gc_lasts, gc_cols, v_news)
    new = _each(lambda s, gl, u: s * jnp.exp(gl) + u, ss, gc_lasts, upds)
    return new, _each(jnp.add, o_states, o_locals), xs


def f_gdn(rev, khb, saved_inverse):
    def f(states, qr, kr, vr, a3, b3, alog3, dtb3, xcat=None):
        heads = [(j, r) for j in range(khb) for r in range(2)]
        cols = [slice(j * HEAD, (j + 1) * HEAD) for j in range(khb)]
        c = qr.shape[0]
        qk_ = _each(lambda cs: (_l2n(jax.nn.silu(qr[:, cs])) * (HEAD ** -0.5), _l2n(jax.nn.silu(kr[:, cs]))), cols)
        vs = [jax.nn.silu(vr[:, (2 * j + r) * HEAD:(2 * j + r + 1) * HEAD]) for j, r in heads]
        row = lambda arr3: [arr3[j][r:r + 1] for j, r in heads]
        xs_saved = [xcat[n * c:(n + 1) * c] for n in range(len(heads))] if saved_inverse else None
        new, outs, xs = _gdn_heads(list(states), [qk_[j][0] for j, _ in heads], [qk_[j][1] for j, _ in heads], vs,
                                   row(a3), row(b3), row(alog3), row(dtb3), rev, xs_saved)
        o = jnp.concatenate(outs, axis=1)
        return tuple(new), ((o,) if saved_inverse else (o, jnp.concatenate(xs, axis=0)))
    return f


def _hbm_spec():
    return pl.BlockSpec(memory_space=pltpu.HBM)


def all_gather(name, xs):
    nt = len(xs)

    def body(*refs):
        x_refs, out_refs = refs[:nt], refs[nt:2 * nt]
        send_sems, recv_sems, local_sems = refs[2 * nt:]
        x, y, c = lax.axis_index("x"), lax.axis_index("y"), lax.axis_index("c")
        me, sibling = (x, y, c), (x, y, 1 - c)
        chips = [(1 - x, y), (x, 1 - y), (1 - x, 1 - y)]

        def slab(t, px, py, pc):
            return out_refs[t].at[4 * px + 2 * py + pc]

        def copy(t, k, block, to, src=None):
            return pltpu.make_async_remote_copy(
                src_ref=slab(t, *block) if src is None else src, dst_ref=slab(t, *block),
                send_sem=send_sems.at[7 * t + k], recv_sem=recv_sems.at[7 * t + k], device_id=to, device_id_type=MESH)

        mine, first, passed = [], [], []
        for t in range(nt):
            mine.append(pltpu.make_async_copy(x_refs[t], slab(t, *me), local_sems.at[t]))
            mine[-1].start()
            cps = [copy(t, 0, me, sibling, src=x_refs[t])]
            cps += [copy(t, 1 + j, me, (*chip, c), src=x_refs[t]) for j, chip in enumerate(chips)]
            for cp in cps:
                cp.start()
            first += cps
        for j, chip in enumerate(chips):
            for t in range(nt):
                copy(t, 1 + j, (*chip, c), me).wait_recv()
                fw = copy(t, 4 + j, (*chip, c), sibling)
                fw.start()
                passed.append(fw)
        for t in range(nt):
            copy(t, 0, sibling, me).wait_recv()
            for j, chip in enumerate(chips):
                copy(t, 4 + j, (*chip, 1 - c), me).wait_recv()
        for cp in first + passed:
            cp.wait_send()
        for cp in mine:
            cp.wait()

    return pl.pallas_call(
        body, name=name, out_shape=[_sds((N_DEV,) + a.shape, a.dtype) for a in xs],
        in_specs=[_hbm_spec()] * nt, out_specs=[_hbm_spec()] * nt,
        scratch_shapes=[pltpu.SemaphoreType.DMA((7 * nt,)), pltpu.SemaphoreType.DMA((7 * nt,)),
                        pltpu.SemaphoreType.DMA((nt,))],
    )(*xs)


def sibling_exchange(name, gs):
    nt = len(gs)

    def body(*refs):
        g_refs, l_refs, send_sems, recv_sems = refs[:nt], refs[nt:2 * nt], refs[2 * nt], refs[2 * nt + 1]
        x, y, c = lax.axis_index("x"), lax.axis_index("y"), lax.axis_index("c")
        cps = []
        for t in range(nt):
            for q in range(4):
                cps.append(pltpu.make_async_remote_copy(
                    src_ref=g_refs[t].at[2 * q + (1 - c)], dst_ref=l_refs[t].at[q], send_sem=send_sems.at[4 * t + q],
                    recv_sem=recv_sems.at[4 * t + q], device_id=(x, y, 1 - c), device_id_type=MESH))
        for cp in cps:
            cp.start()
        for cp in cps:
            cp.wait()

    return pl.pallas_call(
        body, name=name, out_shape=[_sds((4,) + g.shape[1:], g.dtype) for g in gs],
        in_specs=[_hbm_spec()] * nt, out_specs=[_hbm_spec()] * nt,
        scratch_shapes=[pltpu.SemaphoreType.DMA((4 * nt,)), pltpu.SemaphoreType.DMA((4 * nt,))],
    )(*gs)


def chip_exchange(name, hs):
    nt = len(hs)

    def body(*refs):
        h_refs, r_refs = refs[:nt], refs[nt:2 * nt]
        send_sems, recv_sems, local_sems = refs[2 * nt:]
        x, y, c = lax.axis_index("x"), lax.axis_index("y"), lax.axis_index("c")
        my = 2 * x + y
        chips = [(1 - x, y), (x, 1 - y), (1 - x, 1 - y)]

        def copy(t, k, src_slot, dst_slot, to):
            return pltpu.make_async_remote_copy(
                src_ref=h_refs[t].at[src_slot], dst_ref=r_refs[t].at[dst_slot], send_sem=send_sems.at[3 * t + k],
                recv_sem=recv_sems.at[3 * t + k], device_id=(*to, c), device_id_type=MESH)

        mine, sends = [], []
        for t in range(nt):
            mine.append(pltpu.make_async_copy(h_refs[t].at[my], r_refs[t].at[my], local_sems.at[t]))
            mine[-1].start()
            for k, (qx, qy) in enumerate(chips):
                sends.append(copy(t, k, 2 * qx + qy, my, (qx, qy)))
                sends[-1].start()
        for t in range(nt):
            for k, (qx, qy) in enumerate(chips):
                copy(t, k, my, 2 * qx + qy, (qx, qy)).wait_recv()
        for cp in sends:
            cp.wait_send()
        for cp in mine:
            cp.wait()

    return pl.pallas_call(
        body, name=name, out_shape=[_sds(h.shape, h.dtype) for h in hs],
        in_specs=[_hbm_spec()] * nt, out_specs=[_hbm_spec()] * nt,
        scratch_shapes=[pltpu.SemaphoreType.DMA((3 * nt,)), pltpu.SemaphoreType.DMA((3 * nt,)),
                        pltpu.SemaphoreType.DMA((nt,))],
    )(*hs)


def _pack(arrs, dtype, width, row_mult, lead=0):
    ld = arrs[0].shape[:lead]
    flat = jnp.concatenate([a.reshape(ld + (-1,)).astype(dtype) for a in arrs], axis=-1)
    n = flat.shape[-1]
    q = width * row_mult
    npad = -(-n // q) * q
    flat = jnp.pad(flat, [(0, 0)] * lead + [(0, npad - n)])
    return flat.reshape(ld + (npad // width, width))


def _unpack(buf, shapes, lead=0):
    ld = buf.shape[:lead]
    flat = buf.reshape(ld + (-1,))
    out, off = [], 0
    for s in shapes:
        n = int(np.prod(s))
        out.append(flat[..., off:off + n].reshape(ld + tuple(s)))
        off += n
    return out


def _pool_mats(seg_len):
    mats = np.zeros((len(POOL_WINDOWS), ROW_TILE, ROW_TILE), np.float32)
    for gi, win in enumerate(POOL_WINDOWS):
        for p in range(ROW_TILE):
            base = (p // seg_len) * seg_len
            q = p - base
            lo = min(max(q - win // 2, 0), seg_len - 1)
            hi = min(max(q + win - 1 - win // 2, 0), seg_len - 1)
            mats[gi, p, base + lo:base + hi + 1] = 1.0 / (hi - lo + 1)
    return mats


def kernel(x, c, ctx, c_ctx, w_ada, b_ada, norm_w, ev_w_in, ev_lb, ev_a_norm, ev_pool_w, ev_pool_scale, ev_w_out, od_w_in, od_conv, od_A_log, od_dt_bias, od_norm, od_w_out, ffn_w13, ffn_w2, loss_target, m_c_ctx, m_w_ada, m_b_ada, m_norm_w, m_ev_w_in, m_ev_lb, m_ev_a_norm, m_ev_pool_w, m_ev_pool_scale, m_ev_w_out, m_od_w_in, m_od_conv, m_od_A_log, m_od_dt_bias, m_od_norm, m_od_w_out, m_ffn_w13, m_ffn_w2, v_c_ctx, v_w_ada, v_b_ada, v_norm_w, v_ev_w_in, v_ev_lb, v_ev_a_norm, v_ev_pool_w, v_ev_pool_scale, v_ev_w_out, v_od_w_in, v_od_conv, v_od_A_log, v_od_dt_bias, v_od_norm, v_od_w_out, v_ffn_w13, v_ffn_w2):
    names = ["c_ctx", "w_ada", "b_ada", "norm_w", "ev_w_in", "ev_lb", "ev_a_norm", "ev_pool_w", "ev_pool_scale",
             "ev_w_out", "od_w_in", "od_conv", "od_A_log", "od_dt_bias", "od_norm", "od_w_out", "ffn_w13", "ffn_w2"]
    wts = dict(zip(names, [c_ctx, w_ada, b_ada, norm_w, ev_w_in, ev_lb, ev_a_norm, ev_pool_w, ev_pool_scale,
                           ev_w_out, od_w_in, od_conv, od_A_log, od_dt_bias, od_norm, od_w_out, ffn_w13, ffn_w2]))
    mom1 = dict(zip(names, [m_c_ctx, m_w_ada, m_b_ada, m_norm_w, m_ev_w_in, m_ev_lb, m_ev_a_norm, m_ev_pool_w,
                            m_ev_pool_scale, m_ev_w_out, m_od_w_in, m_od_conv, m_od_A_log, m_od_dt_bias, m_od_norm,
                            m_od_w_out, m_ffn_w13, m_ffn_w2]))
    mom2 = dict(zip(names, [v_c_ctx, v_w_ada, v_b_ada, v_norm_w, v_ev_w_in, v_ev_lb, v_ev_a_norm, v_ev_pool_w,
                            v_ev_pool_scale, v_ev_w_out, v_od_w_in, v_od_conv, v_od_A_log, v_od_dt_bias, v_od_norm,
                            v_od_w_out, v_ffn_w13, v_ffn_w2]))

    ax, ay, ac = lax.axis_index("x"), lax.axis_index("y"), lax.axis_index("c")
    me = 4 * ax + 2 * ay + ac
    my_chip = 2 * ax + ay

    seq, d = x.shape[1], x.shape[2]
    n_ctx = ctx.shape[1]
    t = n_ctx + seq
    nt = t // ROW_TILE
    nct = n_ctx // ROW_TILE
    assert n_ctx == ROW_TILE and seq % ROW_TILE == 0 and ROW_TILE % GRID_W == 0
    depth = w_ada.shape[0]
    aw = d // 2
    n_ah = aw // HEAD
    n_grp = len(POOL_WINDOWS)
    dg = aw // n_grp
    assert dg % LANES == 0
    n_kh = d // HEAD
    kw, vw = n_kh * HEAD, 2 * n_kh * HEAD
    n_gate = 8 * n_kh
    ffn_h = ffn_w2.shape[1] * N_DEV
    ada_loc = w_ada.shape[2]
    assert depth == 2

    small_shapes = [(d,), norm_w.shape, ev_lb.shape, ev_pool_w.shape[1:], od_conv.shape[1:]]
    (g1,) = all_gather("ag_small", [_pack([c[0], norm_w, ev_lb, ev_pool_w[0], od_conv[0]], F32, LANES, SUBLANES)])
    c_all, nw_g, lb_g, pw_g, cv_g = _unpack(g1, small_shapes, lead=1)
    nw_full = nw_g.transpose(1, 2, 0, 3).reshape(depth, 4, d)
    lb_full = lb_g.transpose(1, 2, 0, 3).reshape(2, depth + 1, aw)
    pw_full = pw_g.transpose(1, 0, 2, 3).reshape(n_grp, dg, dg)
    cv_full = cv_g.transpose(1, 0, 2).reshape(C_CONV, 2 * kw + vw)

    big = [ev_w_in[0], ev_w_out[0], od_w_in[0], od_w_out[0], ffn_w13[0], ffn_w13[1], ffn_w2[0], ffn_w2[1]]
    g_ev_in, g_ev_out, g_od_in, g_od_out, g_w13a, g_w13b, g_w2a, g_w2b = all_gather(
        "ag_weights", [w_.astype(BF16) for w_ in big])

    def cols_natural(g):
        return g.transpose(1, 0, 2).reshape(g.shape[1], N_DEV * g.shape[2])

    def rows_natural(g):
        return g.reshape(N_DEV * g.shape[1], g.shape[2])

    def col_weight(g):
        return (g, True) if g.shape[2] % LANES == 0 else (cols_natural(g), False)

    w_ev_in = col_weight(g_ev_in)
    w_ev_out = rows_natural(g_ev_out)
    w_od_in = cols_natural(g_od_in)
    w_od_main, w_od_gate = w_od_in[:, :2 * kw + 2 * vw], w_od_in[:, 2 * kw + 2 * vw:]
    w_od_out = rows_natural(g_od_out)
    w13 = [col_weight(g_w13a), col_weight(g_w13b)]
    w2 = [rows_natural(g_w2a), rows_natural(g_w2b)]

    b_loc = lax.dynamic_slice_in_dim(b_ada, me * ada_loc, ada_loc, axis=1).reshape(depth, 1, ada_loc)
    ada_cb = _pick(ada_loc, TN_PREFS)
    ada_grid = (depth, ada_loc // ada_cb)
    ada_args = [
        TArg(c_all, (N_DEV, d), lambda l, j: (0, 0), "const"),
        TArg(c_ctx.reshape(1, d), (1, d), lambda l, j: (0, 0), "par", (0, 1)),
        TArg(w_ada, (None, d, ada_cb), lambda l, j: (l, 0, j)),
        TArg(b_loc, (None, 1, ada_cb), lambda l, j: (l, 0, j)),
    ]
    (m_loc,) = tile_fwd("ada_fwd", f_ada, ada_grid, ada_args,
                        [((depth, 2 * N_DEV, ada_loc), F32, (None, 2 * N_DEV, ada_cb), lambda l, j: (l, 0, j))])
    (m_all,) = all_gather("ag_mod", [m_loc])
    mods = []
    for layer in range(depth):
        lat = lax.dynamic_index_in_dim(m_all[:, layer], me, axis=1, keepdims=False).reshape(6, d)
        cxt = lax.dynamic_index_in_dim(m_all[:, layer], N_DEV + me, axis=1, keepdims=False).reshape(6, d)
        mods.append(jnp.stack([cxt, lat]))

    lb_slots = [TArg(lb_full[:, j], (2, aw), lambda i: (0, 0)) for j in range(depth + 1)]
    (lb0,) = tile_fwd("lb_fwd", f_lb(0), (1,), lb_slots, [((2, aw), F32, (2, aw), lambda i: (0, 0))])
    lb0r = lb0.reshape(2, n_ah, 1, HEAD)

    full_row = lambda i: (i, 0)
    par0 = lambda i: (0, 0)

    def nm_args(xs, layer, slot):
        return [TArg(xs, (ROW_TILE, d), full_row),
                TArg(nw_full[layer, slot].reshape(1, d), (1, d), par0, "par", (0,)),
                TArg(mods[layer], (2, 6, d), lambda i: (0, 0, 0), "par", (0,))]

    def norm_mod(name, xs, layer, slot, si, ci):
        (h,) = tile_fwd(name, f_norm_mod(si, ci, nct), (nt,), nm_args(xs, layer, slot),
                        [((t, d), BF16, (ROW_TILE, d), full_row)])
        return h

    def norm_mod_bwd(name, xs, layer, slot, si, ci, dh, carry):
        return tile_bwd(name, f_norm_mod(si, ci, nct, True), (nt,), nm_args(xs, layer, slot),
                        [(dh, (ROW_TILE, d), full_row), (carry, (ROW_TILE, d), full_row)])

    def gr_args(xs, ys, layer, slot):
        return [TArg(xs, (ROW_TILE, d), full_row, grad=False), TArg(ys, (ROW_TILE, d), full_row, gdtype=BF16),
                TArg(nw_full[layer, slot].reshape(1, d), (1, d), par0, "par", (0,)),
                TArg(mods[layer], (2, 6, d), lambda i: (0, 0, 0), "par", (0,))]

    def gate_res(name, xs, ys, layer, slot, gi):
        (o,) = tile_fwd(name, f_gate_res(gi, nct), (nt,), gr_args(xs, ys, layer, slot),
                        [((t, d), F32, (ROW_TILE, d), full_row)])
        return o

    def gate_res_bwd(name, xs, ys, layer, slot, gi, dx):
        return tile_bwd(name, f_gate_res(gi, nct), (nt,), gr_args(xs, ys, layer, slot),
                        [(dx, (ROW_TILE, d), full_row)])

    sw_rows = ROW_TILE // 2

    def sw_args(gu):
        return [TArg(gu, (sw_rows, 2 * ffn_h), full_row, gdtype=BF16)]

    def ffn_fwd(tag, xs, layer):
        h2 = norm_mod(f"nm2_{tag}", xs, layer, 2, 3, 4)
        gu = matmul(f"w13_{tag}", h2, w13[layer][0], "nn", slabs=w13[layer][1])
        (act,) = tile_fwd(f"swiglu_{tag}", f_swiglu, (t // sw_rows,), sw_args(gu),
                          [((t, ffn_h), BF16, (sw_rows, ffn_h), full_row)])
        fo = matmul(f"w2_{tag}", act, w2[layer], "nn")
        xn = gate_res(f"gr2_{tag}", xs, fo, layer, 3, 5)
        return xn, (xs, h2, gu, act, fo)

    def col_grad(name, a, dy, slabs):
        g = matmul(name, a, dy, "tn", slabs=slabs)
        return g if slabs else g.reshape(g.shape[0], N_DEV, g.shape[1] // N_DEV).transpose(1, 0, 2)

    def row_grad(name, a, dy):
        g = matmul(name, a, dy, "tn")
        return g.reshape(N_DEV, g.shape[0] // N_DEV, g.shape[1])

    def ffn_bwd(tag, saved, layer, dxn, acc):
        xs, h2, gu, act, fo = saved
        dfo, dnw3, dmod_a = gate_res_bwd(f"gr2b_{tag}", xs, fo, layer, 3, 5, dxn)
        dact = matmul(f"w2d_{tag}", dfo, w2[layer], "nt")
        dw2 = row_grad(f"w2w_{tag}", act, dfo)
        (dgu,) = tile_bwd(f"swiglub_{tag}", f_swiglu, (t // sw_rows,), sw_args(gu), [(dact, (sw_rows, ffn_h), full_row)])
        dh2 = matmul(f"w13d_{tag}", dgu, w13[layer][0], "nt", slabs=w13[layer][1])
        dw13 = col_grad(f"w13w_{tag}", h2, dgu, w13[layer][1])
        dxs, dnw2, dmod_b = norm_mod_bwd(f"nm2b_{tag}", xs, layer, 2, 3, 4, dh2, dxn)
        acc["ffn_w13"][layer] = dw13
        acc["ffn_w2"][layer] = dw2
        acc["norm_w"][layer][2] = dnw2
        acc["norm_w"][layer][3] = dnw3
        acc["mods"][layer].extend([dmod_a, dmod_b])
        return dxs

    def head_cols(width):
        return (ROW_TILE, width)

    n_a = t // A_CHUNK
    nca = n_ctx // A_CHUNK

    def a_tok(rev):
        if not rev:
            return lambda i: i
        return lambda i: jnp.where(i < nca, nca - 1 - i, n_a + nca - 1 - i)

    hb = _pick(n_ah, (HEADS_PER_STEP, 2, 1))
    n_hblk = n_ah // hb

    def hg_args(p, direction):
        tok = a_tok(direction == 1)
        blk = (A_CHUNK, hb * HEAD)
        return [TArg(p, blk, lambda h, i: (tok(i), h)),
                TArg(p, blk, lambda h, i: (tok(i), (1 + direction) * n_hblk + h)),
                TArg(p, blk, lambda h, i: (tok(i), 3 * n_hblk + h)),
                TArg(lb0r, (None, hb, 1, HEAD), lambda h, i: (direction, h, 0, 0), "par", (1,))]

    pmats = jnp.asarray(np.stack([_pool_mats(n_ctx), _pool_mats(GRID_W)]))

    def pool_args(p):
        return [TArg(p, (ROW_TILE, dg), lambda g, i: (i, 5 * n_grp + g)),
                TArg(pmats, (None, None, ROW_TILE, ROW_TILE), lambda g, i: (jnp.where(i < nct, 0, 1), g, 0, 0), "const"),
                TArg(pw_full, (None, dg, dg), lambda g, i: (g, 0, 0), "par", (1,)),
                TArg(ev_pool_scale, (1, dg), lambda g, i: (0, g), "par", (1,))]

    def ro_plan(gate_off, n_heads):
        per = _pick(n_heads, (8, 4, 2, 1))
        assert gate_off % per == 0
        return per, n_heads // per, gate_off // per

    def ro_args(o_f, o_b, gate_arr, gate_off, nw_arr, n_heads):
        per, _, goff = ro_plan(gate_off, n_heads)
        blk = (ROW_TILE, per * HEAD)
        return [TArg(o_f, blk, lambda h, i: (i, h)), TArg(o_b, blk, lambda h, i: (i, h), grad=False),
                TArg(gate_arr, blk, lambda h, i: (i, goff + h)),
                TArg(nw_arr, (1, HEAD), lambda h, i: (0, 0), "par", (0, 1))]

    def readout(name, o_f, o_b, gate_arr, gate_off, nw_arr, n_heads):
        per, nblk, _ = ro_plan(gate_off, n_heads)
        (o,) = tile_fwd(name, f_readout(per), (nblk, nt), ro_args(o_f, o_b, gate_arr, gate_off, nw_arr, n_heads),
                        [((t, n_heads * HEAD), BF16, (ROW_TILE, per * HEAD), lambda hh, i: (i, hh))])
        return o

    def readout_bwd(name, o_f, o_b, gate_arr, gate_off, nw_arr, n_heads, dout):
        per, nblk, _ = ro_plan(gate_off, n_heads)
        return tile_bwd(name, f_readout(per), (nblk, nt), ro_args(o_f, o_b, gate_arr, gate_off, nw_arr, n_heads),
                        [(dout, (ROW_TILE, per * HEAD), lambda hh, i: (i, hh))])

    def even_fwd(tag, xs, layer):
        h = norm_mod(f"nm1_{tag}", xs, layer, 0, 0, 1)
        p = matmul(f"win_{tag}", h, w_ev_in[0], "nn", slabs=w_ev_in[1])
        outs, saves = [], []
        for direction in (0, 1):
            (o,), sv = scan_fwd(f"hgrn_{tag}_{direction}", f_hgrn2(direction == 1, hb), n_hblk, n_a, hg_args(p, direction),
                                [((t, aw), F32, (A_CHUNK, hb * HEAD), lambda hh, i, tok=a_tok(direction == 1): (tok(i), hh))], hb)
            outs.append(o)
            saves.append(sv)
        a_out = readout(f"ro_{tag}", outs[0], outs[1], p, 4 * n_ah, ev_a_norm, n_ah)
        (pooled,) = tile_fwd(f"pool_{tag}", f_pool, (n_grp, nt), pool_args(p),
                             [((t, aw), BF16, (ROW_TILE, dg), lambda g, i: (i, g))])
        cat = assemble(f"cat_{tag}", [[(a_out, aw, 0)], [(pooled, aw, 0)]], BF16)
        y = matmul(f"wout_{tag}", cat, w_ev_out, "nn")
        xn = gate_res(f"gr1_{tag}", xs, y, layer, 1, 2)
        return xn, (xs, h, p, outs, saves, cat, y)

    def even_bwd(tag, saved, layer, dxn, acc):
        xs, h, p, outs, saves, cat, y = saved
        dy, dnw1, dmod_a = gate_res_bwd(f"gr1b_{tag}", xs, y, layer, 1, 2, dxn)
        dcat = matmul(f"woutd_{tag}", dy, w_ev_out, "nt")
        acc["ev_w_out"] = row_grad(f"woutw_{tag}", cat, dy)
        do, dgate, d_anorm = readout_bwd(f"rob_{tag}", outs[0], outs[1], p, 4 * n_ah, ev_a_norm, n_ah, dcat)
        du, d_pw, d_ps = tile_bwd(f"poolb_{tag}", f_pool, (n_grp, nt), pool_args(p),
                                  [(dcat, (ROW_TILE, dg), lambda g, i: (i, n_grp + g))])
        dq, df, di, dlb = [], [], [], []
        for direction in (0, 1):
            r = scan_bwd(f"hgrnb_{tag}_{direction}", f_hgrn2(direction == 1, hb), n_hblk, n_a, hg_args(p, direction),
                         saves[direction],
                         [(do, (A_CHUNK, hb * HEAD), lambda hh, i, tok=a_tok(direction == 1): (tok(i), hh))])
            dq.append(r[0])
            df.append(r[1])
            di.append(r[2])
            dlb.append(r[3])
        sec = lambda arr, s: (arr, aw, s)
        dp = assemble(f"dp_{tag}", [[sec(dq[0], 0), sec(dq[1], 0)], [sec(df[0], 1)], [sec(df[1], 2)],
                                    [sec(di[0], 3), sec(di[1], 3)], [sec(dgate, 4)], [sec(du, 5)]], BF16)
        dh = matmul(f"wind_{tag}", dp, w_ev_in[0], "nt", slabs=w_ev_in[1])
        acc["ev_w_in"] = col_grad(f"winw_{tag}", h, dp, w_ev_in[1])
        dxs, dnw0, dmod_b = norm_mod_bwd(f"nm1b_{tag}", xs, layer, 0, 0, 1, dh, dxn)
        acc["norm_w"][layer][0] = dnw0
        acc["norm_w"][layer][1] = dnw1
        acc["mods"][layer].extend([dmod_a, dmod_b])
        acc["ev_a_norm"] = d_anorm
        acc["ev_pool_w"] = d_pw
        acc["ev_pool_scale"] = d_ps
        acc["lb0"] = jnp.stack([dlb[0][0], dlb[1][1]]).reshape(2, aw)
        return dxs

    n_c = t // C_CHUNK
    ncc = n_ctx // C_CHUNK

    def c_tok(rev):
        if not rev:
            return lambda i: i
        return lambda i: jnp.where(i < ncc, ncc - 1 - i, n_c + ncc - 1 - i)

    alog = od_A_log[0].reshape(2, n_kh, 2, 1)
    dtb = od_dt_bias[0].reshape(2, n_kh, 2, 1)

    khb = _pick(n_kh, (HEADS_PER_STEP, 2, 1))
    n_kblk = n_kh // khb

    def gd_args(z, gates, direction):
        tok = c_tok(direction == 1)
        gblk = (None, khb, None, 2, C_CHUNK)
        sblk = (None, khb, 2, 1)
        return [TArg(z, (C_CHUNK, khb * HEAD), lambda kb, i: (tok(i), kb)),
                TArg(z, (C_CHUNK, khb * HEAD), lambda kb, i: (tok(i), n_kblk + kb)),
                TArg(z, (C_CHUNK, khb * 2 * HEAD), lambda kb, i: (tok(i), n_kblk + kb)),
                TArg(gates, gblk, lambda kb, i: (direction, kb, tok(i), 0, 0)),
                TArg(gates, gblk, lambda kb, i: (2 + direction, kb, tok(i), 0, 0)),
                TArg(alog, sblk, lambda kb, i: (direction, kb, 0, 0), "par", (1,)),
                TArg(dtb, sblk, lambda kb, i: (direction, kb, 0, 0), "par", (1,))]

    def odd_fwd(tag, xs, layer):
        h = norm_mod(f"nm1_{tag}", xs, layer, 0, 0, 1)
        pm = matmul(f"win_{tag}", h, w_od_main, "nn")
        pg = matmul(f"wgate_{tag}", h, w_od_gate, "nn")
        z = conv_fwd(f"conv_{tag}", pm, cv_full, 2 * kw + vw, nct)
        gates = pg.reshape(n_c, C_CHUNK, 4, n_kh, 2).transpose(2, 3, 0, 4, 1)
        outs, saves = [], []
        for direction in (0, 1):
            xrows = 2 * khb * C_CHUNK
            (o, xinv), sv = scan_fwd(
                f"gdn_{tag}_{direction}", f_gdn(direction == 1, khb, False), n_kblk, n_c, gd_args(z, gates, direction),
                [((t, vw), F32, (C_CHUNK, khb * 2 * HEAD), lambda kb, i, tok=c_tok(direction == 1): (tok(i), kb)),
                 ((n_kblk, n_c, xrows, C_CHUNK), F32, (None, None, xrows, C_CHUNK), lambda kb, i: (kb, i, 0, 0))],
                2 * khb)
            outs.append(o)
            saves.append((sv, xinv))
        n_vh = 2 * n_kh
        yo = readout(f"ro_{tag}", outs[0], outs[1], pm, 2 * n_kh + n_vh, od_norm, n_vh)
        y = matmul(f"wout_{tag}", yo, w_od_out, "nn")
        xn = gate_res(f"gr1_{tag}", xs, y, layer, 1, 2)
        return xn, (xs, h, pm, z, gates, outs, saves, yo, y)

    def odd_bwd(tag, saved, layer, dxn, acc):
        xs, h, pm, z, gates, outs, saves, yo, y = saved
        n_vh = 2 * n_kh
        dy, dnw1, dmod_a = gate_res_bwd(f"gr1b_{tag}", xs, y, layer, 1, 2, dxn)
        dyo = matmul(f"woutd_{tag}", dy, w_od_out, "nt")
        acc["od_w_out"] = row_grad(f"woutw_{tag}", yo, dy)
        do, dzg, d_onorm = readout_bwd(f"rob_{tag}", outs[0], outs[1], pm, 2 * n_kh + n_vh, od_norm, n_vh, dyo)
        dq, dk, dv, dga, dgb, dal, ddt = [], [], [], [], [], [], []
        for direction in (0, 1):
            sv, xinv = saves[direction]
            xarg = TArg(xinv, (None, None, 2 * khb * C_CHUNK, C_CHUNK), lambda kb, i: (kb, i, 0, 0), "const")
            r = scan_bwd(f"gdnb_{tag}_{direction}", f_gdn(direction == 1, khb, True), n_kblk, n_c,
                         gd_args(z, gates, direction) + [xarg], sv,
                         [(do, (C_CHUNK, khb * 2 * HEAD), lambda kb, i, tok=c_tok(direction == 1): (tok(i), kb))])
            for lst, v_ in zip((dq, dk, dv, dga, dgb, dal, ddt), r):
                lst.append(v_)
        dz = assemble(f"dz_{tag}", [[(dq[0], kw, 0), (dq[1], kw, 0)], [(dk[0], kw, 1), (dk[1], kw, 1)],
                                    [(dv[0], vw, 1), (dv[1], vw, 1)]], F32)
        du, d_conv = conv_bwd(f"convb_{tag}", pm, cv_full, dz, 2 * kw + vw, nct)
        dpm = assemble(f"dpm_{tag}", [[(du, 2 * kw + vw, 0)], [(dzg, vw, 2)]], BF16)
        dgates = jnp.stack([dga[0][0], dga[1][1], dgb[0][2], dgb[1][3]])
        dpg = dgates.transpose(2, 4, 0, 1, 3).reshape(t, n_gate).astype(BF16)
        dh = matmul(f"wgated_{tag}", dpg, w_od_gate, "nt")
        dh = matmul(f"wind_{tag}", dpm, w_od_main, "nt", add=dh)
        dw_in = jnp.concatenate([matmul(f"winw_{tag}", h, dpm, "tn"), matmul(f"wgatew_{tag}", h, dpg, "tn")], axis=1)
        acc["od_w_in"] = dw_in.reshape(d, N_DEV, dw_in.shape[1] // N_DEV).transpose(1, 0, 2)
        dxs, dnw0, dmod_b = norm_mod_bwd(f"nm1b_{tag}", xs, layer, 0, 0, 1, dh, dxn)
        acc["norm_w"][layer][0] = dnw0
        acc["norm_w"][layer][1] = dnw1
        acc["mods"][layer].extend([dmod_a, dmod_b])
        acc["od_norm"] = d_onorm
        acc["od_conv"] = d_conv
        acc["od_A_log"] = jnp.stack([dal[0][0], dal[1][1]]).reshape(1, 2, n_vh)
        acc["od_dt_bias"] = jnp.stack([ddt[0][0], ddt[1][1]]).reshape(1, 2, n_vh)
        return dxs

    xs0 = jnp.concatenate([ctx[0], x[0]], axis=0)
    xs1, sv_e = even_fwd("l0", xs0, 0)
    xs2, sv_f0 = ffn_fwd("l0", xs1, 0)
    xs3, sv_o = odd_fwd("l1", xs2, 1)
    xs4, sv_f1 = ffn_fwd("l1", xs3, 1)
    loss_loc, dxs = loss_kernel("loss", xs4, loss_target[0], nct)
    loss = lax.psum(loss_loc, ("x", "y", "c"))

    acc = {"norm_w": [[None] * 4 for _ in range(depth)], "mods": [[] for _ in range(depth)],
           "ffn_w13": [None] * depth, "ffn_w2": [None] * depth}
    dxs = ffn_bwd("l1", sv_f1, 1, dxs, acc)
    dxs = odd_bwd("l1", sv_o, 1, dxs, acc)
    dxs = ffn_bwd("l0", sv_f0, 0, dxs, acc)
    dxs = even_bwd("l0", sv_e, 0, dxs, acc)
    grad_x = dxs[n_ctx:].reshape(1, seq, d)

    (d_lb_slots) = tile_bwd("lb_bwd", f_lb(0), (1,), lb_slots, [(acc["lb0"], (2, aw), lambda i: (0, 0))])
    d_ev_lb = jnp.stack(d_lb_slots, axis=1)

    dmods = jnp.stack([functools.reduce(jnp.add, acc["mods"][layer]) for layer in range(depth)])
    (dm_all,) = all_gather("ag_dmod", [dmods.reshape(depth * 2 * 6, d)])
    dm_all = dm_all.reshape(N_DEV, depth, 2, 6 * d)
    dm_cols = lax.dynamic_slice_in_dim(dm_all, me * ada_loc, ada_loc, axis=3)
    dm_loc = jnp.concatenate([dm_cols[:, :, 1].transpose(1, 0, 2), dm_cols[:, :, 0].transpose(1, 0, 2)], axis=1)
    d_cctx_part, d_w_ada, d_b_loc = tile_bwd("ada_bwd", f_ada, ada_grid, ada_args,
                                             [(dm_loc, (None, 2 * N_DEV, ada_cb), lambda l, j: (l, 0, j))])

    d_b_full = lax.dynamic_update_slice_in_dim(jnp.zeros_like(b_ada), d_b_loc.reshape(depth, ada_loc), me * ada_loc, axis=1)
    d_nw = jnp.stack([jnp.stack([acc["norm_w"][layer][s].reshape(d) for s in range(4)]) for layer in range(depth)])
    small_grads = [d_cctx_part.reshape(d), d_b_full, d_nw, d_ev_lb, acc["ev_a_norm"], acc["ev_pool_w"],
                   acc["ev_pool_scale"], acc["od_conv"], acc["od_A_log"], acc["od_dt_bias"], acc["od_norm"]]
    sg_shapes = [a.shape for a in small_grads]
    (sg,) = all_gather("ag_smallgrads", [_pack(small_grads, F32, FLAT_W, SUBLANES)])
    sg_sum = sum_leading("sum_smallgrads", sg, F32)
    (g_cctx, g_bada, g_nw, g_lb, g_anorm, g_pw, g_ps, g_conv, g_alog, g_dtb, g_onorm) = _unpack(sg_sum, sg_shapes)

    def my_cols(full, axis):
        loc = full.shape[axis] // N_DEV
        return lax.dynamic_slice_in_dim(full, me * loc, loc, axis=axis)

    grads = {
        "c_ctx": g_cctx, "w_ada": d_w_ada, "b_ada": g_bada, "norm_w": my_cols(g_nw, 2), "ev_lb": my_cols(g_lb, 2),
        "ev_a_norm": g_anorm, "ev_pool_w": my_cols(g_pw, 1)[None], "ev_pool_scale": g_ps,
        "od_conv": my_cols(g_conv, 1)[None], "od_A_log": g_alog, "od_dt_bias": g_dtb, "od_norm": g_onorm,
    }

    tags = ["ev_in", "ev_out", "od_in", "od_out", "w13a", "w13b", "w2a", "w2b"]
    g8 = [acc["ev_w_in"], acc["ev_w_out"], acc["od_w_in"], acc["od_w_out"], acc["ffn_w13"][0], acc["ffn_w13"][1],
          acc["ffn_w2"][0], acc["ffn_w2"][1]]
    core = jnp.reshape(ac, (1,)).astype(jnp.int32)
    got = sibling_exchange("rs_sibling", g8)
    chip_sums = [add_own(f"rs_add_{tg}", g_, o_, core, BF16) for tg, g_, o_ in zip(tags, g8, got)]
    landed = chip_exchange("rs_chips", chip_sums)
    gb = [sum_leading(f"rs_sum_{tg}", l_, F32) for tg, l_ in zip(tags, landed)]
    grads["ev_w_in"], grads["ev_w_out"], grads["od_w_in"], grads["od_w_out"] = gb[0][None], gb[1][None], gb[2][None], gb[3][None]
    grads["ffn_w13"] = jnp.stack([gb[4], gb[5]])
    grads["ffn_w2"] = jnp.stack([gb[6], gb[7]])

    big_names = ["w_ada", "ev_w_in", "ev_w_out", "od_w_in", "od_w_out", "ffn_w13", "ffn_w2"]
    small_names = [n_ for n_ in names if n_ not in big_names]
    gl = {n_: grads[n_].reshape(wts[n_].shape) for n_ in names}
    delta, new_m, new_v = {}, {}, {}
    for n_ in big_names:
        shp = wts[n_].shape
        res = adamw(f"adamw_{n_}", _rows2d(gl[n_]), _rows2d(wts[n_]), _rows2d(mom1[n_]), _rows2d(mom2[n_]))
        delta[n_], new_m[n_], new_v[n_] = (r_.reshape(shp) for r_ in res)
    shapes = [wts[n_].shape for n_ in small_names]
    pk = lambda dct: _pack([dct[n_] for n_ in small_names], F32, FLAT_W, SUBLANES)
    res = adamw("adamw_small", pk(gl), pk(wts), pk(mom1), pk(mom2))
    for dct, r_ in zip((delta, new_m, new_v), res):
        for n_, a_ in zip(small_names, _unpack(r_, shapes)):
            dct[n_] = a_
    return (loss, grad_x, *[gl[n_] for n_ in names], *[delta[n_] for n_ in names], *[new_m[n_] for n_ in names],
            *[new_v[n_] for n_ in names])
```

```python
import functools
from typing import Any, NamedTuple

import numpy as np

import jax
import jax.numpy as jnp
from jax import lax
from jax.experimental import pallas as pl
from jax.experimental.pallas import tpu as pltpu

F32 = jnp.float32
BF16 = jnp.bfloat16
MESH = pl.DeviceIdType.MESH
N_DEV = 8

EPS = 1e-6
GRID_W = 64
HEAD = 128
A_CHUNK = 32
C_CHUNK = 64
C_CONV = 4
POOL_WINDOWS = (2, 4, 8, 16)
ADAM_LR, ADAM_B1, ADAM_B2, ADAM_EPS, ADAM_WD, ADAM_STEP = 0.001, 0.9, 0.999, 1e-08, 0.01, 10

VMEM_LIMIT_BYTES = 56 * 1024 * 1024
LANES = 128
SUBLANES = 8
ROW_TILE = 256
FLAT_W = 1024
FLAT_ROWS = 512
TM_PREFS = (1056, 768, 512, 256, 128, 64, 32, 16)
TN_PREFS = (512, 384, 1408, 256, 128)
TK_PREFS = (2048, 2816, 1408, 1024, 768, 512, 384, 256, 128)
TO_PREFS = (1024, 1408, 768, 704, 512, 384, 256, 128)
HEADS_PER_STEP = 4


def _pick(dim, prefs):
    for p in prefs:
        if p <= dim and dim % p == 0:
            return p
    return dim


def _cparams(ngrid):
    return pltpu.CompilerParams(dimension_semantics=("arbitrary",) * ngrid, vmem_limit_bytes=VMEM_LIMIT_BYTES)


def _sds(shape, dtype):
    return jax.ShapeDtypeStruct(tuple(shape), dtype)


def _split(x):
    hi = x.astype(BF16)
    return hi, (x - hi.astype(F32)).astype(BF16)


def _dot(a, b, ca, cb, hi):
    dims = (((ca,), (cb,)), ((), ()))
    dot = lambda u, v: lax.dot_general(u, v, dims, preferred_element_type=F32)
    if hi:
        (ah, al), (bh, bl) = _split(a.astype(F32)), _split(b.astype(F32))
        return dot(ah, bh) + (dot(ah, bl) + dot(al, bh))
    return dot(a.astype(BF16), b.astype(BF16))


@functools.partial(jax.custom_vjp, nondiff_argnums=(2, 3, 4))
def mm(a, b, ca=1, cb=0, hi=False):
    return _dot(a, b, ca, cb, hi)


def _mm_fwd(a, b, ca, cb, hi):
    return _dot(a, b, ca, cb, hi), (a, b)


def _mm_bwd(ca, cb, hi, res, g):
    a, b = res
    da = _dot(g, b, 1, 1 - cb, hi) if ca == 1 else _dot(b, g, 1 - cb, 1, hi)
    db = _dot(a, g, 1 - ca, 0, hi) if cb == 0 else _dot(g, a, 0, 1 - ca, hi)
    return da, db


mm.defvjp(_mm_fwd, _mm_bwd)


def _iota2(n, m, axis):
    return lax.broadcasted_iota(jnp.int32, (n, m), axis)


class TArg(NamedTuple):
    arr: Any
    block: tuple
    imap: Any
    kind: str = "row"
    acc: tuple = ()
    gdtype: Any = F32
    grad: bool = True


def _load(ref):
    v = ref[...]
    return v.astype(F32) if jnp.issubdtype(v.dtype, jnp.floating) else v


def tile_fwd(name, f, grid, args, outs):
    n_in, ng = len(args), len(grid)

    def body(*refs):
        pids = tuple(pl.program_id(k) for k in range(ng))
        res = f(pids, *[_load(r) for r in refs[:n_in]])
        for r, v in zip(refs[n_in:], res):
            r[...] = v.astype(r.dtype)

    return pl.pallas_call(
        body, grid=grid, name=name,
        in_specs=[pl.BlockSpec(a.block, a.imap) for a in args],
        out_specs=[pl.BlockSpec(b, im) for (_, _, b, im) in outs],
        out_shape=[_sds(s, d) for (s, d, _, _) in outs],
        compiler_params=_cparams(ng),
    )(*[a.arr for a in args])


def _store_grads(args, diff, pids, g_refs, d):
    for k, gr, dv in zip(diff, g_refs, d):
        a = args[k]
        if a.kind == "row" or not a.acc:
            gr[...] = dv.astype(gr.dtype)
        else:
            first = pids[a.acc[0]] == 0
            for ax in a.acc[1:]:
                first = jnp.logical_and(first, pids[ax] == 0)

            @pl.when(first)
            def _(gr=gr, dv=dv):
                gr[...] = dv.astype(gr.dtype)

            @pl.when(jnp.logical_not(first))
            def _(gr=gr, dv=dv):
                gr[...] += dv.astype(gr.dtype)


def tile_bwd(name, f, grid, args, cts):
    n_in, n_ct, ng = len(args), len(cts), len(grid)
    diff = [k for k, a in enumerate(args) if a.kind != "const" and a.grad]

    def body(*refs):
        pids = tuple(pl.program_id(k) for k in range(ng))
        vals = [_load(r) for r in refs[:n_in]]

        def g(*dv):
            full = list(vals)
            for k, v in zip(diff, dv):
                full[k] = v
            return tuple(f(pids, *full))

        _, vjp = jax.vjp(g, *[vals[k] for k in diff])
        d = vjp(tuple(_load(r) for r in refs[n_in:n_in + n_ct]))
        _store_grads(args, diff, pids, refs[n_in + n_ct:], d)

    return pl.pallas_call(
        body, grid=grid, name=name,
        in_specs=[pl.BlockSpec(a.block, a.imap) for a in args] + [pl.BlockSpec(b, im) for (_, b, im) in cts],
        out_specs=[pl.BlockSpec(args[k].block, args[k].imap) for k in diff],
        out_shape=[_sds(args[k].arr.shape, args[k].gdtype) for k in diff],
        compiler_params=_cparams(ng),
    )(*[a.arr for a in args], *[c[0] for c in cts])


def scan_fwd(name, f, n_heads, n_steps, args, outs, n_state):
    n_in, n_out = len(args), len(outs)
    sblock = (None, None, HEAD, HEAD)

    def body(*refs):
        in_refs = refs[:n_in]
        out_refs = refs[n_in:n_in + n_out]
        save_refs = refs[n_in + n_out:n_in + n_out + n_state]
        s_refs = refs[n_in + n_out + n_state:]

        @pl.when(pl.program_id(1) == 0)
        def _():
            for s in s_refs:
                s[...] = jnp.zeros_like(s)

        states = tuple(s[...] for s in s_refs)
        for sv, s in zip(save_refs, states):
            sv[...] = s
        new_states, res = f(states, *[_load(r) for r in in_refs])
        for s, v in zip(s_refs, new_states):
            s[...] = v
        for r, v in zip(out_refs, res):
            r[...] = v.astype(r.dtype)

    res = pl.pallas_call(
        body, grid=(n_heads, n_steps), name=name,
        in_specs=[pl.BlockSpec(a.block, a.imap) for a in args],
        out_specs=[pl.BlockSpec(b, im) for (_, _, b, im) in outs]
        + [pl.BlockSpec(sblock, lambda h, i: (h, i, 0, 0))] * n_state,
        out_shape=[_sds(s, d) for (s, d, _, _) in outs] + [_sds((n_heads, n_steps, HEAD, HEAD), F32)] * n_state,
        scratch_shapes=[pltpu.VMEM((HEAD, HEAD), F32)] * n_state,
        compiler_params=_cparams(2),
    )(*[a.arr for a in args])
    return res[:n_out], res[n_out:]


def scan_bwd(name, f, n_heads, n_steps, args, saves, cts):
    n_in, n_ct, n_state = len(args), len(cts), len(saves)
    diff = [k for k, a in enumerate(args) if a.kind != "const" and a.grad]
    sblock = (None, None, HEAD, HEAD)

    def rv(im):
        return lambda h, i: im(h, n_steps - 1 - i)

    def body(*refs):
        in_refs = refs[:n_in]
        save_refs = refs[n_in:n_in + n_state]
        ct_refs = refs[n_in + n_state:n_in + n_state + n_ct]
        g_refs = refs[n_in + n_state + n_ct:n_in + n_state + n_ct + len(diff)]
        ds_refs = refs[n_in + n_state + n_ct + len(diff):]
        pids = (pl.program_id(0), pl.program_id(1))

        @pl.when(pids[1] == 0)
        def _():
            for s in ds_refs:
                s[...] = jnp.zeros_like(s)

        vals = [_load(r) for r in in_refs]

        def g(states, *dv):
            full = list(vals)
            for k, v in zip(diff, dv):
                full[k] = v
            new_states, res = f(states, *full)
            return tuple(new_states), tuple(res)

        _, vjp = jax.vjp(g, tuple(s[...] for s in save_refs), *[vals[k] for k in diff])
        d = vjp((tuple(s[...] for s in ds_refs), tuple(_load(r) for r in ct_refs)))
        for s, v in zip(ds_refs, d[0]):
            s[...] = v
        _store_grads(args, diff, pids, g_refs, d[1:])

    return pl.pallas_call(
        body, grid=(n_heads, n_steps), name=name,
        in_specs=[pl.BlockSpec(a.block, rv(a.imap)) for a in args]
        + [pl.BlockSpec(sblock, rv(lambda h, i: (h, i, 0, 0)))] * n_state
        + [pl.BlockSpec(b, rv(im)) for (_, b, im) in cts],
        out_specs=[pl.BlockSpec(args[k].block, rv(args[k].imap)) for k in diff],
        out_shape=[_sds(args[k].arr.shape, args[k].gdtype) for k in diff],
        scratch_shapes=[pltpu.VMEM((HEAD, HEAD), F32)] * n_state,
        compiler_params=_cparams(2),
    )(*[a.arr for a in args], *saves, *[c[0] for c in cts])


def matmul(name, a, b, mode, add=None, out_dtype=F32, slabs=False):
    o_spec = None
    if mode == "nn":
        m, k = a.shape
        ns = b.shape[2] if slabs else b.shape[1]
        n = N_DEV * ns if slabs else ns
        to_m, to_n, tr = _pick(m, TM_PREFS), _pick(ns, TN_PREFS), _pick(k, TK_PREFS)
        nb = ns // to_n
        grid = (m // to_m, n // to_n, k // tr)
        a_spec = pl.BlockSpec((to_m, tr), lambda i, j, l: (i, l))
        if slabs:
            b_spec = pl.BlockSpec((None, tr, to_n), lambda i, j, l: (j // nb, l, j % nb))
        else:
            b_spec = pl.BlockSpec((tr, to_n), lambda i, j, l: (l, j))
        dims, oshape = (1, 0), (m, n)
    elif mode == "nt":
        m, n = a.shape
        k = b.shape[1] if slabs else b.shape[0]
        ns = n // N_DEV if slabs else n
        to_m, to_n, tr = _pick(m, TM_PREFS), _pick(k, TO_PREFS), _pick(ns, TK_PREFS)
        nb = ns // tr
        grid = (m // to_m, k // to_n, n // tr)
        a_spec = pl.BlockSpec((to_m, tr), lambda i, j, l: (i, l))
        if slabs:
            b_spec = pl.BlockSpec((None, to_n, tr), lambda i, j, l: (l // nb, j, l % nb))
        else:
            b_spec = pl.BlockSpec((to_n, tr), lambda i, j, l: (j, l))
        dims, oshape = (1, 1), (m, k)
    else:
        (t, k), n = a.shape, b.shape[1]
        ns = n // N_DEV if slabs else n
        to_m, to_n, tr = _pick(k, TO_PREFS), _pick(ns, TO_PREFS), _pick(t, TM_PREFS)
        nb = ns // to_n
        grid = (k // to_m, n // to_n, t // tr)
        a_spec = pl.BlockSpec((tr, to_m), lambda i, j, l: (l, i))
        b_spec = pl.BlockSpec((tr, to_n), lambda i, j, l: (l, j))
        dims, oshape = (0, 0), (k, n)
        if slabs:
            o_spec = pl.BlockSpec((None, to_m, to_n), lambda i, j, l: (j // nb, i, j % nb))
            oshape = (N_DEV, k, ns)
    n_red = grid[2]
    if o_spec is None:
        o_spec = pl.BlockSpec((to_m, to_n), lambda i, j, l: (i, j))
    has_add = add is not None

    def body(a_ref, b_ref, *rest):
        add_ref = rest[0] if has_add else None
        o_ref = rest[1] if has_add else rest[0]
        part = lax.dot_general(a_ref[...].astype(BF16), b_ref[...].astype(BF16),
                               (((dims[0],), (dims[1],)), ((), ())), preferred_element_type=F32)

        def finish(v):
            if has_add:
                v = v + add_ref[...]
            o_ref[...] = v.astype(o_ref.dtype)

        if n_red == 1:
            finish(part)
        else:
            acc = rest[-1]
            step = pl.program_id(2)

            @pl.when(step == 0)
            def _():
                acc[...] = part

            @pl.when(step > 0)
            def _():
                acc[...] += part

            @pl.when(step == n_red - 1)
            def _():
                finish(acc[...])

    return pl.pallas_call(
        body, grid=grid, name=name,
        in_specs=[a_spec, b_spec] + ([o_spec] if has_add else []),
        out_specs=o_spec, out_shape=_sds(oshape, out_dtype),
        scratch_shapes=[pltpu.VMEM((to_m, to_n), F32)] if n_red > 1 else [],
        compiler_params=_cparams(3),
    )(a, b, *([add] if has_add else []))


def assemble(name, pieces, out_dtype):
    flat = [s for piece in pieces for s in piece]
    t = flat[0][0].shape[0]
    widths = [piece[0][1] for piece in pieces]

    def body(*refs):
        o_ref, k, off = refs[-1], 0, 0
        for piece, w in zip(pieces, widths):
            v = refs[k][...].astype(F32)
            k += 1
            for _ in piece[1:]:
                v = v + refs[k][...].astype(F32)
                k += 1
            o_ref[:, off:off + w] = v.astype(o_ref.dtype)
            off += w

    tr = ROW_TILE // 2
    return pl.pallas_call(
        body, grid=(t // tr,), name=name,
        in_specs=[pl.BlockSpec((tr, w), functools.partial(lambda i, cb: (i, cb), cb=cb)) for (_, w, cb) in flat],
        out_specs=pl.BlockSpec((tr, sum(widths)), lambda i: (i, 0)),
        out_shape=_sds((t, sum(widths)), out_dtype),
        compiler_params=_cparams(1),
    )(*[s[0] for s in flat])


ELEM_ROWS = (128, 64, 32, 16, 8)


def _rows2d(a, lead=0):
    return a.reshape(a.shape[:lead] + (-1, a.shape[-1]))


def sum_leading(name, arr, out_dtype):
    k, rows, w = arr.shape
    tr = _pick(rows, ELEM_ROWS)

    def body(a_ref, o_ref):
        v = a_ref[0].astype(F32)
        for j in range(1, k):
            v = v + a_ref[j].astype(F32)
        o_ref[...] = v.astype(o_ref.dtype)

    return pl.pallas_call(
        body, grid=(rows // tr,), name=name,
        in_specs=[pl.BlockSpec((k, tr, w), lambda i: (0, i, 0))],
        out_specs=pl.BlockSpec((tr, w), lambda i: (i, 0)),
        out_shape=_sds((rows, w), out_dtype), compiler_params=_cparams(1),
    )(arr)


def add_own(name, g8, got, core, out_dtype):
    _, rows, w = g8.shape
    tr = _pick(rows, ELEM_ROWS)

    def body(core_ref, a_ref, b_ref, o_ref):
        o_ref[...] = (a_ref[...] + b_ref[...]).astype(o_ref.dtype)

    spec = pl.BlockSpec((None, tr, w), lambda q, i, core_ref: (q, i, 0))
    return pl.pallas_call(
        body, name=name,
        grid_spec=pltpu.PrefetchScalarGridSpec(
            num_scalar_prefetch=1, grid=(4, rows // tr),
            in_specs=[pl.BlockSpec((None, tr, w), lambda q, i, core_ref: (2 * q + core_ref[0], i, 0)), spec],
            out_specs=spec),
        out_shape=_sds(got.shape, out_dtype), compiler_params=_cparams(2),
    )(core, g8, got)


def adamw(name, g, w, m, v):
    rows, wd = g.shape
    tr = _pick(rows, ELEM_ROWS)

    def body(g_ref, w_ref, m_ref, v_ref, d_ref, nm_ref, nv_ref):
        gv = g_ref[...]
        mn = ADAM_B1 * m_ref[...] + (1.0 - ADAM_B1) * gv
        vn = ADAM_B2 * v_ref[...] + (1.0 - ADAM_B2) * jnp.square(gv)
        m_hat = mn / (1.0 - ADAM_B1 ** ADAM_STEP)
        v_hat = vn / (1.0 - ADAM_B2 ** ADAM_STEP)
        d_ref[...] = -ADAM_LR * (m_hat / (jnp.sqrt(v_hat) + ADAM_EPS) + ADAM_WD * w_ref[...])
        nm_ref[...] = mn
        nv_ref[...] = vn

    spec = pl.BlockSpec((tr, wd), lambda i: (i, 0))
    return pl.pallas_call(
        body, grid=(rows // tr,), name=name, in_specs=[spec] * 4, out_specs=[spec] * 3,
        out_shape=[_sds(g.shape, F32)] * 3, compiler_params=_cparams(1),
    )(g, w, m, v)


def loss_kernel(name, xs, target, n_ctx_tiles):
    t, d = xs.shape
    nt = t // ROW_TILE

    def body(x_ref, t_ref, dx_ref, l_ref):
        i = pl.program_id(0)
        is_lat = i >= n_ctx_tiles
        err = jnp.where(is_lat, x_ref[...] - t_ref[...], 0.0)
        dx_ref[...] = err / d
        part = 0.5 * jnp.sum(jnp.mean(jnp.square(err), axis=-1, keepdims=True), axis=0, keepdims=True)

        @pl.when(i == 0)
        def _():
            l_ref[...] = jnp.zeros_like(l_ref)

        l_ref[...] += jnp.broadcast_to(part, l_ref.shape)

    dx, l = pl.pallas_call(
        body, grid=(nt,), name=name,
        in_specs=[pl.BlockSpec((ROW_TILE, d), lambda i: (i, 0)),
                  pl.BlockSpec((ROW_TILE, d), lambda i: (jnp.maximum(i - n_ctx_tiles, 0), 0))],
        out_specs=[pl.BlockSpec((ROW_TILE, d), lambda i: (i, 0)), pl.BlockSpec((SUBLANES, LANES), lambda i: (0, 0))],
        out_shape=[_sds((t, d), F32), _sds((SUBLANES, LANES), F32)], compiler_params=_cparams(1),
    )(xs, target)
    return l[0, 0], dx


CONV_COLS = 512
CONV_LEFT = C_CONV // 2


def _conv_halo_specs(t, n_ctx_tiles):
    nt = t // ROW_TILE
    per = ROW_TILE // SUBLANES
    cur = pl.BlockSpec((ROW_TILE, CONV_COLS), lambda j, i: (i, j))
    prev = pl.BlockSpec((SUBLANES, CONV_COLS), lambda j, i: (jnp.maximum(i * per - 1, 0), j))
    nxt = pl.BlockSpec((SUBLANES, CONV_COLS), lambda j, i: (jnp.minimum((i + 1) * per, nt * per - 1), j))
    return cur, prev, nxt


def _fill_ext(ext, prev_ref, cur_ref, next_ref, i, nt, n_ctx_tiles):
    has_prev = jnp.logical_and(i != 0, i != n_ctx_tiles)
    has_next = jnp.logical_and(i != n_ctx_tiles - 1, i != nt - 1)
    ext[0:SUBLANES, :] = jnp.where(has_prev, prev_ref[...], 0.0)
    ext[SUBLANES:SUBLANES + ROW_TILE, :] = cur_ref[...]
    ext[SUBLANES + ROW_TILE:, :] = jnp.where(has_next, next_ref[...], 0.0)


def conv_fwd(name, p, w, width, n_ctx_tiles):
    t = p.shape[0]
    nt = t // ROW_TILE
    cur, prev, nxt = _conv_halo_specs(t, n_ctx_tiles)

    def body(c_ref, p_ref, n_ref, w_ref, o_ref, ext):
        _fill_ext(ext, p_ref, c_ref, n_ref, pl.program_id(1), nt, n_ctx_tiles)
        acc = None
        for j in range(C_CONV):
            term = ext[pl.ds(SUBLANES + j - CONV_LEFT, ROW_TILE), :] * w_ref[j:j + 1, :]
            acc = term if acc is None else acc + term
        o_ref[...] = acc

    return pl.pallas_call(
        body, grid=(width // CONV_COLS, nt), name=name,
        in_specs=[cur, prev, nxt, pl.BlockSpec((C_CONV, CONV_COLS), lambda j, i: (0, j))],
        out_specs=cur, out_shape=_sds((t, width), F32),
        scratch_shapes=[pltpu.VMEM((ROW_TILE + 2 * SUBLANES, CONV_COLS), F32)],
        compiler_params=_cparams(2),
    )(p, p, p, w)


def conv_bwd(name, p, w, dz, width, n_ctx_tiles):
    t = p.shape[0]
    nt = t // ROW_TILE
    cur, prev, nxt = _conv_halo_specs(t, n_ctx_tiles)

    def body(c_ref, p_ref, n_ref, dc_ref, dp_ref, dn_ref, w_ref, du_ref, dw_ref, ext, dext):
        i = pl.program_id(1)
        _fill_ext(ext, p_ref, c_ref, n_ref, i, nt, n_ctx_tiles)
        _fill_ext(dext, dp_ref, dc_ref, dn_ref, i, nt, n_ctx_tiles)
        dzc = dc_ref[...]
        @pl.when(i == 0)
        def _():
            dw_ref[...] = jnp.zeros_like(dw_ref)

        acc = None
        for j in range(C_CONV):
            term = dext[pl.ds(SUBLANES + CONV_LEFT - j, ROW_TILE), :] * w_ref[j:j + 1, :]
            acc = term if acc is None else acc + term
            dw_ref[j:j + 1, :] += jnp.sum(dzc * ext[pl.ds(SUBLANES + j - CONV_LEFT, ROW_TILE), :], axis=0, keepdims=True)
        du_ref[...] = acc

    wspec = pl.BlockSpec((C_CONV, CONV_COLS), lambda j, i: (0, j))
    return pl.pallas_call(
        body, grid=(width // CONV_COLS, nt), name=name,
        in_specs=[cur, prev, nxt, cur, prev, nxt, wspec],
        out_specs=[cur, wspec], out_shape=[_sds((t, width), F32), _sds((C_CONV, width), F32)],
        scratch_shapes=[pltpu.VMEM((ROW_TILE + 2 * SUBLANES, CONV_COLS), F32)] * 2,
        compiler_params=_cparams(2),
    )(p, p, p, dz, dz, dz, w)


def _rms(x, w):
    return x * lax.rsqrt(jnp.mean(x * x, axis=-1, keepdims=True) + EPS) * w


def _seg_mod(mods, is_ctx):
    return jnp.where(is_ctx, mods[0], mods[1])


def f_norm_mod(shift_i, scale_i, n_ctx_tiles, passthrough=False):
    def f(pids, x, nw, mods):
        m = _seg_mod(mods, pids[0] < n_ctx_tiles)
        h = _rms(x, nw) * (1.0 + m[scale_i:scale_i + 1]) + m[shift_i:shift_i + 1]
        return (h, x) if passthrough else (h,)
    return f


def f_gate_res(gate_i, n_ctx_tiles):
    def f(pids, x, y, nw, mods):
        m = _seg_mod(mods, pids[0] < n_ctx_tiles)
        return (x + m[gate_i:gate_i + 1] * _rms(y, nw),)
    return f


def f_swiglu(pids, gu):
    half = gu.shape[1] // 2
    return (jax.nn.silu(gu[:, :half]) * gu[:, half:],)


def f_readout(n_heads):
    def f(pids, o_a, o_b, gate, nw):
        cols = [slice(j * HEAD, (j + 1) * HEAD) for j in range(n_heads)]
        outs = _each(lambda cs: _rms(o_a[:, cs] + o_b[:, cs], nw) * jax.nn.silu(gate[:, cs]), cols)
        return (jnp.concatenate(outs, axis=1) if n_heads > 1 else outs[0],)
    return f


def f_pool(pids, u, pmat, pw, scale):
    d = mm(pmat, u, 1, 0, True) - u
    return (mm(d, pw) * scale,)


def f_lb(layer):
    def f(pids, *slots):
        top = slots[0]
        for s in slots[1:]:
            top = jnp.maximum(top, s)
        ex = [jnp.exp(s - top) for s in slots]
        tot = ex[0]
        for e in ex[1:]:
            tot = tot + e
        part = ex[0]
        for e in ex[1:layer + 1]:
            part = part + e
        return (part / tot,)
    return f


def f_ada(pids, c_all, c_ctx, w, b):
    c16 = jnp.concatenate([c_all, jnp.broadcast_to(c_ctx, c_all.shape)], axis=0)
    return (mm(jax.nn.silu(c16), w) + b,)


def _each(fn, *lists):
    return [fn(*xs) for xs in zip(*lists)]


def _hgrn2_heads(sts, qrs, frs, irs, lbs, rev):
    c = A_CHUNK
    mid = c - c // 2 if rev else c // 2 - 1
    ri, ci = _iota2(c, c, 0), _iota2(c, c, 1)
    incl = (ri <= ci) if rev else (ri >= ci)
    incl_f = incl.astype(F32)
    qs = _each(jax.nn.silu, qrs)
    log_fs = _each(lambda lb, fr: jnp.log(lb + (1.0 - lb) * jax.nn.sigmoid(fr)), lbs, frs)
    ks = _each(lambda lb, fr: (1.0 - lb) * jax.nn.sigmoid(-fr), lbs, frs)
    bs = _each(lambda lf: mm(incl_f, lf, 1, 0, True), log_fs)
    b_lasts = _each(lambda lf: jnp.sum(lf, axis=0, keepdims=True), log_fs)
    scores = _each(lambda q, k, b: mm(q * jnp.exp(b - b[mid:mid + 1]), k * jnp.exp(b[mid:mid + 1] - b), 1, 1), qs, ks, bs)
    intra = _each(lambda sc, ir: mm(jnp.where(incl, sc, 0.0), ir), scores, irs)
    inter = _each(lambda q, b, st: mm(q * jnp.exp(b), st, 1, 1), qs, bs, sts)
    upd = _each(lambda ir, k, bl, b: mm(ir, k * jnp.exp(bl - b), 0, 0), irs, ks, b_lasts, bs)
    new = _each(lambda st, bl, u: st * jnp.exp(bl) + u, sts, b_lasts, upd)
    return new, _each(jnp.add, intra, inter)


def f_hgrn2(rev, hb):
    def f(states, qr, fr, ir, lb):
        cols = [slice(j * HEAD, (j + 1) * HEAD) for j in range(hb)]
        new, outs = _hgrn2_heads(list(states), [qr[:, cs] for cs in cols], [fr[:, cs] for cs in cols],
                                 [ir[:, cs] for cs in cols], [lb[j] for j in range(hb)], rev)
        return tuple(new), (jnp.concatenate(outs, axis=1) if hb > 1 else outs[0],)
    return f


def _neumann_inv(a_lows):
    n = a_lows[0].shape[0]
    eye = (_iota2(n, n, 0) == _iota2(n, n, 1)).astype(F32)
    ps = _each(lambda a: -a, a_lows)
    xs = _each(lambda p: eye + p, ps)
    k = 2
    while k < n:
        ps = _each(lambda p: mm(p, p, 1, 0, True), ps)
        xs = _each(lambda x, p: x + mm(x, p, 1, 0, True), xs, ps)
        k *= 2
    return tuple(xs)


@jax.custom_vjp
def unit_tri_inv(a_lows):
    return _neumann_inv(a_lows)


def _uti_fwd(a_lows):
    xs = _neumann_inv(a_lows)
    return xs, xs


def _uti_bwd(xs, gs):
    ts = _each(lambda x, g: mm(x, g, 0, 0, True), xs, gs)
    return (tuple(_each(lambda t, x: -mm(t, x, 1, 1, True), ts, xs)),)


unit_tri_inv.defvjp(_uti_fwd, _uti_bwd)


@jax.custom_vjp
def unit_tri_inv_saved(a_lows, xs):
    return xs


def _utis_fwd(a_lows, xs):
    return xs, xs


def _utis_bwd(xs, gs):
    return _uti_bwd(xs, gs) + (tuple(jnp.zeros_like(x) for x in xs),)


unit_tri_inv_saved.defvjp(_utis_fwd, _utis_bwd)


def _l2n(x):
    return x * lax.rsqrt(jnp.sum(x * x, axis=-1, keepdims=True) + EPS)


def _gdn_heads(ss, qs, ks, vs, a_rows, b_rows, alogs, dtbs, rev, xs_saved=None):
    c = qs[0].shape[0]
    ri, ci = _iota2(c, c, 0), _iota2(c, c, 1)
    causal = (ri <= ci) if rev else (ri >= ci)
    causal_t = (ri >= ci) if rev else (ri <= ci)
    strict = (ri < ci) if rev else (ri > ci)
    eye = ri == ci
    sq = lambda row: jnp.broadcast_to(row, (c, c))
    to_col = lambda row: jnp.sum(jnp.where(eye, sq(row), 0.0), axis=1, keepdims=True)
    g_rows = _each(lambda al, a, dt: -jnp.exp(al) * jax.nn.softplus(a + dt), alogs, a_rows, dtbs)
    beta_cols = _each(lambda b: to_col(jax.nn.sigmoid(b)), b_rows)
    g_cols = _each(to_col, g_rows)
    gc_cols = _each(lambda g: jnp.sum(jnp.where(causal, sq(g), 0.0), axis=1, keepdims=True), g_rows)
    gc_rows = _each(lambda g: jnp.sum(jnp.where(causal_t, sq(g), 0.0), axis=0, keepdims=True), g_cols)
    gc_lasts = _each(lambda g: jnp.sum(g, axis=1, keepdims=True), g_rows)
    decays = _each(lambda gc, gr: jnp.where(causal, jnp.exp(jnp.where(causal, gc - gr, 0.0)), 0.0), gc_cols, gc_rows)
    k_betas = _each(jnp.multiply, ks, beta_cols)
    v_betas = _each(jnp.multiply, vs, beta_cols)
    a_lows = _each(lambda kb, k, dec: jnp.where(strict, mm(kb, k, 1, 1) * dec, 0.0), k_betas, ks, decays)
    xs = unit_tri_inv(tuple(a_lows)) if xs_saved is None else unit_tri_inv_saved(tuple(a_lows), tuple(xs_saved))
    egcs = _each(jnp.exp, gc_cols)
    us = _each(lambda x, vb: mm(x, vb, 1, 0, True), xs, v_betas)
    ws = _each(lambda x, kb, e: mm(x, kb * e, 1, 0, True), xs, k_betas, egcs)
    qks = _each(lambda q, k, dec: jnp.where(causal, mm(q, k, 1, 1) * dec, 0.0), qs, ks, decays)
    v_news = _each(lambda u, w, s: u - mm(w, s), us, ws, ss)
    o_states = _each(lambda q, e, s: mm(q * e, s), qs, egcs, ss)
    o_locals = _each(mm, qks, v_news)
    upds = _each(lambda k, gl, gc, vn: mm(k * jnp.exp(gl - gc), vn, 0, 0), ks, gc_lasts, gc_cols, v_news)
    new = _each(lambda s, gl, u: s * jnp.exp(gl) + u, ss, gc_lasts, upds)
    return new, _each(jnp.add, o_states, o_locals), xs


def f_gdn(rev, khb, saved_inverse):
    def f(states, qr, kr, vr, a3, b3, alog3, dtb3, xcat=None):
        heads = [(j, r) for j in range(khb) for r in range(2)]
        cols = [slice(j * HEAD, (j + 1) * HEAD) for j in range(khb)]
        c = qr.shape[0]
        qk_ = _each(lambda cs: (_l2n(jax.nn.silu(qr[:, cs])) * (HEAD ** -0.5), _l2n(jax.nn.silu(kr[:, cs]))), cols)
        vs = [jax.nn.silu(vr[:, (2 * j + r) * HEAD:(2 * j + r + 1) * HEAD]) for j, r in heads]
        row = lambda arr3: [arr3[j][r:r + 1] for j, r in heads]
        xs_saved = [xcat[n * c:(n + 1) * c] for n in range(len(heads))] if saved_inverse else None
        new, outs, xs = _gdn_heads(list(states), [qk_[j][0] for j, _ in heads], [qk_[j][1] for j, _ in heads], vs,
                                   row(a3), row(b3), row(alog3), row(dtb3), rev, xs_saved)
        o = jnp.concatenate(outs, axis=1)
        return tuple(new), ((o,) if saved_inverse else (o, jnp.concatenate(xs, axis=0)))
    return f


def _hbm_spec():
    return pl.BlockSpec(memory_space=pltpu.HBM)


def all_gather(name, xs):
    nt = len(xs)

    def body(*refs):
        x_refs, out_refs = refs[:nt], refs[nt:2 * nt]
        send_sems, recv_sems, local_sems = refs[2 * nt:]
        x, y, c = lax.axis_index("x"), lax.axis_index("y"), lax.axis_index("c")
        me, sibling = (x, y, c), (x, y, 1 - c)
        chips = [(1 - x, y), (x, 1 - y), (1 - x, 1 - y)]

        def slab(t, px, py, pc):
            return out_refs[t].at[4 * px + 2 * py + pc]

        def copy(t, k, block, to, src=None):
            return pltpu.make_async_remote_copy(
                src_ref=slab(t, *block) if src is None else src, dst_ref=slab(t, *block),
                send_sem=send_sems.at[7 * t + k], recv_sem=recv_sems.at[7 * t + k], device_id=to, device_id_type=MESH)

        mine, first, passed = [], [], []
        for t in range(nt):
            mine.append(pltpu.make_async_copy(x_refs[t], slab(t, *me), local_sems.at[t]))
            mine[-1].start()
            cps = [copy(t, 0, me, sibling, src=x_refs[t])]
            cps += [copy(t, 1 + j, me, (*chip, c), src=x_refs[t]) for j, chip in enumerate(chips)]
            for cp in cps:
                cp.start()
            first += cps
        for j, chip in enumerate(chips):
            for t in range(nt):
                copy(t, 1 + j, (*chip, c), me).wait_recv()
                fw = copy(t, 4 + j, (*chip, c), sibling)
                fw.start()
                passed.append(fw)
        for t in range(nt):
            copy(t, 0, sibling, me).wait_recv()
            for j, chip in enumerate(chips):
                copy(t, 4 + j, (*chip, 1 - c), me).wait_recv()
        for cp in first + passed:
            cp.wait_send()
        for cp in mine:
            cp.wait()

    return pl.pallas_call(
        body, name=name, out_shape=[_sds((N_DEV,) + a.shape, a.dtype) for a in xs],
        in_specs=[_hbm_spec()] * nt, out_specs=[_hbm_spec()] * nt,
        scratch_shapes=[pltpu.SemaphoreType.DMA((7 * nt,)), pltpu.SemaphoreType.DMA((7 * nt,)),
                        pltpu.SemaphoreType.DMA((nt,))],
    )(*xs)


def _peers(x, y, c):
    out = []
    for k in range(1, N_DEV):
        px = 1 - x if k & 4 else x
        py = 1 - y if k & 2 else y
        pc = 1 - c if k & 1 else c
        out.append((px, py, pc))
    return out


def _exchange_copies(src_refs, land_refs, send_sems, recv_sems, scatter):
    x, y, c = lax.axis_index("x"), lax.axis_index("y"), lax.axis_index("c")
    me = 4 * x + 2 * y + c
    sends, recvs = [], []
    for t, (src, land) in enumerate(zip(src_refs, land_refs)):
        for k, (px, py, pc) in enumerate(_peers(x, y, c)):
            peer = 4 * px + 2 * py + pc
            sem = dict(send_sem=send_sems.at[7 * t + k], recv_sem=recv_sems.at[7 * t + k],
                       device_id=(px, py, pc), device_id_type=MESH)
            src_k = src.at[peer] if scatter else src
            sends.append(pltpu.make_async_remote_copy(src_ref=src_k, dst_ref=land.at[me], **sem))
            recvs.append(pltpu.make_async_remote_copy(src_ref=src_k, dst_ref=land.at[peer], **sem))
    return sends, recvs


def exchange_start(name, srcs, scatter):
    nt = len(srcs)
    lands = [lax.empty(s.shape if scatter else (N_DEV,) + s.shape, s.dtype) for s in srcs]

    def body(*refs):
        src_refs, land_refs = refs[:nt], refs[nt:2 * nt]
        send_sems, recv_sems = refs[2 * nt], refs[2 * nt + 1]
        token = refs[-1]
        sends, _ = _exchange_copies(src_refs, land_refs, send_sems, recv_sems, scatter)
        for cp in sends:
            cp.start()
        token[...] = jnp.zeros_like(token)

    hbm = lambda a: pltpu.HBM(a.shape, a.dtype)
    res = pl.pallas_call(
        body, name=name,
        out_shape=(pltpu.SemaphoreType.DMA((7 * nt,)), pltpu.SemaphoreType.DMA((7 * nt,)),
                   *[hbm(a) for a in srcs], *[hbm(a) for a in lands], _sds((SUBLANES, LANES), F32)),
        in_specs=[_hbm_spec()] * (2 * nt),
        out_specs=(pl.BlockSpec(memory_space=pltpu.SEMAPHORE), pl.BlockSpec(memory_space=pltpu.SEMAPHORE),
                   *[_hbm_spec()] * (2 * nt), pl.BlockSpec(memory_space=pltpu.VMEM)),
        input_output_aliases={i: 2 + i for i in range(2 * nt)},
        compiler_params=pltpu.CompilerParams(has_side_effects=pltpu.SideEffectType.DATAFLOW_SIDE_EFFECTING),
    )(*[pltpu.with_memory_space_constraint(a, pltpu.HBM) for a in srcs],
      *[pltpu.with_memory_space_constraint(a, pltpu.HBM) for a in lands])
    return res[0], res[1], list(res[2:2 + nt]), list(res[2 + nt:2 + 2 * nt]), res[-1]


def exchange_wait(name, started, after, scatter):
    send_sems, recv_sems, srcs, lands, _ = started
    nt = len(srcs)

    def body(*refs):
        src_refs, land_refs = refs[:nt], refs[nt:2 * nt]
        sends, recvs = _exchange_copies(src_refs, land_refs, refs[2 * nt], refs[2 * nt + 1], scatter)
        for cp in sends:
            cp.wait_send()
        for cp in recvs:
            cp.wait_recv()

    hbm = lambda a: pltpu.HBM(a.shape, a.dtype)
    sem = pl.BlockSpec(memory_space=pltpu.SEMAPHORE)
    res = pl.pallas_call(
        body, name=name,
        out_shape=(*[hbm(a) for a in srcs], *[hbm(a) for a in lands]),
        in_specs=[_hbm_spec()] * (2 * nt) + [sem, sem, pl.BlockSpec(memory_space=pl.ANY)],
        out_specs=tuple([_hbm_spec()] * (2 * nt)),
        input_output_aliases={i: i for i in range(2 * nt)},
        compiler_params=pltpu.CompilerParams(has_side_effects=pltpu.SideEffectType.DATAFLOW_SIDE_EFFECTING),
    )(*srcs, *lands, send_sems, recv_sems, after)
    return list(res[:nt]), list(res[nt:])


def sibling_exchange(name, gs):
    nt = len(gs)

    def body(*refs):
        g_refs, l_refs, send_sems, recv_sems = refs[:nt], refs[nt:2 * nt], refs[2 * nt], refs[2 * nt + 1]
        x, y, c = lax.axis_index("x"), lax.axis_index("y"), lax.axis_index("c")
        cps = []
        for t in range(nt):
            for q in range(4):
                cps.append(pltpu.make_async_remote_copy(
                    src_ref=g_refs[t].at[2 * q + (1 - c)], dst_ref=l_refs[t].at[q], send_sem=send_sems.at[4 * t + q],
                    recv_sem=recv_sems.at[4 * t + q], device_id=(x, y, 1 - c), device_id_type=MESH))
        for cp in cps:
            cp.start()
        for cp in cps:
            cp.wait()

    return pl.pallas_call(
        body, name=name, out_shape=[_sds((4,) + g.shape[1:], g.dtype) for g in gs],
        in_specs=[_hbm_spec()] * nt, out_specs=[_hbm_spec()] * nt,
        scratch_shapes=[pltpu.SemaphoreType.DMA((4 * nt,)), pltpu.SemaphoreType.DMA((4 * nt,))],
    )(*gs)


def chip_exchange(name, hs):
    nt = len(hs)

    def body(*refs):
        h_refs, r_refs = refs[:nt], refs[nt:2 * nt]
        send_sems, recv_sems, local_sems = refs[2 * nt:]
        x, y, c = lax.axis_index("x"), lax.axis_index("y"), lax.axis_index("c")
        my = 2 * x + y
        chips = [(1 - x, y), (x, 1 - y), (1 - x, 1 - y)]

        def copy(t, k, src_slot, dst_slot, to):
            return pltpu.make_async_remote_copy(
                src_ref=h_refs[t].at[src_slot], dst_ref=r_refs[t].at[dst_slot], send_sem=send_sems.at[3 * t + k],
                recv_sem=recv_sems.at[3 * t + k], device_id=(*to, c), device_id_type=MESH)

        mine, sends = [], []
        for t in range(nt):
            mine.append(pltpu.make_async_copy(h_refs[t].at[my], r_refs[t].at[my], local_sems.at[t]))
            mine[-1].start()
            for k, (qx, qy) in enumerate(chips):
                sends.append(copy(t, k, 2 * qx + qy, my, (qx, qy)))
                sends[-1].start()
        for t in range(nt):
            for k, (qx, qy) in enumerate(chips):
                copy(t, k, my, 2 * qx + qy, (qx, qy)).wait_recv()
        for cp in sends:
            cp.wait_send()
        for cp in mine:
            cp.wait()

    return pl.pallas_call(
        body, name=name, out_shape=[_sds(h.shape, h.dtype) for h in hs],
        in_specs=[_hbm_spec()] * nt, out_specs=[_hbm_spec()] * nt,
        scratch_shapes=[pltpu.SemaphoreType.DMA((3 * nt,)), pltpu.SemaphoreType.DMA((3 * nt,)),
                        pltpu.SemaphoreType.DMA((nt,))],
    )(*hs)


def _pack(arrs, dtype, width, row_mult, lead=0):
    ld = arrs[0].shape[:lead]
    flat = jnp.concatenate([a.reshape(ld + (-1,)).astype(dtype) for a in arrs], axis=-1)
    n = flat.shape[-1]
    q = width * row_mult
    npad = -(-n // q) * q
    flat = jnp.pad(flat, [(0, 0)] * lead + [(0, npad - n)])
    return flat.reshape(ld + (npad // width, width))


def _unpack(buf, shapes, lead=0):
    ld = buf.shape[:lead]
    flat = buf.reshape(ld + (-1,))
    out, off = [], 0
    for s in shapes:
        n = int(np.prod(s))
        out.append(flat[..., off:off + n].reshape(ld + tuple(s)))
        off += n
    return out


def _pool_mats(seg_len):
    mats = np.zeros((len(POOL_WINDOWS), ROW_TILE, ROW_TILE), np.float32)
    for gi, win in enumerate(POOL_WINDOWS):
        for p in range(ROW_TILE):
            base = (p // seg_len) * seg_len
            q = p - base
            lo = min(max(q - win // 2, 0), seg_len - 1)
            hi = min(max(q + win - 1 - win // 2, 0), seg_len - 1)
            mats[gi, p, base + lo:base + hi + 1] = 1.0 / (hi - lo + 1)
    return mats


def kernel(x, c, ctx, c_ctx, w_ada, b_ada, norm_w, ev_w_in, ev_lb, ev_a_norm, ev_pool_w, ev_pool_scale, ev_w_out, od_w_in, od_conv, od_A_log, od_dt_bias, od_norm, od_w_out, ffn_w13, ffn_w2, loss_target, m_c_ctx, m_w_ada, m_b_ada, m_norm_w, m_ev_w_in, m_ev_lb, m_ev_a_norm, m_ev_pool_w, m_ev_pool_scale, m_ev_w_out, m_od_w_in, m_od_conv, m_od_A_log, m_od_dt_bias, m_od_norm, m_od_w_out, m_ffn_w13, m_ffn_w2, v_c_ctx, v_w_ada, v_b_ada, v_norm_w, v_ev_w_in, v_ev_lb, v_ev_a_norm, v_ev_pool_w, v_ev_pool_scale, v_ev_w_out, v_od_w_in, v_od_conv, v_od_A_log, v_od_dt_bias, v_od_norm, v_od_w_out, v_ffn_w13, v_ffn_w2):
    names = ["c_ctx", "w_ada", "b_ada", "norm_w", "ev_w_in", "ev_lb", "ev_a_norm", "ev_pool_w", "ev_pool_scale",
             "ev_w_out", "od_w_in", "od_conv", "od_A_log", "od_dt_bias", "od_norm", "od_w_out", "ffn_w13", "ffn_w2"]
    wts = dict(zip(names, [c_ctx, w_ada, b_ada, norm_w, ev_w_in, ev_lb, ev_a_norm, ev_pool_w, ev_pool_scale,
                           ev_w_out, od_w_in, od_conv, od_A_log, od_dt_bias, od_norm, od_w_out, ffn_w13, ffn_w2]))
    mom1 = dict(zip(names, [m_c_ctx, m_w_ada, m_b_ada, m_norm_w, m_ev_w_in, m_ev_lb, m_ev_a_norm, m_ev_pool_w,
                            m_ev_pool_scale, m_ev_w_out, m_od_w_in, m_od_conv, m_od_A_log, m_od_dt_bias, m_od_norm,
                            m_od_w_out, m_ffn_w13, m_ffn_w2]))
    mom2 = dict(zip(names, [v_c_ctx, v_w_ada, v_b_ada, v_norm_w, v_ev_w_in, v_ev_lb, v_ev_a_norm, v_ev_pool_w,
                            v_ev_pool_scale, v_ev_w_out, v_od_w_in, v_od_conv, v_od_A_log, v_od_dt_bias, v_od_norm,
                            v_od_w_out, v_ffn_w13, v_ffn_w2]))

    ax, ay, ac = lax.axis_index("x"), lax.axis_index("y"), lax.axis_index("c")
    me = 4 * ax + 2 * ay + ac
    my_chip = 2 * ax + ay

    seq, d = x.shape[1], x.shape[2]
    n_ctx = ctx.shape[1]
    t = n_ctx + seq
    nt = t // ROW_TILE
    nct = n_ctx // ROW_TILE
    assert n_ctx == ROW_TILE and seq % ROW_TILE == 0 and ROW_TILE % GRID_W == 0
    depth = w_ada.shape[0]
    aw = d // 2
    n_ah = aw // HEAD
    n_grp = len(POOL_WINDOWS)
    dg = aw // n_grp
    assert dg % LANES == 0
    n_kh = d // HEAD
    kw, vw = n_kh * HEAD, 2 * n_kh * HEAD
    n_gate = 8 * n_kh
    ffn_h = ffn_w2.shape[1] * N_DEV
    ada_loc = w_ada.shape[2]
    assert depth == 2

    bf = lambda w_: w_.astype(BF16)
    ag_ffn0 = exchange_start("ag_start_ffn0", [bf(ffn_w13[0]), bf(ffn_w2[0])], False)
    ag_l1 = exchange_start("ag_start_l1", [bf(od_w_in[0]), bf(od_w_out[0]), bf(ffn_w13[1]), bf(ffn_w2[1])], False)
    after_starts = ag_ffn0[4][0, 0] + ag_l1[4][0, 0]

    def gathered(started, after, name):
        srcs, lands = exchange_wait(name, started, after, False)
        return [lax.dynamic_update_index_in_dim(l_, s_, me, 0) for l_, s_ in zip(lands, srcs)]

    small_shapes = [(d,), norm_w.shape, ev_lb.shape, ev_pool_w.shape[1:], od_conv.shape[1:]]
    (g1,) = all_gather("ag_small", [_pack([c[0] + after_starts, norm_w, ev_lb, ev_pool_w[0], od_conv[0]], F32, LANES,
                                          SUBLANES)])
    c_all, nw_g, lb_g, pw_g, cv_g = _unpack(g1, small_shapes, lead=1)
    nw_full = nw_g.transpose(1, 2, 0, 3).reshape(depth, 4, d)
    lb_full = lb_g.transpose(1, 2, 0, 3).reshape(2, depth + 1, aw)
    pw_full = pw_g.transpose(1, 0, 2, 3).reshape(n_grp, dg, dg)
    cv_full = cv_g.transpose(1, 0, 2).reshape(C_CONV, 2 * kw + vw)

    g_ev_in, g_ev_out = all_gather("ag_weights_ev", [bf(ev_w_in[0]), bf(ev_w_out[0])])

    def cols_natural(g):
        return g.transpose(1, 0, 2).reshape(g.shape[1], N_DEV * g.shape[2])

    def rows_natural(g):
        return g.reshape(N_DEV * g.shape[1], g.shape[2])

    def col_weight(g):
        return (g, True) if g.shape[2] % LANES == 0 else (cols_natural(g), False)

    w_ev_in = col_weight(g_ev_in)
    w_ev_out = rows_natural(g_ev_out)
    w13, w2 = [None] * depth, [None] * depth

    b_loc = lax.dynamic_slice_in_dim(b_ada, me * ada_loc, ada_loc, axis=1).reshape(depth, 1, ada_loc)
    ada_cb = _pick(ada_loc, TN_PREFS)
    ada_grid = (depth, ada_loc // ada_cb)
    ada_args = [
        TArg(c_all, (N_DEV, d), lambda l, j: (0, 0), "const"),
        TArg(c_ctx.reshape(1, d), (1, d), lambda l, j: (0, 0), "par", (0, 1)),
        TArg(w_ada, (None, d, ada_cb), lambda l, j: (l, 0, j)),
        TArg(b_loc, (None, 1, ada_cb), lambda l, j: (l, 0, j)),
    ]
    (m_loc,) = tile_fwd("ada_fwd", f_ada, ada_grid, ada_args,
                        [((depth, 2 * N_DEV, ada_loc), F32, (None, 2 * N_DEV, ada_cb), lambda l, j: (l, 0, j))])
    (m_all,) = all_gather("ag_mod", [m_loc])
    mods = []
    for layer in range(depth):
        lat = lax.dynamic_index_in_dim(m_all[:, layer], me, axis=1, keepdims=False).reshape(6, d)
        cxt = lax.dynamic_index_in_dim(m_all[:, layer], N_DEV + me, axis=1, keepdims=False).reshape(6, d)
        mods.append(jnp.stack([cxt, lat]))

    lb_slots = [TArg(lb_full[:, j], (2, aw), lambda i: (0, 0)) for j in range(depth + 1)]
    (lb0,) = tile_fwd("lb_fwd", f_lb(0), (1,), lb_slots, [((2, aw), F32, (2, aw), lambda i: (0, 0))])
    lb0r = lb0.reshape(2, n_ah, 1, HEAD)

    full_row = lambda i: (i, 0)
    par0 = lambda i: (0, 0)

    def nm_args(xs, layer, slot):
        return [TArg(xs, (ROW_TILE, d), full_row),
                TArg(nw_full[layer, slot].reshape(1, d), (1, d), par0, "par", (0,)),
                TArg(mods[layer], (2, 6, d), lambda i: (0, 0, 0), "par", (0,))]

    def norm_mod(name, xs, layer, slot, si, ci):
        (h,) = tile_fwd(name, f_norm_mod(si, ci, nct), (nt,), nm_args(xs, layer, slot),
                        [((t, d), BF16, (ROW_TILE, d), full_row)])
        return h

    def norm_mod_bwd(name, xs, layer, slot, si, ci, dh, carry):
        return tile_bwd(name, f_norm_mod(si, ci, nct, True), (nt,), nm_args(xs, layer, slot),
                        [(dh, (ROW_TILE, d), full_row), (carry, (ROW_TILE, d), full_row)])

    def gr_args(xs, ys, layer, slot):
        return [TArg(xs, (ROW_TILE, d), full_row, grad=False), TArg(ys, (ROW_TILE, d), full_row, gdtype=BF16),
                TArg(nw_full[layer, slot].reshape(1, d), (1, d), par0, "par", (0,)),
                TArg(mods[layer], (2, 6, d), lambda i: (0, 0, 0), "par", (0,))]

    def gate_res(name, xs, ys, layer, slot, gi):
        (o,) = tile_fwd(name, f_gate_res(gi, nct), (nt,), gr_args(xs, ys, layer, slot),
                        [((t, d), F32, (ROW_TILE, d), full_row)])
        return o

    def gate_res_bwd(name, xs, ys, layer, slot, gi, dx):
        return tile_bwd(name, f_gate_res(gi, nct), (nt,), gr_args(xs, ys, layer, slot),
                        [(dx, (ROW_TILE, d), full_row)])

    sw_rows = ROW_TILE // 2

    def sw_args(gu):
        return [TArg(gu, (sw_rows, 2 * ffn_h), full_row, gdtype=BF16)]

    def ffn_fwd(tag, xs, layer):
        h2 = norm_mod(f"nm2_{tag}", xs, layer, 2, 3, 4)
        gu = matmul(f"w13_{tag}", h2, w13[layer][0], "nn", slabs=w13[layer][1])
        (act,) = tile_fwd(f"swiglu_{tag}", f_swiglu, (t // sw_rows,), sw_args(gu),
                          [((t, ffn_h), BF16, (sw_rows, ffn_h), full_row)])
        fo = matmul(f"w2_{tag}", act, w2[layer], "nn")
        xn = gate_res(f"gr2_{tag}", xs, fo, layer, 3, 5)
        return xn, (xs, h2, gu, act, fo)

    def col_grad(name, a, dy, slabs):
        g = matmul(name, a, dy, "tn", slabs=slabs, out_dtype=BF16)
        return g if slabs else g.reshape(g.shape[0], N_DEV, g.shape[1] // N_DEV).transpose(1, 0, 2)

    def row_grad(name, a, dy):
        g = matmul(name, a, dy, "tn", out_dtype=BF16)
        return g.reshape(N_DEV, g.shape[0] // N_DEV, g.shape[1])

    def ffn_bwd(tag, saved, layer, dxn, acc):
        xs, h2, gu, act, fo = saved
        dfo, dnw3, dmod_a = gate_res_bwd(f"gr2b_{tag}", xs, fo, layer, 3, 5, dxn)
        dact = matmul(f"w2d_{tag}", dfo, w2[layer], "nt")
        dw2 = row_grad(f"w2w_{tag}", act, dfo)
        (dgu,) = tile_bwd(f"swiglub_{tag}", f_swiglu, (t // sw_rows,), sw_args(gu), [(dact, (sw_rows, ffn_h), full_row)])
        dh2 = matmul(f"w13d_{tag}", dgu, w13[layer][0], "nt", slabs=w13[layer][1])
        dw13 = col_grad(f"w13w_{tag}", h2, dgu, w13[layer][1])
        dxs, dnw2, dmod_b = norm_mod_bwd(f"nm2b_{tag}", xs, layer, 2, 3, 4, dh2, dxn)
        acc["ffn_w13"][layer] = dw13
        acc["ffn_w2"][layer] = dw2
        acc["norm_w"][layer][2] = dnw2
        acc["norm_w"][layer][3] = dnw3
        acc["mods"][layer].extend([dmod_a, dmod_b])
        return dxs

    def head_cols(width):
        return (ROW_TILE, width)

    n_a = t // A_CHUNK
    nca = n_ctx // A_CHUNK

    def a_tok(rev):
        if not rev:
            return lambda i: i
        return lambda i: jnp.where(i < nca, nca - 1 - i, n_a + nca - 1 - i)

    hb = _pick(n_ah, (HEADS_PER_STEP, 2, 1))
    n_hblk = n_ah // hb

    def hg_args(p, direction):
        tok = a_tok(direction == 1)
        blk = (A_CHUNK, hb * HEAD)
        return [TArg(p, blk, lambda h, i: (tok(i), h)),
                TArg(p, blk, lambda h, i: (tok(i), (1 + direction) * n_hblk + h)),
                TArg(p, blk, lambda h, i: (tok(i), 3 * n_hblk + h)),
                TArg(lb0r, (None, hb, 1, HEAD), lambda h, i: (direction, h, 0, 0), "par", (1,))]

    pmats = jnp.asarray(np.stack([_pool_mats(n_ctx), _pool_mats(GRID_W)]))

    def pool_args(p):
        return [TArg(p, (ROW_TILE, dg), lambda g, i: (i, 5 * n_grp + g)),
                TArg(pmats, (None, None, ROW_TILE, ROW_TILE), lambda g, i: (jnp.where(i < nct, 0, 1), g, 0, 0), "const"),
                TArg(pw_full, (None, dg, dg), lambda g, i: (g, 0, 0), "par", (1,)),
                TArg(ev_pool_scale, (1, dg), lambda g, i: (0, g), "par", (1,))]

    def ro_plan(gate_off, n_heads):
        per = _pick(n_heads, (8, 4, 2, 1))
        assert gate_off % per == 0
        return per, n_heads // per, gate_off // per

    def ro_args(o_f, o_b, gate_arr, gate_off, nw_arr, n_heads):
        per, _, goff = ro_plan(gate_off, n_heads)
        blk = (ROW_TILE, per * HEAD)
        return [TArg(o_f, blk, lambda h, i: (i, h)), TArg(o_b, blk, lambda h, i: (i, h), grad=False),
                TArg(gate_arr, blk, lambda h, i: (i, goff + h)),
                TArg(nw_arr, (1, HEAD), lambda h, i: (0, 0), "par", (0, 1))]

    def readout(name, o_f, o_b, gate_arr, gate_off, nw_arr, n_heads):
        per, nblk, _ = ro_plan(gate_off, n_heads)
        (o,) = tile_fwd(name, f_readout(per), (nblk, nt), ro_args(o_f, o_b, gate_arr, gate_off, nw_arr, n_heads),
                        [((t, n_heads * HEAD), BF16, (ROW_TILE, per * HEAD), lambda hh, i: (i, hh))])
        return o

    def readout_bwd(name, o_f, o_b, gate_arr, gate_off, nw_arr, n_heads, dout):
        per, nblk, _ = ro_plan(gate_off, n_heads)
        return tile_bwd(name, f_readout(per), (nblk, nt), ro_args(o_f, o_b, gate_arr, gate_off, nw_arr, n_heads),
                        [(dout, (ROW_TILE, per * HEAD), lambda hh, i: (i, hh))])

    def even_fwd(tag, xs, layer):
        h = norm_mod(f"nm1_{tag}", xs, layer, 0, 0, 1)
        p = matmul(f"win_{tag}", h, w_ev_in[0], "nn", slabs=w_ev_in[1])
        outs, saves = [], []
        for direction in (0, 1):
            (o,), sv = scan_fwd(f"hgrn_{tag}_{direction}", f_hgrn2(direction == 1, hb), n_hblk, n_a, hg_args(p, direction),
                                [((t, aw), F32, (A_CHUNK, hb * HEAD), lambda hh, i, tok=a_tok(direction == 1): (tok(i), hh))], hb)
            outs.append(o)
            saves.append(sv)
        a_out = readout(f"ro_{tag}", outs[0], outs[1], p, 4 * n_ah, ev_a_norm, n_ah)
        (pooled,) = tile_fwd(f"pool_{tag}", f_pool, (n_grp, nt), pool_args(p),
                             [((t, aw), BF16, (ROW_TILE, dg), lambda g, i: (i, g))])
        cat = assemble(f"cat_{tag}", [[(a_out, aw, 0)], [(pooled, aw, 0)]], BF16)
        y = matmul(f"wout_{tag}", cat, w_ev_out, "nn")
        xn = gate_res(f"gr1_{tag}", xs, y, layer, 1, 2)
        return xn, (xs, h, p, outs, saves, cat, y)

    def even_bwd(tag, saved, layer, dxn, acc):
        xs, h, p, outs, saves, cat, y = saved
        dy, dnw1, dmod_a = gate_res_bwd(f"gr1b_{tag}", xs, y, layer, 1, 2, dxn)
        dcat = matmul(f"woutd_{tag}", dy, w_ev_out, "nt")
        acc["ev_w_out"] = row_grad(f"woutw_{tag}", cat, dy)
        do, dgate, d_anorm = readout_bwd(f"rob_{tag}", outs[0], outs[1], p, 4 * n_ah, ev_a_norm, n_ah, dcat)
        du, d_pw, d_ps = tile_bwd(f"poolb_{tag}", f_pool, (n_grp, nt), pool_args(p),
                                  [(dcat, (ROW_TILE, dg), lambda g, i: (i, n_grp + g))])
        dq, df, di, dlb = [], [], [], []
        for direction in (0, 1):
            r = scan_bwd(f"hgrnb_{tag}_{direction}", f_hgrn2(direction == 1, hb), n_hblk, n_a, hg_args(p, direction),
                         saves[direction],
                         [(do, (A_CHUNK, hb * HEAD), lambda hh, i, tok=a_tok(direction == 1): (tok(i), hh))])
            dq.append(r[0])
            df.append(r[1])
            di.append(r[2])
            dlb.append(r[3])
        sec = lambda arr, s: (arr, aw, s)
        dp = assemble(f"dp_{tag}", [[sec(dq[0], 0), sec(dq[1], 0)], [sec(df[0], 1)], [sec(df[1], 2)],
                                    [sec(di[0], 3), sec(di[1], 3)], [sec(dgate, 4)], [sec(du, 5)]], BF16)
        dh = matmul(f"wind_{tag}", dp, w_ev_in[0], "nt", slabs=w_ev_in[1])
        acc["ev_w_in"] = col_grad(f"winw_{tag}", h, dp, w_ev_in[1])
        dxs, dnw0, dmod_b = norm_mod_bwd(f"nm1b_{tag}", xs, layer, 0, 0, 1, dh, dxn)
        acc["norm_w"][layer][0] = dnw0
        acc["norm_w"][layer][1] = dnw1
        acc["mods"][layer].extend([dmod_a, dmod_b])
        acc["ev_a_norm"] = d_anorm
        acc["ev_pool_w"] = d_pw
        acc["ev_pool_scale"] = d_ps
        acc["lb0"] = jnp.stack([dlb[0][0], dlb[1][1]]).reshape(2, aw)
        return dxs

    n_c = t // C_CHUNK
    ncc = n_ctx // C_CHUNK

    def c_tok(rev):
        if not rev:
            return lambda i: i
        return lambda i: jnp.where(i < ncc, ncc - 1 - i, n_c + ncc - 1 - i)

    alog = od_A_log[0].reshape(2, n_kh, 2, 1)
    dtb = od_dt_bias[0].reshape(2, n_kh, 2, 1)

    khb = _pick(n_kh, (HEADS_PER_STEP, 2, 1))
    n_kblk = n_kh // khb

    def gd_args(z, gates, direction):
        tok = c_tok(direction == 1)
        gblk = (None, khb, None, 2, C_CHUNK)
        sblk = (None, khb, 2, 1)
        return [TArg(z, (C_CHUNK, khb * HEAD), lambda kb, i: (tok(i), kb)),
                TArg(z, (C_CHUNK, khb * HEAD), lambda kb, i: (tok(i), n_kblk + kb)),
                TArg(z, (C_CHUNK, khb * 2 * HEAD), lambda kb, i: (tok(i), n_kblk + kb)),
                TArg(gates, gblk, lambda kb, i: (direction, kb, tok(i), 0, 0)),
                TArg(gates, gblk, lambda kb, i: (2 + direction, kb, tok(i), 0, 0)),
                TArg(alog, sblk, lambda kb, i: (direction, kb, 0, 0), "par", (1,)),
                TArg(dtb, sblk, lambda kb, i: (direction, kb, 0, 0), "par", (1,))]

    def odd_fwd(tag, xs, layer):
        h = norm_mod(f"nm1_{tag}", xs, layer, 0, 0, 1)
        pm = matmul(f"win_{tag}", h, w_od_main, "nn")
        pg = matmul(f"wgate_{tag}", h, w_od_gate, "nn")
        z = conv_fwd(f"conv_{tag}", pm, cv_full, 2 * kw + vw, nct)
        gates = pg.reshape(n_c, C_CHUNK, 4, n_kh, 2).transpose(2, 3, 0, 4, 1)
        outs, saves = [], []
        for direction in (0, 1):
            xrows = 2 * khb * C_CHUNK
            (o, xinv), sv = scan_fwd(
                f"gdn_{tag}_{direction}", f_gdn(direction == 1, khb, False), n_kblk, n_c, gd_args(z, gates, direction),
                [((t, vw), F32, (C_CHUNK, khb * 2 * HEAD), lambda kb, i, tok=c_tok(direction == 1): (tok(i), kb)),
                 ((n_kblk, n_c, xrows, C_CHUNK), F32, (None, None, xrows, C_CHUNK), lambda kb, i: (kb, i, 0, 0))],
                2 * khb)
            outs.append(o)
            saves.append((sv, xinv))
        n_vh = 2 * n_kh
        yo = readout(f"ro_{tag}", outs[0], outs[1], pm, 2 * n_kh + n_vh, od_norm, n_vh)
        y = matmul(f"wout_{tag}", yo, w_od_out, "nn")
        xn = gate_res(f"gr1_{tag}", xs, y, layer, 1, 2)
        return xn, (xs, h, pm, z, gates, outs, saves, yo, y)

    def odd_bwd(tag, saved, layer, dxn, acc):
        xs, h, pm, z, gates, outs, saves, yo, y = saved
        n_vh = 2 * n_kh
        dy, dnw1, dmod_a = gate_res_bwd(f"gr1b_{tag}", xs, y, layer, 1, 2, dxn)
        dyo = matmul(f"woutd_{tag}", dy, w_od_out, "nt")
        acc["od_w_out"] = row_grad(f"woutw_{tag}", yo, dy)
        do, dzg, d_onorm = readout_bwd(f"rob_{tag}", outs[0], outs[1], pm, 2 * n_kh + n_vh, od_norm, n_vh, dyo)
        dq, dk, dv, dga, dgb, dal, ddt = [], [], [], [], [], [], []
        for direction in (0, 1):
            sv, xinv = saves[direction]
            xarg = TArg(xinv, (None, None, 2 * khb * C_CHUNK, C_CHUNK), lambda kb, i: (kb, i, 0, 0), "const")
            r = scan_bwd(f"gdnb_{tag}_{direction}", f_gdn(direction == 1, khb, True), n_kblk, n_c,
                         gd_args(z, gates, direction) + [xarg], sv,
                         [(do, (C_CHUNK, khb * 2 * HEAD), lambda kb, i, tok=c_tok(direction == 1): (tok(i), kb))])
            for lst, v_ in zip((dq, dk, dv, dga, dgb, dal, ddt), r):
                lst.append(v_)
        dz = assemble(f"dz_{tag}", [[(dq[0], kw, 0), (dq[1], kw, 0)], [(dk[0], kw, 1), (dk[1], kw, 1)],
                                    [(dv[0], vw, 1), (dv[1], vw, 1)]], F32)
        du, d_conv = conv_bwd(f"convb_{tag}", pm, cv_full, dz, 2 * kw + vw, nct)
        dpm = assemble(f"dpm_{tag}", [[(du, 2 * kw + vw, 0)], [(dzg, vw, 2)]], BF16)
        dgates = jnp.stack([dga[0][0], dga[1][1], dgb[0][2], dgb[1][3]])
        dpg = dgates.transpose(2, 4, 0, 1, 3).reshape(t, n_gate).astype(BF16)
        dh = matmul(f"wgated_{tag}", dpg, w_od_gate, "nt")
        dh = matmul(f"wind_{tag}", dpm, w_od_main, "nt", add=dh)
        dw_in = jnp.concatenate([matmul(f"winw_{tag}", h, dpm, "tn", out_dtype=BF16),
                                 matmul(f"wgatew_{tag}", h, dpg, "tn", out_dtype=BF16)], axis=1)
        acc["od_w_in"] = dw_in.reshape(d, N_DEV, dw_in.shape[1] // N_DEV).transpose(1, 0, 2)
        dxs, dnw0, dmod_b = norm_mod_bwd(f"nm1b_{tag}", xs, layer, 0, 0, 1, dh, dxn)
        acc["norm_w"][layer][0] = dnw0
        acc["norm_w"][layer][1] = dnw1
        acc["mods"][layer].extend([dmod_a, dmod_b])
        acc["od_norm"] = d_onorm
        acc["od_conv"] = d_conv
        acc["od_A_log"] = jnp.stack([dal[0][0], dal[1][1]]).reshape(1, 2, n_vh)
        acc["od_dt_bias"] = jnp.stack([ddt[0][0], ddt[1][1]]).reshape(1, 2, n_vh)
        return dxs

    xs0 = jnp.concatenate([ctx[0], x[0]], axis=0)
    xs1, sv_e = even_fwd("l0", xs0, 0)
    g_w13a, g_w2a = gathered(ag_ffn0, xs1, "ag_wait_ffn0")
    w13[0], w2[0] = col_weight(g_w13a), rows_natural(g_w2a)
    xs2, sv_f0 = ffn_fwd("l0", xs1, 0)
    g_od_in, g_od_out, g_w13b, g_w2b = gathered(ag_l1, xs2, "ag_wait_l1")
    w_od_in = cols_natural(g_od_in)
    w_od_main, w_od_gate = w_od_in[:, :2 * kw + 2 * vw], w_od_in[:, 2 * kw + 2 * vw:]
    w_od_out = rows_natural(g_od_out)
    w13[1], w2[1] = col_weight(g_w13b), rows_natural(g_w2b)
    xs3, sv_o = odd_fwd("l1", xs2, 1)
    xs4, sv_f1 = ffn_fwd("l1", xs3, 1)
    loss_loc, dxs = loss_kernel("loss", xs4, loss_target[0], nct)
    loss = lax.psum(loss_loc, ("x", "y", "c"))

    acc = {"norm_w": [[None] * 4 for _ in range(depth)], "mods": [[] for _ in range(depth)],
           "ffn_w13": [None] * depth, "ffn_w2": [None] * depth}
    dxs = ffn_bwd("l1", sv_f1, 1, dxs, acc)
    rs_ffn1 = exchange_start("rs_start_ffn1", [acc["ffn_w13"][1], acc["ffn_w2"][1]], True)
    mods[1] = mods[1] + rs_ffn1[4][0, 0]
    dxs = odd_bwd("l1", sv_o, 1, dxs, acc)
    rs_od = exchange_start("rs_start_od", [acc["od_w_in"], acc["od_w_out"]], True)
    mods[0] = mods[0] + rs_od[4][0, 0]
    dxs = ffn_bwd("l0", sv_f0, 0, dxs, acc)
    rs_ffn0 = exchange_start("rs_start_ffn0", [acc["ffn_w13"][0], acc["ffn_w2"][0]], True)
    mods[0] = mods[0] + rs_ffn0[4][0, 0]
    dxs = even_bwd("l0", sv_e, 0, dxs, acc)
    rs_ev = exchange_start("rs_start_ev", [acc["ev_w_in"], acc["ev_w_out"]], True)
    grad_x = dxs[n_ctx:].reshape(1, seq, d)

    (d_lb_slots) = tile_bwd("lb_bwd", f_lb(0), (1,), lb_slots, [(acc["lb0"], (2, aw), lambda i: (0, 0))])
    d_ev_lb = jnp.stack(d_lb_slots, axis=1)

    dmods = jnp.stack([functools.reduce(jnp.add, acc["mods"][layer]) for layer in range(depth)])
    (dm_all,) = all_gather("ag_dmod", [dmods.reshape(depth * 2 * 6, d)])
    dm_all = dm_all.reshape(N_DEV, depth, 2, 6 * d)
    dm_cols = lax.dynamic_slice_in_dim(dm_all, me * ada_loc, ada_loc, axis=3)
    dm_loc = jnp.concatenate([dm_cols[:, :, 1].transpose(1, 0, 2), dm_cols[:, :, 0].transpose(1, 0, 2)], axis=1)
    d_cctx_part, d_w_ada, d_b_loc = tile_bwd("ada_bwd", f_ada, ada_grid, ada_args,
                                             [(dm_loc, (None, 2 * N_DEV, ada_cb), lambda l, j: (l, 0, j))])

    d_b_full = lax.dynamic_update_slice_in_dim(jnp.zeros_like(b_ada), d_b_loc.reshape(depth, ada_loc), me * ada_loc, axis=1)
    d_nw = jnp.stack([jnp.stack([acc["norm_w"][layer][s].reshape(d) for s in range(4)]) for layer in range(depth)])
    small_grads = [d_cctx_part.reshape(d), d_b_full, d_nw, d_ev_lb, acc["ev_a_norm"], acc["ev_pool_w"],
                   acc["ev_pool_scale"], acc["od_conv"], acc["od_A_log"], acc["od_dt_bias"], acc["od_norm"]]
    sg_shapes = [a.shape for a in small_grads]
    (sg,) = all_gather("ag_smallgrads", [_pack(small_grads, F32, FLAT_W, SUBLANES)])
    sg_sum = sum_leading("sum_smallgrads", sg, F32)
    (g_cctx, g_bada, g_nw, g_lb, g_anorm, g_pw, g_ps, g_conv, g_alog, g_dtb, g_onorm) = _unpack(sg_sum, sg_shapes)

    def my_cols(full, axis):
        loc = full.shape[axis] // N_DEV
        return lax.dynamic_slice_in_dim(full, me * loc, loc, axis=axis)

    grads = {
        "c_ctx": g_cctx, "w_ada": d_w_ada, "b_ada": g_bada, "norm_w": my_cols(g_nw, 2), "ev_lb": my_cols(g_lb, 2),
        "ev_a_norm": g_anorm, "ev_pool_w": my_cols(g_pw, 1)[None], "ev_pool_scale": g_ps,
        "od_conv": my_cols(g_conv, 1)[None], "od_A_log": g_alog, "od_dt_bias": g_dtb, "od_norm": g_onorm,
    }

    def reduced(started, tags_, name):
        srcs, lands = exchange_wait(name, started, sg_sum, True)
        out = []
        for tg, s_, l_ in zip(tags_, srcs, lands):
            own = lax.dynamic_index_in_dim(s_, me, 0, keepdims=True)
            out.append(sum_leading(f"rs_sum_{tg}", lax.dynamic_update_slice_in_dim(l_, own, me, 0), F32))
        return out

    g_w13b, g_w2b = reduced(rs_ffn1, ["w13b", "w2b"], "rs_wait_ffn1")
    g_od_in, g_od_out = reduced(rs_od, ["od_in", "od_out"], "rs_wait_od")
    g_w13a, g_w2a = reduced(rs_ffn0, ["w13a", "w2a"], "rs_wait_ffn0")
    g_ev_in, g_ev_out = reduced(rs_ev, ["ev_in", "ev_out"], "rs_wait_ev")
    grads["ev_w_in"], grads["ev_w_out"], grads["od_w_in"], grads["od_w_out"] = (g_ev_in[None], g_ev_out[None],
                                                                                g_od_in[None], g_od_out[None])
    grads["ffn_w13"] = jnp.stack([g_w13a, g_w13b])
    grads["ffn_w2"] = jnp.stack([g_w2a, g_w2b])

    big_names = ["w_ada", "ev_w_in", "ev_w_out", "od_w_in", "od_w_out", "ffn_w13", "ffn_w2"]
    small_names = [n_ for n_ in names if n_ not in big_names]
    gl = {n_: grads[n_].reshape(wts[n_].shape) for n_ in names}
    delta, new_m, new_v = {}, {}, {}
    for n_ in big_names:
        shp = wts[n_].shape
        res = adamw(f"adamw_{n_}", _rows2d(gl[n_]), _rows2d(wts[n_]), _rows2d(mom1[n_]), _rows2d(mom2[n_]))
        delta[n_], new_m[n_], new_v[n_] = (r_.reshape(shp) for r_ in res)
    shapes = [wts[n_].shape for n_ in small_names]
    pk = lambda dct: _pack([dct[n_] for n_ in small_names], F32, FLAT_W, SUBLANES)
    res = adamw("adamw_small", pk(gl), pk(wts), pk(mom1), pk(mom2))
    for dct, r_ in zip((delta, new_m, new_v), res):
        for n_, a_ in zip(small_names, _unpack(r_, shapes)):
            dct[n_] = a_
    return (loss, grad_x, *[gl[n_] for n_ in names], *[delta[n_] for n_ in names], *[new_m[n_] for n_ in names],
            *[new_v[n_] for n_ in names])
```

```python
import functools
from typing import Any, NamedTuple

import numpy as np

import jax
import jax.numpy as jnp
from jax import lax
from jax.experimental import pallas as pl
from jax.experimental.pallas import tpu as pltpu

F32 = jnp.float32
BF16 = jnp.bfloat16
MESH = pl.DeviceIdType.MESH
N_DEV = 8

EPS = 1e-6
GRID_W = 64
HEAD = 128
A_CHUNK = 32
C_CHUNK = 64
C_CONV = 4
POOL_WINDOWS = (2, 4, 8, 16)
ADAM_LR, ADAM_B1, ADAM_B2, ADAM_EPS, ADAM_WD, ADAM_STEP = 0.001, 0.9, 0.999, 1e-08, 0.01, 10

VMEM_LIMIT_BYTES = 56 * 1024 * 1024
LANES = 128
SUBLANES = 8
ROW_TILE = 256
FLAT_W = 1024
FLAT_ROWS = 512
TM_PREFS = (1056, 768, 512, 256, 128, 64, 32, 16)
TN_PREFS = (512, 384, 1408, 256, 128)
TK_PREFS = (2048, 2816, 1408, 1024, 768, 512, 384, 256, 128)
TO_PREFS = (1024, 1408, 768, 704, 512, 384, 256, 128)
HEADS_PER_STEP = 4


def _pick(dim, prefs):
    for p in prefs:
        if p <= dim and dim % p == 0:
            return p
    return dim


def _cparams(ngrid):
    return pltpu.CompilerParams(dimension_semantics=("arbitrary",) * ngrid, vmem_limit_bytes=VMEM_LIMIT_BYTES)


def _sds(shape, dtype):
    return jax.ShapeDtypeStruct(tuple(shape), dtype)


def _split(x):
    hi = x.astype(BF16)
    return hi, (x - hi.astype(F32)).astype(BF16)


def _dot(a, b, ca, cb, hi):
    dims = (((ca,), (cb,)), ((), ()))
    dot = lambda u, v: lax.dot_general(u, v, dims, preferred_element_type=F32)
    if hi:
        (ah, al), (bh, bl) = _split(a.astype(F32)), _split(b.astype(F32))
        return dot(ah, bh) + (dot(ah, bl) + dot(al, bh))
    return dot(a.astype(BF16), b.astype(BF16))


@functools.partial(jax.custom_vjp, nondiff_argnums=(2, 3, 4))
def mm(a, b, ca=1, cb=0, hi=False):
    return _dot(a, b, ca, cb, hi)


def _mm_fwd(a, b, ca, cb, hi):
    return _dot(a, b, ca, cb, hi), (a, b)


def _mm_bwd(ca, cb, hi, res, g):
    a, b = res
    da = _dot(g, b, 1, 1 - cb, hi) if ca == 1 else _dot(b, g, 1 - cb, 1, hi)
    db = _dot(a, g, 1 - ca, 0, hi) if cb == 0 else _dot(g, a, 0, 1 - ca, hi)
    return da, db


mm.defvjp(_mm_fwd, _mm_bwd)


def _iota2(n, m, axis):
    return lax.broadcasted_iota(jnp.int32, (n, m), axis)


class TArg(NamedTuple):
    arr: Any
    block: tuple
    imap: Any
    kind: str = "row"
    acc: tuple = ()
    gdtype: Any = F32
    grad: bool = True


def _load(ref):
    v = ref[...]
    return v.astype(F32) if jnp.issubdtype(v.dtype, jnp.floating) else v


def tile_fwd(name, f, grid, args, outs):
    n_in, ng = len(args), len(grid)

    def body(*refs):
        pids = tuple(pl.program_id(k) for k in range(ng))
        res = f(pids, *[_load(r) for r in refs[:n_in]])
        for r, v in zip(refs[n_in:], res):
            r[...] = v.astype(r.dtype)

    return pl.pallas_call(
        body, grid=grid, name=name,
        in_specs=[pl.BlockSpec(a.block, a.imap) for a in args],
        out_specs=[pl.BlockSpec(b, im) for (_, _, b, im) in outs],
        out_shape=[_sds(s, d) for (s, d, _, _) in outs],
        compiler_params=_cparams(ng),
    )(*[a.arr for a in args])


def _store_grads(args, diff, pids, g_refs, d):
    for k, gr, dv in zip(diff, g_refs, d):
        a = args[k]
        if a.kind == "row" or not a.acc:
            gr[...] = dv.astype(gr.dtype)
        else:
            first = pids[a.acc[0]] == 0
            for ax in a.acc[1:]:
                first = jnp.logical_and(first, pids[ax] == 0)

            @pl.when(first)
            def _(gr=gr, dv=dv):
                gr[...] = dv.astype(gr.dtype)

            @pl.when(jnp.logical_not(first))
            def _(gr=gr, dv=dv):
                gr[...] += dv.astype(gr.dtype)


def tile_bwd(name, f, grid, args, cts):
    n_in, n_ct, ng = len(args), len(cts), len(grid)
    diff = [k for k, a in enumerate(args) if a.kind != "const" and a.grad]

    def body(*refs):
        pids = tuple(pl.program_id(k) for k in range(ng))
        vals = [_load(r) for r in refs[:n_in]]

        def g(*dv):
            full = list(vals)
            for k, v in zip(diff, dv):
                full[k] = v
            return tuple(f(pids, *full))

        _, vjp = jax.vjp(g, *[vals[k] for k in diff])
        d = vjp(tuple(_load(r) for r in refs[n_in:n_in + n_ct]))
        _store_grads(args, diff, pids, refs[n_in + n_ct:], d)

    return pl.pallas_call(
        body, grid=grid, name=name,
        in_specs=[pl.BlockSpec(a.block, a.imap) for a in args] + [pl.BlockSpec(b, im) for (_, b, im) in cts],
        out_specs=[pl.BlockSpec(args[k].block, args[k].imap) for k in diff],
        out_shape=[_sds(args[k].arr.shape, args[k].gdtype) for k in diff],
        compiler_params=_cparams(ng),
    )(*[a.arr for a in args], *[c[0] for c in cts])


def scan_fwd(name, f, n_heads, n_steps, args, outs, n_state):
    n_in, n_out = len(args), len(outs)
    sblock = (None, None, HEAD, HEAD)

    def body(*refs):
        in_refs = refs[:n_in]
        out_refs = refs[n_in:n_in + n_out]
        save_refs = refs[n_in + n_out:n_in + n_out + n_state]
        s_refs = refs[n_in + n_out + n_state:]

        @pl.when(pl.program_id(1) == 0)
        def _():
            for s in s_refs:
                s[...] = jnp.zeros_like(s)

        states = tuple(s[...] for s in s_refs)
        for sv, s in zip(save_refs, states):
            sv[...] = s
        new_states, res = f(states, *[_load(r) for r in in_refs])
        for s, v in zip(s_refs, new_states):
            s[...] = v
        for r, v in zip(out_refs, res):
            r[...] = v.astype(r.dtype)

    res = pl.pallas_call(
        body, grid=(n_heads, n_steps), name=name,
        in_specs=[pl.BlockSpec(a.block, a.imap) for a in args],
        out_specs=[pl.BlockSpec(b, im) for (_, _, b, im) in outs]
        + [pl.BlockSpec(sblock, lambda h, i: (h, i, 0, 0))] * n_state,
        out_shape=[_sds(s, d) for (s, d, _, _) in outs] + [_sds((n_heads, n_steps, HEAD, HEAD), F32)] * n_state,
        scratch_shapes=[pltpu.VMEM((HEAD, HEAD), F32)] * n_state,
        compiler_params=_cparams(2),
    )(*[a.arr for a in args])
    return res[:n_out], res[n_out:]


def scan_bwd(name, f, n_heads, n_steps, args, saves, cts):
    n_in, n_ct, n_state = len(args), len(cts), len(saves)
    diff = [k for k, a in enumerate(args) if a.kind != "const" and a.grad]
    sblock = (None, None, HEAD, HEAD)

    def rv(im):
        return lambda h, i: im(h, n_steps - 1 - i)

    def body(*refs):
        in_refs = refs[:n_in]
        save_refs = refs[n_in:n_in + n_state]
        ct_refs = refs[n_in + n_state:n_in + n_state + n_ct]
        g_refs = refs[n_in + n_state + n_ct:n_in + n_state + n_ct + len(diff)]
        ds_refs = refs[n_in + n_state + n_ct + len(diff):]
        pids = (pl.program_id(0), pl.program_id(1))

        @pl.when(pids[1] == 0)
        def _():
            for s in ds_refs:
                s[...] = jnp.zeros_like(s)

        vals = [_load(r) for r in in_refs]

        def g(states, *dv):
            full = list(vals)
            for k, v in zip(diff, dv):
                full[k] = v
            new_states, res = f(states, *full)
            return tuple(new_states), tuple(res)

        _, vjp = jax.vjp(g, tuple(s[...] for s in save_refs), *[vals[k] for k in diff])
        d = vjp((tuple(s[...] for s in ds_refs), tuple(_load(r) for r in ct_refs)))
        for s, v in zip(ds_refs, d[0]):
            s[...] = v
        _store_grads(args, diff, pids, g_refs, d[1:])

    return pl.pallas_call(
        body, grid=(n_heads, n_steps), name=name,
        in_specs=[pl.BlockSpec(a.block, rv(a.imap)) for a in args]
        + [pl.BlockSpec(sblock, rv(lambda h, i: (h, i, 0, 0)))] * n_state
        + [pl.BlockSpec(b, rv(im)) for (_, b, im) in cts],
        out_specs=[pl.BlockSpec(args[k].block, rv(args[k].imap)) for k in diff],
        out_shape=[_sds(args[k].arr.shape, args[k].gdtype) for k in diff],
        scratch_shapes=[pltpu.VMEM((HEAD, HEAD), F32)] * n_state,
        compiler_params=_cparams(2),
    )(*[a.arr for a in args], *saves, *[c[0] for c in cts])


def matmul(name, a, b, mode, add=None, out_dtype=F32, slabs=False):
    o_spec = None
    if mode == "nn":
        m, k = a.shape
        ns = b.shape[2] if slabs else b.shape[1]
        n = N_DEV * ns if slabs else ns
        to_m, to_n, tr = _pick(m, TM_PREFS), _pick(ns, TN_PREFS), _pick(k, TK_PREFS)
        nb = ns // to_n
        grid = (m // to_m, n // to_n, k // tr)
        a_spec = pl.BlockSpec((to_m, tr), lambda i, j, l: (i, l))
        if slabs:
            b_spec = pl.BlockSpec((None, tr, to_n), lambda i, j, l: (j // nb, l, j % nb))
        else:
            b_spec = pl.BlockSpec((tr, to_n), lambda i, j, l: (l, j))
        dims, oshape = (1, 0), (m, n)
    elif mode == "nt":
        m, n = a.shape
        k = b.shape[1] if slabs else b.shape[0]
        ns = n // N_DEV if slabs else n
        to_m, to_n, tr = _pick(m, TM_PREFS), _pick(k, TO_PREFS), _pick(ns, TK_PREFS)
        nb = ns // tr
        grid = (m // to_m, k // to_n, n // tr)
        a_spec = pl.BlockSpec((to_m, tr), lambda i, j, l: (i, l))
        if slabs:
            b_spec = pl.BlockSpec((None, to_n, tr), lambda i, j, l: (l // nb, j, l % nb))
        else:
            b_spec = pl.BlockSpec((to_n, tr), lambda i, j, l: (j, l))
        dims, oshape = (1, 1), (m, k)
    else:
        (t, k), n = a.shape, b.shape[1]
        ns = n // N_DEV if slabs else n
        to_m, to_n, tr = _pick(k, TO_PREFS), _pick(ns, TO_PREFS), _pick(t, TM_PREFS)
        nb = ns // to_n
        grid = (k // to_m, n // to_n, t // tr)
        a_spec = pl.BlockSpec((tr, to_m), lambda i, j, l: (l, i))
        b_spec = pl.BlockSpec((tr, to_n), lambda i, j, l: (l, j))
        dims, oshape = (0, 0), (k, n)
        if slabs:
            o_spec = pl.BlockSpec((None, to_m, to_n), lambda i, j, l: (j // nb, i, j % nb))
            oshape = (N_DEV, k, ns)
    n_red = grid[2]
    if o_spec is None:
        o_spec = pl.BlockSpec((to_m, to_n), lambda i, j, l: (i, j))
    has_add = add is not None

    def body(a_ref, b_ref, *rest):
        add_ref = rest[0] if has_add else None
        o_ref = rest[1] if has_add else rest[0]
        part = lax.dot_general(a_ref[...].astype(BF16), b_ref[...].astype(BF16),
                               (((dims[0],), (dims[1],)), ((), ())), preferred_element_type=F32)

        def finish(v):
            if has_add:
                v = v + add_ref[...]
            o_ref[...] = v.astype(o_ref.dtype)

        if n_red == 1:
            finish(part)
        else:
            acc = rest[-1]
            step = pl.program_id(2)

            @pl.when(step == 0)
            def _():
                acc[...] = part

            @pl.when(step > 0)
            def _():
                acc[...] += part

            @pl.when(step == n_red - 1)
            def _():
                finish(acc[...])

    return pl.pallas_call(
        body, grid=grid, name=name,
        in_specs=[a_spec, b_spec] + ([o_spec] if has_add else []),
        out_specs=o_spec, out_shape=_sds(oshape, out_dtype),
        scratch_shapes=[pltpu.VMEM((to_m, to_n), F32)] if n_red > 1 else [],
        compiler_params=_cparams(3),
    )(a, b, *([add] if has_add else []))


def assemble(name, pieces, out_dtype):
    flat = [s for piece in pieces for s in piece]
    t = flat[0][0].shape[0]
    widths = [piece[0][1] for piece in pieces]

    def body(*refs):
        o_ref, k, off = refs[-1], 0, 0
        for piece, w in zip(pieces, widths):
            v = refs[k][...].astype(F32)
            k += 1
            for _ in piece[1:]:
                v = v + refs[k][...].astype(F32)
                k += 1
            o_ref[:, off:off + w] = v.astype(o_ref.dtype)
            off += w

    tr = ROW_TILE // 2
    return pl.pallas_call(
        body, grid=(t // tr,), name=name,
        in_specs=[pl.BlockSpec((tr, w), functools.partial(lambda i, cb: (i, cb), cb=cb)) for (_, w, cb) in flat],
        out_specs=pl.BlockSpec((tr, sum(widths)), lambda i: (i, 0)),
        out_shape=_sds((t, sum(widths)), out_dtype),
        compiler_params=_cparams(1),
    )(*[s[0] for s in flat])


ELEM_ROWS = (128, 64, 32, 16, 8)


def _rows2d(a, lead=0):
    return a.reshape(a.shape[:lead] + (-1, a.shape[-1]))


def sum_leading(name, arr, out_dtype):
    k, rows, w = arr.shape
    tr = _pick(rows, ELEM_ROWS)

    def body(a_ref, o_ref):
        v = a_ref[0].astype(F32)
        for j in range(1, k):
            v = v + a_ref[j].astype(F32)
        o_ref[...] = v.astype(o_ref.dtype)

    return pl.pallas_call(
        body, grid=(rows // tr,), name=name,
        in_specs=[pl.BlockSpec((k, tr, w), lambda i: (0, i, 0))],
        out_specs=pl.BlockSpec((tr, w), lambda i: (i, 0)),
        out_shape=_sds((rows, w), out_dtype), compiler_params=_cparams(1),
    )(arr)


def add_own(name, g8, got, core, out_dtype):
    _, rows, w = g8.shape
    tr = _pick(rows, ELEM_ROWS)

    def body(core_ref, a_ref, b_ref, o_ref):
        o_ref[...] = (a_ref[...] + b_ref[...]).astype(o_ref.dtype)

    spec = pl.BlockSpec((None, tr, w), lambda q, i, core_ref: (q, i, 0))
    return pl.pallas_call(
        body, name=name,
        grid_spec=pltpu.PrefetchScalarGridSpec(
            num_scalar_prefetch=1, grid=(4, rows // tr),
            in_specs=[pl.BlockSpec((None, tr, w), lambda q, i, core_ref: (2 * q + core_ref[0], i, 0)), spec],
            out_specs=spec),
        out_shape=_sds(got.shape, out_dtype), compiler_params=_cparams(2),
    )(core, g8, got)


def adamw(name, g, w, m, v):
    rows, wd = g.shape
    tr = _pick(rows, ELEM_ROWS)

    def body(g_ref, w_ref, m_ref, v_ref, d_ref, nm_ref, nv_ref):
        gv = g_ref[...]
        mn = ADAM_B1 * m_ref[...] + (1.0 - ADAM_B1) * gv
        vn = ADAM_B2 * v_ref[...] + (1.0 - ADAM_B2) * jnp.square(gv)
        m_hat = mn / (1.0 - ADAM_B1 ** ADAM_STEP)
        v_hat = vn / (1.0 - ADAM_B2 ** ADAM_STEP)
        d_ref[...] = -ADAM_LR * (m_hat / (jnp.sqrt(v_hat) + ADAM_EPS) + ADAM_WD * w_ref[...])
        nm_ref[...] = mn
        nv_ref[...] = vn

    spec = pl.BlockSpec((tr, wd), lambda i: (i, 0))
    return pl.pallas_call(
        body, grid=(rows // tr,), name=name, in_specs=[spec] * 4, out_specs=[spec] * 3,
        out_shape=[_sds(g.shape, F32)] * 3, compiler_params=_cparams(1),
    )(g, w, m, v)


def loss_kernel(name, xs, target, n_ctx_tiles):
    t, d = xs.shape
    nt = t // ROW_TILE

    def body(x_ref, t_ref, dx_ref, l_ref):
        i = pl.program_id(0)
        is_lat = i >= n_ctx_tiles
        err = jnp.where(is_lat, x_ref[...] - t_ref[...], 0.0)
        dx_ref[...] = err / d
        part = 0.5 * jnp.sum(jnp.mean(jnp.square(err), axis=-1, keepdims=True), axis=0, keepdims=True)

        @pl.when(i == 0)
        def _():
            l_ref[...] = jnp.zeros_like(l_ref)

        l_ref[...] += jnp.broadcast_to(part, l_ref.shape)

    dx, l = pl.pallas_call(
        body, grid=(nt,), name=name,
        in_specs=[pl.BlockSpec((ROW_TILE, d), lambda i: (i, 0)),
                  pl.BlockSpec((ROW_TILE, d), lambda i: (jnp.maximum(i - n_ctx_tiles, 0), 0))],
        out_specs=[pl.BlockSpec((ROW_TILE, d), lambda i: (i, 0)), pl.BlockSpec((SUBLANES, LANES), lambda i: (0, 0))],
        out_shape=[_sds((t, d), F32), _sds((SUBLANES, LANES), F32)], compiler_params=_cparams(1),
    )(xs, target)
    return l[0, 0], dx


CONV_COLS = 512
CONV_LEFT = C_CONV // 2


def _conv_halo_specs(t, n_ctx_tiles):
    nt = t // ROW_TILE
    per = ROW_TILE // SUBLANES
    cur = pl.BlockSpec((ROW_TILE, CONV_COLS), lambda j, i: (i, j))
    prev = pl.BlockSpec((SUBLANES, CONV_COLS), lambda j, i: (jnp.maximum(i * per - 1, 0), j))
    nxt = pl.BlockSpec((SUBLANES, CONV_COLS), lambda j, i: (jnp.minimum((i + 1) * per, nt * per - 1), j))
    return cur, prev, nxt


def _fill_ext(ext, prev_ref, cur_ref, next_ref, i, nt, n_ctx_tiles):
    has_prev = jnp.logical_and(i != 0, i != n_ctx_tiles)
    has_next = jnp.logical_and(i != n_ctx_tiles - 1, i != nt - 1)
    ext[0:SUBLANES, :] = jnp.where(has_prev, prev_ref[...], 0.0)
    ext[SUBLANES:SUBLANES + ROW_TILE, :] = cur_ref[...]
    ext[SUBLANES + ROW_TILE:, :] = jnp.where(has_next, next_ref[...], 0.0)


def conv_fwd(name, p, w, width, n_ctx_tiles):
    t = p.shape[0]
    nt = t // ROW_TILE
    cur, prev, nxt = _conv_halo_specs(t, n_ctx_tiles)

    def body(c_ref, p_ref, n_ref, w_ref, o_ref, ext):
        _fill_ext(ext, p_ref, c_ref, n_ref, pl.program_id(1), nt, n_ctx_tiles)
        acc = None
        for j in range(C_CONV):
            term = ext[pl.ds(SUBLANES + j - CONV_LEFT, ROW_TILE), :] * w_ref[j:j + 1, :]
            acc = term if acc is None else acc + term
        o_ref[...] = acc

    return pl.pallas_call(
        body, grid=(width // CONV_COLS, nt), name=name,
        in_specs=[cur, prev, nxt, pl.BlockSpec((C_CONV, CONV_COLS), lambda j, i: (0, j))],
        out_specs=cur, out_shape=_sds((t, width), F32),
        scratch_shapes=[pltpu.VMEM((ROW_TILE + 2 * SUBLANES, CONV_COLS), F32)],
        compiler_params=_cparams(2),
    )(p, p, p, w)


def conv_bwd(name, p, w, dz, width, n_ctx_tiles):
    t = p.shape[0]
    nt = t // ROW_TILE
    cur, prev, nxt = _conv_halo_specs(t, n_ctx_tiles)

    def body(c_ref, p_ref, n_ref, dc_ref, dp_ref, dn_ref, w_ref, du_ref, dw_ref, ext, dext):
        i = pl.program_id(1)
        _fill_ext(ext, p_ref, c_ref, n_ref, i, nt, n_ctx_tiles)
        _fill_ext(dext, dp_ref, dc_ref, dn_ref, i, nt, n_ctx_tiles)
        dzc = dc_ref[...]
        @pl.when(i == 0)
        def _():
            dw_ref[...] = jnp.zeros_like(dw_ref)

        acc = None
        for j in range(C_CONV):
            term = dext[pl.ds(SUBLANES + CONV_LEFT - j, ROW_TILE), :] * w_ref[j:j + 1, :]
            acc = term if acc is None else acc + term
            dw_ref[j:j + 1, :] += jnp.sum(dzc * ext[pl.ds(SUBLANES + j - CONV_LEFT, ROW_TILE), :], axis=0, keepdims=True)
        du_ref[...] = acc

    wspec = pl.BlockSpec((C_CONV, CONV_COLS), lambda j, i: (0, j))
    return pl.pallas_call(
        body, grid=(width // CONV_COLS, nt), name=name,
        in_specs=[cur, prev, nxt, cur, prev, nxt, wspec],
        out_specs=[cur, wspec], out_shape=[_sds((t, width), F32), _sds((C_CONV, width), F32)],
        scratch_shapes=[pltpu.VMEM((ROW_TILE + 2 * SUBLANES, CONV_COLS), F32)] * 2,
        compiler_params=_cparams(2),
    )(p, p, p, dz, dz, dz, w)


def _rms(x, w):
    return x * lax.rsqrt(jnp.mean(x * x, axis=-1, keepdims=True) + EPS) * w


def _seg_mod(mods, is_ctx):
    return jnp.where(is_ctx, mods[0], mods[1])


def f_norm_mod(shift_i, scale_i, n_ctx_tiles, passthrough=False):
    def f(pids, x, nw, mods):
        m = _seg_mod(mods, pids[0] < n_ctx_tiles)
        h = _rms(x, nw) * (1.0 + m[scale_i:scale_i + 1]) + m[shift_i:shift_i + 1]
        return (h, x) if passthrough else (h,)
    return f


def f_gate_res(gate_i, n_ctx_tiles):
    def f(pids, x, y, nw, mods):
        m = _seg_mod(mods, pids[0] < n_ctx_tiles)
        return (x + m[gate_i:gate_i + 1] * _rms(y, nw),)
    return f


def f_swiglu(pids, gu):
    half = gu.shape[1] // 2
    return (jax.nn.silu(gu[:, :half]) * gu[:, half:],)


def f_readout(n_heads):
    def f(pids, o_a, o_b, gate, nw):
        cols = [slice(j * HEAD, (j + 1) * HEAD) for j in range(n_heads)]
        outs = _each(lambda cs: _rms(o_a[:, cs] + o_b[:, cs], nw) * jax.nn.silu(gate[:, cs]), cols)
        return (jnp.concatenate(outs, axis=1) if n_heads > 1 else outs[0],)
    return f


def f_pool(pids, u, pmat, pw, scale):
    d = mm(pmat, u, 1, 0, True) - u
    return (mm(d, pw) * scale,)


def f_lb(layer):
    def f(pids, *slots):
        top = slots[0]
        for s in slots[1:]:
            top = jnp.maximum(top, s)
        ex = [jnp.exp(s - top) for s in slots]
        tot = ex[0]
        for e in ex[1:]:
            tot = tot + e
        part = ex[0]
        for e in ex[1:layer + 1]:
            part = part + e
        return (part / tot,)
    return f


def f_ada(pids, c_all, c_ctx, w, b):
    c16 = jnp.concatenate([c_all, jnp.broadcast_to(c_ctx, c_all.shape)], axis=0)
    return (mm(jax.nn.silu(c16), w) + b,)


def _each(fn, *lists):
    return [fn(*xs) for xs in zip(*lists)]


def _hgrn2_heads(sts, qrs, frs, irs, lbs, rev):
    c = A_CHUNK
    mid = c - c // 2 if rev else c // 2 - 1
    ri, ci = _iota2(c, c, 0), _iota2(c, c, 1)
    incl = (ri <= ci) if rev else (ri >= ci)
    incl_f = incl.astype(F32)
    qs = _each(jax.nn.silu, qrs)
    log_fs = _each(lambda lb, fr: jnp.log(lb + (1.0 - lb) * jax.nn.sigmoid(fr)), lbs, frs)
    ks = _each(lambda lb, fr: (1.0 - lb) * jax.nn.sigmoid(-fr), lbs, frs)
    bs = _each(lambda lf: mm(incl_f, lf, 1, 0, True), log_fs)
    b_lasts = _each(lambda lf: jnp.sum(lf, axis=0, keepdims=True), log_fs)
    scores = _each(lambda q, k, b: mm(q * jnp.exp(b - b[mid:mid + 1]), k * jnp.exp(b[mid:mid + 1] - b), 1, 1), qs, ks, bs)
    intra = _each(lambda sc, ir: mm(jnp.where(incl, sc, 0.0), ir), scores, irs)
    inter = _each(lambda q, b, st: mm(q * jnp.exp(b), st, 1, 1), qs, bs, sts)
    upd = _each(lambda ir, k, bl, b: mm(ir, k * jnp.exp(bl - b), 0, 0), irs, ks, b_lasts, bs)
    new = _each(lambda st, bl, u: st * jnp.exp(bl) + u, sts, b_lasts, upd)
    return new, _each(jnp.add, intra, inter)


def f_hgrn2(rev, hb):
    def f(states, qr, fr, ir, lb):
        cols = [slice(j * HEAD, (j + 1) * HEAD) for j in range(hb)]
        new, outs = _hgrn2_heads(list(states), [qr[:, cs] for cs in cols], [fr[:, cs] for cs in cols],
                                 [ir[:, cs] for cs in cols], [lb[j] for j in range(hb)], rev)
        return tuple(new), (jnp.concatenate(outs, axis=1) if hb > 1 else outs[0],)
    return f


def _neumann_inv(a_lows):
    n = a_lows[0].shape[0]
    eye = (_iota2(n, n, 0) == _iota2(n, n, 1)).astype(F32)
    ps = _each(lambda a: -a, a_lows)
    xs = _each(lambda p: eye + p, ps)
    k = 2
    while k < n:
        ps = _each(lambda p: mm(p, p, 1, 0, True), ps)
        xs = _each(lambda x, p: x + mm(x, p, 1, 0, True), xs, ps)
        k *= 2
    return tuple(xs)


@jax.custom_vjp
def unit_tri_inv(a_lows):
    return _neumann_inv(a_lows)


def _uti_fwd(a_lows):
    xs = _neumann_inv(a_lows)
    return xs, xs


def _uti_bwd(xs, gs):
    ts = _each(lambda x, g: mm(x, g, 0, 0, True), xs, gs)
    return (tuple(_each(lambda t, x: -mm(t, x, 1, 1, True), ts, xs)),)


unit_tri_inv.defvjp(_uti_fwd, _uti_bwd)


@jax.custom_vjp
def unit_tri_inv_saved(a_lows, xs):
    return xs


def _utis_fwd(a_lows, xs):
    return xs, xs


def _utis_bwd(xs, gs):
    return _uti_bwd(xs, gs) + (tuple(jnp.zeros_like(x) for x in xs),)


unit_tri_inv_saved.defvjp(_utis_fwd, _utis_bwd)


def _l2n(x):
    return x * lax.rsqrt(jnp.sum(x * x, axis=-1, keepdims=True) + EPS)


def _gdn_heads(ss, qs, ks, vs, a_rows, b_rows, alogs, dtbs, rev, xs_saved=None):
    c = qs[0].shape[0]
    ri, ci = _iota2(c, c, 0), _iota2(c, c, 1)
    causal = (ri <= ci) if rev else (ri >= ci)
    causal_t = (ri >= ci) if rev else (ri <= ci)
    strict = (ri < ci) if rev else (ri > ci)
    eye = ri == ci
    sq = lambda row: jnp.broadcast_to(row, (c, c))
    to_col = lambda row: jnp.sum(jnp.where(eye, sq(row), 0.0), axis=1, keepdims=True)
    g_rows = _each(lambda al, a, dt: -jnp.exp(al) * jax.nn.softplus(a + dt), alogs, a_rows, dtbs)
    beta_cols = _each(lambda b: to_col(jax.nn.sigmoid(b)), b_rows)
    g_cols = _each(to_col, g_rows)
    gc_cols = _each(lambda g: jnp.sum(jnp.where(causal, sq(g), 0.0), axis=1, keepdims=True), g_rows)
    gc_rows = _each(lambda g: jnp.sum(jnp.where(causal_t, sq(g), 0.0), axis=0, keepdims=True), g_cols)
    gc_lasts = _each(lambda g: jnp.sum(g, axis=1, keepdims=True), g_rows)
    decays = _each(lambda gc, gr: jnp.where(causal, jnp.exp(jnp.where(causal, gc - gr, 0.0)), 0.0), gc_cols, gc_rows)
    k_betas = _each(jnp.multiply, ks, beta_cols)
    v_betas = _each(jnp.multiply, vs, beta_cols)
    a_lows = _each(lambda kb, k, dec: jnp.where(strict, mm(kb, k, 1, 1) * dec, 0.0), k_betas, ks, decays)
    xs = unit_tri_inv(tuple(a_lows)) if xs_saved is None else unit_tri_inv_saved(tuple(a_lows), tuple(xs_saved))
    egcs = _each(jnp.exp, gc_cols)
    us = _each(lambda x, vb: mm(x, vb, 1, 0, True), xs, v_betas)
    ws = _each(lambda x, kb, e: mm(x, kb * e, 1, 0, True), xs, k_betas, egcs)
    qks = _each(lambda q, k, dec: jnp.where(causal, mm(q, k, 1, 1) * dec, 0.0), qs, ks, decays)
    v_news = _each(lambda u, w, s: u - mm(w, s), us, ws, ss)
    o_states = _each(lambda q, e, s: mm(q * e, s), qs, egcs, ss)
    o_locals = _each(mm, qks, v_news)
    upds = _each(lambda k, gl, gc, vn: mm(k * jnp.exp(gl - gc), vn, 0, 0), ks, gc_lasts, gc_cols, v_news)
    new = _each(lambda s, gl, u: s * jnp.exp(gl) + u, ss, gc_lasts, upds)
    return new, _each(jnp.add, o_states, o_locals), xs


def f_gdn(rev, khb, saved_inverse):
    def f(states, qr, kr, vr, a3, b3, alog3, dtb3, xcat=None):
        heads = [(j, r) for j in range(khb) for r in range(2)]
        cols = [slice(j * HEAD, (j + 1) * HEAD) for j in range(khb)]
        c = qr.shape[0]
        qk_ = _each(lambda cs: (_l2n(jax.nn.silu(qr[:, cs])) * (HEAD ** -0.5), _l2n(jax.nn.silu(kr[:, cs]))), cols)
        vs = [jax.nn.silu(vr[:, (2 * j + r) * HEAD:(2 * j + r + 1) * HEAD]) for j, r in heads]
        row = lambda arr3: [arr3[j][r:r + 1] for j, r in heads]
        xs_saved = [xcat[n * c:(n + 1) * c] for n in range(len(heads))] if saved_inverse else None
        new, outs, xs = _gdn_heads(list(states), [qk_[j][0] for j, _ in heads], [qk_[j][1] for j, _ in heads], vs,
                                   row(a3), row(b3), row(alog3), row(dtb3), rev, xs_saved)
        o = jnp.concatenate(outs, axis=1)
        return tuple(new), ((o,) if saved_inverse else (o, jnp.concatenate(xs, axis=0)))
    return f


def _hbm_spec():
    return pl.BlockSpec(memory_space=pltpu.HBM)


def all_gather(name, xs):
    nt = len(xs)

    def body(*refs):
        x_refs, out_refs = refs[:nt], refs[nt:2 * nt]
        send_sems, recv_sems, local_sems = refs[2 * nt:]
        x, y, c = lax.axis_index("x"), lax.axis_index("y"), lax.axis_index("c")
        me, sibling = (x, y, c), (x, y, 1 - c)
        chips = [(1 - x, y), (x, 1 - y), (1 - x, 1 - y)]

        def slab(t, px, py, pc):
            return out_refs[t].at[4 * px + 2 * py + pc]

        def copy(t, k, block, to, src=None):
            return pltpu.make_async_remote_copy(
                src_ref=slab(t, *block) if src is None else src, dst_ref=slab(t, *block),
                send_sem=send_sems.at[7 * t + k], recv_sem=recv_sems.at[7 * t + k], device_id=to, device_id_type=MESH)

        mine, first, passed = [], [], []
        for t in range(nt):
            mine.append(pltpu.make_async_copy(x_refs[t], slab(t, *me), local_sems.at[t]))
            mine[-1].start()
            cps = [copy(t, 0, me, sibling, src=x_refs[t])]
            cps += [copy(t, 1 + j, me, (*chip, c), src=x_refs[t]) for j, chip in enumerate(chips)]
            for cp in cps:
                cp.start()
            first += cps
        for j, chip in enumerate(chips):
            for t in range(nt):
                copy(t, 1 + j, (*chip, c), me).wait_recv()
                fw = copy(t, 4 + j, (*chip, c), sibling)
                fw.start()
                passed.append(fw)
        for t in range(nt):
            copy(t, 0, sibling, me).wait_recv()
            for j, chip in enumerate(chips):
                copy(t, 4 + j, (*chip, 1 - c), me).wait_recv()
        for cp in first + passed:
            cp.wait_send()
        for cp in mine:
            cp.wait()

    return pl.pallas_call(
        body, name=name, out_shape=[_sds((N_DEV,) + a.shape, a.dtype) for a in xs],
        in_specs=[_hbm_spec()] * nt, out_specs=[_hbm_spec()] * nt,
        scratch_shapes=[pltpu.SemaphoreType.DMA((7 * nt,)), pltpu.SemaphoreType.DMA((7 * nt,)),
                        pltpu.SemaphoreType.DMA((nt,))],
    )(*xs)


def _peers(x, y, c):
    out = []
    for k in range(1, N_DEV):
        px = 1 - x if k & 4 else x
        py = 1 - y if k & 2 else y
        pc = 1 - c if k & 1 else c
        out.append((px, py, pc))
    return out


def _exchange_copies(src_refs, land_refs, send_sems, recv_sems, scatter):
    x, y, c = lax.axis_index("x"), lax.axis_index("y"), lax.axis_index("c")
    me = 4 * x + 2 * y + c
    sends, recvs = [], []
    for t, (src, land) in enumerate(zip(src_refs, land_refs)):
        for k, (px, py, pc) in enumerate(_peers(x, y, c)):
            peer = 4 * px + 2 * py + pc
            sem = dict(send_sem=send_sems.at[7 * t + k], recv_sem=recv_sems.at[7 * t + k],
                       device_id=(px, py, pc), device_id_type=MESH)
            src_k = src.at[peer] if scatter else src
            sends.append(pltpu.make_async_remote_copy(src_ref=src_k, dst_ref=land.at[me], **sem))
            recvs.append(pltpu.make_async_remote_copy(src_ref=src_k, dst_ref=land.at[peer], **sem))
    return sends, recvs


def exchange_start(name, srcs, scatter, after):
    nt = len(srcs)
    lands = [lax.empty(s.shape if scatter else (N_DEV,) + s.shape, s.dtype) for s in srcs]

    def body(*refs):
        src_refs, land_refs = refs[:nt], refs[nt:2 * nt]
        send_sems, recv_sems = refs[2 * nt + 1], refs[2 * nt + 2]
        token = refs[-1]
        sends, _ = _exchange_copies(src_refs, land_refs, send_sems, recv_sems, scatter)
        for cp in sends:
            cp.start()
        token[...] = jnp.zeros_like(token)

    hbm = lambda a: pltpu.HBM(a.shape, a.dtype)
    res = pl.pallas_call(
        body, name=name,
        out_shape=(pltpu.SemaphoreType.DMA((7 * nt,)), pltpu.SemaphoreType.DMA((7 * nt,)),
                   *[hbm(a) for a in srcs], *[hbm(a) for a in lands], _sds((SUBLANES, LANES), F32)),
        in_specs=[_hbm_spec()] * (2 * nt) + [pl.BlockSpec(memory_space=pl.ANY)],
        out_specs=(pl.BlockSpec(memory_space=pltpu.SEMAPHORE), pl.BlockSpec(memory_space=pltpu.SEMAPHORE),
                   *[_hbm_spec()] * (2 * nt), pl.BlockSpec(memory_space=pltpu.VMEM)),
        input_output_aliases={i: 2 + i for i in range(2 * nt)},
        compiler_params=pltpu.CompilerParams(has_side_effects=pltpu.SideEffectType.DATAFLOW_SIDE_EFFECTING),
    )(*[pltpu.with_memory_space_constraint(a, pltpu.HBM) for a in srcs],
      *[pltpu.with_memory_space_constraint(a, pltpu.HBM) for a in lands], after)
    return res[0], res[1], list(res[2:2 + nt]), list(res[2 + nt:2 + 2 * nt]), res[-1]


def exchange_wait(name, started, after, scatter):
    send_sems, recv_sems, srcs, lands, _ = started
    nt = len(srcs)

    def body(*refs):
        src_refs, land_refs = refs[:nt], refs[nt:2 * nt]
        sends, recvs = _exchange_copies(src_refs, land_refs, refs[2 * nt], refs[2 * nt + 1], scatter)
        for cp in sends:
            cp.wait_send()
        for cp in recvs:
            cp.wait_recv()

    hbm = lambda a: pltpu.HBM(a.shape, a.dtype)
    sem = pl.BlockSpec(memory_space=pltpu.SEMAPHORE)
    res = pl.pallas_call(
        body, name=name,
        out_shape=(*[hbm(a) for a in srcs], *[hbm(a) for a in lands]),
        in_specs=[_hbm_spec()] * (2 * nt) + [sem, sem, pl.BlockSpec(memory_space=pl.ANY)],
        out_specs=tuple([_hbm_spec()] * (2 * nt)),
        input_output_aliases={i: i for i in range(2 * nt)},
        compiler_params=pltpu.CompilerParams(has_side_effects=pltpu.SideEffectType.DATAFLOW_SIDE_EFFECTING),
    )(*srcs, *lands, send_sems, recv_sems, after)
    return list(res[:nt]), list(res[nt:])


def sibling_exchange(name, gs):
    nt = len(gs)

    def body(*refs):
        g_refs, l_refs, send_sems, recv_sems = refs[:nt], refs[nt:2 * nt], refs[2 * nt], refs[2 * nt + 1]
        x, y, c = lax.axis_index("x"), lax.axis_index("y"), lax.axis_index("c")
        cps = []
        for t in range(nt):
            for q in range(4):
                cps.append(pltpu.make_async_remote_copy(
                    src_ref=g_refs[t].at[2 * q + (1 - c)], dst_ref=l_refs[t].at[q], send_sem=send_sems.at[4 * t + q],
                    recv_sem=recv_sems.at[4 * t + q], device_id=(x, y, 1 - c), device_id_type=MESH))
        for cp in cps:
            cp.start()
        for cp in cps:
            cp.wait()

    return pl.pallas_call(
        body, name=name, out_shape=[_sds((4,) + g.shape[1:], g.dtype) for g in gs],
        in_specs=[_hbm_spec()] * nt, out_specs=[_hbm_spec()] * nt,
        scratch_shapes=[pltpu.SemaphoreType.DMA((4 * nt,)), pltpu.SemaphoreType.DMA((4 * nt,))],
    )(*gs)


def chip_exchange(name, hs):
    nt = len(hs)

    def body(*refs):
        h_refs, r_refs = refs[:nt], refs[nt:2 * nt]
        send_sems, recv_sems, local_sems = refs[2 * nt:]
        x, y, c = lax.axis_index("x"), lax.axis_index("y"), lax.axis_index("c")
        my = 2 * x + y
        chips = [(1 - x, y), (x, 1 - y), (1 - x, 1 - y)]

        def copy(t, k, src_slot, dst_slot, to):
            return pltpu.make_async_remote_copy(
                src_ref=h_refs[t].at[src_slot], dst_ref=r_refs[t].at[dst_slot], send_sem=send_sems.at[3 * t + k],
                recv_sem=recv_sems.at[3 * t + k], device_id=(*to, c), device_id_type=MESH)

        mine, sends = [], []
        for t in range(nt):
            mine.append(pltpu.make_async_copy(h_refs[t].at[my], r_refs[t].at[my], local_sems.at[t]))
            mine[-1].start()
            for k, (qx, qy) in enumerate(chips):
                sends.append(copy(t, k, 2 * qx + qy, my, (qx, qy)))
                sends[-1].start()
        for t in range(nt):
            for k, (qx, qy) in enumerate(chips):
                copy(t, k, my, 2 * qx + qy, (qx, qy)).wait_recv()
        for cp in sends:
            cp.wait_send()
        for cp in mine:
            cp.wait()

    return pl.pallas_call(
        body, name=name, out_shape=[_sds(h.shape, h.dtype) for h in hs],
        in_specs=[_hbm_spec()] * nt, out_specs=[_hbm_spec()] * nt,
        scratch_shapes=[pltpu.SemaphoreType.DMA((3 * nt,)), pltpu.SemaphoreType.DMA((3 * nt,)),
                        pltpu.SemaphoreType.DMA((nt,))],
    )(*hs)


def _pack(arrs, dtype, width, row_mult, lead=0):
    ld = arrs[0].shape[:lead]
    flat = jnp.concatenate([a.reshape(ld + (-1,)).astype(dtype) for a in arrs], axis=-1)
    n = flat.shape[-1]
    q = width * row_mult
    npad = -(-n // q) * q
    flat = jnp.pad(flat, [(0, 0)] * lead + [(0, npad - n)])
    return flat.reshape(ld + (npad // width, width))


def _unpack(buf, shapes, lead=0):
    ld = buf.shape[:lead]
    flat = buf.reshape(ld + (-1,))
    out, off = [], 0
    for s in shapes:
        n = int(np.prod(s))
        out.append(flat[..., off:off + n].reshape(ld + tuple(s)))
        off += n
    return out


def _pool_mats(seg_len):
    mats = np.zeros((len(POOL_WINDOWS), ROW_TILE, ROW_TILE), np.float32)
    for gi, win in enumerate(POOL_WINDOWS):
        for p in range(ROW_TILE):
            base = (p // seg_len) * seg_len
            q = p - base
            lo = min(max(q - win // 2, 0), seg_len - 1)
            hi = min(max(q + win - 1 - win // 2, 0), seg_len - 1)
            mats[gi, p, base + lo:base + hi + 1] = 1.0 / (hi - lo + 1)
    return mats


def kernel(x, c, ctx, c_ctx, w_ada, b_ada, norm_w, ev_w_in, ev_lb, ev_a_norm, ev_pool_w, ev_pool_scale, ev_w_out, od_w_in, od_conv, od_A_log, od_dt_bias, od_norm, od_w_out, ffn_w13, ffn_w2, loss_target, m_c_ctx, m_w_ada, m_b_ada, m_norm_w, m_ev_w_in, m_ev_lb, m_ev_a_norm, m_ev_pool_w, m_ev_pool_scale, m_ev_w_out, m_od_w_in, m_od_conv, m_od_A_log, m_od_dt_bias, m_od_norm, m_od_w_out, m_ffn_w13, m_ffn_w2, v_c_ctx, v_w_ada, v_b_ada, v_norm_w, v_ev_w_in, v_ev_lb, v_ev_a_norm, v_ev_pool_w, v_ev_pool_scale, v_ev_w_out, v_od_w_in, v_od_conv, v_od_A_log, v_od_dt_bias, v_od_norm, v_od_w_out, v_ffn_w13, v_ffn_w2):
    names = ["c_ctx", "w_ada", "b_ada", "norm_w", "ev_w_in", "ev_lb", "ev_a_norm", "ev_pool_w", "ev_pool_scale",
             "ev_w_out", "od_w_in", "od_conv", "od_A_log", "od_dt_bias", "od_norm", "od_w_out", "ffn_w13", "ffn_w2"]
    wts = dict(zip(names, [c_ctx, w_ada, b_ada, norm_w, ev_w_in, ev_lb, ev_a_norm, ev_pool_w, ev_pool_scale,
                           ev_w_out, od_w_in, od_conv, od_A_log, od_dt_bias, od_norm, od_w_out, ffn_w13, ffn_w2]))
    mom1 = dict(zip(names, [m_c_ctx, m_w_ada, m_b_ada, m_norm_w, m_ev_w_in, m_ev_lb, m_ev_a_norm, m_ev_pool_w,
                            m_ev_pool_scale, m_ev_w_out, m_od_w_in, m_od_conv, m_od_A_log, m_od_dt_bias, m_od_norm,
                            m_od_w_out, m_ffn_w13, m_ffn_w2]))
    mom2 = dict(zip(names, [v_c_ctx, v_w_ada, v_b_ada, v_norm_w, v_ev_w_in, v_ev_lb, v_ev_a_norm, v_ev_pool_w,
                            v_ev_pool_scale, v_ev_w_out, v_od_w_in, v_od_conv, v_od_A_log, v_od_dt_bias, v_od_norm,
                            v_od_w_out, v_ffn_w13, v_ffn_w2]))

    ax, ay, ac = lax.axis_index("x"), lax.axis_index("y"), lax.axis_index("c")
    me = 4 * ax + 2 * ay + ac
    my_chip = 2 * ax + ay

    seq, d = x.shape[1], x.shape[2]
    n_ctx = ctx.shape[1]
    t = n_ctx + seq
    nt = t // ROW_TILE
    nct = n_ctx // ROW_TILE
    assert n_ctx == ROW_TILE and seq % ROW_TILE == 0 and ROW_TILE % GRID_W == 0
    depth = w_ada.shape[0]
    aw = d // 2
    n_ah = aw // HEAD
    n_grp = len(POOL_WINDOWS)
    dg = aw // n_grp
    assert dg % LANES == 0
    n_kh = d // HEAD
    kw, vw = n_kh * HEAD, 2 * n_kh * HEAD
    n_gate = 8 * n_kh
    ffn_h = ffn_w2.shape[1] * N_DEV
    ada_loc = w_ada.shape[2]
    assert depth == 2

    bf = lambda w_: w_.astype(BF16)

    def gathered(started, after, name):
        srcs, lands = exchange_wait(name, started, after, False)
        return [lax.dynamic_update_index_in_dim(l_, s_, me, 0) for l_, s_ in zip(lands, srcs)]

    small_shapes = [(d,), norm_w.shape, ev_lb.shape, ev_pool_w.shape[1:], od_conv.shape[1:]]
    (g1,) = all_gather("ag_small", [_pack([c[0], norm_w, ev_lb, ev_pool_w[0], od_conv[0]], F32, LANES, SUBLANES)])
    c_all, nw_g, lb_g, pw_g, cv_g = _unpack(g1, small_shapes, lead=1)
    nw_full = nw_g.transpose(1, 2, 0, 3).reshape(depth, 4, d)
    lb_full = lb_g.transpose(1, 2, 0, 3).reshape(2, depth + 1, aw)
    pw_full = pw_g.transpose(1, 0, 2, 3).reshape(n_grp, dg, dg)
    cv_full = cv_g.transpose(1, 0, 2).reshape(C_CONV, 2 * kw + vw)

    g_ev_in, g_ev_out = all_gather("ag_weights_ev", [bf(ev_w_in[0]), bf(ev_w_out[0])])

    def cols_natural(g):
        return g.transpose(1, 0, 2).reshape(g.shape[1], N_DEV * g.shape[2])

    def rows_natural(g):
        return g.reshape(N_DEV * g.shape[1], g.shape[2])

    def col_weight(g):
        return (g, True) if g.shape[2] % LANES == 0 else (cols_natural(g), False)

    w_ev_in = col_weight(g_ev_in)
    w_ev_out = rows_natural(g_ev_out)
    w13, w2 = [None] * depth, [None] * depth

    b_loc = lax.dynamic_slice_in_dim(b_ada, me * ada_loc, ada_loc, axis=1).reshape(depth, 1, ada_loc)
    ada_cb = _pick(ada_loc, TN_PREFS)
    ada_grid = (depth, ada_loc // ada_cb)
    ada_args = [
        TArg(c_all, (N_DEV, d), lambda l, j: (0, 0), "const"),
        TArg(c_ctx.reshape(1, d), (1, d), lambda l, j: (0, 0), "par", (0, 1)),
        TArg(w_ada, (None, d, ada_cb), lambda l, j: (l, 0, j)),
        TArg(b_loc, (None, 1, ada_cb), lambda l, j: (l, 0, j)),
    ]
    (m_loc,) = tile_fwd("ada_fwd", f_ada, ada_grid, ada_args,
                        [((depth, 2 * N_DEV, ada_loc), F32, (None, 2 * N_DEV, ada_cb), lambda l, j: (l, 0, j))])
    (m_all,) = all_gather("ag_mod", [m_loc])
    mods = []
    for layer in range(depth):
        lat = lax.dynamic_index_in_dim(m_all[:, layer], me, axis=1, keepdims=False).reshape(6, d)
        cxt = lax.dynamic_index_in_dim(m_all[:, layer], N_DEV + me, axis=1, keepdims=False).reshape(6, d)
        mods.append(jnp.stack([cxt, lat]))

    gathers_done = mods[0][0, :1, :SUBLANES] + g_ev_out[0, :1, :SUBLANES].astype(F32)
    ag_ffn0 = exchange_start("ag_start_ffn0", [bf(ffn_w13[0]), bf(ffn_w2[0])], False, gathers_done)
    ag_l1 = exchange_start("ag_start_l1", [bf(od_w_in[0]), bf(od_w_out[0]), bf(ffn_w13[1]), bf(ffn_w2[1])], False,
                           gathers_done)
    mods[0] = mods[0] + (ag_ffn0[4][0, 0] + ag_l1[4][0, 0])

    lb_slots = [TArg(lb_full[:, j], (2, aw), lambda i: (0, 0)) for j in range(depth + 1)]
    (lb0,) = tile_fwd("lb_fwd", f_lb(0), (1,), lb_slots, [((2, aw), F32, (2, aw), lambda i: (0, 0))])
    lb0r = lb0.reshape(2, n_ah, 1, HEAD)

    full_row = lambda i: (i, 0)
    par0 = lambda i: (0, 0)

    def nm_args(xs, layer, slot):
        return [TArg(xs, (ROW_TILE, d), full_row),
                TArg(nw_full[layer, slot].reshape(1, d), (1, d), par0, "par", (0,)),
                TArg(mods[layer], (2, 6, d), lambda i: (0, 0, 0), "par", (0,))]

    def norm_mod(name, xs, layer, slot, si, ci):
        (h,) = tile_fwd(name, f_norm_mod(si, ci, nct), (nt,), nm_args(xs, layer, slot),
                        [((t, d), BF16, (ROW_TILE, d), full_row)])
        return h

    def norm_mod_bwd(name, xs, layer, slot, si, ci, dh, carry):
        return tile_bwd(name, f_norm_mod(si, ci, nct, True), (nt,), nm_args(xs, layer, slot),
                        [(dh, (ROW_TILE, d), full_row), (carry, (ROW_TILE, d), full_row)])

    def gr_args(xs, ys, layer, slot):
        return [TArg(xs, (ROW_TILE, d), full_row, grad=False), TArg(ys, (ROW_TILE, d), full_row, gdtype=BF16),
                TArg(nw_full[layer, slot].reshape(1, d), (1, d), par0, "par", (0,)),
                TArg(mods[layer], (2, 6, d), lambda i: (0, 0, 0), "par", (0,))]

    def gate_res(name, xs, ys, layer, slot, gi):
        (o,) = tile_fwd(name, f_gate_res(gi, nct), (nt,), gr_args(xs, ys, layer, slot),
                        [((t, d), F32, (ROW_TILE, d), full_row)])
        return o

    def gate_res_bwd(name, xs, ys, layer, slot, gi, dx):
        return tile_bwd(name, f_gate_res(gi, nct), (nt,), gr_args(xs, ys, layer, slot),
                        [(dx, (ROW_TILE, d), full_row)])

    sw_rows = ROW_TILE // 2

    def sw_args(gu):
        return [TArg(gu, (sw_rows, 2 * ffn_h), full_row, gdtype=BF16)]

    def ffn_fwd(tag, xs, layer):
        h2 = norm_mod(f"nm2_{tag}", xs, layer, 2, 3, 4)
        gu = matmul(f"w13_{tag}", h2, w13[layer][0], "nn", slabs=w13[layer][1])
        (act,) = tile_fwd(f"swiglu_{tag}", f_swiglu, (t // sw_rows,), sw_args(gu),
                          [((t, ffn_h), BF16, (sw_rows, ffn_h), full_row)])
        fo = matmul(f"w2_{tag}", act, w2[layer], "nn")
        xn = gate_res(f"gr2_{tag}", xs, fo, layer, 3, 5)
        return xn, (xs, h2, gu, act, fo)

    def col_grad(name, a, dy, slabs):
        g = matmul(name, a, dy, "tn", slabs=slabs, out_dtype=BF16)
        return g if slabs else g.reshape(g.shape[0], N_DEV, g.shape[1] // N_DEV).transpose(1, 0, 2)

    def row_grad(name, a, dy):
        g = matmul(name, a, dy, "tn", out_dtype=BF16)
        return g.reshape(N_DEV, g.shape[0] // N_DEV, g.shape[1])

    def ffn_bwd(tag, saved, layer, dxn, acc):
        xs, h2, gu, act, fo = saved
        dfo, dnw3, dmod_a = gate_res_bwd(f"gr2b_{tag}", xs, fo, layer, 3, 5, dxn)
        dact = matmul(f"w2d_{tag}", dfo, w2[layer], "nt")
        dw2 = row_grad(f"w2w_{tag}", act, dfo)
        (dgu,) = tile_bwd(f"swiglub_{tag}", f_swiglu, (t // sw_rows,), sw_args(gu), [(dact, (sw_rows, ffn_h), full_row)])
        dh2 = matmul(f"w13d_{tag}", dgu, w13[layer][0], "nt", slabs=w13[layer][1])
        dw13 = col_grad(f"w13w_{tag}", h2, dgu, w13[layer][1])
        dxs, dnw2, dmod_b = norm_mod_bwd(f"nm2b_{tag}", xs, layer, 2, 3, 4, dh2, dxn)
        acc["ffn_w13"][layer] = dw13
        acc["ffn_w2"][layer] = dw2
        acc["norm_w"][layer][2] = dnw2
        acc["norm_w"][layer][3] = dnw3
        acc["mods"][layer].extend([dmod_a, dmod_b])
        return dxs

    def head_cols(width):
        return (ROW_TILE, width)

    n_a = t // A_CHUNK
    nca = n_ctx // A_CHUNK

    def a_tok(rev):
        if not rev:
            return lambda i: i
        return lambda i: jnp.where(i < nca, nca - 1 - i, n_a + nca - 1 - i)

    hb = _pick(n_ah, (HEADS_PER_STEP, 2, 1))
    n_hblk = n_ah // hb

    def hg_args(p, direction):
        tok = a_tok(direction == 1)
        blk = (A_CHUNK, hb * HEAD)
        return [TArg(p, blk, lambda h, i: (tok(i), h)),
                TArg(p, blk, lambda h, i: (tok(i), (1 + direction) * n_hblk + h)),
                TArg(p, blk, lambda h, i: (tok(i), 3 * n_hblk + h)),
                TArg(lb0r, (None, hb, 1, HEAD), lambda h, i: (direction, h, 0, 0), "par", (1,))]

    pmats = jnp.asarray(np.stack([_pool_mats(n_ctx), _pool_mats(GRID_W)]))

    def pool_args(p):
        return [TArg(p, (ROW_TILE, dg), lambda g, i: (i, 5 * n_grp + g)),
                TArg(pmats, (None, None, ROW_TILE, ROW_TILE), lambda g, i: (jnp.where(i < nct, 0, 1), g, 0, 0), "const"),
                TArg(pw_full, (None, dg, dg), lambda g, i: (g, 0, 0), "par", (1,)),
                TArg(ev_pool_scale, (1, dg), lambda g, i: (0, g), "par", (1,))]

    def ro_plan(gate_off, n_heads):
        per = _pick(n_heads, (8, 4, 2, 1))
        assert gate_off % per == 0
        return per, n_heads // per, gate_off // per

    def ro_args(o_f, o_b, gate_arr, gate_off, nw_arr, n_heads):
        per, _, goff = ro_plan(gate_off, n_heads)
        blk = (ROW_TILE, per * HEAD)
        return [TArg(o_f, blk, lambda h, i: (i, h)), TArg(o_b, blk, lambda h, i: (i, h), grad=False),
                TArg(gate_arr, blk, lambda h, i: (i, goff + h)),
                TArg(nw_arr, (1, HEAD), lambda h, i: (0, 0), "par", (0, 1))]

    def readout(name, o_f, o_b, gate_arr, gate_off, nw_arr, n_heads):
        per, nblk, _ = ro_plan(gate_off, n_heads)
        (o,) = tile_fwd(name, f_readout(per), (nblk, nt), ro_args(o_f, o_b, gate_arr, gate_off, nw_arr, n_heads),
                        [((t, n_heads * HEAD), BF16, (ROW_TILE, per * HEAD), lambda hh, i: (i, hh))])
        return o

    def readout_bwd(name, o_f, o_b, gate_arr, gate_off, nw_arr, n_heads, dout):
        per, nblk, _ = ro_plan(gate_off, n_heads)
        return tile_bwd(name, f_readout(per), (nblk, nt), ro_args(o_f, o_b, gate_arr, gate_off, nw_arr, n_heads),
                        [(dout, (ROW_TILE, per * HEAD), lambda hh, i: (i, hh))])

    def even_fwd(tag, xs, layer):
        h = norm_mod(f"nm1_{tag}", xs, layer, 0, 0, 1)
        p = matmul(f"win_{tag}", h, w_ev_in[0], "nn", slabs=w_ev_in[1])
        outs, saves = [], []
        for direction in (0, 1):
            (o,), sv = scan_fwd(f"hgrn_{tag}_{direction}", f_hgrn2(direction == 1, hb), n_hblk, n_a, hg_args(p, direction),
                                [((t, aw), F32, (A_CHUNK, hb * HEAD), lambda hh, i, tok=a_tok(direction == 1): (tok(i), hh))], hb)
            outs.append(o)
            saves.append(sv)
        a_out = readout(f"ro_{tag}", outs[0], outs[1], p, 4 * n_ah, ev_a_norm, n_ah)
        (pooled,) = tile_fwd(f"pool_{tag}", f_pool, (n_grp, nt), pool_args(p),
                             [((t, aw), BF16, (ROW_TILE, dg), lambda g, i: (i, g))])
        cat = assemble(f"cat_{tag}", [[(a_out, aw, 0)], [(pooled, aw, 0)]], BF16)
        y = matmul(f"wout_{tag}", cat, w_ev_out, "nn")
        xn = gate_res(f"gr1_{tag}", xs, y, layer, 1, 2)
        return xn, (xs, h, p, outs, saves, cat, y)

    def even_bwd(tag, saved, layer, dxn, acc):
        xs, h, p, outs, saves, cat, y = saved
        dy, dnw1, dmod_a = gate_res_bwd(f"gr1b_{tag}", xs, y, layer, 1, 2, dxn)
        dcat = matmul(f"woutd_{tag}", dy, w_ev_out, "nt")
        acc["ev_w_out"] = row_grad(f"woutw_{tag}", cat, dy)
        do, dgate, d_anorm = readout_bwd(f"rob_{tag}", outs[0], outs[1], p, 4 * n_ah, ev_a_norm, n_ah, dcat)
        du, d_pw, d_ps = tile_bwd(f"poolb_{tag}", f_pool, (n_grp, nt), pool_args(p),
                                  [(dcat, (ROW_TILE, dg), lambda g, i: (i, n_grp + g))])
        dq, df, di, dlb = [], [], [], []
        for direction in (0, 1):
            r = scan_bwd(f"hgrnb_{tag}_{direction}", f_hgrn2(direction == 1, hb), n_hblk, n_a, hg_args(p, direction),
                         saves[direction],
                         [(do, (A_CHUNK, hb * HEAD), lambda hh, i, tok=a_tok(direction == 1): (tok(i), hh))])
            dq.append(r[0])
            df.append(r[1])
            di.append(r[2])
            dlb.append(r[3])
        sec = lambda arr, s: (arr, aw, s)
        dp = assemble(f"dp_{tag}", [[sec(dq[0], 0), sec(dq[1], 0)], [sec(df[0], 1)], [sec(df[1], 2)],
                                    [sec(di[0], 3), sec(di[1], 3)], [sec(dgate, 4)], [sec(du, 5)]], BF16)
        dh = matmul(f"wind_{tag}", dp, w_ev_in[0], "nt", slabs=w_ev_in[1])
        acc["ev_w_in"] = col_grad(f"winw_{tag}", h, dp, w_ev_in[1])
        dxs, dnw0, dmod_b = norm_mod_bwd(f"nm1b_{tag}", xs, layer, 0, 0, 1, dh, dxn)
        acc["norm_w"][layer][0] = dnw0
        acc["norm_w"][layer][1] = dnw1
        acc["mods"][layer].extend([dmod_a, dmod_b])
        acc["ev_a_norm"] = d_anorm
        acc["ev_pool_w"] = d_pw
        acc["ev_pool_scale"] = d_ps
        acc["lb0"] = jnp.stack([dlb[0][0], dlb[1][1]]).reshape(2, aw)
        return dxs

    n_c = t // C_CHUNK
    ncc = n_ctx // C_CHUNK

    def c_tok(rev):
        if not rev:
            return lambda i: i
        return lambda i: jnp.where(i < ncc, ncc - 1 - i, n_c + ncc - 1 - i)

    alog = od_A_log[0].reshape(2, n_kh, 2, 1)
    dtb = od_dt_bias[0].reshape(2, n_kh, 2, 1)

    khb = _pick(n_kh, (HEADS_PER_STEP, 2, 1))
    n_kblk = n_kh // khb

    def gd_args(z, gates, direction):
        tok = c_tok(direction == 1)
        gblk = (None, khb, None, 2, C_CHUNK)
        sblk = (None, khb, 2, 1)
        return [TArg(z, (C_CHUNK, khb * HEAD), lambda kb, i: (tok(i), kb)),
                TArg(z, (C_CHUNK, khb * HEAD), lambda kb, i: (tok(i), n_kblk + kb)),
                TArg(z, (C_CHUNK, khb * 2 * HEAD), lambda kb, i: (tok(i), n_kblk + kb)),
                TArg(gates, gblk, lambda kb, i: (direction, kb, tok(i), 0, 0)),
                TArg(gates, gblk, lambda kb, i: (2 + direction, kb, tok(i), 0, 0)),
                TArg(alog, sblk, lambda kb, i: (direction, kb, 0, 0), "par", (1,)),
                TArg(dtb, sblk, lambda kb, i: (direction, kb, 0, 0), "par", (1,))]

    def odd_fwd(tag, xs, layer):
        h = norm_mod(f"nm1_{tag}", xs, layer, 0, 0, 1)
        pm = matmul(f"win_{tag}", h, w_od_main, "nn")
        pg = matmul(f"wgate_{tag}", h, w_od_gate, "nn")
        z = conv_fwd(f"conv_{tag}", pm, cv_full, 2 * kw + vw, nct)
        gates = pg.reshape(n_c, C_CHUNK, 4, n_kh, 2).transpose(2, 3, 0, 4, 1)
        outs, saves = [], []
        for direction in (0, 1):
            xrows = 2 * khb * C_CHUNK
            (o, xinv), sv = scan_fwd(
                f"gdn_{tag}_{direction}", f_gdn(direction == 1, khb, False), n_kblk, n_c, gd_args(z, gates, direction),
                [((t, vw), F32, (C_CHUNK, khb * 2 * HEAD), lambda kb, i, tok=c_tok(direction == 1): (tok(i), kb)),
                 ((n_kblk, n_c, xrows, C_CHUNK), F32, (None, None, xrows, C_CHUNK), lambda kb, i: (kb, i, 0, 0))],
                2 * khb)
            outs.append(o)
            saves.append((sv, xinv))
        n_vh = 2 * n_kh
        yo = readout(f"ro_{tag}", outs[0], outs[1], pm, 2 * n_kh + n_vh, od_norm, n_vh)
        y = matmul(f"wout_{tag}", yo, w_od_out, "nn")
        xn = gate_res(f"gr1_{tag}", xs, y, layer, 1, 2)
        return xn, (xs, h, pm, z, gates, outs, saves, yo, y)

    def odd_bwd(tag, saved, layer, dxn, acc):
        xs, h, pm, z, gates, outs, saves, yo, y = saved
        n_vh = 2 * n_kh
        dy, dnw1, dmod_a = gate_res_bwd(f"gr1b_{tag}", xs, y, layer, 1, 2, dxn)
        dyo = matmul(f"woutd_{tag}", dy, w_od_out, "nt")
        acc["od_w_out"] = row_grad(f"woutw_{tag}", yo, dy)
        do, dzg, d_onorm = readout_bwd(f"rob_{tag}", outs[0], outs[1], pm, 2 * n_kh + n_vh, od_norm, n_vh, dyo)
        dq, dk, dv, dga, dgb, dal, ddt = [], [], [], [], [], [], []
        for direction in (0, 1):
            sv, xinv = saves[direction]
            xarg = TArg(xinv, (None, None, 2 * khb * C_CHUNK, C_CHUNK), lambda kb, i: (kb, i, 0, 0), "const")
            r = scan_bwd(f"gdnb_{tag}_{direction}", f_gdn(direction == 1, khb, True), n_kblk, n_c,
                         gd_args(z, gates, direction) + [xarg], sv,
                         [(do, (C_CHUNK, khb * 2 * HEAD), lambda kb, i, tok=c_tok(direction == 1): (tok(i), kb))])
            for lst, v_ in zip((dq, dk, dv, dga, dgb, dal, ddt), r):
                lst.append(v_)
        dz = assemble(f"dz_{tag}", [[(dq[0], kw, 0), (dq[1], kw, 0)], [(dk[0], kw, 1), (dk[1], kw, 1)],
                                    [(dv[0], vw, 1), (dv[1], vw, 1)]], F32)
        du, d_conv = conv_bwd(f"convb_{tag}", pm, cv_full, dz, 2 * kw + vw, nct)
        dpm = assemble(f"dpm_{tag}", [[(du, 2 * kw + vw, 0)], [(dzg, vw, 2)]], BF16)
        dgates = jnp.stack([dga[0][0], dga[1][1], dgb[0][2], dgb[1][3]])
        dpg = dgates.transpose(2, 4, 0, 1, 3).reshape(t, n_gate).astype(BF16)
        dh = matmul(f"wgated_{tag}", dpg, w_od_gate, "nt")
        dh = matmul(f"wind_{tag}", dpm, w_od_main, "nt", add=dh)
        dw_in = jnp.concatenate([matmul(f"winw_{tag}", h, dpm, "tn", out_dtype=BF16),
                                 matmul(f"wgatew_{tag}", h, dpg, "tn", out_dtype=BF16)], axis=1)
        acc["od_w_in"] = dw_in.reshape(d, N_DEV, dw_in.shape[1] // N_DEV).transpose(1, 0, 2)
        dxs, dnw0, dmod_b = norm_mod_bwd(f"nm1b_{tag}", xs, layer, 0, 0, 1, dh, dxn)
        acc["norm_w"][layer][0] = dnw0
        acc["norm_w"][layer][1] = dnw1
        acc["mods"][layer].extend([dmod_a, dmod_b])
        acc["od_norm"] = d_onorm
        acc["od_conv"] = d_conv
        acc["od_A_log"] = jnp.stack([dal[0][0], dal[1][1]]).reshape(1, 2, n_vh)
        acc["od_dt_bias"] = jnp.stack([ddt[0][0], ddt[1][1]]).reshape(1, 2, n_vh)
        return dxs

    xs0 = jnp.concatenate([ctx[0], x[0]], axis=0)
    xs1, sv_e = even_fwd("l0", xs0, 0)
    g_w13a, g_w2a = gathered(ag_ffn0, xs1, "ag_wait_ffn0")
    w13[0], w2[0] = col_weight(g_w13a), rows_natural(g_w2a)
    xs2, sv_f0 = ffn_fwd("l0", xs1, 0)
    g_od_in, g_od_out, g_w13b, g_w2b = gathered(ag_l1, xs2, "ag_wait_l1")
    w_od_in = cols_natural(g_od_in)
    w_od_main, w_od_gate = w_od_in[:, :2 * kw + 2 * vw], w_od_in[:, 2 * kw + 2 * vw:]
    w_od_out = rows_natural(g_od_out)
    w13[1], w2[1] = col_weight(g_w13b), rows_natural(g_w2b)
    xs3, sv_o = odd_fwd("l1", xs2, 1)
    xs4, sv_f1 = ffn_fwd("l1", xs3, 1)
    loss_loc, dxs = loss_kernel("loss", xs4, loss_target[0], nct)
    loss = lax.psum(loss_loc, ("x", "y", "c"))

    acc = {"norm_w": [[None] * 4 for _ in range(depth)], "mods": [[] for _ in range(depth)],
           "ffn_w13": [None] * depth, "ffn_w2": [None] * depth}
    dxs = ffn_bwd("l1", sv_f1, 1, dxs, acc)
    rs_ffn1 = exchange_start("rs_start_ffn1", [acc["ffn_w13"][1], acc["ffn_w2"][1]], True, dxs)
    mods[1] = mods[1] + rs_ffn1[4][0, 0]
    dxs = odd_bwd("l1", sv_o, 1, dxs, acc)
    rs_od = exchange_start("rs_start_od", [acc["od_w_in"], acc["od_w_out"]], True, dxs)
    mods[0] = mods[0] + rs_od[4][0, 0]
    dxs = ffn_bwd("l0", sv_f0, 0, dxs, acc)
    rs_ffn0 = exchange_start("rs_start_ffn0", [acc["ffn_w13"][0], acc["ffn_w2"][0]], True, dxs)
    mods[0] = mods[0] + rs_ffn0[4][0, 0]
    dxs = even_bwd("l0", sv_e, 0, dxs, acc)
    rs_ev = exchange_start("rs_start_ev", [acc["ev_w_in"], acc["ev_w_out"]], True, dxs)
    grad_x = dxs[n_ctx:].reshape(1, seq, d)

    (d_lb_slots) = tile_bwd("lb_bwd", f_lb(0), (1,), lb_slots, [(acc["lb0"], (2, aw), lambda i: (0, 0))])
    d_ev_lb = jnp.stack(d_lb_slots, axis=1)

    dmods = jnp.stack([functools.reduce(jnp.add, acc["mods"][layer]) for layer in range(depth)])
    (dm_all,) = all_gather("ag_dmod", [dmods.reshape(depth * 2 * 6, d)])
    dm_all = dm_all.reshape(N_DEV, depth, 2, 6 * d)
    dm_cols = lax.dynamic_slice_in_dim(dm_all, me * ada_loc, ada_loc, axis=3)
    dm_loc = jnp.concatenate([dm_cols[:, :, 1].transpose(1, 0, 2), dm_cols[:, :, 0].transpose(1, 0, 2)], axis=1)
    d_cctx_part, d_w_ada, d_b_loc = tile_bwd("ada_bwd", f_ada, ada_grid, ada_args,
                                             [(dm_loc, (None, 2 * N_DEV, ada_cb), lambda l, j: (l, 0, j))])

    d_b_full = lax.dynamic_update_slice_in_dim(jnp.zeros_like(b_ada), d_b_loc.reshape(depth, ada_loc), me * ada_loc, axis=1)
    d_nw = jnp.stack([jnp.stack([acc["norm_w"][layer][s].reshape(d) for s in range(4)]) for layer in range(depth)])
    small_grads = [d_cctx_part.reshape(d), d_b_full, d_nw, d_ev_lb, acc["ev_a_norm"], acc["ev_pool_w"],
                   acc["ev_pool_scale"], acc["od_conv"], acc["od_A_log"], acc["od_dt_bias"], acc["od_norm"]]
    sg_shapes = [a.shape for a in small_grads]
    (sg,) = all_gather("ag_smallgrads", [_pack(small_grads, F32, FLAT_W, SUBLANES)])
    sg_sum = sum_leading("sum_smallgrads", sg, F32)
    (g_cctx, g_bada, g_nw, g_lb, g_anorm, g_pw, g_ps, g_conv, g_alog, g_dtb, g_onorm) = _unpack(sg_sum, sg_shapes)

    def my_cols(full, axis):
        loc = full.shape[axis] // N_DEV
        return lax.dynamic_slice_in_dim(full, me * loc, loc, axis=axis)

    grads = {
        "c_ctx": g_cctx, "w_ada": d_w_ada, "b_ada": g_bada, "norm_w": my_cols(g_nw, 2), "ev_lb": my_cols(g_lb, 2),
        "ev_a_norm": g_anorm, "ev_pool_w": my_cols(g_pw, 1)[None], "ev_pool_scale": g_ps,
        "od_conv": my_cols(g_conv, 1)[None], "od_A_log": g_alog, "od_dt_bias": g_dtb, "od_norm": g_onorm,
    }

    def reduced(started, tags_, name):
        srcs, lands = exchange_wait(name, started, sg_sum, True)
        out = []
        for tg, s_, l_ in zip(tags_, srcs, lands):
            own = lax.dynamic_index_in_dim(s_, me, 0, keepdims=True)
            out.append(sum_leading(f"rs_sum_{tg}", lax.dynamic_update_slice_in_dim(l_, own, me, 0), F32))
        return out

    g_w13b, g_w2b = reduced(rs_ffn1, ["w13b", "w2b"], "rs_wait_ffn1")
    g_od_in, g_od_out = reduced(rs_od, ["od_in", "od_out"], "rs_wait_od")
    g_w13a, g_w2a = reduced(rs_ffn0, ["w13a", "w2a"], "rs_wait_ffn0")
    g_ev_in, g_ev_out = reduced(rs_ev, ["ev_in", "ev_out"], "rs_wait_ev")
    grads["ev_w_in"], grads["ev_w_out"], grads["od_w_in"], grads["od_w_out"] = (g_ev_in[None], g_ev_out[None],
                                                                                g_od_in[None], g_od_out[None])
    grads["ffn_w13"] = jnp.stack([g_w13a, g_w13b])
    grads["ffn_w2"] = jnp.stack([g_w2a, g_w2b])

    big_names = ["w_ada", "ev_w_in", "ev_w_out", "od_w_in", "od_w_out", "ffn_w13", "ffn_w2"]
    small_names = [n_ for n_ in names if n_ not in big_names]
    gl = {n_: grads[n_].reshape(wts[n_].shape) for n_ in names}
    delta, new_m, new_v = {}, {}, {}
    for n_ in big_names:
        shp = wts[n_].shape
        res = adamw(f"adamw_{n_}", _rows2d(gl[n_]), _rows2d(wts[n_]), _rows2d(mom1[n_]), _rows2d(mom2[n_]))
        delta[n_], new_m[n_], new_v[n_] = (r_.reshape(shp) for r_ in res)
    shapes = [wts[n_].shape for n_ in small_names]
    pk = lambda dct: _pack([dct[n_] for n_ in small_names], F32, FLAT_W, SUBLANES)
    res = adamw("adamw_small", pk(gl), pk(wts), pk(mom1), pk(mom2))
    for dct, r_ in zip((delta, new_m, new_v), res):
        for n_, a_ in zip(small_names, _unpack(r_, shapes)):
            dct[n_] = a_
    return (loss, grad_x, *[gl[n_] for n_ in names], *[delta[n_] for n_ in names], *[new_m[n_] for n_ in names],
            *[new_v[n_] for n_ in names])
```

```python
import functools
from typing import Any, NamedTuple

import numpy as np

import jax
import jax.numpy as jnp
from jax import lax
from jax.experimental import pallas as pl
from jax.experimental.pallas import tpu as pltpu

F32 = jnp.float32
BF16 = jnp.bfloat16
MESH = pl.DeviceIdType.MESH
N_DEV = 8

EPS = 1e-6
GRID_W = 64
HEAD = 128
A_CHUNK = 32
C_CHUNK = 64
C_CONV = 4
POOL_WINDOWS = (2, 4, 8, 16)
ADAM_LR, ADAM_B1, ADAM_B2, ADAM_EPS, ADAM_WD, ADAM_STEP = 0.001, 0.9, 0.999, 1e-08, 0.01, 10

VMEM_LIMIT_BYTES = 56 * 1024 * 1024
LANES = 128
SUBLANES = 8
ROW_TILE = 256
FLAT_W = 1024
FLAT_ROWS = 512
TM_PREFS = (1056, 768, 512, 256, 128, 64, 32, 16)
TN_PREFS = (512, 384, 1408, 256, 128)
TK_PREFS = (2048, 2816, 1408, 1024, 768, 512, 384, 256, 128)
TO_PREFS = (1024, 1408, 768, 704, 512, 384, 256, 128)
HEADS_PER_STEP = 4


def _pick(dim, prefs):
    for p in prefs:
        if p <= dim and dim % p == 0:
            return p
    return dim


def _cparams(ngrid):
    return pltpu.CompilerParams(dimension_semantics=("arbitrary",) * ngrid, vmem_limit_bytes=VMEM_LIMIT_BYTES)


def _sds(shape, dtype):
    return jax.ShapeDtypeStruct(tuple(shape), dtype)


def _split(x):
    hi = x.astype(BF16)
    return hi, (x - hi.astype(F32)).astype(BF16)


def _dot(a, b, ca, cb, hi):
    dims = (((ca,), (cb,)), ((), ()))
    dot = lambda u, v: lax.dot_general(u, v, dims, preferred_element_type=F32)
    if hi:
        (ah, al), (bh, bl) = _split(a.astype(F32)), _split(b.astype(F32))
        return dot(ah, bh) + (dot(ah, bl) + dot(al, bh))
    return dot(a.astype(BF16), b.astype(BF16))


@functools.partial(jax.custom_vjp, nondiff_argnums=(2, 3, 4))
def mm(a, b, ca=1, cb=0, hi=False):
    return _dot(a, b, ca, cb, hi)


def _mm_fwd(a, b, ca, cb, hi):
    return _dot(a, b, ca, cb, hi), (a, b)


def _mm_bwd(ca, cb, hi, res, g):
    a, b = res
    da = _dot(g, b, 1, 1 - cb, hi) if ca == 1 else _dot(b, g, 1 - cb, 1, hi)
    db = _dot(a, g, 1 - ca, 0, hi) if cb == 0 else _dot(g, a, 0, 1 - ca, hi)
    return da, db


mm.defvjp(_mm_fwd, _mm_bwd)


def _iota2(n, m, axis):
    return lax.broadcasted_iota(jnp.int32, (n, m), axis)


class TArg(NamedTuple):
    arr: Any
    block: tuple
    imap: Any
    kind: str = "row"
    acc: tuple = ()
    gdtype: Any = F32
    grad: bool = True


def _load(ref):
    v = ref[...]
    return v.astype(F32) if jnp.issubdtype(v.dtype, jnp.floating) else v


def tile_fwd(name, f, grid, args, outs):
    n_in, ng = len(args), len(grid)

    def body(*refs):
        pids = tuple(pl.program_id(k) for k in range(ng))
        res = f(pids, *[_load(r) for r in refs[:n_in]])
        for r, v in zip(refs[n_in:], res):
            r[...] = v.astype(r.dtype)

    return pl.pallas_call(
        body, grid=grid, name=name,
        in_specs=[pl.BlockSpec(a.block, a.imap) for a in args],
        out_specs=[pl.BlockSpec(b, im) for (_, _, b, im) in outs],
        out_shape=[_sds(s, d) for (s, d, _, _) in outs],
        compiler_params=_cparams(ng),
    )(*[a.arr for a in args])


def _store_grads(args, diff, pids, g_refs, d):
    for k, gr, dv in zip(diff, g_refs, d):
        a = args[k]
        if a.kind == "row" or not a.acc:
            gr[...] = dv.astype(gr.dtype)
        else:
            first = pids[a.acc[0]] == 0
            for ax in a.acc[1:]:
                first = jnp.logical_and(first, pids[ax] == 0)

            @pl.when(first)
            def _(gr=gr, dv=dv):
                gr[...] = dv.astype(gr.dtype)

            @pl.when(jnp.logical_not(first))
            def _(gr=gr, dv=dv):
                gr[...] += dv.astype(gr.dtype)


def tile_bwd(name, f, grid, args, cts):
    n_in, n_ct, ng = len(args), len(cts), len(grid)
    diff = [k for k, a in enumerate(args) if a.kind != "const" and a.grad]

    def body(*refs):
        pids = tuple(pl.program_id(k) for k in range(ng))
        vals = [_load(r) for r in refs[:n_in]]

        def g(*dv):
            full = list(vals)
            for k, v in zip(diff, dv):
                full[k] = v
            return tuple(f(pids, *full))

        _, vjp = jax.vjp(g, *[vals[k] for k in diff])
        d = vjp(tuple(_load(r) for r in refs[n_in:n_in + n_ct]))
        _store_grads(args, diff, pids, refs[n_in + n_ct:], d)

    return pl.pallas_call(
        body, grid=grid, name=name,
        in_specs=[pl.BlockSpec(a.block, a.imap) for a in args] + [pl.BlockSpec(b, im) for (_, b, im) in cts],
        out_specs=[pl.BlockSpec(args[k].block, args[k].imap) for k in diff],
        out_shape=[_sds(args[k].arr.shape, args[k].gdtype) for k in diff],
        compiler_params=_cparams(ng),
    )(*[a.arr for a in args], *[c[0] for c in cts])


def scan_fwd(name, f, n_heads, n_steps, args, outs, n_state):
    n_in, n_out = len(args), len(outs)
    sblock = (None, None, HEAD, HEAD)

    def body(*refs):
        in_refs = refs[:n_in]
        out_refs = refs[n_in:n_in + n_out]
        save_refs = refs[n_in + n_out:n_in + n_out + n_state]
        s_refs = refs[n_in + n_out + n_state:]

        @pl.when(pl.program_id(1) == 0)
        def _():
            for s in s_refs:
                s[...] = jnp.zeros_like(s)

        states = tuple(s[...] for s in s_refs)
        for sv, s in zip(save_refs, states):
            sv[...] = s
        new_states, res = f(states, *[_load(r) for r in in_refs])
        for s, v in zip(s_refs, new_states):
            s[...] = v
        for r, v in zip(out_refs, res):
            r[...] = v.astype(r.dtype)

    res = pl.pallas_call(
        body, grid=(n_heads, n_steps), name=name,
        in_specs=[pl.BlockSpec(a.block, a.imap) for a in args],
        out_specs=[pl.BlockSpec(b, im) for (_, _, b, im) in outs]
        + [pl.BlockSpec(sblock, lambda h, i: (h, i, 0, 0))] * n_state,
        out_shape=[_sds(s, d) for (s, d, _, _) in outs] + [_sds((n_heads, n_steps, HEAD, HEAD), F32)] * n_state,
        scratch_shapes=[pltpu.VMEM((HEAD, HEAD), F32)] * n_state,
        compiler_params=_cparams(2),
    )(*[a.arr for a in args])
    return res[:n_out], res[n_out:]


def scan_bwd(name, f, n_heads, n_steps, args, saves, cts):
    n_in, n_ct, n_state = len(args), len(cts), len(saves)
    diff = [k for k, a in enumerate(args) if a.kind != "const" and a.grad]
    sblock = (None, None, HEAD, HEAD)

    def rv(im):
        return lambda h, i: im(h, n_steps - 1 - i)

    def body(*refs):
        in_refs = refs[:n_in]
        save_refs = refs[n_in:n_in + n_state]
        ct_refs = refs[n_in + n_state:n_in + n_state + n_ct]
        g_refs = refs[n_in + n_state + n_ct:n_in + n_state + n_ct + len(diff)]
        ds_refs = refs[n_in + n_state + n_ct + len(diff):]
        pids = (pl.program_id(0), pl.program_id(1))

        @pl.when(pids[1] == 0)
        def _():
            for s in ds_refs:
                s[...] = jnp.zeros_like(s)

        vals = [_load(r) for r in in_refs]

        def g(states, *dv):
            full = list(vals)
            for k, v in zip(diff, dv):
                full[k] = v
            new_states, res = f(states, *full)
            return tuple(new_states), tuple(res)

        _, vjp = jax.vjp(g, tuple(s[...] for s in save_refs), *[vals[k] for k in diff])
        d = vjp((tuple(s[...] for s in ds_refs), tuple(_load(r) for r in ct_refs)))
        for s, v in zip(ds_refs, d[0]):
            s[...] = v
        _store_grads(args, diff, pids, g_refs, d[1:])

    return pl.pallas_call(
        body, grid=(n_heads, n_steps), name=name,
        in_specs=[pl.BlockSpec(a.block, rv(a.imap)) for a in args]
        + [pl.BlockSpec(sblock, rv(lambda h, i: (h, i, 0, 0)))] * n_state
        + [pl.BlockSpec(b, rv(im)) for (_, b, im) in cts],
        out_specs=[pl.BlockSpec(args[k].block, rv(args[k].imap)) for k in diff],
        out_shape=[_sds(args[k].arr.shape, args[k].gdtype) for k in diff],
        scratch_shapes=[pltpu.VMEM((HEAD, HEAD), F32)] * n_state,
        compiler_params=_cparams(2),
    )(*[a.arr for a in args], *saves, *[c[0] for c in cts])


def matmul(name, a, b, mode, add=None, out_dtype=F32, slabs=False):
    o_spec = None
    if mode == "nn":
        m, k = a.shape
        ns = b.shape[2] if slabs else b.shape[1]
        n = N_DEV * ns if slabs else ns
        to_m, to_n, tr = _pick(m, TM_PREFS), _pick(ns, TN_PREFS), _pick(k, TK_PREFS)
        nb = ns // to_n
        grid = (m // to_m, n // to_n, k // tr)
        a_spec = pl.BlockSpec((to_m, tr), lambda i, j, l: (i, l))
        if slabs:
            b_spec = pl.BlockSpec((None, tr, to_n), lambda i, j, l: (j // nb, l, j % nb))
        else:
            b_spec = pl.BlockSpec((tr, to_n), lambda i, j, l: (l, j))
        dims, oshape = (1, 0), (m, n)
    elif mode == "nt":
        m, n = a.shape
        k = b.shape[1] if slabs else b.shape[0]
        ns = n // N_DEV if slabs else n
        to_m, to_n, tr = _pick(m, TM_PREFS), _pick(k, TO_PREFS), _pick(ns, TK_PREFS)
        nb = ns // tr
        grid = (m // to_m, k // to_n, n // tr)
        a_spec = pl.BlockSpec((to_m, tr), lambda i, j, l: (i, l))
        if slabs:
            b_spec = pl.BlockSpec((None, to_n, tr), lambda i, j, l: (l // nb, j, l % nb))
        else:
            b_spec = pl.BlockSpec((to_n, tr), lambda i, j, l: (j, l))
        dims, oshape = (1, 1), (m, k)
    else:
        (t, k), n = a.shape, b.shape[1]
        ns = n // N_DEV if slabs else n
        to_m, to_n, tr = _pick(k, TO_PREFS), _pick(ns, TO_PREFS), _pick(t, TM_PREFS)
        nb = ns // to_n
        grid = (k // to_m, n // to_n, t // tr)
        a_spec = pl.BlockSpec((tr, to_m), lambda i, j, l: (l, i))
        b_spec = pl.BlockSpec((tr, to_n), lambda i, j, l: (l, j))
        dims, oshape = (0, 0), (k, n)
        if slabs:
            o_spec = pl.BlockSpec((None, to_m, to_n), lambda i, j, l: (j // nb, i, j % nb))
            oshape = (N_DEV, k, ns)
    n_red = grid[2]
    if o_spec is None:
        o_spec = pl.BlockSpec((to_m, to_n), lambda i, j, l: (i, j))
    has_add = add is not None

    def body(a_ref, b_ref, *rest):
        add_ref = rest[0] if has_add else None
        o_ref = rest[1] if has_add else rest[0]
        part = lax.dot_general(a_ref[...].astype(BF16), b_ref[...].astype(BF16),
                               (((dims[0],), (dims[1],)), ((), ())), preferred_element_type=F32)

        def finish(v):
            if has_add:
                v = v + add_ref[...]
            o_ref[...] = v.astype(o_ref.dtype)

        if n_red == 1:
            finish(part)
        else:
            acc = rest[-1]
            step = pl.program_id(2)

            @pl.when(step == 0)
            def _():
                acc[...] = part

            @pl.when(step > 0)
            def _():
                acc[...] += part

            @pl.when(step == n_red - 1)
            def _():
                finish(acc[...])

    return pl.pallas_call(
        body, grid=grid, name=name,
        in_specs=[a_spec, b_spec] + ([o_spec] if has_add else []),
        out_specs=o_spec, out_shape=_sds(oshape, out_dtype),
        scratch_shapes=[pltpu.VMEM((to_m, to_n), F32)] if n_red > 1 else [],
        compiler_params=_cparams(3),
    )(a, b, *([add] if has_add else []))


def assemble(name, pieces, out_dtype):
    flat = [s for piece in pieces for s in piece]
    t = flat[0][0].shape[0]
    widths = [piece[0][1] for piece in pieces]

    def body(*refs):
        o_ref, k, off = refs[-1], 0, 0
        for piece, w in zip(pieces, widths):
            v = refs[k][...].astype(F32)
            k += 1
            for _ in piece[1:]:
                v = v + refs[k][...].astype(F32)
                k += 1
            o_ref[:, off:off + w] = v.astype(o_ref.dtype)
            off += w

    tr = ROW_TILE // 2
    return pl.pallas_call(
        body, grid=(t // tr,), name=name,
        in_specs=[pl.BlockSpec((tr, w), functools.partial(lambda i, cb: (i, cb), cb=cb)) for (_, w, cb) in flat],
        out_specs=pl.BlockSpec((tr, sum(widths)), lambda i: (i, 0)),
        out_shape=_sds((t, sum(widths)), out_dtype),
        compiler_params=_cparams(1),
    )(*[s[0] for s in flat])


ELEM_ROWS = (128, 64, 32, 16, 8)


def _rows2d(a, lead=0):
    return a.reshape(a.shape[:lead] + (-1, a.shape[-1]))


def sum_leading(name, arr, out_dtype):
    k, rows, w = arr.shape
    tr = _pick(rows, ELEM_ROWS)

    def body(a_ref, o_ref):
        v = a_ref[0].astype(F32)
        for j in range(1, k):
            v = v + a_ref[j].astype(F32)
        o_ref[...] = v.astype(o_ref.dtype)

    return pl.pallas_call(
        body, grid=(rows // tr,), name=name,
        in_specs=[pl.BlockSpec((k, tr, w), lambda i: (0, i, 0))],
        out_specs=pl.BlockSpec((tr, w), lambda i: (i, 0)),
        out_shape=_sds((rows, w), out_dtype), compiler_params=_cparams(1),
    )(arr)


def add_own(name, g8, got, core, out_dtype):
    _, rows, w = g8.shape
    tr = _pick(rows, ELEM_ROWS)

    def body(core_ref, a_ref, b_ref, o_ref):
        o_ref[...] = (a_ref[...] + b_ref[...]).astype(o_ref.dtype)

    spec = pl.BlockSpec((None, tr, w), lambda q, i, core_ref: (q, i, 0))
    return pl.pallas_call(
        body, name=name,
        grid_spec=pltpu.PrefetchScalarGridSpec(
            num_scalar_prefetch=1, grid=(4, rows // tr),
            in_specs=[pl.BlockSpec((None, tr, w), lambda q, i, core_ref: (2 * q + core_ref[0], i, 0)), spec],
            out_specs=spec),
        out_shape=_sds(got.shape, out_dtype), compiler_params=_cparams(2),
    )(core, g8, got)


def adamw(name, g, w, m, v):
    rows, wd = g.shape
    tr = _pick(rows, ELEM_ROWS)

    def body(g_ref, w_ref, m_ref, v_ref, d_ref, nm_ref, nv_ref):
        gv = g_ref[...]
        mn = ADAM_B1 * m_ref[...] + (1.0 - ADAM_B1) * gv
        vn = ADAM_B2 * v_ref[...] + (1.0 - ADAM_B2) * jnp.square(gv)
        m_hat = mn / (1.0 - ADAM_B1 ** ADAM_STEP)
        v_hat = vn / (1.0 - ADAM_B2 ** ADAM_STEP)
        d_ref[...] = -ADAM_LR * (m_hat / (jnp.sqrt(v_hat) + ADAM_EPS) + ADAM_WD * w_ref[...])
        nm_ref[...] = mn
        nv_ref[...] = vn

    spec = pl.BlockSpec((tr, wd), lambda i: (i, 0))
    return pl.pallas_call(
        body, grid=(rows // tr,), name=name, in_specs=[spec] * 4, out_specs=[spec] * 3,
        out_shape=[_sds(g.shape, F32)] * 3, compiler_params=_cparams(1),
    )(g, w, m, v)


def loss_kernel(name, xs, target, n_ctx_tiles):
    t, d = xs.shape
    nt = t // ROW_TILE

    def body(x_ref, t_ref, dx_ref, l_ref):
        i = pl.program_id(0)
        is_lat = i >= n_ctx_tiles
        err = jnp.where(is_lat, x_ref[...] - t_ref[...], 0.0)
        dx_ref[...] = err / d
        part = 0.5 * jnp.sum(jnp.mean(jnp.square(err), axis=-1, keepdims=True), axis=0, keepdims=True)

        @pl.when(i == 0)
        def _():
            l_ref[...] = jnp.zeros_like(l_ref)

        l_ref[...] += jnp.broadcast_to(part, l_ref.shape)

    dx, l = pl.pallas_call(
        body, grid=(nt,), name=name,
        in_specs=[pl.BlockSpec((ROW_TILE, d), lambda i: (i, 0)),
                  pl.BlockSpec((ROW_TILE, d), lambda i: (jnp.maximum(i - n_ctx_tiles, 0), 0))],
        out_specs=[pl.BlockSpec((ROW_TILE, d), lambda i: (i, 0)), pl.BlockSpec((SUBLANES, LANES), lambda i: (0, 0))],
        out_shape=[_sds((t, d), F32), _sds((SUBLANES, LANES), F32)], compiler_params=_cparams(1),
    )(xs, target)
    return l[0, 0], dx


CONV_COLS = 2048
CONV_LEFT = C_CONV // 2


def _conv_halo_specs(t, n_ctx_tiles):
    nt = t // ROW_TILE
    per = ROW_TILE // SUBLANES
    cur = pl.BlockSpec((ROW_TILE, CONV_COLS), lambda j, i: (i, j))
    prev = pl.BlockSpec((SUBLANES, CONV_COLS), lambda j, i: (jnp.maximum(i * per - 1, 0), j))
    nxt = pl.BlockSpec((SUBLANES, CONV_COLS), lambda j, i: (jnp.minimum((i + 1) * per, nt * per - 1), j))
    return cur, prev, nxt


def _fill_ext(ext, prev_ref, cur_ref, next_ref, i, nt, n_ctx_tiles):
    has_prev = jnp.logical_and(i != 0, i != n_ctx_tiles)
    has_next = jnp.logical_and(i != n_ctx_tiles - 1, i != nt - 1)
    ext[0:SUBLANES, :] = jnp.where(has_prev, prev_ref[...], 0.0)
    ext[SUBLANES:SUBLANES + ROW_TILE, :] = cur_ref[...]
    ext[SUBLANES + ROW_TILE:, :] = jnp.where(has_next, next_ref[...], 0.0)


def conv_fwd(name, p, w, width, n_ctx_tiles):
    t = p.shape[0]
    nt = t // ROW_TILE
    cur, prev, nxt = _conv_halo_specs(t, n_ctx_tiles)

    def body(c_ref, p_ref, n_ref, w_ref, o_ref, ext):
        _fill_ext(ext, p_ref, c_ref, n_ref, pl.program_id(1), nt, n_ctx_tiles)
        acc = None
        for j in range(C_CONV):
            term = ext[pl.ds(SUBLANES + j - CONV_LEFT, ROW_TILE), :] * w_ref[j:j + 1, :]
            acc = term if acc is None else acc + term
        o_ref[...] = acc

    return pl.pallas_call(
        body, grid=(width // CONV_COLS, nt), name=name,
        in_specs=[cur, prev, nxt, pl.BlockSpec((C_CONV, CONV_COLS), lambda j, i: (0, j))],
        out_specs=cur, out_shape=_sds((t, width), F32),
        scratch_shapes=[pltpu.VMEM((ROW_TILE + 2 * SUBLANES, CONV_COLS), F32)],
        compiler_params=_cparams(2),
    )(p, p, p, w)


def conv_bwd(name, p, w, dz, width, n_ctx_tiles):
    t = p.shape[0]
    nt = t // ROW_TILE
    cur, prev, nxt = _conv_halo_specs(t, n_ctx_tiles)

    def body(c_ref, p_ref, n_ref, dc_ref, dp_ref, dn_ref, w_ref, du_ref, dw_ref, ext, dext):
        i = pl.program_id(1)
        _fill_ext(ext, p_ref, c_ref, n_ref, i, nt, n_ctx_tiles)
        _fill_ext(dext, dp_ref, dc_ref, dn_ref, i, nt, n_ctx_tiles)
        dzc = dc_ref[...]
        @pl.when(i == 0)
        def _():
            dw_ref[...] = jnp.zeros_like(dw_ref)

        acc = None
        for j in range(C_CONV):
            term = dext[pl.ds(SUBLANES + CONV_LEFT - j, ROW_TILE), :] * w_ref[j:j + 1, :]
            acc = term if acc is None else acc + term
            dw_ref[j:j + 1, :] += jnp.sum(dzc * ext[pl.ds(SUBLANES + j - CONV_LEFT, ROW_TILE), :], axis=0, keepdims=True)
        du_ref[...] = acc

    wspec = pl.BlockSpec((C_CONV, CONV_COLS), lambda j, i: (0, j))
    return pl.pallas_call(
        body, grid=(width // CONV_COLS, nt), name=name,
        in_specs=[cur, prev, nxt, cur, prev, nxt, wspec],
        out_specs=[cur, wspec], out_shape=[_sds((t, width), F32), _sds((C_CONV, width), F32)],
        scratch_shapes=[pltpu.VMEM((ROW_TILE + 2 * SUBLANES, CONV_COLS), F32)] * 2,
        compiler_params=_cparams(2),
    )(p, p, p, dz, dz, dz, w)


def _rms(x, w):
    return x * lax.rsqrt(jnp.mean(x * x, axis=-1, keepdims=True) + EPS) * w


def _seg_mod(mods, is_ctx):
    return jnp.where(is_ctx, mods[0], mods[1])


def f_norm_mod(shift_i, scale_i, n_ctx_tiles, passthrough=False):
    def f(pids, x, nw, mods):
        m = _seg_mod(mods, pids[0] < n_ctx_tiles)
        h = _rms(x, nw) * (1.0 + m[scale_i:scale_i + 1]) + m[shift_i:shift_i + 1]
        return (h, x) if passthrough else (h,)
    return f


def f_gate_res(gate_i, n_ctx_tiles):
    def f(pids, x, y, nw, mods):
        m = _seg_mod(mods, pids[0] < n_ctx_tiles)
        return (x + m[gate_i:gate_i + 1] * _rms(y, nw),)
    return f


def f_swiglu(pids, gu):
    half = gu.shape[1] // 2
    return (jax.nn.silu(gu[:, :half]) * gu[:, half:],)


def f_readout(n_heads):
    def f(pids, o_a, o_b, gate, nw):
        cols = [slice(j * HEAD, (j + 1) * HEAD) for j in range(n_heads)]
        outs = _each(lambda cs: _rms(o_a[:, cs] + o_b[:, cs], nw) * jax.nn.silu(gate[:, cs]), cols)
        return (jnp.concatenate(outs, axis=1) if n_heads > 1 else outs[0],)
    return f


def f_pool(pids, u, pmat, pw, scale):
    d = mm(pmat, u, 1, 0, True) - u
    return (mm(d, pw) * scale,)


def f_lb(layer):
    def f(pids, *slots):
        top = slots[0]
        for s in slots[1:]:
            top = jnp.maximum(top, s)
        ex = [jnp.exp(s - top) for s in slots]
        tot = ex[0]
        for e in ex[1:]:
            tot = tot + e
        part = ex[0]
        for e in ex[1:layer + 1]:
            part = part + e
        return (part / tot,)
    return f


def f_ada(pids, c_all, c_ctx, w, b):
    c16 = jnp.concatenate([c_all, jnp.broadcast_to(c_ctx, c_all.shape)], axis=0)
    return (mm(jax.nn.silu(c16), w) + b,)


def _each(fn, *lists):
    return [fn(*xs) for xs in zip(*lists)]


def _hgrn2_heads(sts, qrs, frs, irs, lbs, rev):
    c = A_CHUNK
    mid = c - c // 2 if rev else c // 2 - 1
    ri, ci = _iota2(c, c, 0), _iota2(c, c, 1)
    incl = (ri <= ci) if rev else (ri >= ci)
    incl_f = incl.astype(F32)
    qs = _each(jax.nn.silu, qrs)
    log_fs = _each(lambda lb, fr: jnp.log(lb + (1.0 - lb) * jax.nn.sigmoid(fr)), lbs, frs)
    ks = _each(lambda lb, fr: (1.0 - lb) * jax.nn.sigmoid(-fr), lbs, frs)
    bs = _each(lambda lf: mm(incl_f, lf, 1, 0, True), log_fs)
    b_lasts = _each(lambda lf: jnp.sum(lf, axis=0, keepdims=True), log_fs)
    scores = _each(lambda q, k, b: mm(q * jnp.exp(b - b[mid:mid + 1]), k * jnp.exp(b[mid:mid + 1] - b), 1, 1), qs, ks, bs)
    intra = _each(lambda sc, ir: mm(jnp.where(incl, sc, 0.0), ir), scores, irs)
    inter = _each(lambda q, b, st: mm(q * jnp.exp(b), st, 1, 1), qs, bs, sts)
    upd = _each(lambda ir, k, bl, b: mm(ir, k * jnp.exp(bl - b), 0, 0), irs, ks, b_lasts, bs)
    new = _each(lambda st, bl, u: st * jnp.exp(bl) + u, sts, b_lasts, upd)
    return new, _each(jnp.add, intra, inter)


def f_hgrn2(rev, hb):
    def f(states, qr, fr, ir, lb):
        cols = [slice(j * HEAD, (j + 1) * HEAD) for j in range(hb)]
        new, outs = _hgrn2_heads(list(states), [qr[:, cs] for cs in cols], [fr[:, cs] for cs in cols],
                                 [ir[:, cs] for cs in cols], [lb[j] for j in range(hb)], rev)
        return tuple(new), (jnp.concatenate(outs, axis=1) if hb > 1 else outs[0],)
    return f


def _neumann_inv(a_lows):
    n = a_lows[0].shape[0]
    eye = (_iota2(n, n, 0) == _iota2(n, n, 1)).astype(F32)
    ps = _each(lambda a: -a, a_lows)
    xs = _each(lambda p: eye + p, ps)
    k = 2
    while k < n:
        ps = _each(lambda p: mm(p, p, 1, 0, True), ps)
        xs = _each(lambda x, p: x + mm(x, p, 1, 0, True), xs, ps)
        k *= 2
    return tuple(xs)


@jax.custom_vjp
def unit_tri_inv(a_lows):
    return _neumann_inv(a_lows)


def _uti_fwd(a_lows):
    xs = _neumann_inv(a_lows)
    return xs, xs


def _uti_bwd(xs, gs):
    ts = _each(lambda x, g: mm(x, g, 0, 0, True), xs, gs)
    return (tuple(_each(lambda t, x: -mm(t, x, 1, 1, True), ts, xs)),)


unit_tri_inv.defvjp(_uti_fwd, _uti_bwd)


@jax.custom_vjp
def unit_tri_inv_saved(a_lows, xs):
    return xs


def _utis_fwd(a_lows, xs):
    return xs, xs


def _utis_bwd(xs, gs):
    return _uti_bwd(xs, gs) + (tuple(jnp.zeros_like(x) for x in xs),)


unit_tri_inv_saved.defvjp(_utis_fwd, _utis_bwd)


def _l2n(x):
    return x * lax.rsqrt(jnp.sum(x * x, axis=-1, keepdims=True) + EPS)


def _gdn_heads(ss, qs, ks, vs, a_rows, b_rows, alogs, dtbs, rev, xs_saved=None):
    c = qs[0].shape[0]
    ri, ci = _iota2(c, c, 0), _iota2(c, c, 1)
    causal = (ri <= ci) if rev else (ri >= ci)
    causal_t = (ri >= ci) if rev else (ri <= ci)
    strict = (ri < ci) if rev else (ri > ci)
    eye = ri == ci
    sq = lambda row: jnp.broadcast_to(row, (c, c))
    to_col = lambda row: jnp.sum(jnp.where(eye, sq(row), 0.0), axis=1, keepdims=True)
    g_rows = _each(lambda al, a, dt: -jnp.exp(al) * jax.nn.softplus(a + dt), alogs, a_rows, dtbs)
    beta_cols = _each(lambda b: to_col(jax.nn.sigmoid(b)), b_rows)
    g_cols = _each(to_col, g_rows)
    gc_cols = _each(lambda g: jnp.sum(jnp.where(causal, sq(g), 0.0), axis=1, keepdims=True), g_rows)
    gc_rows = _each(lambda g: jnp.sum(jnp.where(causal_t, sq(g), 0.0), axis=0, keepdims=True), g_cols)
    gc_lasts = _each(lambda g: jnp.sum(g, axis=1, keepdims=True), g_rows)
    decays = _each(lambda gc, gr: jnp.where(causal, jnp.exp(jnp.where(causal, gc - gr, 0.0)), 0.0), gc_cols, gc_rows)
    k_betas = _each(jnp.multiply, ks, beta_cols)
    v_betas = _each(jnp.multiply, vs, beta_cols)
    kq_ks = _each(lambda kb, q, k: mm(jnp.concatenate([kb, q], axis=0), k, 1, 1), k_betas, qs, ks)
    a_lows = _each(lambda kk, dec: jnp.where(strict, kk[:c] * dec, 0.0), kq_ks, decays)
    qks = _each(lambda kk, dec: jnp.where(causal, kk[c:] * dec, 0.0), kq_ks, decays)
    xs = unit_tri_inv(tuple(a_lows)) if xs_saved is None else unit_tri_inv_saved(tuple(a_lows), tuple(xs_saved))
    egcs = _each(jnp.exp, gc_cols)
    uws = _each(lambda x, vb, kb, e: mm(x, jnp.concatenate([vb, kb * e], axis=1), 1, 0, True), xs, v_betas, k_betas, egcs)
    dv = vs[0].shape[1]
    wq_ss = _each(lambda uw, q, e, s: mm(jnp.concatenate([uw[:, dv:], q * e], axis=0), s), uws, qs, egcs, ss)
    v_news = _each(lambda uw, wq: uw[:, :dv] - wq[:c], uws, wq_ss)
    o_states = _each(lambda wq: wq[c:], wq_ss)
    o_locals = _each(mm, qks, v_news)
    upds = _each(lambda k, gl, gc, vn: mm(k * jnp.exp(gl - gc), vn, 0, 0), ks, gc_lasts, gc_cols, v_news)
    new = _each(lambda s, gl, u: s * jnp.exp(gl) + u, ss, gc_lasts, upds)
    return new, _each(jnp.add, o_states, o_locals), xs


def f_gdn(rev, khb, saved_inverse):
    def f(states, qr, kr, vr, a3, b3, alog3, dtb3, xcat=None):
        heads = [(j, r) for j in range(khb) for r in range(2)]
        cols = [slice(j * HEAD, (j + 1) * HEAD) for j in range(khb)]
        c = qr.shape[0]
        qk_ = _each(lambda cs: (_l2n(jax.nn.silu(qr[:, cs])) * (HEAD ** -0.5), _l2n(jax.nn.silu(kr[:, cs]))), cols)
        vs = [jax.nn.silu(vr[:, (2 * j + r) * HEAD:(2 * j + r + 1) * HEAD]) for j, r in heads]
        row = lambda arr3: [arr3[j][r:r + 1] for j, r in heads]
        xs_saved = [xcat[n * c:(n + 1) * c] for n in range(len(heads))] if saved_inverse else None
        new, outs, xs = _gdn_heads(list(states), [qk_[j][0] for j, _ in heads], [qk_[j][1] for j, _ in heads], vs,
                                   row(a3), row(b3), row(alog3), row(dtb3), rev, xs_saved)
        o = jnp.concatenate(outs, axis=1)
        return tuple(new), ((o,) if saved_inverse else (o, jnp.concatenate(xs, axis=0)))
    return f


def _hbm_spec():
    return pl.BlockSpec(memory_space=pltpu.HBM)


def all_gather(name, xs):
    nt = len(xs)

    def body(*refs):
        x_refs, out_refs = refs[:nt], refs[nt:2 * nt]
        send_sems, recv_sems, local_sems = refs[2 * nt:]
        x, y, c = lax.axis_index("x"), lax.axis_index("y"), lax.axis_index("c")
        me, sibling = (x, y, c), (x, y, 1 - c)
        chips = [(1 - x, y), (x, 1 - y), (1 - x, 1 - y)]

        def slab(t, px, py, pc):
            return out_refs[t].at[4 * px + 2 * py + pc]

        def copy(t, k, block, to, src=None):
            return pltpu.make_async_remote_copy(
                src_ref=slab(t, *block) if src is None else src, dst_ref=slab(t, *block),
                send_sem=send_sems.at[7 * t + k], recv_sem=recv_sems.at[7 * t + k], device_id=to, device_id_type=MESH)

        mine, first, passed = [], [], []
        for t in range(nt):
            mine.append(pltpu.make_async_copy(x_refs[t], slab(t, *me), local_sems.at[t]))
            mine[-1].start()
            cps = [copy(t, 0, me, sibling, src=x_refs[t])]
            cps += [copy(t, 1 + j, me, (*chip, c), src=x_refs[t]) for j, chip in enumerate(chips)]
            for cp in cps:
                cp.start()
            first += cps
        for j, chip in enumerate(chips):
            for t in range(nt):
                copy(t, 1 + j, (*chip, c), me).wait_recv()
                fw = copy(t, 4 + j, (*chip, c), sibling)
                fw.start()
                passed.append(fw)
        for t in range(nt):
            copy(t, 0, sibling, me).wait_recv()
            for j, chip in enumerate(chips):
                copy(t, 4 + j, (*chip, 1 - c), me).wait_recv()
        for cp in first + passed:
            cp.wait_send()
        for cp in mine:
            cp.wait()

    return pl.pallas_call(
        body, name=name, out_shape=[_sds((N_DEV,) + a.shape, a.dtype) for a in xs],
        in_specs=[_hbm_spec()] * nt, out_specs=[_hbm_spec()] * nt,
        scratch_shapes=[pltpu.SemaphoreType.DMA((7 * nt,)), pltpu.SemaphoreType.DMA((7 * nt,)),
                        pltpu.SemaphoreType.DMA((nt,))],
    )(*xs)


def _peers(x, y, c):
    out = []
    for k in range(1, N_DEV):
        px = 1 - x if k & 4 else x
        py = 1 - y if k & 2 else y
        pc = 1 - c if k & 1 else c
        out.append((px, py, pc))
    return out


def _exchange_copies(src_refs, land_refs, send_sems, recv_sems, scatter):
    x, y, c = lax.axis_index("x"), lax.axis_index("y"), lax.axis_index("c")
    me = 4 * x + 2 * y + c
    sends, recvs = [], []
    for t, (src, land) in enumerate(zip(src_refs, land_refs)):
        for k, (px, py, pc) in enumerate(_peers(x, y, c)):
            peer = 4 * px + 2 * py + pc
            sem = dict(send_sem=send_sems.at[7 * t + k], recv_sem=recv_sems.at[7 * t + k],
                       device_id=(px, py, pc), device_id_type=MESH)
            src_k = src.at[peer] if scatter else src
            sends.append(pltpu.make_async_remote_copy(src_ref=src_k, dst_ref=land.at[me], **sem))
            recvs.append(pltpu.make_async_remote_copy(src_ref=src_k, dst_ref=land.at[peer], **sem))
    return sends, recvs


def exchange_start(name, srcs, scatter, after, carry=None):
    nt = len(srcs)
    lands = [lax.empty(s.shape if scatter else (N_DEV,) + s.shape, s.dtype) for s in srcs]
    thru = list(srcs) + lands + ([carry] if carry is not None else [])
    n_thru = len(thru)

    def body(*refs):
        src_refs, land_refs = refs[:nt], refs[nt:2 * nt]
        send_sems, recv_sems = refs[n_thru + 1], refs[n_thru + 2]
        token = refs[-1]
        sends, _ = _exchange_copies(src_refs, land_refs, send_sems, recv_sems, scatter)
        for cp in sends:
            cp.start()
        token[...] = jnp.zeros_like(token)

    res = pl.pallas_call(
        body, name=name,
        out_shape=(pltpu.SemaphoreType.DMA((7 * nt,)), pltpu.SemaphoreType.DMA((7 * nt,)),
                   *[pltpu.HBM(a.shape, a.dtype) for a in thru], _sds((SUBLANES, LANES), F32)),
        in_specs=[_hbm_spec()] * n_thru + [pl.BlockSpec(memory_space=pl.ANY)],
        out_specs=(pl.BlockSpec(memory_space=pltpu.SEMAPHORE), pl.BlockSpec(memory_space=pltpu.SEMAPHORE),
                   *[_hbm_spec()] * n_thru, pl.BlockSpec(memory_space=pltpu.VMEM)),
        input_output_aliases={i: 2 + i for i in range(n_thru)},
        compiler_params=pltpu.CompilerParams(has_side_effects=pltpu.SideEffectType.DATAFLOW_SIDE_EFFECTING),
    )(*[pltpu.with_memory_space_constraint(a, pltpu.HBM) for a in thru], after)
    return (res[0], res[1], list(res[2:2 + nt]), list(res[2 + nt:2 + 2 * nt]), res[-1],
            res[2 + 2 * nt] if carry is not None else None)


def exchange_wait(name, started, after, scatter):
    send_sems, recv_sems, srcs, lands = started[:4]
    nt = len(srcs)

    def body(*refs):
        src_refs, land_refs = refs[:nt], refs[nt:2 * nt]
        sends, recvs = _exchange_copies(src_refs, land_refs, refs[2 * nt], refs[2 * nt + 1], scatter)
        for cp in sends:
            cp.wait_send()
        for cp in recvs:
            cp.wait_recv()

    hbm = lambda a: pltpu.HBM(a.shape, a.dtype)
    sem = pl.BlockSpec(memory_space=pltpu.SEMAPHORE)
    res = pl.pallas_call(
        body, name=name,
        out_shape=(*[hbm(a) for a in srcs], *[hbm(a) for a in lands]),
        in_specs=[_hbm_spec()] * (2 * nt) + [sem, sem, pl.BlockSpec(memory_space=pl.ANY)],
        out_specs=tuple([_hbm_spec()] * (2 * nt)),
        input_output_aliases={i: i for i in range(2 * nt)},
        compiler_params=pltpu.CompilerParams(has_side_effects=pltpu.SideEffectType.DATAFLOW_SIDE_EFFECTING),
    )(*srcs, *lands, send_sems, recv_sems, after)
    return list(res[:nt]), list(res[nt:])


def sibling_exchange(name, gs):
    nt = len(gs)

    def body(*refs):
        g_refs, l_refs, send_sems, recv_sems = refs[:nt], refs[nt:2 * nt], refs[2 * nt], refs[2 * nt + 1]
        x, y, c = lax.axis_index("x"), lax.axis_index("y"), lax.axis_index("c")
        cps = []
        for t in range(nt):
            for q in range(4):
                cps.append(pltpu.make_async_remote_copy(
                    src_ref=g_refs[t].at[2 * q + (1 - c)], dst_ref=l_refs[t].at[q], send_sem=send_sems.at[4 * t + q],
                    recv_sem=recv_sems.at[4 * t + q], device_id=(x, y, 1 - c), device_id_type=MESH))
        for cp in cps:
            cp.start()
        for cp in cps:
            cp.wait()

    return pl.pallas_call(
        body, name=name, out_shape=[_sds((4,) + g.shape[1:], g.dtype) for g in gs],
        in_specs=[_hbm_spec()] * nt, out_specs=[_hbm_spec()] * nt,
        scratch_shapes=[pltpu.SemaphoreType.DMA((4 * nt,)), pltpu.SemaphoreType.DMA((4 * nt,))],
    )(*gs)


def chip_exchange(name, hs):
    nt = len(hs)

    def body(*refs):
        h_refs, r_refs = refs[:nt], refs[nt:2 * nt]
        send_sems, recv_sems, local_sems = refs[2 * nt:]
        x, y, c = lax.axis_index("x"), lax.axis_index("y"), lax.axis_index("c")
        my = 2 * x + y
        chips = [(1 - x, y), (x, 1 - y), (1 - x, 1 - y)]

        def copy(t, k, src_slot, dst_slot, to):
            return pltpu.make_async_remote_copy(
                src_ref=h_refs[t].at[src_slot], dst_ref=r_refs[t].at[dst_slot], send_sem=send_sems.at[3 * t + k],
                recv_sem=recv_sems.at[3 * t + k], device_id=(*to, c), device_id_type=MESH)

        mine, sends = [], []
        for t in range(nt):
            mine.append(pltpu.make_async_copy(h_refs[t].at[my], r_refs[t].at[my], local_sems.at[t]))
            mine[-1].start()
            for k, (qx, qy) in enumerate(chips):
                sends.append(copy(t, k, 2 * qx + qy, my, (qx, qy)))
                sends[-1].start()
        for t in range(nt):
            for k, (qx, qy) in enumerate(chips):
                copy(t, k, my, 2 * qx + qy, (qx, qy)).wait_recv()
        for cp in sends:
            cp.wait_send()
        for cp in mine:
            cp.wait()

    return pl.pallas_call(
        body, name=name, out_shape=[_sds(h.shape, h.dtype) for h in hs],
        in_specs=[_hbm_spec()] * nt, out_specs=[_hbm_spec()] * nt,
        scratch_shapes=[pltpu.SemaphoreType.DMA((3 * nt,)), pltpu.SemaphoreType.DMA((3 * nt,)),
                        pltpu.SemaphoreType.DMA((nt,))],
    )(*hs)


def _pack(arrs, dtype, width, row_mult, lead=0):
    ld = arrs[0].shape[:lead]
    flat = jnp.concatenate([a.reshape(ld + (-1,)).astype(dtype) for a in arrs], axis=-1)
    n = flat.shape[-1]
    q = width * row_mult
    npad = -(-n // q) * q
    flat = jnp.pad(flat, [(0, 0)] * lead + [(0, npad - n)])
    return flat.reshape(ld + (npad // width, width))


def _unpack(buf, shapes, lead=0):
    ld = buf.shape[:lead]
    flat = buf.reshape(ld + (-1,))
    out, off = [], 0
    for s in shapes:
        n = int(np.prod(s))
        out.append(flat[..., off:off + n].reshape(ld + tuple(s)))
        off += n
    return out


def _pool_mats(seg_len):
    mats = np.zeros((len(POOL_WINDOWS), ROW_TILE, ROW_TILE), np.float32)
    for gi, win in enumerate(POOL_WINDOWS):
        for p in range(ROW_TILE):
            base = (p // seg_len) * seg_len
            q = p - base
            lo = min(max(q - win // 2, 0), seg_len - 1)
            hi = min(max(q + win - 1 - win // 2, 0), seg_len - 1)
            mats[gi, p, base + lo:base + hi + 1] = 1.0 / (hi - lo + 1)
    return mats


def kernel(x, c, ctx, c_ctx, w_ada, b_ada, norm_w, ev_w_in, ev_lb, ev_a_norm, ev_pool_w, ev_pool_scale, ev_w_out, od_w_in, od_conv, od_A_log, od_dt_bias, od_norm, od_w_out, ffn_w13, ffn_w2, loss_target, m_c_ctx, m_w_ada, m_b_ada, m_norm_w, m_ev_w_in, m_ev_lb, m_ev_a_norm, m_ev_pool_w, m_ev_pool_scale, m_ev_w_out, m_od_w_in, m_od_conv, m_od_A_log, m_od_dt_bias, m_od_norm, m_od_w_out, m_ffn_w13, m_ffn_w2, v_c_ctx, v_w_ada, v_b_ada, v_norm_w, v_ev_w_in, v_ev_lb, v_ev_a_norm, v_ev_pool_w, v_ev_pool_scale, v_ev_w_out, v_od_w_in, v_od_conv, v_od_A_log, v_od_dt_bias, v_od_norm, v_od_w_out, v_ffn_w13, v_ffn_w2):
    names = ["c_ctx", "w_ada", "b_ada", "norm_w", "ev_w_in", "ev_lb", "ev_a_norm", "ev_pool_w", "ev_pool_scale",
             "ev_w_out", "od_w_in", "od_conv", "od_A_log", "od_dt_bias", "od_norm", "od_w_out", "ffn_w13", "ffn_w2"]
    wts = dict(zip(names, [c_ctx, w_ada, b_ada, norm_w, ev_w_in, ev_lb, ev_a_norm, ev_pool_w, ev_pool_scale,
                           ev_w_out, od_w_in, od_conv, od_A_log, od_dt_bias, od_norm, od_w_out, ffn_w13, ffn_w2]))
    mom1 = dict(zip(names, [m_c_ctx, m_w_ada, m_b_ada, m_norm_w, m_ev_w_in, m_ev_lb, m_ev_a_norm, m_ev_pool_w,
                            m_ev_pool_scale, m_ev_w_out, m_od_w_in, m_od_conv, m_od_A_log, m_od_dt_bias, m_od_norm,
                            m_od_w_out, m_ffn_w13, m_ffn_w2]))
    mom2 = dict(zip(names, [v_c_ctx, v_w_ada, v_b_ada, v_norm_w, v_ev_w_in, v_ev_lb, v_ev_a_norm, v_ev_pool_w,
                            v_ev_pool_scale, v_ev_w_out, v_od_w_in, v_od_conv, v_od_A_log, v_od_dt_bias, v_od_norm,
                            v_od_w_out, v_ffn_w13, v_ffn_w2]))

    ax, ay, ac = lax.axis_index("x"), lax.axis_index("y"), lax.axis_index("c")
    me = 4 * ax + 2 * ay + ac
    my_chip = 2 * ax + ay

    seq, d = x.shape[1], x.shape[2]
    n_ctx = ctx.shape[1]
    t = n_ctx + seq
    nt = t // ROW_TILE
    nct = n_ctx // ROW_TILE
    assert n_ctx == ROW_TILE and seq % ROW_TILE == 0 and ROW_TILE % GRID_W == 0
    depth = w_ada.shape[0]
    aw = d // 2
    n_ah = aw // HEAD
    n_grp = len(POOL_WINDOWS)
    dg = aw // n_grp
    assert dg % LANES == 0
    n_kh = d // HEAD
    kw, vw = n_kh * HEAD, 2 * n_kh * HEAD
    n_gate = 8 * n_kh
    ffn_h = ffn_w2.shape[1] * N_DEV
    ada_loc = w_ada.shape[2]
    assert depth == 2

    bf = lambda w_: w_.astype(BF16)

    def gathered(started, after, name):
        srcs, lands = exchange_wait(name, started, after, False)
        return [lax.dynamic_update_index_in_dim(l_, s_, me, 0) for l_, s_ in zip(lands, srcs)]

    small_shapes = [(d,), norm_w.shape, ev_lb.shape, ev_pool_w.shape[1:], od_conv.shape[1:]]
    (g1,) = all_gather("ag_small", [_pack([c[0], norm_w, ev_lb, ev_pool_w[0], od_conv[0]], F32, LANES, SUBLANES)])
    c_all, nw_g, lb_g, pw_g, cv_g = _unpack(g1, small_shapes, lead=1)
    nw_full = nw_g.transpose(1, 2, 0, 3).reshape(depth, 4, d)
    lb_full = lb_g.transpose(1, 2, 0, 3).reshape(2, depth + 1, aw)
    pw_full = pw_g.transpose(1, 0, 2, 3).reshape(n_grp, dg, dg)
    cv_full = cv_g.transpose(1, 0, 2).reshape(C_CONV, 2 * kw + vw)

    g_ev_in, g_ev_out = all_gather("ag_weights_ev", [bf(ev_w_in[0]), bf(ev_w_out[0])])

    def cols_natural(g):
        return g.transpose(1, 0, 2).reshape(g.shape[1], N_DEV * g.shape[2])

    def rows_natural(g):
        return g.reshape(N_DEV * g.shape[1], g.shape[2])

    def col_weight(g):
        return (g, True) if g.shape[2] % LANES == 0 else (cols_natural(g), False)

    w_ev_in = col_weight(g_ev_in)
    w_ev_out = rows_natural(g_ev_out)
    w13, w2 = [None] * depth, [None] * depth

    b_loc = lax.dynamic_slice_in_dim(b_ada, me * ada_loc, ada_loc, axis=1).reshape(depth, 1, ada_loc)
    ada_cb = _pick(ada_loc, TN_PREFS)
    ada_grid = (depth, ada_loc // ada_cb)
    ada_args = [
        TArg(c_all, (N_DEV, d), lambda l, j: (0, 0), "const"),
        TArg(c_ctx.reshape(1, d), (1, d), lambda l, j: (0, 0), "par", (0, 1)),
        TArg(w_ada, (None, d, ada_cb), lambda l, j: (l, 0, j)),
        TArg(b_loc, (None, 1, ada_cb), lambda l, j: (l, 0, j)),
    ]
    (m_loc,) = tile_fwd("ada_fwd", f_ada, ada_grid, ada_args,
                        [((depth, 2 * N_DEV, ada_loc), F32, (None, 2 * N_DEV, ada_cb), lambda l, j: (l, 0, j))])
    (m_all,) = all_gather("ag_mod", [m_loc])
    mods = []
    for layer in range(depth):
        lat = lax.dynamic_index_in_dim(m_all[:, layer], me, axis=1, keepdims=False).reshape(6, d)
        cxt = lax.dynamic_index_in_dim(m_all[:, layer], N_DEV + me, axis=1, keepdims=False).reshape(6, d)
        mods.append(jnp.stack([cxt, lat]))

    gathers_done = mods[0][0, :1, :SUBLANES] + g_ev_out[0, :1, :SUBLANES].astype(F32)
    ag_ffn0 = exchange_start("ag_start_ffn0", [bf(ffn_w13[0]), bf(ffn_w2[0])], False, gathers_done)
    ag_l1 = exchange_start("ag_start_l1", [bf(od_w_in[0]), bf(od_w_out[0]), bf(ffn_w13[1]), bf(ffn_w2[1])], False,
                           gathers_done)
    mods[0] = mods[0] + (ag_ffn0[4][0, 0] + ag_l1[4][0, 0])

    lb_slots = [TArg(lb_full[:, j], (2, aw), lambda i: (0, 0)) for j in range(depth + 1)]
    (lb0,) = tile_fwd("lb_fwd", f_lb(0), (1,), lb_slots, [((2, aw), F32, (2, aw), lambda i: (0, 0))])
    lb0r = lb0.reshape(2, n_ah, 1, HEAD)

    full_row = lambda i: (i, 0)
    par0 = lambda i: (0, 0)

    def nm_args(xs, layer, slot):
        return [TArg(xs, (ROW_TILE, d), full_row),
                TArg(nw_full[layer, slot].reshape(1, d), (1, d), par0, "par", (0,)),
                TArg(mods[layer], (2, 6, d), lambda i: (0, 0, 0), "par", (0,))]

    def norm_mod(name, xs, layer, slot, si, ci):
        (h,) = tile_fwd(name, f_norm_mod(si, ci, nct), (nt,), nm_args(xs, layer, slot),
                        [((t, d), BF16, (ROW_TILE, d), full_row)])
        return h

    def norm_mod_bwd(name, xs, layer, slot, si, ci, dh, carry):
        return tile_bwd(name, f_norm_mod(si, ci, nct, True), (nt,), nm_args(xs, layer, slot),
                        [(dh, (ROW_TILE, d), full_row), (carry, (ROW_TILE, d), full_row)])

    def gr_args(xs, ys, layer, slot):
        return [TArg(xs, (ROW_TILE, d), full_row, grad=False), TArg(ys, (ROW_TILE, d), full_row, gdtype=BF16),
                TArg(nw_full[layer, slot].reshape(1, d), (1, d), par0, "par", (0,)),
                TArg(mods[layer], (2, 6, d), lambda i: (0, 0, 0), "par", (0,))]

    def gate_res(name, xs, ys, layer, slot, gi):
        (o,) = tile_fwd(name, f_gate_res(gi, nct), (nt,), gr_args(xs, ys, layer, slot),
                        [((t, d), F32, (ROW_TILE, d), full_row)])
        return o

    def gate_res_bwd(name, xs, ys, layer, slot, gi, dx):
        return tile_bwd(name, f_gate_res(gi, nct), (nt,), gr_args(xs, ys, layer, slot),
                        [(dx, (ROW_TILE, d), full_row)])

    sw_rows = ROW_TILE // 2

    def sw_args(gu):
        return [TArg(gu, (sw_rows, 2 * ffn_h), full_row, gdtype=BF16)]

    def ffn_fwd(tag, xs, layer):
        h2 = norm_mod(f"nm2_{tag}", xs, layer, 2, 3, 4)
        gu = matmul(f"w13_{tag}", h2, w13[layer][0], "nn", slabs=w13[layer][1], out_dtype=BF16)
        (act,) = tile_fwd(f"swiglu_{tag}", f_swiglu, (t // sw_rows,), sw_args(gu),
                          [((t, ffn_h), BF16, (sw_rows, ffn_h), full_row)])
        fo = matmul(f"w2_{tag}", act, w2[layer], "nn")
        xn = gate_res(f"gr2_{tag}", xs, fo, layer, 3, 5)
        return xn, (xs, h2, gu, act, fo)

    def col_grad(name, a, dy, slabs):
        g = matmul(name, a, dy, "tn", slabs=slabs, out_dtype=BF16)
        return g if slabs else g.reshape(g.shape[0], N_DEV, g.shape[1] // N_DEV).transpose(1, 0, 2)

    def row_grad(name, a, dy):
        g = matmul(name, a, dy, "tn", out_dtype=BF16)
        return g.reshape(N_DEV, g.shape[0] // N_DEV, g.shape[1])

    def ffn_bwd(tag, saved, layer, dxn, acc):
        xs, h2, gu, act, fo = saved
        dfo, dnw3, dmod_a = gate_res_bwd(f"gr2b_{tag}", xs, fo, layer, 3, 5, dxn)
        dact = matmul(f"w2d_{tag}", dfo, w2[layer], "nt")
        dw2 = row_grad(f"w2w_{tag}", act, dfo)
        (dgu,) = tile_bwd(f"swiglub_{tag}", f_swiglu, (t // sw_rows,), sw_args(gu), [(dact, (sw_rows, ffn_h), full_row)])
        dh2 = matmul(f"w13d_{tag}", dgu, w13[layer][0], "nt", slabs=w13[layer][1])
        dw13 = col_grad(f"w13w_{tag}", h2, dgu, w13[layer][1])
        dxs, dnw2, dmod_b = norm_mod_bwd(f"nm2b_{tag}", xs, layer, 2, 3, 4, dh2, dxn)
        acc["ffn_w13"][layer] = dw13
        acc["ffn_w2"][layer] = dw2
        acc["norm_w"][layer][2] = dnw2
        acc["norm_w"][layer][3] = dnw3
        acc["mods"][layer].extend([dmod_a, dmod_b])
        return dxs

    def head_cols(width):
        return (ROW_TILE, width)

    n_a = t // A_CHUNK
    nca = n_ctx // A_CHUNK

    def a_tok(rev):
        if not rev:
            return lambda i: i
        return lambda i: jnp.where(i < nca, nca - 1 - i, n_a + nca - 1 - i)

    hb = _pick(n_ah, (HEADS_PER_STEP, 2, 1))
    n_hblk = n_ah // hb

    def hg_args(p, direction):
        tok = a_tok(direction == 1)
        blk = (A_CHUNK, hb * HEAD)
        return [TArg(p, blk, lambda h, i: (tok(i), h)),
                TArg(p, blk, lambda h, i: (tok(i), (1 + direction) * n_hblk + h)),
                TArg(p, blk, lambda h, i: (tok(i), 3 * n_hblk + h)),
                TArg(lb0r, (None, hb, 1, HEAD), lambda h, i: (direction, h, 0, 0), "par", (1,))]

    pmats = jnp.asarray(np.stack([_pool_mats(n_ctx), _pool_mats(GRID_W)]))

    def pool_args(p):
        return [TArg(p, (ROW_TILE, dg), lambda g, i: (i, 5 * n_grp + g)),
                TArg(pmats, (None, None, ROW_TILE, ROW_TILE), lambda g, i: (jnp.where(i < nct, 0, 1), g, 0, 0), "const"),
                TArg(pw_full, (None, dg, dg), lambda g, i: (g, 0, 0), "par", (1,)),
                TArg(ev_pool_scale, (1, dg), lambda g, i: (0, g), "par", (1,))]

    def ro_plan(gate_off, n_heads):
        per = _pick(n_heads, (8, 4, 2, 1))
        assert gate_off % per == 0
        return per, n_heads // per, gate_off // per

    def ro_args(o_f, o_b, gate_arr, gate_off, nw_arr, n_heads):
        per, _, goff = ro_plan(gate_off, n_heads)
        blk = (ROW_TILE, per * HEAD)
        return [TArg(o_f, blk, lambda h, i: (i, h)), TArg(o_b, blk, lambda h, i: (i, h), grad=False),
                TArg(gate_arr, blk, lambda h, i: (i, goff + h)),
                TArg(nw_arr, (1, HEAD), lambda h, i: (0, 0), "par", (0, 1))]

    def readout(name, o_f, o_b, gate_arr, gate_off, nw_arr, n_heads):
        per, nblk, _ = ro_plan(gate_off, n_heads)
        (o,) = tile_fwd(name, f_readout(per), (nblk, nt), ro_args(o_f, o_b, gate_arr, gate_off, nw_arr, n_heads),
                        [((t, n_heads * HEAD), BF16, (ROW_TILE, per * HEAD), lambda hh, i: (i, hh))])
        return o

    def readout_bwd(name, o_f, o_b, gate_arr, gate_off, nw_arr, n_heads, dout):
        per, nblk, _ = ro_plan(gate_off, n_heads)
        return tile_bwd(name, f_readout(per), (nblk, nt), ro_args(o_f, o_b, gate_arr, gate_off, nw_arr, n_heads),
                        [(dout, (ROW_TILE, per * HEAD), lambda hh, i: (i, hh))])

    def even_fwd(tag, xs, layer):
        h = norm_mod(f"nm1_{tag}", xs, layer, 0, 0, 1)
        p = matmul(f"win_{tag}", h, w_ev_in[0], "nn", slabs=w_ev_in[1])
        outs, saves = [], []
        for direction in (0, 1):
            (o,), sv = scan_fwd(f"hgrn_{tag}_{direction}", f_hgrn2(direction == 1, hb), n_hblk, n_a, hg_args(p, direction),
                                [((t, aw), F32, (A_CHUNK, hb * HEAD), lambda hh, i, tok=a_tok(direction == 1): (tok(i), hh))], hb)
            outs.append(o)
            saves.append(sv)
        a_out = readout(f"ro_{tag}", outs[0], outs[1], p, 4 * n_ah, ev_a_norm, n_ah)
        (pooled,) = tile_fwd(f"pool_{tag}", f_pool, (n_grp, nt), pool_args(p),
                             [((t, aw), BF16, (ROW_TILE, dg), lambda g, i: (i, g))])
        cat = assemble(f"cat_{tag}", [[(a_out, aw, 0)], [(pooled, aw, 0)]], BF16)
        y = matmul(f"wout_{tag}", cat, w_ev_out, "nn")
        xn = gate_res(f"gr1_{tag}", xs, y, layer, 1, 2)
        return xn, (xs, h, p, outs, saves, cat, y)

    def even_bwd(tag, saved, layer, dxn, acc):
        xs, h, p, outs, saves, cat, y = saved
        dy, dnw1, dmod_a = gate_res_bwd(f"gr1b_{tag}", xs, y, layer, 1, 2, dxn)
        dcat = matmul(f"woutd_{tag}", dy, w_ev_out, "nt")
        acc["ev_w_out"] = row_grad(f"woutw_{tag}", cat, dy)
        do, dgate, d_anorm = readout_bwd(f"rob_{tag}", outs[0], outs[1], p, 4 * n_ah, ev_a_norm, n_ah, dcat)
        du, d_pw, d_ps = tile_bwd(f"poolb_{tag}", f_pool, (n_grp, nt), pool_args(p),
                                  [(dcat, (ROW_TILE, dg), lambda g, i: (i, n_grp + g))])
        dq, df, di, dlb = [], [], [], []
        for direction in (0, 1):
            r = scan_bwd(f"hgrnb_{tag}_{direction}", f_hgrn2(direction == 1, hb), n_hblk, n_a, hg_args(p, direction),
                         saves[direction],
                         [(do, (A_CHUNK, hb * HEAD), lambda hh, i, tok=a_tok(direction == 1): (tok(i), hh))])
            dq.append(r[0])
            df.append(r[1])
            di.append(r[2])
            dlb.append(r[3])
        sec = lambda arr, s: (arr, aw, s)
        dp = assemble(f"dp_{tag}", [[sec(dq[0], 0), sec(dq[1], 0)], [sec(df[0], 1)], [sec(df[1], 2)],
                                    [sec(di[0], 3), sec(di[1], 3)], [sec(dgate, 4)], [sec(du, 5)]], BF16)
        acc["ev_w_in"] = col_grad(f"winw_{tag}", h, dp, w_ev_in[1])
        acc["rs_ev"] = exchange_start("rs_start_ev", [acc["ev_w_in"], acc["ev_w_out"]], True, dp, carry=w_ev_in[0])
        dh = matmul(f"wind_{tag}", dp, acc["rs_ev"][5], "nt", slabs=w_ev_in[1])
        dxs, dnw0, dmod_b = norm_mod_bwd(f"nm1b_{tag}", xs, layer, 0, 0, 1, dh, dxn)
        acc["norm_w"][layer][0] = dnw0
        acc["norm_w"][layer][1] = dnw1
        acc["mods"][layer].extend([dmod_a, dmod_b])
        acc["ev_a_norm"] = d_anorm
        acc["ev_pool_w"] = d_pw
        acc["ev_pool_scale"] = d_ps
        acc["lb0"] = jnp.stack([dlb[0][0], dlb[1][1]]).reshape(2, aw)
        return dxs

    n_c = t // C_CHUNK
    ncc = n_ctx // C_CHUNK

    def c_tok(rev):
        if not rev:
            return lambda i: i
        return lambda i: jnp.where(i < ncc, ncc - 1 - i, n_c + ncc - 1 - i)

    alog = od_A_log[0].reshape(2, n_kh, 2, 1)
    dtb = od_dt_bias[0].reshape(2, n_kh, 2, 1)

    khb = _pick(n_kh, (HEADS_PER_STEP, 2, 1))
    n_kblk = n_kh // khb

    def gd_args(z, gates, direction):
        tok = c_tok(direction == 1)
        gblk = (None, khb, None, 2, C_CHUNK)
        sblk = (None, khb, 2, 1)
        return [TArg(z, (C_CHUNK, khb * HEAD), lambda kb, i: (tok(i), kb)),
                TArg(z, (C_CHUNK, khb * HEAD), lambda kb, i: (tok(i), n_kblk + kb)),
                TArg(z, (C_CHUNK, khb * 2 * HEAD), lambda kb, i: (tok(i), n_kblk + kb)),
                TArg(gates, gblk, lambda kb, i: (direction, kb, tok(i), 0, 0)),
                TArg(gates, gblk, lambda kb, i: (2 + direction, kb, tok(i), 0, 0)),
                TArg(alog, sblk, lambda kb, i: (direction, kb, 0, 0), "par", (1,)),
                TArg(dtb, sblk, lambda kb, i: (direction, kb, 0, 0), "par", (1,))]

    def odd_fwd(tag, xs, layer):
        h = norm_mod(f"nm1_{tag}", xs, layer, 0, 0, 1)
        pm = matmul(f"win_{tag}", h, w_od_main, "nn")
        pg = matmul(f"wgate_{tag}", h, w_od_gate, "nn")
        z = conv_fwd(f"conv_{tag}", pm, cv_full, 2 * kw + vw, nct)
        gates = pg.reshape(n_c, C_CHUNK, 4, n_kh, 2).transpose(2, 3, 0, 4, 1)
        outs, saves = [], []
        for direction in (0, 1):
            xrows = 2 * khb * C_CHUNK
            (o, xinv), sv = scan_fwd(
                f"gdn_{tag}_{direction}", f_gdn(direction == 1, khb, False), n_kblk, n_c, gd_args(z, gates, direction),
                [((t, vw), F32, (C_CHUNK, khb * 2 * HEAD), lambda kb, i, tok=c_tok(direction == 1): (tok(i), kb)),
                 ((n_kblk, n_c, xrows, C_CHUNK), F32, (None, None, xrows, C_CHUNK), lambda kb, i: (kb, i, 0, 0))],
                2 * khb)
            outs.append(o)
            saves.append((sv, xinv))
        n_vh = 2 * n_kh
        yo = readout(f"ro_{tag}", outs[0], outs[1], pm, 2 * n_kh + n_vh, od_norm, n_vh)
        y = matmul(f"wout_{tag}", yo, w_od_out, "nn")
        xn = gate_res(f"gr1_{tag}", xs, y, layer, 1, 2)
        return xn, (xs, h, pm, z, gates, outs, saves, yo, y)

    def odd_bwd(tag, saved, layer, dxn, acc):
        xs, h, pm, z, gates, outs, saves, yo, y = saved
        n_vh = 2 * n_kh
        dy, dnw1, dmod_a = gate_res_bwd(f"gr1b_{tag}", xs, y, layer, 1, 2, dxn)
        dyo = matmul(f"woutd_{tag}", dy, w_od_out, "nt")
        acc["od_w_out"] = row_grad(f"woutw_{tag}", yo, dy)
        do, dzg, d_onorm = readout_bwd(f"rob_{tag}", outs[0], outs[1], pm, 2 * n_kh + n_vh, od_norm, n_vh, dyo)
        dq, dk, dv, dga, dgb, dal, ddt = [], [], [], [], [], [], []
        for direction in (0, 1):
            sv, xinv = saves[direction]
            xarg = TArg(xinv, (None, None, 2 * khb * C_CHUNK, C_CHUNK), lambda kb, i: (kb, i, 0, 0), "const")
            r = scan_bwd(f"gdnb_{tag}_{direction}", f_gdn(direction == 1, khb, True), n_kblk, n_c,
                         gd_args(z, gates, direction) + [xarg], sv,
                         [(do, (C_CHUNK, khb * 2 * HEAD), lambda kb, i, tok=c_tok(direction == 1): (tok(i), kb))])
            for lst, v_ in zip((dq, dk, dv, dga, dgb, dal, ddt), r):
                lst.append(v_)
        dz = assemble(f"dz_{tag}", [[(dq[0], kw, 0), (dq[1], kw, 0)], [(dk[0], kw, 1), (dk[1], kw, 1)],
                                    [(dv[0], vw, 1), (dv[1], vw, 1)]], F32)
        du, d_conv = conv_bwd(f"convb_{tag}", pm, cv_full, dz, 2 * kw + vw, nct)
        dpm = assemble(f"dpm_{tag}", [[(du, 2 * kw + vw, 0)], [(dzg, vw, 2)]], BF16)
        dgates = jnp.stack([dga[0][0], dga[1][1], dgb[0][2], dgb[1][3]])
        dpg = dgates.transpose(2, 4, 0, 1, 3).reshape(t, n_gate).astype(BF16)
        dh = matmul(f"wgated_{tag}", dpg, w_od_gate, "nt")
        dh = matmul(f"wind_{tag}", dpm, w_od_main, "nt", add=dh)
        dw_in = jnp.concatenate([matmul(f"winw_{tag}", h, dpm, "tn", out_dtype=BF16),
                                 matmul(f"wgatew_{tag}", h, dpg, "tn", out_dtype=BF16)], axis=1)
        acc["od_w_in"] = dw_in.reshape(d, N_DEV, dw_in.shape[1] // N_DEV).transpose(1, 0, 2)
        dxs, dnw0, dmod_b = norm_mod_bwd(f"nm1b_{tag}", xs, layer, 0, 0, 1, dh, dxn)
        acc["norm_w"][layer][0] = dnw0
        acc["norm_w"][layer][1] = dnw1
        acc["mods"][layer].extend([dmod_a, dmod_b])
        acc["od_norm"] = d_onorm
        acc["od_conv"] = d_conv
        acc["od_A_log"] = jnp.stack([dal[0][0], dal[1][1]]).reshape(1, 2, n_vh)
        acc["od_dt_bias"] = jnp.stack([ddt[0][0], ddt[1][1]]).reshape(1, 2, n_vh)
        return dxs

    xs0 = jnp.concatenate([ctx[0], x[0]], axis=0)
    xs1, sv_e = even_fwd("l0", xs0, 0)
    g_w13a, g_w2a = gathered(ag_ffn0, xs1, "ag_wait_ffn0")
    w13[0], w2[0] = col_weight(g_w13a), rows_natural(g_w2a)
    xs2, sv_f0 = ffn_fwd("l0", xs1, 0)
    g_od_in, g_od_out, g_w13b, g_w2b = gathered(ag_l1, xs2, "ag_wait_l1")
    w_od_in = cols_natural(g_od_in)
    w_od_main, w_od_gate = w_od_in[:, :2 * kw + 2 * vw], w_od_in[:, 2 * kw + 2 * vw:]
    w_od_out = rows_natural(g_od_out)
    w13[1], w2[1] = col_weight(g_w13b), rows_natural(g_w2b)
    xs3, sv_o = odd_fwd("l1", xs2, 1)
    xs4, sv_f1 = ffn_fwd("l1", xs3, 1)
    loss_loc, dxs = loss_kernel("loss", xs4, loss_target[0], nct)
    loss = lax.psum(loss_loc, ("x", "y", "c"))

    acc = {"norm_w": [[None] * 4 for _ in range(depth)], "mods": [[] for _ in range(depth)],
           "ffn_w13": [None] * depth, "ffn_w2": [None] * depth}
    dxs = ffn_bwd("l1", sv_f1, 1, dxs, acc)
    rs_ffn1 = exchange_start("rs_start_ffn1", [acc["ffn_w13"][1], acc["ffn_w2"][1]], True, dxs)
    mods[1] = mods[1] + rs_ffn1[4][0, 0]
    dxs = odd_bwd("l1", sv_o, 1, dxs, acc)
    rs_od = exchange_start("rs_start_od", [acc["od_w_in"], acc["od_w_out"]], True, dxs)
    mods[0] = mods[0] + rs_od[4][0, 0]
    dxs = ffn_bwd("l0", sv_f0, 0, dxs, acc)
    rs_ffn0 = exchange_start("rs_start_ffn0", [acc["ffn_w13"][0], acc["ffn_w2"][0]], True, dxs)
    mods[0] = mods[0] + rs_ffn0[4][0, 0]
    dxs = even_bwd("l0", sv_e, 0, dxs, acc)
    rs_ev = acc["rs_ev"]
    grad_x = dxs[n_ctx:].reshape(1, seq, d)

    (d_lb_slots) = tile_bwd("lb_bwd", f_lb(0), (1,), lb_slots, [(acc["lb0"], (2, aw), lambda i: (0, 0))])
    d_ev_lb = jnp.stack(d_lb_slots, axis=1)

    dmods = jnp.stack([functools.reduce(jnp.add, acc["mods"][layer]) for layer in range(depth)])
    (dm_all,) = all_gather("ag_dmod", [dmods.reshape(depth * 2 * 6, d)])
    dm_all = dm_all.reshape(N_DEV, depth, 2, 6 * d)
    dm_cols = lax.dynamic_slice_in_dim(dm_all, me * ada_loc, ada_loc, axis=3)
    dm_loc = jnp.concatenate([dm_cols[:, :, 1].transpose(1, 0, 2), dm_cols[:, :, 0].transpose(1, 0, 2)], axis=1)
    d_cctx_part, d_w_ada, d_b_loc = tile_bwd("ada_bwd", f_ada, ada_grid, ada_args,
                                             [(dm_loc, (None, 2 * N_DEV, ada_cb), lambda l, j: (l, 0, j))])

    d_b_full = lax.dynamic_update_slice_in_dim(jnp.zeros_like(b_ada), d_b_loc.reshape(depth, ada_loc), me * ada_loc, axis=1)
    d_nw = jnp.stack([jnp.stack([acc["norm_w"][layer][s].reshape(d) for s in range(4)]) for layer in range(depth)])
    small_grads = [d_cctx_part.reshape(d), d_b_full, d_nw, d_ev_lb, acc["ev_a_norm"], acc["ev_pool_w"],
                   acc["ev_pool_scale"], acc["od_conv"], acc["od_A_log"], acc["od_dt_bias"], acc["od_norm"]]
    sg_shapes = [a.shape for a in small_grads]
    (sg,) = all_gather("ag_smallgrads", [_pack(small_grads, F32, FLAT_W, SUBLANES)])
    sg_sum = sum_leading("sum_smallgrads", sg, F32)
    (g_cctx, g_bada, g_nw, g_lb, g_anorm, g_pw, g_ps, g_conv, g_alog, g_dtb, g_onorm) = _unpack(sg_sum, sg_shapes)

    def my_cols(full, axis):
        loc = full.shape[axis] // N_DEV
        return lax.dynamic_slice_in_dim(full, me * loc, loc, axis=axis)

    grads = {
        "c_ctx": g_cctx, "w_ada": d_w_ada, "b_ada": g_bada, "norm_w": my_cols(g_nw, 2), "ev_lb": my_cols(g_lb, 2),
        "ev_a_norm": g_anorm, "ev_pool_w": my_cols(g_pw, 1)[None], "ev_pool_scale": g_ps,
        "od_conv": my_cols(g_conv, 1)[None], "od_A_log": g_alog, "od_dt_bias": g_dtb, "od_norm": g_onorm,
    }

    def reduced(started, tags_, name):
        srcs, lands = exchange_wait(name, started, sg_sum, True)
        out = []
        for tg, s_, l_ in zip(tags_, srcs, lands):
            own = lax.dynamic_index_in_dim(s_, me, 0, keepdims=True)
            out.append(sum_leading(f"rs_sum_{tg}", lax.dynamic_update_slice_in_dim(l_, own, me, 0), F32))
        return out

    g_w13b, g_w2b = reduced(rs_ffn1, ["w13b", "w2b"], "rs_wait_ffn1")
    g_od_in, g_od_out = reduced(rs_od, ["od_in", "od_out"], "rs_wait_od")
    g_w13a, g_w2a = reduced(rs_ffn0, ["w13a", "w2a"], "rs_wait_ffn0")
    g_ev_in, g_ev_out = reduced(rs_ev, ["ev_in", "ev_out"], "rs_wait_ev")
    grads["ev_w_in"], grads["ev_w_out"], grads["od_w_in"], grads["od_w_out"] = (g_ev_in[None], g_ev_out[None],
                                                                                g_od_in[None], g_od_out[None])
    grads["ffn_w13"] = jnp.stack([g_w13a, g_w13b])
    grads["ffn_w2"] = jnp.stack([g_w2a, g_w2b])

    big_names = ["w_ada", "ev_w_in", "ev_w_out", "od_w_in", "od_w_out", "ffn_w13", "ffn_w2"]
    small_names = [n_ for n_ in names if n_ not in big_names]
    gl = {n_: grads[n_].reshape(wts[n_].shape) for n_ in names}
    delta, new_m, new_v = {}, {}, {}
    for n_ in big_names:
        shp = wts[n_].shape
        res = adamw(f"adamw_{n_}", _rows2d(gl[n_]), _rows2d(wts[n_]), _rows2d(mom1[n_]), _rows2d(mom2[n_]))
        delta[n_], new_m[n_], new_v[n_] = (r_.reshape(shp) for r_ in res)
    shapes = [wts[n_].shape for n_ in small_names]
    pk = lambda dct: _pack([dct[n_] for n_ in small_names], F32, FLAT_W, SUBLANES)
    res = adamw("adamw_small", pk(gl), pk(wts), pk(mom1), pk(mom2))
    for dct, r_ in zip((delta, new_m, new_v), res):
        for n_, a_ in zip(small_names, _unpack(r_, shapes)):
            dct[n_] = a_
    return (loss, grad_x, *[gl[n_] for n_ in names], *[delta[n_] for n_ in names], *[new_m[n_] for n_ in names],
            *[new_v[n_] for n_ in names])
```

```python
import functools
from typing import Any, NamedTuple

import numpy as np

import jax
import jax.numpy as jnp
from jax import lax
from jax.experimental import pallas as pl
from jax.experimental.pallas import tpu as pltpu

F32 = jnp.float32
BF16 = jnp.bfloat16
MESH = pl.DeviceIdType.MESH
N_DEV = 8

EPS = 1e-6
GRID_W = 64
HEAD = 128
A_CHUNK = 32
C_CHUNK = 64
C_CONV = 4
POOL_WINDOWS = (2, 4, 8, 16)
ADAM_LR, ADAM_B1, ADAM_B2, ADAM_EPS, ADAM_WD, ADAM_STEP = 0.001, 0.9, 0.999, 1e-08, 0.01, 10

VMEM_LIMIT_BYTES = 56 * 1024 * 1024
LANES = 128
SUBLANES = 8
ROW_TILE = 256
FLAT_W = 1024
FLAT_ROWS = 512
TM_PREFS = (1056, 768, 512, 256, 128, 64, 32, 16)
TN_PREFS = (768, 512, 1408, 256, 128)
TK_PREFS = (2048, 2816, 1408, 1024, 768, 512, 384, 256, 128)
TO_PREFS = (1024, 1408, 768, 704, 512, 384, 256, 128)
HEADS_PER_STEP = 8


def _pick(dim, prefs):
    for p in prefs:
        if p <= dim and dim % p == 0:
            return p
    return dim


def _cparams(ngrid):
    return pltpu.CompilerParams(dimension_semantics=("arbitrary",) * ngrid, vmem_limit_bytes=VMEM_LIMIT_BYTES)


def _sds(shape, dtype):
    return jax.ShapeDtypeStruct(tuple(shape), dtype)


def _split(x):
    hi = x.astype(BF16)
    return hi, (x - hi.astype(F32)).astype(BF16)


def _dot(a, b, ca, cb, hi):
    dims = (((ca,), (cb,)), ((), ()))
    dot = lambda u, v: lax.dot_general(u, v, dims, preferred_element_type=F32)
    if hi:
        (ah, al), (bh, bl) = _split(a.astype(F32)), _split(b.astype(F32))
        return dot(ah, bh) + (dot(ah, bl) + dot(al, bh))
    return dot(a.astype(BF16), b.astype(BF16))


@functools.partial(jax.custom_vjp, nondiff_argnums=(2, 3, 4))
def mm(a, b, ca=1, cb=0, hi=False):
    return _dot(a, b, ca, cb, hi)


def _mm_fwd(a, b, ca, cb, hi):
    return _dot(a, b, ca, cb, hi), (a, b)


def _mm_bwd(ca, cb, hi, res, g):
    a, b = res
    da = _dot(g, b, 1, 1 - cb, hi) if ca == 1 else _dot(b, g, 1 - cb, 1, hi)
    db = _dot(a, g, 1 - ca, 0, hi) if cb == 0 else _dot(g, a, 0, 1 - ca, hi)
    return da, db


mm.defvjp(_mm_fwd, _mm_bwd)


def _iota2(n, m, axis):
    return lax.broadcasted_iota(jnp.int32, (n, m), axis)


class TArg(NamedTuple):
    arr: Any
    block: tuple
    imap: Any
    kind: str = "row"
    acc: tuple = ()
    gdtype: Any = F32
    grad: bool = True


def _load(ref):
    v = ref[...]
    return v.astype(F32) if jnp.issubdtype(v.dtype, jnp.floating) else v


def tile_fwd(name, f, grid, args, outs):
    n_in, ng = len(args), len(grid)

    def body(*refs):
        pids = tuple(pl.program_id(k) for k in range(ng))
        res = f(pids, *[_load(r) for r in refs[:n_in]])
        for r, v in zip(refs[n_in:], res):
            r[...] = v.astype(r.dtype)

    return pl.pallas_call(
        body, grid=grid, name=name,
        in_specs=[pl.BlockSpec(a.block, a.imap) for a in args],
        out_specs=[pl.BlockSpec(b, im) for (_, _, b, im) in outs],
        out_shape=[_sds(s, d) for (s, d, _, _) in outs],
        compiler_params=_cparams(ng),
    )(*[a.arr for a in args])


def _store_grads(args, diff, pids, g_refs, d):
    for k, gr, dv in zip(diff, g_refs, d):
        a = args[k]
        if a.kind == "row" or not a.acc:
            gr[...] = dv.astype(gr.dtype)
        else:
            first = pids[a.acc[0]] == 0
            for ax in a.acc[1:]:
                first = jnp.logical_and(first, pids[ax] == 0)

            @pl.when(first)
            def _(gr=gr, dv=dv):
                gr[...] = dv.astype(gr.dtype)

            @pl.when(jnp.logical_not(first))
            def _(gr=gr, dv=dv):
                gr[...] += dv.astype(gr.dtype)


def tile_bwd(name, f, grid, args, cts):
    n_in, n_ct, ng = len(args), len(cts), len(grid)
    diff = [k for k, a in enumerate(args) if a.kind != "const" and a.grad]

    def body(*refs):
        pids = tuple(pl.program_id(k) for k in range(ng))
        vals = [_load(r) for r in refs[:n_in]]

        def g(*dv):
            full = list(vals)
            for k, v in zip(diff, dv):
                full[k] = v
            return tuple(f(pids, *full))

        _, vjp = jax.vjp(g, *[vals[k] for k in diff])
        d = vjp(tuple(_load(r) for r in refs[n_in:n_in + n_ct]))
        _store_grads(args, diff, pids, refs[n_in + n_ct:], d)

    return pl.pallas_call(
        body, grid=grid, name=name,
        in_specs=[pl.BlockSpec(a.block, a.imap) for a in args] + [pl.BlockSpec(b, im) for (_, b, im) in cts],
        out_specs=[pl.BlockSpec(args[k].block, args[k].imap) for k in diff],
        out_shape=[_sds(args[k].arr.shape, args[k].gdtype) for k in diff],
        compiler_params=_cparams(ng),
    )(*[a.arr for a in args], *[c[0] for c in cts])


def scan_fwd(name, f, n_heads, n_steps, args, outs, n_state):
    n_in, n_out = len(args), len(outs)
    sblock = (None, None, HEAD, HEAD)

    def body(*refs):
        in_refs = refs[:n_in]
        out_refs = refs[n_in:n_in + n_out]
        save_refs = refs[n_in + n_out:n_in + n_out + n_state]
        s_refs = refs[n_in + n_out + n_state:]

        @pl.when(pl.program_id(1) == 0)
        def _():
            for s in s_refs:
                s[...] = jnp.zeros_like(s)

        states = tuple(s[...] for s in s_refs)
        for sv, s in zip(save_refs, states):
            sv[...] = s
        new_states, res = f(states, *[_load(r) for r in in_refs])
        for s, v in zip(s_refs, new_states):
            s[...] = v
        for r, v in zip(out_refs, res):
            r[...] = v.astype(r.dtype)

    res = pl.pallas_call(
        body, grid=(n_heads, n_steps), name=name,
        in_specs=[pl.BlockSpec(a.block, a.imap) for a in args],
        out_specs=[pl.BlockSpec(b, im) for (_, _, b, im) in outs]
        + [pl.BlockSpec(sblock, lambda h, i: (h, i, 0, 0))] * n_state,
        out_shape=[_sds(s, d) for (s, d, _, _) in outs] + [_sds((n_heads, n_steps, HEAD, HEAD), F32)] * n_state,
        scratch_shapes=[pltpu.VMEM((HEAD, HEAD), F32)] * n_state,
        compiler_params=_cparams(2),
    )(*[a.arr for a in args])
    return res[:n_out], res[n_out:]


def scan_bwd(name, f, n_heads, n_steps, args, saves, cts):
    n_in, n_ct, n_state = len(args), len(cts), len(saves)
    diff = [k for k, a in enumerate(args) if a.kind != "const" and a.grad]
    sblock = (None, None, HEAD, HEAD)

    def rv(im):
        return lambda h, i: im(h, n_steps - 1 - i)

    def body(*refs):
        in_refs = refs[:n_in]
        save_refs = refs[n_in:n_in + n_state]
        ct_refs = refs[n_in + n_state:n_in + n_state + n_ct]
        g_refs = refs[n_in + n_state + n_ct:n_in + n_state + n_ct + len(diff)]
        ds_refs = refs[n_in + n_state + n_ct + len(diff):]
        pids = (pl.program_id(0), pl.program_id(1))

        @pl.when(pids[1] == 0)
        def _():
            for s in ds_refs:
                s[...] = jnp.zeros_like(s)

        vals = [_load(r) for r in in_refs]

        def g(states, *dv):
            full = list(vals)
            for k, v in zip(diff, dv):
                full[k] = v
            new_states, res = f(states, *full)
            return tuple(new_states), tuple(res)

        _, vjp = jax.vjp(g, tuple(s[...] for s in save_refs), *[vals[k] for k in diff])
        d = vjp((tuple(s[...] for s in ds_refs), tuple(_load(r) for r in ct_refs)))
        for s, v in zip(ds_refs, d[0]):
            s[...] = v
        _store_grads(args, diff, pids, g_refs, d[1:])

    return pl.pallas_call(
        body, grid=(n_heads, n_steps), name=name,
        in_specs=[pl.BlockSpec(a.block, rv(a.imap)) for a in args]
        + [pl.BlockSpec(sblock, rv(lambda h, i: (h, i, 0, 0)))] * n_state
        + [pl.BlockSpec(b, rv(im)) for (_, b, im) in cts],
        out_specs=[pl.BlockSpec(args[k].block, rv(args[k].imap)) for k in diff],
        out_shape=[_sds(args[k].arr.shape, args[k].gdtype) for k in diff],
        scratch_shapes=[pltpu.VMEM((HEAD, HEAD), F32)] * n_state,
        compiler_params=_cparams(2),
    )(*[a.arr for a in args], *saves, *[c[0] for c in cts])


def matmul(name, a, b, mode, add=None, out_dtype=F32, slabs=False):
    o_spec = None
    if mode == "nn":
        m, k = a.shape
        ns = b.shape[2] if slabs else b.shape[1]
        n = N_DEV * ns if slabs else ns
        to_m, to_n, tr = _pick(m, TM_PREFS), _pick(ns, TN_PREFS), _pick(k, TK_PREFS)
        nb = ns // to_n
        grid = (m // to_m, n // to_n, k // tr)
        a_spec = pl.BlockSpec((to_m, tr), lambda i, j, l: (i, l))
        if slabs:
            b_spec = pl.BlockSpec((None, tr, to_n), lambda i, j, l: (j // nb, l, j % nb))
        else:
            b_spec = pl.BlockSpec((tr, to_n), lambda i, j, l: (l, j))
        dims, oshape = (1, 0), (m, n)
    elif mode == "nt":
        m, n = a.shape
        k = b.shape[1] if slabs else b.shape[0]
        ns = n // N_DEV if slabs else n
        to_m, to_n, tr = _pick(m, TM_PREFS), _pick(k, TO_PREFS), _pick(ns, TK_PREFS)
        nb = ns // tr
        grid = (m // to_m, k // to_n, n // tr)
        a_spec = pl.BlockSpec((to_m, tr), lambda i, j, l: (i, l))
        if slabs:
            b_spec = pl.BlockSpec((None, to_n, tr), lambda i, j, l: (l // nb, j, l % nb))
        else:
            b_spec = pl.BlockSpec((to_n, tr), lambda i, j, l: (j, l))
        dims, oshape = (1, 1), (m, k)
    else:
        (t, k), n = a.shape, b.shape[1]
        ns = n // N_DEV if slabs else n
        to_m, to_n, tr = _pick(k, TO_PREFS), _pick(ns, TO_PREFS), _pick(t, TM_PREFS)
        nb = ns // to_n
        grid = (k // to_m, n // to_n, t // tr)
        a_spec = pl.BlockSpec((tr, to_m), lambda i, j, l: (l, i))
        b_spec = pl.BlockSpec((tr, to_n), lambda i, j, l: (l, j))
        dims, oshape = (0, 0), (k, n)
        if slabs:
            o_spec = pl.BlockSpec((None, to_m, to_n), lambda i, j, l: (j // nb, i, j % nb))
            oshape = (N_DEV, k, ns)
    n_red = grid[2]
    if o_spec is None:
        o_spec = pl.BlockSpec((to_m, to_n), lambda i, j, l: (i, j))
    has_add = add is not None

    def body(a_ref, b_ref, *rest):
        add_ref = rest[0] if has_add else None
        o_ref = rest[1] if has_add else rest[0]
        part = lax.dot_general(a_ref[...].astype(BF16), b_ref[...].astype(BF16),
                               (((dims[0],), (dims[1],)), ((), ())), preferred_element_type=F32)

        def finish(v):
            if has_add:
                v = v + add_ref[...]
            o_ref[...] = v.astype(o_ref.dtype)

        if n_red == 1:
            finish(part)
        else:
            acc = rest[-1]
            step = pl.program_id(2)

            @pl.when(step == 0)
            def _():
                acc[...] = part

            @pl.when(step > 0)
            def _():
                acc[...] += part

            @pl.when(step == n_red - 1)
            def _():
                finish(acc[...])

    return pl.pallas_call(
        body, grid=grid, name=name,
        in_specs=[a_spec, b_spec] + ([o_spec] if has_add else []),
        out_specs=o_spec, out_shape=_sds(oshape, out_dtype),
        scratch_shapes=[pltpu.VMEM((to_m, to_n), F32)] if n_red > 1 else [],
        compiler_params=_cparams(3),
    )(a, b, *([add] if has_add else []))


def assemble(name, pieces, out_dtype):
    flat = [s for piece in pieces for s in piece]
    t = flat[0][0].shape[0]
    widths = [piece[0][1] for piece in pieces]

    def body(*refs):
        o_ref, k, off = refs[-1], 0, 0
        for piece, w in zip(pieces, widths):
            v = refs[k][...].astype(F32)
            k += 1
            for _ in piece[1:]:
                v = v + refs[k][...].astype(F32)
                k += 1
            o_ref[:, off:off + w] = v.astype(o_ref.dtype)
            off += w

    tr = ROW_TILE // 2
    return pl.pallas_call(
        body, grid=(t // tr,), name=name,
        in_specs=[pl.BlockSpec((tr, w), functools.partial(lambda i, cb: (i, cb), cb=cb)) for (_, w, cb) in flat],
        out_specs=pl.BlockSpec((tr, sum(widths)), lambda i: (i, 0)),
        out_shape=_sds((t, sum(widths)), out_dtype),
        compiler_params=_cparams(1),
    )(*[s[0] for s in flat])


ELEM_ROWS = (128, 64, 32, 16, 8)


def _rows2d(a, lead=0):
    return a.reshape(a.shape[:lead] + (-1, a.shape[-1]))


def sum_leading(name, arr, out_dtype):
    k, rows, w = arr.shape
    tr = _pick(rows, ELEM_ROWS)

    def body(a_ref, o_ref):
        v = a_ref[0].astype(F32)
        for j in range(1, k):
            v = v + a_ref[j].astype(F32)
        o_ref[...] = v.astype(o_ref.dtype)

    return pl.pallas_call(
        body, grid=(rows // tr,), name=name,
        in_specs=[pl.BlockSpec((k, tr, w), lambda i: (0, i, 0))],
        out_specs=pl.BlockSpec((tr, w), lambda i: (i, 0)),
        out_shape=_sds((rows, w), out_dtype), compiler_params=_cparams(1),
    )(arr)


def add_own(name, g8, got, core, out_dtype):
    _, rows, w = g8.shape
    tr = _pick(rows, ELEM_ROWS)

    def body(core_ref, a_ref, b_ref, o_ref):
        o_ref[...] = (a_ref[...] + b_ref[...]).astype(o_ref.dtype)

    spec = pl.BlockSpec((None, tr, w), lambda q, i, core_ref: (q, i, 0))
    return pl.pallas_call(
        body, name=name,
        grid_spec=pltpu.PrefetchScalarGridSpec(
            num_scalar_prefetch=1, grid=(4, rows // tr),
            in_specs=[pl.BlockSpec((None, tr, w), lambda q, i, core_ref: (2 * q + core_ref[0], i, 0)), spec],
            out_specs=spec),
        out_shape=_sds(got.shape, out_dtype), compiler_params=_cparams(2),
    )(core, g8, got)


def adamw(name, g, w, m, v):
    rows, wd = g.shape
    tr = _pick(rows, ELEM_ROWS)

    def body(g_ref, w_ref, m_ref, v_ref, d_ref, nm_ref, nv_ref):
        gv = g_ref[...]
        mn = ADAM_B1 * m_ref[...] + (1.0 - ADAM_B1) * gv
        vn = ADAM_B2 * v_ref[...] + (1.0 - ADAM_B2) * jnp.square(gv)
        m_hat = mn / (1.0 - ADAM_B1 ** ADAM_STEP)
        v_hat = vn / (1.0 - ADAM_B2 ** ADAM_STEP)
        d_ref[...] = -ADAM_LR * (m_hat / (jnp.sqrt(v_hat) + ADAM_EPS) + ADAM_WD * w_ref[...])
        nm_ref[...] = mn
        nv_ref[...] = vn

    spec = pl.BlockSpec((tr, wd), lambda i: (i, 0))
    return pl.pallas_call(
        body, grid=(rows // tr,), name=name, in_specs=[spec] * 4, out_specs=[spec] * 3,
        out_shape=[_sds(g.shape, F32)] * 3, compiler_params=_cparams(1),
    )(g, w, m, v)


def loss_kernel(name, xs, target, n_ctx_tiles):
    t, d = xs.shape
    nt = t // ROW_TILE

    def body(x_ref, t_ref, dx_ref, l_ref):
        i = pl.program_id(0)
        is_lat = i >= n_ctx_tiles
        err = jnp.where(is_lat, x_ref[...] - t_ref[...], 0.0)
        dx_ref[...] = err / d
        part = 0.5 * jnp.sum(jnp.mean(jnp.square(err), axis=-1, keepdims=True), axis=0, keepdims=True)

        @pl.when(i == 0)
        def _():
            l_ref[...] = jnp.zeros_like(l_ref)

        l_ref[...] += jnp.broadcast_to(part, l_ref.shape)

    dx, l = pl.pallas_call(
        body, grid=(nt,), name=name,
        in_specs=[pl.BlockSpec((ROW_TILE, d), lambda i: (i, 0)),
                  pl.BlockSpec((ROW_TILE, d), lambda i: (jnp.maximum(i - n_ctx_tiles, 0), 0))],
        out_specs=[pl.BlockSpec((ROW_TILE, d), lambda i: (i, 0)), pl.BlockSpec((SUBLANES, LANES), lambda i: (0, 0))],
        out_shape=[_sds((t, d), F32), _sds((SUBLANES, LANES), F32)], compiler_params=_cparams(1),
    )(xs, target)
    return l[0, 0], dx


CONV_COLS = 2048
CONV_LEFT = C_CONV // 2


def _conv_halo_specs(t, n_ctx_tiles):
    nt = t // ROW_TILE
    per = ROW_TILE // SUBLANES
    cur = pl.BlockSpec((ROW_TILE, CONV_COLS), lambda j, i: (i, j))
    prev = pl.BlockSpec((SUBLANES, CONV_COLS), lambda j, i: (jnp.maximum(i * per - 1, 0), j))
    nxt = pl.BlockSpec((SUBLANES, CONV_COLS), lambda j, i: (jnp.minimum((i + 1) * per, nt * per - 1), j))
    return cur, prev, nxt


def _fill_ext(ext, prev_ref, cur_ref, next_ref, i, nt, n_ctx_tiles):
    has_prev = jnp.logical_and(i != 0, i != n_ctx_tiles)
    has_next = jnp.logical_and(i != n_ctx_tiles - 1, i != nt - 1)
    ext[0:SUBLANES, :] = jnp.where(has_prev, prev_ref[...], 0.0)
    ext[SUBLANES:SUBLANES + ROW_TILE, :] = cur_ref[...]
    ext[SUBLANES + ROW_TILE:, :] = jnp.where(has_next, next_ref[...], 0.0)


def conv_fwd(name, p, w, width, n_ctx_tiles):
    t = p.shape[0]
    nt = t // ROW_TILE
    cur, prev, nxt = _conv_halo_specs(t, n_ctx_tiles)

    def body(c_ref, p_ref, n_ref, w_ref, o_ref, ext):
        _fill_ext(ext, p_ref, c_ref, n_ref, pl.program_id(1), nt, n_ctx_tiles)
        acc = None
        for j in range(C_CONV):
            term = ext[pl.ds(SUBLANES + j - CONV_LEFT, ROW_TILE), :] * w_ref[j:j + 1, :]
            acc = term if acc is None else acc + term
        o_ref[...] = acc

    return pl.pallas_call(
        body, grid=(width // CONV_COLS, nt), name=name,
        in_specs=[cur, prev, nxt, pl.BlockSpec((C_CONV, CONV_COLS), lambda j, i: (0, j))],
        out_specs=cur, out_shape=_sds((t, width), F32),
        scratch_shapes=[pltpu.VMEM((ROW_TILE + 2 * SUBLANES, CONV_COLS), F32)],
        compiler_params=_cparams(2),
    )(p, p, p, w)


def conv_bwd(name, p, w, dz, width, n_ctx_tiles):
    t = p.shape[0]
    nt = t // ROW_TILE
    cur, prev, nxt = _conv_halo_specs(t, n_ctx_tiles)

    def body(c_ref, p_ref, n_ref, dc_ref, dp_ref, dn_ref, w_ref, du_ref, dw_ref, ext, dext):
        i = pl.program_id(1)
        _fill_ext(ext, p_ref, c_ref, n_ref, i, nt, n_ctx_tiles)
        _fill_ext(dext, dp_ref, dc_ref, dn_ref, i, nt, n_ctx_tiles)
        dzc = dc_ref[...]
        @pl.when(i == 0)
        def _():
            dw_ref[...] = jnp.zeros_like(dw_ref)

        acc = None
        for j in range(C_CONV):
            term = dext[pl.ds(SUBLANES + CONV_LEFT - j, ROW_TILE), :] * w_ref[j:j + 1, :]
            acc = term if acc is None else acc + term
            dw_ref[j:j + 1, :] += jnp.sum(dzc * ext[pl.ds(SUBLANES + j - CONV_LEFT, ROW_TILE), :], axis=0, keepdims=True)
        du_ref[...] = acc

    wspec = pl.BlockSpec((C_CONV, CONV_COLS), lambda j, i: (0, j))
    return pl.pallas_call(
        body, grid=(width // CONV_COLS, nt), name=name,
        in_specs=[cur, prev, nxt, cur, prev, nxt, wspec],
        out_specs=[cur, wspec], out_shape=[_sds((t, width), F32), _sds((C_CONV, width), F32)],
        scratch_shapes=[pltpu.VMEM((ROW_TILE + 2 * SUBLANES, CONV_COLS), F32)] * 2,
        compiler_params=_cparams(2),
    )(p, p, p, dz, dz, dz, w)


def _rms(x, w):
    return x * lax.rsqrt(jnp.mean(x * x, axis=-1, keepdims=True) + EPS) * w


def _seg_mod(mods, is_ctx):
    return jnp.where(is_ctx, mods[0], mods[1])


def f_norm_mod(shift_i, scale_i, n_ctx_tiles, passthrough=False):
    def f(pids, x, nw, mods):
        m = _seg_mod(mods, pids[0] < n_ctx_tiles)
        h = _rms(x, nw) * (1.0 + m[scale_i:scale_i + 1]) + m[shift_i:shift_i + 1]
        return (h, x) if passthrough else (h,)
    return f


def f_gate_res(gate_i, n_ctx_tiles):
    def f(pids, x, y, nw, mods):
        m = _seg_mod(mods, pids[0] < n_ctx_tiles)
        return (x + m[gate_i:gate_i + 1] * _rms(y, nw),)
    return f


def f_swiglu(pids, gu):
    half = gu.shape[1] // 2
    return (jax.nn.silu(gu[:, :half]) * gu[:, half:],)


def f_readout(n_heads):
    def f(pids, o_a, o_b, gate, nw):
        cols = [slice(j * HEAD, (j + 1) * HEAD) for j in range(n_heads)]
        outs = _each(lambda cs: _rms(o_a[:, cs] + o_b[:, cs], nw) * jax.nn.silu(gate[:, cs]), cols)
        return (jnp.concatenate(outs, axis=1) if n_heads > 1 else outs[0],)
    return f


def f_pool(pids, u, pmat, pw, scale):
    d = mm(pmat, u, 1, 0, True) - u
    return (mm(d, pw) * scale,)


def f_lb(layer):
    def f(pids, *slots):
        top = slots[0]
        for s in slots[1:]:
            top = jnp.maximum(top, s)
        ex = [jnp.exp(s - top) for s in slots]
        tot = ex[0]
        for e in ex[1:]:
            tot = tot + e
        part = ex[0]
        for e in ex[1:layer + 1]:
            part = part + e
        return (part / tot,)
    return f


def f_ada(pids, c_all, c_ctx, w, b):
    c16 = jnp.concatenate([c_all, jnp.broadcast_to(c_ctx, c_all.shape)], axis=0)
    return (mm(jax.nn.silu(c16), w) + b,)


def _each(fn, *lists):
    return [fn(*xs) for xs in zip(*lists)]


def _hgrn2_heads(sts, qrs, frs, irs, lbs, rev):
    c = A_CHUNK
    mid = c - c // 2 if rev else c // 2 - 1
    ri, ci = _iota2(c, c, 0), _iota2(c, c, 1)
    incl = (ri <= ci) if rev else (ri >= ci)
    incl_f = incl.astype(F32)
    qs = _each(jax.nn.silu, qrs)
    log_fs = _each(lambda lb, fr: jnp.log(lb + (1.0 - lb) * jax.nn.sigmoid(fr)), lbs, frs)
    ks = _each(lambda lb, fr: (1.0 - lb) * jax.nn.sigmoid(-fr), lbs, frs)
    bs = _each(lambda lf: mm(incl_f, lf, 1, 0, True), log_fs)
    b_lasts = _each(lambda lf: jnp.sum(lf, axis=0, keepdims=True), log_fs)
    scores = _each(lambda q, k, b: mm(q * jnp.exp(b - b[mid:mid + 1]), k * jnp.exp(b[mid:mid + 1] - b), 1, 1), qs, ks, bs)
    intra = _each(lambda sc, ir: mm(jnp.where(incl, sc, 0.0), ir), scores, irs)
    inter = _each(lambda q, b, st: mm(q * jnp.exp(b), st, 1, 1), qs, bs, sts)
    upd = _each(lambda ir, k, bl, b: mm(ir, k * jnp.exp(bl - b), 0, 0), irs, ks, b_lasts, bs)
    new = _each(lambda st, bl, u: st * jnp.exp(bl) + u, sts, b_lasts, upd)
    return new, _each(jnp.add, intra, inter)


def f_hgrn2(rev, hb):
    def f(states, qr, fr, ir, lb):
        cols = [slice(j * HEAD, (j + 1) * HEAD) for j in range(hb)]
        new, outs = _hgrn2_heads(list(states), [qr[:, cs] for cs in cols], [fr[:, cs] for cs in cols],
                                 [ir[:, cs] for cs in cols], [lb[j] for j in range(hb)], rev)
        return tuple(new), (jnp.concatenate(outs, axis=1) if hb > 1 else outs[0],)
    return f


def _neumann_inv(a_lows):
    n = a_lows[0].shape[0]
    eye = (_iota2(n, n, 0) == _iota2(n, n, 1)).astype(F32)
    ps = _each(lambda a: -a, a_lows)
    xs = _each(lambda p: eye + p, ps)
    ps = _each(lambda p: mm(p, p, 1, 0, True), ps)
    k = 2
    while 2 * k < n:
        ys = _each(lambda p, x: mm(jnp.concatenate([p, x], axis=0), p, 1, 0, True), ps, xs)
        ps = _each(lambda y: y[:n], ys)
        xs = _each(lambda x, y: x + y[n:], xs, ys)
        k *= 2
    return tuple(_each(lambda x, p: x + mm(x, p, 1, 0, True), xs, ps))


@jax.custom_vjp
def unit_tri_inv(a_lows):
    return _neumann_inv(a_lows)


def _uti_fwd(a_lows):
    xs = _neumann_inv(a_lows)
    return xs, xs


def _uti_bwd(xs, gs):
    ts = _each(lambda x, g: mm(x, g, 0, 0, True), xs, gs)
    return (tuple(_each(lambda t, x: -mm(t, x, 1, 1, True), ts, xs)),)


unit_tri_inv.defvjp(_uti_fwd, _uti_bwd)


@jax.custom_vjp
def unit_tri_inv_saved(a_lows, xs):
    return xs


def _utis_fwd(a_lows, xs):
    return xs, xs


def _utis_bwd(xs, gs):
    return _uti_bwd(xs, gs) + (tuple(jnp.zeros_like(x) for x in xs),)


unit_tri_inv_saved.defvjp(_utis_fwd, _utis_bwd)


def _l2n(x):
    return x * lax.rsqrt(jnp.sum(x * x, axis=-1, keepdims=True) + EPS)


def _gdn_heads(ss, qs, ks, vs, a_rows, b_rows, alogs, dtbs, rev, xs_saved=None):
    c = qs[0].shape[0]
    ri, ci = _iota2(c, c, 0), _iota2(c, c, 1)
    causal = (ri <= ci) if rev else (ri >= ci)
    causal_t = (ri >= ci) if rev else (ri <= ci)
    strict = (ri < ci) if rev else (ri > ci)
    eye = ri == ci
    sq = lambda row: jnp.broadcast_to(row, (c, c))
    to_col = lambda row: jnp.sum(jnp.where(eye, sq(row), 0.0), axis=1, keepdims=True)
    g_rows = _each(lambda al, a, dt: -jnp.exp(al) * jax.nn.softplus(a + dt), alogs, a_rows, dtbs)
    beta_cols = _each(lambda b: to_col(jax.nn.sigmoid(b)), b_rows)
    g_cols = _each(to_col, g_rows)
    gc_cols = _each(lambda g: jnp.sum(jnp.where(causal, sq(g), 0.0), axis=1, keepdims=True), g_rows)
    gc_rows = _each(lambda g: jnp.sum(jnp.where(causal_t, sq(g), 0.0), axis=0, keepdims=True), g_cols)
    gc_lasts = _each(lambda g: jnp.sum(g, axis=1, keepdims=True), g_rows)
    decays = _each(lambda gc, gr: jnp.where(causal, jnp.exp(jnp.where(causal, gc - gr, 0.0)), 0.0), gc_cols, gc_rows)
    k_betas = _each(jnp.multiply, ks, beta_cols)
    v_betas = _each(jnp.multiply, vs, beta_cols)
    kq_ks = _each(lambda kb, q, k: mm(jnp.concatenate([kb, q], axis=0), k, 1, 1), k_betas, qs, ks)
    a_lows = _each(lambda kk, dec: jnp.where(strict, kk[:c] * dec, 0.0), kq_ks, decays)
    qks = _each(lambda kk, dec: jnp.where(causal, kk[c:] * dec, 0.0), kq_ks, decays)
    xs = unit_tri_inv(tuple(a_lows)) if xs_saved is None else unit_tri_inv_saved(tuple(a_lows), tuple(xs_saved))
    egcs = _each(jnp.exp, gc_cols)
    uws = _each(lambda x, vb, kb, e: mm(x, jnp.concatenate([vb, kb * e], axis=1), 1, 0, True), xs, v_betas, k_betas, egcs)
    dv = vs[0].shape[1]
    wq_ss = _each(lambda uw, q, e, s: mm(jnp.concatenate([uw[:, dv:], q * e], axis=0), s), uws, qs, egcs, ss)
    v_news = _each(lambda uw, wq: uw[:, :dv] - wq[:c], uws, wq_ss)
    o_states = _each(lambda wq: wq[c:], wq_ss)
    o_locals = _each(mm, qks, v_news)
    upds = _each(lambda k, gl, gc, vn: mm(k * jnp.exp(gl - gc), vn, 0, 0), ks, gc_lasts, gc_cols, v_news)
    new = _each(lambda s, gl, u: s * jnp.exp(gl) + u, ss, gc_lasts, upds)
    return new, _each(jnp.add, o_states, o_locals), xs


def f_gdn(rev, khb, saved_inverse):
    def f(states, qr, kr, vr, a3, b3, alog3, dtb3, xcat=None):
        heads = [(j, r) for j in range(khb) for r in range(2)]
        cols = [slice(j * HEAD, (j + 1) * HEAD) for j in range(khb)]
        c = qr.shape[0]
        qk_ = _each(lambda cs: (_l2n(jax.nn.silu(qr[:, cs])) * (HEAD ** -0.5), _l2n(jax.nn.silu(kr[:, cs]))), cols)
        vs = [jax.nn.silu(vr[:, (2 * j + r) * HEAD:(2 * j + r + 1) * HEAD]) for j, r in heads]
        row = lambda arr3: [arr3[j][r:r + 1] for j, r in heads]
        xs_saved = [xcat[n * c:(n + 1) * c] for n in range(len(heads))] if saved_inverse else None
        new, outs, xs = _gdn_heads(list(states), [qk_[j][0] for j, _ in heads], [qk_[j][1] for j, _ in heads], vs,
                                   row(a3), row(b3), row(alog3), row(dtb3), rev, xs_saved)
        o = jnp.concatenate(outs, axis=1)
        return tuple(new), ((o,) if saved_inverse else (o, jnp.concatenate(xs, axis=0)))
    return f


def _hbm_spec():
    return pl.BlockSpec(memory_space=pltpu.HBM)


def all_gather(name, xs):
    nt = len(xs)

    def body(*refs):
        x_refs, out_refs = refs[:nt], refs[nt:2 * nt]
        send_sems, recv_sems, local_sems = refs[2 * nt:]
        x, y, c = lax.axis_index("x"), lax.axis_index("y"), lax.axis_index("c")
        me, sibling = (x, y, c), (x, y, 1 - c)
        chips = [(1 - x, y), (x, 1 - y), (1 - x, 1 - y)]

        def slab(t, px, py, pc):
            return out_refs[t].at[4 * px + 2 * py + pc]

        def copy(t, k, block, to, src=None):
            return pltpu.make_async_remote_copy(
                src_ref=slab(t, *block) if src is None else src, dst_ref=slab(t, *block),
                send_sem=send_sems.at[7 * t + k], recv_sem=recv_sems.at[7 * t + k], device_id=to, device_id_type=MESH)

        mine, first, passed = [], [], []
        for t in range(nt):
            mine.append(pltpu.make_async_copy(x_refs[t], slab(t, *me), local_sems.at[t]))
            mine[-1].start()
            cps = [copy(t, 0, me, sibling, src=x_refs[t])]
            cps += [copy(t, 1 + j, me, (*chip, c), src=x_refs[t]) for j, chip in enumerate(chips)]
            for cp in cps:
                cp.start()
            first += cps
        for j, chip in enumerate(chips):
            for t in range(nt):
                copy(t, 1 + j, (*chip, c), me).wait_recv()
                fw = copy(t, 4 + j, (*chip, c), sibling)
                fw.start()
                passed.append(fw)
        for t in range(nt):
            copy(t, 0, sibling, me).wait_recv()
            for j, chip in enumerate(chips):
                copy(t, 4 + j, (*chip, 1 - c), me).wait_recv()
        for cp in first + passed:
            cp.wait_send()
        for cp in mine:
            cp.wait()

    return pl.pallas_call(
        body, name=name, out_shape=[_sds((N_DEV,) + a.shape, a.dtype) for a in xs],
        in_specs=[_hbm_spec()] * nt, out_specs=[_hbm_spec()] * nt,
        scratch_shapes=[pltpu.SemaphoreType.DMA((7 * nt,)), pltpu.SemaphoreType.DMA((7 * nt,)),
                        pltpu.SemaphoreType.DMA((nt,))],
    )(*xs)


def _peers(x, y, c):
    out = []
    for k in range(1, N_DEV):
        px = 1 - x if k & 4 else x
        py = 1 - y if k & 2 else y
        pc = 1 - c if k & 1 else c
        out.append((px, py, pc))
    return out


def _exchange_copies(src_refs, land_refs, send_sems, recv_sems, scatter):
    x, y, c = lax.axis_index("x"), lax.axis_index("y"), lax.axis_index("c")
    me = 4 * x + 2 * y + c
    sends, recvs = [], []
    for t, (src, land) in enumerate(zip(src_refs, land_refs)):
        for k, (px, py, pc) in enumerate(_peers(x, y, c)):
            peer = 4 * px + 2 * py + pc
            sem = dict(send_sem=send_sems.at[7 * t + k], recv_sem=recv_sems.at[7 * t + k],
                       device_id=(px, py, pc), device_id_type=MESH)
            src_k = src.at[peer] if scatter else src
            sends.append(pltpu.make_async_remote_copy(src_ref=src_k, dst_ref=land.at[me], **sem))
            recvs.append(pltpu.make_async_remote_copy(src_ref=src_k, dst_ref=land.at[peer], **sem))
    return sends, recvs


def exchange_start(name, srcs, scatter, after, carry=None):
    nt = len(srcs)
    lands = [lax.empty(s.shape if scatter else (N_DEV,) + s.shape, s.dtype) for s in srcs]
    thru = list(srcs) + lands + ([carry] if carry is not None else [])
    n_thru = len(thru)

    def body(*refs):
        src_refs, land_refs = refs[:nt], refs[nt:2 * nt]
        send_sems, recv_sems = refs[n_thru + 1], refs[n_thru + 2]
        token = refs[-1]
        sends, _ = _exchange_copies(src_refs, land_refs, send_sems, recv_sems, scatter)
        for cp in sends:
            cp.start()
        token[...] = jnp.zeros_like(token)

    res = pl.pallas_call(
        body, name=name,
        out_shape=(pltpu.SemaphoreType.DMA((7 * nt,)), pltpu.SemaphoreType.DMA((7 * nt,)),
                   *[pltpu.HBM(a.shape, a.dtype) for a in thru], _sds((SUBLANES, LANES), F32)),
        in_specs=[_hbm_spec()] * n_thru + [pl.BlockSpec(memory_space=pl.ANY)],
        out_specs=(pl.BlockSpec(memory_space=pltpu.SEMAPHORE), pl.BlockSpec(memory_space=pltpu.SEMAPHORE),
                   *[_hbm_spec()] * n_thru, pl.BlockSpec(memory_space=pltpu.VMEM)),
        input_output_aliases={i: 2 + i for i in range(n_thru)},
        compiler_params=pltpu.CompilerParams(has_side_effects=pltpu.SideEffectType.DATAFLOW_SIDE_EFFECTING),
    )(*[pltpu.with_memory_space_constraint(a, pltpu.HBM) for a in thru], after)
    return (res[0], res[1], list(res[2:2 + nt]), list(res[2 + nt:2 + 2 * nt]), res[-1],
            res[2 + 2 * nt] if carry is not None else None)


def exchange_wait(name, started, after, scatter):
    send_sems, recv_sems, srcs, lands = started[:4]
    nt = len(srcs)

    def body(*refs):
        src_refs, land_refs = refs[:nt], refs[nt:2 * nt]
        sends, recvs = _exchange_copies(src_refs, land_refs, refs[2 * nt], refs[2 * nt + 1], scatter)
        for cp in sends:
            cp.wait_send()
        for cp in recvs:
            cp.wait_recv()

    hbm = lambda a: pltpu.HBM(a.shape, a.dtype)
    sem = pl.BlockSpec(memory_space=pltpu.SEMAPHORE)
    res = pl.pallas_call(
        body, name=name,
        out_shape=(*[hbm(a) for a in srcs], *[hbm(a) for a in lands]),
        in_specs=[_hbm_spec()] * (2 * nt) + [sem, sem, pl.BlockSpec(memory_space=pl.ANY)],
        out_specs=tuple([_hbm_spec()] * (2 * nt)),
        input_output_aliases={i: i for i in range(2 * nt)},
        compiler_params=pltpu.CompilerParams(has_side_effects=pltpu.SideEffectType.DATAFLOW_SIDE_EFFECTING),
    )(*srcs, *lands, send_sems, recv_sems, after)
    return list(res[:nt]), list(res[nt:])


def sibling_exchange(name, gs):
    nt = len(gs)

    def body(*refs):
        g_refs, l_refs, send_sems, recv_sems = refs[:nt], refs[nt:2 * nt], refs[2 * nt], refs[2 * nt + 1]
        x, y, c = lax.axis_index("x"), lax.axis_index("y"), lax.axis_index("c")
        cps = []
        for t in range(nt):
            for q in range(4):
                cps.append(pltpu.make_async_remote_copy(
                    src_ref=g_refs[t].at[2 * q + (1 - c)], dst_ref=l_refs[t].at[q], send_sem=send_sems.at[4 * t + q],
                    recv_sem=recv_sems.at[4 * t + q], device_id=(x, y, 1 - c), device_id_type=MESH))
        for cp in cps:
            cp.start()
        for cp in cps:
            cp.wait()

    return pl.pallas_call(
        body, name=name, out_shape=[_sds((4,) + g.shape[1:], g.dtype) for g in gs],
        in_specs=[_hbm_spec()] * nt, out_specs=[_hbm_spec()] * nt,
        scratch_shapes=[pltpu.SemaphoreType.DMA((4 * nt,)), pltpu.SemaphoreType.DMA((4 * nt,))],
    )(*gs)


def chip_exchange(name, hs):
    nt = len(hs)

    def body(*refs):
        h_refs, r_refs = refs[:nt], refs[nt:2 * nt]
        send_sems, recv_sems, local_sems = refs[2 * nt:]
        x, y, c = lax.axis_index("x"), lax.axis_index("y"), lax.axis_index("c")
        my = 2 * x + y
        chips = [(1 - x, y), (x, 1 - y), (1 - x, 1 - y)]

        def copy(t, k, src_slot, dst_slot, to):
            return pltpu.make_async_remote_copy(
                src_ref=h_refs[t].at[src_slot], dst_ref=r_refs[t].at[dst_slot], send_sem=send_sems.at[3 * t + k],
                recv_sem=recv_sems.at[3 * t + k], device_id=(*to, c), device_id_type=MESH)

        mine, sends = [], []
        for t in range(nt):
            mine.append(pltpu.make_async_copy(h_refs[t].at[my], r_refs[t].at[my], local_sems.at[t]))
            mine[-1].start()
            for k, (qx, qy) in enumerate(chips):
                sends.append(copy(t, k, 2 * qx + qy, my, (qx, qy)))
                sends[-1].start()
        for t in range(nt):
            for k, (qx, qy) in enumerate(chips):
                copy(t, k, my, 2 * qx + qy, (qx, qy)).wait_recv()
        for cp in sends:
            cp.wait_send()
        for cp in mine:
            cp.wait()

    return pl.pallas_call(
        body, name=name, out_shape=[_sds(h.shape, h.dtype) for h in hs],
        in_specs=[_hbm_spec()] * nt, out_specs=[_hbm_spec()] * nt,
        scratch_shapes=[pltpu.SemaphoreType.DMA((3 * nt,)), pltpu.SemaphoreType.DMA((3 * nt,)),
                        pltpu.SemaphoreType.DMA((nt,))],
    )(*hs)


def _pack(arrs, dtype, width, row_mult, lead=0):
    ld = arrs[0].shape[:lead]
    flat = jnp.concatenate([a.reshape(ld + (-1,)).astype(dtype) for a in arrs], axis=-1)
    n = flat.shape[-1]
    q = width * row_mult
    npad = -(-n // q) * q
    flat = jnp.pad(flat, [(0, 0)] * lead + [(0, npad - n)])
    return flat.reshape(ld + (npad // width, width))


def _unpack(buf, shapes, lead=0):
    ld = buf.shape[:lead]
    flat = buf.reshape(ld + (-1,))
    out, off = [], 0
    for s in shapes:
        n = int(np.prod(s))
        out.append(flat[..., off:off + n].reshape(ld + tuple(s)))
        off += n
    return out


def _pool_mats(seg_len):
    mats = np.zeros((len(POOL_WINDOWS), ROW_TILE, ROW_TILE), np.float32)
    for gi, win in enumerate(POOL_WINDOWS):
        for p in range(ROW_TILE):
            base = (p // seg_len) * seg_len
            q = p - base
            lo = min(max(q - win // 2, 0), seg_len - 1)
            hi = min(max(q + win - 1 - win // 2, 0), seg_len - 1)
            mats[gi, p, base + lo:base + hi + 1] = 1.0 / (hi - lo + 1)
    return mats


def kernel(x, c, ctx, c_ctx, w_ada, b_ada, norm_w, ev_w_in, ev_lb, ev_a_norm, ev_pool_w, ev_pool_scale, ev_w_out, od_w_in, od_conv, od_A_log, od_dt_bias, od_norm, od_w_out, ffn_w13, ffn_w2, loss_target, m_c_ctx, m_w_ada, m_b_ada, m_norm_w, m_ev_w_in, m_ev_lb, m_ev_a_norm, m_ev_pool_w, m_ev_pool_scale, m_ev_w_out, m_od_w_in, m_od_conv, m_od_A_log, m_od_dt_bias, m_od_norm, m_od_w_out, m_ffn_w13, m_ffn_w2, v_c_ctx, v_w_ada, v_b_ada, v_norm_w, v_ev_w_in, v_ev_lb, v_ev_a_norm, v_ev_pool_w, v_ev_pool_scale, v_ev_w_out, v_od_w_in, v_od_conv, v_od_A_log, v_od_dt_bias, v_od_norm, v_od_w_out, v_ffn_w13, v_ffn_w2):
    names = ["c_ctx", "w_ada", "b_ada", "norm_w", "ev_w_in", "ev_lb", "ev_a_norm", "ev_pool_w", "ev_pool_scale",
             "ev_w_out", "od_w_in", "od_conv", "od_A_log", "od_dt_bias", "od_norm", "od_w_out", "ffn_w13", "ffn_w2"]
    wts = dict(zip(names, [c_ctx, w_ada, b_ada, norm_w, ev_w_in, ev_lb, ev_a_norm, ev_pool_w, ev_pool_scale,
                           ev_w_out, od_w_in, od_conv, od_A_log, od_dt_bias, od_norm, od_w_out, ffn_w13, ffn_w2]))
    mom1 = dict(zip(names, [m_c_ctx, m_w_ada, m_b_ada, m_norm_w, m_ev_w_in, m_ev_lb, m_ev_a_norm, m_ev_pool_w,
                            m_ev_pool_scale, m_ev_w_out, m_od_w_in, m_od_conv, m_od_A_log, m_od_dt_bias, m_od_norm,
                            m_od_w_out, m_ffn_w13, m_ffn_w2]))
    mom2 = dict(zip(names, [v_c_ctx, v_w_ada, v_b_ada, v_norm_w, v_ev_w_in, v_ev_lb, v_ev_a_norm, v_ev_pool_w,
                            v_ev_pool_scale, v_ev_w_out, v_od_w_in, v_od_conv, v_od_A_log, v_od_dt_bias, v_od_norm,
                            v_od_w_out, v_ffn_w13, v_ffn_w2]))

    ax, ay, ac = lax.axis_index("x"), lax.axis_index("y"), lax.axis_index("c")
    me = 4 * ax + 2 * ay + ac
    my_chip = 2 * ax + ay

    seq, d = x.shape[1], x.shape[2]
    n_ctx = ctx.shape[1]
    t = n_ctx + seq
    nt = t // ROW_TILE
    nct = n_ctx // ROW_TILE
    assert n_ctx == ROW_TILE and seq % ROW_TILE == 0 and ROW_TILE % GRID_W == 0
    depth = w_ada.shape[0]
    aw = d // 2
    n_ah = aw // HEAD
    n_grp = len(POOL_WINDOWS)
    dg = aw // n_grp
    assert dg % LANES == 0
    n_kh = d // HEAD
    kw, vw = n_kh * HEAD, 2 * n_kh * HEAD
    n_gate = 8 * n_kh
    ffn_h = ffn_w2.shape[1] * N_DEV
    ada_loc = w_ada.shape[2]
    assert depth == 2

    bf = lambda w_: w_.astype(BF16)

    def gathered(started, after, name):
        srcs, lands = exchange_wait(name, started, after, False)
        return [lax.dynamic_update_index_in_dim(l_, s_, me, 0) for l_, s_ in zip(lands, srcs)]

    small_shapes = [(d,), norm_w.shape, ev_lb.shape, ev_pool_w.shape[1:], od_conv.shape[1:]]
    (g1,) = all_gather("ag_small", [_pack([c[0], norm_w, ev_lb, ev_pool_w[0], od_conv[0]], F32, LANES, SUBLANES)])
    c_all, nw_g, lb_g, pw_g, cv_g = _unpack(g1, small_shapes, lead=1)
    nw_full = nw_g.transpose(1, 2, 0, 3).reshape(depth, 4, d)
    lb_full = lb_g.transpose(1, 2, 0, 3).reshape(2, depth + 1, aw)
    pw_full = pw_g.transpose(1, 0, 2, 3).reshape(n_grp, dg, dg)
    cv_full = cv_g.transpose(1, 0, 2).reshape(C_CONV, 2 * kw + vw)

    g_ev_in, g_ev_out = all_gather("ag_weights_ev", [bf(ev_w_in[0]), bf(ev_w_out[0])])

    def cols_natural(g):
        return g.transpose(1, 0, 2).reshape(g.shape[1], N_DEV * g.shape[2])

    def rows_natural(g):
        return g.reshape(N_DEV * g.shape[1], g.shape[2])

    def col_weight(g):
        return (g, True) if g.shape[2] % LANES == 0 else (cols_natural(g), False)

    w_ev_in = col_weight(g_ev_in)
    w_ev_out = rows_natural(g_ev_out)
    w13, w2 = [None] * depth, [None] * depth

    b_loc = lax.dynamic_slice_in_dim(b_ada, me * ada_loc, ada_loc, axis=1).reshape(depth, 1, ada_loc)
    ada_cb = _pick(ada_loc, TN_PREFS)
    ada_grid = (depth, ada_loc // ada_cb)
    ada_args = [
        TArg(c_all, (N_DEV, d), lambda l, j: (0, 0), "const"),
        TArg(c_ctx.reshape(1, d), (1, d), lambda l, j: (0, 0), "par", (0, 1)),
        TArg(w_ada, (None, d, ada_cb), lambda l, j: (l, 0, j)),
        TArg(b_loc, (None, 1, ada_cb), lambda l, j: (l, 0, j)),
    ]
    (m_loc,) = tile_fwd("ada_fwd", f_ada, ada_grid, ada_args,
                        [((depth, 2 * N_DEV, ada_loc), F32, (None, 2 * N_DEV, ada_cb), lambda l, j: (l, 0, j))])
    (m_all,) = all_gather("ag_mod", [m_loc])
    mods = []
    for layer in range(depth):
        lat = lax.dynamic_index_in_dim(m_all[:, layer], me, axis=1, keepdims=False).reshape(6, d)
        cxt = lax.dynamic_index_in_dim(m_all[:, layer], N_DEV + me, axis=1, keepdims=False).reshape(6, d)
        mods.append(jnp.stack([cxt, lat]))

    gathers_done = mods[0][0, :1, :SUBLANES] + g_ev_out[0, :1, :SUBLANES].astype(F32)
    ag_ffn0 = exchange_start("ag_start_ffn0", [bf(ffn_w13[0]), bf(ffn_w2[0])], False, gathers_done)
    ag_l1 = exchange_start("ag_start_l1", [bf(od_w_in[0]), bf(od_w_out[0]), bf(ffn_w13[1]), bf(ffn_w2[1])], False,
                           gathers_done)
    mods[0] = mods[0] + (ag_ffn0[4][0, 0] + ag_l1[4][0, 0])

    lb_slots = [TArg(lb_full[:, j], (2, aw), lambda i: (0, 0)) for j in range(depth + 1)]
    (lb0,) = tile_fwd("lb_fwd", f_lb(0), (1,), lb_slots, [((2, aw), F32, (2, aw), lambda i: (0, 0))])
    lb0r = lb0.reshape(2, n_ah, 1, HEAD)

    full_row = lambda i: (i, 0)
    par0 = lambda i: (0, 0)

    def nm_args(xs, layer, slot):
        return [TArg(xs, (ROW_TILE, d), full_row),
                TArg(nw_full[layer, slot].reshape(1, d), (1, d), par0, "par", (0,)),
                TArg(mods[layer], (2, 6, d), lambda i: (0, 0, 0), "par", (0,))]

    def norm_mod(name, xs, layer, slot, si, ci):
        (h,) = tile_fwd(name, f_norm_mod(si, ci, nct), (nt,), nm_args(xs, layer, slot),
                        [((t, d), BF16, (ROW_TILE, d), full_row)])
        return h

    def norm_mod_bwd(name, xs, layer, slot, si, ci, dh, carry):
        return tile_bwd(name, f_norm_mod(si, ci, nct, True), (nt,), nm_args(xs, layer, slot),
                        [(dh, (ROW_TILE, d), full_row), (carry, (ROW_TILE, d), full_row)])

    def gr_args(xs, ys, layer, slot):
        return [TArg(xs, (ROW_TILE, d), full_row, grad=False), TArg(ys, (ROW_TILE, d), full_row, gdtype=BF16),
                TArg(nw_full[layer, slot].reshape(1, d), (1, d), par0, "par", (0,)),
                TArg(mods[layer], (2, 6, d), lambda i: (0, 0, 0), "par", (0,))]

    def gate_res(name, xs, ys, layer, slot, gi):
        (o,) = tile_fwd(name, f_gate_res(gi, nct), (nt,), gr_args(xs, ys, layer, slot),
                        [((t, d), F32, (ROW_TILE, d), full_row)])
        return o

    def gate_res_bwd(name, xs, ys, layer, slot, gi, dx):
        return tile_bwd(name, f_gate_res(gi, nct), (nt,), gr_args(xs, ys, layer, slot),
                        [(dx, (ROW_TILE, d), full_row)])

    sw_rows = ROW_TILE // 2

    def sw_args(gu):
        return [TArg(gu, (sw_rows, 2 * ffn_h), full_row, gdtype=BF16)]

    def ffn_fwd(tag, xs, layer):
        h2 = norm_mod(f"nm2_{tag}", xs, layer, 2, 3, 4)
        gu = matmul(f"w13_{tag}", h2, w13[layer][0], "nn", slabs=w13[layer][1], out_dtype=BF16)
        (act,) = tile_fwd(f"swiglu_{tag}", f_swiglu, (t // sw_rows,), sw_args(gu),
                          [((t, ffn_h), BF16, (sw_rows, ffn_h), full_row)])
        fo = matmul(f"w2_{tag}", act, w2[layer], "nn")
        xn = gate_res(f"gr2_{tag}", xs, fo, layer, 3, 5)
        return xn, (xs, h2, gu, act, fo)

    def col_grad(name, a, dy, slabs):
        g = matmul(name, a, dy, "tn", slabs=slabs, out_dtype=BF16)
        return g if slabs else g.reshape(g.shape[0], N_DEV, g.shape[1] // N_DEV).transpose(1, 0, 2)

    def row_grad(name, a, dy):
        g = matmul(name, a, dy, "tn", out_dtype=BF16)
        return g.reshape(N_DEV, g.shape[0] // N_DEV, g.shape[1])

    def ffn_bwd(tag, saved, layer, dxn, acc):
        xs, h2, gu, act, fo = saved
        dfo, dnw3, dmod_a = gate_res_bwd(f"gr2b_{tag}", xs, fo, layer, 3, 5, dxn)
        dact = matmul(f"w2d_{tag}", dfo, w2[layer], "nt")
        dw2 = row_grad(f"w2w_{tag}", act, dfo)
        (dgu,) = tile_bwd(f"swiglub_{tag}", f_swiglu, (t // sw_rows,), sw_args(gu), [(dact, (sw_rows, ffn_h), full_row)])
        dh2 = matmul(f"w13d_{tag}", dgu, w13[layer][0], "nt", slabs=w13[layer][1])
        dw13 = col_grad(f"w13w_{tag}", h2, dgu, w13[layer][1])
        dxs, dnw2, dmod_b = norm_mod_bwd(f"nm2b_{tag}", xs, layer, 2, 3, 4, dh2, dxn)
        acc["ffn_w13"][layer] = dw13
        acc["ffn_w2"][layer] = dw2
        acc["norm_w"][layer][2] = dnw2
        acc["norm_w"][layer][3] = dnw3
        acc["mods"][layer].extend([dmod_a, dmod_b])
        return dxs

    def head_cols(width):
        return (ROW_TILE, width)

    n_a = t // A_CHUNK
    nca = n_ctx // A_CHUNK

    def a_tok(rev):
        if not rev:
            return lambda i: i
        return lambda i: jnp.where(i < nca, nca - 1 - i, n_a + nca - 1 - i)

    hb = _pick(n_ah, (HEADS_PER_STEP, 2, 1))
    n_hblk = n_ah // hb

    def hg_args(p, direction):
        tok = a_tok(direction == 1)
        blk = (A_CHUNK, hb * HEAD)
        return [TArg(p, blk, lambda h, i: (tok(i), h)),
                TArg(p, blk, lambda h, i: (tok(i), (1 + direction) * n_hblk + h)),
                TArg(p, blk, lambda h, i: (tok(i), 3 * n_hblk + h)),
                TArg(lb0r, (None, hb, 1, HEAD), lambda h, i: (direction, h, 0, 0), "par", (1,))]

    pmats = jnp.asarray(np.stack([_pool_mats(n_ctx), _pool_mats(GRID_W)]))

    def pool_args(p):
        return [TArg(p, (ROW_TILE, dg), lambda g, i: (i, 5 * n_grp + g)),
                TArg(pmats, (None, None, ROW_TILE, ROW_TILE), lambda g, i: (jnp.where(i < nct, 0, 1), g, 0, 0), "const"),
                TArg(pw_full, (None, dg, dg), lambda g, i: (g, 0, 0), "par", (1,)),
                TArg(ev_pool_scale, (1, dg), lambda g, i: (0, g), "par", (1,))]

    def ro_plan(gate_off, n_heads):
        per = _pick(n_heads, (8, 4, 2, 1))
        assert gate_off % per == 0
        return per, n_heads // per, gate_off // per

    def ro_args(o_f, o_b, gate_arr, gate_off, nw_arr, n_heads):
        per, _, goff = ro_plan(gate_off, n_heads)
        blk = (ROW_TILE, per * HEAD)
        return [TArg(o_f, blk, lambda h, i: (i, h)), TArg(o_b, blk, lambda h, i: (i, h), grad=False),
                TArg(gate_arr, blk, lambda h, i: (i, goff + h)),
                TArg(nw_arr, (1, HEAD), lambda h, i: (0, 0), "par", (0, 1))]

    def readout(name, o_f, o_b, gate_arr, gate_off, nw_arr, n_heads):
        per, nblk, _ = ro_plan(gate_off, n_heads)
        (o,) = tile_fwd(name, f_readout(per), (nblk, nt), ro_args(o_f, o_b, gate_arr, gate_off, nw_arr, n_heads),
                        [((t, n_heads * HEAD), BF16, (ROW_TILE, per * HEAD), lambda hh, i: (i, hh))])
        return o

    def readout_bwd(name, o_f, o_b, gate_arr, gate_off, nw_arr, n_heads, dout):
        per, nblk, _ = ro_plan(gate_off, n_heads)
        return tile_bwd(name, f_readout(per), (nblk, nt), ro_args(o_f, o_b, gate_arr, gate_off, nw_arr, n_heads),
                        [(dout, (ROW_TILE, per * HEAD), lambda hh, i: (i, hh))])

    def even_fwd(tag, xs, layer):
        h = norm_mod(f"nm1_{tag}", xs, layer, 0, 0, 1)
        p = matmul(f"win_{tag}", h, w_ev_in[0], "nn", slabs=w_ev_in[1])
        outs, saves = [], []
        for direction in (0, 1):
            (o,), sv = scan_fwd(f"hgrn_{tag}_{direction}", f_hgrn2(direction == 1, hb), n_hblk, n_a, hg_args(p, direction),
                                [((t, aw), F32, (A_CHUNK, hb * HEAD), lambda hh, i, tok=a_tok(direction == 1): (tok(i), hh))], hb)
            outs.append(o)
            saves.append(sv)
        a_out = readout(f"ro_{tag}", outs[0], outs[1], p, 4 * n_ah, ev_a_norm, n_ah)
        (pooled,) = tile_fwd(f"pool_{tag}", f_pool, (n_grp, nt), pool_args(p),
                             [((t, aw), BF16, (ROW_TILE, dg), lambda g, i: (i, g))])
        cat = assemble(f"cat_{tag}", [[(a_out, aw, 0)], [(pooled, aw, 0)]], BF16)
        y = matmul(f"wout_{tag}", cat, w_ev_out, "nn")
        xn = gate_res(f"gr1_{tag}", xs, y, layer, 1, 2)
        return xn, (xs, h, p, outs, saves, cat, y)

    def even_bwd(tag, saved, layer, dxn, acc):
        xs, h, p, outs, saves, cat, y = saved
        dy, dnw1, dmod_a = gate_res_bwd(f"gr1b_{tag}", xs, y, layer, 1, 2, dxn)
        dcat = matmul(f"woutd_{tag}", dy, w_ev_out, "nt")
        acc["ev_w_out"] = row_grad(f"woutw_{tag}", cat, dy)
        do, dgate, d_anorm = readout_bwd(f"rob_{tag}", outs[0], outs[1], p, 4 * n_ah, ev_a_norm, n_ah, dcat)
        du, d_pw, d_ps = tile_bwd(f"poolb_{tag}", f_pool, (n_grp, nt), pool_args(p),
                                  [(dcat, (ROW_TILE, dg), lambda g, i: (i, n_grp + g))])
        dq, df, di, dlb = [], [], [], []
        for direction in (0, 1):
            r = scan_bwd(f"hgrnb_{tag}_{direction}", f_hgrn2(direction == 1, hb), n_hblk, n_a, hg_args(p, direction),
                         saves[direction],
                         [(do, (A_CHUNK, hb * HEAD), lambda hh, i, tok=a_tok(direction == 1): (tok(i), hh))])
            dq.append(r[0])
            df.append(r[1])
            di.append(r[2])
            dlb.append(r[3])
        sec = lambda arr, s: (arr, aw, s)
        dp = assemble(f"dp_{tag}", [[sec(dq[0], 0), sec(dq[1], 0)], [sec(df[0], 1)], [sec(df[1], 2)],
                                    [sec(di[0], 3), sec(di[1], 3)], [sec(dgate, 4)], [sec(du, 5)]], BF16)
        acc["ev_w_in"] = col_grad(f"winw_{tag}", h, dp, w_ev_in[1])
        acc["rs_ev"] = exchange_start("rs_start_ev", [acc["ev_w_in"], acc["ev_w_out"]], True, dp, carry=w_ev_in[0])
        dh = matmul(f"wind_{tag}", dp, acc["rs_ev"][5], "nt", slabs=w_ev_in[1])
        dxs, dnw0, dmod_b = norm_mod_bwd(f"nm1b_{tag}", xs, layer, 0, 0, 1, dh, dxn)
        acc["norm_w"][layer][0] = dnw0
        acc["norm_w"][layer][1] = dnw1
        acc["mods"][layer].extend([dmod_a, dmod_b])
        acc["ev_a_norm"] = d_anorm
        acc["ev_pool_w"] = d_pw
        acc["ev_pool_scale"] = d_ps
        acc["lb0"] = jnp.stack([dlb[0][0], dlb[1][1]]).reshape(2, aw)
        return dxs

    n_c = t // C_CHUNK
    ncc = n_ctx // C_CHUNK

    def c_tok(rev):
        if not rev:
            return lambda i: i
        return lambda i: jnp.where(i < ncc, ncc - 1 - i, n_c + ncc - 1 - i)

    alog = od_A_log[0].reshape(2, n_kh, 2, 1)
    dtb = od_dt_bias[0].reshape(2, n_kh, 2, 1)

    khb = _pick(n_kh, (HEADS_PER_STEP, 2, 1))
    n_kblk = n_kh // khb

    def gd_args(z, gates, direction):
        tok = c_tok(direction == 1)
        gblk = (None, khb, None, 2, C_CHUNK)
        sblk = (None, khb, 2, 1)
        return [TArg(z, (C_CHUNK, khb * HEAD), lambda kb, i: (tok(i), kb)),
                TArg(z, (C_CHUNK, khb * HEAD), lambda kb, i: (tok(i), n_kblk + kb)),
                TArg(z, (C_CHUNK, khb * 2 * HEAD), lambda kb, i: (tok(i), n_kblk + kb)),
                TArg(gates, gblk, lambda kb, i: (direction, kb, tok(i), 0, 0)),
                TArg(gates, gblk, lambda kb, i: (2 + direction, kb, tok(i), 0, 0)),
                TArg(alog, sblk, lambda kb, i: (direction, kb, 0, 0), "par", (1,)),
                TArg(dtb, sblk, lambda kb, i: (direction, kb, 0, 0), "par", (1,))]

    def odd_fwd(tag, xs, layer):
        h = norm_mod(f"nm1_{tag}", xs, layer, 0, 0, 1)
        pm = matmul(f"win_{tag}", h, w_od_main, "nn")
        pg = matmul(f"wgate_{tag}", h, w_od_gate, "nn")
        z = conv_fwd(f"conv_{tag}", pm, cv_full, 2 * kw + vw, nct)
        gates = pg.reshape(n_c, C_CHUNK, 4, n_kh, 2).transpose(2, 3, 0, 4, 1)
        outs, saves = [], []
        for direction in (0, 1):
            xrows = 2 * khb * C_CHUNK
            (o, xinv), sv = scan_fwd(
                f"gdn_{tag}_{direction}", f_gdn(direction == 1, khb, False), n_kblk, n_c, gd_args(z, gates, direction),
                [((t, vw), F32, (C_CHUNK, khb * 2 * HEAD), lambda kb, i, tok=c_tok(direction == 1): (tok(i), kb)),
                 ((n_kblk, n_c, xrows, C_CHUNK), F32, (None, None, xrows, C_CHUNK), lambda kb, i: (kb, i, 0, 0))],
                2 * khb)
            outs.append(o)
            saves.append((sv, xinv))
        n_vh = 2 * n_kh
        yo = readout(f"ro_{tag}", outs[0], outs[1], pm, 2 * n_kh + n_vh, od_norm, n_vh)
        y = matmul(f"wout_{tag}", yo, w_od_out, "nn")
        xn = gate_res(f"gr1_{tag}", xs, y, layer, 1, 2)
        return xn, (xs, h, pm, z, gates, outs, saves, yo, y)

    def odd_bwd(tag, saved, layer, dxn, acc):
        xs, h, pm, z, gates, outs, saves, yo, y = saved
        n_vh = 2 * n_kh
        dy, dnw1, dmod_a = gate_res_bwd(f"gr1b_{tag}", xs, y, layer, 1, 2, dxn)
        dyo = matmul(f"woutd_{tag}", dy, w_od_out, "nt")
        acc["od_w_out"] = row_grad(f"woutw_{tag}", yo, dy)
        do, dzg, d_onorm = readout_bwd(f"rob_{tag}", outs[0], outs[1], pm, 2 * n_kh + n_vh, od_norm, n_vh, dyo)
        dq, dk, dv, dga, dgb, dal, ddt = [], [], [], [], [], [], []
        for direction in (0, 1):
            sv, xinv = saves[direction]
            xarg = TArg(xinv, (None, None, 2 * khb * C_CHUNK, C_CHUNK), lambda kb, i: (kb, i, 0, 0), "const")
            r = scan_bwd(f"gdnb_{tag}_{direction}", f_gdn(direction == 1, khb, True), n_kblk, n_c,
                         gd_args(z, gates, direction) + [xarg], sv,
                         [(do, (C_CHUNK, khb * 2 * HEAD), lambda kb, i, tok=c_tok(direction == 1): (tok(i), kb))])
            for lst, v_ in zip((dq, dk, dv, dga, dgb, dal, ddt), r):
                lst.append(v_)
        dz = assemble(f"dz_{tag}", [[(dq[0], kw, 0), (dq[1], kw, 0)], [(dk[0], kw, 1), (dk[1], kw, 1)],
                                    [(dv[0], vw, 1), (dv[1], vw, 1)]], F32)
        du, d_conv = conv_bwd(f"convb_{tag}", pm, cv_full, dz, 2 * kw + vw, nct)
        dpm = assemble(f"dpm_{tag}", [[(du, 2 * kw + vw, 0)], [(dzg, vw, 2)]], BF16)
        dgates = jnp.stack([dga[0][0], dga[1][1], dgb[0][2], dgb[1][3]])
        dpg = dgates.transpose(2, 4, 0, 1, 3).reshape(t, n_gate).astype(BF16)
        dh = matmul(f"wgated_{tag}", dpg, w_od_gate, "nt")
        dh = matmul(f"wind_{tag}", dpm, w_od_main, "nt", add=dh)
        dw_in = jnp.concatenate([matmul(f"winw_{tag}", h, dpm, "tn", out_dtype=BF16),
                                 matmul(f"wgatew_{tag}", h, dpg, "tn", out_dtype=BF16)], axis=1)
        acc["od_w_in"] = dw_in.reshape(d, N_DEV, dw_in.shape[1] // N_DEV).transpose(1, 0, 2)
        dxs, dnw0, dmod_b = norm_mod_bwd(f"nm1b_{tag}", xs, layer, 0, 0, 1, dh, dxn)
        acc["norm_w"][layer][0] = dnw0
        acc["norm_w"][layer][1] = dnw1
        acc["mods"][layer].extend([dmod_a, dmod_b])
        acc["od_norm"] = d_onorm
        acc["od_conv"] = d_conv
        acc["od_A_log"] = jnp.stack([dal[0][0], dal[1][1]]).reshape(1, 2, n_vh)
        acc["od_dt_bias"] = jnp.stack([ddt[0][0], ddt[1][1]]).reshape(1, 2, n_vh)
        return dxs

    xs0 = jnp.concatenate([ctx[0], x[0]], axis=0)
    xs1, sv_e = even_fwd("l0", xs0, 0)
    g_w13a, g_w2a = gathered(ag_ffn0, xs1, "ag_wait_ffn0")
    w13[0], w2[0] = col_weight(g_w13a), rows_natural(g_w2a)
    xs2, sv_f0 = ffn_fwd("l0", xs1, 0)
    g_od_in, g_od_out, g_w13b, g_w2b = gathered(ag_l1, xs2, "ag_wait_l1")
    w_od_in = cols_natural(g_od_in)
    w_od_main, w_od_gate = w_od_in[:, :2 * kw + 2 * vw], w_od_in[:, 2 * kw + 2 * vw:]
    w_od_out = rows_natural(g_od_out)
    w13[1], w2[1] = col_weight(g_w13b), rows_natural(g_w2b)
    xs3, sv_o = odd_fwd("l1", xs2, 1)
    xs4, sv_f1 = ffn_fwd("l1", xs3, 1)
    loss_loc, dxs = loss_kernel("loss", xs4, loss_target[0], nct)
    loss = lax.psum(loss_loc, ("x", "y", "c"))

    acc = {"norm_w": [[None] * 4 for _ in range(depth)], "mods": [[] for _ in range(depth)],
           "ffn_w13": [None] * depth, "ffn_w2": [None] * depth}
    dxs = ffn_bwd("l1", sv_f1, 1, dxs, acc)
    rs_ffn1 = exchange_start("rs_start_ffn1", [acc["ffn_w13"][1], acc["ffn_w2"][1]], True, dxs)
    mods[1] = mods[1] + rs_ffn1[4][0, 0]
    dxs = odd_bwd("l1", sv_o, 1, dxs, acc)
    rs_od = exchange_start("rs_start_od", [acc["od_w_in"], acc["od_w_out"]], True, dxs)
    mods[0] = mods[0] + rs_od[4][0, 0]
    dxs = ffn_bwd("l0", sv_f0, 0, dxs, acc)
    rs_ffn0 = exchange_start("rs_start_ffn0", [acc["ffn_w13"][0], acc["ffn_w2"][0]], True, dxs)
    mods[0] = mods[0] + rs_ffn0[4][0, 0]
    dxs = even_bwd("l0", sv_e, 0, dxs, acc)
    rs_ev = acc["rs_ev"]
    grad_x = dxs[n_ctx:].reshape(1, seq, d)

    (d_lb_slots) = tile_bwd("lb_bwd", f_lb(0), (1,), lb_slots, [(acc["lb0"], (2, aw), lambda i: (0, 0))])
    d_ev_lb = jnp.stack(d_lb_slots, axis=1)

    dmods = jnp.stack([functools.reduce(jnp.add, acc["mods"][layer]) for layer in range(depth)])
    (dm_all,) = all_gather("ag_dmod", [dmods.reshape(depth * 2 * 6, d)])
    dm_all = dm_all.reshape(N_DEV, depth, 2, 6 * d)
    dm_cols = lax.dynamic_slice_in_dim(dm_all, me * ada_loc, ada_loc, axis=3)
    dm_loc = jnp.concatenate([dm_cols[:, :, 1].transpose(1, 0, 2), dm_cols[:, :, 0].transpose(1, 0, 2)], axis=1)
    d_cctx_part, d_w_ada, d_b_loc = tile_bwd("ada_bwd", f_ada, ada_grid, ada_args,
                                             [(dm_loc, (None, 2 * N_DEV, ada_cb), lambda l, j: (l, 0, j))])

    d_b_full = lax.dynamic_update_slice_in_dim(jnp.zeros_like(b_ada), d_b_loc.reshape(depth, ada_loc), me * ada_loc, axis=1)
    d_nw = jnp.stack([jnp.stack([acc["norm_w"][layer][s].reshape(d) for s in range(4)]) for layer in range(depth)])
    small_grads = [d_cctx_part.reshape(d), d_b_full, d_nw, d_ev_lb, acc["ev_a_norm"], acc["ev_pool_w"],
                   acc["ev_pool_scale"], acc["od_conv"], acc["od_A_log"], acc["od_dt_bias"], acc["od_norm"]]
    sg_shapes = [a.shape for a in small_grads]
    (sg,) = all_gather("ag_smallgrads", [_pack(small_grads, F32, FLAT_W, SUBLANES)])
    sg_sum = sum_leading("sum_smallgrads", sg, F32)
    (g_cctx, g_bada, g_nw, g_lb, g_anorm, g_pw, g_ps, g_conv, g_alog, g_dtb, g_onorm) = _unpack(sg_sum, sg_shapes)

    def my_cols(full, axis):
        loc = full.shape[axis] // N_DEV
        return lax.dynamic_slice_in_dim(full, me * loc, loc, axis=axis)

    grads = {
        "c_ctx": g_cctx, "w_ada": d_w_ada, "b_ada": g_bada, "norm_w": my_cols(g_nw, 2), "ev_lb": my_cols(g_lb, 2),
        "ev_a_norm": g_anorm, "ev_pool_w": my_cols(g_pw, 1)[None], "ev_pool_scale": g_ps,
        "od_conv": my_cols(g_conv, 1)[None], "od_A_log": g_alog, "od_dt_bias": g_dtb, "od_norm": g_onorm,
    }

    def reduced(started, tags_, name):
        srcs, lands = exchange_wait(name, started, sg_sum, True)
        out = []
        for tg, s_, l_ in zip(tags_, srcs, lands):
            own = lax.dynamic_index_in_dim(s_, me, 0, keepdims=True)
            out.append(sum_leading(f"rs_sum_{tg}", lax.dynamic_update_slice_in_dim(l_, own, me, 0), F32))
        return out

    g_w13b, g_w2b = reduced(rs_ffn1, ["w13b", "w2b"], "rs_wait_ffn1")
    g_od_in, g_od_out = reduced(rs_od, ["od_in", "od_out"], "rs_wait_od")
    g_w13a, g_w2a = reduced(rs_ffn0, ["w13a", "w2a"], "rs_wait_ffn0")
    g_ev_in, g_ev_out = reduced(rs_ev, ["ev_in", "ev_out"], "rs_wait_ev")
    grads["ev_w_in"], grads["ev_w_out"], grads["od_w_in"], grads["od_w_out"] = (g_ev_in[None], g_ev_out[None],
                                                                                g_od_in[None], g_od_out[None])
    grads["ffn_w13"] = jnp.stack([g_w13a, g_w13b])
    grads["ffn_w2"] = jnp.stack([g_w2a, g_w2b])

    big_names = ["w_ada", "ev_w_in", "ev_w_out", "od_w_in", "od_w_out", "ffn_w13", "ffn_w2"]
    small_names = [n_ for n_ in names if n_ not in big_names]
    gl = {n_: grads[n_].reshape(wts[n_].shape) for n_ in names}
    delta, new_m, new_v = {}, {}, {}
    for n_ in big_names:
        shp = wts[n_].shape
        res = adamw(f"adamw_{n_}", _rows2d(gl[n_]), _rows2d(wts[n_]), _rows2d(mom1[n_]), _rows2d(mom2[n_]))
        delta[n_], new_m[n_], new_v[n_] = (r_.reshape(shp) for r_ in res)
    shapes = [wts[n_].shape for n_ in small_names]
    pk = lambda dct: _pack([dct[n_] for n_ in small_names], F32, FLAT_W, SUBLANES)
    res = adamw("adamw_small", pk(gl), pk(wts), pk(mom1), pk(mom2))
    for dct, r_ in zip((delta, new_m, new_v), res):
        for n_, a_ in zip(small_names, _unpack(r_, shapes)):
            dct[n_] = a_
    return (loss, grad_x, *[gl[n_] for n_ in names], *[delta[n_] for n_ in names], *[new_m[n_] for n_ in names],
            *[new_v[n_] for n_ in names])
```

```python
import functools
from typing import Any, NamedTuple

import numpy as np

import jax
import jax.numpy as jnp
from jax import lax
from jax.experimental import pallas as pl
from jax.experimental.pallas import tpu as pltpu

F32 = jnp.float32
BF16 = jnp.bfloat16
MESH = pl.DeviceIdType.MESH
N_DEV = 8

EPS = 1e-6
GRID_W = 64
HEAD = 128
A_CHUNK = 32
C_CHUNK = 64
C_CONV = 4
POOL_WINDOWS = (2, 4, 8, 16)
ADAM_LR, ADAM_B1, ADAM_B2, ADAM_EPS, ADAM_WD, ADAM_STEP = 0.001, 0.9, 0.999, 1e-08, 0.01, 10

VMEM_LIMIT_BYTES = 56 * 1024 * 1024
LANES = 128
SUBLANES = 8
ROW_TILE = 256
FLAT_W = 1024
FLAT_ROWS = 512
TM_PREFS = (1056, 1024, 768, 512, 256, 128, 64, 32, 16)
TN_PREFS = (768, 512, 1408, 256, 128)
TK_PREFS = (2048, 2816, 1408, 1024, 768, 512, 384, 256, 128)
TO_PREFS = (1024, 1408, 768, 704, 512, 384, 256, 128)
WGRAD_TRANSPOSE_MAX_K = 2048
HEADS_PER_STEP = 8


def _pick(dim, prefs):
    for p in prefs:
        if p <= dim and dim % p == 0:
            return p
    return dim


def _cparams(ngrid):
    return pltpu.CompilerParams(dimension_semantics=("arbitrary",) * ngrid, vmem_limit_bytes=VMEM_LIMIT_BYTES)


def _sds(shape, dtype):
    return jax.ShapeDtypeStruct(tuple(shape), dtype)


def _split(x):
    hi = x.astype(BF16)
    return hi, (x - hi.astype(F32)).astype(BF16)


def _dot(a, b, ca, cb, hi):
    dims = (((ca,), (cb,)), ((), ()))
    dot = lambda u, v: lax.dot_general(u, v, dims, preferred_element_type=F32)
    if hi:
        (ah, al), (bh, bl) = _split(a.astype(F32)), _split(b.astype(F32))
        return dot(ah, bh) + (dot(ah, bl) + dot(al, bh))
    return dot(a.astype(BF16), b.astype(BF16))


@functools.partial(jax.custom_vjp, nondiff_argnums=(2, 3, 4))
def mm(a, b, ca=1, cb=0, hi=False):
    return _dot(a, b, ca, cb, hi)


def _mm_fwd(a, b, ca, cb, hi):
    return _dot(a, b, ca, cb, hi), (a, b)


def _mm_bwd(ca, cb, hi, res, g):
    a, b = res
    da = _dot(g, b, 1, 1 - cb, hi) if ca == 1 else _dot(b, g, 1 - cb, 1, hi)
    db = _dot(a, g, 1 - ca, 0, hi) if cb == 0 else _dot(g, a, 0, 1 - ca, hi)
    return da, db


mm.defvjp(_mm_fwd, _mm_bwd)


def _iota2(n, m, axis):
    return lax.broadcasted_iota(jnp.int32, (n, m), axis)


class TArg(NamedTuple):
    arr: Any
    block: tuple
    imap: Any
    kind: str = "row"
    acc: tuple = ()
    gdtype: Any = F32
    grad: bool = True


def _load(ref):
    v = ref[...]
    return v.astype(F32) if jnp.issubdtype(v.dtype, jnp.floating) else v


def tile_fwd(name, f, grid, args, outs):
    n_in, ng = len(args), len(grid)

    def body(*refs):
        pids = tuple(pl.program_id(k) for k in range(ng))
        res = f(pids, *[_load(r) for r in refs[:n_in]])
        for r, v in zip(refs[n_in:], res):
            r[...] = v.astype(r.dtype)

    return pl.pallas_call(
        body, grid=grid, name=name,
        in_specs=[pl.BlockSpec(a.block, a.imap) for a in args],
        out_specs=[pl.BlockSpec(b, im) for (_, _, b, im) in outs],
        out_shape=[_sds(s, d) for (s, d, _, _) in outs],
        compiler_params=_cparams(ng),
    )(*[a.arr for a in args])


def _store_grads(args, diff, pids, g_refs, d):
    for k, gr, dv in zip(diff, g_refs, d):
        a = args[k]
        if a.kind == "row" or not a.acc:
            gr[...] = dv.astype(gr.dtype)
        else:
            first = pids[a.acc[0]] == 0
            for ax in a.acc[1:]:
                first = jnp.logical_and(first, pids[ax] == 0)

            @pl.when(first)
            def _(gr=gr, dv=dv):
                gr[...] = dv.astype(gr.dtype)

            @pl.when(jnp.logical_not(first))
            def _(gr=gr, dv=dv):
                gr[...] += dv.astype(gr.dtype)


def tile_bwd(name, f, grid, args, cts):
    n_in, n_ct, ng = len(args), len(cts), len(grid)
    diff = [k for k, a in enumerate(args) if a.kind != "const" and a.grad]

    def body(*refs):
        pids = tuple(pl.program_id(k) for k in range(ng))
        vals = [_load(r) for r in refs[:n_in]]

        def g(*dv):
            full = list(vals)
            for k, v in zip(diff, dv):
                full[k] = v
            return tuple(f(pids, *full))

        _, vjp = jax.vjp(g, *[vals[k] for k in diff])
        d = vjp(tuple(_load(r) for r in refs[n_in:n_in + n_ct]))
        _store_grads(args, diff, pids, refs[n_in + n_ct:], d)

    return pl.pallas_call(
        body, grid=grid, name=name,
        in_specs=[pl.BlockSpec(a.block, a.imap) for a in args] + [pl.BlockSpec(b, im) for (_, b, im) in cts],
        out_specs=[pl.BlockSpec(args[k].block, args[k].imap) for k in diff],
        out_shape=[_sds(args[k].arr.shape, args[k].gdtype) for k in diff],
        compiler_params=_cparams(ng),
    )(*[a.arr for a in args], *[c[0] for c in cts])


def scan_fwd(name, f, n_heads, n_steps, args, outs, n_state):
    n_in, n_out = len(args), len(outs)
    sblock = (None, None, HEAD, HEAD)

    def body(*refs):
        in_refs = refs[:n_in]
        out_refs = refs[n_in:n_in + n_out]
        save_refs = refs[n_in + n_out:n_in + n_out + n_state]
        s_refs = refs[n_in + n_out + n_state:]

        @pl.when(pl.program_id(1) == 0)
        def _():
            for s in s_refs:
                s[...] = jnp.zeros_like(s)

        states = tuple(s[...] for s in s_refs)
        for sv, s in zip(save_refs, states):
            sv[...] = s
        new_states, res = f(states, *[_load(r) for r in in_refs])
        for s, v in zip(s_refs, new_states):
            s[...] = v
        for r, v in zip(out_refs, res):
            r[...] = v.astype(r.dtype)

    res = pl.pallas_call(
        body, grid=(n_heads, n_steps), name=name,
        in_specs=[pl.BlockSpec(a.block, a.imap) for a in args],
        out_specs=[pl.BlockSpec(b, im) for (_, _, b, im) in outs]
        + [pl.BlockSpec(sblock, lambda h, i: (h, i, 0, 0))] * n_state,
        out_shape=[_sds(s, d) for (s, d, _, _) in outs] + [_sds((n_heads, n_steps, HEAD, HEAD), F32)] * n_state,
        scratch_shapes=[pltpu.VMEM((HEAD, HEAD), F32)] * n_state,
        compiler_params=_cparams(2),
    )(*[a.arr for a in args])
    return res[:n_out], res[n_out:]


def scan_bwd(name, f, n_heads, n_steps, args, saves, cts):
    n_in, n_ct, n_state = len(args), len(cts), len(saves)
    diff = [k for k, a in enumerate(args) if a.kind != "const" and a.grad]
    sblock = (None, None, HEAD, HEAD)

    def rv(im):
        return lambda h, i: im(h, n_steps - 1 - i)

    def body(*refs):
        in_refs = refs[:n_in]
        save_refs = refs[n_in:n_in + n_state]
        ct_refs = refs[n_in + n_state:n_in + n_state + n_ct]
        g_refs = refs[n_in + n_state + n_ct:n_in + n_state + n_ct + len(diff)]
        ds_refs = refs[n_in + n_state + n_ct + len(diff):]
        pids = (pl.program_id(0), pl.program_id(1))

        @pl.when(pids[1] == 0)
        def _():
            for s in ds_refs:
                s[...] = jnp.zeros_like(s)

        vals = [_load(r) for r in in_refs]

        def g(states, *dv):
            full = list(vals)
            for k, v in zip(diff, dv):
                full[k] = v
            new_states, res = f(states, *full)
            return tuple(new_states), tuple(res)

        _, vjp = jax.vjp(g, tuple(s[...] for s in save_refs), *[vals[k] for k in diff])
        d = vjp((tuple(s[...] for s in ds_refs), tuple(_load(r) for r in ct_refs)))
        for s, v in zip(ds_refs, d[0]):
            s[...] = v
        _store_grads(args, diff, pids, g_refs, d[1:])

    return pl.pallas_call(
        body, grid=(n_heads, n_steps), name=name,
        in_specs=[pl.BlockSpec(a.block, rv(a.imap)) for a in args]
        + [pl.BlockSpec(sblock, rv(lambda h, i: (h, i, 0, 0)))] * n_state
        + [pl.BlockSpec(b, rv(im)) for (_, b, im) in cts],
        out_specs=[pl.BlockSpec(args[k].block, rv(args[k].imap)) for k in diff],
        out_shape=[_sds(args[k].arr.shape, args[k].gdtype) for k in diff],
        scratch_shapes=[pltpu.VMEM((HEAD, HEAD), F32)] * n_state,
        compiler_params=_cparams(2),
    )(*[a.arr for a in args], *saves, *[c[0] for c in cts])


def matmul(name, a, b, mode, add=None, out_dtype=F32, slabs=False, out_slabs=False):
    o_spec = None
    if mode == "nn":
        m, k = a.shape
        ns = b.shape[2] if slabs else (b.shape[1] // N_DEV if out_slabs else b.shape[1])
        n = N_DEV * ns if (slabs or out_slabs) else ns
        to_m, to_n, tr = _pick(m, TM_PREFS), _pick(ns, TN_PREFS), _pick(k, TK_PREFS)
        nb = ns // to_n
        grid = (m // to_m, n // to_n, k // tr)
        a_spec = pl.BlockSpec((to_m, tr), lambda i, j, l: (i, l))
        if slabs:
            b_spec = pl.BlockSpec((None, tr, to_n), lambda i, j, l: (j // nb, l, j % nb))
        else:
            b_spec = pl.BlockSpec((tr, to_n), lambda i, j, l: (l, j))
        dims, oshape = (1, 0), (m, n)
        if out_slabs:
            o_spec = pl.BlockSpec((None, to_m, to_n), lambda i, j, l: (j // nb, i, j % nb))
            oshape = (N_DEV, m, ns)
    elif mode == "nt":
        m, n = a.shape
        k = b.shape[1] if slabs else b.shape[0]
        ns = n // N_DEV if slabs else n
        to_m, to_n, tr = _pick(m, TM_PREFS), _pick(k, TO_PREFS), _pick(ns, TK_PREFS)
        nb = ns // tr
        grid = (m // to_m, k // to_n, n // tr)
        a_spec = pl.BlockSpec((to_m, tr), lambda i, j, l: (i, l))
        if slabs:
            b_spec = pl.BlockSpec((None, to_n, tr), lambda i, j, l: (l // nb, j, l % nb))
        else:
            b_spec = pl.BlockSpec((to_n, tr), lambda i, j, l: (j, l))
        dims, oshape = (1, 1), (m, k)
    else:
        (t, k), n = a.shape, b.shape[1]
        ns = n // N_DEV if slabs else n
        to_m, to_n, tr = _pick(k, TO_PREFS), _pick(ns, (2048,) + TO_PREFS), _pick(t, TM_PREFS)
        nb = ns // to_n
        grid = (k // to_m, n // to_n, t // tr)
        a_spec = pl.BlockSpec((tr, to_m), lambda i, j, l: (l, i))
        b_spec = pl.BlockSpec((tr, to_n), lambda i, j, l: (l, j))
        dims, oshape = (0, 0), (k, n)
        if slabs:
            o_spec = pl.BlockSpec((None, to_m, to_n), lambda i, j, l: (j // nb, i, j % nb))
            oshape = (N_DEV, k, ns)
    n_red = grid[2]
    if o_spec is None:
        o_spec = pl.BlockSpec((to_m, to_n), lambda i, j, l: (i, j))
    has_add = add is not None

    def body(a_ref, b_ref, *rest):
        add_ref = rest[0] if has_add else None
        o_ref = rest[1] if has_add else rest[0]
        part = lax.dot_general(a_ref[...].astype(BF16), b_ref[...].astype(BF16),
                               (((dims[0],), (dims[1],)), ((), ())), preferred_element_type=F32)

        def finish(v):
            if has_add:
                v = v + add_ref[...]
            o_ref[...] = v.astype(o_ref.dtype)

        if n_red == 1:
            finish(part)
        else:
            acc = rest[-1]
            step = pl.program_id(2)

            @pl.when(step == 0)
            def _():
                acc[...] = part

            @pl.when(step > 0)
            def _():
                acc[...] += part

            @pl.when(step == n_red - 1)
            def _():
                finish(acc[...])

    return pl.pallas_call(
        body, grid=grid, name=name,
        in_specs=[a_spec, b_spec] + ([o_spec] if has_add else []),
        out_specs=o_spec, out_shape=_sds(oshape, out_dtype),
        scratch_shapes=[pltpu.VMEM((to_m, to_n), F32)] if n_red > 1 else [],
        compiler_params=_cparams(3),
    )(a, b, *([add] if has_add else []))


def assemble(name, pieces, out_dtype):
    flat = [s for piece in pieces for s in piece]
    t = flat[0][0].shape[0]
    widths = [piece[0][1] for piece in pieces]

    def body(*refs):
        o_ref, k, off = refs[-1], 0, 0
        for piece, w in zip(pieces, widths):
            v = refs[k][...].astype(F32)
            k += 1
            for _ in piece[1:]:
                v = v + refs[k][...].astype(F32)
                k += 1
            o_ref[:, off:off + w] = v.astype(o_ref.dtype)
            off += w

    tr = ROW_TILE // 2
    return pl.pallas_call(
        body, grid=(t // tr,), name=name,
        in_specs=[pl.BlockSpec((tr, w), functools.partial(lambda i, cb: (i, cb), cb=cb)) for (_, w, cb) in flat],
        out_specs=pl.BlockSpec((tr, sum(widths)), lambda i: (i, 0)),
        out_shape=_sds((t, sum(widths)), out_dtype),
        compiler_params=_cparams(1),
    )(*[s[0] for s in flat])


ELEM_ROWS = (128, 64, 32, 16, 8)


def _rows2d(a, lead=0):
    return a.reshape(a.shape[:lead] + (-1, a.shape[-1]))


def sum_leading(name, arr, out_dtype):
    k, rows, w = arr.shape
    tr = _pick(rows, ELEM_ROWS)

    def body(a_ref, o_ref):
        v = a_ref[0].astype(F32)
        for j in range(1, k):
            v = v + a_ref[j].astype(F32)
        o_ref[...] = v.astype(o_ref.dtype)

    return pl.pallas_call(
        body, grid=(rows // tr,), name=name,
        in_specs=[pl.BlockSpec((k, tr, w), lambda i: (0, i, 0))],
        out_specs=pl.BlockSpec((tr, w), lambda i: (i, 0)),
        out_shape=_sds((rows, w), out_dtype), compiler_params=_cparams(1),
    )(arr)


def add_own(name, g8, got, core, out_dtype):
    _, rows, w = g8.shape
    tr = _pick(rows, ELEM_ROWS)

    def body(core_ref, a_ref, b_ref, o_ref):
        o_ref[...] = (a_ref[...] + b_ref[...]).astype(o_ref.dtype)

    spec = pl.BlockSpec((None, tr, w), lambda q, i, core_ref: (q, i, 0))
    return pl.pallas_call(
        body, name=name,
        grid_spec=pltpu.PrefetchScalarGridSpec(
            num_scalar_prefetch=1, grid=(4, rows // tr),
            in_specs=[pl.BlockSpec((None, tr, w), lambda q, i, core_ref: (2 * q + core_ref[0], i, 0)), spec],
            out_specs=spec),
        out_shape=_sds(got.shape, out_dtype), compiler_params=_cparams(2),
    )(core, g8, got)


def adamw(name, g, w, m, v):
    rows, wd = g.shape
    tr = _pick(rows, ELEM_ROWS)

    def body(g_ref, w_ref, m_ref, v_ref, d_ref, nm_ref, nv_ref):
        gv = g_ref[...]
        mn = ADAM_B1 * m_ref[...] + (1.0 - ADAM_B1) * gv
        vn = ADAM_B2 * v_ref[...] + (1.0 - ADAM_B2) * jnp.square(gv)
        m_hat = mn / (1.0 - ADAM_B1 ** ADAM_STEP)
        v_hat = vn / (1.0 - ADAM_B2 ** ADAM_STEP)
        d_ref[...] = -ADAM_LR * (m_hat / (jnp.sqrt(v_hat) + ADAM_EPS) + ADAM_WD * w_ref[...])
        nm_ref[...] = mn
        nv_ref[...] = vn

    spec = pl.BlockSpec((tr, wd), lambda i: (i, 0))
    return pl.pallas_call(
        body, grid=(rows // tr,), name=name, in_specs=[spec] * 4, out_specs=[spec] * 3,
        out_shape=[_sds(g.shape, F32)] * 3, compiler_params=_cparams(1),
    )(g, w, m, v)


def loss_kernel(name, xs, target, n_ctx_tiles):
    t, d = xs.shape
    nt = t // ROW_TILE

    def body(x_ref, t_ref, dx_ref, l_ref):
        i = pl.program_id(0)
        is_lat = i >= n_ctx_tiles
        err = jnp.where(is_lat, x_ref[...] - t_ref[...], 0.0)
        dx_ref[...] = err / d
        part = 0.5 * jnp.sum(jnp.mean(jnp.square(err), axis=-1, keepdims=True), axis=0, keepdims=True)

        @pl.when(i == 0)
        def _():
            l_ref[...] = jnp.zeros_like(l_ref)

        l_ref[...] += jnp.broadcast_to(part, l_ref.shape)

    dx, l = pl.pallas_call(
        body, grid=(nt,), name=name,
        in_specs=[pl.BlockSpec((ROW_TILE, d), lambda i: (i, 0)),
                  pl.BlockSpec((ROW_TILE, d), lambda i: (jnp.maximum(i - n_ctx_tiles, 0), 0))],
        out_specs=[pl.BlockSpec((ROW_TILE, d), lambda i: (i, 0)), pl.BlockSpec((SUBLANES, LANES), lambda i: (0, 0))],
        out_shape=[_sds((t, d), F32), _sds((SUBLANES, LANES), F32)], compiler_params=_cparams(1),
    )(xs, target)
    return l[0, 0], dx


CONV_COLS = 2048
CONV_LEFT = C_CONV // 2


def _conv_halo_specs(t, n_ctx_tiles):
    nt = t // ROW_TILE
    per = ROW_TILE // SUBLANES
    cur = pl.BlockSpec((ROW_TILE, CONV_COLS), lambda j, i: (i, j))
    prev = pl.BlockSpec((SUBLANES, CONV_COLS), lambda j, i: (jnp.maximum(i * per - 1, 0), j))
    nxt = pl.BlockSpec((SUBLANES, CONV_COLS), lambda j, i: (jnp.minimum((i + 1) * per, nt * per - 1), j))
    return cur, prev, nxt


def _fill_ext(ext, prev_ref, cur_ref, next_ref, i, nt, n_ctx_tiles):
    has_prev = jnp.logical_and(i != 0, i != n_ctx_tiles)
    has_next = jnp.logical_and(i != n_ctx_tiles - 1, i != nt - 1)
    ext[0:SUBLANES, :] = jnp.where(has_prev, prev_ref[...], 0.0)
    ext[SUBLANES:SUBLANES + ROW_TILE, :] = cur_ref[...]
    ext[SUBLANES + ROW_TILE:, :] = jnp.where(has_next, next_ref[...], 0.0)


def conv_fwd(name, p, w, width, n_ctx_tiles):
    t = p.shape[0]
    nt = t // ROW_TILE
    cur, prev, nxt = _conv_halo_specs(t, n_ctx_tiles)

    def body(c_ref, p_ref, n_ref, w_ref, o_ref, ext):
        _fill_ext(ext, p_ref, c_ref, n_ref, pl.program_id(1), nt, n_ctx_tiles)
        acc = None
        for j in range(C_CONV):
            term = ext[pl.ds(SUBLANES + j - CONV_LEFT, ROW_TILE), :] * w_ref[j:j + 1, :]
            acc = term if acc is None else acc + term
        o_ref[...] = acc

    return pl.pallas_call(
        body, grid=(width // CONV_COLS, nt), name=name,
        in_specs=[cur, prev, nxt, pl.BlockSpec((C_CONV, CONV_COLS), lambda j, i: (0, j))],
        out_specs=cur, out_shape=_sds((t, width), F32),
        scratch_shapes=[pltpu.VMEM((ROW_TILE + 2 * SUBLANES, CONV_COLS), F32)],
        compiler_params=_cparams(2),
    )(p, p, p, w)


def conv_bwd(name, p, w, dz, width, n_ctx_tiles):
    t = p.shape[0]
    nt = t // ROW_TILE
    cur, prev, nxt = _conv_halo_specs(t, n_ctx_tiles)

    def body(c_ref, p_ref, n_ref, dc_ref, dp_ref, dn_ref, w_ref, du_ref, dw_ref, ext, dext):
        i = pl.program_id(1)
        _fill_ext(ext, p_ref, c_ref, n_ref, i, nt, n_ctx_tiles)
        _fill_ext(dext, dp_ref, dc_ref, dn_ref, i, nt, n_ctx_tiles)
        dzc = dc_ref[...]
        @pl.when(i == 0)
        def _():
            dw_ref[...] = jnp.zeros_like(dw_ref)

        acc = None
        for j in range(C_CONV):
            term = dext[pl.ds(SUBLANES + CONV_LEFT - j, ROW_TILE), :] * w_ref[j:j + 1, :]
            acc = term if acc is None else acc + term
            dw_ref[j:j + 1, :] += jnp.sum(dzc * ext[pl.ds(SUBLANES + j - CONV_LEFT, ROW_TILE), :], axis=0, keepdims=True)
        du_ref[...] = acc

    wspec = pl.BlockSpec((C_CONV, CONV_COLS), lambda j, i: (0, j))
    return pl.pallas_call(
        body, grid=(width // CONV_COLS, nt), name=name,
        in_specs=[cur, prev, nxt, cur, prev, nxt, wspec],
        out_specs=[cur, wspec], out_shape=[_sds((t, width), F32), _sds((C_CONV, width), F32)],
        scratch_shapes=[pltpu.VMEM((ROW_TILE + 2 * SUBLANES, CONV_COLS), F32)] * 2,
        compiler_params=_cparams(2),
    )(p, p, p, dz, dz, dz, w)


def _rms(x, w):
    return x * lax.rsqrt(jnp.mean(x * x, axis=-1, keepdims=True) + EPS) * w


def _seg_mod(mods, is_ctx):
    return jnp.where(is_ctx, mods[0], mods[1])


def f_norm_mod(shift_i, scale_i, n_ctx_tiles, passthrough=False):
    def f(pids, x, nw, mods):
        m = _seg_mod(mods, pids[0] < n_ctx_tiles)
        h = _rms(x, nw) * (1.0 + m[scale_i:scale_i + 1]) + m[shift_i:shift_i + 1]
        return (h, x) if passthrough else (h,)
    return f


def f_gate_res(gate_i, n_ctx_tiles):
    def f(pids, x, y, nw, mods):
        m = _seg_mod(mods, pids[0] < n_ctx_tiles)
        return (x + m[gate_i:gate_i + 1] * _rms(y, nw),)
    return f


def f_swiglu(pids, gu):
    half = gu.shape[1] // 2
    return (jax.nn.silu(gu[:, :half]) * gu[:, half:],)


def f_readout(n_heads):
    def f(pids, o_a, o_b, gate, nw):
        cols = [slice(j * HEAD, (j + 1) * HEAD) for j in range(n_heads)]
        outs = _each(lambda cs: _rms(o_a[:, cs] + o_b[:, cs], nw) * jax.nn.silu(gate[:, cs]), cols)
        return (jnp.concatenate(outs, axis=1) if n_heads > 1 else outs[0],)
    return f


def f_pool(pids, u, pmat, pw, scale):
    d = mm(pmat, u, 1, 0, True) - u
    return (mm(d, pw) * scale,)


def f_lb(layer):
    def f(pids, *slots):
        top = slots[0]
        for s in slots[1:]:
            top = jnp.maximum(top, s)
        ex = [jnp.exp(s - top) for s in slots]
        tot = ex[0]
        for e in ex[1:]:
            tot = tot + e
        part = ex[0]
        for e in ex[1:layer + 1]:
            part = part + e
        return (part / tot,)
    return f


def f_ada(pids, c_all, c_ctx, w, b):
    c16 = jnp.concatenate([c_all, jnp.broadcast_to(c_ctx, c_all.shape)], axis=0)
    return (mm(jax.nn.silu(c16), w) + b,)


def _each(fn, *lists):
    return [fn(*xs) for xs in zip(*lists)]


def _hgrn2_heads(sts, qrs, frs, irs, lbs, rev):
    c = A_CHUNK
    mid = c - c // 2 if rev else c // 2 - 1
    ri, ci = _iota2(c, c, 0), _iota2(c, c, 1)
    incl = (ri <= ci) if rev else (ri >= ci)
    incl_f = incl.astype(F32)
    qs = _each(jax.nn.silu, qrs)
    log_fs = _each(lambda lb, fr: jnp.log(lb + (1.0 - lb) * jax.nn.sigmoid(fr)), lbs, frs)
    ks = _each(lambda lb, fr: (1.0 - lb) * jax.nn.sigmoid(-fr), lbs, frs)
    bs = _each(lambda lf: mm(incl_f, lf, 1, 0, True), log_fs)
    b_lasts = _each(lambda lf: jnp.sum(lf, axis=0, keepdims=True), log_fs)
    scores = _each(lambda q, k, b: mm(q * jnp.exp(b - b[mid:mid + 1]), k * jnp.exp(b[mid:mid + 1] - b), 1, 1), qs, ks, bs)
    intra = _each(lambda sc, ir: mm(jnp.where(incl, sc, 0.0), ir), scores, irs)
    inter = _each(lambda q, b, st: mm(q * jnp.exp(b), st, 1, 1), qs, bs, sts)
    upd = _each(lambda ir, k, bl, b: mm(ir, k * jnp.exp(bl - b), 0, 0), irs, ks, b_lasts, bs)
    new = _each(lambda st, bl, u: st * jnp.exp(bl) + u, sts, b_lasts, upd)
    return new, _each(jnp.add, intra, inter)


def f_hgrn2(rev, hb):
    def f(states, qr, fr, ir, lb):
        cols = [slice(j * HEAD, (j + 1) * HEAD) for j in range(hb)]
        new, outs = _hgrn2_heads(list(states), [qr[:, cs] for cs in cols], [fr[:, cs] for cs in cols],
                                 [ir[:, cs] for cs in cols], [lb[j] for j in range(hb)], rev)
        return tuple(new), (jnp.concatenate(outs, axis=1) if hb > 1 else outs[0],)
    return f


def _neumann_inv(a_lows):
    n = a_lows[0].shape[0]
    eye = (_iota2(n, n, 0) == _iota2(n, n, 1)).astype(F32)
    ps = _each(lambda a: -a, a_lows)
    xs = _each(lambda p: eye + p, ps)
    ps = _each(lambda p: mm(p, p, 1, 0, True), ps)
    k = 2
    while 2 * k < n:
        ys = _each(lambda p, x: mm(jnp.concatenate([p, x], axis=0), p, 1, 0, True), ps, xs)
        ps = _each(lambda y: y[:n], ys)
        xs = _each(lambda x, y: x + y[n:], xs, ys)
        k *= 2
    return tuple(_each(lambda x, p: x + mm(x, p, 1, 0, True), xs, ps))


@jax.custom_vjp
def unit_tri_inv(a_lows):
    return _neumann_inv(a_lows)


def _uti_fwd(a_lows):
    xs = _neumann_inv(a_lows)
    return xs, xs


def _uti_bwd(xs, gs):
    ts = _each(lambda x, g: mm(x, g, 0, 0, True), xs, gs)
    return (tuple(_each(lambda t, x: -mm(t, x, 1, 1, True), ts, xs)),)


unit_tri_inv.defvjp(_uti_fwd, _uti_bwd)


@jax.custom_vjp
def unit_tri_inv_saved(a_lows, xs):
    return xs


def _utis_fwd(a_lows, xs):
    return xs, xs


def _utis_bwd(xs, gs):
    return _uti_bwd(xs, gs) + (tuple(jnp.zeros_like(x) for x in xs),)


unit_tri_inv_saved.defvjp(_utis_fwd, _utis_bwd)


def _l2n(x):
    return x * lax.rsqrt(jnp.sum(x * x, axis=-1, keepdims=True) + EPS)


def _gdn_heads(ss, qs, ks, vs, a_rows, b_rows, alogs, dtbs, rev, xs_saved=None):
    c = qs[0].shape[0]
    ri, ci = _iota2(c, c, 0), _iota2(c, c, 1)
    causal = (ri <= ci) if rev else (ri >= ci)
    causal_t = (ri >= ci) if rev else (ri <= ci)
    strict = (ri < ci) if rev else (ri > ci)
    eye = ri == ci
    sq = lambda row: jnp.broadcast_to(row, (c, c))
    to_col = lambda row: jnp.sum(jnp.where(eye, sq(row), 0.0), axis=1, keepdims=True)
    g_rows = _each(lambda al, a, dt: -jnp.exp(al) * jax.nn.softplus(a + dt), alogs, a_rows, dtbs)
    beta_cols = _each(lambda b: to_col(jax.nn.sigmoid(b)), b_rows)
    g_cols = _each(to_col, g_rows)
    gc_cols = _each(lambda g: jnp.sum(jnp.where(causal, sq(g), 0.0), axis=1, keepdims=True), g_rows)
    gc_rows = _each(lambda g: jnp.sum(jnp.where(causal_t, sq(g), 0.0), axis=0, keepdims=True), g_cols)
    gc_lasts = _each(lambda g: jnp.sum(g, axis=1, keepdims=True), g_rows)
    decays = _each(lambda gc, gr: jnp.where(causal, jnp.exp(jnp.where(causal, gc - gr, 0.0)), 0.0), gc_cols, gc_rows)
    k_betas = _each(jnp.multiply, ks, beta_cols)
    v_betas = _each(jnp.multiply, vs, beta_cols)
    kq_ks = _each(lambda kb, q, k: mm(jnp.concatenate([kb, q], axis=0), k, 1, 1), k_betas, qs, ks)
    a_lows = _each(lambda kk, dec: jnp.where(strict, kk[:c] * dec, 0.0), kq_ks, decays)
    qks = _each(lambda kk, dec: jnp.where(causal, kk[c:] * dec, 0.0), kq_ks, decays)
    xs = unit_tri_inv(tuple(a_lows)) if xs_saved is None else unit_tri_inv_saved(tuple(a_lows), tuple(xs_saved))
    egcs = _each(jnp.exp, gc_cols)
    uws = _each(lambda x, vb, kb, e: mm(x, jnp.concatenate([vb, kb * e], axis=1), 1, 0, True), xs, v_betas, k_betas, egcs)
    dv = vs[0].shape[1]
    wq_ss = _each(lambda uw, q, e, s: mm(jnp.concatenate([uw[:, dv:], q * e], axis=0), s), uws, qs, egcs, ss)
    v_news = _each(lambda uw, wq: uw[:, :dv] - wq[:c], uws, wq_ss)
    o_states = _each(lambda wq: wq[c:], wq_ss)
    o_locals = _each(mm, qks, v_news)
    upds = _each(lambda k, gl, gc, vn: mm(k * jnp.exp(gl - gc), vn, 0, 0), ks, gc_lasts, gc_cols, v_news)
    new = _each(lambda s, gl, u: s * jnp.exp(gl) + u, ss, gc_lasts, upds)
    return new, _each(jnp.add, o_states, o_locals), xs


def f_gdn(rev, khb, saved_inverse):
    def f(states, qr, kr, vr, a3, b3, alog3, dtb3, xcat=None):
        heads = [(j, r) for j in range(khb) for r in range(2)]
        cols = [slice(j * HEAD, (j + 1) * HEAD) for j in range(khb)]
        c = qr.shape[0]
        qk_ = _each(lambda cs: (_l2n(jax.nn.silu(qr[:, cs])) * (HEAD ** -0.5), _l2n(jax.nn.silu(kr[:, cs]))), cols)
        vs = [jax.nn.silu(vr[:, (2 * j + r) * HEAD:(2 * j + r + 1) * HEAD]) for j, r in heads]
        row = lambda arr3: [arr3[j][r:r + 1] for j, r in heads]
        xs_saved = [xcat[n * c:(n + 1) * c] for n in range(len(heads))] if saved_inverse else None
        new, outs, xs = _gdn_heads(list(states), [qk_[j][0] for j, _ in heads], [qk_[j][1] for j, _ in heads], vs,
                                   row(a3), row(b3), row(alog3), row(dtb3), rev, xs_saved)
        o = jnp.concatenate(outs, axis=1)
        return tuple(new), ((o,) if saved_inverse else (o, jnp.concatenate(xs, axis=0)))
    return f


def _hbm_spec():
    return pl.BlockSpec(memory_space=pltpu.HBM)


def all_gather(name, xs):
    nt = len(xs)

    def body(*refs):
        x_refs, out_refs = refs[:nt], refs[nt:2 * nt]
        send_sems, recv_sems, local_sems = refs[2 * nt:]
        x, y, c = lax.axis_index("x"), lax.axis_index("y"), lax.axis_index("c")
        me, sibling = (x, y, c), (x, y, 1 - c)
        chips = [(1 - x, y), (x, 1 - y), (1 - x, 1 - y)]

        def slab(t, px, py, pc):
            return out_refs[t].at[4 * px + 2 * py + pc]

        def copy(t, k, block, to, src=None):
            return pltpu.make_async_remote_copy(
                src_ref=slab(t, *block) if src is None else src, dst_ref=slab(t, *block),
                send_sem=send_sems.at[7 * t + k], recv_sem=recv_sems.at[7 * t + k], device_id=to, device_id_type=MESH)

        mine, first, passed = [], [], []
        for t in range(nt):
            mine.append(pltpu.make_async_copy(x_refs[t], slab(t, *me), local_sems.at[t]))
            mine[-1].start()
            cps = [copy(t, 0, me, sibling, src=x_refs[t])]
            cps += [copy(t, 1 + j, me, (*chip, c), src=x_refs[t]) for j, chip in enumerate(chips)]
            for cp in cps:
                cp.start()
            first += cps
        for j, chip in enumerate(chips):
            for t in range(nt):
                copy(t, 1 + j, (*chip, c), me).wait_recv()
                fw = copy(t, 4 + j, (*chip, c), sibling)
                fw.start()
                passed.append(fw)
        for t in range(nt):
            copy(t, 0, sibling, me).wait_recv()
            for j, chip in enumerate(chips):
                copy(t, 4 + j, (*chip, 1 - c), me).wait_recv()
        for cp in first + passed:
            cp.wait_send()
        for cp in mine:
            cp.wait()

    return pl.pallas_call(
        body, name=name, out_shape=[_sds((N_DEV,) + a.shape, a.dtype) for a in xs],
        in_specs=[_hbm_spec()] * nt, out_specs=[_hbm_spec()] * nt,
        scratch_shapes=[pltpu.SemaphoreType.DMA((7 * nt,)), pltpu.SemaphoreType.DMA((7 * nt,)),
                        pltpu.SemaphoreType.DMA((nt,))],
    )(*xs)


def _peers(x, y, c):
    out = []
    for k in range(1, N_DEV):
        px = 1 - x if k & 4 else x
        py = 1 - y if k & 2 else y
        pc = 1 - c if k & 1 else c
        out.append((px, py, pc))
    return out


def _exchange_copies(src_refs, land_refs, send_sems, recv_sems, scatter):
    x, y, c = lax.axis_index("x"), lax.axis_index("y"), lax.axis_index("c")
    me = 4 * x + 2 * y + c
    sends, recvs = [], []
    for t, (src, land) in enumerate(zip(src_refs, land_refs)):
        for k, (px, py, pc) in enumerate(_peers(x, y, c)):
            peer = 4 * px + 2 * py + pc
            sem = dict(send_sem=send_sems.at[7 * t + k], recv_sem=recv_sems.at[7 * t + k],
                       device_id=(px, py, pc), device_id_type=MESH)
            src_k = src.at[peer] if scatter else src
            sends.append(pltpu.make_async_remote_copy(src_ref=src_k, dst_ref=land.at[me], **sem))
            recvs.append(pltpu.make_async_remote_copy(src_ref=src_k, dst_ref=land.at[peer], **sem))
    return sends, recvs


def exchange_start(name, srcs, scatter, after, carry=None):
    nt = len(srcs)
    lands = [lax.empty(s.shape if scatter else (N_DEV,) + s.shape, s.dtype) for s in srcs]
    thru = list(srcs) + lands + ([carry] if carry is not None else [])
    n_thru = len(thru)

    def body(*refs):
        src_refs, land_refs = refs[:nt], refs[nt:2 * nt]
        send_sems, recv_sems = refs[n_thru + 1], refs[n_thru + 2]
        token = refs[-1]
        sends, _ = _exchange_copies(src_refs, land_refs, send_sems, recv_sems, scatter)
        for cp in sends:
            cp.start()
        token[...] = jnp.zeros_like(token)

    res = pl.pallas_call(
        body, name=name,
        out_shape=(pltpu.SemaphoreType.DMA((7 * nt,)), pltpu.SemaphoreType.DMA((7 * nt,)),
                   *[pltpu.HBM(a.shape, a.dtype) for a in thru], _sds((SUBLANES, LANES), F32)),
        in_specs=[_hbm_spec()] * n_thru + [pl.BlockSpec(memory_space=pl.ANY)],
        out_specs=(pl.BlockSpec(memory_space=pltpu.SEMAPHORE), pl.BlockSpec(memory_space=pltpu.SEMAPHORE),
                   *[_hbm_spec()] * n_thru, pl.BlockSpec(memory_space=pltpu.VMEM)),
        input_output_aliases={i: 2 + i for i in range(n_thru)},
        compiler_params=pltpu.CompilerParams(has_side_effects=pltpu.SideEffectType.DATAFLOW_SIDE_EFFECTING),
    )(*[pltpu.with_memory_space_constraint(a, pltpu.HBM) for a in thru], after)
    return (res[0], res[1], list(res[2:2 + nt]), list(res[2 + nt:2 + 2 * nt]), res[-1],
            res[2 + 2 * nt] if carry is not None else None)


def exchange_wait(name, started, after, scatter):
    send_sems, recv_sems, srcs, lands = started[:4]
    nt = len(srcs)

    def body(*refs):
        src_refs, land_refs = refs[:nt], refs[nt:2 * nt]
        sends, recvs = _exchange_copies(src_refs, land_refs, refs[2 * nt], refs[2 * nt + 1], scatter)
        for cp in sends:
            cp.wait_send()
        for cp in recvs:
            cp.wait_recv()

    hbm = lambda a: pltpu.HBM(a.shape, a.dtype)
    sem = pl.BlockSpec(memory_space=pltpu.SEMAPHORE)
    res = pl.pallas_call(
        body, name=name,
        out_shape=(*[hbm(a) for a in srcs], *[hbm(a) for a in lands]),
        in_specs=[_hbm_spec()] * (2 * nt) + [sem, sem, pl.BlockSpec(memory_space=pl.ANY)],
        out_specs=tuple([_hbm_spec()] * (2 * nt)),
        input_output_aliases={i: i for i in range(2 * nt)},
        compiler_params=pltpu.CompilerParams(has_side_effects=pltpu.SideEffectType.DATAFLOW_SIDE_EFFECTING),
    )(*srcs, *lands, send_sems, recv_sems, after)
    return list(res[:nt]), list(res[nt:])


def sibling_exchange(name, gs):
    nt = len(gs)

    def body(*refs):
        g_refs, l_refs, send_sems, recv_sems = refs[:nt], refs[nt:2 * nt], refs[2 * nt], refs[2 * nt + 1]
        x, y, c = lax.axis_index("x"), lax.axis_index("y"), lax.axis_index("c")
        cps = []
        for t in range(nt):
            for q in range(4):
                cps.append(pltpu.make_async_remote_copy(
                    src_ref=g_refs[t].at[2 * q + (1 - c)], dst_ref=l_refs[t].at[q], send_sem=send_sems.at[4 * t + q],
                    recv_sem=recv_sems.at[4 * t + q], device_id=(x, y, 1 - c), device_id_type=MESH))
        for cp in cps:
            cp.start()
        for cp in cps:
            cp.wait()

    return pl.pallas_call(
        body, name=name, out_shape=[_sds((4,) + g.shape[1:], g.dtype) for g in gs],
        in_specs=[_hbm_spec()] * nt, out_specs=[_hbm_spec()] * nt,
        scratch_shapes=[pltpu.SemaphoreType.DMA((4 * nt,)), pltpu.SemaphoreType.DMA((4 * nt,))],
    )(*gs)


def chip_exchange(name, hs):
    nt = len(hs)

    def body(*refs):
        h_refs, r_refs = refs[:nt], refs[nt:2 * nt]
        send_sems, recv_sems, local_sems = refs[2 * nt:]
        x, y, c = lax.axis_index("x"), lax.axis_index("y"), lax.axis_index("c")
        my = 2 * x + y
        chips = [(1 - x, y), (x, 1 - y), (1 - x, 1 - y)]

        def copy(t, k, src_slot, dst_slot, to):
            return pltpu.make_async_remote_copy(
                src_ref=h_refs[t].at[src_slot], dst_ref=r_refs[t].at[dst_slot], send_sem=send_sems.at[3 * t + k],
                recv_sem=recv_sems.at[3 * t + k], device_id=(*to, c), device_id_type=MESH)

        mine, sends = [], []
        for t in range(nt):
            mine.append(pltpu.make_async_copy(h_refs[t].at[my], r_refs[t].at[my], local_sems.at[t]))
            mine[-1].start()
            for k, (qx, qy) in enumerate(chips):
                sends.append(copy(t, k, 2 * qx + qy, my, (qx, qy)))
                sends[-1].start()
        for t in range(nt):
            for k, (qx, qy) in enumerate(chips):
                copy(t, k, my, 2 * qx + qy, (qx, qy)).wait_recv()
        for cp in sends:
            cp.wait_send()
        for cp in mine:
            cp.wait()

    return pl.pallas_call(
        body, name=name, out_shape=[_sds(h.shape, h.dtype) for h in hs],
        in_specs=[_hbm_spec()] * nt, out_specs=[_hbm_spec()] * nt,
        scratch_shapes=[pltpu.SemaphoreType.DMA((3 * nt,)), pltpu.SemaphoreType.DMA((3 * nt,)),
                        pltpu.SemaphoreType.DMA((nt,))],
    )(*hs)


def _pack(arrs, dtype, width, row_mult, lead=0):
    ld = arrs[0].shape[:lead]
    flat = jnp.concatenate([a.reshape(ld + (-1,)).astype(dtype) for a in arrs], axis=-1)
    n = flat.shape[-1]
    q = width * row_mult
    npad = -(-n // q) * q
    flat = jnp.pad(flat, [(0, 0)] * lead + [(0, npad - n)])
    return flat.reshape(ld + (npad // width, width))


def _unpack(buf, shapes, lead=0):
    ld = buf.shape[:lead]
    flat = buf.reshape(ld + (-1,))
    out, off = [], 0
    for s in shapes:
        n = int(np.prod(s))
        out.append(flat[..., off:off + n].reshape(ld + tuple(s)))
        off += n
    return out


def _pool_mats(seg_len):
    mats = np.zeros((len(POOL_WINDOWS), ROW_TILE, ROW_TILE), np.float32)
    for gi, win in enumerate(POOL_WINDOWS):
        for p in range(ROW_TILE):
            base = (p // seg_len) * seg_len
            q = p - base
            lo = min(max(q - win // 2, 0), seg_len - 1)
            hi = min(max(q + win - 1 - win // 2, 0), seg_len - 1)
            mats[gi, p, base + lo:base + hi + 1] = 1.0 / (hi - lo + 1)
    return mats


def kernel(x, c, ctx, c_ctx, w_ada, b_ada, norm_w, ev_w_in, ev_lb, ev_a_norm, ev_pool_w, ev_pool_scale, ev_w_out, od_w_in, od_conv, od_A_log, od_dt_bias, od_norm, od_w_out, ffn_w13, ffn_w2, loss_target, m_c_ctx, m_w_ada, m_b_ada, m_norm_w, m_ev_w_in, m_ev_lb, m_ev_a_norm, m_ev_pool_w, m_ev_pool_scale, m_ev_w_out, m_od_w_in, m_od_conv, m_od_A_log, m_od_dt_bias, m_od_norm, m_od_w_out, m_ffn_w13, m_ffn_w2, v_c_ctx, v_w_ada, v_b_ada, v_norm_w, v_ev_w_in, v_ev_lb, v_ev_a_norm, v_ev_pool_w, v_ev_pool_scale, v_ev_w_out, v_od_w_in, v_od_conv, v_od_A_log, v_od_dt_bias, v_od_norm, v_od_w_out, v_ffn_w13, v_ffn_w2):
    names = ["c_ctx", "w_ada", "b_ada", "norm_w", "ev_w_in", "ev_lb", "ev_a_norm", "ev_pool_w", "ev_pool_scale",
             "ev_w_out", "od_w_in", "od_conv", "od_A_log", "od_dt_bias", "od_norm", "od_w_out", "ffn_w13", "ffn_w2"]
    wts = dict(zip(names, [c_ctx, w_ada, b_ada, norm_w, ev_w_in, ev_lb, ev_a_norm, ev_pool_w, ev_pool_scale,
                           ev_w_out, od_w_in, od_conv, od_A_log, od_dt_bias, od_norm, od_w_out, ffn_w13, ffn_w2]))
    mom1 = dict(zip(names, [m_c_ctx, m_w_ada, m_b_ada, m_norm_w, m_ev_w_in, m_ev_lb, m_ev_a_norm, m_ev_pool_w,
                            m_ev_pool_scale, m_ev_w_out, m_od_w_in, m_od_conv, m_od_A_log, m_od_dt_bias, m_od_norm,
                            m_od_w_out, m_ffn_w13, m_ffn_w2]))
    mom2 = dict(zip(names, [v_c_ctx, v_w_ada, v_b_ada, v_norm_w, v_ev_w_in, v_ev_lb, v_ev_a_norm, v_ev_pool_w,
                            v_ev_pool_scale, v_ev_w_out, v_od_w_in, v_od_conv, v_od_A_log, v_od_dt_bias, v_od_norm,
                            v_od_w_out, v_ffn_w13, v_ffn_w2]))

    ax, ay, ac = lax.axis_index("x"), lax.axis_index("y"), lax.axis_index("c")
    me = 4 * ax + 2 * ay + ac
    my_chip = 2 * ax + ay

    seq, d = x.shape[1], x.shape[2]
    n_ctx = ctx.shape[1]
    t = n_ctx + seq
    nt = t // ROW_TILE
    nct = n_ctx // ROW_TILE
    assert n_ctx == ROW_TILE and seq % ROW_TILE == 0 and ROW_TILE % GRID_W == 0
    depth = w_ada.shape[0]
    aw = d // 2
    n_ah = aw // HEAD
    n_grp = len(POOL_WINDOWS)
    dg = aw // n_grp
    assert dg % LANES == 0
    n_kh = d // HEAD
    kw, vw = n_kh * HEAD, 2 * n_kh * HEAD
    n_gate = 8 * n_kh
    ffn_h = ffn_w2.shape[1] * N_DEV
    ada_loc = w_ada.shape[2]
    assert depth == 2

    bf = lambda w_: w_.astype(BF16)

    def gathered(started, after, name):
        srcs, lands = exchange_wait(name, started, after, False)
        return [lax.dynamic_update_index_in_dim(l_, s_, me, 0) for l_, s_ in zip(lands, srcs)]

    small_shapes = [(d,), norm_w.shape, ev_lb.shape, ev_pool_w.shape[1:], od_conv.shape[1:]]
    (g1,) = all_gather("ag_small", [_pack([c[0], norm_w, ev_lb, ev_pool_w[0], od_conv[0]], F32, LANES, SUBLANES)])
    c_all, nw_g, lb_g, pw_g, cv_g = _unpack(g1, small_shapes, lead=1)
    nw_full = nw_g.transpose(1, 2, 0, 3).reshape(depth, 4, d)
    lb_full = lb_g.transpose(1, 2, 0, 3).reshape(2, depth + 1, aw)
    pw_full = pw_g.transpose(1, 0, 2, 3).reshape(n_grp, dg, dg)
    cv_full = cv_g.transpose(1, 0, 2).reshape(C_CONV, 2 * kw + vw)

    g_ev_in, g_ev_out = all_gather("ag_weights_ev", [bf(ev_w_in[0]), bf(ev_w_out[0])])

    def cols_natural(g):
        return g.transpose(1, 0, 2).reshape(g.shape[1], N_DEV * g.shape[2])

    def rows_natural(g):
        return g.reshape(N_DEV * g.shape[1], g.shape[2])

    def col_weight(g):
        return (g, True) if g.shape[2] % LANES == 0 else (cols_natural(g), False)

    w_ev_in = col_weight(g_ev_in)
    w_ev_out = rows_natural(g_ev_out)
    w13, w2 = [None] * depth, [None] * depth

    b_loc = lax.dynamic_slice_in_dim(b_ada, me * ada_loc, ada_loc, axis=1).reshape(depth, 1, ada_loc)
    ada_cb = _pick(ada_loc, TN_PREFS)
    ada_grid = (depth, ada_loc // ada_cb)
    ada_args = [
        TArg(c_all, (N_DEV, d), lambda l, j: (0, 0), "const"),
        TArg(c_ctx.reshape(1, d), (1, d), lambda l, j: (0, 0), "par", (0, 1)),
        TArg(w_ada, (None, d, ada_cb), lambda l, j: (l, 0, j)),
        TArg(b_loc, (None, 1, ada_cb), lambda l, j: (l, 0, j)),
    ]
    (m_loc,) = tile_fwd("ada_fwd", f_ada, ada_grid, ada_args,
                        [((depth, 2 * N_DEV, ada_loc), F32, (None, 2 * N_DEV, ada_cb), lambda l, j: (l, 0, j))])
    (m_all,) = all_gather("ag_mod", [m_loc])
    mods = []
    for layer in range(depth):
        lat = lax.dynamic_index_in_dim(m_all[:, layer], me, axis=1, keepdims=False).reshape(6, d)
        cxt = lax.dynamic_index_in_dim(m_all[:, layer], N_DEV + me, axis=1, keepdims=False).reshape(6, d)
        mods.append(jnp.stack([cxt, lat]))

    gathers_done = mods[0][0, :1, :SUBLANES] + g_ev_out[0, :1, :SUBLANES].astype(F32)
    ag_ffn0 = exchange_start("ag_start_ffn0", [bf(ffn_w13[0]), bf(ffn_w2[0])], False, gathers_done)
    ag_l1 = exchange_start("ag_start_l1", [bf(od_w_in[0]), bf(od_w_out[0]), bf(ffn_w13[1]), bf(ffn_w2[1])], False,
                           gathers_done)
    mods[0] = mods[0] + (ag_ffn0[4][0, 0] + ag_l1[4][0, 0])

    lb_slots = [TArg(lb_full[:, j], (2, aw), lambda i: (0, 0)) for j in range(depth + 1)]
    (lb0,) = tile_fwd("lb_fwd", f_lb(0), (1,), lb_slots, [((2, aw), F32, (2, aw), lambda i: (0, 0))])
    lb0r = lb0.reshape(2, n_ah, 1, HEAD)

    full_row = lambda i: (i, 0)
    par0 = lambda i: (0, 0)

    def nm_args(xs, layer, slot):
        return [TArg(xs, (ROW_TILE, d), full_row),
                TArg(nw_full[layer, slot].reshape(1, d), (1, d), par0, "par", (0,)),
                TArg(mods[layer], (2, 6, d), lambda i: (0, 0, 0), "par", (0,))]

    def norm_mod(name, xs, layer, slot, si, ci):
        (h,) = tile_fwd(name, f_norm_mod(si, ci, nct), (nt,), nm_args(xs, layer, slot),
                        [((t, d), BF16, (ROW_TILE, d), full_row)])
        return h

    def norm_mod_bwd(name, xs, layer, slot, si, ci, dh, carry):
        return tile_bwd(name, f_norm_mod(si, ci, nct, True), (nt,), nm_args(xs, layer, slot),
                        [(dh, (ROW_TILE, d), full_row), (carry, (ROW_TILE, d), full_row)])

    def gr_args(xs, ys, layer, slot):
        return [TArg(xs, (ROW_TILE, d), full_row, grad=False), TArg(ys, (ROW_TILE, d), full_row, gdtype=BF16),
                TArg(nw_full[layer, slot].reshape(1, d), (1, d), par0, "par", (0,)),
                TArg(mods[layer], (2, 6, d), lambda i: (0, 0, 0), "par", (0,))]

    def gate_res(name, xs, ys, layer, slot, gi):
        (o,) = tile_fwd(name, f_gate_res(gi, nct), (nt,), gr_args(xs, ys, layer, slot),
                        [((t, d), F32, (ROW_TILE, d), full_row)])
        return o

    def gate_res_bwd(name, xs, ys, layer, slot, gi, dx):
        return tile_bwd(name, f_gate_res(gi, nct), (nt,), gr_args(xs, ys, layer, slot),
                        [(dx, (ROW_TILE, d), full_row)])

    sw_rows = ROW_TILE // 2

    def sw_args(gu):
        return [TArg(gu, (sw_rows, 2 * ffn_h), full_row, gdtype=BF16)]

    def ffn_fwd(tag, xs, layer):
        h2 = norm_mod(f"nm2_{tag}", xs, layer, 2, 3, 4)
        gu = matmul(f"w13_{tag}", h2, w13[layer][0], "nn", slabs=w13[layer][1], out_dtype=BF16)
        (act,) = tile_fwd(f"swiglu_{tag}", f_swiglu, (t // sw_rows,), sw_args(gu),
                          [((t, ffn_h), BF16, (sw_rows, ffn_h), full_row)])
        fo = matmul(f"w2_{tag}", act, w2[layer], "nn")
        xn = gate_res(f"gr2_{tag}", xs, fo, layer, 3, 5)
        return xn, (xs, h2, gu, act, fo)

    def wgrad(name, a, dy, slabs=False):
        if a.shape[1] <= WGRAD_TRANSPOSE_MAX_K:
            return matmul(name, a.T, dy, "nn", out_slabs=slabs, out_dtype=BF16)
        return matmul(name, a, dy, "tn", slabs=slabs, out_dtype=BF16)

    def col_grad(name, a, dy, slabs):
        g = wgrad(name, a, dy, slabs)
        return g if slabs else g.reshape(g.shape[0], N_DEV, g.shape[1] // N_DEV).transpose(1, 0, 2)

    def row_grad(name, a, dy):
        g = matmul(name, a, dy, "tn", out_dtype=BF16)
        return g.reshape(N_DEV, g.shape[0] // N_DEV, g.shape[1])

    def ffn_bwd(tag, saved, layer, dxn, acc):
        xs, h2, gu, act, fo = saved
        dfo, dnw3, dmod_a = gate_res_bwd(f"gr2b_{tag}", xs, fo, layer, 3, 5, dxn)
        dact = matmul(f"w2d_{tag}", dfo, w2[layer], "nt")
        dw2 = row_grad(f"w2w_{tag}", act, dfo)
        (dgu,) = tile_bwd(f"swiglub_{tag}", f_swiglu, (t // sw_rows,), sw_args(gu), [(dact, (sw_rows, ffn_h), full_row)])
        dh2 = matmul(f"w13d_{tag}", dgu, w13[layer][0], "nt", slabs=w13[layer][1])
        dw13 = col_grad(f"w13w_{tag}", h2, dgu, w13[layer][1])
        dxs, dnw2, dmod_b = norm_mod_bwd(f"nm2b_{tag}", xs, layer, 2, 3, 4, dh2, dxn)
        acc["ffn_w13"][layer] = dw13
        acc["ffn_w2"][layer] = dw2
        acc["norm_w"][layer][2] = dnw2
        acc["norm_w"][layer][3] = dnw3
        acc["mods"][layer].extend([dmod_a, dmod_b])
        return dxs

    def head_cols(width):
        return (ROW_TILE, width)

    n_a = t // A_CHUNK
    nca = n_ctx // A_CHUNK

    def a_tok(rev):
        if not rev:
            return lambda i: i
        return lambda i: jnp.where(i < nca, nca - 1 - i, n_a + nca - 1 - i)

    hb = _pick(n_ah, (HEADS_PER_STEP, 2, 1))
    n_hblk = n_ah // hb

    def hg_args(p, direction):
        tok = a_tok(direction == 1)
        blk = (A_CHUNK, hb * HEAD)
        return [TArg(p, blk, lambda h, i: (tok(i), h)),
                TArg(p, blk, lambda h, i: (tok(i), (1 + direction) * n_hblk + h)),
                TArg(p, blk, lambda h, i: (tok(i), 3 * n_hblk + h)),
                TArg(lb0r, (None, hb, 1, HEAD), lambda h, i: (direction, h, 0, 0), "par", (1,))]

    pmats = jnp.asarray(np.stack([_pool_mats(n_ctx), _pool_mats(GRID_W)]))

    def pool_args(p):
        return [TArg(p, (ROW_TILE, dg), lambda g, i: (i, 5 * n_grp + g)),
                TArg(pmats, (None, None, ROW_TILE, ROW_TILE), lambda g, i: (jnp.where(i < nct, 0, 1), g, 0, 0), "const"),
                TArg(pw_full, (None, dg, dg), lambda g, i: (g, 0, 0), "par", (1,)),
                TArg(ev_pool_scale, (1, dg), lambda g, i: (0, g), "par", (1,))]

    def ro_plan(gate_off, n_heads):
        per = _pick(n_heads, (8, 4, 2, 1))
        assert gate_off % per == 0
        return per, n_heads // per, gate_off // per

    def ro_args(o_f, o_b, gate_arr, gate_off, nw_arr, n_heads):
        per, _, goff = ro_plan(gate_off, n_heads)
        blk = (ROW_TILE, per * HEAD)
        return [TArg(o_f, blk, lambda h, i: (i, h)), TArg(o_b, blk, lambda h, i: (i, h), grad=False),
                TArg(gate_arr, blk, lambda h, i: (i, goff + h)),
                TArg(nw_arr, (1, HEAD), lambda h, i: (0, 0), "par", (0, 1))]

    def readout(name, o_f, o_b, gate_arr, gate_off, nw_arr, n_heads):
        per, nblk, _ = ro_plan(gate_off, n_heads)
        (o,) = tile_fwd(name, f_readout(per), (nblk, nt), ro_args(o_f, o_b, gate_arr, gate_off, nw_arr, n_heads),
                        [((t, n_heads * HEAD), BF16, (ROW_TILE, per * HEAD), lambda hh, i: (i, hh))])
        return o

    def readout_bwd(name, o_f, o_b, gate_arr, gate_off, nw_arr, n_heads, dout):
        per, nblk, _ = ro_plan(gate_off, n_heads)
        return tile_bwd(name, f_readout(per), (nblk, nt), ro_args(o_f, o_b, gate_arr, gate_off, nw_arr, n_heads),
                        [(dout, (ROW_TILE, per * HEAD), lambda hh, i: (i, hh))])

    def even_fwd(tag, xs, layer):
        h = norm_mod(f"nm1_{tag}", xs, layer, 0, 0, 1)
        p = matmul(f"win_{tag}", h, w_ev_in[0], "nn", slabs=w_ev_in[1])
        outs, saves = [], []
        for direction in (0, 1):
            (o,), sv = scan_fwd(f"hgrn_{tag}_{direction}", f_hgrn2(direction == 1, hb), n_hblk, n_a, hg_args(p, direction),
                                [((t, aw), F32, (A_CHUNK, hb * HEAD), lambda hh, i, tok=a_tok(direction == 1): (tok(i), hh))], hb)
            outs.append(o)
            saves.append(sv)
        a_out = readout(f"ro_{tag}", outs[0], outs[1], p, 4 * n_ah, ev_a_norm, n_ah)
        (pooled,) = tile_fwd(f"pool_{tag}", f_pool, (n_grp, nt), pool_args(p),
                             [((t, aw), BF16, (ROW_TILE, dg), lambda g, i: (i, g))])
        cat = assemble(f"cat_{tag}", [[(a_out, aw, 0)], [(pooled, aw, 0)]], BF16)
        y = matmul(f"wout_{tag}", cat, w_ev_out, "nn")
        xn = gate_res(f"gr1_{tag}", xs, y, layer, 1, 2)
        return xn, (xs, h, p, outs, saves, cat, y)

    def even_bwd(tag, saved, layer, dxn, acc):
        xs, h, p, outs, saves, cat, y = saved
        dy, dnw1, dmod_a = gate_res_bwd(f"gr1b_{tag}", xs, y, layer, 1, 2, dxn)
        dcat = matmul(f"woutd_{tag}", dy, w_ev_out, "nt")
        acc["ev_w_out"] = row_grad(f"woutw_{tag}", cat, dy)
        do, dgate, d_anorm = readout_bwd(f"rob_{tag}", outs[0], outs[1], p, 4 * n_ah, ev_a_norm, n_ah, dcat)
        du, d_pw, d_ps = tile_bwd(f"poolb_{tag}", f_pool, (n_grp, nt), pool_args(p),
                                  [(dcat, (ROW_TILE, dg), lambda g, i: (i, n_grp + g))])
        dq, df, di, dlb = [], [], [], []
        for direction in (0, 1):
            r = scan_bwd(f"hgrnb_{tag}_{direction}", f_hgrn2(direction == 1, hb), n_hblk, n_a, hg_args(p, direction),
                         saves[direction],
                         [(do, (A_CHUNK, hb * HEAD), lambda hh, i, tok=a_tok(direction == 1): (tok(i), hh))])
            dq.append(r[0])
            df.append(r[1])
            di.append(r[2])
            dlb.append(r[3])
        sec = lambda arr, s: (arr, aw, s)
        dp = assemble(f"dp_{tag}", [[sec(dq[0], 0), sec(dq[1], 0)], [sec(df[0], 1)], [sec(df[1], 2)],
                                    [sec(di[0], 3), sec(di[1], 3)], [sec(dgate, 4)], [sec(du, 5)]], BF16)
        acc["ev_w_in"] = col_grad(f"winw_{tag}", h, dp, w_ev_in[1])
        acc["rs_ev"] = exchange_start("rs_start_ev", [acc["ev_w_in"], acc["ev_w_out"]], True, dp, carry=w_ev_in[0])
        dh = matmul(f"wind_{tag}", dp, acc["rs_ev"][5], "nt", slabs=w_ev_in[1])
        dxs, dnw0, dmod_b = norm_mod_bwd(f"nm1b_{tag}", xs, layer, 0, 0, 1, dh, dxn)
        acc["norm_w"][layer][0] = dnw0
        acc["norm_w"][layer][1] = dnw1
        acc["mods"][layer].extend([dmod_a, dmod_b])
        acc["ev_a_norm"] = d_anorm
        acc["ev_pool_w"] = d_pw
        acc["ev_pool_scale"] = d_ps
        acc["lb0"] = jnp.stack([dlb[0][0], dlb[1][1]]).reshape(2, aw)
        return dxs

    n_c = t // C_CHUNK
    ncc = n_ctx // C_CHUNK

    def c_tok(rev):
        if not rev:
            return lambda i: i
        return lambda i: jnp.where(i < ncc, ncc - 1 - i, n_c + ncc - 1 - i)

    alog = od_A_log[0].reshape(2, n_kh, 2, 1)
    dtb = od_dt_bias[0].reshape(2, n_kh, 2, 1)

    khb = _pick(n_kh, (HEADS_PER_STEP, 2, 1))
    n_kblk = n_kh // khb

    def gd_args(z, gates, direction):
        tok = c_tok(direction == 1)
        gblk = (None, khb, None, 2, C_CHUNK)
        sblk = (None, khb, 2, 1)
        return [TArg(z, (C_CHUNK, khb * HEAD), lambda kb, i: (tok(i), kb)),
                TArg(z, (C_CHUNK, khb * HEAD), lambda kb, i: (tok(i), n_kblk + kb)),
                TArg(z, (C_CHUNK, khb * 2 * HEAD), lambda kb, i: (tok(i), n_kblk + kb)),
                TArg(gates, gblk, lambda kb, i: (direction, kb, tok(i), 0, 0)),
                TArg(gates, gblk, lambda kb, i: (2 + direction, kb, tok(i), 0, 0)),
                TArg(alog, sblk, lambda kb, i: (direction, kb, 0, 0), "par", (1,)),
                TArg(dtb, sblk, lambda kb, i: (direction, kb, 0, 0), "par", (1,))]

    def odd_fwd(tag, xs, layer):
        h = norm_mod(f"nm1_{tag}", xs, layer, 0, 0, 1)
        pm = matmul(f"win_{tag}", h, w_od_main, "nn")
        pg = matmul(f"wgate_{tag}", h, w_od_gate, "nn")
        z = conv_fwd(f"conv_{tag}", pm, cv_full, 2 * kw + vw, nct)
        gates = pg.reshape(n_c, C_CHUNK, 4, n_kh, 2).transpose(2, 3, 0, 4, 1)
        outs, saves = [], []
        for direction in (0, 1):
            xrows = 2 * khb * C_CHUNK
            (o, xinv), sv = scan_fwd(
                f"gdn_{tag}_{direction}", f_gdn(direction == 1, khb, False), n_kblk, n_c, gd_args(z, gates, direction),
                [((t, vw), F32, (C_CHUNK, khb * 2 * HEAD), lambda kb, i, tok=c_tok(direction == 1): (tok(i), kb)),
                 ((n_kblk, n_c, xrows, C_CHUNK), F32, (None, None, xrows, C_CHUNK), lambda kb, i: (kb, i, 0, 0))],
                2 * khb)
            outs.append(o)
            saves.append((sv, xinv))
        n_vh = 2 * n_kh
        yo = readout(f"ro_{tag}", outs[0], outs[1], pm, 2 * n_kh + n_vh, od_norm, n_vh)
        y = matmul(f"wout_{tag}", yo, w_od_out, "nn")
        xn = gate_res(f"gr1_{tag}", xs, y, layer, 1, 2)
        return xn, (xs, h, pm, z, gates, outs, saves, yo, y)

    def odd_bwd(tag, saved, layer, dxn, acc):
        xs, h, pm, z, gates, outs, saves, yo, y = saved
        n_vh = 2 * n_kh
        dy, dnw1, dmod_a = gate_res_bwd(f"gr1b_{tag}", xs, y, layer, 1, 2, dxn)
        dyo = matmul(f"woutd_{tag}", dy, w_od_out, "nt")
        acc["od_w_out"] = row_grad(f"woutw_{tag}", yo, dy)
        do, dzg, d_onorm = readout_bwd(f"rob_{tag}", outs[0], outs[1], pm, 2 * n_kh + n_vh, od_norm, n_vh, dyo)
        dq, dk, dv, dga, dgb, dal, ddt = [], [], [], [], [], [], []
        for direction in (0, 1):
            sv, xinv = saves[direction]
            xarg = TArg(xinv, (None, None, 2 * khb * C_CHUNK, C_CHUNK), lambda kb, i: (kb, i, 0, 0), "const")
            r = scan_bwd(f"gdnb_{tag}_{direction}", f_gdn(direction == 1, khb, True), n_kblk, n_c,
                         gd_args(z, gates, direction) + [xarg], sv,
                         [(do, (C_CHUNK, khb * 2 * HEAD), lambda kb, i, tok=c_tok(direction == 1): (tok(i), kb))])
            for lst, v_ in zip((dq, dk, dv, dga, dgb, dal, ddt), r):
                lst.append(v_)
        dz = assemble(f"dz_{tag}", [[(dq[0], kw, 0), (dq[1], kw, 0)], [(dk[0], kw, 1), (dk[1], kw, 1)],
                                    [(dv[0], vw, 1), (dv[1], vw, 1)]], F32)
        du, d_conv = conv_bwd(f"convb_{tag}", pm, cv_full, dz, 2 * kw + vw, nct)
        dpm = assemble(f"dpm_{tag}", [[(du, 2 * kw + vw, 0)], [(dzg, vw, 2)]], BF16)
        dgates = jnp.stack([dga[0][0], dga[1][1], dgb[0][2], dgb[1][3]])
        dpg = dgates.transpose(2, 4, 0, 1, 3).reshape(t, n_gate).astype(BF16)
        dh = matmul(f"wgated_{tag}", dpg, w_od_gate, "nt")
        dh = matmul(f"wind_{tag}", dpm, w_od_main, "nt", add=dh)
        dw_in = jnp.concatenate([wgrad(f"winw_{tag}", h, dpm), wgrad(f"wgatew_{tag}", h, dpg)], axis=1)
        acc["od_w_in"] = dw_in.reshape(d, N_DEV, dw_in.shape[1] // N_DEV).transpose(1, 0, 2)
        dxs, dnw0, dmod_b = norm_mod_bwd(f"nm1b_{tag}", xs, layer, 0, 0, 1, dh, dxn)
        acc["norm_w"][layer][0] = dnw0
        acc["norm_w"][layer][1] = dnw1
        acc["mods"][layer].extend([dmod_a, dmod_b])
        acc["od_norm"] = d_onorm
        acc["od_conv"] = d_conv
        acc["od_A_log"] = jnp.stack([dal[0][0], dal[1][1]]).reshape(1, 2, n_vh)
        acc["od_dt_bias"] = jnp.stack([ddt[0][0], ddt[1][1]]).reshape(1, 2, n_vh)
        return dxs

    xs0 = jnp.concatenate([ctx[0], x[0]], axis=0)
    xs1, sv_e = even_fwd("l0", xs0, 0)
    g_w13a, g_w2a = gathered(ag_ffn0, xs1, "ag_wait_ffn0")
    w13[0], w2[0] = col_weight(g_w13a), rows_natural(g_w2a)
    xs2, sv_f0 = ffn_fwd("l0", xs1, 0)
    g_od_in, g_od_out, g_w13b, g_w2b = gathered(ag_l1, xs2, "ag_wait_l1")
    w_od_in = cols_natural(g_od_in)
    w_od_main, w_od_gate = w_od_in[:, :2 * kw + 2 * vw], w_od_in[:, 2 * kw + 2 * vw:]
    w_od_out = rows_natural(g_od_out)
    w13[1], w2[1] = col_weight(g_w13b), rows_natural(g_w2b)
    xs3, sv_o = odd_fwd("l1", xs2, 1)
    xs4, sv_f1 = ffn_fwd("l1", xs3, 1)
    loss_loc, dxs = loss_kernel("loss", xs4, loss_target[0], nct)
    loss = lax.psum(loss_loc, ("x", "y", "c"))

    acc = {"norm_w": [[None] * 4 for _ in range(depth)], "mods": [[] for _ in range(depth)],
           "ffn_w13": [None] * depth, "ffn_w2": [None] * depth}
    dxs = ffn_bwd("l1", sv_f1, 1, dxs, acc)
    rs_ffn1 = exchange_start("rs_start_ffn1", [acc["ffn_w13"][1], acc["ffn_w2"][1]], True, dxs)
    mods[1] = mods[1] + rs_ffn1[4][0, 0]
    dxs = odd_bwd("l1", sv_o, 1, dxs, acc)
    rs_od = exchange_start("rs_start_od", [acc["od_w_in"], acc["od_w_out"]], True, dxs)
    mods[0] = mods[0] + rs_od[4][0, 0]
    dxs = ffn_bwd("l0", sv_f0, 0, dxs, acc)
    rs_ffn0 = exchange_start("rs_start_ffn0", [acc["ffn_w13"][0], acc["ffn_w2"][0]], True, dxs)
    mods[0] = mods[0] + rs_ffn0[4][0, 0]
    dxs = even_bwd("l0", sv_e, 0, dxs, acc)
    rs_ev = acc["rs_ev"]
    grad_x = dxs[n_ctx:].reshape(1, seq, d)

    (d_lb_slots) = tile_bwd("lb_bwd", f_lb(0), (1,), lb_slots, [(acc["lb0"], (2, aw), lambda i: (0, 0))])
    d_ev_lb = jnp.stack(d_lb_slots, axis=1)

    dmods = jnp.stack([functools.reduce(jnp.add, acc["mods"][layer]) for layer in range(depth)])
    (dm_all,) = all_gather("ag_dmod", [dmods.reshape(depth * 2 * 6, d)])
    dm_all = dm_all.reshape(N_DEV, depth, 2, 6 * d)
    dm_cols = lax.dynamic_slice_in_dim(dm_all, me * ada_loc, ada_loc, axis=3)
    dm_loc = jnp.concatenate([dm_cols[:, :, 1].transpose(1, 0, 2), dm_cols[:, :, 0].transpose(1, 0, 2)], axis=1)
    d_cctx_part, d_w_ada, d_b_loc = tile_bwd("ada_bwd", f_ada, ada_grid, ada_args,
                                             [(dm_loc, (None, 2 * N_DEV, ada_cb), lambda l, j: (l, 0, j))])

    d_b_full = lax.dynamic_update_slice_in_dim(jnp.zeros_like(b_ada), d_b_loc.reshape(depth, ada_loc), me * ada_loc, axis=1)
    d_nw = jnp.stack([jnp.stack([acc["norm_w"][layer][s].reshape(d) for s in range(4)]) for layer in range(depth)])
    small_grads = [d_cctx_part.reshape(d), d_b_full, d_nw, d_ev_lb, acc["ev_a_norm"], acc["ev_pool_w"],
                   acc["ev_pool_scale"], acc["od_conv"], acc["od_A_log"], acc["od_dt_bias"], acc["od_norm"]]
    sg_shapes = [a.shape for a in small_grads]
    (sg,) = all_gather("ag_smallgrads", [_pack(small_grads, F32, FLAT_W, SUBLANES)])
    sg_sum = sum_leading("sum_smallgrads", sg, F32)
    (g_cctx, g_bada, g_nw, g_lb, g_anorm, g_pw, g_ps, g_conv, g_alog, g_dtb, g_onorm) = _unpack(sg_sum, sg_shapes)

    def my_cols(full, axis):
        loc = full.shape[axis] // N_DEV
        return lax.dynamic_slice_in_dim(full, me * loc, loc, axis=axis)

    grads = {
        "c_ctx": g_cctx, "w_ada": d_w_ada, "b_ada": g_bada, "norm_w": my_cols(g_nw, 2), "ev_lb": my_cols(g_lb, 2),
        "ev_a_norm": g_anorm, "ev_pool_w": my_cols(g_pw, 1)[None], "ev_pool_scale": g_ps,
        "od_conv": my_cols(g_conv, 1)[None], "od_A_log": g_alog, "od_dt_bias": g_dtb, "od_norm": g_onorm,
    }

    def reduced(started, tags_, name):
        srcs, lands = exchange_wait(name, started, sg_sum, True)
        out = []
        for tg, s_, l_ in zip(tags_, srcs, lands):
            own = lax.dynamic_index_in_dim(s_, me, 0, keepdims=True)
            out.append(sum_leading(f"rs_sum_{tg}", lax.dynamic_update_slice_in_dim(l_, own, me, 0), F32))
        return out

    g_w13b, g_w2b = reduced(rs_ffn1, ["w13b", "w2b"], "rs_wait_ffn1")
    g_od_in, g_od_out = reduced(rs_od, ["od_in", "od_out"], "rs_wait_od")
    g_w13a, g_w2a = reduced(rs_ffn0, ["w13a", "w2a"], "rs_wait_ffn0")
    g_ev_in, g_ev_out = reduced(rs_ev, ["ev_in", "ev_out"], "rs_wait_ev")
    grads["ev_w_in"], grads["ev_w_out"], grads["od_w_in"], grads["od_w_out"] = (g_ev_in[None], g_ev_out[None],
                                                                                g_od_in[None], g_od_out[None])
    grads["ffn_w13"] = jnp.stack([g_w13a, g_w13b])
    grads["ffn_w2"] = jnp.stack([g_w2a, g_w2b])

    big_names = ["w_ada", "ev_w_in", "ev_w_out", "od_w_in", "od_w_out", "ffn_w13", "ffn_w2"]
    small_names = [n_ for n_ in names if n_ not in big_names]
    gl = {n_: grads[n_].reshape(wts[n_].shape) for n_ in names}
    delta, new_m, new_v = {}, {}, {}
    for n_ in big_names:
        shp = wts[n_].shape
        res = adamw(f"adamw_{n_}", _rows2d(gl[n_]), _rows2d(wts[n_]), _rows2d(mom1[n_]), _rows2d(mom2[n_]))
        delta[n_], new_m[n_], new_v[n_] = (r_.reshape(shp) for r_ in res)
    shapes = [wts[n_].shape for n_ in small_names]
    pk = lambda dct: _pack([dct[n_] for n_ in small_names], F32, FLAT_W, SUBLANES)
    res = adamw("adamw_small", pk(gl), pk(wts), pk(mom1), pk(mom2))
    for dct, r_ in zip((delta, new_m, new_v), res):
        for n_, a_ in zip(small_names, _unpack(r_, shapes)):
            dct[n_] = a_
    return (loss, grad_x, *[gl[n_] for n_ in names], *[delta[n_] for n_ in names], *[new_m[n_] for n_ in names],
            *[new_v[n_] for n_ in names])
```

```python
import functools
from typing import Any, NamedTuple

import numpy as np

import jax
import jax.numpy as jnp
from jax import lax
from jax.experimental import pallas as pl
from jax.experimental.pallas import tpu as pltpu

F32 = jnp.float32
BF16 = jnp.bfloat16
MESH = pl.DeviceIdType.MESH
N_DEV = 8

EPS = 1e-6
GRID_W = 64
HEAD = 128
A_CHUNK = 32
C_CHUNK = 64
C_CONV = 4
POOL_WINDOWS = (2, 4, 8, 16)
ADAM_LR, ADAM_B1, ADAM_B2, ADAM_EPS, ADAM_WD, ADAM_STEP = 0.001, 0.9, 0.999, 1e-08, 0.01, 10

VMEM_LIMIT_BYTES = 56 * 1024 * 1024
LANES = 128
SUBLANES = 8
ROW_TILE = 256
FLAT_W = 1024
TM_PREFS = (1056, 1024, 768, 512, 256, 128, 64, 32, 16)
TN_PREFS = (768, 512, 1408, 256, 128)
TK_PREFS = (2048, 2816, 1408, 1024, 768, 512, 384, 256, 128)
TO_PREFS = (1024, 1408, 768, 704, 512, 384, 256, 128)
HEADS_PER_STEP = 8


def _pick(dim, prefs):
    for p in prefs:
        if p <= dim and dim % p == 0:
            return p
    return dim


def _cparams(ngrid):
    return pltpu.CompilerParams(dimension_semantics=("arbitrary",) * ngrid, vmem_limit_bytes=VMEM_LIMIT_BYTES)


def _sds(shape, dtype):
    return jax.ShapeDtypeStruct(tuple(shape), dtype)


def _split(x):
    hi = x.astype(BF16)
    return hi, (x - hi.astype(F32)).astype(BF16)


def _dot(a, b, ca, cb, hi):
    dims = (((ca,), (cb,)), ((), ()))
    dot = lambda u, v: lax.dot_general(u, v, dims, preferred_element_type=F32)
    if hi:
        (ah, al), (bh, bl) = _split(a.astype(F32)), _split(b.astype(F32))
        return dot(ah, bh) + (dot(ah, bl) + dot(al, bh))
    return dot(a.astype(BF16), b.astype(BF16))


@functools.partial(jax.custom_vjp, nondiff_argnums=(2, 3, 4))
def mm(a, b, ca=1, cb=0, hi=False):
    return _dot(a, b, ca, cb, hi)


def _mm_fwd(a, b, ca, cb, hi):
    return _dot(a, b, ca, cb, hi), (a, b)


def _mm_bwd(ca, cb, hi, res, g):
    a, b = res
    da = _dot(g, b, 1, 1 - cb, hi) if ca == 1 else _dot(b, g, 1 - cb, 1, hi)
    db = _dot(a, g, 1 - ca, 0, hi) if cb == 0 else _dot(g, a, 0, 1 - ca, hi)
    return da, db


mm.defvjp(_mm_fwd, _mm_bwd)


def _iota2(n, m, axis):
    return lax.broadcasted_iota(jnp.int32, (n, m), axis)


class TArg(NamedTuple):
    arr: Any
    block: tuple
    imap: Any
    kind: str = "row"
    acc: tuple = ()
    gdtype: Any = F32
    grad: bool = True


def _load(ref):
    v = ref[...]
    return v.astype(F32) if jnp.issubdtype(v.dtype, jnp.floating) else v


def tile_fwd(name, f, grid, args, outs):
    n_in, ng = len(args), len(grid)

    def body(*refs):
        pids = tuple(pl.program_id(k) for k in range(ng))
        res = f(pids, *[_load(r) for r in refs[:n_in]])
        for r, v in zip(refs[n_in:], res):
            r[...] = v.astype(r.dtype)

    return pl.pallas_call(
        body, grid=grid, name=name,
        in_specs=[pl.BlockSpec(a.block, a.imap) for a in args],
        out_specs=[pl.BlockSpec(b, im) for (_, _, b, im) in outs],
        out_shape=[_sds(s, d) for (s, d, _, _) in outs],
        compiler_params=_cparams(ng),
    )(*[a.arr for a in args])


def _store_grads(args, diff, pids, g_refs, d):
    for k, gr, dv in zip(diff, g_refs, d):
        a = args[k]
        if a.kind == "row" or not a.acc:
            gr[...] = dv.astype(gr.dtype)
        else:
            first = pids[a.acc[0]] == 0
            for ax in a.acc[1:]:
                first = jnp.logical_and(first, pids[ax] == 0)

            @pl.when(first)
            def _(gr=gr, dv=dv):
                gr[...] = dv.astype(gr.dtype)

            @pl.when(jnp.logical_not(first))
            def _(gr=gr, dv=dv):
                gr[...] += dv.astype(gr.dtype)


def tile_bwd(name, f, grid, args, cts):
    n_in, n_ct, ng = len(args), len(cts), len(grid)
    diff = [k for k, a in enumerate(args) if a.kind != "const" and a.grad]

    def body(*refs):
        pids = tuple(pl.program_id(k) for k in range(ng))
        vals = [_load(r) for r in refs[:n_in]]

        def g(*dv):
            full = list(vals)
            for k, v in zip(diff, dv):
                full[k] = v
            return tuple(f(pids, *full))

        _, vjp = jax.vjp(g, *[vals[k] for k in diff])
        d = vjp(tuple(_load(r) for r in refs[n_in:n_in + n_ct]))
        _store_grads(args, diff, pids, refs[n_in + n_ct:], d)

    return pl.pallas_call(
        body, grid=grid, name=name,
        in_specs=[pl.BlockSpec(a.block, a.imap) for a in args] + [pl.BlockSpec(b, im) for (_, b, im) in cts],
        out_specs=[pl.BlockSpec(args[k].block, args[k].imap) for k in diff],
        out_shape=[_sds(args[k].arr.shape, args[k].gdtype) for k in diff],
        compiler_params=_cparams(ng),
    )(*[a.arr for a in args], *[c[0] for c in cts])


def scan_fwd(name, f, n_heads, n_steps, args, outs, n_state):
    n_in, n_out = len(args), len(outs)
    sblock = (None, None, HEAD, HEAD)

    def body(*refs):
        in_refs = refs[:n_in]
        out_refs = refs[n_in:n_in + n_out]
        save_refs = refs[n_in + n_out:n_in + n_out + n_state]
        s_refs = refs[n_in + n_out + n_state:]

        @pl.when(pl.program_id(1) == 0)
        def _():
            for s in s_refs:
                s[...] = jnp.zeros_like(s)

        states = tuple(s[...] for s in s_refs)
        for sv, s in zip(save_refs, states):
            sv[...] = s
        new_states, res = f(states, *[_load(r) for r in in_refs])
        for s, v in zip(s_refs, new_states):
            s[...] = v
        for r, v in zip(out_refs, res):
            r[...] = v.astype(r.dtype)

    res = pl.pallas_call(
        body, grid=(n_heads, n_steps), name=name,
        in_specs=[pl.BlockSpec(a.block, a.imap) for a in args],
        out_specs=[pl.BlockSpec(b, im) for (_, _, b, im) in outs]
        + [pl.BlockSpec(sblock, lambda h, i: (h, i, 0, 0))] * n_state,
        out_shape=[_sds(s, d) for (s, d, _, _) in outs] + [_sds((n_heads, n_steps, HEAD, HEAD), F32)] * n_state,
        scratch_shapes=[pltpu.VMEM((HEAD, HEAD), F32)] * n_state,
        compiler_params=_cparams(2),
    )(*[a.arr for a in args])
    return res[:n_out], res[n_out:]


def scan_bwd(name, f, n_heads, n_steps, args, saves, cts):
    n_in, n_ct, n_state = len(args), len(cts), len(saves)
    diff = [k for k, a in enumerate(args) if a.kind != "const" and a.grad]
    sblock = (None, None, HEAD, HEAD)

    def rv(im):
        return lambda h, i: im(h, n_steps - 1 - i)

    def body(*refs):
        in_refs = refs[:n_in]
        save_refs = refs[n_in:n_in + n_state]
        ct_refs = refs[n_in + n_state:n_in + n_state + n_ct]
        g_refs = refs[n_in + n_state + n_ct:n_in + n_state + n_ct + len(diff)]
        ds_refs = refs[n_in + n_state + n_ct + len(diff):]
        pids = (pl.program_id(0), pl.program_id(1))

        @pl.when(pids[1] == 0)
        def _():
            for s in ds_refs:
                s[...] = jnp.zeros_like(s)

        vals = [_load(r) for r in in_refs]

        def g(states, *dv):
            full = list(vals)
            for k, v in zip(diff, dv):
                full[k] = v
            new_states, res = f(states, *full)
            return tuple(new_states), tuple(res)

        _, vjp = jax.vjp(g, tuple(s[...] for s in save_refs), *[vals[k] for k in diff])
        d = vjp((tuple(s[...] for s in ds_refs), tuple(_load(r) for r in ct_refs)))
        for s, v in zip(ds_refs, d[0]):
            s[...] = v
        _store_grads(args, diff, pids, g_refs, d[1:])

    return pl.pallas_call(
        body, grid=(n_heads, n_steps), name=name,
        in_specs=[pl.BlockSpec(a.block, rv(a.imap)) for a in args]
        + [pl.BlockSpec(sblock, rv(lambda h, i: (h, i, 0, 0)))] * n_state
        + [pl.BlockSpec(b, rv(im)) for (_, b, im) in cts],
        out_specs=[pl.BlockSpec(args[k].block, rv(args[k].imap)) for k in diff],
        out_shape=[_sds(args[k].arr.shape, args[k].gdtype) for k in diff],
        scratch_shapes=[pltpu.VMEM((HEAD, HEAD), F32)] * n_state,
        compiler_params=_cparams(2),
    )(*[a.arr for a in args], *saves, *[c[0] for c in cts])


def matmul(name, a, b, mode, add=None, out_dtype=F32, slabs=False, out_slabs=False):
    o_spec = None
    if mode == "nn":
        m, k = a.shape
        ns = b.shape[2] if slabs else (b.shape[1] // N_DEV if out_slabs else b.shape[1])
        n = N_DEV * ns if (slabs or out_slabs) else ns
        to_m, to_n, tr = _pick(m, TM_PREFS), _pick(ns, TN_PREFS), _pick(k, TK_PREFS)
        nb = ns // to_n
        grid = (m // to_m, n // to_n, k // tr)
        a_spec = pl.BlockSpec((to_m, tr), lambda i, j, l: (i, l))
        if slabs:
            b_spec = pl.BlockSpec((None, tr, to_n), lambda i, j, l: (j // nb, l, j % nb))
        else:
            b_spec = pl.BlockSpec((tr, to_n), lambda i, j, l: (l, j))
        dims, oshape = (1, 0), (m, n)
        if out_slabs:
            o_spec = pl.BlockSpec((None, to_m, to_n), lambda i, j, l: (j // nb, i, j % nb))
            oshape = (N_DEV, m, ns)
    elif mode == "nt":
        m, n = a.shape
        k = b.shape[1] if slabs else b.shape[0]
        ns = n // N_DEV if slabs else n
        to_m, to_n, tr = _pick(m, TM_PREFS), _pick(k, TO_PREFS), _pick(ns, TK_PREFS)
        nb = ns // tr
        grid = (m // to_m, k // to_n, n // tr)
        a_spec = pl.BlockSpec((to_m, tr), lambda i, j, l: (i, l))
        if slabs:
            b_spec = pl.BlockSpec((None, to_n, tr), lambda i, j, l: (l // nb, j, l % nb))
        else:
            b_spec = pl.BlockSpec((to_n, tr), lambda i, j, l: (j, l))
        dims, oshape = (1, 1), (m, k)
    else:
        (t, k), n = a.shape, b.shape[1]
        ns = n // N_DEV if slabs else n
        to_m, to_n, tr = _pick(k, TO_PREFS), _pick(ns, (2048,) + TO_PREFS), _pick(t, TM_PREFS)
        nb = ns // to_n
        grid = (k // to_m, n // to_n, t // tr)
        a_spec = pl.BlockSpec((tr, to_m), lambda i, j, l: (l, i))
        b_spec = pl.BlockSpec((tr, to_n), lambda i, j, l: (l, j))
        dims, oshape = (0, 0), (k, n)
        if slabs:
            o_spec = pl.BlockSpec((None, to_m, to_n), lambda i, j, l: (j // nb, i, j % nb))
            oshape = (N_DEV, k, ns)
    n_red = grid[2]
    if o_spec is None:
        o_spec = pl.BlockSpec((to_m, to_n), lambda i, j, l: (i, j))
    has_add = add is not None

    def body(a_ref, b_ref, *rest):
        add_ref = rest[0] if has_add else None
        o_ref = rest[1] if has_add else rest[0]
        part = lax.dot_general(a_ref[...].astype(BF16), b_ref[...].astype(BF16),
                               (((dims[0],), (dims[1],)), ((), ())), preferred_element_type=F32)

        def finish(v):
            if has_add:
                v = v + add_ref[...]
            o_ref[...] = v.astype(o_ref.dtype)

        if n_red == 1:
            finish(part)
        else:
            acc = rest[-1]
            step = pl.program_id(2)

            @pl.when(step == 0)
            def _():
                acc[...] = part

            @pl.when(step > 0)
            def _():
                acc[...] += part

            @pl.when(step == n_red - 1)
            def _():
                finish(acc[...])

    return pl.pallas_call(
        body, grid=grid, name=name,
        in_specs=[a_spec, b_spec] + ([o_spec] if has_add else []),
        out_specs=o_spec, out_shape=_sds(oshape, out_dtype),
        scratch_shapes=[pltpu.VMEM((to_m, to_n), F32)] if n_red > 1 else [],
        compiler_params=_cparams(3),
    )(a, b, *([add] if has_add else []))


def assemble(name, pieces, out_dtype):
    flat = [s for piece in pieces for s in piece]
    t = flat[0][0].shape[0]
    widths = [piece[0][1] for piece in pieces]

    def body(*refs):
        o_ref, k, off = refs[-1], 0, 0
        for piece, w in zip(pieces, widths):
            v = refs[k][...].astype(F32)
            k += 1
            for _ in piece[1:]:
                v = v + refs[k][...].astype(F32)
                k += 1
            o_ref[:, off:off + w] = v.astype(o_ref.dtype)
            off += w

    tr = ROW_TILE // 2
    return pl.pallas_call(
        body, grid=(t // tr,), name=name,
        in_specs=[pl.BlockSpec((tr, w), functools.partial(lambda i, cb: (i, cb), cb=cb)) for (_, w, cb) in flat],
        out_specs=pl.BlockSpec((tr, sum(widths)), lambda i: (i, 0)),
        out_shape=_sds((t, sum(widths)), out_dtype),
        compiler_params=_cparams(1),
    )(*[s[0] for s in flat])


ELEM_ROWS = (128, 64, 32, 16, 8)


def _rows2d(a, lead=0):
    return a.reshape(a.shape[:lead] + (-1, a.shape[-1]))


def sum_leading(name, arr, out_dtype):
    k, rows, w = arr.shape
    tr = _pick(rows, ELEM_ROWS)

    def body(a_ref, o_ref):
        v = a_ref[0].astype(F32)
        for j in range(1, k):
            v = v + a_ref[j].astype(F32)
        o_ref[...] = v.astype(o_ref.dtype)

    return pl.pallas_call(
        body, grid=(rows // tr,), name=name,
        in_specs=[pl.BlockSpec((k, tr, w), lambda i: (0, i, 0))],
        out_specs=pl.BlockSpec((tr, w), lambda i: (i, 0)),
        out_shape=_sds((rows, w), out_dtype), compiler_params=_cparams(1),
    )(arr)


def adamw(name, g, w, m, v):
    rows, wd = g.shape
    tr = _pick(rows, ELEM_ROWS)

    def body(g_ref, w_ref, m_ref, v_ref, d_ref, nm_ref, nv_ref):
        gv = g_ref[...]
        mn = ADAM_B1 * m_ref[...] + (1.0 - ADAM_B1) * gv
        vn = ADAM_B2 * v_ref[...] + (1.0 - ADAM_B2) * jnp.square(gv)
        m_hat = mn / (1.0 - ADAM_B1 ** ADAM_STEP)
        v_hat = vn / (1.0 - ADAM_B2 ** ADAM_STEP)
        d_ref[...] = -ADAM_LR * (m_hat / (jnp.sqrt(v_hat) + ADAM_EPS) + ADAM_WD * w_ref[...])
        nm_ref[...] = mn
        nv_ref[...] = vn

    spec = pl.BlockSpec((tr, wd), lambda i: (i, 0))
    return pl.pallas_call(
        body, grid=(rows // tr,), name=name, in_specs=[spec] * 4, out_specs=[spec] * 3,
        out_shape=[_sds(g.shape, F32)] * 3, compiler_params=_cparams(1),
    )(g, w, m, v)


def loss_kernel(name, xs, target, n_ctx_tiles):
    t, d = xs.shape
    nt = t // ROW_TILE

    def body(x_ref, t_ref, dx_ref, l_ref):
        i = pl.program_id(0)
        is_lat = i >= n_ctx_tiles
        err = jnp.where(is_lat, x_ref[...] - t_ref[...], 0.0)
        dx_ref[...] = err / d
        part = 0.5 * jnp.sum(jnp.mean(jnp.square(err), axis=-1, keepdims=True), axis=0, keepdims=True)

        @pl.when(i == 0)
        def _():
            l_ref[...] = jnp.zeros_like(l_ref)

        l_ref[...] += jnp.broadcast_to(part, l_ref.shape)

    dx, l = pl.pallas_call(
        body, grid=(nt,), name=name,
        in_specs=[pl.BlockSpec((ROW_TILE, d), lambda i: (i, 0)),
                  pl.BlockSpec((ROW_TILE, d), lambda i: (jnp.maximum(i - n_ctx_tiles, 0), 0))],
        out_specs=[pl.BlockSpec((ROW_TILE, d), lambda i: (i, 0)), pl.BlockSpec((SUBLANES, LANES), lambda i: (0, 0))],
        out_shape=[_sds((t, d), F32), _sds((SUBLANES, LANES), F32)], compiler_params=_cparams(1),
    )(xs, target)
    return l[0, 0], dx


CONV_COLS = 2048
CONV_LEFT = C_CONV // 2


def _conv_halo_specs(t, n_ctx_tiles):
    nt = t // ROW_TILE
    per = ROW_TILE // SUBLANES
    cur = pl.BlockSpec((ROW_TILE, CONV_COLS), lambda j, i: (i, j))
    prev = pl.BlockSpec((SUBLANES, CONV_COLS), lambda j, i: (jnp.maximum(i * per - 1, 0), j))
    nxt = pl.BlockSpec((SUBLANES, CONV_COLS), lambda j, i: (jnp.minimum((i + 1) * per, nt * per - 1), j))
    return cur, prev, nxt


def _fill_ext(ext, prev_ref, cur_ref, next_ref, i, nt, n_ctx_tiles):
    has_prev = jnp.logical_and(i != 0, i != n_ctx_tiles)
    has_next = jnp.logical_and(i != n_ctx_tiles - 1, i != nt - 1)
    ext[0:SUBLANES, :] = jnp.where(has_prev, prev_ref[...], 0.0)
    ext[SUBLANES:SUBLANES + ROW_TILE, :] = cur_ref[...]
    ext[SUBLANES + ROW_TILE:, :] = jnp.where(has_next, next_ref[...], 0.0)


def conv_fwd(name, p, w, width, n_ctx_tiles):
    t = p.shape[0]
    nt = t // ROW_TILE
    cur, prev, nxt = _conv_halo_specs(t, n_ctx_tiles)

    def body(c_ref, p_ref, n_ref, w_ref, o_ref, ext):
        _fill_ext(ext, p_ref, c_ref, n_ref, pl.program_id(1), nt, n_ctx_tiles)
        acc = None
        for j in range(C_CONV):
            term = ext[pl.ds(SUBLANES + j - CONV_LEFT, ROW_TILE), :] * w_ref[j:j + 1, :]
            acc = term if acc is None else acc + term
        o_ref[...] = acc

    return pl.pallas_call(
        body, grid=(width // CONV_COLS, nt), name=name,
        in_specs=[cur, prev, nxt, pl.BlockSpec((C_CONV, CONV_COLS), lambda j, i: (0, j))],
        out_specs=cur, out_shape=_sds((t, width), F32),
        scratch_shapes=[pltpu.VMEM((ROW_TILE + 2 * SUBLANES, CONV_COLS), F32)],
        compiler_params=_cparams(2),
    )(p, p, p, w)


def conv_bwd(name, p, w, dz, width, n_ctx_tiles, into):
    t = p.shape[0]
    nt = t // ROW_TILE
    cur, prev, nxt = _conv_halo_specs(t, n_ctx_tiles)

    def body(c_ref, p_ref, n_ref, dc_ref, dp_ref, dn_ref, w_ref, into_ref, du_ref, dw_ref, ext, dext):
        i = pl.program_id(1)
        _fill_ext(ext, p_ref, c_ref, n_ref, i, nt, n_ctx_tiles)
        _fill_ext(dext, dp_ref, dc_ref, dn_ref, i, nt, n_ctx_tiles)
        dzc = dc_ref[...]
        @pl.when(i == 0)
        def _():
            dw_ref[...] = jnp.zeros_like(dw_ref)

        acc = None
        for j in range(C_CONV):
            term = dext[pl.ds(SUBLANES + CONV_LEFT - j, ROW_TILE), :] * w_ref[j:j + 1, :]
            acc = term if acc is None else acc + term
            dw_ref[j:j + 1, :] += jnp.sum(dzc * ext[pl.ds(SUBLANES + j - CONV_LEFT, ROW_TILE), :], axis=0, keepdims=True)
        du_ref[...] = acc.astype(du_ref.dtype)

    wspec = pl.BlockSpec((C_CONV, CONV_COLS), lambda j, i: (0, j))
    return pl.pallas_call(
        body, grid=(width // CONV_COLS, nt), name=name,
        in_specs=[cur, prev, nxt, cur, prev, nxt, wspec, pl.BlockSpec(memory_space=pl.ANY)],
        out_specs=[cur, wspec], out_shape=[_sds(into.shape, into.dtype), _sds((C_CONV, width), F32)],
        input_output_aliases={7: 0},
        scratch_shapes=[pltpu.VMEM((ROW_TILE + 2 * SUBLANES, CONV_COLS), F32)] * 2,
        compiler_params=_cparams(2),
    )(p, p, p, dz, dz, dz, w, into)


def _rms(x, w):
    return x * lax.rsqrt(jnp.mean(x * x, axis=-1, keepdims=True) + EPS) * w


def _seg_mod(mods, is_ctx):
    return jnp.where(is_ctx, mods[0], mods[1])


def f_norm_mod(shift_i, scale_i, n_ctx_tiles, passthrough=False, transposed=False):
    def f(pids, x, nw, mods):
        m = _seg_mod(mods, pids[0] < n_ctx_tiles)
        h = _rms(x, nw) * (1.0 + m[scale_i:scale_i + 1]) + m[shift_i:shift_i + 1]
        if transposed:
            return (h, h.T)
        return (h, x) if passthrough else (h,)
    return f


def f_gate_res(gate_i, n_ctx_tiles):
    def f(pids, x, y, nw, mods):
        m = _seg_mod(mods, pids[0] < n_ctx_tiles)
        return (x + m[gate_i:gate_i + 1] * _rms(y, nw),)
    return f


def f_swiglu(pids, gu):
    half = gu.shape[1] // 2
    return (jax.nn.silu(gu[:, :half]) * gu[:, half:],)


def f_readout(n_heads):
    def f(pids, o_a, o_b, gate, nw):
        cols = [slice(j * HEAD, (j + 1) * HEAD) for j in range(n_heads)]
        outs = _each(lambda cs: _rms(o_a[:, cs] + o_b[:, cs], nw) * jax.nn.silu(gate[:, cs]), cols)
        return (jnp.concatenate(outs, axis=1) if n_heads > 1 else outs[0],)
    return f


def f_pool(pids, u, pmat, pw, scale):
    d = mm(pmat, u, 1, 0, True) - u
    return (mm(d, pw) * scale,)


def f_lb(layer):
    def f(pids, *slots):
        top = slots[0]
        for s in slots[1:]:
            top = jnp.maximum(top, s)
        ex = [jnp.exp(s - top) for s in slots]
        tot = ex[0]
        for e in ex[1:]:
            tot = tot + e
        part = ex[0]
        for e in ex[1:layer + 1]:
            part = part + e
        return (part / tot,)
    return f


def f_ada(pids, c_all, c_ctx, w, b):
    c16 = jnp.concatenate([c_all, jnp.broadcast_to(c_ctx, c_all.shape)], axis=0)
    return (mm(jax.nn.silu(c16), w) + b,)


def _each(fn, *lists):
    return [fn(*xs) for xs in zip(*lists)]


def _hgrn2_heads(sts, qrs, frs, irs, lbs, rev):
    c = A_CHUNK
    mid = c - c // 2 if rev else c // 2 - 1
    ri, ci = _iota2(c, c, 0), _iota2(c, c, 1)
    incl = (ri <= ci) if rev else (ri >= ci)
    incl_f = incl.astype(F32)
    qs = _each(jax.nn.silu, qrs)
    log_fs = _each(lambda lb, fr: jnp.log(lb + (1.0 - lb) * jax.nn.sigmoid(fr)), lbs, frs)
    ks = _each(lambda lb, fr: (1.0 - lb) * jax.nn.sigmoid(-fr), lbs, frs)
    bs = _each(lambda lf: mm(incl_f, lf, 1, 0, True), log_fs)
    b_lasts = _each(lambda lf: jnp.sum(lf, axis=0, keepdims=True), log_fs)
    scores = _each(lambda q, k, b: mm(q * jnp.exp(b - b[mid:mid + 1]), k * jnp.exp(b[mid:mid + 1] - b), 1, 1), qs, ks, bs)
    intra = _each(lambda sc, ir: mm(jnp.where(incl, sc, 0.0), ir), scores, irs)
    inter = _each(lambda q, b, st: mm(q * jnp.exp(b), st, 1, 1), qs, bs, sts)
    upd = _each(lambda ir, k, bl, b: mm(ir, k * jnp.exp(bl - b), 0, 0), irs, ks, b_lasts, bs)
    new = _each(lambda st, bl, u: st * jnp.exp(bl) + u, sts, b_lasts, upd)
    return new, _each(jnp.add, intra, inter)


def f_hgrn2(rev, hb):
    def f(states, qr, fr, ir, lb):
        cols = [slice(j * HEAD, (j + 1) * HEAD) for j in range(hb)]
        new, outs = _hgrn2_heads(list(states), [qr[:, cs] for cs in cols], [fr[:, cs] for cs in cols],
                                 [ir[:, cs] for cs in cols], [lb[j] for j in range(hb)], rev)
        return tuple(new), (jnp.concatenate(outs, axis=1) if hb > 1 else outs[0],)
    return f


def _neumann_inv(a_lows):
    n = a_lows[0].shape[0]
    eye = (_iota2(n, n, 0) == _iota2(n, n, 1)).astype(F32)
    ps = _each(lambda a: -a, a_lows)
    xs = _each(lambda p: eye + p, ps)
    ps = _each(lambda p: mm(p, p, 1, 0, True), ps)
    k = 2
    while 2 * k < n:
        ys = _each(lambda p, x: mm(jnp.concatenate([p, x], axis=0), p, 1, 0, True), ps, xs)
        ps = _each(lambda y: y[:n], ys)
        xs = _each(lambda x, y: x + y[n:], xs, ys)
        k *= 2
    return tuple(_each(lambda x, p: x + mm(x, p, 1, 0, True), xs, ps))


@jax.custom_vjp
def unit_tri_inv(a_lows):
    return _neumann_inv(a_lows)


def _uti_fwd(a_lows):
    xs = _neumann_inv(a_lows)
    return xs, xs


def _uti_bwd(xs, gs):
    ts = _each(lambda x, g: mm(x, g, 0, 0, True), xs, gs)
    return (tuple(_each(lambda t, x: -mm(t, x, 1, 1, True), ts, xs)),)


unit_tri_inv.defvjp(_uti_fwd, _uti_bwd)


@jax.custom_vjp
def unit_tri_inv_saved(a_lows, xs):
    return xs


def _utis_fwd(a_lows, xs):
    return xs, xs


def _utis_bwd(xs, gs):
    return _uti_bwd(xs, gs) + (tuple(jnp.zeros_like(x) for x in xs),)


unit_tri_inv_saved.defvjp(_utis_fwd, _utis_bwd)


def _l2n(x):
    return x * lax.rsqrt(jnp.sum(x * x, axis=-1, keepdims=True) + EPS)


def _gdn_heads(ss, qs, ks, vs, a_rows, b_rows, alogs, dtbs, rev, xs_saved=None):
    c = qs[0].shape[0]
    ri, ci = _iota2(c, c, 0), _iota2(c, c, 1)
    causal = (ri <= ci) if rev else (ri >= ci)
    causal_t = (ri >= ci) if rev else (ri <= ci)
    strict = (ri < ci) if rev else (ri > ci)
    eye = ri == ci
    sq = lambda row: jnp.broadcast_to(row, (c, c))
    to_col = lambda row: jnp.sum(jnp.where(eye, sq(row), 0.0), axis=1, keepdims=True)
    g_rows = _each(lambda al, a, dt: -jnp.exp(al) * jax.nn.softplus(a + dt), alogs, a_rows, dtbs)
    beta_cols = _each(lambda b: to_col(jax.nn.sigmoid(b)), b_rows)
    g_cols = _each(to_col, g_rows)
    gc_cols = _each(lambda g: jnp.sum(jnp.where(causal, sq(g), 0.0), axis=1, keepdims=True), g_rows)
    gc_rows = _each(lambda g: jnp.sum(jnp.where(causal_t, sq(g), 0.0), axis=0, keepdims=True), g_cols)
    gc_lasts = _each(lambda g: jnp.sum(g, axis=1, keepdims=True), g_rows)
    decays = _each(lambda gc, gr: jnp.where(causal, jnp.exp(jnp.where(causal, gc - gr, 0.0)), 0.0), gc_cols, gc_rows)
    k_betas = _each(jnp.multiply, ks, beta_cols)
    v_betas = _each(jnp.multiply, vs, beta_cols)
    kq_ks = _each(lambda kb, q, k: mm(jnp.concatenate([kb, q], axis=0), k, 1, 1), k_betas, qs, ks)
    a_lows = _each(lambda kk, dec: jnp.where(strict, kk[:c] * dec, 0.0), kq_ks, decays)
    qks = _each(lambda kk, dec: jnp.where(causal, kk[c:] * dec, 0.0), kq_ks, decays)
    xs = unit_tri_inv(tuple(a_lows)) if xs_saved is None else unit_tri_inv_saved(tuple(a_lows), tuple(xs_saved))
    egcs = _each(jnp.exp, gc_cols)
    uws = _each(lambda x, vb, kb, e: mm(x, jnp.concatenate([vb, kb * e], axis=1), 1, 0, True), xs, v_betas, k_betas, egcs)
    dv = vs[0].shape[1]
    wq_ss = _each(lambda uw, q, e, s: mm(jnp.concatenate([uw[:, dv:], q * e], axis=0), s), uws, qs, egcs, ss)
    v_news = _each(lambda uw, wq: uw[:, :dv] - wq[:c], uws, wq_ss)
    o_states = _each(lambda wq: wq[c:], wq_ss)
    o_locals = _each(mm, qks, v_news)
    upds = _each(lambda k, gl, gc, vn: mm(k * jnp.exp(gl - gc), vn, 0, 0), ks, gc_lasts, gc_cols, v_news)
    new = _each(lambda s, gl, u: s * jnp.exp(gl) + u, ss, gc_lasts, upds)
    return new, _each(jnp.add, o_states, o_locals), xs


def f_gdn(rev, khb, saved_inverse):
    def f(states, qr, kr, vr, a3, b3, alog3, dtb3, xcat=None):
        heads = [(j, r) for j in range(khb) for r in range(2)]
        cols = [slice(j * HEAD, (j + 1) * HEAD) for j in range(khb)]
        c = qr.shape[0]
        qk_ = _each(lambda cs: (_l2n(jax.nn.silu(qr[:, cs])) * (HEAD ** -0.5), _l2n(jax.nn.silu(kr[:, cs]))), cols)
        vs = [jax.nn.silu(vr[:, (2 * j + r) * HEAD:(2 * j + r + 1) * HEAD]) for j, r in heads]
        row = lambda arr3: [arr3[j][r:r + 1] for j, r in heads]
        xs_saved = [xcat[n * c:(n + 1) * c] for n in range(len(heads))] if saved_inverse else None
        new, outs, xs = _gdn_heads(list(states), [qk_[j][0] for j, _ in heads], [qk_[j][1] for j, _ in heads], vs,
                                   row(a3), row(b3), row(alog3), row(dtb3), rev, xs_saved)
        o = jnp.concatenate(outs, axis=1)
        return tuple(new), ((o,) if saved_inverse else (o, jnp.concatenate(xs, axis=0)))
    return f


def _hbm_spec():
    return pl.BlockSpec(memory_space=pltpu.HBM)


def all_gather(name, xs):
    nt = len(xs)

    def body(*refs):
        x_refs, out_refs = refs[:nt], refs[nt:2 * nt]
        send_sems, recv_sems, local_sems = refs[2 * nt:]
        x, y, c = lax.axis_index("x"), lax.axis_index("y"), lax.axis_index("c")
        me, sibling = (x, y, c), (x, y, 1 - c)
        chips = [(1 - x, y), (x, 1 - y), (1 - x, 1 - y)]

        def slab(t, px, py, pc):
            return out_refs[t].at[4 * px + 2 * py + pc]

        def copy(t, k, block, to, src=None):
            return pltpu.make_async_remote_copy(
                src_ref=slab(t, *block) if src is None else src, dst_ref=slab(t, *block),
                send_sem=send_sems.at[7 * t + k], recv_sem=recv_sems.at[7 * t + k], device_id=to, device_id_type=MESH)

        mine, first, passed = [], [], []
        for t in range(nt):
            mine.append(pltpu.make_async_copy(x_refs[t], slab(t, *me), local_sems.at[t]))
            mine[-1].start()
            cps = [copy(t, 0, me, sibling, src=x_refs[t])]
            cps += [copy(t, 1 + j, me, (*chip, c), src=x_refs[t]) for j, chip in enumerate(chips)]
            for cp in cps:
                cp.start()
            first += cps
        for j, chip in enumerate(chips):
            for t in range(nt):
                copy(t, 1 + j, (*chip, c), me).wait_recv()
                fw = copy(t, 4 + j, (*chip, c), sibling)
                fw.start()
                passed.append(fw)
        for t in range(nt):
            copy(t, 0, sibling, me).wait_recv()
            for j, chip in enumerate(chips):
                copy(t, 4 + j, (*chip, 1 - c), me).wait_recv()
        for cp in first + passed:
            cp.wait_send()
        for cp in mine:
            cp.wait()

    return pl.pallas_call(
        body, name=name, out_shape=[_sds((N_DEV,) + a.shape, a.dtype) for a in xs],
        in_specs=[_hbm_spec()] * nt, out_specs=[_hbm_spec()] * nt,
        scratch_shapes=[pltpu.SemaphoreType.DMA((7 * nt,)), pltpu.SemaphoreType.DMA((7 * nt,)),
                        pltpu.SemaphoreType.DMA((nt,))],
    )(*xs)


def _peers(x, y, c):
    out = []
    for k in range(1, N_DEV):
        px = 1 - x if k & 4 else x
        py = 1 - y if k & 2 else y
        pc = 1 - c if k & 1 else c
        out.append((px, py, pc))
    return out


def _exchange_copies(src_refs, land_refs, send_sems, recv_sems, scatter):
    x, y, c = lax.axis_index("x"), lax.axis_index("y"), lax.axis_index("c")
    me = 4 * x + 2 * y + c
    sends, recvs = [], []
    for t, (src, land) in enumerate(zip(src_refs, land_refs)):
        for k, (px, py, pc) in enumerate(_peers(x, y, c)):
            peer = 4 * px + 2 * py + pc
            sem = dict(send_sem=send_sems.at[7 * t + k], recv_sem=recv_sems.at[7 * t + k],
                       device_id=(px, py, pc), device_id_type=MESH)
            src_k = src.at[peer] if scatter else src
            sends.append(pltpu.make_async_remote_copy(src_ref=src_k, dst_ref=land.at[me], **sem))
            recvs.append(pltpu.make_async_remote_copy(src_ref=src_k, dst_ref=land.at[peer], **sem))
    return sends, recvs


def exchange_start(name, srcs, scatter, after, carry=None):
    nt = len(srcs)
    lands = [lax.empty(s.shape if scatter else (N_DEV,) + s.shape, s.dtype) for s in srcs]
    thru = list(srcs) + lands + ([carry] if carry is not None else [])
    n_thru = len(thru)

    def body(*refs):
        src_refs, land_refs = refs[:nt], refs[nt:2 * nt]
        send_sems, recv_sems = refs[n_thru + 1], refs[n_thru + 2]
        token = refs[-1]
        sends, _ = _exchange_copies(src_refs, land_refs, send_sems, recv_sems, scatter)
        for cp in sends:
            cp.start()
        token[...] = jnp.zeros_like(token)

    res = pl.pallas_call(
        body, name=name,
        out_shape=(pltpu.SemaphoreType.DMA((7 * nt,)), pltpu.SemaphoreType.DMA((7 * nt,)),
                   *[pltpu.HBM(a.shape, a.dtype) for a in thru], _sds((SUBLANES, LANES), F32)),
        in_specs=[_hbm_spec()] * n_thru + [pl.BlockSpec(memory_space=pl.ANY)],
        out_specs=(pl.BlockSpec(memory_space=pltpu.SEMAPHORE), pl.BlockSpec(memory_space=pltpu.SEMAPHORE),
                   *[_hbm_spec()] * n_thru, pl.BlockSpec(memory_space=pltpu.VMEM)),
        input_output_aliases={i: 2 + i for i in range(n_thru)},
        compiler_params=pltpu.CompilerParams(has_side_effects=pltpu.SideEffectType.DATAFLOW_SIDE_EFFECTING),
    )(*[pltpu.with_memory_space_constraint(a, pltpu.HBM) for a in thru], after)
    return (res[0], res[1], list(res[2:2 + nt]), list(res[2 + nt:2 + 2 * nt]), res[-1],
            res[2 + 2 * nt] if carry is not None else None)


def exchange_wait(name, started, after, scatter):
    send_sems, recv_sems, srcs, lands = started[:4]
    nt = len(srcs)

    def body(*refs):
        src_refs, land_refs = refs[:nt], refs[nt:2 * nt]
        sends, recvs = _exchange_copies(src_refs, land_refs, refs[2 * nt], refs[2 * nt + 1], scatter)
        for cp in sends:
            cp.wait_send()
        for cp in recvs:
            cp.wait_recv()

    hbm = lambda a: pltpu.HBM(a.shape, a.dtype)
    sem = pl.BlockSpec(memory_space=pltpu.SEMAPHORE)
    res = pl.pallas_call(
        body, name=name,
        out_shape=(*[hbm(a) for a in srcs], *[hbm(a) for a in lands]),
        in_specs=[_hbm_spec()] * (2 * nt) + [sem, sem, pl.BlockSpec(memory_space=pl.ANY)],
        out_specs=tuple([_hbm_spec()] * (2 * nt)),
        input_output_aliases={i: i for i in range(2 * nt)},
        compiler_params=pltpu.CompilerParams(has_side_effects=pltpu.SideEffectType.DATAFLOW_SIDE_EFFECTING),
    )(*srcs, *lands, send_sems, recv_sems, after)
    return list(res[:nt]), list(res[nt:])


def _pack(arrs, dtype, width, row_mult, lead=0):
    ld = arrs[0].shape[:lead]
    flat = jnp.concatenate([a.reshape(ld + (-1,)).astype(dtype) for a in arrs], axis=-1)
    n = flat.shape[-1]
    q = width * row_mult
    npad = -(-n // q) * q
    flat = jnp.pad(flat, [(0, 0)] * lead + [(0, npad - n)])
    return flat.reshape(ld + (npad // width, width))


def _unpack(buf, shapes, lead=0):
    ld = buf.shape[:lead]
    flat = buf.reshape(ld + (-1,))
    out, off = [], 0
    for s in shapes:
        n = int(np.prod(s))
        out.append(flat[..., off:off + n].reshape(ld + tuple(s)))
        off += n
    return out


def _pool_mats(seg_len):
    mats = np.zeros((len(POOL_WINDOWS), ROW_TILE, ROW_TILE), np.float32)
    for gi, win in enumerate(POOL_WINDOWS):
        for p in range(ROW_TILE):
            base = (p // seg_len) * seg_len
            q = p - base
            lo = min(max(q - win // 2, 0), seg_len - 1)
            hi = min(max(q + win - 1 - win // 2, 0), seg_len - 1)
            mats[gi, p, base + lo:base + hi + 1] = 1.0 / (hi - lo + 1)
    return mats


def kernel(x, c, ctx, c_ctx, w_ada, b_ada, norm_w, ev_w_in, ev_lb, ev_a_norm, ev_pool_w, ev_pool_scale, ev_w_out, od_w_in, od_conv, od_A_log, od_dt_bias, od_norm, od_w_out, ffn_w13, ffn_w2, loss_target, m_c_ctx, m_w_ada, m_b_ada, m_norm_w, m_ev_w_in, m_ev_lb, m_ev_a_norm, m_ev_pool_w, m_ev_pool_scale, m_ev_w_out, m_od_w_in, m_od_conv, m_od_A_log, m_od_dt_bias, m_od_norm, m_od_w_out, m_ffn_w13, m_ffn_w2, v_c_ctx, v_w_ada, v_b_ada, v_norm_w, v_ev_w_in, v_ev_lb, v_ev_a_norm, v_ev_pool_w, v_ev_pool_scale, v_ev_w_out, v_od_w_in, v_od_conv, v_od_A_log, v_od_dt_bias, v_od_norm, v_od_w_out, v_ffn_w13, v_ffn_w2):
    names = ["c_ctx", "w_ada", "b_ada", "norm_w", "ev_w_in", "ev_lb", "ev_a_norm", "ev_pool_w", "ev_pool_scale",
             "ev_w_out", "od_w_in", "od_conv", "od_A_log", "od_dt_bias", "od_norm", "od_w_out", "ffn_w13", "ffn_w2"]
    wts = dict(zip(names, [c_ctx, w_ada, b_ada, norm_w, ev_w_in, ev_lb, ev_a_norm, ev_pool_w, ev_pool_scale,
                           ev_w_out, od_w_in, od_conv, od_A_log, od_dt_bias, od_norm, od_w_out, ffn_w13, ffn_w2]))
    mom1 = dict(zip(names, [m_c_ctx, m_w_ada, m_b_ada, m_norm_w, m_ev_w_in, m_ev_lb, m_ev_a_norm, m_ev_pool_w,
                            m_ev_pool_scale, m_ev_w_out, m_od_w_in, m_od_conv, m_od_A_log, m_od_dt_bias, m_od_norm,
                            m_od_w_out, m_ffn_w13, m_ffn_w2]))
    mom2 = dict(zip(names, [v_c_ctx, v_w_ada, v_b_ada, v_norm_w, v_ev_w_in, v_ev_lb, v_ev_a_norm, v_ev_pool_w,
                            v_ev_pool_scale, v_ev_w_out, v_od_w_in, v_od_conv, v_od_A_log, v_od_dt_bias, v_od_norm,
                            v_od_w_out, v_ffn_w13, v_ffn_w2]))

    ax, ay, ac = lax.axis_index("x"), lax.axis_index("y"), lax.axis_index("c")
    me = 4 * ax + 2 * ay + ac

    seq, d = x.shape[1], x.shape[2]
    n_ctx = ctx.shape[1]
    t = n_ctx + seq
    nt = t // ROW_TILE
    nct = n_ctx // ROW_TILE
    assert n_ctx == ROW_TILE and seq % ROW_TILE == 0 and ROW_TILE % GRID_W == 0
    depth = w_ada.shape[0]
    aw = d // 2
    n_ah = aw // HEAD
    n_grp = len(POOL_WINDOWS)
    dg = aw // n_grp
    assert dg % LANES == 0
    n_kh = d // HEAD
    kw, vw = n_kh * HEAD, 2 * n_kh * HEAD
    n_gate = 8 * n_kh
    ffn_h = ffn_w2.shape[1] * N_DEV
    ada_loc = w_ada.shape[2]
    assert depth == 2

    bf = lambda w_: w_.astype(BF16)

    def gathered(started, after, name):
        srcs, lands = exchange_wait(name, started, after, False)
        return [lax.dynamic_update_index_in_dim(l_, s_, me, 0) for l_, s_ in zip(lands, srcs)]

    small_shapes = [(d,), norm_w.shape, ev_lb.shape, ev_pool_w.shape[1:], od_conv.shape[1:]]
    (g1,) = all_gather("ag_small", [_pack([c[0], norm_w, ev_lb, ev_pool_w[0], od_conv[0]], F32, LANES, SUBLANES)])
    c_all, nw_g, lb_g, pw_g, cv_g = _unpack(g1, small_shapes, lead=1)
    nw_full = nw_g.transpose(1, 2, 0, 3).reshape(depth, 4, d)
    lb_full = lb_g.transpose(1, 2, 0, 3).reshape(2, depth + 1, aw)
    pw_full = pw_g.transpose(1, 0, 2, 3).reshape(n_grp, dg, dg)
    cv_full = cv_g.transpose(1, 0, 2).reshape(C_CONV, 2 * kw + vw)

    g_ev_in, g_ev_out = all_gather("ag_weights_ev", [bf(ev_w_in[0]), bf(ev_w_out[0])])

    def cols_natural(g):
        return g.transpose(1, 0, 2).reshape(g.shape[1], N_DEV * g.shape[2])

    def rows_natural(g):
        return g.reshape(N_DEV * g.shape[1], g.shape[2])

    def col_weight(g):
        return (g, True) if g.shape[2] % LANES == 0 else (cols_natural(g), False)

    w_ev_in = col_weight(g_ev_in)
    w_ev_out = rows_natural(g_ev_out)
    w13, w2 = [None] * depth, [None] * depth

    b_loc = lax.dynamic_slice_in_dim(b_ada, me * ada_loc, ada_loc, axis=1).reshape(depth, 1, ada_loc)
    ada_cb = _pick(ada_loc, TN_PREFS)
    ada_grid = (depth, ada_loc // ada_cb)
    ada_args = [
        TArg(c_all, (N_DEV, d), lambda l, j: (0, 0), "const"),
        TArg(c_ctx.reshape(1, d), (1, d), lambda l, j: (0, 0), "par", (0, 1)),
        TArg(w_ada, (None, d, ada_cb), lambda l, j: (l, 0, j)),
        TArg(b_loc, (None, 1, ada_cb), lambda l, j: (l, 0, j)),
    ]
    (m_loc,) = tile_fwd("ada_fwd", f_ada, ada_grid, ada_args,
                        [((depth, 2 * N_DEV, ada_loc), F32, (None, 2 * N_DEV, ada_cb), lambda l, j: (l, 0, j))])
    (m_all,) = all_gather("ag_mod", [m_loc])
    mods = []
    for layer in range(depth):
        lat = lax.dynamic_index_in_dim(m_all[:, layer], me, axis=1, keepdims=False).reshape(6, d)
        cxt = lax.dynamic_index_in_dim(m_all[:, layer], N_DEV + me, axis=1, keepdims=False).reshape(6, d)
        mods.append(jnp.stack([cxt, lat]))

    gathers_done = mods[0][0, :1, :SUBLANES] + g_ev_out[0, :1, :SUBLANES].astype(F32)
    ag_ffn0 = exchange_start("ag_start_ffn0", [bf(ffn_w13[0]), bf(ffn_w2[0])], False, gathers_done)
    ag_l1 = exchange_start("ag_start_l1", [bf(od_w_in[0]), bf(od_w_out[0]), bf(ffn_w13[1]), bf(ffn_w2[1])], False,
                           gathers_done)
    mods[0] = mods[0] + (ag_ffn0[4][0, 0] + ag_l1[4][0, 0])

    lb_slots = [TArg(lb_full[:, j], (2, aw), lambda i: (0, 0)) for j in range(depth + 1)]
    (lb0,) = tile_fwd("lb_fwd", f_lb(0), (1,), lb_slots, [((2, aw), F32, (2, aw), lambda i: (0, 0))])
    lb0r = lb0.reshape(2, n_ah, 1, HEAD)

    full_row = lambda i: (i, 0)
    par0 = lambda i: (0, 0)

    def nm_args(xs, layer, slot):
        return [TArg(xs, (ROW_TILE, d), full_row),
                TArg(nw_full[layer, slot].reshape(1, d), (1, d), par0, "par", (0,)),
                TArg(mods[layer], (2, 6, d), lambda i: (0, 0, 0), "par", (0,))]

    def norm_mod(name, xs, layer, slot, si, ci):
        return tile_fwd(name, f_norm_mod(si, ci, nct, transposed=True), (nt,), nm_args(xs, layer, slot),
                        [((t, d), BF16, (ROW_TILE, d), full_row), ((d, t), BF16, (d, ROW_TILE), lambda i: (0, i))])

    def norm_mod_bwd(name, xs, layer, slot, si, ci, dh, carry):
        return tile_bwd(name, f_norm_mod(si, ci, nct, True), (nt,), nm_args(xs, layer, slot),
                        [(dh, (ROW_TILE, d), full_row), (carry, (ROW_TILE, d), full_row)])

    def gr_args(xs, ys, layer, slot):
        return [TArg(xs, (ROW_TILE, d), full_row, grad=False), TArg(ys, (ROW_TILE, d), full_row, gdtype=BF16),
                TArg(nw_full[layer, slot].reshape(1, d), (1, d), par0, "par", (0,)),
                TArg(mods[layer], (2, 6, d), lambda i: (0, 0, 0), "par", (0,))]

    def gate_res(name, xs, ys, layer, slot, gi):
        (o,) = tile_fwd(name, f_gate_res(gi, nct), (nt,), gr_args(xs, ys, layer, slot),
                        [((t, d), F32, (ROW_TILE, d), full_row)])
        return o

    def gate_res_bwd(name, xs, ys, layer, slot, gi, dx):
        return tile_bwd(name, f_gate_res(gi, nct), (nt,), gr_args(xs, ys, layer, slot),
                        [(dx, (ROW_TILE, d), full_row)])

    sw_rows = ROW_TILE // 2

    def sw_args(gu):
        return [TArg(gu, (sw_rows, 2 * ffn_h), full_row, gdtype=BF16)]

    def ffn_fwd(tag, xs, layer):
        h2, h2t = norm_mod(f"nm2_{tag}", xs, layer, 2, 3, 4)
        gu = matmul(f"w13_{tag}", h2, w13[layer][0], "nn", slabs=w13[layer][1], out_dtype=BF16)
        (act,) = tile_fwd(f"swiglu_{tag}", f_swiglu, (t // sw_rows,), sw_args(gu),
                          [((t, ffn_h), BF16, (sw_rows, ffn_h), full_row)])
        fo = matmul(f"w2_{tag}", act, w2[layer], "nn")
        xn = gate_res(f"gr2_{tag}", xs, fo, layer, 3, 5)
        return xn, (xs, h2, h2t, gu, act, fo)

    def wgrad(name, a, dy, slabs=False, a_t=None):
        if a_t is not None:
            return matmul(name, a_t, dy, "nn", out_slabs=slabs, out_dtype=BF16)
        return matmul(name, a, dy, "tn", slabs=slabs, out_dtype=BF16)

    def col_grad(name, a, dy, slabs, a_t=None):
        g = wgrad(name, a, dy, slabs, a_t)
        return g if slabs else g.reshape(g.shape[0], N_DEV, g.shape[1] // N_DEV).transpose(1, 0, 2)

    def row_grad(name, a, dy):
        g = matmul(name, a, dy, "tn", out_dtype=BF16)
        return g.reshape(N_DEV, g.shape[0] // N_DEV, g.shape[1])

    def ffn_bwd(tag, saved, layer, dxn, acc):
        xs, h2, h2t, gu, act, fo = saved
        dfo, dnw3, dmod_a = gate_res_bwd(f"gr2b_{tag}", xs, fo, layer, 3, 5, dxn)
        dact = matmul(f"w2d_{tag}", dfo, w2[layer], "nt")
        dw2 = row_grad(f"w2w_{tag}", act, dfo)
        (dgu,) = tile_bwd(f"swiglub_{tag}", f_swiglu, (t // sw_rows,), sw_args(gu), [(dact, (sw_rows, ffn_h), full_row)])
        dh2 = matmul(f"w13d_{tag}", dgu, w13[layer][0], "nt", slabs=w13[layer][1])
        dw13 = col_grad(f"w13w_{tag}", h2, dgu, w13[layer][1], h2t)
        dxs, dnw2, dmod_b = norm_mod_bwd(f"nm2b_{tag}", xs, layer, 2, 3, 4, dh2, dxn)
        acc["ffn_w13"][layer] = dw13
        acc["ffn_w2"][layer] = dw2
        acc["norm_w"][layer][2] = dnw2
        acc["norm_w"][layer][3] = dnw3
        acc["mods"][layer].extend([dmod_a, dmod_b])
        return dxs

    n_a = t // A_CHUNK
    nca = n_ctx // A_CHUNK

    def a_tok(rev):
        if not rev:
            return lambda i: i
        return lambda i: jnp.where(i < nca, nca - 1 - i, n_a + nca - 1 - i)

    hb = _pick(n_ah, (HEADS_PER_STEP, 2, 1))
    n_hblk = n_ah // hb

    def hg_args(p, direction):
        tok = a_tok(direction == 1)
        blk = (A_CHUNK, hb * HEAD)
        return [TArg(p, blk, lambda h, i: (tok(i), h)),
                TArg(p, blk, lambda h, i: (tok(i), (1 + direction) * n_hblk + h)),
                TArg(p, blk, lambda h, i: (tok(i), 3 * n_hblk + h)),
                TArg(lb0r, (None, hb, 1, HEAD), lambda h, i: (direction, h, 0, 0), "par", (1,))]

    pmats = jnp.asarray(np.stack([_pool_mats(n_ctx), _pool_mats(GRID_W)]))

    def pool_args(p):
        return [TArg(p, (ROW_TILE, dg), lambda g, i: (i, 5 * n_grp + g)),
                TArg(pmats, (None, None, ROW_TILE, ROW_TILE), lambda g, i: (jnp.where(i < nct, 0, 1), g, 0, 0), "const"),
                TArg(pw_full, (None, dg, dg), lambda g, i: (g, 0, 0), "par", (1,)),
                TArg(ev_pool_scale, (1, dg), lambda g, i: (0, g), "par", (1,))]

    def ro_plan(gate_off, n_heads):
        per = _pick(n_heads, (8, 4, 2, 1))
        assert gate_off % per == 0
        return per, n_heads // per, gate_off // per

    def ro_args(o_f, o_b, gate_arr, gate_off, nw_arr, n_heads):
        per, _, goff = ro_plan(gate_off, n_heads)
        blk = (ROW_TILE, per * HEAD)
        return [TArg(o_f, blk, lambda h, i: (i, h)), TArg(o_b, blk, lambda h, i: (i, h), grad=False),
                TArg(gate_arr, blk, lambda h, i: (i, goff + h), gdtype=BF16),
                TArg(nw_arr, (1, HEAD), lambda h, i: (0, 0), "par", (0, 1))]

    def readout(name, o_f, o_b, gate_arr, gate_off, nw_arr, n_heads):
        per, nblk, _ = ro_plan(gate_off, n_heads)
        (o,) = tile_fwd(name, f_readout(per), (nblk, nt), ro_args(o_f, o_b, gate_arr, gate_off, nw_arr, n_heads),
                        [((t, n_heads * HEAD), BF16, (ROW_TILE, per * HEAD), lambda hh, i: (i, hh))])
        return o

    def readout_bwd(name, o_f, o_b, gate_arr, gate_off, nw_arr, n_heads, dout):
        per, nblk, _ = ro_plan(gate_off, n_heads)
        return tile_bwd(name, f_readout(per), (nblk, nt), ro_args(o_f, o_b, gate_arr, gate_off, nw_arr, n_heads),
                        [(dout, (ROW_TILE, per * HEAD), lambda hh, i: (i, hh))])

    def even_fwd(tag, xs, layer):
        h, ht = norm_mod(f"nm1_{tag}", xs, layer, 0, 0, 1)
        p = matmul(f"win_{tag}", h, w_ev_in[0], "nn", slabs=w_ev_in[1])
        outs, saves = [], []
        for direction in (0, 1):
            (o,), sv = scan_fwd(f"hgrn_{tag}_{direction}", f_hgrn2(direction == 1, hb), n_hblk, n_a, hg_args(p, direction),
                                [((t, aw), F32, (A_CHUNK, hb * HEAD), lambda hh, i, tok=a_tok(direction == 1): (tok(i), hh))], hb)
            outs.append(o)
            saves.append(sv)
        a_out = readout(f"ro_{tag}", outs[0], outs[1], p, 4 * n_ah, ev_a_norm, n_ah)
        (pooled,) = tile_fwd(f"pool_{tag}", f_pool, (n_grp, nt), pool_args(p),
                             [((t, aw), BF16, (ROW_TILE, dg), lambda g, i: (i, g))])
        cat = assemble(f"cat_{tag}", [[(a_out, aw, 0)], [(pooled, aw, 0)]], BF16)
        y = matmul(f"wout_{tag}", cat, w_ev_out, "nn")
        xn = gate_res(f"gr1_{tag}", xs, y, layer, 1, 2)
        return xn, (xs, h, ht, p, outs, saves, cat, y)

    def even_bwd(tag, saved, layer, dxn, acc):
        xs, h, ht, p, outs, saves, cat, y = saved
        dy, dnw1, dmod_a = gate_res_bwd(f"gr1b_{tag}", xs, y, layer, 1, 2, dxn)
        dcat = matmul(f"woutd_{tag}", dy, w_ev_out, "nt")
        acc["ev_w_out"] = row_grad(f"woutw_{tag}", cat, dy)
        do, dgate, d_anorm = readout_bwd(f"rob_{tag}", outs[0], outs[1], p, 4 * n_ah, ev_a_norm, n_ah, dcat)
        du, d_pw, d_ps = tile_bwd(f"poolb_{tag}", f_pool, (n_grp, nt), pool_args(p),
                                  [(dcat, (ROW_TILE, dg), lambda g, i: (i, n_grp + g))])
        dq, df, di, dlb = [], [], [], []
        for direction in (0, 1):
            r = scan_bwd(f"hgrnb_{tag}_{direction}", f_hgrn2(direction == 1, hb), n_hblk, n_a, hg_args(p, direction),
                         saves[direction],
                         [(do, (A_CHUNK, hb * HEAD), lambda hh, i, tok=a_tok(direction == 1): (tok(i), hh))])
            dq.append(r[0])
            df.append(r[1])
            di.append(r[2])
            dlb.append(r[3])
        sec = lambda arr, s: (arr, aw, s)
        dp = assemble(f"dp_{tag}", [[sec(dq[0], 0), sec(dq[1], 0)], [sec(df[0], 1)], [sec(df[1], 2)],
                                    [sec(di[0], 3), sec(di[1], 3)], [sec(dgate, 4)], [sec(du, 5)]], BF16)
        acc["ev_w_in"] = col_grad(f"winw_{tag}", h, dp, w_ev_in[1], ht)
        acc["rs_ev"] = exchange_start("rs_start_ev", [acc["ev_w_in"], acc["ev_w_out"]], True, dp, carry=w_ev_in[0])
        dh = matmul(f"wind_{tag}", dp, acc["rs_ev"][5], "nt", slabs=w_ev_in[1])
        dxs, dnw0, dmod_b = norm_mod_bwd(f"nm1b_{tag}", xs, layer, 0, 0, 1, dh, dxn)
        acc["norm_w"][layer][0] = dnw0
        acc["norm_w"][layer][1] = dnw1
        acc["mods"][layer].extend([dmod_a, dmod_b])
        acc["ev_a_norm"] = d_anorm
        acc["ev_pool_w"] = d_pw
        acc["ev_pool_scale"] = d_ps
        acc["lb0"] = jnp.stack([dlb[0][0], dlb[1][1]]).reshape(2, aw)
        return dxs

    n_c = t // C_CHUNK
    ncc = n_ctx // C_CHUNK

    def c_tok(rev):
        if not rev:
            return lambda i: i
        return lambda i: jnp.where(i < ncc, ncc - 1 - i, n_c + ncc - 1 - i)

    alog = od_A_log[0].reshape(2, n_kh, 2, 1)
    dtb = od_dt_bias[0].reshape(2, n_kh, 2, 1)

    khb = _pick(n_kh, (HEADS_PER_STEP, 2, 1))
    n_kblk = n_kh // khb

    def gd_args(z, gates, direction):
        tok = c_tok(direction == 1)
        gblk = (None, khb, None, 2, C_CHUNK)
        sblk = (None, khb, 2, 1)
        return [TArg(z, (C_CHUNK, khb * HEAD), lambda kb, i: (tok(i), kb)),
                TArg(z, (C_CHUNK, khb * HEAD), lambda kb, i: (tok(i), n_kblk + kb)),
                TArg(z, (C_CHUNK, khb * 2 * HEAD), lambda kb, i: (tok(i), n_kblk + kb)),
                TArg(gates, gblk, lambda kb, i: (direction, kb, tok(i), 0, 0)),
                TArg(gates, gblk, lambda kb, i: (2 + direction, kb, tok(i), 0, 0)),
                TArg(alog, sblk, lambda kb, i: (direction, kb, 0, 0), "par", (1,)),
                TArg(dtb, sblk, lambda kb, i: (direction, kb, 0, 0), "par", (1,))]

    def odd_fwd(tag, xs, layer):
        h, ht = norm_mod(f"nm1_{tag}", xs, layer, 0, 0, 1)
        pm = matmul(f"win_{tag}", h, w_od_main, "nn")
        pg = matmul(f"wgate_{tag}", h, w_od_gate, "nn")
        z = conv_fwd(f"conv_{tag}", pm, cv_full, 2 * kw + vw, nct)
        gates = pg.reshape(n_c, C_CHUNK, 4, n_kh, 2).transpose(2, 3, 0, 4, 1)
        outs, saves = [], []
        for direction in (0, 1):
            xrows = 2 * khb * C_CHUNK
            (o, xinv), sv = scan_fwd(
                f"gdn_{tag}_{direction}", f_gdn(direction == 1, khb, False), n_kblk, n_c, gd_args(z, gates, direction),
                [((t, vw), F32, (C_CHUNK, khb * 2 * HEAD), lambda kb, i, tok=c_tok(direction == 1): (tok(i), kb)),
                 ((n_kblk, n_c, xrows, C_CHUNK), F32, (None, None, xrows, C_CHUNK), lambda kb, i: (kb, i, 0, 0))],
                2 * khb)
            outs.append(o)
            saves.append((sv, xinv))
        n_vh = 2 * n_kh
        yo = readout(f"ro_{tag}", outs[0], outs[1], pm, 2 * n_kh + n_vh, od_norm, n_vh)
        y = matmul(f"wout_{tag}", yo, w_od_out, "nn")
        xn = gate_res(f"gr1_{tag}", xs, y, layer, 1, 2)
        return xn, (xs, ht, pm, z, gates, outs, saves, yo, y)

    def odd_bwd(tag, saved, layer, dxn, acc):
        xs, ht, pm, z, gates, outs, saves, yo, y = saved
        n_vh = 2 * n_kh
        dy, dnw1, dmod_a = gate_res_bwd(f"gr1b_{tag}", xs, y, layer, 1, 2, dxn)
        dyo = matmul(f"woutd_{tag}", dy, w_od_out, "nt")
        acc["od_w_out"] = row_grad(f"woutw_{tag}", yo, dy)
        do, dzg, d_onorm = readout_bwd(f"rob_{tag}", outs[0], outs[1], pm, 2 * n_kh + n_vh, od_norm, n_vh, dyo)
        dq, dk, dv, dga, dgb, dal, ddt = [], [], [], [], [], [], []
        for direction in (0, 1):
            sv, xinv = saves[direction]
            xarg = TArg(xinv, (None, None, 2 * khb * C_CHUNK, C_CHUNK), lambda kb, i: (kb, i, 0, 0), "const")
            r = scan_bwd(f"gdnb_{tag}_{direction}", f_gdn(direction == 1, khb, True), n_kblk, n_c,
                         gd_args(z, gates, direction) + [xarg], sv,
                         [(do, (C_CHUNK, khb * 2 * HEAD), lambda kb, i, tok=c_tok(direction == 1): (tok(i), kb))])
            for lst, v_ in zip((dq, dk, dv, dga, dgb, dal, ddt), r):
                lst.append(v_)
        dz = assemble(f"dz_{tag}", [[(dq[0], kw, 0), (dq[1], kw, 0)], [(dk[0], kw, 1), (dk[1], kw, 1)],
                                    [(dv[0], vw, 1), (dv[1], vw, 1)]], F32)
        dpm, d_conv = conv_bwd(f"convb_{tag}", pm, cv_full, dz, 2 * kw + vw, nct, dzg)
        dgates = jnp.stack([dga[0][0], dga[1][1], dgb[0][2], dgb[1][3]])
        dpg = dgates.transpose(2, 4, 0, 1, 3).reshape(t, n_gate).astype(BF16)
        dh = matmul(f"wgated_{tag}", dpg, w_od_gate, "nt")
        dh = matmul(f"wind_{tag}", dpm, w_od_main, "nt", add=dh)
        dw_in = jnp.concatenate([wgrad(f"winw_{tag}", None, dpm, a_t=ht), wgrad(f"wgatew_{tag}", None, dpg, a_t=ht)],
                                axis=1)
        acc["od_w_in"] = dw_in.reshape(d, N_DEV, dw_in.shape[1] // N_DEV).transpose(1, 0, 2)
        dxs, dnw0, dmod_b = norm_mod_bwd(f"nm1b_{tag}", xs, layer, 0, 0, 1, dh, dxn)
        acc["norm_w"][layer][0] = dnw0
        acc["norm_w"][layer][1] = dnw1
        acc["mods"][layer].extend([dmod_a, dmod_b])
        acc["od_norm"] = d_onorm
        acc["od_conv"] = d_conv
        acc["od_A_log"] = jnp.stack([dal[0][0], dal[1][1]]).reshape(1, 2, n_vh)
        acc["od_dt_bias"] = jnp.stack([ddt[0][0], ddt[1][1]]).reshape(1, 2, n_vh)
        return dxs

    xs0 = jnp.concatenate([ctx[0], x[0]], axis=0)
    xs1, sv_e = even_fwd("l0", xs0, 0)
    g_w13a, g_w2a = gathered(ag_ffn0, xs1, "ag_wait_ffn0")
    w13[0], w2[0] = col_weight(g_w13a), rows_natural(g_w2a)
    xs2, sv_f0 = ffn_fwd("l0", xs1, 0)
    g_od_in, g_od_out, g_w13b, g_w2b = gathered(ag_l1, xs2, "ag_wait_l1")
    w_od_in = cols_natural(g_od_in)
    w_od_main, w_od_gate = w_od_in[:, :2 * kw + 2 * vw], w_od_in[:, 2 * kw + 2 * vw:]
    w_od_out = rows_natural(g_od_out)
    w13[1], w2[1] = col_weight(g_w13b), rows_natural(g_w2b)
    xs3, sv_o = odd_fwd("l1", xs2, 1)
    xs4, sv_f1 = ffn_fwd("l1", xs3, 1)
    loss_loc, dxs = loss_kernel("loss", xs4, loss_target[0], nct)
    loss = lax.psum(loss_loc, ("x", "y", "c"))

    acc = {"norm_w": [[None] * 4 for _ in range(depth)], "mods": [[] for _ in range(depth)],
           "ffn_w13": [None] * depth, "ffn_w2": [None] * depth}
    dxs = ffn_bwd("l1", sv_f1, 1, dxs, acc)
    rs_ffn1 = exchange_start("rs_start_ffn1", [acc["ffn_w13"][1], acc["ffn_w2"][1]], True, dxs)
    mods[1] = mods[1] + rs_ffn1[4][0, 0]
    dxs = odd_bwd("l1", sv_o, 1, dxs, acc)
    rs_od = exchange_start("rs_start_od", [acc["od_w_in"], acc["od_w_out"]], True, dxs)
    mods[0] = mods[0] + rs_od[4][0, 0]
    dxs = ffn_bwd("l0", sv_f0, 0, dxs, acc)
    rs_ffn0 = exchange_start("rs_start_ffn0", [acc["ffn_w13"][0], acc["ffn_w2"][0]], True, dxs)
    mods[0] = mods[0] + rs_ffn0[4][0, 0]
    dxs = even_bwd("l0", sv_e, 0, dxs, acc)
    rs_ev = acc["rs_ev"]
    grad_x = dxs[n_ctx:].reshape(1, seq, d)

    (d_lb_slots) = tile_bwd("lb_bwd", f_lb(0), (1,), lb_slots, [(acc["lb0"], (2, aw), lambda i: (0, 0))])
    d_ev_lb = jnp.stack(d_lb_slots, axis=1)

    dmods = jnp.stack([functools.reduce(jnp.add, acc["mods"][layer]) for layer in range(depth)])
    (dm_all,) = all_gather("ag_dmod", [dmods.reshape(depth * 2 * 6, d)])
    dm_all = dm_all.reshape(N_DEV, depth, 2, 6 * d)
    dm_cols = lax.dynamic_slice_in_dim(dm_all, me * ada_loc, ada_loc, axis=3)
    dm_loc = jnp.concatenate([dm_cols[:, :, 1].transpose(1, 0, 2), dm_cols[:, :, 0].transpose(1, 0, 2)], axis=1)
    d_cctx_part, d_w_ada, d_b_loc = tile_bwd("ada_bwd", f_ada, ada_grid, ada_args,
                                             [(dm_loc, (None, 2 * N_DEV, ada_cb), lambda l, j: (l, 0, j))])

    d_b_full = lax.dynamic_update_slice_in_dim(jnp.zeros_like(b_ada), d_b_loc.reshape(depth, ada_loc), me * ada_loc, axis=1)
    d_nw = jnp.stack([jnp.stack([acc["norm_w"][layer][s].reshape(d) for s in range(4)]) for layer in range(depth)])
    small_grads = [d_cctx_part.reshape(d), d_b_full, d_nw, d_ev_lb, acc["ev_a_norm"], acc["ev_pool_w"],
                   acc["ev_pool_scale"], acc["od_conv"], acc["od_A_log"], acc["od_dt_bias"], acc["od_norm"]]
    sg_shapes = [a.shape for a in small_grads]
    (sg,) = all_gather("ag_smallgrads", [_pack(small_grads, F32, FLAT_W, SUBLANES)])
    sg_sum = sum_leading("sum_smallgrads", sg, F32)
    (g_cctx, g_bada, g_nw, g_lb, g_anorm, g_pw, g_ps, g_conv, g_alog, g_dtb, g_onorm) = _unpack(sg_sum, sg_shapes)

    def my_cols(full, axis):
        loc = full.shape[axis] // N_DEV
        return lax.dynamic_slice_in_dim(full, me * loc, loc, axis=axis)

    grads = {
        "c_ctx": g_cctx, "w_ada": d_w_ada, "b_ada": g_bada, "norm_w": my_cols(g_nw, 2), "ev_lb": my_cols(g_lb, 2),
        "ev_a_norm": g_anorm, "ev_pool_w": my_cols(g_pw, 1)[None], "ev_pool_scale": g_ps,
        "od_conv": my_cols(g_conv, 1)[None], "od_A_log": g_alog, "od_dt_bias": g_dtb, "od_norm": g_onorm,
    }

    def reduced(started, tags_, name):
        srcs, lands = exchange_wait(name, started, sg_sum, True)
        out = []
        for tg, s_, l_ in zip(tags_, srcs, lands):
            own = lax.dynamic_index_in_dim(s_, me, 0, keepdims=True)
            out.append(sum_leading(f"rs_sum_{tg}", lax.dynamic_update_slice_in_dim(l_, own, me, 0), F32))
        return out

    g_w13b, g_w2b = reduced(rs_ffn1, ["w13b", "w2b"], "rs_wait_ffn1")
    g_od_in, g_od_out = reduced(rs_od, ["od_in", "od_out"], "rs_wait_od")
    g_w13a, g_w2a = reduced(rs_ffn0, ["w13a", "w2a"], "rs_wait_ffn0")
    g_ev_in, g_ev_out = reduced(rs_ev, ["ev_in", "ev_out"], "rs_wait_ev")
    grads["ev_w_in"], grads["ev_w_out"], grads["od_w_in"], grads["od_w_out"] = (g_ev_in[None], g_ev_out[None],
                                                                                g_od_in[None], g_od_out[None])
    grads["ffn_w13"] = jnp.stack([g_w13a, g_w13b])
    grads["ffn_w2"] = jnp.stack([g_w2a, g_w2b])

    big_names = ["w_ada", "ev_w_in", "ev_w_out", "od_w_in", "od_w_out", "ffn_w13", "ffn_w2"]
    small_names = [n_ for n_ in names if n_ not in big_names]
    gl = {n_: grads[n_].reshape(wts[n_].shape) for n_ in names}
    delta, new_m, new_v = {}, {}, {}
    for n_ in big_names:
        shp = wts[n_].shape
        res = adamw(f"adamw_{n_}", _rows2d(gl[n_]), _rows2d(wts[n_]), _rows2d(mom1[n_]), _rows2d(mom2[n_]))
        delta[n_], new_m[n_], new_v[n_] = (r_.reshape(shp) for r_ in res)
    shapes = [wts[n_].shape for n_ in small_names]
    pk = lambda dct: _pack([dct[n_] for n_ in small_names], F32, FLAT_W, SUBLANES)
    res = adamw("adamw_small", pk(gl), pk(wts), pk(mom1), pk(mom2))
    for dct, r_ in zip((delta, new_m, new_v), res):
        for n_, a_ in zip(small_names, _unpack(r_, shapes)):
            dct[n_] = a_
    return (loss, grad_x, *[gl[n_] for n_ in names], *[delta[n_] for n_ in names], *[new_m[n_] for n_ in names],
            *[new_v[n_] for n_ in names])
```

```python
import functools
from typing import Any, NamedTuple

import numpy as np

import jax
import jax.numpy as jnp
from jax import lax
from jax.experimental import pallas as pl
from jax.experimental.pallas import tpu as pltpu

F32 = jnp.float32
BF16 = jnp.bfloat16
MESH = pl.DeviceIdType.MESH
N_DEV = 8

EPS = 1e-6
GRID_W = 64
HEAD = 128
A_CHUNK = 32
C_CHUNK = 64
C_CONV = 4
POOL_WINDOWS = (2, 4, 8, 16)
MASKED_EXPONENT = -1e30
ADAM_LR, ADAM_B1, ADAM_B2, ADAM_EPS, ADAM_WD, ADAM_STEP = 0.001, 0.9, 0.999, 1e-08, 0.01, 10

VMEM_LIMIT_BYTES = 56 * 1024 * 1024
LANES = 128
SUBLANES = 8
ROW_TILE = 256
FLAT_W = 1024
TM_PREFS = (1056, 1024, 768, 512, 256, 128, 64, 32, 16)
TN_PREFS = (768, 512, 1408, 256, 128)
TK_PREFS = (2048, 2816, 1408, 1024, 768, 512, 384, 256, 128)
TO_PREFS = (1024, 1408, 768, 704, 512, 384, 256, 128)
HEADS_PER_STEP = 8


def _pick(dim, prefs):
    for p in prefs:
        if p <= dim and dim % p == 0:
            return p
    return dim


def _cparams(ngrid):
    return pltpu.CompilerParams(dimension_semantics=("arbitrary",) * ngrid, vmem_limit_bytes=VMEM_LIMIT_BYTES)


def _sds(shape, dtype):
    return jax.ShapeDtypeStruct(tuple(shape), dtype)


def _split(x):
    hi = x.astype(BF16)
    return hi, (x - hi.astype(F32)).astype(BF16)


def _dot(a, b, ca, cb, hi):
    dims = (((ca,), (cb,)), ((), ()))
    dot = lambda u, v: lax.dot_general(u, v, dims, preferred_element_type=F32)
    if hi:
        (ah, al), (bh, bl) = _split(a.astype(F32)), _split(b.astype(F32))
        return dot(ah, bh) + (dot(ah, bl) + dot(al, bh))
    return dot(a.astype(BF16), b.astype(BF16))


@functools.partial(jax.custom_vjp, nondiff_argnums=(2, 3, 4))
def mm(a, b, ca=1, cb=0, hi=False):
    return _dot(a, b, ca, cb, hi)


def _mm_fwd(a, b, ca, cb, hi):
    return _dot(a, b, ca, cb, hi), (a, b)


HI_FWD, HI_BOTH = 1, 2


def _mm_bwd(ca, cb, hi, res, g):
    a, b = res
    bhi = hi == HI_BOTH
    da = _dot(g, b, 1, 1 - cb, bhi) if ca == 1 else _dot(b, g, 1 - cb, 1, bhi)
    db = _dot(a, g, 1 - ca, 0, bhi) if cb == 0 else _dot(g, a, 0, 1 - ca, bhi)
    return da, db


mm.defvjp(_mm_fwd, _mm_bwd)


def _iota2(n, m, axis):
    return lax.broadcasted_iota(jnp.int32, (n, m), axis)


class TArg(NamedTuple):
    arr: Any
    block: tuple
    imap: Any
    kind: str = "row"
    acc: tuple = ()
    gdtype: Any = F32
    grad: bool = True


def _load(ref):
    v = ref[...]
    return v.astype(F32) if jnp.issubdtype(v.dtype, jnp.floating) else v


def tile_fwd(name, f, grid, args, outs):
    n_in, ng = len(args), len(grid)

    def body(*refs):
        pids = tuple(pl.program_id(k) for k in range(ng))
        res = f(pids, *[_load(r) for r in refs[:n_in]])
        for r, v in zip(refs[n_in:], res):
            r[...] = v.astype(r.dtype)

    return pl.pallas_call(
        body, grid=grid, name=name,
        in_specs=[pl.BlockSpec(a.block, a.imap) for a in args],
        out_specs=[pl.BlockSpec(b, im) for (_, _, b, im) in outs],
        out_shape=[_sds(s, d) for (s, d, _, _) in outs],
        compiler_params=_cparams(ng),
    )(*[a.arr for a in args])


def _store_grads(args, diff, pids, g_refs, d):
    for k, gr, dv in zip(diff, g_refs, d):
        a = args[k]
        if a.kind == "row" or not a.acc:
            gr[...] = dv.astype(gr.dtype)
        else:
            first = pids[a.acc[0]] == 0
            for ax in a.acc[1:]:
                first = jnp.logical_and(first, pids[ax] == 0)

            @pl.when(first)
            def _(gr=gr, dv=dv):
                gr[...] = dv.astype(gr.dtype)

            @pl.when(jnp.logical_not(first))
            def _(gr=gr, dv=dv):
                gr[...] += dv.astype(gr.dtype)


def tile_bwd(name, f, grid, args, cts):
    n_in, n_ct, ng = len(args), len(cts), len(grid)
    diff = [k for k, a in enumerate(args) if a.kind != "const" and a.grad]

    def body(*refs):
        pids = tuple(pl.program_id(k) for k in range(ng))
        vals = [_load(r) for r in refs[:n_in]]

        def g(*dv):
            full = list(vals)
            for k, v in zip(diff, dv):
                full[k] = v
            return tuple(f(pids, *full))

        _, vjp = jax.vjp(g, *[vals[k] for k in diff])
        d = vjp(tuple(_load(r) for r in refs[n_in:n_in + n_ct]))
        _store_grads(args, diff, pids, refs[n_in + n_ct:], d)

    return pl.pallas_call(
        body, grid=grid, name=name,
        in_specs=[pl.BlockSpec(a.block, a.imap) for a in args] + [pl.BlockSpec(b, im) for (_, b, im) in cts],
        out_specs=[pl.BlockSpec(args[k].block, args[k].imap) for k in diff],
        out_shape=[_sds(args[k].arr.shape, args[k].gdtype) for k in diff],
        compiler_params=_cparams(ng),
    )(*[a.arr for a in args], *[c[0] for c in cts])


def scan_fwd(name, f, n_heads, n_steps, args, outs, n_state):
    n_in, n_out = len(args), len(outs)
    sblock = (None, None, HEAD, HEAD)

    def body(*refs):
        in_refs = refs[:n_in]
        out_refs = refs[n_in:n_in + n_out]
        save_refs = refs[n_in + n_out:n_in + n_out + n_state]
        s_refs = refs[n_in + n_out + n_state:]

        @pl.when(pl.program_id(1) == 0)
        def _():
            for s in s_refs:
                s[...] = jnp.zeros_like(s)

        states = tuple(s[...] for s in s_refs)
        for sv, s in zip(save_refs, states):
            sv[...] = s
        new_states, res = f(states, *[_load(r) for r in in_refs])
        for s, v in zip(s_refs, new_states):
            s[...] = v
        for r, v in zip(out_refs, res):
            r[...] = v.astype(r.dtype)

    res = pl.pallas_call(
        body, grid=(n_heads, n_steps), name=name,
        in_specs=[pl.BlockSpec(a.block, a.imap) for a in args],
        out_specs=[pl.BlockSpec(b, im) for (_, _, b, im) in outs]
        + [pl.BlockSpec(sblock, lambda h, i: (h, i, 0, 0))] * n_state,
        out_shape=[_sds(s, d) for (s, d, _, _) in outs] + [_sds((n_heads, n_steps, HEAD, HEAD), F32)] * n_state,
        scratch_shapes=[pltpu.VMEM((HEAD, HEAD), F32)] * n_state,
        compiler_params=_cparams(2),
    )(*[a.arr for a in args])
    return res[:n_out], res[n_out:]


def scan_bwd(name, f, n_heads, n_steps, args, saves, cts):
    n_in, n_ct, n_state = len(args), len(cts), len(saves)
    diff = [k for k, a in enumerate(args) if a.kind != "const" and a.grad]
    sblock = (None, None, HEAD, HEAD)

    def rv(im):
        return lambda h, i: im(h, n_steps - 1 - i)

    def body(*refs):
        in_refs = refs[:n_in]
        save_refs = refs[n_in:n_in + n_state]
        ct_refs = refs[n_in + n_state:n_in + n_state + n_ct]
        g_refs = refs[n_in + n_state + n_ct:n_in + n_state + n_ct + len(diff)]
        ds_refs = refs[n_in + n_state + n_ct + len(diff):]
        pids = (pl.program_id(0), pl.program_id(1))

        @pl.when(pids[1] == 0)
        def _():
            for s in ds_refs:
                s[...] = jnp.zeros_like(s)

        vals = [_load(r) for r in in_refs]

        def g(states, *dv):
            full = list(vals)
            for k, v in zip(diff, dv):
                full[k] = v
            new_states, res = f(states, *full)
            return tuple(new_states), tuple(res)

        _, vjp = jax.vjp(g, tuple(s[...] for s in save_refs), *[vals[k] for k in diff])
        d = vjp((tuple(s[...] for s in ds_refs), tuple(_load(r) for r in ct_refs)))
        for s, v in zip(ds_refs, d[0]):
            s[...] = v
        _store_grads(args, diff, pids, g_refs, d[1:])

    return pl.pallas_call(
        body, grid=(n_heads, n_steps), name=name,
        in_specs=[pl.BlockSpec(a.block, rv(a.imap)) for a in args]
        + [pl.BlockSpec(sblock, rv(lambda h, i: (h, i, 0, 0)))] * n_state
        + [pl.BlockSpec(b, rv(im)) for (_, b, im) in cts],
        out_specs=[pl.BlockSpec(args[k].block, rv(args[k].imap)) for k in diff],
        out_shape=[_sds(args[k].arr.shape, args[k].gdtype) for k in diff],
        scratch_shapes=[pltpu.VMEM((HEAD, HEAD), F32)] * n_state,
        compiler_params=_cparams(2),
    )(*[a.arr for a in args], *saves, *[c[0] for c in cts])


def matmul(name, a, b, mode, add=None, out_dtype=F32, slabs=False, out_slabs=False):
    o_spec = None
    if mode == "nn":
        m, k = a.shape
        ns = b.shape[2] if slabs else (b.shape[1] // N_DEV if out_slabs else b.shape[1])
        n = N_DEV * ns if (slabs or out_slabs) else ns
        to_m, to_n, tr = _pick(m, TM_PREFS), _pick(ns, TN_PREFS), _pick(k, TK_PREFS)
        nb = ns // to_n
        grid = (m // to_m, n // to_n, k // tr)
        a_spec = pl.BlockSpec((to_m, tr), lambda i, j, l: (i, l))
        if slabs:
            b_spec = pl.BlockSpec((None, tr, to_n), lambda i, j, l: (j // nb, l, j % nb))
        else:
            b_spec = pl.BlockSpec((tr, to_n), lambda i, j, l: (l, j))
        dims, oshape = (1, 0), (m, n)
        if out_slabs:
            o_spec = pl.BlockSpec((None, to_m, to_n), lambda i, j, l: (j // nb, i, j % nb))
            oshape = (N_DEV, m, ns)
    elif mode == "nt":
        m, n = a.shape
        k = b.shape[1] if slabs else b.shape[0]
        ns = n // N_DEV if slabs else n
        to_m, to_n, tr = _pick(m, TM_PREFS), _pick(k, TO_PREFS), _pick(ns, TK_PREFS)
        nb = ns // tr
        grid = (m // to_m, k // to_n, n // tr)
        a_spec = pl.BlockSpec((to_m, tr), lambda i, j, l: (i, l))
        if slabs:
            b_spec = pl.BlockSpec((None, to_n, tr), lambda i, j, l: (l // nb, j, l % nb))
        else:
            b_spec = pl.BlockSpec((to_n, tr), lambda i, j, l: (j, l))
        dims, oshape = (1, 1), (m, k)
    else:
        (t, k), n = a.shape, b.shape[1]
        ns = n // N_DEV if slabs else n
        to_m, to_n, tr = _pick(k, TO_PREFS), _pick(ns, (2048,) + TO_PREFS), _pick(t, TM_PREFS)
        nb = ns // to_n
        grid = (k // to_m, n // to_n, t // tr)
        a_spec = pl.BlockSpec((tr, to_m), lambda i, j, l: (l, i))
        b_spec = pl.BlockSpec((tr, to_n), lambda i, j, l: (l, j))
        dims, oshape = (0, 0), (k, n)
        if slabs:
            o_spec = pl.BlockSpec((None, to_m, to_n), lambda i, j, l: (j // nb, i, j % nb))
            oshape = (N_DEV, k, ns)
    n_red = grid[2]
    if o_spec is None:
        o_spec = pl.BlockSpec((to_m, to_n), lambda i, j, l: (i, j))
    has_add = add is not None

    def body(a_ref, b_ref, *rest):
        add_ref = rest[0] if has_add else None
        o_ref = rest[1] if has_add else rest[0]
        part = lax.dot_general(a_ref[...].astype(BF16), b_ref[...].astype(BF16),
                               (((dims[0],), (dims[1],)), ((), ())), preferred_element_type=F32)

        def finish(v):
            if has_add:
                v = v + add_ref[...]
            o_ref[...] = v.astype(o_ref.dtype)

        if n_red == 1:
            finish(part)
        else:
            acc = rest[-1]
            step = pl.program_id(2)

            @pl.when(step == 0)
            def _():
                acc[...] = part

            @pl.when(step > 0)
            def _():
                acc[...] += part

            @pl.when(step == n_red - 1)
            def _():
                finish(acc[...])

    return pl.pallas_call(
        body, grid=grid, name=name,
        in_specs=[a_spec, b_spec] + ([o_spec] if has_add else []),
        out_specs=o_spec, out_shape=_sds(oshape, out_dtype),
        scratch_shapes=[pltpu.VMEM((to_m, to_n), F32)] if n_red > 1 else [],
        compiler_params=_cparams(3),
    )(a, b, *([add] if has_add else []))


def assemble(name, pieces, out_dtype):
    flat = [s for piece in pieces for s in piece]
    t = flat[0][0].shape[0]
    widths = [piece[0][1] for piece in pieces]

    def body(*refs):
        o_ref, k, off = refs[-1], 0, 0
        for piece, w in zip(pieces, widths):
            v = refs[k][...].astype(F32)
            k += 1
            for _ in piece[1:]:
                v = v + refs[k][...].astype(F32)
                k += 1
            o_ref[:, off:off + w] = v.astype(o_ref.dtype)
            off += w

    tr = ROW_TILE // 2
    return pl.pallas_call(
        body, grid=(t // tr,), name=name,
        in_specs=[pl.BlockSpec((tr, w), functools.partial(lambda i, cb: (i, cb), cb=cb)) for (_, w, cb) in flat],
        out_specs=pl.BlockSpec((tr, sum(widths)), lambda i: (i, 0)),
        out_shape=_sds((t, sum(widths)), out_dtype),
        compiler_params=_cparams(1),
    )(*[s[0] for s in flat])


ELEM_ROWS = (128, 64, 32, 16, 8)


def _rows2d(a, lead=0):
    return a.reshape(a.shape[:lead] + (-1, a.shape[-1]))


def sum_leading(name, arr, out_dtype):
    k, rows, w = arr.shape
    tr = _pick(rows, ELEM_ROWS)

    def body(a_ref, o_ref):
        v = a_ref[0].astype(F32)
        for j in range(1, k):
            v = v + a_ref[j].astype(F32)
        o_ref[...] = v.astype(o_ref.dtype)

    return pl.pallas_call(
        body, grid=(rows // tr,), name=name,
        in_specs=[pl.BlockSpec((k, tr, w), lambda i: (0, i, 0))],
        out_specs=pl.BlockSpec((tr, w), lambda i: (i, 0)),
        out_shape=_sds((rows, w), out_dtype), compiler_params=_cparams(1),
    )(arr)


def adamw(name, g, w, m, v):
    rows, wd = g.shape
    tr = _pick(rows, ELEM_ROWS)

    def body(g_ref, w_ref, m_ref, v_ref, d_ref, nm_ref, nv_ref):
        gv = g_ref[...]
        mn = ADAM_B1 * m_ref[...] + (1.0 - ADAM_B1) * gv
        vn = ADAM_B2 * v_ref[...] + (1.0 - ADAM_B2) * jnp.square(gv)
        m_hat = mn / (1.0 - ADAM_B1 ** ADAM_STEP)
        v_hat = vn / (1.0 - ADAM_B2 ** ADAM_STEP)
        d_ref[...] = -ADAM_LR * (m_hat / (jnp.sqrt(v_hat) + ADAM_EPS) + ADAM_WD * w_ref[...])
        nm_ref[...] = mn
        nv_ref[...] = vn

    spec = pl.BlockSpec((tr, wd), lambda i: (i, 0))
    return pl.pallas_call(
        body, grid=(rows // tr,), name=name, in_specs=[spec] * 4, out_specs=[spec] * 3,
        out_shape=[_sds(g.shape, F32)] * 3, compiler_params=_cparams(1),
    )(g, w, m, v)


def loss_kernel(name, xs, target, n_ctx_tiles):
    t, d = xs.shape
    nt = t // ROW_TILE

    def body(x_ref, t_ref, dx_ref, l_ref):
        i = pl.program_id(0)
        is_lat = i >= n_ctx_tiles
        err = jnp.where(is_lat, x_ref[...] - t_ref[...], 0.0)
        dx_ref[...] = err / d
        part = 0.5 * jnp.sum(jnp.mean(jnp.square(err), axis=-1, keepdims=True), axis=0, keepdims=True)

        @pl.when(i == 0)
        def _():
            l_ref[...] = jnp.zeros_like(l_ref)

        l_ref[...] += jnp.broadcast_to(part, l_ref.shape)

    dx, l = pl.pallas_call(
        body, grid=(nt,), name=name,
        in_specs=[pl.BlockSpec((ROW_TILE, d), lambda i: (i, 0)),
                  pl.BlockSpec((ROW_TILE, d), lambda i: (jnp.maximum(i - n_ctx_tiles, 0), 0))],
        out_specs=[pl.BlockSpec((ROW_TILE, d), lambda i: (i, 0)), pl.BlockSpec((SUBLANES, LANES), lambda i: (0, 0))],
        out_shape=[_sds((t, d), F32), _sds((SUBLANES, LANES), F32)], compiler_params=_cparams(1),
    )(xs, target)
    return l[0, 0], dx


CONV_COLS = 2048
CONV_LEFT = C_CONV // 2


def _conv_halo_specs(t, n_ctx_tiles):
    nt = t // ROW_TILE
    per = ROW_TILE // SUBLANES
    cur = pl.BlockSpec((ROW_TILE, CONV_COLS), lambda j, i: (i, j))
    prev = pl.BlockSpec((SUBLANES, CONV_COLS), lambda j, i: (jnp.maximum(i * per - 1, 0), j))
    nxt = pl.BlockSpec((SUBLANES, CONV_COLS), lambda j, i: (jnp.minimum((i + 1) * per, nt * per - 1), j))
    return cur, prev, nxt


def _fill_ext(ext, prev_ref, cur_ref, next_ref, i, nt, n_ctx_tiles):
    has_prev = jnp.logical_and(i != 0, i != n_ctx_tiles)
    has_next = jnp.logical_and(i != n_ctx_tiles - 1, i != nt - 1)
    ext[0:SUBLANES, :] = jnp.where(has_prev, prev_ref[...], 0.0)
    ext[SUBLANES:SUBLANES + ROW_TILE, :] = cur_ref[...]
    ext[SUBLANES + ROW_TILE:, :] = jnp.where(has_next, next_ref[...], 0.0)


def conv_fwd(name, p, w, width, n_ctx_tiles):
    t = p.shape[0]
    nt = t // ROW_TILE
    cur, prev, nxt = _conv_halo_specs(t, n_ctx_tiles)

    def body(c_ref, p_ref, n_ref, w_ref, o_ref, ext):
        _fill_ext(ext, p_ref, c_ref, n_ref, pl.program_id(1), nt, n_ctx_tiles)
        acc = None
        for j in range(C_CONV):
            term = ext[pl.ds(SUBLANES + j - CONV_LEFT, ROW_TILE), :] * w_ref[j:j + 1, :]
            acc = term if acc is None else acc + term
        o_ref[...] = acc

    return pl.pallas_call(
        body, grid=(width // CONV_COLS, nt), name=name,
        in_specs=[cur, prev, nxt, pl.BlockSpec((C_CONV, CONV_COLS), lambda j, i: (0, j))],
        out_specs=cur, out_shape=_sds((t, width), F32),
        scratch_shapes=[pltpu.VMEM((ROW_TILE + 2 * SUBLANES, CONV_COLS), F32)],
        compiler_params=_cparams(2),
    )(p, p, p, w)


def conv_bwd(name, p, w, dz, width, n_ctx_tiles, into):
    t = p.shape[0]
    nt = t // ROW_TILE
    cur, prev, nxt = _conv_halo_specs(t, n_ctx_tiles)

    def body(c_ref, p_ref, n_ref, dc_ref, dp_ref, dn_ref, w_ref, into_ref, du_ref, dw_ref, ext, dext):
        i = pl.program_id(1)
        _fill_ext(ext, p_ref, c_ref, n_ref, i, nt, n_ctx_tiles)
        _fill_ext(dext, dp_ref, dc_ref, dn_ref, i, nt, n_ctx_tiles)
        dzc = dc_ref[...]
        @pl.when(i == 0)
        def _():
            dw_ref[...] = jnp.zeros_like(dw_ref)

        acc = None
        for j in range(C_CONV):
            term = dext[pl.ds(SUBLANES + CONV_LEFT - j, ROW_TILE), :] * w_ref[j:j + 1, :]
            acc = term if acc is None else acc + term
            dw_ref[j:j + 1, :] += jnp.sum(dzc * ext[pl.ds(SUBLANES + j - CONV_LEFT, ROW_TILE), :], axis=0, keepdims=True)
        du_ref[...] = acc.astype(du_ref.dtype)

    wspec = pl.BlockSpec((C_CONV, CONV_COLS), lambda j, i: (0, j))
    return pl.pallas_call(
        body, grid=(width // CONV_COLS, nt), name=name,
        in_specs=[cur, prev, nxt, cur, prev, nxt, wspec, pl.BlockSpec(memory_space=pl.ANY)],
        out_specs=[cur, wspec], out_shape=[_sds(into.shape, into.dtype), _sds((C_CONV, width), F32)],
        input_output_aliases={7: 0},
        scratch_shapes=[pltpu.VMEM((ROW_TILE + 2 * SUBLANES, CONV_COLS), F32)] * 2,
        compiler_params=_cparams(2),
    )(p, p, p, dz, dz, dz, w, into)


def _rms(x, w):
    return x * lax.rsqrt(jnp.mean(x * x, axis=-1, keepdims=True) + EPS) * w


def _seg_mod(mods, is_ctx):
    return jnp.where(is_ctx, mods[0], mods[1])


def f_norm_mod(shift_i, scale_i, n_ctx_tiles, passthrough=False, transposed=False):
    def f(pids, x, nw, mods):
        m = _seg_mod(mods, pids[0] < n_ctx_tiles)
        h = _rms(x, nw) * (1.0 + m[scale_i:scale_i + 1]) + m[shift_i:shift_i + 1]
        if transposed:
            return (h, h.T)
        return (h, x) if passthrough else (h,)
    return f


def f_gate_res(gate_i, n_ctx_tiles):
    def f(pids, x, y, nw, mods):
        m = _seg_mod(mods, pids[0] < n_ctx_tiles)
        return (x + m[gate_i:gate_i + 1] * _rms(y, nw),)
    return f


def f_swiglu(pids, gu):
    half = gu.shape[1] // 2
    return (jax.nn.silu(gu[:, :half]) * gu[:, half:],)


def f_readout(n_heads):
    def f(pids, o_a, o_b, gate, nw):
        cols = [slice(j * HEAD, (j + 1) * HEAD) for j in range(n_heads)]
        outs = _each(lambda cs: _rms(o_a[:, cs] + o_b[:, cs], nw) * jax.nn.silu(gate[:, cs]), cols)
        return (jnp.concatenate(outs, axis=1) if n_heads > 1 else outs[0],)
    return f


def f_pool(pids, u, pmat, pw, scale):
    d = mm(pmat, u, 1, 0, HI_BOTH) - u
    return (mm(d, pw) * scale,)


def f_lb(layer):
    def f(pids, *slots):
        top = slots[0]
        for s in slots[1:]:
            top = jnp.maximum(top, s)
        ex = [jnp.exp(s - top) for s in slots]
        tot = ex[0]
        for e in ex[1:]:
            tot = tot + e
        part = ex[0]
        for e in ex[1:layer + 1]:
            part = part + e
        return (part / tot,)
    return f


def f_ada(pids, c_all, c_ctx, w, b):
    c16 = jnp.concatenate([c_all, jnp.broadcast_to(c_ctx, c_all.shape)], axis=0)
    return (mm(jax.nn.silu(c16), w) + b,)


def _each(fn, *lists):
    return [fn(*xs) for xs in zip(*lists)]


def _hgrn2_heads(sts, qrs, frs, irs, lbs, rev):
    c = A_CHUNK
    mid = c - c // 2 if rev else c // 2 - 1
    ri, ci = _iota2(c, c, 0), _iota2(c, c, 1)
    incl = (ri <= ci) if rev else (ri >= ci)
    incl_f = incl.astype(F32)
    qs = _each(jax.nn.silu, qrs)
    log_fs = _each(lambda lb, fr: jnp.log(lb + (1.0 - lb) * jax.nn.sigmoid(fr)), lbs, frs)
    ks = _each(lambda lb, fr: (1.0 - lb) * jax.nn.sigmoid(-fr), lbs, frs)
    bs = _each(lambda lf: mm(incl_f, lf, 1, 0, HI_BOTH), log_fs)
    b_lasts = _each(lambda lf: jnp.sum(lf, axis=0, keepdims=True), log_fs)
    scores = _each(lambda q, k, b: mm(q * jnp.exp(b - b[mid:mid + 1]), k * jnp.exp(b[mid:mid + 1] - b), 1, 1), qs, ks, bs)
    intra = _each(lambda sc, ir: mm(jnp.where(incl, sc, 0.0), ir), scores, irs)
    inter = _each(lambda q, b, st: mm(q * jnp.exp(b), st, 1, 1), qs, bs, sts)
    upd = _each(lambda ir, k, bl, b: mm(ir, k * jnp.exp(bl - b), 0, 0), irs, ks, b_lasts, bs)
    new = _each(lambda st, bl, u: st * jnp.exp(bl) + u, sts, b_lasts, upd)
    return new, _each(jnp.add, intra, inter)


def f_hgrn2(rev, hb):
    def f(states, qr, fr, ir, lb):
        cols = [slice(j * HEAD, (j + 1) * HEAD) for j in range(hb)]
        new, outs = _hgrn2_heads(list(states), [qr[:, cs] for cs in cols], [fr[:, cs] for cs in cols],
                                 [ir[:, cs] for cs in cols], [lb[j] for j in range(hb)], rev)
        return tuple(new), (jnp.concatenate(outs, axis=1) if hb > 1 else outs[0],)
    return f


def _neumann_inv(a_lows):
    n = a_lows[0].shape[0]
    eye = (_iota2(n, n, 0) == _iota2(n, n, 1)).astype(F32)
    ps = _each(lambda a: -a, a_lows)
    xs = _each(lambda p: eye + p, ps)
    ps = _each(lambda p: mm(p, p, 1, 0, HI_FWD), ps)
    k = 2
    while 2 * k < n:
        ys = _each(lambda p, x: mm(jnp.concatenate([p, x], axis=0), p, 1, 0, HI_FWD), ps, xs)
        ps = _each(lambda y: y[:n], ys)
        xs = _each(lambda x, y: x + y[n:], xs, ys)
        k *= 2
    return tuple(_each(lambda x, p: x + mm(x, p, 1, 0, HI_FWD), xs, ps))


@jax.custom_vjp
def unit_tri_inv(a_lows):
    return _neumann_inv(a_lows)


def _uti_fwd(a_lows):
    xs = _neumann_inv(a_lows)
    return xs, xs


def _uti_bwd(xs, gs):
    ts = _each(lambda x, g: mm(x, g, 0, 0), xs, gs)
    return (tuple(_each(lambda t, x: -mm(t, x, 1, 1), ts, xs)),)


unit_tri_inv.defvjp(_uti_fwd, _uti_bwd)


@jax.custom_vjp
def unit_tri_inv_saved(a_lows, xs):
    return xs


def _utis_fwd(a_lows, xs):
    return xs, xs


def _utis_bwd(xs, gs):
    return _uti_bwd(xs, gs) + (tuple(jnp.zeros_like(x) for x in xs),)


unit_tri_inv_saved.defvjp(_utis_fwd, _utis_bwd)


def _l2n(x):
    return x * lax.rsqrt(jnp.sum(x * x, axis=-1, keepdims=True) + EPS)


def _gdn_heads(ss, qs, ks, vs, a_rows, b_rows, alogs, dtbs, rev, xs_saved=None):
    c = qs[0].shape[0]
    ri, ci = _iota2(c, c, 0), _iota2(c, c, 1)
    causal = (ri <= ci) if rev else (ri >= ci)
    causal_t = (ri >= ci) if rev else (ri <= ci)
    strict = (ri < ci) if rev else (ri > ci)
    eye = ri == ci
    sq = lambda row: jnp.broadcast_to(row, (c, c))
    to_col = lambda row: jnp.sum(jnp.where(eye, sq(row), 0.0), axis=1, keepdims=True)
    g_rows = _each(lambda al, a, dt: -jnp.exp(al) * jax.nn.softplus(a + dt), alogs, a_rows, dtbs)
    beta_cols = _each(lambda b: to_col(jax.nn.sigmoid(b)), b_rows)
    g_cols = _each(to_col, g_rows)
    gc_cols = _each(lambda g: jnp.sum(jnp.where(causal, sq(g), 0.0), axis=1, keepdims=True), g_rows)
    gc_rows = _each(lambda g: jnp.sum(jnp.where(causal_t, sq(g), 0.0), axis=0, keepdims=True), g_cols)
    gc_lasts = _each(lambda g: jnp.sum(g, axis=1, keepdims=True), g_rows)
    decays = _each(lambda gc, gr: jnp.exp(jnp.where(causal, gc - gr, MASKED_EXPONENT)), gc_cols, gc_rows)
    k_betas = _each(jnp.multiply, ks, beta_cols)
    v_betas = _each(jnp.multiply, vs, beta_cols)
    kq_ks = _each(lambda kb, q, k: mm(jnp.concatenate([kb, q], axis=0), k, 1, 1), k_betas, qs, ks)
    a_lows = _each(lambda kk, dec: jnp.where(strict, kk[:c] * dec, 0.0), kq_ks, decays)
    qks = _each(lambda kk, dec: kk[c:] * dec, kq_ks, decays)
    xs = unit_tri_inv(tuple(a_lows)) if xs_saved is None else unit_tri_inv_saved(tuple(a_lows), tuple(xs_saved))
    egcs = _each(jnp.exp, gc_cols)
    uws = _each(lambda x, vb, kb, e: mm(x, jnp.concatenate([vb, kb * e], axis=1), 1, 0, HI_FWD), xs, v_betas, k_betas, egcs)
    dv = vs[0].shape[1]
    wq_ss = _each(lambda uw, q, e, s: mm(jnp.concatenate([uw[:, dv:], q * e], axis=0), s), uws, qs, egcs, ss)
    v_news = _each(lambda uw, wq: uw[:, :dv] - wq[:c], uws, wq_ss)
    o_states = _each(lambda wq: wq[c:], wq_ss)
    o_locals = _each(mm, qks, v_news)
    upds = _each(lambda k, gl, gc, vn: mm(k * jnp.exp(gl - gc), vn, 0, 0), ks, gc_lasts, gc_cols, v_news)
    new = _each(lambda s, gl, u: s * jnp.exp(gl) + u, ss, gc_lasts, upds)
    return new, _each(jnp.add, o_states, o_locals), xs


def f_gdn(rev, khb, saved_inverse):
    def f(states, qr, kr, vr, a3, b3, alog3, dtb3, xcat=None):
        heads = [(j, r) for j in range(khb) for r in range(2)]
        cols = [slice(j * HEAD, (j + 1) * HEAD) for j in range(khb)]
        c = qr.shape[0]
        qk_ = _each(lambda cs: (_l2n(jax.nn.silu(qr[:, cs])) * (HEAD ** -0.5), _l2n(jax.nn.silu(kr[:, cs]))), cols)
        vs = [jax.nn.silu(vr[:, (2 * j + r) * HEAD:(2 * j + r + 1) * HEAD]) for j, r in heads]
        row = lambda arr3: [arr3[j][r:r + 1] for j, r in heads]
        xs_saved = [xcat[n * c:(n + 1) * c] for n in range(len(heads))] if saved_inverse else None
        new, outs, xs = _gdn_heads(list(states), [qk_[j][0] for j, _ in heads], [qk_[j][1] for j, _ in heads], vs,
                                   row(a3), row(b3), row(alog3), row(dtb3), rev, xs_saved)
        o = jnp.concatenate(outs, axis=1)
        return tuple(new), ((o,) if saved_inverse else (o, jnp.concatenate(xs, axis=0)))
    return f


def _hbm_spec():
    return pl.BlockSpec(memory_space=pltpu.HBM)


def all_gather(name, xs):
    nt = len(xs)

    def body(*refs):
        x_refs, out_refs = refs[:nt], refs[nt:2 * nt]
        send_sems, recv_sems, local_sems = refs[2 * nt:]
        x, y, c = lax.axis_index("x"), lax.axis_index("y"), lax.axis_index("c")
        me, sibling = (x, y, c), (x, y, 1 - c)
        chips = [(1 - x, y), (x, 1 - y), (1 - x, 1 - y)]

        def slab(t, px, py, pc):
            return out_refs[t].at[4 * px + 2 * py + pc]

        def copy(t, k, block, to, src=None):
            return pltpu.make_async_remote_copy(
                src_ref=slab(t, *block) if src is None else src, dst_ref=slab(t, *block),
                send_sem=send_sems.at[7 * t + k], recv_sem=recv_sems.at[7 * t + k], device_id=to, device_id_type=MESH)

        mine, first, passed = [], [], []
        for t in range(nt):
            mine.append(pltpu.make_async_copy(x_refs[t], slab(t, *me), local_sems.at[t]))
            mine[-1].start()
            cps = [copy(t, 0, me, sibling, src=x_refs[t])]
            cps += [copy(t, 1 + j, me, (*chip, c), src=x_refs[t]) for j, chip in enumerate(chips)]
            for cp in cps:
                cp.start()
            first += cps
        for j, chip in enumerate(chips):
            for t in range(nt):
                copy(t, 1 + j, (*chip, c), me).wait_recv()
                fw = copy(t, 4 + j, (*chip, c), sibling)
                fw.start()
                passed.append(fw)
        for t in range(nt):
            copy(t, 0, sibling, me).wait_recv()
            for j, chip in enumerate(chips):
                copy(t, 4 + j, (*chip, 1 - c), me).wait_recv()
        for cp in first + passed:
            cp.wait_send()
        for cp in mine:
            cp.wait()

    return pl.pallas_call(
        body, name=name, out_shape=[_sds((N_DEV,) + a.shape, a.dtype) for a in xs],
        in_specs=[_hbm_spec()] * nt, out_specs=[_hbm_spec()] * nt,
        scratch_shapes=[pltpu.SemaphoreType.DMA((7 * nt,)), pltpu.SemaphoreType.DMA((7 * nt,)),
                        pltpu.SemaphoreType.DMA((nt,))],
    )(*xs)


def _peers(x, y, c):
    out = []
    for k in range(1, N_DEV):
        px = 1 - x if k & 4 else x
        py = 1 - y if k & 2 else y
        pc = 1 - c if k & 1 else c
        out.append((px, py, pc))
    return out


def _exchange_copies(src_refs, land_refs, send_sems, recv_sems, scatter):
    x, y, c = lax.axis_index("x"), lax.axis_index("y"), lax.axis_index("c")
    me = 4 * x + 2 * y + c
    sends, recvs = [], []
    for t, (src, land) in enumerate(zip(src_refs, land_refs)):
        for k, (px, py, pc) in enumerate(_peers(x, y, c)):
            peer = 4 * px + 2 * py + pc
            sem = dict(send_sem=send_sems.at[7 * t + k], recv_sem=recv_sems.at[7 * t + k],
                       device_id=(px, py, pc), device_id_type=MESH)
            src_k = src.at[peer] if scatter else src
            sends.append(pltpu.make_async_remote_copy(src_ref=src_k, dst_ref=land.at[me], **sem))
            recvs.append(pltpu.make_async_remote_copy(src_ref=src_k, dst_ref=land.at[peer], **sem))
    return sends, recvs


def exchange_start(name, srcs, scatter, after, carry=None):
    nt = len(srcs)
    lands = [lax.empty(s.shape if scatter else (N_DEV,) + s.shape, s.dtype) for s in srcs]
    thru = list(srcs) + lands + ([carry] if carry is not None else [])
    n_thru = len(thru)

    def body(*refs):
        src_refs, land_refs = refs[:nt], refs[nt:2 * nt]
        send_sems, recv_sems = refs[n_thru + 1], refs[n_thru + 2]
        token = refs[-1]
        sends, _ = _exchange_copies(src_refs, land_refs, send_sems, recv_sems, scatter)
        for cp in sends:
            cp.start()
        token[...] = jnp.zeros_like(token)

    res = pl.pallas_call(
        body, name=name,
        out_shape=(pltpu.SemaphoreType.DMA((7 * nt,)), pltpu.SemaphoreType.DMA((7 * nt,)),
                   *[pltpu.HBM(a.shape, a.dtype) for a in thru], _sds((SUBLANES, LANES), F32)),
        in_specs=[_hbm_spec()] * n_thru + [pl.BlockSpec(memory_space=pl.ANY)],
        out_specs=(pl.BlockSpec(memory_space=pltpu.SEMAPHORE), pl.BlockSpec(memory_space=pltpu.SEMAPHORE),
                   *[_hbm_spec()] * n_thru, pl.BlockSpec(memory_space=pltpu.VMEM)),
        input_output_aliases={i: 2 + i for i in range(n_thru)},
        compiler_params=pltpu.CompilerParams(has_side_effects=pltpu.SideEffectType.DATAFLOW_SIDE_EFFECTING),
    )(*[pltpu.with_memory_space_constraint(a, pltpu.HBM) for a in thru], after)
    return (res[0], res[1], list(res[2:2 + nt]), list(res[2 + nt:2 + 2 * nt]), res[-1],
            res[2 + 2 * nt] if carry is not None else None)


def exchange_wait(name, started, after, scatter):
    send_sems, recv_sems, srcs, lands = started[:4]
    nt = len(srcs)

    def body(*refs):
        src_refs, land_refs = refs[:nt], refs[nt:2 * nt]
        sends, recvs = _exchange_copies(src_refs, land_refs, refs[2 * nt], refs[2 * nt + 1], scatter)
        for cp in sends:
            cp.wait_send()
        for cp in recvs:
            cp.wait_recv()

    hbm = lambda a: pltpu.HBM(a.shape, a.dtype)
    sem = pl.BlockSpec(memory_space=pltpu.SEMAPHORE)
    res = pl.pallas_call(
        body, name=name,
        out_shape=(*[hbm(a) for a in srcs], *[hbm(a) for a in lands]),
        in_specs=[_hbm_spec()] * (2 * nt) + [sem, sem, pl.BlockSpec(memory_space=pl.ANY)],
        out_specs=tuple([_hbm_spec()] * (2 * nt)),
        input_output_aliases={i: i for i in range(2 * nt)},
        compiler_params=pltpu.CompilerParams(has_side_effects=pltpu.SideEffectType.DATAFLOW_SIDE_EFFECTING),
    )(*srcs, *lands, send_sems, recv_sems, after)
    return list(res[:nt]), list(res[nt:])


def _pack(arrs, dtype, width, row_mult, lead=0):
    ld = arrs[0].shape[:lead]
    flat = jnp.concatenate([a.reshape(ld + (-1,)).astype(dtype) for a in arrs], axis=-1)
    n = flat.shape[-1]
    q = width * row_mult
    npad = -(-n // q) * q
    flat = jnp.pad(flat, [(0, 0)] * lead + [(0, npad - n)])
    return flat.reshape(ld + (npad // width, width))


def _unpack(buf, shapes, lead=0):
    ld = buf.shape[:lead]
    flat = buf.reshape(ld + (-1,))
    out, off = [], 0
    for s in shapes:
        n = int(np.prod(s))
        out.append(flat[..., off:off + n].reshape(ld + tuple(s)))
        off += n
    return out


def _pool_mats(seg_len):
    mats = np.zeros((len(POOL_WINDOWS), ROW_TILE, ROW_TILE), np.float32)
    for gi, win in enumerate(POOL_WINDOWS):
        for p in range(ROW_TILE):
            base = (p // seg_len) * seg_len
            q = p - base
            lo = min(max(q - win // 2, 0), seg_len - 1)
            hi = min(max(q + win - 1 - win // 2, 0), seg_len - 1)
            mats[gi, p, base + lo:base + hi + 1] = 1.0 / (hi - lo + 1)
    return mats


def kernel(x, c, ctx, c_ctx, w_ada, b_ada, norm_w, ev_w_in, ev_lb, ev_a_norm, ev_pool_w, ev_pool_scale, ev_w_out, od_w_in, od_conv, od_A_log, od_dt_bias, od_norm, od_w_out, ffn_w13, ffn_w2, loss_target, m_c_ctx, m_w_ada, m_b_ada, m_norm_w, m_ev_w_in, m_ev_lb, m_ev_a_norm, m_ev_pool_w, m_ev_pool_scale, m_ev_w_out, m_od_w_in, m_od_conv, m_od_A_log, m_od_dt_bias, m_od_norm, m_od_w_out, m_ffn_w13, m_ffn_w2, v_c_ctx, v_w_ada, v_b_ada, v_norm_w, v_ev_w_in, v_ev_lb, v_ev_a_norm, v_ev_pool_w, v_ev_pool_scale, v_ev_w_out, v_od_w_in, v_od_conv, v_od_A_log, v_od_dt_bias, v_od_norm, v_od_w_out, v_ffn_w13, v_ffn_w2):
    names = ["c_ctx", "w_ada", "b_ada", "norm_w", "ev_w_in", "ev_lb", "ev_a_norm", "ev_pool_w", "ev_pool_scale",
             "ev_w_out", "od_w_in", "od_conv", "od_A_log", "od_dt_bias", "od_norm", "od_w_out", "ffn_w13", "ffn_w2"]
    wts = dict(zip(names, [c_ctx, w_ada, b_ada, norm_w, ev_w_in, ev_lb, ev_a_norm, ev_pool_w, ev_pool_scale,
                           ev_w_out, od_w_in, od_conv, od_A_log, od_dt_bias, od_norm, od_w_out, ffn_w13, ffn_w2]))
    mom1 = dict(zip(names, [m_c_ctx, m_w_ada, m_b_ada, m_norm_w, m_ev_w_in, m_ev_lb, m_ev_a_norm, m_ev_pool_w,
                            m_ev_pool_scale, m_ev_w_out, m_od_w_in, m_od_conv, m_od_A_log, m_od_dt_bias, m_od_norm,
                            m_od_w_out, m_ffn_w13, m_ffn_w2]))
    mom2 = dict(zip(names, [v_c_ctx, v_w_ada, v_b_ada, v_norm_w, v_ev_w_in, v_ev_lb, v_ev_a_norm, v_ev_pool_w,
                            v_ev_pool_scale, v_ev_w_out, v_od_w_in, v_od_conv, v_od_A_log, v_od_dt_bias, v_od_norm,
                            v_od_w_out, v_ffn_w13, v_ffn_w2]))

    ax, ay, ac = lax.axis_index("x"), lax.axis_index("y"), lax.axis_index("c")
    me = 4 * ax + 2 * ay + ac

    seq, d = x.shape[1], x.shape[2]
    n_ctx = ctx.shape[1]
    t = n_ctx + seq
    nt = t // ROW_TILE
    nct = n_ctx // ROW_TILE
    assert n_ctx == ROW_TILE and seq % ROW_TILE == 0 and ROW_TILE % GRID_W == 0
    depth = w_ada.shape[0]
    aw = d // 2
    n_ah = aw // HEAD
    n_grp = len(POOL_WINDOWS)
    dg = aw // n_grp
    assert dg % LANES == 0
    n_kh = d // HEAD
    kw, vw = n_kh * HEAD, 2 * n_kh * HEAD
    n_gate = 8 * n_kh
    ffn_h = ffn_w2.shape[1] * N_DEV
    ada_loc = w_ada.shape[2]
    assert depth == 2

    bf = lambda w_: w_.astype(BF16)

    def gathered(started, after, name):
        srcs, lands = exchange_wait(name, started, after, False)
        return [lax.dynamic_update_index_in_dim(l_, s_, me, 0) for l_, s_ in zip(lands, srcs)]

    small_shapes = [(d,), norm_w.shape, ev_lb.shape, ev_pool_w.shape[1:], od_conv.shape[1:]]
    (g1,) = all_gather("ag_small", [_pack([c[0], norm_w, ev_lb, ev_pool_w[0], od_conv[0]], F32, LANES, SUBLANES)])
    c_all, nw_g, lb_g, pw_g, cv_g = _unpack(g1, small_shapes, lead=1)
    nw_full = nw_g.transpose(1, 2, 0, 3).reshape(depth, 4, d)
    lb_full = lb_g.transpose(1, 2, 0, 3).reshape(2, depth + 1, aw)
    pw_full = pw_g.transpose(1, 0, 2, 3).reshape(n_grp, dg, dg)
    cv_full = cv_g.transpose(1, 0, 2).reshape(C_CONV, 2 * kw + vw)

    g_ev_in, g_ev_out = all_gather("ag_weights_ev", [bf(ev_w_in[0]), bf(ev_w_out[0])])

    def cols_natural(g):
        return g.transpose(1, 0, 2).reshape(g.shape[1], N_DEV * g.shape[2])

    def rows_natural(g):
        return g.reshape(N_DEV * g.shape[1], g.shape[2])

    def col_weight(g):
        return (g, True) if g.shape[2] % LANES == 0 else (cols_natural(g), False)

    w_ev_in = col_weight(g_ev_in)
    w_ev_out = rows_natural(g_ev_out)
    w13, w2 = [None] * depth, [None] * depth

    b_loc = lax.dynamic_slice_in_dim(b_ada, me * ada_loc, ada_loc, axis=1).reshape(depth, 1, ada_loc)
    ada_cb = _pick(ada_loc, TN_PREFS)
    ada_grid = (depth, ada_loc // ada_cb)
    ada_args = [
        TArg(c_all, (N_DEV, d), lambda l, j: (0, 0), "const"),
        TArg(c_ctx.reshape(1, d), (1, d), lambda l, j: (0, 0), "par", (0, 1)),
        TArg(w_ada, (None, d, ada_cb), lambda l, j: (l, 0, j)),
        TArg(b_loc, (None, 1, ada_cb), lambda l, j: (l, 0, j)),
    ]
    (m_loc,) = tile_fwd("ada_fwd", f_ada, ada_grid, ada_args,
                        [((depth, 2 * N_DEV, ada_loc), F32, (None, 2 * N_DEV, ada_cb), lambda l, j: (l, 0, j))])
    (m_all,) = all_gather("ag_mod", [m_loc])
    mods = []
    for layer in range(depth):
        lat = lax.dynamic_index_in_dim(m_all[:, layer], me, axis=1, keepdims=False).reshape(6, d)
        cxt = lax.dynamic_index_in_dim(m_all[:, layer], N_DEV + me, axis=1, keepdims=False).reshape(6, d)
        mods.append(jnp.stack([cxt, lat]))

    gathers_done = mods[0][0, :1, :SUBLANES] + g_ev_out[0, :1, :SUBLANES].astype(F32)
    ag_ffn0 = exchange_start("ag_start_ffn0", [bf(ffn_w13[0]), bf(ffn_w2[0])], False, gathers_done)
    ag_l1 = exchange_start("ag_start_l1", [bf(od_w_in[0]), bf(od_w_out[0]), bf(ffn_w13[1]), bf(ffn_w2[1])], False,
                           gathers_done)
    mods[0] = mods[0] + (ag_ffn0[4][0, 0] + ag_l1[4][0, 0])

    lb_slots = [TArg(lb_full[:, j], (2, aw), lambda i: (0, 0)) for j in range(depth + 1)]
    (lb0,) = tile_fwd("lb_fwd", f_lb(0), (1,), lb_slots, [((2, aw), F32, (2, aw), lambda i: (0, 0))])
    lb0r = lb0.reshape(2, n_ah, 1, HEAD)

    full_row = lambda i: (i, 0)
    par0 = lambda i: (0, 0)

    def nm_args(xs, layer, slot):
        return [TArg(xs, (ROW_TILE, d), full_row),
                TArg(nw_full[layer, slot].reshape(1, d), (1, d), par0, "par", (0,)),
                TArg(mods[layer], (2, 6, d), lambda i: (0, 0, 0), "par", (0,))]

    def norm_mod(name, xs, layer, slot, si, ci):
        return tile_fwd(name, f_norm_mod(si, ci, nct, transposed=True), (nt,), nm_args(xs, layer, slot),
                        [((t, d), BF16, (ROW_TILE, d), full_row), ((d, t), BF16, (d, ROW_TILE), lambda i: (0, i))])

    def norm_mod_bwd(name, xs, layer, slot, si, ci, dh, carry):
        return tile_bwd(name, f_norm_mod(si, ci, nct, True), (nt,), nm_args(xs, layer, slot),
                        [(dh, (ROW_TILE, d), full_row), (carry, (ROW_TILE, d), full_row)])

    def gr_args(xs, ys, layer, slot):
        return [TArg(xs, (ROW_TILE, d), full_row, grad=False), TArg(ys, (ROW_TILE, d), full_row, gdtype=BF16),
                TArg(nw_full[layer, slot].reshape(1, d), (1, d), par0, "par", (0,)),
                TArg(mods[layer], (2, 6, d), lambda i: (0, 0, 0), "par", (0,))]

    def gate_res(name, xs, ys, layer, slot, gi):
        (o,) = tile_fwd(name, f_gate_res(gi, nct), (nt,), gr_args(xs, ys, layer, slot),
                        [((t, d), F32, (ROW_TILE, d), full_row)])
        return o

    def gate_res_bwd(name, xs, ys, layer, slot, gi, dx):
        return tile_bwd(name, f_gate_res(gi, nct), (nt,), gr_args(xs, ys, layer, slot),
                        [(dx, (ROW_TILE, d), full_row)])

    sw_rows = ROW_TILE // 2

    def sw_args(gu):
        return [TArg(gu, (sw_rows, 2 * ffn_h), full_row, gdtype=BF16)]

    def ffn_fwd(tag, xs, layer):
        h2, h2t = norm_mod(f"nm2_{tag}", xs, layer, 2, 3, 4)
        gu = matmul(f"w13_{tag}", h2, w13[layer][0], "nn", slabs=w13[layer][1], out_dtype=BF16)
        (act,) = tile_fwd(f"swiglu_{tag}", f_swiglu, (t // sw_rows,), sw_args(gu),
                          [((t, ffn_h), BF16, (sw_rows, ffn_h), full_row)])
        fo = matmul(f"w2_{tag}", act, w2[layer], "nn")
        xn = gate_res(f"gr2_{tag}", xs, fo, layer, 3, 5)
        return xn, (xs, h2, h2t, gu, act, fo)

    def wgrad(name, a, dy, slabs=False, a_t=None):
        if a_t is not None:
            return matmul(name, a_t, dy, "nn", out_slabs=slabs, out_dtype=BF16)
        return matmul(name, a, dy, "tn", slabs=slabs, out_dtype=BF16)

    def col_grad(name, a, dy, slabs, a_t=None):
        g = wgrad(name, a, dy, slabs, a_t)
        return g if slabs else g.reshape(g.shape[0], N_DEV, g.shape[1] // N_DEV).transpose(1, 0, 2)

    def row_grad(name, a, dy):
        g = matmul(name, a, dy, "tn", out_dtype=BF16)
        return g.reshape(N_DEV, g.shape[0] // N_DEV, g.shape[1])

    def ffn_bwd(tag, saved, layer, dxn, acc):
        xs, h2, h2t, gu, act, fo = saved
        dfo, dnw3, dmod_a = gate_res_bwd(f"gr2b_{tag}", xs, fo, layer, 3, 5, dxn)
        dact = matmul(f"w2d_{tag}", dfo, w2[layer], "nt")
        dw2 = row_grad(f"w2w_{tag}", act, dfo)
        (dgu,) = tile_bwd(f"swiglub_{tag}", f_swiglu, (t // sw_rows,), sw_args(gu), [(dact, (sw_rows, ffn_h), full_row)])
        dh2 = matmul(f"w13d_{tag}", dgu, w13[layer][0], "nt", slabs=w13[layer][1])
        dw13 = col_grad(f"w13w_{tag}", h2, dgu, w13[layer][1], h2t)
        dxs, dnw2, dmod_b = norm_mod_bwd(f"nm2b_{tag}", xs, layer, 2, 3, 4, dh2, dxn)
        acc["ffn_w13"][layer] = dw13
        acc["ffn_w2"][layer] = dw2
        acc["norm_w"][layer][2] = dnw2
        acc["norm_w"][layer][3] = dnw3
        acc["mods"][layer].extend([dmod_a, dmod_b])
        return dxs

    n_a = t // A_CHUNK
    nca = n_ctx // A_CHUNK

    def a_tok(rev):
        if not rev:
            return lambda i: i
        return lambda i: jnp.where(i < nca, nca - 1 - i, n_a + nca - 1 - i)

    hb = _pick(n_ah, (HEADS_PER_STEP, 2, 1))
    n_hblk = n_ah // hb

    def hg_args(p, direction):
        tok = a_tok(direction == 1)
        blk = (A_CHUNK, hb * HEAD)
        return [TArg(p, blk, lambda h, i: (tok(i), h)),
                TArg(p, blk, lambda h, i: (tok(i), (1 + direction) * n_hblk + h)),
                TArg(p, blk, lambda h, i: (tok(i), 3 * n_hblk + h)),
                TArg(lb0r, (None, hb, 1, HEAD), lambda h, i: (direction, h, 0, 0), "par", (1,))]

    pmats = jnp.asarray(np.stack([_pool_mats(n_ctx), _pool_mats(GRID_W)]))

    def pool_args(p):
        return [TArg(p, (ROW_TILE, dg), lambda g, i: (i, 5 * n_grp + g)),
                TArg(pmats, (None, None, ROW_TILE, ROW_TILE), lambda g, i: (jnp.where(i < nct, 0, 1), g, 0, 0), "const"),
                TArg(pw_full, (None, dg, dg), lambda g, i: (g, 0, 0), "par", (1,)),
                TArg(ev_pool_scale, (1, dg), lambda g, i: (0, g), "par", (1,))]

    def ro_plan(gate_off, n_heads):
        per = _pick(n_heads, (8, 4, 2, 1))
        assert gate_off % per == 0
        return per, n_heads // per, gate_off // per

    def ro_args(o_f, o_b, gate_arr, gate_off, nw_arr, n_heads):
        per, _, goff = ro_plan(gate_off, n_heads)
        blk = (ROW_TILE, per * HEAD)
        return [TArg(o_f, blk, lambda h, i: (i, h)), TArg(o_b, blk, lambda h, i: (i, h), grad=False),
                TArg(gate_arr, blk, lambda h, i: (i, goff + h), gdtype=BF16),
                TArg(nw_arr, (1, HEAD), lambda h, i: (0, 0), "par", (0, 1))]

    def readout(name, o_f, o_b, gate_arr, gate_off, nw_arr, n_heads):
        per, nblk, _ = ro_plan(gate_off, n_heads)
        (o,) = tile_fwd(name, f_readout(per), (nblk, nt), ro_args(o_f, o_b, gate_arr, gate_off, nw_arr, n_heads),
                        [((t, n_heads * HEAD), BF16, (ROW_TILE, per * HEAD), lambda hh, i: (i, hh))])
        return o

    def readout_bwd(name, o_f, o_b, gate_arr, gate_off, nw_arr, n_heads, dout):
        per, nblk, _ = ro_plan(gate_off, n_heads)
        return tile_bwd(name, f_readout(per), (nblk, nt), ro_args(o_f, o_b, gate_arr, gate_off, nw_arr, n_heads),
                        [(dout, (ROW_TILE, per * HEAD), lambda hh, i: (i, hh))])

    def even_fwd(tag, xs, layer):
        h, ht = norm_mod(f"nm1_{tag}", xs, layer, 0, 0, 1)
        p = matmul(f"win_{tag}", h, w_ev_in[0], "nn", slabs=w_ev_in[1])
        outs, saves = [], []
        for direction in (0, 1):
            (o,), sv = scan_fwd(f"hgrn_{tag}_{direction}", f_hgrn2(direction == 1, hb), n_hblk, n_a, hg_args(p, direction),
                                [((t, aw), F32, (A_CHUNK, hb * HEAD), lambda hh, i, tok=a_tok(direction == 1): (tok(i), hh))], hb)
            outs.append(o)
            saves.append(sv)
        a_out = readout(f"ro_{tag}", outs[0], outs[1], p, 4 * n_ah, ev_a_norm, n_ah)
        (pooled,) = tile_fwd(f"pool_{tag}", f_pool, (n_grp, nt), pool_args(p),
                             [((t, aw), BF16, (ROW_TILE, dg), lambda g, i: (i, g))])
        cat = assemble(f"cat_{tag}", [[(a_out, aw, 0)], [(pooled, aw, 0)]], BF16)
        y = matmul(f"wout_{tag}", cat, w_ev_out, "nn")
        xn = gate_res(f"gr1_{tag}", xs, y, layer, 1, 2)
        return xn, (xs, h, ht, p, outs, saves, cat, y)

    def even_bwd(tag, saved, layer, dxn, acc):
        xs, h, ht, p, outs, saves, cat, y = saved
        dy, dnw1, dmod_a = gate_res_bwd(f"gr1b_{tag}", xs, y, layer, 1, 2, dxn)
        dcat = matmul(f"woutd_{tag}", dy, w_ev_out, "nt")
        acc["ev_w_out"] = row_grad(f"woutw_{tag}", cat, dy)
        do, dgate, d_anorm = readout_bwd(f"rob_{tag}", outs[0], outs[1], p, 4 * n_ah, ev_a_norm, n_ah, dcat)
        du, d_pw, d_ps = tile_bwd(f"poolb_{tag}", f_pool, (n_grp, nt), pool_args(p),
                                  [(dcat, (ROW_TILE, dg), lambda g, i: (i, n_grp + g))])
        dq, df, di, dlb = [], [], [], []
        for direction in (0, 1):
            r = scan_bwd(f"hgrnb_{tag}_{direction}", f_hgrn2(direction == 1, hb), n_hblk, n_a, hg_args(p, direction),
                         saves[direction],
                         [(do, (A_CHUNK, hb * HEAD), lambda hh, i, tok=a_tok(direction == 1): (tok(i), hh))])
            dq.append(r[0])
            df.append(r[1])
            di.append(r[2])
            dlb.append(r[3])
        sec = lambda arr, s: (arr, aw, s)
        dp = assemble(f"dp_{tag}", [[sec(dq[0], 0), sec(dq[1], 0)], [sec(df[0], 1)], [sec(df[1], 2)],
                                    [sec(di[0], 3), sec(di[1], 3)], [sec(dgate, 4)], [sec(du, 5)]], BF16)
        acc["ev_w_in"] = col_grad(f"winw_{tag}", h, dp, w_ev_in[1], ht)
        acc["rs_ev"] = exchange_start("rs_start_ev", [acc["ev_w_in"], acc["ev_w_out"]], True, dp, carry=w_ev_in[0])
        dh = matmul(f"wind_{tag}", dp, acc["rs_ev"][5], "nt", slabs=w_ev_in[1])
        dxs, dnw0, dmod_b = norm_mod_bwd(f"nm1b_{tag}", xs, layer, 0, 0, 1, dh, dxn)
        acc["norm_w"][layer][0] = dnw0
        acc["norm_w"][layer][1] = dnw1
        acc["mods"][layer].extend([dmod_a, dmod_b])
        acc["ev_a_norm"] = d_anorm
        acc["ev_pool_w"] = d_pw
        acc["ev_pool_scale"] = d_ps
        acc["lb0"] = jnp.stack([dlb[0][0], dlb[1][1]]).reshape(2, aw)
        return dxs

    n_c = t // C_CHUNK
    ncc = n_ctx // C_CHUNK

    def c_tok(rev):
        if not rev:
            return lambda i: i
        return lambda i: jnp.where(i < ncc, ncc - 1 - i, n_c + ncc - 1 - i)

    alog = od_A_log[0].reshape(2, n_kh, 2, 1)
    dtb = od_dt_bias[0].reshape(2, n_kh, 2, 1)

    khb = _pick(n_kh, (HEADS_PER_STEP, 2, 1))
    n_kblk = n_kh // khb

    def gd_args(z, gates, direction):
        tok = c_tok(direction == 1)
        gblk = (None, khb, None, 2, C_CHUNK)
        sblk = (None, khb, 2, 1)
        return [TArg(z, (C_CHUNK, khb * HEAD), lambda kb, i: (tok(i), kb)),
                TArg(z, (C_CHUNK, khb * HEAD), lambda kb, i: (tok(i), n_kblk + kb)),
                TArg(z, (C_CHUNK, khb * 2 * HEAD), lambda kb, i: (tok(i), n_kblk + kb)),
                TArg(gates, gblk, lambda kb, i: (direction, kb, tok(i), 0, 0)),
                TArg(gates, gblk, lambda kb, i: (2 + direction, kb, tok(i), 0, 0)),
                TArg(alog, sblk, lambda kb, i: (direction, kb, 0, 0), "par", (1,)),
                TArg(dtb, sblk, lambda kb, i: (direction, kb, 0, 0), "par", (1,))]

    def odd_fwd(tag, xs, layer):
        h, ht = norm_mod(f"nm1_{tag}", xs, layer, 0, 0, 1)
        pm = matmul(f"win_{tag}", h, w_od_main, "nn")
        pg = matmul(f"wgate_{tag}", h, w_od_gate, "nn")
        z = conv_fwd(f"conv_{tag}", pm, cv_full, 2 * kw + vw, nct)
        gates = pg.reshape(n_c, C_CHUNK, 4, n_kh, 2).transpose(2, 3, 0, 4, 1)
        outs, saves = [], []
        for direction in (0, 1):
            xrows = 2 * khb * C_CHUNK
            (o, xinv), sv = scan_fwd(
                f"gdn_{tag}_{direction}", f_gdn(direction == 1, khb, False), n_kblk, n_c, gd_args(z, gates, direction),
                [((t, vw), F32, (C_CHUNK, khb * 2 * HEAD), lambda kb, i, tok=c_tok(direction == 1): (tok(i), kb)),
                 ((n_kblk, n_c, xrows, C_CHUNK), F32, (None, None, xrows, C_CHUNK), lambda kb, i: (kb, i, 0, 0))],
                2 * khb)
            outs.append(o)
            saves.append((sv, xinv))
        n_vh = 2 * n_kh
        yo = readout(f"ro_{tag}", outs[0], outs[1], pm, 2 * n_kh + n_vh, od_norm, n_vh)
        y = matmul(f"wout_{tag}", yo, w_od_out, "nn")
        xn = gate_res(f"gr1_{tag}", xs, y, layer, 1, 2)
        return xn, (xs, ht, pm, z, gates, outs, saves, yo, y)

    def odd_bwd(tag, saved, layer, dxn, acc):
        xs, ht, pm, z, gates, outs, saves, yo, y = saved
        n_vh = 2 * n_kh
        dy, dnw1, dmod_a = gate_res_bwd(f"gr1b_{tag}", xs, y, layer, 1, 2, dxn)
        dyo = matmul(f"woutd_{tag}", dy, w_od_out, "nt")
        acc["od_w_out"] = row_grad(f"woutw_{tag}", yo, dy)
        do, dzg, d_onorm = readout_bwd(f"rob_{tag}", outs[0], outs[1], pm, 2 * n_kh + n_vh, od_norm, n_vh, dyo)
        dq, dk, dv, dga, dgb, dal, ddt = [], [], [], [], [], [], []
        for direction in (0, 1):
            sv, xinv = saves[direction]
            xarg = TArg(xinv, (None, None, 2 * khb * C_CHUNK, C_CHUNK), lambda kb, i: (kb, i, 0, 0), "const")
            r = scan_bwd(f"gdnb_{tag}_{direction}", f_gdn(direction == 1, khb, True), n_kblk, n_c,
                         gd_args(z, gates, direction) + [xarg], sv,
                         [(do, (C_CHUNK, khb * 2 * HEAD), lambda kb, i, tok=c_tok(direction == 1): (tok(i), kb))])
            for lst, v_ in zip((dq, dk, dv, dga, dgb, dal, ddt), r):
                lst.append(v_)
        dz = assemble(f"dz_{tag}", [[(dq[0], kw, 0), (dq[1], kw, 0)], [(dk[0], kw, 1), (dk[1], kw, 1)],
                                    [(dv[0], vw, 1), (dv[1], vw, 1)]], F32)
        dpm, d_conv = conv_bwd(f"convb_{tag}", pm, cv_full, dz, 2 * kw + vw, nct, dzg)
        dgates = jnp.stack([dga[0][0], dga[1][1], dgb[0][2], dgb[1][3]])
        dpg = dgates.transpose(2, 4, 0, 1, 3).reshape(t, n_gate).astype(BF16)
        dh = matmul(f"wgated_{tag}", dpg, w_od_gate, "nt")
        dh = matmul(f"wind_{tag}", dpm, w_od_main, "nt", add=dh)
        dw_in = jnp.concatenate([wgrad(f"winw_{tag}", None, dpm, a_t=ht), wgrad(f"wgatew_{tag}", None, dpg, a_t=ht)],
                                axis=1)
        acc["od_w_in"] = dw_in.reshape(d, N_DEV, dw_in.shape[1] // N_DEV).transpose(1, 0, 2)
        dxs, dnw0, dmod_b = norm_mod_bwd(f"nm1b_{tag}", xs, layer, 0, 0, 1, dh, dxn)
        acc["norm_w"][layer][0] = dnw0
        acc["norm_w"][layer][1] = dnw1
        acc["mods"][layer].extend([dmod_a, dmod_b])
        acc["od_norm"] = d_onorm
        acc["od_conv"] = d_conv
        acc["od_A_log"] = jnp.stack([dal[0][0], dal[1][1]]).reshape(1, 2, n_vh)
        acc["od_dt_bias"] = jnp.stack([ddt[0][0], ddt[1][1]]).reshape(1, 2, n_vh)
        return dxs

    xs0 = jnp.concatenate([ctx[0], x[0]], axis=0)
    xs1, sv_e = even_fwd("l0", xs0, 0)
    g_w13a, g_w2a = gathered(ag_ffn0, xs1, "ag_wait_ffn0")
    w13[0], w2[0] = col_weight(g_w13a), rows_natural(g_w2a)
    xs2, sv_f0 = ffn_fwd("l0", xs1, 0)
    g_od_in, g_od_out, g_w13b, g_w2b = gathered(ag_l1, xs2, "ag_wait_l1")
    w_od_in = cols_natural(g_od_in)
    w_od_main, w_od_gate = w_od_in[:, :2 * kw + 2 * vw], w_od_in[:, 2 * kw + 2 * vw:]
    w_od_out = rows_natural(g_od_out)
    w13[1], w2[1] = col_weight(g_w13b), rows_natural(g_w2b)
    xs3, sv_o = odd_fwd("l1", xs2, 1)
    xs4, sv_f1 = ffn_fwd("l1", xs3, 1)
    loss_loc, dxs = loss_kernel("loss", xs4, loss_target[0], nct)
    loss = lax.psum(loss_loc, ("x", "y", "c"))

    acc = {"norm_w": [[None] * 4 for _ in range(depth)], "mods": [[] for _ in range(depth)],
           "ffn_w13": [None] * depth, "ffn_w2": [None] * depth}
    dxs = ffn_bwd("l1", sv_f1, 1, dxs, acc)
    rs_ffn1 = exchange_start("rs_start_ffn1", [acc["ffn_w13"][1], acc["ffn_w2"][1]], True, dxs)
    mods[1] = mods[1] + rs_ffn1[4][0, 0]
    dxs = odd_bwd("l1", sv_o, 1, dxs, acc)
    rs_od = exchange_start("rs_start_od", [acc["od_w_in"], acc["od_w_out"]], True, dxs)
    mods[0] = mods[0] + rs_od[4][0, 0]
    dxs = ffn_bwd("l0", sv_f0, 0, dxs, acc)
    rs_ffn0 = exchange_start("rs_start_ffn0", [acc["ffn_w13"][0], acc["ffn_w2"][0]], True, dxs)
    mods[0] = mods[0] + rs_ffn0[4][0, 0]
    dxs = even_bwd("l0", sv_e, 0, dxs, acc)
    rs_ev = acc["rs_ev"]
    grad_x = dxs[n_ctx:].reshape(1, seq, d)

    (d_lb_slots) = tile_bwd("lb_bwd", f_lb(0), (1,), lb_slots, [(acc["lb0"], (2, aw), lambda i: (0, 0))])
    d_ev_lb = jnp.stack(d_lb_slots, axis=1)

    dmods = jnp.stack([functools.reduce(jnp.add, acc["mods"][layer]) for layer in range(depth)])
    (dm_all,) = all_gather("ag_dmod", [dmods.reshape(depth * 2 * 6, d)])
    dm_all = dm_all.reshape(N_DEV, depth, 2, 6 * d)
    dm_cols = lax.dynamic_slice_in_dim(dm_all, me * ada_loc, ada_loc, axis=3)
    dm_loc = jnp.concatenate([dm_cols[:, :, 1].transpose(1, 0, 2), dm_cols[:, :, 0].transpose(1, 0, 2)], axis=1)
    d_cctx_part, d_w_ada, d_b_loc = tile_bwd("ada_bwd", f_ada, ada_grid, ada_args,
                                             [(dm_loc, (None, 2 * N_DEV, ada_cb), lambda l, j: (l, 0, j))])

    d_b_full = lax.dynamic_update_slice_in_dim(jnp.zeros_like(b_ada), d_b_loc.reshape(depth, ada_loc), me * ada_loc, axis=1)
    d_nw = jnp.stack([jnp.stack([acc["norm_w"][layer][s].reshape(d) for s in range(4)]) for layer in range(depth)])
    small_grads = [d_cctx_part.reshape(d), d_b_full, d_nw, d_ev_lb, acc["ev_a_norm"], acc["ev_pool_w"],
                   acc["ev_pool_scale"], acc["od_conv"], acc["od_A_log"], acc["od_dt_bias"], acc["od_norm"]]
    sg_shapes = [a.shape for a in small_grads]
    (sg,) = all_gather("ag_smallgrads", [_pack(small_grads, F32, FLAT_W, SUBLANES)])
    sg_sum = sum_leading("sum_smallgrads", sg, F32)
    (g_cctx, g_bada, g_nw, g_lb, g_anorm, g_pw, g_ps, g_conv, g_alog, g_dtb, g_onorm) = _unpack(sg_sum, sg_shapes)

    def my_cols(full, axis):
        loc = full.shape[axis] // N_DEV
        return lax.dynamic_slice_in_dim(full, me * loc, loc, axis=axis)

    grads = {
        "c_ctx": g_cctx, "w_ada": d_w_ada, "b_ada": g_bada, "norm_w": my_cols(g_nw, 2), "ev_lb": my_cols(g_lb, 2),
        "ev_a_norm": g_anorm, "ev_pool_w": my_cols(g_pw, 1)[None], "ev_pool_scale": g_ps,
        "od_conv": my_cols(g_conv, 1)[None], "od_A_log": g_alog, "od_dt_bias": g_dtb, "od_norm": g_onorm,
    }

    def reduced(started, tags_, name):
        srcs, lands = exchange_wait(name, started, sg_sum, True)
        out = []
        for tg, s_, l_ in zip(tags_, srcs, lands):
            own = lax.dynamic_index_in_dim(s_, me, 0, keepdims=True)
            out.append(sum_leading(f"rs_sum_{tg}", lax.dynamic_update_slice_in_dim(l_, own, me, 0), F32))
        return out

    g_w13b, g_w2b = reduced(rs_ffn1, ["w13b", "w2b"], "rs_wait_ffn1")
    g_od_in, g_od_out = reduced(rs_od, ["od_in", "od_out"], "rs_wait_od")
    g_w13a, g_w2a = reduced(rs_ffn0, ["w13a", "w2a"], "rs_wait_ffn0")
    g_ev_in, g_ev_out = reduced(rs_ev, ["ev_in", "ev_out"], "rs_wait_ev")
    grads["ev_w_in"], grads["ev_w_out"], grads["od_w_in"], grads["od_w_out"] = (g_ev_in[None], g_ev_out[None],
                                                                                g_od_in[None], g_od_out[None])
    grads["ffn_w13"] = jnp.stack([g_w13a, g_w13b])
    grads["ffn_w2"] = jnp.stack([g_w2a, g_w2b])

    big_names = ["w_ada", "ev_w_in", "ev_w_out", "od_w_in", "od_w_out", "ffn_w13", "ffn_w2"]
    small_names = [n_ for n_ in names if n_ not in big_names]
    gl = {n_: grads[n_].reshape(wts[n_].shape) for n_ in names}
    delta, new_m, new_v = {}, {}, {}
    for n_ in big_names:
        shp = wts[n_].shape
        res = adamw(f"adamw_{n_}", _rows2d(gl[n_]), _rows2d(wts[n_]), _rows2d(mom1[n_]), _rows2d(mom2[n_]))
        delta[n_], new_m[n_], new_v[n_] = (r_.reshape(shp) for r_ in res)
    shapes = [wts[n_].shape for n_ in small_names]
    pk = lambda dct: _pack([dct[n_] for n_ in small_names], F32, FLAT_W, SUBLANES)
    res = adamw("adamw_small", pk(gl), pk(wts), pk(mom1), pk(mom2))
    for dct, r_ in zip((delta, new_m, new_v), res):
        for n_, a_ in zip(small_names, _unpack(r_, shapes)):
            dct[n_] = a_
    return (loss, grad_x, *[gl[n_] for n_ in names], *[delta[n_] for n_ in names], *[new_m[n_] for n_ in names],
            *[new_v[n_] for n_ in names])
```

```python
import functools
from typing import Any, NamedTuple

import numpy as np

import jax
import jax.numpy as jnp
from jax import lax
from jax.experimental import pallas as pl
from jax.experimental.pallas import tpu as pltpu

F32 = jnp.float32
BF16 = jnp.bfloat16
MESH = pl.DeviceIdType.MESH
N_DEV = 8

EPS = 1e-6
GRID_W = 64
HEAD = 128
A_CHUNK = 32
C_CHUNK = 64
C_CONV = 4
POOL_WINDOWS = (2, 4, 8, 16)
MASKED_EXPONENT = -1e30
ADAM_LR, ADAM_B1, ADAM_B2, ADAM_EPS, ADAM_WD, ADAM_STEP = 0.001, 0.9, 0.999, 1e-08, 0.01, 10

VMEM_LIMIT_BYTES = 56 * 1024 * 1024
LANES = 128
SUBLANES = 8
ROW_TILE = 256
FLAT_W = 1024
TM_PREFS = (1056, 1024, 768, 512, 256, 128, 64, 32, 16)
TN_PREFS = (768, 512, 1408, 256, 128)
TK_PREFS = (2048, 2816, 1408, 1024, 768, 512, 384, 256, 128)
TO_PREFS = (1024, 1408, 768, 704, 512, 384, 256, 128)
HEADS_PER_STEP = 8


def _pick(dim, prefs):
    for p in prefs:
        if p <= dim and dim % p == 0:
            return p
    return dim


def _cparams(ngrid):
    return pltpu.CompilerParams(dimension_semantics=("arbitrary",) * ngrid, vmem_limit_bytes=VMEM_LIMIT_BYTES)


def _sds(shape, dtype):
    return jax.ShapeDtypeStruct(tuple(shape), dtype)


def _split(x):
    hi = x.astype(BF16)
    return hi, (x - hi.astype(F32)).astype(BF16)


def _dot(a, b, ca, cb, hi):
    dims = (((ca,), (cb,)), ((), ()))
    dot = lambda u, v: lax.dot_general(u, v, dims, preferred_element_type=F32)
    if hi:
        (ah, al), (bh, bl) = _split(a.astype(F32)), _split(b.astype(F32))
        return dot(ah, bh) + (dot(ah, bl) + dot(al, bh))
    return dot(a.astype(BF16), b.astype(BF16))


@functools.partial(jax.custom_vjp, nondiff_argnums=(2, 3, 4))
def mm(a, b, ca=1, cb=0, hi=False):
    return _dot(a, b, ca, cb, hi)


def _mm_fwd(a, b, ca, cb, hi):
    return _dot(a, b, ca, cb, hi), (a, b)


HI_FWD, HI_BOTH = 1, 2


def _mm_bwd(ca, cb, hi, res, g):
    a, b = res
    bhi = hi == HI_BOTH
    da = _dot(g, b, 1, 1 - cb, bhi) if ca == 1 else _dot(b, g, 1 - cb, 1, bhi)
    db = _dot(a, g, 1 - ca, 0, bhi) if cb == 0 else _dot(g, a, 0, 1 - ca, bhi)
    return da, db


mm.defvjp(_mm_fwd, _mm_bwd)


def _iota2(n, m, axis):
    return lax.broadcasted_iota(jnp.int32, (n, m), axis)


class TArg(NamedTuple):
    arr: Any
    block: tuple
    imap: Any
    kind: str = "row"
    acc: tuple = ()
    gdtype: Any = F32
    grad: bool = True


def _load(ref):
    v = ref[...]
    return v.astype(F32) if jnp.issubdtype(v.dtype, jnp.floating) else v


def tile_fwd(name, f, grid, args, outs):
    n_in, ng = len(args), len(grid)

    def body(*refs):
        pids = tuple(pl.program_id(k) for k in range(ng))
        res = f(pids, *[_load(r) for r in refs[:n_in]])
        for r, v in zip(refs[n_in:], res):
            r[...] = v.astype(r.dtype)

    return pl.pallas_call(
        body, grid=grid, name=name,
        in_specs=[pl.BlockSpec(a.block, a.imap) for a in args],
        out_specs=[pl.BlockSpec(b, im) for (_, _, b, im) in outs],
        out_shape=[_sds(s, d) for (s, d, _, _) in outs],
        compiler_params=_cparams(ng),
    )(*[a.arr for a in args])


def _store_grads(args, diff, pids, g_refs, d):
    for k, gr, dv in zip(diff, g_refs, d):
        a = args[k]
        if a.kind == "row" or not a.acc:
            gr[...] = dv.astype(gr.dtype)
        else:
            first = pids[a.acc[0]] == 0
            for ax in a.acc[1:]:
                first = jnp.logical_and(first, pids[ax] == 0)

            @pl.when(first)
            def _(gr=gr, dv=dv):
                gr[...] = dv.astype(gr.dtype)

            @pl.when(jnp.logical_not(first))
            def _(gr=gr, dv=dv):
                gr[...] += dv.astype(gr.dtype)


def tile_bwd(name, f, grid, args, cts):
    n_in, n_ct, ng = len(args), len(cts), len(grid)
    diff = [k for k, a in enumerate(args) if a.kind != "const" and a.grad]

    def body(*refs):
        pids = tuple(pl.program_id(k) for k in range(ng))
        vals = [_load(r) for r in refs[:n_in]]

        def g(*dv):
            full = list(vals)
            for k, v in zip(diff, dv):
                full[k] = v
            return tuple(f(pids, *full))

        _, vjp = jax.vjp(g, *[vals[k] for k in diff])
        d = vjp(tuple(_load(r) for r in refs[n_in:n_in + n_ct]))
        _store_grads(args, diff, pids, refs[n_in + n_ct:], d)

    return pl.pallas_call(
        body, grid=grid, name=name,
        in_specs=[pl.BlockSpec(a.block, a.imap) for a in args] + [pl.BlockSpec(b, im) for (_, b, im) in cts],
        out_specs=[pl.BlockSpec(args[k].block, args[k].imap) for k in diff],
        out_shape=[_sds(args[k].arr.shape, args[k].gdtype) for k in diff],
        compiler_params=_cparams(ng),
    )(*[a.arr for a in args], *[c[0] for c in cts])


def scan_fwd(name, f, n_heads, n_steps, args, outs, n_state):
    n_in, n_out = len(args), len(outs)
    sblock = (None, None, HEAD, HEAD)

    def body(*refs):
        in_refs = refs[:n_in]
        out_refs = refs[n_in:n_in + n_out]
        save_refs = refs[n_in + n_out:n_in + n_out + n_state]
        s_refs = refs[n_in + n_out + n_state:]

        @pl.when(pl.program_id(1) == 0)
        def _():
            for s in s_refs:
                s[...] = jnp.zeros_like(s)

        states = tuple(s[...] for s in s_refs)
        for sv, s in zip(save_refs, states):
            sv[...] = s
        new_states, res = f(states, *[_load(r) for r in in_refs])
        for s, v in zip(s_refs, new_states):
            s[...] = v
        for r, v in zip(out_refs, res):
            r[...] = v.astype(r.dtype)

    res = pl.pallas_call(
        body, grid=(n_heads, n_steps), name=name,
        in_specs=[pl.BlockSpec(a.block, a.imap) for a in args],
        out_specs=[pl.BlockSpec(b, im) for (_, _, b, im) in outs]
        + [pl.BlockSpec(sblock, lambda h, i: (h, i, 0, 0))] * n_state,
        out_shape=[_sds(s, d) for (s, d, _, _) in outs] + [_sds((n_heads, n_steps, HEAD, HEAD), F32)] * n_state,
        scratch_shapes=[pltpu.VMEM((HEAD, HEAD), F32)] * n_state,
        compiler_params=_cparams(2),
    )(*[a.arr for a in args])
    return res[:n_out], res[n_out:]


def scan_bwd(name, f, n_heads, n_steps, args, saves, cts):
    n_in, n_ct, n_state = len(args), len(cts), len(saves)
    diff = [k for k, a in enumerate(args) if a.kind != "const" and a.grad]
    sblock = (None, None, HEAD, HEAD)

    def rv(im):
        return lambda h, i: im(h, n_steps - 1 - i)

    def body(*refs):
        in_refs = refs[:n_in]
        save_refs = refs[n_in:n_in + n_state]
        ct_refs = refs[n_in + n_state:n_in + n_state + n_ct]
        g_refs = refs[n_in + n_state + n_ct:n_in + n_state + n_ct + len(diff)]
        ds_refs = refs[n_in + n_state + n_ct + len(diff):]
        pids = (pl.program_id(0), pl.program_id(1))

        @pl.when(pids[1] == 0)
        def _():
            for s in ds_refs:
                s[...] = jnp.zeros_like(s)

        vals = [_load(r) for r in in_refs]

        def g(states, *dv):
            full = list(vals)
            for k, v in zip(diff, dv):
                full[k] = v
            new_states, res = f(states, *full)
            return tuple(new_states), tuple(res)

        _, vjp = jax.vjp(g, tuple(s[...] for s in save_refs), *[vals[k] for k in diff])
        d = vjp((tuple(s[...] for s in ds_refs), tuple(_load(r) for r in ct_refs)))
        for s, v in zip(ds_refs, d[0]):
            s[...] = v
        _store_grads(args, diff, pids, g_refs, d[1:])

    return pl.pallas_call(
        body, grid=(n_heads, n_steps), name=name,
        in_specs=[pl.BlockSpec(a.block, rv(a.imap)) for a in args]
        + [pl.BlockSpec(sblock, rv(lambda h, i: (h, i, 0, 0)))] * n_state
        + [pl.BlockSpec(b, rv(im)) for (_, b, im) in cts],
        out_specs=[pl.BlockSpec(args[k].block, rv(args[k].imap)) for k in diff],
        out_shape=[_sds(args[k].arr.shape, args[k].gdtype) for k in diff],
        scratch_shapes=[pltpu.VMEM((HEAD, HEAD), F32)] * n_state,
        compiler_params=_cparams(2),
    )(*[a.arr for a in args], *saves, *[c[0] for c in cts])


def matmul(name, a, b, mode, add=None, out_dtype=F32, slabs=False, out_slabs=False):
    o_spec = None
    if mode == "nn":
        m, k = a.shape
        ns = b.shape[2] if slabs else (b.shape[1] // N_DEV if out_slabs else b.shape[1])
        n = N_DEV * ns if (slabs or out_slabs) else ns
        to_m, to_n, tr = _pick(m, TM_PREFS), _pick(ns, TN_PREFS), _pick(k, TK_PREFS)
        nb = ns // to_n
        grid = (m // to_m, n // to_n, k // tr)
        a_spec = pl.BlockSpec((to_m, tr), lambda i, j, l: (i, l))
        if slabs:
            b_spec = pl.BlockSpec((None, tr, to_n), lambda i, j, l: (j // nb, l, j % nb))
        else:
            b_spec = pl.BlockSpec((tr, to_n), lambda i, j, l: (l, j))
        dims, oshape = (1, 0), (m, n)
        if out_slabs:
            o_spec = pl.BlockSpec((None, to_m, to_n), lambda i, j, l: (j // nb, i, j % nb))
            oshape = (N_DEV, m, ns)
    elif mode == "nt":
        m, n = a.shape
        k = b.shape[1] if slabs else b.shape[0]
        ns = n // N_DEV if slabs else n
        to_m, to_n, tr = _pick(m, TM_PREFS), _pick(k, TO_PREFS), _pick(ns, TK_PREFS)
        nb = ns // tr
        grid = (m // to_m, k // to_n, n // tr)
        a_spec = pl.BlockSpec((to_m, tr), lambda i, j, l: (i, l))
        if slabs:
            b_spec = pl.BlockSpec((None, to_n, tr), lambda i, j, l: (l // nb, j, l % nb))
        else:
            b_spec = pl.BlockSpec((to_n, tr), lambda i, j, l: (j, l))
        dims, oshape = (1, 1), (m, k)
    else:
        (t, k), n = a.shape, b.shape[1]
        ns = n // N_DEV if slabs else n
        to_m, to_n, tr = _pick(k, TO_PREFS), _pick(ns, (2048,) + TO_PREFS), _pick(t, TM_PREFS)
        nb = ns // to_n
        grid = (k // to_m, n // to_n, t // tr)
        a_spec = pl.BlockSpec((tr, to_m), lambda i, j, l: (l, i))
        b_spec = pl.BlockSpec((tr, to_n), lambda i, j, l: (l, j))
        dims, oshape = (0, 0), (k, n)
        if slabs:
            o_spec = pl.BlockSpec((None, to_m, to_n), lambda i, j, l: (j // nb, i, j % nb))
            oshape = (N_DEV, k, ns)
    n_red = grid[2]
    if o_spec is None:
        o_spec = pl.BlockSpec((to_m, to_n), lambda i, j, l: (i, j))
    has_add = add is not None

    def body(a_ref, b_ref, *rest):
        add_ref = rest[0] if has_add else None
        o_ref = rest[1] if has_add else rest[0]
        part = lax.dot_general(a_ref[...].astype(BF16), b_ref[...].astype(BF16),
                               (((dims[0],), (dims[1],)), ((), ())), preferred_element_type=F32)

        def finish(v):
            if has_add:
                v = v + add_ref[...]
            o_ref[...] = v.astype(o_ref.dtype)

        if n_red == 1:
            finish(part)
        else:
            acc = rest[-1]
            step = pl.program_id(2)

            @pl.when(step == 0)
            def _():
                acc[...] = part

            @pl.when(step > 0)
            def _():
                acc[...] += part

            @pl.when(step == n_red - 1)
            def _():
                finish(acc[...])

    return pl.pallas_call(
        body, grid=grid, name=name,
        in_specs=[a_spec, b_spec] + ([o_spec] if has_add else []),
        out_specs=o_spec, out_shape=_sds(oshape, out_dtype),
        scratch_shapes=[pltpu.VMEM((to_m, to_n), F32)] if n_red > 1 else [],
        compiler_params=_cparams(3),
    )(a, b, *([add] if has_add else []))


def assemble(name, pieces, out_dtype):
    flat = [s for piece in pieces for s in piece]
    t = flat[0][0].shape[0]
    widths = [piece[0][1] for piece in pieces]

    def body(*refs):
        o_ref, k, off = refs[-1], 0, 0
        for piece, w in zip(pieces, widths):
            v = refs[k][...].astype(F32)
            k += 1
            for _ in piece[1:]:
                v = v + refs[k][...].astype(F32)
                k += 1
            o_ref[:, off:off + w] = v.astype(o_ref.dtype)
            off += w

    tr = ROW_TILE // 2
    return pl.pallas_call(
        body, grid=(t // tr,), name=name,
        in_specs=[pl.BlockSpec((tr, w), functools.partial(lambda i, cb: (i, cb), cb=cb)) for (_, w, cb) in flat],
        out_specs=pl.BlockSpec((tr, sum(widths)), lambda i: (i, 0)),
        out_shape=_sds((t, sum(widths)), out_dtype),
        compiler_params=_cparams(1),
    )(*[s[0] for s in flat])


ELEM_ROWS = (128, 64, 32, 16, 8)


def _rows2d(a, lead=0):
    return a.reshape(a.shape[:lead] + (-1, a.shape[-1]))


def sum_leading(name, arr, out_dtype):
    k, rows, w = arr.shape
    tr = _pick(rows, ELEM_ROWS)

    def body(a_ref, o_ref):
        v = a_ref[0].astype(F32)
        for j in range(1, k):
            v = v + a_ref[j].astype(F32)
        o_ref[...] = v.astype(o_ref.dtype)

    return pl.pallas_call(
        body, grid=(rows // tr,), name=name,
        in_specs=[pl.BlockSpec((k, tr, w), lambda i: (0, i, 0))],
        out_specs=pl.BlockSpec((tr, w), lambda i: (i, 0)),
        out_shape=_sds((rows, w), out_dtype), compiler_params=_cparams(1),
    )(arr)


def adamw(name, g, w, m, v):
    rows, wd = g.shape
    tr = _pick(rows, ELEM_ROWS)

    def body(g_ref, w_ref, m_ref, v_ref, d_ref, nm_ref, nv_ref):
        gv = g_ref[...]
        mn = ADAM_B1 * m_ref[...] + (1.0 - ADAM_B1) * gv
        vn = ADAM_B2 * v_ref[...] + (1.0 - ADAM_B2) * jnp.square(gv)
        m_hat = mn / (1.0 - ADAM_B1 ** ADAM_STEP)
        v_hat = vn / (1.0 - ADAM_B2 ** ADAM_STEP)
        d_ref[...] = -ADAM_LR * (m_hat / (jnp.sqrt(v_hat) + ADAM_EPS) + ADAM_WD * w_ref[...])
        nm_ref[...] = mn
        nv_ref[...] = vn

    spec = pl.BlockSpec((tr, wd), lambda i: (i, 0))
    return pl.pallas_call(
        body, grid=(rows // tr,), name=name, in_specs=[spec] * 4, out_specs=[spec] * 3,
        out_shape=[_sds(g.shape, F32)] * 3, compiler_params=_cparams(1),
    )(g, w, m, v)


def loss_kernel(name, xs, target, n_ctx_tiles):
    t, d = xs.shape
    nt = t // ROW_TILE

    def body(x_ref, t_ref, dx_ref, l_ref):
        i = pl.program_id(0)
        is_lat = i >= n_ctx_tiles
        err = jnp.where(is_lat, x_ref[...] - t_ref[...], 0.0)
        dx_ref[...] = err / d
        part = 0.5 * jnp.sum(jnp.mean(jnp.square(err), axis=-1, keepdims=True), axis=0, keepdims=True)

        @pl.when(i == 0)
        def _():
            l_ref[...] = jnp.zeros_like(l_ref)

        l_ref[...] += jnp.broadcast_to(part, l_ref.shape)

    dx, l = pl.pallas_call(
        body, grid=(nt,), name=name,
        in_specs=[pl.BlockSpec((ROW_TILE, d), lambda i: (i, 0)),
                  pl.BlockSpec((ROW_TILE, d), lambda i: (jnp.maximum(i - n_ctx_tiles, 0), 0))],
        out_specs=[pl.BlockSpec((ROW_TILE, d), lambda i: (i, 0)), pl.BlockSpec((SUBLANES, LANES), lambda i: (0, 0))],
        out_shape=[_sds((t, d), F32), _sds((SUBLANES, LANES), F32)], compiler_params=_cparams(1),
    )(xs, target)
    return l[0, 0], dx


CONV_COLS = 2048
CONV_LEFT = C_CONV // 2


def _conv_halo_specs(t, n_ctx_tiles):
    nt = t // ROW_TILE
    per = ROW_TILE // SUBLANES
    cur = pl.BlockSpec((ROW_TILE, CONV_COLS), lambda j, i: (i, j))
    prev = pl.BlockSpec((SUBLANES, CONV_COLS), lambda j, i: (jnp.maximum(i * per - 1, 0), j))
    nxt = pl.BlockSpec((SUBLANES, CONV_COLS), lambda j, i: (jnp.minimum((i + 1) * per, nt * per - 1), j))
    return cur, prev, nxt


def _fill_ext(ext, prev_ref, cur_ref, next_ref, i, nt, n_ctx_tiles):
    has_prev = jnp.logical_and(i != 0, i != n_ctx_tiles)
    has_next = jnp.logical_and(i != n_ctx_tiles - 1, i != nt - 1)
    ext[0:SUBLANES, :] = jnp.where(has_prev, prev_ref[...], 0.0)
    ext[SUBLANES:SUBLANES + ROW_TILE, :] = cur_ref[...]
    ext[SUBLANES + ROW_TILE:, :] = jnp.where(has_next, next_ref[...], 0.0)


def conv_fwd(name, p, w, width, n_ctx_tiles):
    t = p.shape[0]
    nt = t // ROW_TILE
    cur, prev, nxt = _conv_halo_specs(t, n_ctx_tiles)

    def body(c_ref, p_ref, n_ref, w_ref, o_ref, ext):
        _fill_ext(ext, p_ref, c_ref, n_ref, pl.program_id(1), nt, n_ctx_tiles)
        acc = None
        for j in range(C_CONV):
            term = ext[pl.ds(SUBLANES + j - CONV_LEFT, ROW_TILE), :] * w_ref[j:j + 1, :]
            acc = term if acc is None else acc + term
        o_ref[...] = acc

    return pl.pallas_call(
        body, grid=(width // CONV_COLS, nt), name=name,
        in_specs=[cur, prev, nxt, pl.BlockSpec((C_CONV, CONV_COLS), lambda j, i: (0, j))],
        out_specs=cur, out_shape=_sds((t, width), F32),
        scratch_shapes=[pltpu.VMEM((ROW_TILE + 2 * SUBLANES, CONV_COLS), F32)],
        compiler_params=_cparams(2),
    )(p, p, p, w)


def conv_bwd(name, p, w, dz, width, n_ctx_tiles, into):
    t = p.shape[0]
    nt = t // ROW_TILE
    cur, prev, nxt = _conv_halo_specs(t, n_ctx_tiles)

    def body(c_ref, p_ref, n_ref, dc_ref, dp_ref, dn_ref, w_ref, into_ref, du_ref, dw_ref, ext, dext):
        i = pl.program_id(1)
        _fill_ext(ext, p_ref, c_ref, n_ref, i, nt, n_ctx_tiles)
        _fill_ext(dext, dp_ref, dc_ref, dn_ref, i, nt, n_ctx_tiles)
        dzc = dc_ref[...]
        @pl.when(i == 0)
        def _():
            dw_ref[...] = jnp.zeros_like(dw_ref)

        acc = None
        for j in range(C_CONV):
            term = dext[pl.ds(SUBLANES + CONV_LEFT - j, ROW_TILE), :] * w_ref[j:j + 1, :]
            acc = term if acc is None else acc + term
            dw_ref[j:j + 1, :] += jnp.sum(dzc * ext[pl.ds(SUBLANES + j - CONV_LEFT, ROW_TILE), :], axis=0, keepdims=True)
        du_ref[...] = acc.astype(du_ref.dtype)

    wspec = pl.BlockSpec((C_CONV, CONV_COLS), lambda j, i: (0, j))
    return pl.pallas_call(
        body, grid=(width // CONV_COLS, nt), name=name,
        in_specs=[cur, prev, nxt, cur, prev, nxt, wspec, pl.BlockSpec(memory_space=pl.ANY)],
        out_specs=[cur, wspec], out_shape=[_sds(into.shape, into.dtype), _sds((C_CONV, width), F32)],
        input_output_aliases={7: 0},
        scratch_shapes=[pltpu.VMEM((ROW_TILE + 2 * SUBLANES, CONV_COLS), F32)] * 2,
        compiler_params=_cparams(2),
    )(p, p, p, dz, dz, dz, w, into)


def _rms(x, w):
    return x * lax.rsqrt(jnp.mean(x * x, axis=-1, keepdims=True) + EPS) * w


def _seg_mod(mods, is_ctx):
    return jnp.where(is_ctx, mods[0], mods[1])


def f_norm_mod(shift_i, scale_i, n_ctx_tiles, passthrough=False, transposed=False):
    def f(pids, x, nw, mods):
        m = _seg_mod(mods, pids[0] < n_ctx_tiles)
        h = _rms(x, nw) * (1.0 + m[scale_i:scale_i + 1]) + m[shift_i:shift_i + 1]
        if transposed:
            return (h, h.T)
        return (h, x) if passthrough else (h,)
    return f


def f_gate_res(gate_i, n_ctx_tiles):
    def f(pids, x, y, nw, mods):
        m = _seg_mod(mods, pids[0] < n_ctx_tiles)
        return (x + m[gate_i:gate_i + 1] * _rms(y, nw),)
    return f


def f_res_norm(gate_i, shift_i, scale_i, n_ctx_tiles, transposed=False):
    def f(pids, x, y, nw_g, mods_g, nw_n, mods_n):
        is_ctx = pids[0] < n_ctx_tiles
        mg, mn = _seg_mod(mods_g, is_ctx), _seg_mod(mods_n, is_ctx)
        x1 = x + mg[gate_i:gate_i + 1] * _rms(y, nw_g)
        h = _rms(x1, nw_n) * (1.0 + mn[scale_i:scale_i + 1]) + mn[shift_i:shift_i + 1]
        return (x1, h, h.T) if transposed else (x1, h)
    return f


def f_swiglu(pids, gu):
    half = gu.shape[1] // 2
    return (jax.nn.silu(gu[:, :half]) * gu[:, half:],)


def f_readout(n_heads):
    def f(pids, o_a, o_b, gate, nw):
        cols = [slice(j * HEAD, (j + 1) * HEAD) for j in range(n_heads)]
        outs = _each(lambda cs: _rms(o_a[:, cs] + o_b[:, cs], nw) * jax.nn.silu(gate[:, cs]), cols)
        return (jnp.concatenate(outs, axis=1) if n_heads > 1 else outs[0],)
    return f


def f_pool(pids, u, pmat, pw, scale):
    d = mm(pmat, u, 1, 0, HI_BOTH) - u
    return (mm(d, pw) * scale,)


def f_lb(layer):
    def f(pids, *slots):
        top = slots[0]
        for s in slots[1:]:
            top = jnp.maximum(top, s)
        ex = [jnp.exp(s - top) for s in slots]
        tot = ex[0]
        for e in ex[1:]:
            tot = tot + e
        part = ex[0]
        for e in ex[1:layer + 1]:
            part = part + e
        return (part / tot,)
    return f


def f_ada(pids, c_all, c_ctx, w, b):
    c16 = jnp.concatenate([c_all, jnp.broadcast_to(c_ctx, c_all.shape)], axis=0)
    return (mm(jax.nn.silu(c16), w) + b,)


def _each(fn, *lists):
    return [fn(*xs) for xs in zip(*lists)]


def _hgrn2_heads(sts, qrs, frs, irs, lbs, rev):
    c = A_CHUNK
    mid = c - c // 2 if rev else c // 2 - 1
    ri, ci = _iota2(c, c, 0), _iota2(c, c, 1)
    incl = (ri <= ci) if rev else (ri >= ci)
    incl_f = incl.astype(F32)
    qs = _each(jax.nn.silu, qrs)
    log_fs = _each(lambda lb, fr: jnp.log(lb + (1.0 - lb) * jax.nn.sigmoid(fr)), lbs, frs)
    ks = _each(lambda lb, fr: (1.0 - lb) * jax.nn.sigmoid(-fr), lbs, frs)
    bs = _each(lambda lf: mm(incl_f, lf, 1, 0, HI_BOTH), log_fs)
    b_lasts = _each(lambda lf: jnp.sum(lf, axis=0, keepdims=True), log_fs)
    scores = _each(lambda q, k, b: mm(q * jnp.exp(b - b[mid:mid + 1]), k * jnp.exp(b[mid:mid + 1] - b), 1, 1), qs, ks, bs)
    intra = _each(lambda sc, ir: mm(jnp.where(incl, sc, 0.0), ir), scores, irs)
    inter = _each(lambda q, b, st: mm(q * jnp.exp(b), st, 1, 1), qs, bs, sts)
    upd = _each(lambda ir, k, bl, b: mm(ir, k * jnp.exp(bl - b), 0, 0), irs, ks, b_lasts, bs)
    new = _each(lambda st, bl, u: st * jnp.exp(bl) + u, sts, b_lasts, upd)
    return new, _each(jnp.add, intra, inter)


def f_hgrn2(rev, hb):
    def f(states, qr, fr, ir, lb):
        cols = [slice(j * HEAD, (j + 1) * HEAD) for j in range(hb)]
        new, outs = _hgrn2_heads(list(states), [qr[:, cs] for cs in cols], [fr[:, cs] for cs in cols],
                                 [ir[:, cs] for cs in cols], [lb[j] for j in range(hb)], rev)
        return tuple(new), (jnp.concatenate(outs, axis=1) if hb > 1 else outs[0],)
    return f


def _neumann_inv(a_lows):
    n = a_lows[0].shape[0]
    eye = (_iota2(n, n, 0) == _iota2(n, n, 1)).astype(F32)
    ps = _each(lambda a: -a, a_lows)
    xs = _each(lambda p: eye + p, ps)
    ps = _each(lambda p: mm(p, p, 1, 0, HI_FWD), ps)
    k = 2
    while 2 * k < n:
        ys = _each(lambda p, x: mm(jnp.concatenate([p, x], axis=0), p, 1, 0, HI_FWD), ps, xs)
        ps = _each(lambda y: y[:n], ys)
        xs = _each(lambda x, y: x + y[n:], xs, ys)
        k *= 2
    return tuple(_each(lambda x, p: x + mm(x, p, 1, 0, HI_FWD), xs, ps))


@jax.custom_vjp
def unit_tri_inv(a_lows):
    return _neumann_inv(a_lows)


def _uti_fwd(a_lows):
    xs = _neumann_inv(a_lows)
    return xs, xs


def _uti_bwd(xs, gs):
    ts = _each(lambda x, g: mm(x, g, 0, 0), xs, gs)
    return (tuple(_each(lambda t, x: -mm(t, x, 1, 1), ts, xs)),)


unit_tri_inv.defvjp(_uti_fwd, _uti_bwd)


@jax.custom_vjp
def unit_tri_inv_saved(a_lows, xs):
    return xs


def _utis_fwd(a_lows, xs):
    return xs, xs


def _utis_bwd(xs, gs):
    return _uti_bwd(xs, gs) + (tuple(jnp.zeros_like(x) for x in xs),)


unit_tri_inv_saved.defvjp(_utis_fwd, _utis_bwd)


def _l2n(x):
    return x * lax.rsqrt(jnp.sum(x * x, axis=-1, keepdims=True) + EPS)


def _gdn_heads(ss, qs, ks, vs, a_rows, b_rows, alogs, dtbs, rev, xs_saved=None):
    c = qs[0].shape[0]
    ri, ci = _iota2(c, c, 0), _iota2(c, c, 1)
    causal = (ri <= ci) if rev else (ri >= ci)
    causal_t = (ri >= ci) if rev else (ri <= ci)
    strict = (ri < ci) if rev else (ri > ci)
    eye = ri == ci
    sq = lambda row: jnp.broadcast_to(row, (c, c))
    to_col = lambda row: jnp.sum(jnp.where(eye, sq(row), 0.0), axis=1, keepdims=True)
    g_rows = _each(lambda al, a, dt: -jnp.exp(al) * jax.nn.softplus(a + dt), alogs, a_rows, dtbs)
    beta_cols = _each(lambda b: to_col(jax.nn.sigmoid(b)), b_rows)
    g_cols = _each(to_col, g_rows)
    gc_cols = _each(lambda g: jnp.sum(jnp.where(causal, sq(g), 0.0), axis=1, keepdims=True), g_rows)
    gc_rows = _each(lambda g: jnp.sum(jnp.where(causal_t, sq(g), 0.0), axis=0, keepdims=True), g_cols)
    gc_lasts = _each(lambda g: jnp.sum(g, axis=1, keepdims=True), g_rows)
    decays = _each(lambda gc, gr: jnp.exp(jnp.where(causal, gc - gr, MASKED_EXPONENT)), gc_cols, gc_rows)
    k_betas = _each(jnp.multiply, ks, beta_cols)
    v_betas = _each(jnp.multiply, vs, beta_cols)
    kq_ks = _each(lambda kb, q, k: mm(jnp.concatenate([kb, q], axis=0), k, 1, 1), k_betas, qs, ks)
    a_lows = _each(lambda kk, dec: jnp.where(strict, kk[:c] * dec, 0.0), kq_ks, decays)
    qks = _each(lambda kk, dec: kk[c:] * dec, kq_ks, decays)
    xs = unit_tri_inv(tuple(a_lows)) if xs_saved is None else unit_tri_inv_saved(tuple(a_lows), tuple(xs_saved))
    egcs = _each(jnp.exp, gc_cols)
    uws = _each(lambda x, vb, kb, e: mm(x, jnp.concatenate([vb, kb * e], axis=1), 1, 0, HI_FWD), xs, v_betas, k_betas, egcs)
    dv = vs[0].shape[1]
    wq_ss = _each(lambda uw, q, e, s: mm(jnp.concatenate([uw[:, dv:], q * e], axis=0), s), uws, qs, egcs, ss)
    v_news = _each(lambda uw, wq: uw[:, :dv] - wq[:c], uws, wq_ss)
    o_states = _each(lambda wq: wq[c:], wq_ss)
    o_locals = _each(mm, qks, v_news)
    upds = _each(lambda k, gl, gc, vn: mm(k * jnp.exp(gl - gc), vn, 0, 0), ks, gc_lasts, gc_cols, v_news)
    new = _each(lambda s, gl, u: s * jnp.exp(gl) + u, ss, gc_lasts, upds)
    return new, _each(jnp.add, o_states, o_locals), xs


def f_gdn(rev, khb, saved_inverse):
    def f(states, qr, kr, vr, a3, b3, alog3, dtb3, xcat=None):
        heads = [(j, r) for j in range(khb) for r in range(2)]
        cols = [slice(j * HEAD, (j + 1) * HEAD) for j in range(khb)]
        c = qr.shape[0]
        qk_ = _each(lambda cs: (_l2n(jax.nn.silu(qr[:, cs])) * (HEAD ** -0.5), _l2n(jax.nn.silu(kr[:, cs]))), cols)
        vs = [jax.nn.silu(vr[:, (2 * j + r) * HEAD:(2 * j + r + 1) * HEAD]) for j, r in heads]
        row = lambda arr3: [arr3[j][r:r + 1] for j, r in heads]
        xs_saved = [xcat[n * c:(n + 1) * c] for n in range(len(heads))] if saved_inverse else None
        new, outs, xs = _gdn_heads(list(states), [qk_[j][0] for j, _ in heads], [qk_[j][1] for j, _ in heads], vs,
                                   row(a3), row(b3), row(alog3), row(dtb3), rev, xs_saved)
        o = jnp.concatenate(outs, axis=1)
        return tuple(new), ((o,) if saved_inverse else (o, jnp.concatenate(xs, axis=0)))
    return f


def _hbm_spec():
    return pl.BlockSpec(memory_space=pltpu.HBM)


def all_gather(name, xs):
    nt = len(xs)

    def body(*refs):
        x_refs, out_refs = refs[:nt], refs[nt:2 * nt]
        send_sems, recv_sems, local_sems = refs[2 * nt:]
        x, y, c = lax.axis_index("x"), lax.axis_index("y"), lax.axis_index("c")
        me, sibling = (x, y, c), (x, y, 1 - c)
        chips = [(1 - x, y), (x, 1 - y), (1 - x, 1 - y)]

        def slab(t, px, py, pc):
            return out_refs[t].at[4 * px + 2 * py + pc]

        def copy(t, k, block, to, src=None):
            return pltpu.make_async_remote_copy(
                src_ref=slab(t, *block) if src is None else src, dst_ref=slab(t, *block),
                send_sem=send_sems.at[7 * t + k], recv_sem=recv_sems.at[7 * t + k], device_id=to, device_id_type=MESH)

        mine, first, passed = [], [], []
        for t in range(nt):
            mine.append(pltpu.make_async_copy(x_refs[t], slab(t, *me), local_sems.at[t]))
            mine[-1].start()
            cps = [copy(t, 0, me, sibling, src=x_refs[t])]
            cps += [copy(t, 1 + j, me, (*chip, c), src=x_refs[t]) for j, chip in enumerate(chips)]
            for cp in cps:
                cp.start()
            first += cps
        for j, chip in enumerate(chips):
            for t in range(nt):
                copy(t, 1 + j, (*chip, c), me).wait_recv()
                fw = copy(t, 4 + j, (*chip, c), sibling)
                fw.start()
                passed.append(fw)
        for t in range(nt):
            copy(t, 0, sibling, me).wait_recv()
            for j, chip in enumerate(chips):
                copy(t, 4 + j, (*chip, 1 - c), me).wait_recv()
        for cp in first + passed:
            cp.wait_send()
        for cp in mine:
            cp.wait()

    return pl.pallas_call(
        body, name=name, out_shape=[_sds((N_DEV,) + a.shape, a.dtype) for a in xs],
        in_specs=[_hbm_spec()] * nt, out_specs=[_hbm_spec()] * nt,
        scratch_shapes=[pltpu.SemaphoreType.DMA((7 * nt,)), pltpu.SemaphoreType.DMA((7 * nt,)),
                        pltpu.SemaphoreType.DMA((nt,))],
    )(*xs)


def _peers(x, y, c):
    out = []
    for k in range(1, N_DEV):
        px = 1 - x if k & 4 else x
        py = 1 - y if k & 2 else y
        pc = 1 - c if k & 1 else c
        out.append((px, py, pc))
    return out


def _exchange_copies(src_refs, land_refs, send_sems, recv_sems, scatter):
    x, y, c = lax.axis_index("x"), lax.axis_index("y"), lax.axis_index("c")
    me = 4 * x + 2 * y + c
    sends, recvs = [], []
    for t, (src, land) in enumerate(zip(src_refs, land_refs)):
        for k, (px, py, pc) in enumerate(_peers(x, y, c)):
            peer = 4 * px + 2 * py + pc
            sem = dict(send_sem=send_sems.at[7 * t + k], recv_sem=recv_sems.at[7 * t + k],
                       device_id=(px, py, pc), device_id_type=MESH)
            src_k = src.at[peer] if scatter else src
            sends.append(pltpu.make_async_remote_copy(src_ref=src_k, dst_ref=land.at[me], **sem))
            recvs.append(pltpu.make_async_remote_copy(src_ref=src_k, dst_ref=land.at[peer], **sem))
    return sends, recvs


def exchange_start(name, srcs, scatter, after, carry=None):
    nt = len(srcs)
    lands = [lax.empty(s.shape if scatter else (N_DEV,) + s.shape, s.dtype) for s in srcs]
    thru = list(srcs) + lands + ([carry] if carry is not None else [])
    n_thru = len(thru)

    def body(*refs):
        src_refs, land_refs = refs[:nt], refs[nt:2 * nt]
        send_sems, recv_sems = refs[n_thru + 1], refs[n_thru + 2]
        token = refs[-1]
        sends, _ = _exchange_copies(src_refs, land_refs, send_sems, recv_sems, scatter)
        for cp in sends:
            cp.start()
        token[...] = jnp.zeros_like(token)

    res = pl.pallas_call(
        body, name=name,
        out_shape=(pltpu.SemaphoreType.DMA((7 * nt,)), pltpu.SemaphoreType.DMA((7 * nt,)),
                   *[pltpu.HBM(a.shape, a.dtype) for a in thru], _sds((SUBLANES, LANES), F32)),
        in_specs=[_hbm_spec()] * n_thru + [pl.BlockSpec(memory_space=pl.ANY)],
        out_specs=(pl.BlockSpec(memory_space=pltpu.SEMAPHORE), pl.BlockSpec(memory_space=pltpu.SEMAPHORE),
                   *[_hbm_spec()] * n_thru, pl.BlockSpec(memory_space=pltpu.VMEM)),
        input_output_aliases={i: 2 + i for i in range(n_thru)},
        compiler_params=pltpu.CompilerParams(has_side_effects=pltpu.SideEffectType.DATAFLOW_SIDE_EFFECTING),
    )(*[pltpu.with_memory_space_constraint(a, pltpu.HBM) for a in thru], after)
    return (res[0], res[1], list(res[2:2 + nt]), list(res[2 + nt:2 + 2 * nt]), res[-1],
            res[2 + 2 * nt] if carry is not None else None)


def exchange_wait(name, started, after, scatter):
    send_sems, recv_sems, srcs, lands = started[:4]
    nt = len(srcs)

    def body(*refs):
        src_refs, land_refs = refs[:nt], refs[nt:2 * nt]
        sends, recvs = _exchange_copies(src_refs, land_refs, refs[2 * nt], refs[2 * nt + 1], scatter)
        for cp in sends:
            cp.wait_send()
        for cp in recvs:
            cp.wait_recv()

    hbm = lambda a: pltpu.HBM(a.shape, a.dtype)
    sem = pl.BlockSpec(memory_space=pltpu.SEMAPHORE)
    res = pl.pallas_call(
        body, name=name,
        out_shape=(*[hbm(a) for a in srcs], *[hbm(a) for a in lands]),
        in_specs=[_hbm_spec()] * (2 * nt) + [sem, sem, pl.BlockSpec(memory_space=pl.ANY)],
        out_specs=tuple([_hbm_spec()] * (2 * nt)),
        input_output_aliases={i: i for i in range(2 * nt)},
        compiler_params=pltpu.CompilerParams(has_side_effects=pltpu.SideEffectType.DATAFLOW_SIDE_EFFECTING),
    )(*srcs, *lands, send_sems, recv_sems, after)
    return list(res[:nt]), list(res[nt:])


def _pack(arrs, dtype, width, row_mult, lead=0):
    ld = arrs[0].shape[:lead]
    flat = jnp.concatenate([a.reshape(ld + (-1,)).astype(dtype) for a in arrs], axis=-1)
    n = flat.shape[-1]
    q = width * row_mult
    npad = -(-n // q) * q
    flat = jnp.pad(flat, [(0, 0)] * lead + [(0, npad - n)])
    return flat.reshape(ld + (npad // width, width))


def _unpack(buf, shapes, lead=0):
    ld = buf.shape[:lead]
    flat = buf.reshape(ld + (-1,))
    out, off = [], 0
    for s in shapes:
        n = int(np.prod(s))
        out.append(flat[..., off:off + n].reshape(ld + tuple(s)))
        off += n
    return out


def _pool_mats(seg_len):
    mats = np.zeros((len(POOL_WINDOWS), ROW_TILE, ROW_TILE), np.float32)
    for gi, win in enumerate(POOL_WINDOWS):
        for p in range(ROW_TILE):
            base = (p // seg_len) * seg_len
            q = p - base
            lo = min(max(q - win // 2, 0), seg_len - 1)
            hi = min(max(q + win - 1 - win // 2, 0), seg_len - 1)
            mats[gi, p, base + lo:base + hi + 1] = 1.0 / (hi - lo + 1)
    return mats


def kernel(x, c, ctx, c_ctx, w_ada, b_ada, norm_w, ev_w_in, ev_lb, ev_a_norm, ev_pool_w, ev_pool_scale, ev_w_out, od_w_in, od_conv, od_A_log, od_dt_bias, od_norm, od_w_out, ffn_w13, ffn_w2, loss_target, m_c_ctx, m_w_ada, m_b_ada, m_norm_w, m_ev_w_in, m_ev_lb, m_ev_a_norm, m_ev_pool_w, m_ev_pool_scale, m_ev_w_out, m_od_w_in, m_od_conv, m_od_A_log, m_od_dt_bias, m_od_norm, m_od_w_out, m_ffn_w13, m_ffn_w2, v_c_ctx, v_w_ada, v_b_ada, v_norm_w, v_ev_w_in, v_ev_lb, v_ev_a_norm, v_ev_pool_w, v_ev_pool_scale, v_ev_w_out, v_od_w_in, v_od_conv, v_od_A_log, v_od_dt_bias, v_od_norm, v_od_w_out, v_ffn_w13, v_ffn_w2):
    names = ["c_ctx", "w_ada", "b_ada", "norm_w", "ev_w_in", "ev_lb", "ev_a_norm", "ev_pool_w", "ev_pool_scale",
             "ev_w_out", "od_w_in", "od_conv", "od_A_log", "od_dt_bias", "od_norm", "od_w_out", "ffn_w13", "ffn_w2"]
    wts = dict(zip(names, [c_ctx, w_ada, b_ada, norm_w, ev_w_in, ev_lb, ev_a_norm, ev_pool_w, ev_pool_scale,
                           ev_w_out, od_w_in, od_conv, od_A_log, od_dt_bias, od_norm, od_w_out, ffn_w13, ffn_w2]))
    mom1 = dict(zip(names, [m_c_ctx, m_w_ada, m_b_ada, m_norm_w, m_ev_w_in, m_ev_lb, m_ev_a_norm, m_ev_pool_w,
                            m_ev_pool_scale, m_ev_w_out, m_od_w_in, m_od_conv, m_od_A_log, m_od_dt_bias, m_od_norm,
                            m_od_w_out, m_ffn_w13, m_ffn_w2]))
    mom2 = dict(zip(names, [v_c_ctx, v_w_ada, v_b_ada, v_norm_w, v_ev_w_in, v_ev_lb, v_ev_a_norm, v_ev_pool_w,
                            v_ev_pool_scale, v_ev_w_out, v_od_w_in, v_od_conv, v_od_A_log, v_od_dt_bias, v_od_norm,
                            v_od_w_out, v_ffn_w13, v_ffn_w2]))

    ax, ay, ac = lax.axis_index("x"), lax.axis_index("y"), lax.axis_index("c")
    me = 4 * ax + 2 * ay + ac

    seq, d = x.shape[1], x.shape[2]
    n_ctx = ctx.shape[1]
    t = n_ctx + seq
    nt = t // ROW_TILE
    nct = n_ctx // ROW_TILE
    assert n_ctx == ROW_TILE and seq % ROW_TILE == 0 and ROW_TILE % GRID_W == 0
    depth = w_ada.shape[0]
    aw = d // 2
    n_ah = aw // HEAD
    n_grp = len(POOL_WINDOWS)
    dg = aw // n_grp
    assert dg % LANES == 0
    n_kh = d // HEAD
    kw, vw = n_kh * HEAD, 2 * n_kh * HEAD
    n_gate = 8 * n_kh
    ffn_h = ffn_w2.shape[1] * N_DEV
    ada_loc = w_ada.shape[2]
    assert depth == 2

    bf = lambda w_: w_.astype(BF16)

    def gathered(started, after, name):
        srcs, lands = exchange_wait(name, started, after, False)
        return [lax.dynamic_update_index_in_dim(l_, s_, me, 0) for l_, s_ in zip(lands, srcs)]

    small_shapes = [(d,), norm_w.shape, ev_lb.shape, ev_pool_w.shape[1:], od_conv.shape[1:]]
    (g1,) = all_gather("ag_small", [_pack([c[0], norm_w, ev_lb, ev_pool_w[0], od_conv[0]], F32, LANES, SUBLANES)])
    c_all, nw_g, lb_g, pw_g, cv_g = _unpack(g1, small_shapes, lead=1)
    nw_full = nw_g.transpose(1, 2, 0, 3).reshape(depth, 4, d)
    lb_full = lb_g.transpose(1, 2, 0, 3).reshape(2, depth + 1, aw)
    pw_full = pw_g.transpose(1, 0, 2, 3).reshape(n_grp, dg, dg)
    cv_full = cv_g.transpose(1, 0, 2).reshape(C_CONV, 2 * kw + vw)

    g_ev_in, g_ev_out = all_gather("ag_weights_ev", [bf(ev_w_in[0]), bf(ev_w_out[0])])

    def cols_natural(g):
        return g.transpose(1, 0, 2).reshape(g.shape[1], N_DEV * g.shape[2])

    def rows_natural(g):
        return g.reshape(N_DEV * g.shape[1], g.shape[2])

    def col_weight(g):
        return (g, True) if g.shape[2] % LANES == 0 else (cols_natural(g), False)

    w_ev_in = col_weight(g_ev_in)
    w_ev_out = rows_natural(g_ev_out)
    w13, w2 = [None] * depth, [None] * depth

    b_loc = lax.dynamic_slice_in_dim(b_ada, me * ada_loc, ada_loc, axis=1).reshape(depth, 1, ada_loc)
    ada_cb = _pick(ada_loc, TN_PREFS)
    ada_grid = (depth, ada_loc // ada_cb)
    ada_args = [
        TArg(c_all, (N_DEV, d), lambda l, j: (0, 0), "const"),
        TArg(c_ctx.reshape(1, d), (1, d), lambda l, j: (0, 0), "par", (0, 1)),
        TArg(w_ada, (None, d, ada_cb), lambda l, j: (l, 0, j)),
        TArg(b_loc, (None, 1, ada_cb), lambda l, j: (l, 0, j)),
    ]
    (m_loc,) = tile_fwd("ada_fwd", f_ada, ada_grid, ada_args,
                        [((depth, 2 * N_DEV, ada_loc), F32, (None, 2 * N_DEV, ada_cb), lambda l, j: (l, 0, j))])
    (m_all,) = all_gather("ag_mod", [m_loc])
    mods = []
    for layer in range(depth):
        lat = lax.dynamic_index_in_dim(m_all[:, layer], me, axis=1, keepdims=False).reshape(6, d)
        cxt = lax.dynamic_index_in_dim(m_all[:, layer], N_DEV + me, axis=1, keepdims=False).reshape(6, d)
        mods.append(jnp.stack([cxt, lat]))

    gathers_done = mods[0][0, :1, :SUBLANES] + g_ev_out[0, :1, :SUBLANES].astype(F32)
    ag_ffn0 = exchange_start("ag_start_ffn0", [bf(ffn_w13[0]), bf(ffn_w2[0])], False, gathers_done)
    ag_l1 = exchange_start("ag_start_l1", [bf(od_w_in[0]), bf(od_w_out[0]), bf(ffn_w13[1]), bf(ffn_w2[1])], False,
                           gathers_done)
    mods[0] = mods[0] + (ag_ffn0[4][0, 0] + ag_l1[4][0, 0])

    lb_slots = [TArg(lb_full[:, j], (2, aw), lambda i: (0, 0)) for j in range(depth + 1)]
    (lb0,) = tile_fwd("lb_fwd", f_lb(0), (1,), lb_slots, [((2, aw), F32, (2, aw), lambda i: (0, 0))])
    lb0r = lb0.reshape(2, n_ah, 1, HEAD)

    full_row = lambda i: (i, 0)
    par0 = lambda i: (0, 0)

    def nm_args(xs, layer, slot):
        return [TArg(xs, (ROW_TILE, d), full_row),
                TArg(nw_full[layer, slot].reshape(1, d), (1, d), par0, "par", (0,)),
                TArg(mods[layer], (2, 6, d), lambda i: (0, 0, 0), "par", (0,))]

    def norm_mod(name, xs, layer, slot, si, ci):
        return tile_fwd(name, f_norm_mod(si, ci, nct, transposed=True), (nt,), nm_args(xs, layer, slot),
                        [((t, d), BF16, (ROW_TILE, d), full_row), ((d, t), BF16, (d, ROW_TILE), lambda i: (0, i))])

    def norm_mod_bwd(name, xs, layer, slot, si, ci, dh, carry):
        return tile_bwd(name, f_norm_mod(si, ci, nct, True), (nt,), nm_args(xs, layer, slot),
                        [(dh, (ROW_TILE, d), full_row), (carry, (ROW_TILE, d), full_row)])

    def gr_args(xs, ys, layer, slot):
        return [TArg(xs, (ROW_TILE, d), full_row, grad=False), TArg(ys, (ROW_TILE, d), full_row, gdtype=BF16),
                TArg(nw_full[layer, slot].reshape(1, d), (1, d), par0, "par", (0,)),
                TArg(mods[layer], (2, 6, d), lambda i: (0, 0, 0), "par", (0,))]

    def gate_res(name, xs, ys, layer, slot, gi):
        (o,) = tile_fwd(name, f_gate_res(gi, nct), (nt,), gr_args(xs, ys, layer, slot),
                        [((t, d), F32, (ROW_TILE, d), full_row)])
        return o

    def gate_res_bwd(name, xs, ys, layer, slot, gi, dx):
        return tile_bwd(name, f_gate_res(gi, nct), (nt,), gr_args(xs, ys, layer, slot),
                        [(dx, (ROW_TILE, d), full_row)])

    def rn_args(xs, ys, gate, norm):
        return [TArg(xs, (ROW_TILE, d), full_row), TArg(ys, (ROW_TILE, d), full_row, gdtype=BF16),
                TArg(nw_full[gate[0], gate[1]].reshape(1, d), (1, d), par0, "par", (0,)),
                TArg(mods[gate[0]], (2, 6, d), lambda i: (0, 0, 0), "par", (0,)),
                TArg(nw_full[norm[0], norm[1]].reshape(1, d), (1, d), par0, "par", (0,)),
                TArg(mods[norm[0]], (2, 6, d), lambda i: (0, 0, 0), "par", (0,))]

    def res_norm(name, xs, ys, gate, gi, norm, si, ci):
        return tile_fwd(name, f_res_norm(gi, si, ci, nct, transposed=True), (nt,), rn_args(xs, ys, gate, norm),
                        [((t, d), F32, (ROW_TILE, d), full_row), ((t, d), BF16, (ROW_TILE, d), full_row),
                         ((d, t), BF16, (d, ROW_TILE), lambda i: (0, i))])

    def res_norm_bwd(name, xs, ys, gate, gi, norm, si, ci, carry, dh, acc):
        dx, dy, dnw_g, dmod_g, dnw_n, dmod_n = tile_bwd(
            name, f_res_norm(gi, si, ci, nct), (nt,), rn_args(xs, ys, gate, norm),
            [(carry, (ROW_TILE, d), full_row), (dh, (ROW_TILE, d), full_row)])
        acc["norm_w"][gate[0]][gate[1]] = dnw_g
        acc["norm_w"][norm[0]][norm[1]] = dnw_n
        acc["mods"][gate[0]].append(dmod_g)
        acc["mods"][norm[0]].append(dmod_n)
        return dx, dy

    sw_rows = ROW_TILE // 2

    def sw_args(gu):
        return [TArg(gu, (sw_rows, 2 * ffn_h), full_row, gdtype=BF16)]

    def ffn_fwd(tag, h2, h2t, layer):
        gu = matmul(f"w13_{tag}", h2, w13[layer][0], "nn", slabs=w13[layer][1], out_dtype=BF16)
        (act,) = tile_fwd(f"swiglu_{tag}", f_swiglu, (t // sw_rows,), sw_args(gu),
                          [((t, ffn_h), BF16, (sw_rows, ffn_h), full_row)])
        return matmul(f"w2_{tag}", act, w2[layer], "nn"), (h2t, gu, act)

    def wgrad(name, a, dy, slabs=False, a_t=None):
        if a_t is not None:
            return matmul(name, a_t, dy, "nn", out_slabs=slabs, out_dtype=BF16)
        return matmul(name, a, dy, "tn", slabs=slabs, out_dtype=BF16)

    def col_grad(name, a, dy, slabs, a_t=None):
        g = wgrad(name, a, dy, slabs, a_t)
        return g if slabs else g.reshape(g.shape[0], N_DEV, g.shape[1] // N_DEV).transpose(1, 0, 2)

    def row_grad(name, a, dy):
        g = matmul(name, a, dy, "tn", out_dtype=BF16)
        return g.reshape(N_DEV, g.shape[0] // N_DEV, g.shape[1])

    def ffn_bwd(tag, saved, layer, dfo, acc):
        h2t, gu, act = saved
        dact = matmul(f"w2d_{tag}", dfo, w2[layer], "nt")
        acc["ffn_w2"][layer] = row_grad(f"w2w_{tag}", act, dfo)
        (dgu,) = tile_bwd(f"swiglub_{tag}", f_swiglu, (t // sw_rows,), sw_args(gu), [(dact, (sw_rows, ffn_h), full_row)])
        acc["ffn_w13"][layer] = col_grad(f"w13w_{tag}", None, dgu, w13[layer][1], h2t)
        return matmul(f"w13d_{tag}", dgu, w13[layer][0], "nt", slabs=w13[layer][1])

    n_a = t // A_CHUNK
    nca = n_ctx // A_CHUNK

    def a_tok(rev):
        if not rev:
            return lambda i: i
        return lambda i: jnp.where(i < nca, nca - 1 - i, n_a + nca - 1 - i)

    hb = _pick(n_ah, (HEADS_PER_STEP, 2, 1))
    n_hblk = n_ah // hb

    def hg_args(p, direction):
        tok = a_tok(direction == 1)
        blk = (A_CHUNK, hb * HEAD)
        return [TArg(p, blk, lambda h, i: (tok(i), h)),
                TArg(p, blk, lambda h, i: (tok(i), (1 + direction) * n_hblk + h)),
                TArg(p, blk, lambda h, i: (tok(i), 3 * n_hblk + h)),
                TArg(lb0r, (None, hb, 1, HEAD), lambda h, i: (direction, h, 0, 0), "par", (1,))]

    pmats = jnp.asarray(np.stack([_pool_mats(n_ctx), _pool_mats(GRID_W)]))

    def pool_args(p):
        return [TArg(p, (ROW_TILE, dg), lambda g, i: (i, 5 * n_grp + g)),
                TArg(pmats, (None, None, ROW_TILE, ROW_TILE), lambda g, i: (jnp.where(i < nct, 0, 1), g, 0, 0), "const"),
                TArg(pw_full, (None, dg, dg), lambda g, i: (g, 0, 0), "par", (1,)),
                TArg(ev_pool_scale, (1, dg), lambda g, i: (0, g), "par", (1,))]

    def ro_plan(gate_off, n_heads):
        per = _pick(n_heads, (8, 4, 2, 1))
        assert gate_off % per == 0
        return per, n_heads // per, gate_off // per

    def ro_args(o_f, o_b, gate_arr, gate_off, nw_arr, n_heads):
        per, _, goff = ro_plan(gate_off, n_heads)
        blk = (ROW_TILE, per * HEAD)
        return [TArg(o_f, blk, lambda h, i: (i, h)), TArg(o_b, blk, lambda h, i: (i, h), grad=False),
                TArg(gate_arr, blk, lambda h, i: (i, goff + h), gdtype=BF16),
                TArg(nw_arr, (1, HEAD), lambda h, i: (0, 0), "par", (0, 1))]

    def readout(name, o_f, o_b, gate_arr, gate_off, nw_arr, n_heads):
        per, nblk, _ = ro_plan(gate_off, n_heads)
        (o,) = tile_fwd(name, f_readout(per), (nblk, nt), ro_args(o_f, o_b, gate_arr, gate_off, nw_arr, n_heads),
                        [((t, n_heads * HEAD), BF16, (ROW_TILE, per * HEAD), lambda hh, i: (i, hh))])
        return o

    def readout_bwd(name, o_f, o_b, gate_arr, gate_off, nw_arr, n_heads, dout):
        per, nblk, _ = ro_plan(gate_off, n_heads)
        return tile_bwd(name, f_readout(per), (nblk, nt), ro_args(o_f, o_b, gate_arr, gate_off, nw_arr, n_heads),
                        [(dout, (ROW_TILE, per * HEAD), lambda hh, i: (i, hh))])

    def even_fwd(tag, h, ht):
        p = matmul(f"win_{tag}", h, w_ev_in[0], "nn", slabs=w_ev_in[1])
        outs, saves = [], []
        for direction in (0, 1):
            (o,), sv = scan_fwd(f"hgrn_{tag}_{direction}", f_hgrn2(direction == 1, hb), n_hblk, n_a, hg_args(p, direction),
                                [((t, aw), F32, (A_CHUNK, hb * HEAD), lambda hh, i, tok=a_tok(direction == 1): (tok(i), hh))], hb)
            outs.append(o)
            saves.append(sv)
        a_out = readout(f"ro_{tag}", outs[0], outs[1], p, 4 * n_ah, ev_a_norm, n_ah)
        (pooled,) = tile_fwd(f"pool_{tag}", f_pool, (n_grp, nt), pool_args(p),
                             [((t, aw), BF16, (ROW_TILE, dg), lambda g, i: (i, g))])
        cat = assemble(f"cat_{tag}", [[(a_out, aw, 0)], [(pooled, aw, 0)]], BF16)
        return matmul(f"wout_{tag}", cat, w_ev_out, "nn"), (ht, p, outs, saves, cat)

    def even_bwd(tag, saved, dy, acc):
        ht, p, outs, saves, cat = saved
        dcat = matmul(f"woutd_{tag}", dy, w_ev_out, "nt")
        acc["ev_w_out"] = row_grad(f"woutw_{tag}", cat, dy)
        do, dgate, d_anorm = readout_bwd(f"rob_{tag}", outs[0], outs[1], p, 4 * n_ah, ev_a_norm, n_ah, dcat)
        du, d_pw, d_ps = tile_bwd(f"poolb_{tag}", f_pool, (n_grp, nt), pool_args(p),
                                  [(dcat, (ROW_TILE, dg), lambda g, i: (i, n_grp + g))])
        dq, df, di, dlb = [], [], [], []
        for direction in (0, 1):
            r = scan_bwd(f"hgrnb_{tag}_{direction}", f_hgrn2(direction == 1, hb), n_hblk, n_a, hg_args(p, direction),
                         saves[direction],
                         [(do, (A_CHUNK, hb * HEAD), lambda hh, i, tok=a_tok(direction == 1): (tok(i), hh))])
            dq.append(r[0])
            df.append(r[1])
            di.append(r[2])
            dlb.append(r[3])
        sec = lambda arr, s: (arr, aw, s)
        dp = assemble(f"dp_{tag}", [[sec(dq[0], 0), sec(dq[1], 0)], [sec(df[0], 1)], [sec(df[1], 2)],
                                    [sec(di[0], 3), sec(di[1], 3)], [sec(dgate, 4)], [sec(du, 5)]], BF16)
        acc["ev_w_in"] = col_grad(f"winw_{tag}", None, dp, w_ev_in[1], ht)
        acc["rs_ev"] = exchange_start("rs_start_ev", [acc["ev_w_in"], acc["ev_w_out"]], True, dp, carry=w_ev_in[0])
        acc["ev_a_norm"] = d_anorm
        acc["ev_pool_w"] = d_pw
        acc["ev_pool_scale"] = d_ps
        acc["lb0"] = jnp.stack([dlb[0][0], dlb[1][1]]).reshape(2, aw)
        return matmul(f"wind_{tag}", dp, acc["rs_ev"][5], "nt", slabs=w_ev_in[1])

    n_c = t // C_CHUNK
    ncc = n_ctx // C_CHUNK

    def c_tok(rev):
        if not rev:
            return lambda i: i
        return lambda i: jnp.where(i < ncc, ncc - 1 - i, n_c + ncc - 1 - i)

    alog = od_A_log[0].reshape(2, n_kh, 2, 1)
    dtb = od_dt_bias[0].reshape(2, n_kh, 2, 1)

    khb = _pick(n_kh, (HEADS_PER_STEP, 2, 1))
    n_kblk = n_kh // khb

    def gd_args(z, gates, direction):
        tok = c_tok(direction == 1)
        gblk = (None, khb, None, 2, C_CHUNK)
        sblk = (None, khb, 2, 1)
        return [TArg(z, (C_CHUNK, khb * HEAD), lambda kb, i: (tok(i), kb)),
                TArg(z, (C_CHUNK, khb * HEAD), lambda kb, i: (tok(i), n_kblk + kb)),
                TArg(z, (C_CHUNK, khb * 2 * HEAD), lambda kb, i: (tok(i), n_kblk + kb)),
                TArg(gates, gblk, lambda kb, i: (direction, kb, tok(i), 0, 0)),
                TArg(gates, gblk, lambda kb, i: (2 + direction, kb, tok(i), 0, 0)),
                TArg(alog, sblk, lambda kb, i: (direction, kb, 0, 0), "par", (1,)),
                TArg(dtb, sblk, lambda kb, i: (direction, kb, 0, 0), "par", (1,))]

    def odd_fwd(tag, h, ht):
        pm = matmul(f"win_{tag}", h, w_od_main, "nn")
        pg = matmul(f"wgate_{tag}", h, w_od_gate, "nn")
        z = conv_fwd(f"conv_{tag}", pm, cv_full, 2 * kw + vw, nct)
        gates = pg.reshape(n_c, C_CHUNK, 4, n_kh, 2).transpose(2, 3, 0, 4, 1)
        outs, saves = [], []
        for direction in (0, 1):
            xrows = 2 * khb * C_CHUNK
            (o, xinv), sv = scan_fwd(
                f"gdn_{tag}_{direction}", f_gdn(direction == 1, khb, False), n_kblk, n_c, gd_args(z, gates, direction),
                [((t, vw), F32, (C_CHUNK, khb * 2 * HEAD), lambda kb, i, tok=c_tok(direction == 1): (tok(i), kb)),
                 ((n_kblk, n_c, xrows, C_CHUNK), F32, (None, None, xrows, C_CHUNK), lambda kb, i: (kb, i, 0, 0))],
                2 * khb)
            outs.append(o)
            saves.append((sv, xinv))
        n_vh = 2 * n_kh
        yo = readout(f"ro_{tag}", outs[0], outs[1], pm, 2 * n_kh + n_vh, od_norm, n_vh)
        return matmul(f"wout_{tag}", yo, w_od_out, "nn"), (ht, pm, z, gates, outs, saves, yo)

    def odd_bwd(tag, saved, dy, acc):
        ht, pm, z, gates, outs, saves, yo = saved
        n_vh = 2 * n_kh
        dyo = matmul(f"woutd_{tag}", dy, w_od_out, "nt")
        acc["od_w_out"] = row_grad(f"woutw_{tag}", yo, dy)
        do, dzg, d_onorm = readout_bwd(f"rob_{tag}", outs[0], outs[1], pm, 2 * n_kh + n_vh, od_norm, n_vh, dyo)
        dq, dk, dv, dga, dgb, dal, ddt = [], [], [], [], [], [], []
        for direction in (0, 1):
            sv, xinv = saves[direction]
            xarg = TArg(xinv, (None, None, 2 * khb * C_CHUNK, C_CHUNK), lambda kb, i: (kb, i, 0, 0), "const")
            r = scan_bwd(f"gdnb_{tag}_{direction}", f_gdn(direction == 1, khb, True), n_kblk, n_c,
                         gd_args(z, gates, direction) + [xarg], sv,
                         [(do, (C_CHUNK, khb * 2 * HEAD), lambda kb, i, tok=c_tok(direction == 1): (tok(i), kb))])
            for lst, v_ in zip((dq, dk, dv, dga, dgb, dal, ddt), r):
                lst.append(v_)
        dz = assemble(f"dz_{tag}", [[(dq[0], kw, 0), (dq[1], kw, 0)], [(dk[0], kw, 1), (dk[1], kw, 1)],
                                    [(dv[0], vw, 1), (dv[1], vw, 1)]], F32)
        dpm, d_conv = conv_bwd(f"convb_{tag}", pm, cv_full, dz, 2 * kw + vw, nct, dzg)
        dgates = jnp.stack([dga[0][0], dga[1][1], dgb[0][2], dgb[1][3]])
        dpg = dgates.transpose(2, 4, 0, 1, 3).reshape(t, n_gate).astype(BF16)
        dh = matmul(f"wgated_{tag}", dpg, w_od_gate, "nt")
        dh = matmul(f"wind_{tag}", dpm, w_od_main, "nt", add=dh)
        dw_in = jnp.concatenate([wgrad(f"winw_{tag}", None, dpm, a_t=ht), wgrad(f"wgatew_{tag}", None, dpg, a_t=ht)],
                                axis=1)
        acc["od_w_in"] = dw_in.reshape(d, N_DEV, dw_in.shape[1] // N_DEV).transpose(1, 0, 2)
        acc["od_norm"] = d_onorm
        acc["od_conv"] = d_conv
        acc["od_A_log"] = jnp.stack([dal[0][0], dal[1][1]]).reshape(1, 2, n_vh)
        acc["od_dt_bias"] = jnp.stack([ddt[0][0], ddt[1][1]]).reshape(1, 2, n_vh)
        return dh

    xs0 = jnp.concatenate([ctx[0], x[0]], axis=0)
    h0, h0t = norm_mod("nm1_l0", xs0, 0, 0, 0, 1)
    y0, sv_e = even_fwd("l0", h0, h0t)
    xs1, h1, h1t = res_norm("rn1_l0", xs0, y0, (0, 1), 2, (0, 2), 3, 4)
    g_w13a, g_w2a = gathered(ag_ffn0, xs1, "ag_wait_ffn0")
    w13[0], w2[0] = col_weight(g_w13a), rows_natural(g_w2a)
    y1, sv_f0 = ffn_fwd("l0", h1, h1t, 0)
    xs2, h2, h2t = res_norm("rn2_l0", xs1, y1, (0, 3), 5, (1, 0), 0, 1)
    g_od_in, g_od_out, g_w13b, g_w2b = gathered(ag_l1, xs2, "ag_wait_l1")
    w_od_in = cols_natural(g_od_in)
    w_od_main, w_od_gate = w_od_in[:, :2 * kw + 2 * vw], w_od_in[:, 2 * kw + 2 * vw:]
    w_od_out = rows_natural(g_od_out)
    w13[1], w2[1] = col_weight(g_w13b), rows_natural(g_w2b)
    y2, sv_o = odd_fwd("l1", h2, h2t)
    xs3, h3, h3t = res_norm("rn1_l1", xs2, y2, (1, 1), 2, (1, 2), 3, 4)
    y3, sv_f1 = ffn_fwd("l1", h3, h3t, 1)
    xs4 = gate_res("gr2_l1", xs3, y3, 1, 3, 5)
    loss_loc, dx4 = loss_kernel("loss", xs4, loss_target[0], nct)
    loss = lax.psum(loss_loc, ("x", "y", "c"))

    acc = {"norm_w": [[None] * 4 for _ in range(depth)], "mods": [[] for _ in range(depth)],
           "ffn_w13": [None] * depth, "ffn_w2": [None] * depth}
    dy3, acc["norm_w"][1][3], dmod = gate_res_bwd("gr2b_l1", xs3, y3, 1, 3, 5, dx4)
    acc["mods"][1].append(dmod)
    dh3 = ffn_bwd("l1", sv_f1, 1, dy3, acc)
    rs_ffn1 = exchange_start("rs_start_ffn1", [acc["ffn_w13"][1], acc["ffn_w2"][1]], True, dh3)
    mods[1] = mods[1] + rs_ffn1[4][0, 0]
    dx3, dy2 = res_norm_bwd("rn1b_l1", xs2, y2, (1, 1), 2, (1, 2), 3, 4, dx4, dh3, acc)
    dh2 = odd_bwd("l1", sv_o, dy2, acc)
    rs_od = exchange_start("rs_start_od", [acc["od_w_in"], acc["od_w_out"]], True, dh2)
    mods[0] = mods[0] + rs_od[4][0, 0]
    dx2, dy1 = res_norm_bwd("rn2b_l0", xs1, y1, (0, 3), 5, (1, 0), 0, 1, dx3, dh2, acc)
    dh1 = ffn_bwd("l0", sv_f0, 0, dy1, acc)
    rs_ffn0 = exchange_start("rs_start_ffn0", [acc["ffn_w13"][0], acc["ffn_w2"][0]], True, dh1)
    mods[0] = mods[0] + rs_ffn0[4][0, 0]
    dx1, dy0 = res_norm_bwd("rn1b_l0", xs0, y0, (0, 1), 2, (0, 2), 3, 4, dx2, dh1, acc)
    dh0 = even_bwd("l0", sv_e, dy0, acc)
    rs_ev = acc["rs_ev"]
    dx0, acc["norm_w"][0][0], dmod = norm_mod_bwd("nm1b_l0", xs0, 0, 0, 0, 1, dh0, dx1)
    acc["mods"][0].append(dmod)
    grad_x = dx0[n_ctx:].reshape(1, seq, d)

    (d_lb_slots) = tile_bwd("lb_bwd", f_lb(0), (1,), lb_slots, [(acc["lb0"], (2, aw), lambda i: (0, 0))])
    d_ev_lb = jnp.stack(d_lb_slots, axis=1)

    dmods = jnp.stack([functools.reduce(jnp.add, acc["mods"][layer]) for layer in range(depth)])
    (dm_all,) = all_gather("ag_dmod", [dmods.reshape(depth * 2 * 6, d)])
    dm_all = dm_all.reshape(N_DEV, depth, 2, 6 * d)
    dm_cols = lax.dynamic_slice_in_dim(dm_all, me * ada_loc, ada_loc, axis=3)
    dm_loc = jnp.concatenate([dm_cols[:, :, 1].transpose(1, 0, 2), dm_cols[:, :, 0].transpose(1, 0, 2)], axis=1)
    d_cctx_part, d_w_ada, d_b_loc = tile_bwd("ada_bwd", f_ada, ada_grid, ada_args,
                                             [(dm_loc, (None, 2 * N_DEV, ada_cb), lambda l, j: (l, 0, j))])

    d_b_full = lax.dynamic_update_slice_in_dim(jnp.zeros_like(b_ada), d_b_loc.reshape(depth, ada_loc), me * ada_loc, axis=1)
    d_nw = jnp.stack([jnp.stack([acc["norm_w"][layer][s].reshape(d) for s in range(4)]) for layer in range(depth)])
    small_grads = [d_cctx_part.reshape(d), d_b_full, d_nw, d_ev_lb, acc["ev_a_norm"], acc["ev_pool_w"],
                   acc["ev_pool_scale"], acc["od_conv"], acc["od_A_log"], acc["od_dt_bias"], acc["od_norm"]]
    sg_shapes = [a.shape for a in small_grads]
    (sg,) = all_gather("ag_smallgrads", [_pack(small_grads, F32, FLAT_W, SUBLANES)])
    sg_sum = sum_leading("sum_smallgrads", sg, F32)
    (g_cctx, g_bada, g_nw, g_lb, g_anorm, g_pw, g_ps, g_conv, g_alog, g_dtb, g_onorm) = _unpack(sg_sum, sg_shapes)

    def my_cols(full, axis):
        loc = full.shape[axis] // N_DEV
        return lax.dynamic_slice_in_dim(full, me * loc, loc, axis=axis)

    grads = {
        "c_ctx": g_cctx, "w_ada": d_w_ada, "b_ada": g_bada, "norm_w": my_cols(g_nw, 2), "ev_lb": my_cols(g_lb, 2),
        "ev_a_norm": g_anorm, "ev_pool_w": my_cols(g_pw, 1)[None], "ev_pool_scale": g_ps,
        "od_conv": my_cols(g_conv, 1)[None], "od_A_log": g_alog, "od_dt_bias": g_dtb, "od_norm": g_onorm,
    }

    def reduced(started, tags_, name):
        srcs, lands = exchange_wait(name, started, sg_sum, True)
        out = []
        for tg, s_, l_ in zip(tags_, srcs, lands):
            own = lax.dynamic_index_in_dim(s_, me, 0, keepdims=True)
            out.append(sum_leading(f"rs_sum_{tg}", lax.dynamic_update_slice_in_dim(l_, own, me, 0), F32))
        return out

    g_w13b, g_w2b = reduced(rs_ffn1, ["w13b", "w2b"], "rs_wait_ffn1")
    g_od_in, g_od_out = reduced(rs_od, ["od_in", "od_out"], "rs_wait_od")
    g_w13a, g_w2a = reduced(rs_ffn0, ["w13a", "w2a"], "rs_wait_ffn0")
    g_ev_in, g_ev_out = reduced(rs_ev, ["ev_in", "ev_out"], "rs_wait_ev")
    grads["ev_w_in"], grads["ev_w_out"], grads["od_w_in"], grads["od_w_out"] = (g_ev_in[None], g_ev_out[None],
                                                                                g_od_in[None], g_od_out[None])
    grads["ffn_w13"] = jnp.stack([g_w13a, g_w13b])
    grads["ffn_w2"] = jnp.stack([g_w2a, g_w2b])

    big_names = ["w_ada", "ev_w_in", "ev_w_out", "od_w_in", "od_w_out", "ffn_w13", "ffn_w2"]
    small_names = [n_ for n_ in names if n_ not in big_names]
    gl = {n_: grads[n_].reshape(wts[n_].shape) for n_ in names}
    delta, new_m, new_v = {}, {}, {}
    for n_ in big_names:
        shp = wts[n_].shape
        res = adamw(f"adamw_{n_}", _rows2d(gl[n_]), _rows2d(wts[n_]), _rows2d(mom1[n_]), _rows2d(mom2[n_]))
        delta[n_], new_m[n_], new_v[n_] = (r_.reshape(shp) for r_ in res)
    shapes = [wts[n_].shape for n_ in small_names]
    pk = lambda dct: _pack([dct[n_] for n_ in small_names], F32, FLAT_W, SUBLANES)
    res = adamw("adamw_small", pk(gl), pk(wts), pk(mom1), pk(mom2))
    for dct, r_ in zip((delta, new_m, new_v), res):
        for n_, a_ in zip(small_names, _unpack(r_, shapes)):
            dct[n_] = a_
    return (loss, grad_x, *[gl[n_] for n_ in names], *[delta[n_] for n_ in names], *[new_m[n_] for n_ in names],
            *[new_v[n_] for n_ in names])
```

```python
import functools
from typing import Any, NamedTuple

import numpy as np

import jax
import jax.numpy as jnp
from jax import lax
from jax.experimental import pallas as pl
from jax.experimental.pallas import tpu as pltpu

F32 = jnp.float32
BF16 = jnp.bfloat16
MESH = pl.DeviceIdType.MESH
N_DEV = 8

EPS = 1e-6
GRID_W = 64
HEAD = 128
A_CHUNK = 32
C_CHUNK = 64
C_CONV = 4
POOL_WINDOWS = (2, 4, 8, 16)
MASKED_EXPONENT = -1e30
ADAM_LR, ADAM_B1, ADAM_B2, ADAM_EPS, ADAM_WD, ADAM_STEP = 0.001, 0.9, 0.999, 1e-08, 0.01, 10

VMEM_LIMIT_BYTES = 56 * 1024 * 1024
LANES = 128
SUBLANES = 8
ROW_TILE = 256
FLAT_W = 1024
TM_PREFS = (1056, 1024, 768, 512, 256, 128, 64, 32, 16)
TN_PREFS = (768, 512, 1408, 256, 128)
TK_PREFS = (2048, 2816, 1408, 1024, 768, 512, 384, 256, 128)
TO_PREFS = (1024, 1408, 768, 704, 512, 384, 256, 128)
HEADS_PER_STEP = 8


def _pick(dim, prefs):
    for p in prefs:
        if p <= dim and dim % p == 0:
            return p
    return dim


def _cparams(ngrid):
    return pltpu.CompilerParams(dimension_semantics=("arbitrary",) * ngrid, vmem_limit_bytes=VMEM_LIMIT_BYTES)


def _sds(shape, dtype):
    return jax.ShapeDtypeStruct(tuple(shape), dtype)


def _split(x):
    hi = x.astype(BF16)
    return hi, (x - hi.astype(F32)).astype(BF16)


def _dot(a, b, ca, cb, hi):
    dims = (((ca,), (cb,)), ((), ()))
    dot = lambda u, v: lax.dot_general(u, v, dims, preferred_element_type=F32)
    if hi:
        (ah, al), (bh, bl) = _split(a.astype(F32)), _split(b.astype(F32))
        return dot(ah, bh) + (dot(ah, bl) + dot(al, bh))
    return dot(a.astype(BF16), b.astype(BF16))


@functools.partial(jax.custom_vjp, nondiff_argnums=(2, 3, 4))
def mm(a, b, ca=1, cb=0, hi=False):
    return _dot(a, b, ca, cb, hi)


def _mm_fwd(a, b, ca, cb, hi):
    return _dot(a, b, ca, cb, hi), (a, b)


HI_FWD, HI_BOTH = 1, 2


def _mm_bwd(ca, cb, hi, res, g):
    a, b = res
    bhi = hi == HI_BOTH
    da = _dot(g, b, 1, 1 - cb, bhi) if ca == 1 else _dot(b, g, 1 - cb, 1, bhi)
    db = _dot(a, g, 1 - ca, 0, bhi) if cb == 0 else _dot(g, a, 0, 1 - ca, bhi)
    return da, db


mm.defvjp(_mm_fwd, _mm_bwd)


def _iota2(n, m, axis):
    return lax.broadcasted_iota(jnp.int32, (n, m), axis)


class TArg(NamedTuple):
    arr: Any
    block: tuple
    imap: Any
    kind: str = "row"
    acc: tuple = ()
    gdtype: Any = F32
    grad: bool = True


def _load(ref):
    v = ref[...]
    return v.astype(F32) if jnp.issubdtype(v.dtype, jnp.floating) else v


def tile_fwd(name, f, grid, args, outs):
    n_in, ng = len(args), len(grid)

    def body(*refs):
        pids = tuple(pl.program_id(k) for k in range(ng))
        res = f(pids, *[_load(r) for r in refs[:n_in]])
        for r, v in zip(refs[n_in:], res):
            r[...] = v.astype(r.dtype)

    return pl.pallas_call(
        body, grid=grid, name=name,
        in_specs=[pl.BlockSpec(a.block, a.imap) for a in args],
        out_specs=[pl.BlockSpec(b, im) for (_, _, b, im) in outs],
        out_shape=[_sds(s, d) for (s, d, _, _) in outs],
        compiler_params=_cparams(ng),
    )(*[a.arr for a in args])


def _store_grads(args, diff, pids, g_refs, d):
    for k, gr, dv in zip(diff, g_refs, d):
        a = args[k]
        if a.kind == "row" or not a.acc:
            gr[...] = dv.astype(gr.dtype)
        else:
            first = pids[a.acc[0]] == 0
            for ax in a.acc[1:]:
                first = jnp.logical_and(first, pids[ax] == 0)

            @pl.when(first)
            def _(gr=gr, dv=dv):
                gr[...] = dv.astype(gr.dtype)

            @pl.when(jnp.logical_not(first))
            def _(gr=gr, dv=dv):
                gr[...] += dv.astype(gr.dtype)


def tile_bwd(name, f, grid, args, cts):
    n_in, n_ct, ng = len(args), len(cts), len(grid)
    diff = [k for k, a in enumerate(args) if a.kind != "const" and a.grad]

    def body(*refs):
        pids = tuple(pl.program_id(k) for k in range(ng))
        vals = [_load(r) for r in refs[:n_in]]

        def g(*dv):
            full = list(vals)
            for k, v in zip(diff, dv):
                full[k] = v
            return tuple(f(pids, *full))

        _, vjp = jax.vjp(g, *[vals[k] for k in diff])
        d = vjp(tuple(_load(r) for r in refs[n_in:n_in + n_ct]))
        _store_grads(args, diff, pids, refs[n_in + n_ct:], d)

    return pl.pallas_call(
        body, grid=grid, name=name,
        in_specs=[pl.BlockSpec(a.block, a.imap) for a in args] + [pl.BlockSpec(b, im) for (_, b, im) in cts],
        out_specs=[pl.BlockSpec(args[k].block, args[k].imap) for k in diff],
        out_shape=[_sds(args[k].arr.shape, args[k].gdtype) for k in diff],
        compiler_params=_cparams(ng),
    )(*[a.arr for a in args], *[c[0] for c in cts])


def scan_fwd(name, f, n_heads, n_steps, args, outs, n_state):
    n_in, n_out = len(args), len(outs)
    sblock = (None, None, HEAD, HEAD)

    def body(*refs):
        in_refs = refs[:n_in]
        out_refs = refs[n_in:n_in + n_out]
        save_refs = refs[n_in + n_out:n_in + n_out + n_state]
        s_refs = refs[n_in + n_out + n_state:]

        @pl.when(pl.program_id(1) == 0)
        def _():
            for s in s_refs:
                s[...] = jnp.zeros_like(s)

        states = tuple(s[...] for s in s_refs)
        for sv, s in zip(save_refs, states):
            sv[...] = s
        new_states, res = f(states, *[_load(r) for r in in_refs])
        for s, v in zip(s_refs, new_states):
            s[...] = v
        for r, v in zip(out_refs, res):
            r[...] = v.astype(r.dtype)

    res = pl.pallas_call(
        body, grid=(n_heads, n_steps), name=name,
        in_specs=[pl.BlockSpec(a.block, a.imap) for a in args],
        out_specs=[pl.BlockSpec(b, im) for (_, _, b, im) in outs]
        + [pl.BlockSpec(sblock, lambda h, i: (h, i, 0, 0))] * n_state,
        out_shape=[_sds(s, d) for (s, d, _, _) in outs] + [_sds((n_heads, n_steps, HEAD, HEAD), F32)] * n_state,
        scratch_shapes=[pltpu.VMEM((HEAD, HEAD), F32)] * n_state,
        compiler_params=_cparams(2),
    )(*[a.arr for a in args])
    return res[:n_out], res[n_out:]


def scan_bwd(name, f, n_heads, n_steps, args, saves, cts):
    n_in, n_ct, n_state = len(args), len(cts), len(saves)
    diff = [k for k, a in enumerate(args) if a.kind != "const" and a.grad]
    sblock = (None, None, HEAD, HEAD)

    def rv(im):
        return lambda h, i: im(h, n_steps - 1 - i)

    def body(*refs):
        in_refs = refs[:n_in]
        save_refs = refs[n_in:n_in + n_state]
        ct_refs = refs[n_in + n_state:n_in + n_state + n_ct]
        g_refs = refs[n_in + n_state + n_ct:n_in + n_state + n_ct + len(diff)]
        ds_refs = refs[n_in + n_state + n_ct + len(diff):]
        pids = (pl.program_id(0), pl.program_id(1))

        @pl.when(pids[1] == 0)
        def _():
            for s in ds_refs:
                s[...] = jnp.zeros_like(s)

        vals = [_load(r) for r in in_refs]

        def g(states, *dv):
            full = list(vals)
            for k, v in zip(diff, dv):
                full[k] = v
            new_states, res = f(states, *full)
            return tuple(new_states), tuple(res)

        _, vjp = jax.vjp(g, tuple(s[...] for s in save_refs), *[vals[k] for k in diff])
        d = vjp((tuple(s[...] for s in ds_refs), tuple(_load(r) for r in ct_refs)))
        for s, v in zip(ds_refs, d[0]):
            s[...] = v
        _store_grads(args, diff, pids, g_refs, d[1:])

    return pl.pallas_call(
        body, grid=(n_heads, n_steps), name=name,
        in_specs=[pl.BlockSpec(a.block, rv(a.imap)) for a in args]
        + [pl.BlockSpec(sblock, rv(lambda h, i: (h, i, 0, 0)))] * n_state
        + [pl.BlockSpec(b, rv(im)) for (_, b, im) in cts],
        out_specs=[pl.BlockSpec(args[k].block, rv(args[k].imap)) for k in diff],
        out_shape=[_sds(args[k].arr.shape, args[k].gdtype) for k in diff],
        scratch_shapes=[pltpu.VMEM((HEAD, HEAD), F32)] * n_state,
        compiler_params=_cparams(2),
    )(*[a.arr for a in args], *saves, *[c[0] for c in cts])


def matmul(name, a, b, mode, add=None, out_dtype=F32, slabs=False, out_slabs=False):
    o_spec = None
    if mode == "nn":
        m, k = a.shape
        ns = b.shape[2] if slabs else (b.shape[1] // N_DEV if out_slabs else b.shape[1])
        n = N_DEV * ns if (slabs or out_slabs) else ns
        to_m, to_n, tr = _pick(m, TM_PREFS), _pick(ns, TN_PREFS), _pick(k, TK_PREFS)
        nb = ns // to_n
        grid = (m // to_m, n // to_n, k // tr)
        a_spec = pl.BlockSpec((to_m, tr), lambda i, j, l: (i, l))
        if slabs:
            b_spec = pl.BlockSpec((None, tr, to_n), lambda i, j, l: (j // nb, l, j % nb))
        else:
            b_spec = pl.BlockSpec((tr, to_n), lambda i, j, l: (l, j))
        dims, oshape = (1, 0), (m, n)
        if out_slabs:
            o_spec = pl.BlockSpec((None, to_m, to_n), lambda i, j, l: (j // nb, i, j % nb))
            oshape = (N_DEV, m, ns)
    elif mode == "nt":
        m, n = a.shape
        k = b.shape[1] if slabs else b.shape[0]
        ns = n // N_DEV if slabs else n
        to_m, to_n, tr = _pick(m, TM_PREFS), _pick(k, TO_PREFS), _pick(ns, TK_PREFS)
        nb = ns // tr
        grid = (m // to_m, k // to_n, n // tr)
        a_spec = pl.BlockSpec((to_m, tr), lambda i, j, l: (i, l))
        if slabs:
            b_spec = pl.BlockSpec((None, to_n, tr), lambda i, j, l: (l // nb, j, l % nb))
        else:
            b_spec = pl.BlockSpec((to_n, tr), lambda i, j, l: (j, l))
        dims, oshape = (1, 1), (m, k)
    else:
        (t, k), n = a.shape, b.shape[1]
        ns = n // N_DEV if slabs else n
        to_m, to_n, tr = _pick(k, TO_PREFS), _pick(ns, (2048,) + TO_PREFS), _pick(t, TM_PREFS)
        nb = ns // to_n
        grid = (k // to_m, n // to_n, t // tr)
        a_spec = pl.BlockSpec((tr, to_m), lambda i, j, l: (l, i))
        b_spec = pl.BlockSpec((tr, to_n), lambda i, j, l: (l, j))
        dims, oshape = (0, 0), (k, n)
        if slabs:
            o_spec = pl.BlockSpec((None, to_m, to_n), lambda i, j, l: (j // nb, i, j % nb))
            oshape = (N_DEV, k, ns)
    n_red = grid[2]
    if o_spec is None:
        o_spec = pl.BlockSpec((to_m, to_n), lambda i, j, l: (i, j))
    has_add = add is not None

    def body(a_ref, b_ref, *rest):
        add_ref = rest[0] if has_add else None
        o_ref = rest[1] if has_add else rest[0]
        part = lax.dot_general(a_ref[...].astype(BF16), b_ref[...].astype(BF16),
                               (((dims[0],), (dims[1],)), ((), ())), preferred_element_type=F32)

        def finish(v):
            if has_add:
                v = v + add_ref[...]
            o_ref[...] = v.astype(o_ref.dtype)

        if n_red == 1:
            finish(part)
        else:
            acc = rest[-1]
            step = pl.program_id(2)

            @pl.when(step == 0)
            def _():
                acc[...] = part

            @pl.when(step > 0)
            def _():
                acc[...] += part

            @pl.when(step == n_red - 1)
            def _():
                finish(acc[...])

    return pl.pallas_call(
        body, grid=grid, name=name,
        in_specs=[a_spec, b_spec] + ([o_spec] if has_add else []),
        out_specs=o_spec, out_shape=_sds(oshape, out_dtype),
        scratch_shapes=[pltpu.VMEM((to_m, to_n), F32)] if n_red > 1 else [],
        compiler_params=_cparams(3),
    )(a, b, *([add] if has_add else []))


def assemble(name, pieces, out_dtype):
    flat = [s for piece in pieces for s in piece]
    t = flat[0][0].shape[0]
    widths = [piece[0][1] for piece in pieces]

    def body(*refs):
        o_ref, k, off = refs[-1], 0, 0
        for piece, w in zip(pieces, widths):
            v = refs[k][...].astype(F32)
            k += 1
            for _ in piece[1:]:
                v = v + refs[k][...].astype(F32)
                k += 1
            o_ref[:, off:off + w] = v.astype(o_ref.dtype)
            off += w

    tr = ROW_TILE // 2
    return pl.pallas_call(
        body, grid=(t // tr,), name=name,
        in_specs=[pl.BlockSpec((tr, w), functools.partial(lambda i, cb: (i, cb), cb=cb)) for (_, w, cb) in flat],
        out_specs=pl.BlockSpec((tr, sum(widths)), lambda i: (i, 0)),
        out_shape=_sds((t, sum(widths)), out_dtype),
        compiler_params=_cparams(1),
    )(*[s[0] for s in flat])


ELEM_ROWS = (128, 64, 32, 16, 8)


def _rows2d(a, lead=0):
    return a.reshape(a.shape[:lead] + (-1, a.shape[-1]))


def sum_leading(name, arr, out_dtype):
    k, rows, w = arr.shape
    tr = _pick(rows, ELEM_ROWS)

    def body(a_ref, o_ref):
        v = a_ref[0].astype(F32)
        for j in range(1, k):
            v = v + a_ref[j].astype(F32)
        o_ref[...] = v.astype(o_ref.dtype)

    return pl.pallas_call(
        body, grid=(rows // tr,), name=name,
        in_specs=[pl.BlockSpec((k, tr, w), lambda i: (0, i, 0))],
        out_specs=pl.BlockSpec((tr, w), lambda i: (i, 0)),
        out_shape=_sds((rows, w), out_dtype), compiler_params=_cparams(1),
    )(arr)


def adamw(name, g, w, m, v):
    rows, wd = g.shape
    tr = _pick(rows, ELEM_ROWS)

    def body(g_ref, w_ref, m_ref, v_ref, d_ref, nm_ref, nv_ref):
        gv = g_ref[...]
        mn = ADAM_B1 * m_ref[...] + (1.0 - ADAM_B1) * gv
        vn = ADAM_B2 * v_ref[...] + (1.0 - ADAM_B2) * jnp.square(gv)
        m_hat = mn / (1.0 - ADAM_B1 ** ADAM_STEP)
        v_hat = vn / (1.0 - ADAM_B2 ** ADAM_STEP)
        d_ref[...] = -ADAM_LR * (m_hat / (jnp.sqrt(v_hat) + ADAM_EPS) + ADAM_WD * w_ref[...])
        nm_ref[...] = mn
        nv_ref[...] = vn

    spec = pl.BlockSpec((tr, wd), lambda i: (i, 0))
    return pl.pallas_call(
        body, grid=(rows // tr,), name=name, in_specs=[spec] * 4, out_specs=[spec] * 3,
        out_shape=[_sds(g.shape, F32)] * 3, compiler_params=_cparams(1),
    )(g, w, m, v)


def loss_kernel(name, xs, target, n_ctx_tiles):
    t, d = xs.shape
    nt = t // ROW_TILE

    def body(x_ref, t_ref, dx_ref, l_ref):
        i = pl.program_id(0)
        is_lat = i >= n_ctx_tiles
        err = jnp.where(is_lat, x_ref[...] - t_ref[...], 0.0)
        dx_ref[...] = err / d
        part = 0.5 * jnp.sum(jnp.mean(jnp.square(err), axis=-1, keepdims=True), axis=0, keepdims=True)

        @pl.when(i == 0)
        def _():
            l_ref[...] = jnp.zeros_like(l_ref)

        l_ref[...] += jnp.broadcast_to(part, l_ref.shape)

    dx, l = pl.pallas_call(
        body, grid=(nt,), name=name,
        in_specs=[pl.BlockSpec((ROW_TILE, d), lambda i: (i, 0)),
                  pl.BlockSpec((ROW_TILE, d), lambda i: (jnp.maximum(i - n_ctx_tiles, 0), 0))],
        out_specs=[pl.BlockSpec((ROW_TILE, d), lambda i: (i, 0)), pl.BlockSpec((SUBLANES, LANES), lambda i: (0, 0))],
        out_shape=[_sds((t, d), F32), _sds((SUBLANES, LANES), F32)], compiler_params=_cparams(1),
    )(xs, target)
    return l[0, 0], dx


CONV_COLS = 2048
CONV_LEFT = C_CONV // 2


def _conv_halo_specs(t, n_ctx_tiles):
    nt = t // ROW_TILE
    per = ROW_TILE // SUBLANES
    cur = pl.BlockSpec((ROW_TILE, CONV_COLS), lambda j, i: (i, j))
    prev = pl.BlockSpec((SUBLANES, CONV_COLS), lambda j, i: (jnp.maximum(i * per - 1, 0), j))
    nxt = pl.BlockSpec((SUBLANES, CONV_COLS), lambda j, i: (jnp.minimum((i + 1) * per, nt * per - 1), j))
    return cur, prev, nxt


def _fill_ext(ext, prev_ref, cur_ref, next_ref, i, nt, n_ctx_tiles):
    has_prev = jnp.logical_and(i != 0, i != n_ctx_tiles)
    has_next = jnp.logical_and(i != n_ctx_tiles - 1, i != nt - 1)
    ext[0:SUBLANES, :] = jnp.where(has_prev, prev_ref[...], 0.0)
    ext[SUBLANES:SUBLANES + ROW_TILE, :] = cur_ref[...]
    ext[SUBLANES + ROW_TILE:, :] = jnp.where(has_next, next_ref[...], 0.0)


def conv_fwd(name, p, w, width, n_ctx_tiles):
    t = p.shape[0]
    nt = t // ROW_TILE
    cur, prev, nxt = _conv_halo_specs(t, n_ctx_tiles)

    def body(c_ref, p_ref, n_ref, w_ref, o_ref, ext):
        _fill_ext(ext, p_ref, c_ref, n_ref, pl.program_id(1), nt, n_ctx_tiles)
        acc = None
        for j in range(C_CONV):
            term = ext[pl.ds(SUBLANES + j - CONV_LEFT, ROW_TILE), :] * w_ref[j:j + 1, :]
            acc = term if acc is None else acc + term
        o_ref[...] = acc

    return pl.pallas_call(
        body, grid=(width // CONV_COLS, nt), name=name,
        in_specs=[cur, prev, nxt, pl.BlockSpec((C_CONV, CONV_COLS), lambda j, i: (0, j))],
        out_specs=cur, out_shape=_sds((t, width), F32),
        scratch_shapes=[pltpu.VMEM((ROW_TILE + 2 * SUBLANES, CONV_COLS), F32)],
        compiler_params=_cparams(2),
    )(p, p, p, w)


def conv_bwd(name, p, w, dz, width, n_ctx_tiles, into):
    t = p.shape[0]
    nt = t // ROW_TILE
    cur, prev, nxt = _conv_halo_specs(t, n_ctx_tiles)

    def body(c_ref, p_ref, n_ref, dc_ref, dp_ref, dn_ref, w_ref, into_ref, du_ref, dw_ref, ext, dext):
        i = pl.program_id(1)
        _fill_ext(ext, p_ref, c_ref, n_ref, i, nt, n_ctx_tiles)
        _fill_ext(dext, dp_ref, dc_ref, dn_ref, i, nt, n_ctx_tiles)
        dzc = dc_ref[...]
        @pl.when(i == 0)
        def _():
            dw_ref[...] = jnp.zeros_like(dw_ref)

        acc = None
        for j in range(C_CONV):
            term = dext[pl.ds(SUBLANES + CONV_LEFT - j, ROW_TILE), :] * w_ref[j:j + 1, :]
            acc = term if acc is None else acc + term
            dw_ref[j:j + 1, :] += jnp.sum(dzc * ext[pl.ds(SUBLANES + j - CONV_LEFT, ROW_TILE), :], axis=0, keepdims=True)
        du_ref[...] = acc.astype(du_ref.dtype)

    wspec = pl.BlockSpec((C_CONV, CONV_COLS), lambda j, i: (0, j))
    return pl.pallas_call(
        body, grid=(width // CONV_COLS, nt), name=name,
        in_specs=[cur, prev, nxt, cur, prev, nxt, wspec, pl.BlockSpec(memory_space=pl.ANY)],
        out_specs=[cur, wspec], out_shape=[_sds(into.shape, into.dtype), _sds((C_CONV, width), F32)],
        input_output_aliases={7: 0},
        scratch_shapes=[pltpu.VMEM((ROW_TILE + 2 * SUBLANES, CONV_COLS), F32)] * 2,
        compiler_params=_cparams(2),
    )(p, p, p, dz, dz, dz, w, into)


def _rms(x, w):
    return x * lax.rsqrt(jnp.mean(x * x, axis=-1, keepdims=True) + EPS) * w


def _seg_mod(mods, is_ctx):
    return jnp.where(is_ctx, mods[0], mods[1])


def f_norm_mod(shift_i, scale_i, n_ctx_tiles, passthrough=False, transposed=False):
    def f(pids, x, nw, mods):
        m = _seg_mod(mods, pids[0] < n_ctx_tiles)
        h = _rms(x, nw) * (1.0 + m[scale_i:scale_i + 1]) + m[shift_i:shift_i + 1]
        if transposed:
            return (h, h.T)
        return (h, x) if passthrough else (h,)
    return f


def f_gate_res(gate_i, n_ctx_tiles):
    def f(pids, x, y, nw, mods):
        m = _seg_mod(mods, pids[0] < n_ctx_tiles)
        return (x + m[gate_i:gate_i + 1] * _rms(y, nw),)
    return f


def f_res_norm(gate_i, shift_i, scale_i, n_ctx_tiles, transposed=False):
    def f(pids, x, y, nw_g, mods_g, nw_n, mods_n):
        is_ctx = pids[0] < n_ctx_tiles
        mg, mn = _seg_mod(mods_g, is_ctx), _seg_mod(mods_n, is_ctx)
        x1 = x + mg[gate_i:gate_i + 1] * _rms(y, nw_g)
        h = _rms(x1, nw_n) * (1.0 + mn[scale_i:scale_i + 1]) + mn[shift_i:shift_i + 1]
        return (x1, h, h.T) if transposed else (x1, h)
    return f


def f_swiglu(pids, gu):
    half = gu.shape[1] // 2
    return (jax.nn.silu(gu[:, :half]) * gu[:, half:],)


def f_readout(n_heads):
    def f(pids, o_a, o_b, gate, nw):
        cols = [slice(j * HEAD, (j + 1) * HEAD) for j in range(n_heads)]
        outs = _each(lambda cs: _rms(o_a[:, cs] + o_b[:, cs], nw) * jax.nn.silu(gate[:, cs]), cols)
        return (jnp.concatenate(outs, axis=1) if n_heads > 1 else outs[0],)
    return f


def f_pool(pids, u, pmat, pw, scale):
    d = mm(pmat, u, 1, 0, HI_BOTH) - u
    return (mm(d, pw) * scale,)


def f_lb(layer):
    def f(pids, *slots):
        top = slots[0]
        for s in slots[1:]:
            top = jnp.maximum(top, s)
        ex = [jnp.exp(s - top) for s in slots]
        tot = ex[0]
        for e in ex[1:]:
            tot = tot + e
        part = ex[0]
        for e in ex[1:layer + 1]:
            part = part + e
        return (part / tot,)
    return f


def f_ada(pids, c_all, c_ctx, w, b):
    c16 = jnp.concatenate([c_all, jnp.broadcast_to(c_ctx, c_all.shape)], axis=0)
    return (mm(jax.nn.silu(c16), w) + b,)


def _each(fn, *lists):
    return [fn(*xs) for xs in zip(*lists)]


def _hgrn2_heads(sts, qrs, frs, irs, lbs, rev):
    c = A_CHUNK
    mid = c - c // 2 if rev else c // 2 - 1
    ri, ci = _iota2(c, c, 0), _iota2(c, c, 1)
    incl = (ri <= ci) if rev else (ri >= ci)
    incl_f = incl.astype(F32)
    qs = _each(jax.nn.silu, qrs)
    log_fs = _each(lambda lb, fr: jnp.log(lb + (1.0 - lb) * jax.nn.sigmoid(fr)), lbs, frs)
    ks = _each(lambda lb, fr: (1.0 - lb) * jax.nn.sigmoid(-fr), lbs, frs)
    bs = _each(lambda lf: mm(incl_f, lf, 1, 0, HI_BOTH), log_fs)
    b_lasts = _each(lambda lf: jnp.sum(lf, axis=0, keepdims=True), log_fs)
    scores = _each(lambda q, k, b: mm(q * jnp.exp(b - b[mid:mid + 1]), k * jnp.exp(b[mid:mid + 1] - b), 1, 1), qs, ks, bs)
    intra = _each(lambda sc, ir: mm(jnp.where(incl, sc, 0.0), ir), scores, irs)
    inter = _each(lambda q, b, st: mm(q * jnp.exp(b), st, 1, 1), qs, bs, sts)
    upd = _each(lambda ir, k, bl, b: mm(ir, k * jnp.exp(bl - b), 0, 0), irs, ks, b_lasts, bs)
    new = _each(lambda st, bl, u: st * jnp.exp(bl) + u, sts, b_lasts, upd)
    return new, _each(jnp.add, intra, inter)


def f_hgrn2(rev, hb):
    def f(states, qr, fr, ir, lb):
        cols = [slice(j * HEAD, (j + 1) * HEAD) for j in range(hb)]
        new, outs = _hgrn2_heads(list(states), [qr[:, cs] for cs in cols], [fr[:, cs] for cs in cols],
                                 [ir[:, cs] for cs in cols], [lb[j] for j in range(hb)], rev)
        return tuple(new), (jnp.concatenate(outs, axis=1) if hb > 1 else outs[0],)
    return f


def _neumann_inv(a_lows):
    n = a_lows[0].shape[0]
    eye = (_iota2(n, n, 0) == _iota2(n, n, 1)).astype(F32)
    ps = _each(lambda a: -a, a_lows)
    xs = _each(lambda p: eye + p, ps)
    ps = _each(lambda p: mm(p, p, 1, 0, HI_FWD), ps)
    k = 2
    while 2 * k < n:
        ys = _each(lambda p, x: mm(jnp.concatenate([p, x], axis=0), p, 1, 0, HI_FWD), ps, xs)
        ps = _each(lambda y: y[:n], ys)
        xs = _each(lambda x, y: x + y[n:], xs, ys)
        k *= 2
    return tuple(_each(lambda x, p: x + mm(x, p, 1, 0, HI_FWD), xs, ps))


@jax.custom_vjp
def unit_tri_inv(a_lows):
    return _neumann_inv(a_lows)


def _uti_fwd(a_lows):
    xs = _neumann_inv(a_lows)
    return xs, xs


def _uti_bwd(xs, gs):
    ts = _each(lambda x, g: mm(x, g, 0, 0), xs, gs)
    return (tuple(_each(lambda t, x: -mm(t, x, 1, 1), ts, xs)),)


unit_tri_inv.defvjp(_uti_fwd, _uti_bwd)


@jax.custom_vjp
def unit_tri_inv_saved(a_lows, xs):
    return xs


def _utis_fwd(a_lows, xs):
    return xs, xs


def _utis_bwd(xs, gs):
    return _uti_bwd(xs, gs) + (tuple(jnp.zeros_like(x) for x in xs),)


unit_tri_inv_saved.defvjp(_utis_fwd, _utis_bwd)


def _l2n(x):
    return x * lax.rsqrt(jnp.sum(x * x, axis=-1, keepdims=True) + EPS)


def _gdn_heads(ss, qs, ks, vs, a_rows, b_rows, alogs, dtbs, rev, xs_saved=None):
    c = qs[0].shape[0]
    ri, ci = _iota2(c, c, 0), _iota2(c, c, 1)
    causal = (ri <= ci) if rev else (ri >= ci)
    causal_t = (ri >= ci) if rev else (ri <= ci)
    strict = (ri < ci) if rev else (ri > ci)
    eye = ri == ci
    sq = lambda row: jnp.broadcast_to(row, (c, c))
    to_col = lambda row: jnp.sum(jnp.where(eye, sq(row), 0.0), axis=1, keepdims=True)
    g_rows = _each(lambda al, a, dt: -jnp.exp(al) * jax.nn.softplus(a + dt), alogs, a_rows, dtbs)
    beta_cols = _each(lambda b: to_col(jax.nn.sigmoid(b)), b_rows)
    g_cols = _each(to_col, g_rows)
    gc_cols = _each(lambda g: jnp.sum(jnp.where(causal, sq(g), 0.0), axis=1, keepdims=True), g_rows)
    gc_rows = _each(lambda g: jnp.sum(jnp.where(causal_t, sq(g), 0.0), axis=0, keepdims=True), g_cols)
    gc_lasts = _each(lambda g: jnp.sum(g, axis=1, keepdims=True), g_rows)
    decays = _each(lambda gc, gr: jnp.exp(jnp.where(causal, gc - gr, MASKED_EXPONENT)), gc_cols, gc_rows)
    k_betas = _each(jnp.multiply, ks, beta_cols)
    v_betas = _each(jnp.multiply, vs, beta_cols)
    kq_ks = _each(lambda kb, q, k: mm(jnp.concatenate([kb, q], axis=0), k, 1, 1), k_betas, qs, ks)
    a_lows = _each(lambda kk, dec: jnp.where(strict, kk[:c] * dec, 0.0), kq_ks, decays)
    qks = _each(lambda kk, dec: kk[c:] * dec, kq_ks, decays)
    xs = unit_tri_inv(tuple(a_lows)) if xs_saved is None else unit_tri_inv_saved(tuple(a_lows), tuple(xs_saved))
    egcs = _each(jnp.exp, gc_cols)
    uws = _each(lambda x, vb, kb, e: mm(x, jnp.concatenate([vb, kb * e], axis=1), 1, 0, HI_FWD), xs, v_betas, k_betas, egcs)
    dv = vs[0].shape[1]
    wq_ss = _each(lambda uw, q, e, s: mm(jnp.concatenate([uw[:, dv:], q * e], axis=0), s), uws, qs, egcs, ss)
    v_news = _each(lambda uw, wq: uw[:, :dv] - wq[:c], uws, wq_ss)
    o_states = _each(lambda wq: wq[c:], wq_ss)
    o_locals = _each(mm, qks, v_news)
    upds = _each(lambda k, gl, gc, vn: mm(k * jnp.exp(gl - gc), vn, 0, 0), ks, gc_lasts, gc_cols, v_news)
    new = _each(lambda s, gl, u: s * jnp.exp(gl) + u, ss, gc_lasts, upds)
    return new, _each(jnp.add, o_states, o_locals), xs


def f_gdn(rev, khb, saved_inverse):
    def f(states, qr, kr, vr, a3, b3, alog3, dtb3, xcat=None):
        heads = [(j, r) for j in range(khb) for r in range(2)]
        cols = [slice(j * HEAD, (j + 1) * HEAD) for j in range(khb)]
        c = qr.shape[0]
        qk_ = _each(lambda cs: (_l2n(jax.nn.silu(qr[:, cs])) * (HEAD ** -0.5), _l2n(jax.nn.silu(kr[:, cs]))), cols)
        vs = [jax.nn.silu(vr[:, (2 * j + r) * HEAD:(2 * j + r + 1) * HEAD]) for j, r in heads]
        row = lambda arr3: [arr3[j][r:r + 1] for j, r in heads]
        xs_saved = [xcat[n * c:(n + 1) * c] for n in range(len(heads))] if saved_inverse else None
        new, outs, xs = _gdn_heads(list(states), [qk_[j][0] for j, _ in heads], [qk_[j][1] for j, _ in heads], vs,
                                   row(a3), row(b3), row(alog3), row(dtb3), rev, xs_saved)
        o = jnp.concatenate(outs, axis=1)
        return tuple(new), ((o,) if saved_inverse else (o, jnp.concatenate(xs, axis=0)))
    return f


def _hbm_spec():
    return pl.BlockSpec(memory_space=pltpu.HBM)


def all_gather(name, xs):
    nt = len(xs)

    def body(*refs):
        x_refs, out_refs = refs[:nt], refs[nt:2 * nt]
        send_sems, recv_sems, local_sems = refs[2 * nt:]
        x, y, c = lax.axis_index("x"), lax.axis_index("y"), lax.axis_index("c")
        me, sibling = (x, y, c), (x, y, 1 - c)
        chips = [(1 - x, y), (x, 1 - y), (1 - x, 1 - y)]

        def slab(t, px, py, pc):
            return out_refs[t].at[4 * px + 2 * py + pc]

        def copy(t, k, block, to, src=None):
            return pltpu.make_async_remote_copy(
                src_ref=slab(t, *block) if src is None else src, dst_ref=slab(t, *block),
                send_sem=send_sems.at[7 * t + k], recv_sem=recv_sems.at[7 * t + k], device_id=to, device_id_type=MESH)

        mine, first, passed = [], [], []
        for t in range(nt):
            mine.append(pltpu.make_async_copy(x_refs[t], slab(t, *me), local_sems.at[t]))
            mine[-1].start()
            cps = [copy(t, 0, me, sibling, src=x_refs[t])]
            cps += [copy(t, 1 + j, me, (*chip, c), src=x_refs[t]) for j, chip in enumerate(chips)]
            for cp in cps:
                cp.start()
            first += cps
        for j, chip in enumerate(chips):
            for t in range(nt):
                copy(t, 1 + j, (*chip, c), me).wait_recv()
                fw = copy(t, 4 + j, (*chip, c), sibling)
                fw.start()
                passed.append(fw)
        for t in range(nt):
            copy(t, 0, sibling, me).wait_recv()
            for j, chip in enumerate(chips):
                copy(t, 4 + j, (*chip, 1 - c), me).wait_recv()
        for cp in first + passed:
            cp.wait_send()
        for cp in mine:
            cp.wait()

    return pl.pallas_call(
        body, name=name, out_shape=[_sds((N_DEV,) + a.shape, a.dtype) for a in xs],
        in_specs=[_hbm_spec()] * nt, out_specs=[_hbm_spec()] * nt,
        scratch_shapes=[pltpu.SemaphoreType.DMA((7 * nt,)), pltpu.SemaphoreType.DMA((7 * nt,)),
                        pltpu.SemaphoreType.DMA((nt,))],
    )(*xs)


def _peers(x, y, c):
    out = []
    for k in range(1, N_DEV):
        px = 1 - x if k & 4 else x
        py = 1 - y if k & 2 else y
        pc = 1 - c if k & 1 else c
        out.append((px, py, pc))
    return out


def _exchange_copies(src_refs, land_refs, send_sems, recv_sems, scatter):
    x, y, c = lax.axis_index("x"), lax.axis_index("y"), lax.axis_index("c")
    me = 4 * x + 2 * y + c
    sends, recvs = [], []
    for t, (src, land) in enumerate(zip(src_refs, land_refs)):
        for k, (px, py, pc) in enumerate(_peers(x, y, c)):
            peer = 4 * px + 2 * py + pc
            sem = dict(send_sem=send_sems.at[7 * t + k], recv_sem=recv_sems.at[7 * t + k],
                       device_id=(px, py, pc), device_id_type=MESH)
            src_k = src.at[peer] if scatter else src
            sends.append(pltpu.make_async_remote_copy(src_ref=src_k, dst_ref=land.at[me], **sem))
            recvs.append(pltpu.make_async_remote_copy(src_ref=src_k, dst_ref=land.at[peer], **sem))
    return sends, recvs


def exchange_start(name, srcs, scatter, after, carry=None):
    nt = len(srcs)
    lands = [lax.empty(s.shape if scatter else (N_DEV,) + s.shape, s.dtype) for s in srcs]
    thru = list(srcs) + lands + ([carry] if carry is not None else [])
    n_thru = len(thru)

    def body(*refs):
        src_refs, land_refs = refs[:nt], refs[nt:2 * nt]
        send_sems, recv_sems = refs[n_thru + 1], refs[n_thru + 2]
        token = refs[-1]
        sends, _ = _exchange_copies(src_refs, land_refs, send_sems, recv_sems, scatter)
        for cp in sends:
            cp.start()
        token[...] = jnp.zeros_like(token)

    res = pl.pallas_call(
        body, name=name,
        out_shape=(pltpu.SemaphoreType.DMA((7 * nt,)), pltpu.SemaphoreType.DMA((7 * nt,)),
                   *[pltpu.HBM(a.shape, a.dtype) for a in thru], _sds((SUBLANES, LANES), F32)),
        in_specs=[_hbm_spec()] * n_thru + [pl.BlockSpec(memory_space=pl.ANY)],
        out_specs=(pl.BlockSpec(memory_space=pltpu.SEMAPHORE), pl.BlockSpec(memory_space=pltpu.SEMAPHORE),
                   *[_hbm_spec()] * n_thru, pl.BlockSpec(memory_space=pltpu.VMEM)),
        input_output_aliases={i: 2 + i for i in range(n_thru)},
        compiler_params=pltpu.CompilerParams(has_side_effects=pltpu.SideEffectType.DATAFLOW_SIDE_EFFECTING),
    )(*[pltpu.with_memory_space_constraint(a, pltpu.HBM) for a in thru], after)
    return (res[0], res[1], list(res[2:2 + nt]), list(res[2 + nt:2 + 2 * nt]), res[-1],
            res[2 + 2 * nt] if carry is not None else None)


def exchange_wait(name, started, after, scatter):
    send_sems, recv_sems, srcs, lands = started[:4]
    nt = len(srcs)

    def body(*refs):
        src_refs, land_refs = refs[:nt], refs[nt:2 * nt]
        sends, recvs = _exchange_copies(src_refs, land_refs, refs[2 * nt], refs[2 * nt + 1], scatter)
        for cp in sends:
            cp.wait_send()
        for cp in recvs:
            cp.wait_recv()

    hbm = lambda a: pltpu.HBM(a.shape, a.dtype)
    sem = pl.BlockSpec(memory_space=pltpu.SEMAPHORE)
    res = pl.pallas_call(
        body, name=name,
        out_shape=(*[hbm(a) for a in srcs], *[hbm(a) for a in lands]),
        in_specs=[_hbm_spec()] * (2 * nt) + [sem, sem, pl.BlockSpec(memory_space=pl.ANY)],
        out_specs=tuple([_hbm_spec()] * (2 * nt)),
        input_output_aliases={i: i for i in range(2 * nt)},
        compiler_params=pltpu.CompilerParams(has_side_effects=pltpu.SideEffectType.DATAFLOW_SIDE_EFFECTING),
    )(*srcs, *lands, send_sems, recv_sems, after)
    return list(res[:nt]), list(res[nt:])


def _pack(arrs, dtype, width, row_mult, lead=0):
    ld = arrs[0].shape[:lead]
    flat = jnp.concatenate([a.reshape(ld + (-1,)).astype(dtype) for a in arrs], axis=-1)
    n = flat.shape[-1]
    q = width * row_mult
    npad = -(-n // q) * q
    flat = jnp.pad(flat, [(0, 0)] * lead + [(0, npad - n)])
    return flat.reshape(ld + (npad // width, width))


def _unpack(buf, shapes, lead=0):
    ld = buf.shape[:lead]
    flat = buf.reshape(ld + (-1,))
    out, off = [], 0
    for s in shapes:
        n = int(np.prod(s))
        out.append(flat[..., off:off + n].reshape(ld + tuple(s)))
        off += n
    return out


def _pool_mats(seg_len):
    mats = np.zeros((len(POOL_WINDOWS), ROW_TILE, ROW_TILE), np.float32)
    for gi, win in enumerate(POOL_WINDOWS):
        for p in range(ROW_TILE):
            base = (p // seg_len) * seg_len
            q = p - base
            lo = min(max(q - win // 2, 0), seg_len - 1)
            hi = min(max(q + win - 1 - win // 2, 0), seg_len - 1)
            mats[gi, p, base + lo:base + hi + 1] = 1.0 / (hi - lo + 1)
    return mats


def kernel(x, c, ctx, c_ctx, w_ada, b_ada, norm_w, ev_w_in, ev_lb, ev_a_norm, ev_pool_w, ev_pool_scale, ev_w_out, od_w_in, od_conv, od_A_log, od_dt_bias, od_norm, od_w_out, ffn_w13, ffn_w2, loss_target, m_c_ctx, m_w_ada, m_b_ada, m_norm_w, m_ev_w_in, m_ev_lb, m_ev_a_norm, m_ev_pool_w, m_ev_pool_scale, m_ev_w_out, m_od_w_in, m_od_conv, m_od_A_log, m_od_dt_bias, m_od_norm, m_od_w_out, m_ffn_w13, m_ffn_w2, v_c_ctx, v_w_ada, v_b_ada, v_norm_w, v_ev_w_in, v_ev_lb, v_ev_a_norm, v_ev_pool_w, v_ev_pool_scale, v_ev_w_out, v_od_w_in, v_od_conv, v_od_A_log, v_od_dt_bias, v_od_norm, v_od_w_out, v_ffn_w13, v_ffn_w2):
    names = ["c_ctx", "w_ada", "b_ada", "norm_w", "ev_w_in", "ev_lb", "ev_a_norm", "ev_pool_w", "ev_pool_scale",
             "ev_w_out", "od_w_in", "od_conv", "od_A_log", "od_dt_bias", "od_norm", "od_w_out", "ffn_w13", "ffn_w2"]
    wts = dict(zip(names, [c_ctx, w_ada, b_ada, norm_w, ev_w_in, ev_lb, ev_a_norm, ev_pool_w, ev_pool_scale,
                           ev_w_out, od_w_in, od_conv, od_A_log, od_dt_bias, od_norm, od_w_out, ffn_w13, ffn_w2]))
    mom1 = dict(zip(names, [m_c_ctx, m_w_ada, m_b_ada, m_norm_w, m_ev_w_in, m_ev_lb, m_ev_a_norm, m_ev_pool_w,
                            m_ev_pool_scale, m_ev_w_out, m_od_w_in, m_od_conv, m_od_A_log, m_od_dt_bias, m_od_norm,
                            m_od_w_out, m_ffn_w13, m_ffn_w2]))
    mom2 = dict(zip(names, [v_c_ctx, v_w_ada, v_b_ada, v_norm_w, v_ev_w_in, v_ev_lb, v_ev_a_norm, v_ev_pool_w,
                            v_ev_pool_scale, v_ev_w_out, v_od_w_in, v_od_conv, v_od_A_log, v_od_dt_bias, v_od_norm,
                            v_od_w_out, v_ffn_w13, v_ffn_w2]))

    ax, ay, ac = lax.axis_index("x"), lax.axis_index("y"), lax.axis_index("c")
    me = 4 * ax + 2 * ay + ac

    seq, d = x.shape[1], x.shape[2]
    n_ctx = ctx.shape[1]
    t = n_ctx + seq
    nt = t // ROW_TILE
    nct = n_ctx // ROW_TILE
    assert n_ctx == ROW_TILE and seq % ROW_TILE == 0 and ROW_TILE % GRID_W == 0
    depth = w_ada.shape[0]
    aw = d // 2
    n_ah = aw // HEAD
    n_grp = len(POOL_WINDOWS)
    dg = aw // n_grp
    assert dg % LANES == 0
    n_kh = d // HEAD
    kw, vw = n_kh * HEAD, 2 * n_kh * HEAD
    n_gate = 8 * n_kh
    ffn_h = ffn_w2.shape[1] * N_DEV
    ada_loc = w_ada.shape[2]
    assert depth == 2

    bf = lambda w_: w_.astype(BF16)

    def gathered(started, after, name):
        srcs, lands = exchange_wait(name, started, after, False)
        return [lax.dynamic_update_index_in_dim(l_, s_, me, 0) for l_, s_ in zip(lands, srcs)]

    small_shapes = [(d,), norm_w.shape, ev_lb.shape, ev_pool_w.shape[1:], od_conv.shape[1:]]
    (g1,) = all_gather("ag_small", [_pack([c[0], norm_w, ev_lb, ev_pool_w[0], od_conv[0]], F32, LANES, SUBLANES)])
    c_all, nw_g, lb_g, pw_g, cv_g = _unpack(g1, small_shapes, lead=1)
    nw_full = nw_g.transpose(1, 2, 0, 3).reshape(depth, 4, d)
    lb_full = lb_g.transpose(1, 2, 0, 3).reshape(2, depth + 1, aw)
    pw_full = pw_g.transpose(1, 0, 2, 3).reshape(n_grp, dg, dg)
    cv_full = cv_g.transpose(1, 0, 2).reshape(C_CONV, 2 * kw + vw)

    g_ev_in, g_ev_out = all_gather("ag_weights_ev", [bf(ev_w_in[0]), bf(ev_w_out[0])])

    def cols_natural(g):
        return g.transpose(1, 0, 2).reshape(g.shape[1], N_DEV * g.shape[2])

    def rows_natural(g):
        return g.reshape(N_DEV * g.shape[1], g.shape[2])

    def col_weight(g):
        return (g, True) if g.shape[2] % LANES == 0 else (cols_natural(g), False)

    w_ev_in = col_weight(g_ev_in)
    w_ev_out = rows_natural(g_ev_out)
    w13, w2 = [None] * depth, [None] * depth

    b_loc = lax.dynamic_slice_in_dim(b_ada, me * ada_loc, ada_loc, axis=1).reshape(depth, 1, ada_loc)
    ada_cb = _pick(ada_loc, TN_PREFS)
    ada_grid = (depth, ada_loc // ada_cb)
    ada_args = [
        TArg(c_all, (N_DEV, d), lambda l, j: (0, 0), "const"),
        TArg(c_ctx.reshape(1, d), (1, d), lambda l, j: (0, 0), "par", (0, 1)),
        TArg(w_ada, (None, d, ada_cb), lambda l, j: (l, 0, j)),
        TArg(b_loc, (None, 1, ada_cb), lambda l, j: (l, 0, j)),
    ]
    (m_loc,) = tile_fwd("ada_fwd", f_ada, ada_grid, ada_args,
                        [((depth, 2 * N_DEV, ada_loc), F32, (None, 2 * N_DEV, ada_cb), lambda l, j: (l, 0, j))])
    (m_all,) = all_gather("ag_mod", [m_loc])
    mods = []
    for layer in range(depth):
        lat = lax.dynamic_index_in_dim(m_all[:, layer], me, axis=1, keepdims=False).reshape(6, d)
        cxt = lax.dynamic_index_in_dim(m_all[:, layer], N_DEV + me, axis=1, keepdims=False).reshape(6, d)
        mods.append(jnp.stack([cxt, lat]))

    gathers_done = mods[0][0, :1, :SUBLANES] + g_ev_out[0, :1, :SUBLANES].astype(F32)
    ag_ffn0 = exchange_start("ag_start_ffn0", [bf(ffn_w13[0]), bf(ffn_w2[0])], False, gathers_done)
    ag_l1 = exchange_start("ag_start_l1", [bf(od_w_in[0]), bf(od_w_out[0]), bf(ffn_w13[1]), bf(ffn_w2[1])], False,
                           gathers_done)
    mods[0] = mods[0] + (ag_ffn0[4][0, 0] + ag_l1[4][0, 0])

    lb_slots = [TArg(lb_full[:, j], (2, aw), lambda i: (0, 0)) for j in range(depth + 1)]
    (lb0,) = tile_fwd("lb_fwd", f_lb(0), (1,), lb_slots, [((2, aw), F32, (2, aw), lambda i: (0, 0))])
    lb0r = lb0.reshape(2, n_ah, 1, HEAD)

    full_row = lambda i: (i, 0)
    par0 = lambda i: (0, 0)

    def nm_args(xs, layer, slot):
        return [TArg(xs, (ROW_TILE, d), full_row),
                TArg(nw_full[layer, slot].reshape(1, d), (1, d), par0, "par", (0,)),
                TArg(mods[layer], (2, 6, d), lambda i: (0, 0, 0), "par", (0,))]

    def norm_mod(name, xs, layer, slot, si, ci):
        return tile_fwd(name, f_norm_mod(si, ci, nct, transposed=True), (nt,), nm_args(xs, layer, slot),
                        [((t, d), BF16, (ROW_TILE, d), full_row), ((d, t), BF16, (d, ROW_TILE), lambda i: (0, i))])

    def norm_mod_bwd(name, xs, layer, slot, si, ci, dh, carry):
        return tile_bwd(name, f_norm_mod(si, ci, nct, True), (nt,), nm_args(xs, layer, slot),
                        [(dh, (ROW_TILE, d), full_row), (carry, (ROW_TILE, d), full_row)])

    def gr_args(xs, ys, layer, slot):
        return [TArg(xs, (ROW_TILE, d), full_row, grad=False), TArg(ys, (ROW_TILE, d), full_row, gdtype=BF16),
                TArg(nw_full[layer, slot].reshape(1, d), (1, d), par0, "par", (0,)),
                TArg(mods[layer], (2, 6, d), lambda i: (0, 0, 0), "par", (0,))]

    def gate_res(name, xs, ys, layer, slot, gi):
        (o,) = tile_fwd(name, f_gate_res(gi, nct), (nt,), gr_args(xs, ys, layer, slot),
                        [((t, d), F32, (ROW_TILE, d), full_row)])
        return o

    def gate_res_bwd(name, xs, ys, layer, slot, gi, dx):
        return tile_bwd(name, f_gate_res(gi, nct), (nt,), gr_args(xs, ys, layer, slot),
                        [(dx, (ROW_TILE, d), full_row)])

    def rn_args(xs, ys, gate, norm):
        return [TArg(xs, (ROW_TILE, d), full_row), TArg(ys, (ROW_TILE, d), full_row, gdtype=BF16),
                TArg(nw_full[gate[0], gate[1]].reshape(1, d), (1, d), par0, "par", (0,)),
                TArg(mods[gate[0]], (2, 6, d), lambda i: (0, 0, 0), "par", (0,)),
                TArg(nw_full[norm[0], norm[1]].reshape(1, d), (1, d), par0, "par", (0,)),
                TArg(mods[norm[0]], (2, 6, d), lambda i: (0, 0, 0), "par", (0,))]

    def res_norm(name, xs, ys, gate, gi, norm, si, ci):
        return tile_fwd(name, f_res_norm(gi, si, ci, nct, transposed=True), (nt,), rn_args(xs, ys, gate, norm),
                        [((t, d), F32, (ROW_TILE, d), full_row), ((t, d), BF16, (ROW_TILE, d), full_row),
                         ((d, t), BF16, (d, ROW_TILE), lambda i: (0, i))])

    def res_norm_bwd(name, xs, ys, gate, gi, norm, si, ci, carry, dh, acc):
        dx, dy, dnw_g, dmod_g, dnw_n, dmod_n = tile_bwd(
            name, f_res_norm(gi, si, ci, nct), (nt,), rn_args(xs, ys, gate, norm),
            [(carry, (ROW_TILE, d), full_row), (dh, (ROW_TILE, d), full_row)])
        acc["norm_w"][gate[0]][gate[1]] = dnw_g
        acc["norm_w"][norm[0]][norm[1]] = dnw_n
        acc["mods"][gate[0]].append(dmod_g)
        acc["mods"][norm[0]].append(dmod_n)
        return dx, dy

    sw_rows = ROW_TILE // 2

    def sw_args(gu):
        return [TArg(gu, (sw_rows, 2 * ffn_h), full_row, gdtype=BF16)]

    def ffn_fwd(tag, h2, h2t, layer):
        gu = matmul(f"w13_{tag}", h2, w13[layer][0], "nn", slabs=w13[layer][1], out_dtype=BF16)
        (act,) = tile_fwd(f"swiglu_{tag}", f_swiglu, (t // sw_rows,), sw_args(gu),
                          [((t, ffn_h), BF16, (sw_rows, ffn_h), full_row)])
        return matmul(f"w2_{tag}", act, w2[layer], "nn"), (h2t, gu, act)

    def wgrad(name, a, dy, slabs=False, a_t=None):
        if a_t is not None:
            return matmul(name, a_t, dy, "nn", out_slabs=slabs, out_dtype=BF16)
        return matmul(name, a, dy, "tn", slabs=slabs, out_dtype=BF16)

    def col_grad(name, a, dy, slabs, a_t=None):
        g = wgrad(name, a, dy, slabs, a_t)
        return g if slabs else g.reshape(g.shape[0], N_DEV, g.shape[1] // N_DEV).transpose(1, 0, 2)

    def row_grad(name, a, dy):
        g = matmul(name, a, dy, "tn", out_dtype=BF16)
        return g.reshape(N_DEV, g.shape[0] // N_DEV, g.shape[1])

    def ffn_bwd(tag, saved, layer, dfo, acc):
        h2t, gu, act = saved
        dact = matmul(f"w2d_{tag}", dfo, w2[layer], "nt")
        acc["ffn_w2"][layer] = row_grad(f"w2w_{tag}", act, dfo)
        (dgu,) = tile_bwd(f"swiglub_{tag}", f_swiglu, (t // sw_rows,), sw_args(gu), [(dact, (sw_rows, ffn_h), full_row)])
        acc["ffn_w13"][layer] = col_grad(f"w13w_{tag}", None, dgu, w13[layer][1], h2t)
        return matmul(f"w13d_{tag}", dgu, w13[layer][0], "nt", slabs=w13[layer][1])

    n_a = t // A_CHUNK
    nca = n_ctx // A_CHUNK

    def a_tok(rev):
        if not rev:
            return lambda i: i
        return lambda i: jnp.where(i < nca, nca - 1 - i, n_a + nca - 1 - i)

    hb = _pick(n_ah, (HEADS_PER_STEP, 2, 1))
    n_hblk = n_ah // hb

    def hg_args(p, direction):
        tok = a_tok(direction == 1)
        blk = (A_CHUNK, hb * HEAD)
        return [TArg(p, blk, lambda h, i: (tok(i), h)),
                TArg(p, blk, lambda h, i: (tok(i), (1 + direction) * n_hblk + h)),
                TArg(p, blk, lambda h, i: (tok(i), 3 * n_hblk + h)),
                TArg(lb0r, (None, hb, 1, HEAD), lambda h, i: (direction, h, 0, 0), "par", (1,))]

    pmats = jnp.asarray(np.stack([_pool_mats(n_ctx), _pool_mats(GRID_W)]))

    def pool_args(p):
        return [TArg(p, (ROW_TILE, dg), lambda g, i: (i, 5 * n_grp + g)),
                TArg(pmats, (None, None, ROW_TILE, ROW_TILE), lambda g, i: (jnp.where(i < nct, 0, 1), g, 0, 0), "const"),
                TArg(pw_full, (None, dg, dg), lambda g, i: (g, 0, 0), "par", (1,)),
                TArg(ev_pool_scale, (1, dg), lambda g, i: (0, g), "par", (1,))]

    def ro_plan(gate_off, n_heads):
        per = _pick(n_heads, (8, 4, 2, 1))
        assert gate_off % per == 0
        return per, n_heads // per, gate_off // per

    def ro_args(o_f, o_b, gate_arr, gate_off, nw_arr, n_heads):
        per, _, goff = ro_plan(gate_off, n_heads)
        blk = (ROW_TILE, per * HEAD)
        return [TArg(o_f, blk, lambda h, i: (i, h)), TArg(o_b, blk, lambda h, i: (i, h), grad=False),
                TArg(gate_arr, blk, lambda h, i: (i, goff + h), gdtype=BF16),
                TArg(nw_arr, (1, HEAD), lambda h, i: (0, 0), "par", (0, 1))]

    def readout(name, o_f, o_b, gate_arr, gate_off, nw_arr, n_heads):
        per, nblk, _ = ro_plan(gate_off, n_heads)
        (o,) = tile_fwd(name, f_readout(per), (nblk, nt), ro_args(o_f, o_b, gate_arr, gate_off, nw_arr, n_heads),
                        [((t, n_heads * HEAD), BF16, (ROW_TILE, per * HEAD), lambda hh, i: (i, hh))])
        return o

    def readout_bwd(name, o_f, o_b, gate_arr, gate_off, nw_arr, n_heads, dout):
        per, nblk, _ = ro_plan(gate_off, n_heads)
        return tile_bwd(name, f_readout(per), (nblk, nt), ro_args(o_f, o_b, gate_arr, gate_off, nw_arr, n_heads),
                        [(dout, (ROW_TILE, per * HEAD), lambda hh, i: (i, hh))])

    def even_fwd(tag, h, ht):
        p = matmul(f"win_{tag}", h, w_ev_in[0], "nn", slabs=w_ev_in[1])
        outs, saves = [], []
        for direction in (0, 1):
            (o,), sv = scan_fwd(f"hgrn_{tag}_{direction}", f_hgrn2(direction == 1, hb), n_hblk, n_a, hg_args(p, direction),
                                [((t, aw), F32, (A_CHUNK, hb * HEAD), lambda hh, i, tok=a_tok(direction == 1): (tok(i), hh))], hb)
            outs.append(o)
            saves.append(sv)
        a_out = readout(f"ro_{tag}", outs[0], outs[1], p, 4 * n_ah, ev_a_norm, n_ah)
        (pooled,) = tile_fwd(f"pool_{tag}", f_pool, (n_grp, nt), pool_args(p),
                             [((t, aw), BF16, (ROW_TILE, dg), lambda g, i: (i, g))])
        cat = assemble(f"cat_{tag}", [[(a_out, aw, 0)], [(pooled, aw, 0)]], BF16)
        return matmul(f"wout_{tag}", cat, w_ev_out, "nn"), (ht, p, outs, saves, cat)

    def even_bwd(tag, saved, dy, acc):
        ht, p, outs, saves, cat = saved
        dcat = matmul(f"woutd_{tag}", dy, w_ev_out, "nt")
        acc["ev_w_out"] = row_grad(f"woutw_{tag}", cat, dy)
        do, dgate, d_anorm = readout_bwd(f"rob_{tag}", outs[0], outs[1], p, 4 * n_ah, ev_a_norm, n_ah, dcat)
        du, d_pw, d_ps = tile_bwd(f"poolb_{tag}", f_pool, (n_grp, nt), pool_args(p),
                                  [(dcat, (ROW_TILE, dg), lambda g, i: (i, n_grp + g))])
        dq, df, di, dlb = [], [], [], []
        for direction in (0, 1):
            r = scan_bwd(f"hgrnb_{tag}_{direction}", f_hgrn2(direction == 1, hb), n_hblk, n_a, hg_args(p, direction),
                         saves[direction],
                         [(do, (A_CHUNK, hb * HEAD), lambda hh, i, tok=a_tok(direction == 1): (tok(i), hh))])
            dq.append(r[0])
            df.append(r[1])
            di.append(r[2])
            dlb.append(r[3])
        sec = lambda arr, s: (arr, aw, s)
        dp = assemble(f"dp_{tag}", [[sec(dq[0], 0), sec(dq[1], 0)], [sec(df[0], 1)], [sec(df[1], 2)],
                                    [sec(di[0], 3), sec(di[1], 3)], [sec(dgate, 4)], [sec(du, 5)]], BF16)
        acc["ev_w_in"] = col_grad(f"winw_{tag}", None, dp, w_ev_in[1], ht)
        acc["rs_ev"] = exchange_start("rs_start_ev", [acc["ev_w_in"], acc["ev_w_out"]], True, dp, carry=w_ev_in[0])
        acc["ev_a_norm"] = d_anorm
        acc["ev_pool_w"] = d_pw
        acc["ev_pool_scale"] = d_ps
        acc["lb0"] = jnp.stack([dlb[0][0], dlb[1][1]]).reshape(2, aw)
        return matmul(f"wind_{tag}", dp, acc["rs_ev"][5], "nt", slabs=w_ev_in[1])

    n_c = t // C_CHUNK
    ncc = n_ctx // C_CHUNK

    def c_tok(rev):
        if not rev:
            return lambda i: i
        return lambda i: jnp.where(i < ncc, ncc - 1 - i, n_c + ncc - 1 - i)

    alog = od_A_log[0].reshape(2, n_kh, 2, 1)
    dtb = od_dt_bias[0].reshape(2, n_kh, 2, 1)

    khb = _pick(n_kh, (2 * HEADS_PER_STEP, HEADS_PER_STEP, 2, 1))
    n_kblk = n_kh // khb

    def gd_args(z, gates, direction):
        tok = c_tok(direction == 1)
        gblk = (None, khb, None, 2, C_CHUNK)
        sblk = (None, khb, 2, 1)
        return [TArg(z, (C_CHUNK, khb * HEAD), lambda kb, i: (tok(i), kb)),
                TArg(z, (C_CHUNK, khb * HEAD), lambda kb, i: (tok(i), n_kblk + kb)),
                TArg(z, (C_CHUNK, khb * 2 * HEAD), lambda kb, i: (tok(i), n_kblk + kb)),
                TArg(gates, gblk, lambda kb, i: (direction, kb, tok(i), 0, 0)),
                TArg(gates, gblk, lambda kb, i: (2 + direction, kb, tok(i), 0, 0)),
                TArg(alog, sblk, lambda kb, i: (direction, kb, 0, 0), "par", (1,)),
                TArg(dtb, sblk, lambda kb, i: (direction, kb, 0, 0), "par", (1,))]

    def odd_fwd(tag, h, ht):
        pm = matmul(f"win_{tag}", h, w_od_main, "nn")
        pg = matmul(f"wgate_{tag}", h, w_od_gate, "nn")
        z = conv_fwd(f"conv_{tag}", pm, cv_full, 2 * kw + vw, nct)
        gates = pg.reshape(n_c, C_CHUNK, 4, n_kh, 2).transpose(2, 3, 0, 4, 1)
        outs, saves = [], []
        for direction in (0, 1):
            xrows = 2 * khb * C_CHUNK
            (o, xinv), sv = scan_fwd(
                f"gdn_{tag}_{direction}", f_gdn(direction == 1, khb, False), n_kblk, n_c, gd_args(z, gates, direction),
                [((t, vw), F32, (C_CHUNK, khb * 2 * HEAD), lambda kb, i, tok=c_tok(direction == 1): (tok(i), kb)),
                 ((n_kblk, n_c, xrows, C_CHUNK), F32, (None, None, xrows, C_CHUNK), lambda kb, i: (kb, i, 0, 0))],
                2 * khb)
            outs.append(o)
            saves.append((sv, xinv))
        n_vh = 2 * n_kh
        yo = readout(f"ro_{tag}", outs[0], outs[1], pm, 2 * n_kh + n_vh, od_norm, n_vh)
        return matmul(f"wout_{tag}", yo, w_od_out, "nn"), (ht, pm, z, gates, outs, saves, yo)

    def odd_bwd(tag, saved, dy, acc):
        ht, pm, z, gates, outs, saves, yo = saved
        n_vh = 2 * n_kh
        dyo = matmul(f"woutd_{tag}", dy, w_od_out, "nt")
        acc["od_w_out"] = row_grad(f"woutw_{tag}", yo, dy)
        do, dzg, d_onorm = readout_bwd(f"rob_{tag}", outs[0], outs[1], pm, 2 * n_kh + n_vh, od_norm, n_vh, dyo)
        dq, dk, dv, dga, dgb, dal, ddt = [], [], [], [], [], [], []
        for direction in (0, 1):
            sv, xinv = saves[direction]
            xarg = TArg(xinv, (None, None, 2 * khb * C_CHUNK, C_CHUNK), lambda kb, i: (kb, i, 0, 0), "const")
            r = scan_bwd(f"gdnb_{tag}_{direction}", f_gdn(direction == 1, khb, True), n_kblk, n_c,
                         gd_args(z, gates, direction) + [xarg], sv,
                         [(do, (C_CHUNK, khb * 2 * HEAD), lambda kb, i, tok=c_tok(direction == 1): (tok(i), kb))])
            for lst, v_ in zip((dq, dk, dv, dga, dgb, dal, ddt), r):
                lst.append(v_)
        dz = assemble(f"dz_{tag}", [[(dq[0], kw, 0), (dq[1], kw, 0)], [(dk[0], kw, 1), (dk[1], kw, 1)],
                                    [(dv[0], vw, 1), (dv[1], vw, 1)]], F32)
        dpm, d_conv = conv_bwd(f"convb_{tag}", pm, cv_full, dz, 2 * kw + vw, nct, dzg)
        dgates = jnp.stack([dga[0][0], dga[1][1], dgb[0][2], dgb[1][3]])
        dpg = dgates.transpose(2, 4, 0, 1, 3).reshape(t, n_gate).astype(BF16)
        dh = matmul(f"wgated_{tag}", dpg, w_od_gate, "nt")
        dh = matmul(f"wind_{tag}", dpm, w_od_main, "nt", add=dh)
        dw_in = jnp.concatenate([wgrad(f"winw_{tag}", None, dpm, a_t=ht), wgrad(f"wgatew_{tag}", None, dpg, a_t=ht)],
                                axis=1)
        acc["od_w_in"] = dw_in.reshape(d, N_DEV, dw_in.shape[1] // N_DEV).transpose(1, 0, 2)
        acc["od_norm"] = d_onorm
        acc["od_conv"] = d_conv
        acc["od_A_log"] = jnp.stack([dal[0][0], dal[1][1]]).reshape(1, 2, n_vh)
        acc["od_dt_bias"] = jnp.stack([ddt[0][0], ddt[1][1]]).reshape(1, 2, n_vh)
        return dh

    xs0 = jnp.concatenate([ctx[0], x[0]], axis=0)
    h0, h0t = norm_mod("nm1_l0", xs0, 0, 0, 0, 1)
    y0, sv_e = even_fwd("l0", h0, h0t)
    xs1, h1, h1t = res_norm("rn1_l0", xs0, y0, (0, 1), 2, (0, 2), 3, 4)
    g_w13a, g_w2a = gathered(ag_ffn0, xs1, "ag_wait_ffn0")
    w13[0], w2[0] = col_weight(g_w13a), rows_natural(g_w2a)
    y1, sv_f0 = ffn_fwd("l0", h1, h1t, 0)
    xs2, h2, h2t = res_norm("rn2_l0", xs1, y1, (0, 3), 5, (1, 0), 0, 1)
    g_od_in, g_od_out, g_w13b, g_w2b = gathered(ag_l1, xs2, "ag_wait_l1")
    w_od_in = cols_natural(g_od_in)
    w_od_main, w_od_gate = w_od_in[:, :2 * kw + 2 * vw], w_od_in[:, 2 * kw + 2 * vw:]
    w_od_out = rows_natural(g_od_out)
    w13[1], w2[1] = col_weight(g_w13b), rows_natural(g_w2b)
    y2, sv_o = odd_fwd("l1", h2, h2t)
    xs3, h3, h3t = res_norm("rn1_l1", xs2, y2, (1, 1), 2, (1, 2), 3, 4)
    y3, sv_f1 = ffn_fwd("l1", h3, h3t, 1)
    xs4 = gate_res("gr2_l1", xs3, y3, 1, 3, 5)
    loss_loc, dx4 = loss_kernel("loss", xs4, loss_target[0], nct)
    loss = lax.psum(loss_loc, ("x", "y", "c"))

    acc = {"norm_w": [[None] * 4 for _ in range(depth)], "mods": [[] for _ in range(depth)],
           "ffn_w13": [None] * depth, "ffn_w2": [None] * depth}
    dy3, acc["norm_w"][1][3], dmod = gate_res_bwd("gr2b_l1", xs3, y3, 1, 3, 5, dx4)
    acc["mods"][1].append(dmod)
    dh3 = ffn_bwd("l1", sv_f1, 1, dy3, acc)
    rs_ffn1 = exchange_start("rs_start_ffn1", [acc["ffn_w13"][1], acc["ffn_w2"][1]], True, dh3)
    mods[1] = mods[1] + rs_ffn1[4][0, 0]
    dx3, dy2 = res_norm_bwd("rn1b_l1", xs2, y2, (1, 1), 2, (1, 2), 3, 4, dx4, dh3, acc)
    dh2 = odd_bwd("l1", sv_o, dy2, acc)
    rs_od = exchange_start("rs_start_od", [acc["od_w_in"], acc["od_w_out"]], True, dh2)
    mods[0] = mods[0] + rs_od[4][0, 0]
    dx2, dy1 = res_norm_bwd("rn2b_l0", xs1, y1, (0, 3), 5, (1, 0), 0, 1, dx3, dh2, acc)
    dh1 = ffn_bwd("l0", sv_f0, 0, dy1, acc)
    rs_ffn0 = exchange_start("rs_start_ffn0", [acc["ffn_w13"][0], acc["ffn_w2"][0]], True, dh1)
    mods[0] = mods[0] + rs_ffn0[4][0, 0]
    dx1, dy0 = res_norm_bwd("rn1b_l0", xs0, y0, (0, 1), 2, (0, 2), 3, 4, dx2, dh1, acc)
    dh0 = even_bwd("l0", sv_e, dy0, acc)
    rs_ev = acc["rs_ev"]
    dx0, acc["norm_w"][0][0], dmod = norm_mod_bwd("nm1b_l0", xs0, 0, 0, 0, 1, dh0, dx1)
    acc["mods"][0].append(dmod)
    grad_x = dx0[n_ctx:].reshape(1, seq, d)

    (d_lb_slots) = tile_bwd("lb_bwd", f_lb(0), (1,), lb_slots, [(acc["lb0"], (2, aw), lambda i: (0, 0))])
    d_ev_lb = jnp.stack(d_lb_slots, axis=1)

    dmods = jnp.stack([functools.reduce(jnp.add, acc["mods"][layer]) for layer in range(depth)])
    (dm_all,) = all_gather("ag_dmod", [dmods.reshape(depth * 2 * 6, d)])
    dm_all = dm_all.reshape(N_DEV, depth, 2, 6 * d)
    dm_cols = lax.dynamic_slice_in_dim(dm_all, me * ada_loc, ada_loc, axis=3)
    dm_loc = jnp.concatenate([dm_cols[:, :, 1].transpose(1, 0, 2), dm_cols[:, :, 0].transpose(1, 0, 2)], axis=1)
    d_cctx_part, d_w_ada, d_b_loc = tile_bwd("ada_bwd", f_ada, ada_grid, ada_args,
                                             [(dm_loc, (None, 2 * N_DEV, ada_cb), lambda l, j: (l, 0, j))])

    d_b_full = lax.dynamic_update_slice_in_dim(jnp.zeros_like(b_ada), d_b_loc.reshape(depth, ada_loc), me * ada_loc, axis=1)
    d_nw = jnp.stack([jnp.stack([acc["norm_w"][layer][s].reshape(d) for s in range(4)]) for layer in range(depth)])
    small_grads = [d_cctx_part.reshape(d), d_b_full, d_nw, d_ev_lb, acc["ev_a_norm"], acc["ev_pool_w"],
                   acc["ev_pool_scale"], acc["od_conv"], acc["od_A_log"], acc["od_dt_bias"], acc["od_norm"]]
    sg_shapes = [a.shape for a in small_grads]
    (sg,) = all_gather("ag_smallgrads", [_pack(small_grads, F32, FLAT_W, SUBLANES)])
    sg_sum = sum_leading("sum_smallgrads", sg, F32)
    (g_cctx, g_bada, g_nw, g_lb, g_anorm, g_pw, g_ps, g_conv, g_alog, g_dtb, g_onorm) = _unpack(sg_sum, sg_shapes)

    def my_cols(full, axis):
        loc = full.shape[axis] // N_DEV
        return lax.dynamic_slice_in_dim(full, me * loc, loc, axis=axis)

    grads = {
        "c_ctx": g_cctx, "w_ada": d_w_ada, "b_ada": g_bada, "norm_w": my_cols(g_nw, 2), "ev_lb": my_cols(g_lb, 2),
        "ev_a_norm": g_anorm, "ev_pool_w": my_cols(g_pw, 1)[None], "ev_pool_scale": g_ps,
        "od_conv": my_cols(g_conv, 1)[None], "od_A_log": g_alog, "od_dt_bias": g_dtb, "od_norm": g_onorm,
    }

    def reduced(started, tags_, name):
        srcs, lands = exchange_wait(name, started, sg_sum, True)
        out = []
        for tg, s_, l_ in zip(tags_, srcs, lands):
            own = lax.dynamic_index_in_dim(s_, me, 0, keepdims=True)
            out.append(sum_leading(f"rs_sum_{tg}", lax.dynamic_update_slice_in_dim(l_, own, me, 0), F32))
        return out

    g_w13b, g_w2b = reduced(rs_ffn1, ["w13b", "w2b"], "rs_wait_ffn1")
    g_od_in, g_od_out = reduced(rs_od, ["od_in", "od_out"], "rs_wait_od")
    g_w13a, g_w2a = reduced(rs_ffn0, ["w13a", "w2a"], "rs_wait_ffn0")
    g_ev_in, g_ev_out = reduced(rs_ev, ["ev_in", "ev_out"], "rs_wait_ev")
    grads["ev_w_in"], grads["ev_w_out"], grads["od_w_in"], grads["od_w_out"] = (g_ev_in[None], g_ev_out[None],
                                                                                g_od_in[None], g_od_out[None])
    grads["ffn_w13"] = jnp.stack([g_w13a, g_w13b])
    grads["ffn_w2"] = jnp.stack([g_w2a, g_w2b])

    big_names = ["w_ada", "ev_w_in", "ev_w_out", "od_w_in", "od_w_out", "ffn_w13", "ffn_w2"]
    small_names = [n_ for n_ in names if n_ not in big_names]
    gl = {n_: grads[n_].reshape(wts[n_].shape) for n_ in names}
    delta, new_m, new_v = {}, {}, {}
    for n_ in big_names:
        shp = wts[n_].shape
        res = adamw(f"adamw_{n_}", _rows2d(gl[n_]), _rows2d(wts[n_]), _rows2d(mom1[n_]), _rows2d(mom2[n_]))
        delta[n_], new_m[n_], new_v[n_] = (r_.reshape(shp) for r_ in res)
    shapes = [wts[n_].shape for n_ in small_names]
    pk = lambda dct: _pack([dct[n_] for n_ in small_names], F32, FLAT_W, SUBLANES)
    res = adamw("adamw_small", pk(gl), pk(wts), pk(mom1), pk(mom2))
    for dct, r_ in zip((delta, new_m, new_v), res):
        for n_, a_ in zip(small_names, _unpack(r_, shapes)):
            dct[n_] = a_
    return (loss, grad_x, *[gl[n_] for n_ in names], *[delta[n_] for n_ in names], *[new_m[n_] for n_ in names],
            *[new_v[n_] for n_ in names])
```

```python
import functools
from typing import Any, NamedTuple

import numpy as np

import jax
import jax.numpy as jnp
from jax import lax
from jax.experimental import pallas as pl
from jax.experimental.pallas import tpu as pltpu

F32 = jnp.float32
BF16 = jnp.bfloat16
MESH = pl.DeviceIdType.MESH
N_DEV = 8

EPS = 1e-6
GRID_W = 64
HEAD = 128
A_CHUNK = 32
C_CHUNK = 64
C_CONV = 4
POOL_WINDOWS = (2, 4, 8, 16)
MASKED_EXPONENT = -1e30
ADAM_LR, ADAM_B1, ADAM_B2, ADAM_EPS, ADAM_WD, ADAM_STEP = 0.001, 0.9, 0.999, 1e-08, 0.01, 10

VMEM_LIMIT_BYTES = 56 * 1024 * 1024
LANES = 128
SUBLANES = 8
ROW_TILE = 256
FLAT_W = 1024
TM_PREFS = (1056, 1024, 768, 512, 256, 128, 64, 32, 16)
TN_PREFS = (768, 1024, 512, 1408, 256, 128)
TK_PREFS = (2048, 2816, 1408, 1024, 768, 512, 384, 256, 128)
TO_PREFS = (1024, 1408, 768, 704, 512, 384, 256, 128)
HEADS_PER_STEP = 8


def _pick(dim, prefs):
    for p in prefs:
        if p <= dim and dim % p == 0:
            return p
    return dim


def _cparams(ngrid):
    return pltpu.CompilerParams(dimension_semantics=("arbitrary",) * ngrid, vmem_limit_bytes=VMEM_LIMIT_BYTES)


def _sds(shape, dtype):
    return jax.ShapeDtypeStruct(tuple(shape), dtype)


def _split(x):
    hi = x.astype(BF16)
    return hi, (x - hi.astype(F32)).astype(BF16)


def _dot(a, b, ca, cb, hi):
    dims = (((ca,), (cb,)), ((), ()))
    dot = lambda u, v: lax.dot_general(u, v, dims, preferred_element_type=F32)
    if hi:
        (ah, al), (bh, bl) = _split(a.astype(F32)), _split(b.astype(F32))
        return dot(ah, bh) + (dot(ah, bl) + dot(al, bh))
    return dot(a.astype(BF16), b.astype(BF16))


@functools.partial(jax.custom_vjp, nondiff_argnums=(2, 3, 4))
def mm(a, b, ca=1, cb=0, hi=False):
    return _dot(a, b, ca, cb, hi)


def _mm_fwd(a, b, ca, cb, hi):
    return _dot(a, b, ca, cb, hi), (a, b)


HI_FWD, HI_BOTH = 1, 2


def _mm_bwd(ca, cb, hi, res, g):
    a, b = res
    bhi = hi == HI_BOTH
    da = _dot(g, b, 1, 1 - cb, bhi) if ca == 1 else _dot(b, g, 1 - cb, 1, bhi)
    db = _dot(a, g, 1 - ca, 0, bhi) if cb == 0 else _dot(g, a, 0, 1 - ca, bhi)
    return da, db


mm.defvjp(_mm_fwd, _mm_bwd)


def _iota2(n, m, axis):
    return lax.broadcasted_iota(jnp.int32, (n, m), axis)


class TArg(NamedTuple):
    arr: Any
    block: tuple
    imap: Any
    kind: str = "row"
    acc: tuple = ()
    gdtype: Any = F32
    grad: bool = True


def _load(ref):
    v = ref[...]
    return v.astype(F32) if jnp.issubdtype(v.dtype, jnp.floating) else v


def tile_fwd(name, f, grid, args, outs):
    n_in, ng = len(args), len(grid)

    def body(*refs):
        pids = tuple(pl.program_id(k) for k in range(ng))
        res = f(pids, *[_load(r) for r in refs[:n_in]])
        for r, v in zip(refs[n_in:], res):
            r[...] = v.astype(r.dtype)

    return pl.pallas_call(
        body, grid=grid, name=name,
        in_specs=[pl.BlockSpec(a.block, a.imap) for a in args],
        out_specs=[pl.BlockSpec(b, im) for (_, _, b, im) in outs],
        out_shape=[_sds(s, d) for (s, d, _, _) in outs],
        compiler_params=_cparams(ng),
    )(*[a.arr for a in args])


def _store_grads(args, diff, pids, g_refs, d):
    for k, gr, dv in zip(diff, g_refs, d):
        a = args[k]
        if a.kind == "row" or not a.acc:
            gr[...] = dv.astype(gr.dtype)
        else:
            first = pids[a.acc[0]] == 0
            for ax in a.acc[1:]:
                first = jnp.logical_and(first, pids[ax] == 0)

            @pl.when(first)
            def _(gr=gr, dv=dv):
                gr[...] = dv.astype(gr.dtype)

            @pl.when(jnp.logical_not(first))
            def _(gr=gr, dv=dv):
                gr[...] += dv.astype(gr.dtype)


def tile_bwd(name, f, grid, args, cts):
    n_in, n_ct, ng = len(args), len(cts), len(grid)
    diff = [k for k, a in enumerate(args) if a.kind != "const" and a.grad]

    def body(*refs):
        pids = tuple(pl.program_id(k) for k in range(ng))
        vals = [_load(r) for r in refs[:n_in]]

        def g(*dv):
            full = list(vals)
            for k, v in zip(diff, dv):
                full[k] = v
            return tuple(f(pids, *full))

        _, vjp = jax.vjp(g, *[vals[k] for k in diff])
        d = vjp(tuple(_load(r) for r in refs[n_in:n_in + n_ct]))
        _store_grads(args, diff, pids, refs[n_in + n_ct:], d)

    return pl.pallas_call(
        body, grid=grid, name=name,
        in_specs=[pl.BlockSpec(a.block, a.imap) for a in args] + [pl.BlockSpec(b, im) for (_, b, im) in cts],
        out_specs=[pl.BlockSpec(args[k].block, args[k].imap) for k in diff],
        out_shape=[_sds(args[k].arr.shape, args[k].gdtype) for k in diff],
        compiler_params=_cparams(ng),
    )(*[a.arr for a in args], *[c[0] for c in cts])


def scan_fwd(name, f, n_heads, n_steps, args, outs, n_state):
    n_in, n_out = len(args), len(outs)
    sblock = (None, None, HEAD, HEAD)

    def body(*refs):
        in_refs = refs[:n_in]
        out_refs = refs[n_in:n_in + n_out]
        save_refs = refs[n_in + n_out:n_in + n_out + n_state]
        s_refs = refs[n_in + n_out + n_state:]

        @pl.when(pl.program_id(1) == 0)
        def _():
            for s in s_refs:
                s[...] = jnp.zeros_like(s)

        states = tuple(s[...] for s in s_refs)
        for sv, s in zip(save_refs, states):
            sv[...] = s
        new_states, res = f(states, *[_load(r) for r in in_refs])
        for s, v in zip(s_refs, new_states):
            s[...] = v
        for r, v in zip(out_refs, res):
            r[...] = v.astype(r.dtype)

    res = pl.pallas_call(
        body, grid=(n_heads, n_steps), name=name,
        in_specs=[pl.BlockSpec(a.block, a.imap) for a in args],
        out_specs=[pl.BlockSpec(b, im) for (_, _, b, im) in outs]
        + [pl.BlockSpec(sblock, lambda h, i: (h, i, 0, 0))] * n_state,
        out_shape=[_sds(s, d) for (s, d, _, _) in outs] + [_sds((n_heads, n_steps, HEAD, HEAD), F32)] * n_state,
        scratch_shapes=[pltpu.VMEM((HEAD, HEAD), F32)] * n_state,
        compiler_params=_cparams(2),
    )(*[a.arr for a in args])
    return res[:n_out], res[n_out:]


def scan_bwd(name, f, n_heads, n_steps, args, saves, cts):
    n_in, n_ct, n_state = len(args), len(cts), len(saves)
    diff = [k for k, a in enumerate(args) if a.kind != "const" and a.grad]
    sblock = (None, None, HEAD, HEAD)

    def rv(im):
        return lambda h, i: im(h, n_steps - 1 - i)

    def body(*refs):
        in_refs = refs[:n_in]
        save_refs = refs[n_in:n_in + n_state]
        ct_refs = refs[n_in + n_state:n_in + n_state + n_ct]
        g_refs = refs[n_in + n_state + n_ct:n_in + n_state + n_ct + len(diff)]
        ds_refs = refs[n_in + n_state + n_ct + len(diff):]
        pids = (pl.program_id(0), pl.program_id(1))

        @pl.when(pids[1] == 0)
        def _():
            for s in ds_refs:
                s[...] = jnp.zeros_like(s)

        vals = [_load(r) for r in in_refs]

        def g(states, *dv):
            full = list(vals)
            for k, v in zip(diff, dv):
                full[k] = v
            new_states, res = f(states, *full)
            return tuple(new_states), tuple(res)

        _, vjp = jax.vjp(g, tuple(s[...] for s in save_refs), *[vals[k] for k in diff])
        d = vjp((tuple(s[...] for s in ds_refs), tuple(_load(r) for r in ct_refs)))
        for s, v in zip(ds_refs, d[0]):
            s[...] = v
        _store_grads(args, diff, pids, g_refs, d[1:])

    return pl.pallas_call(
        body, grid=(n_heads, n_steps), name=name,
        in_specs=[pl.BlockSpec(a.block, rv(a.imap)) for a in args]
        + [pl.BlockSpec(sblock, rv(lambda h, i: (h, i, 0, 0)))] * n_state
        + [pl.BlockSpec(b, rv(im)) for (_, b, im) in cts],
        out_specs=[pl.BlockSpec(args[k].block, rv(args[k].imap)) for k in diff],
        out_shape=[_sds(args[k].arr.shape, args[k].gdtype) for k in diff],
        scratch_shapes=[pltpu.VMEM((HEAD, HEAD), F32)] * n_state,
        compiler_params=_cparams(2),
    )(*[a.arr for a in args], *saves, *[c[0] for c in cts])


def matmul(name, a, b, mode, add=None, out_dtype=F32, slabs=False, out_slabs=False):
    o_spec = None
    if mode == "nn":
        m, k = a.shape
        ns = b.shape[2] if slabs else (b.shape[1] // N_DEV if out_slabs else b.shape[1])
        n = N_DEV * ns if (slabs or out_slabs) else ns
        to_m, to_n, tr = _pick(m, TM_PREFS), _pick(ns, TN_PREFS), _pick(k, TK_PREFS)
        nb = ns // to_n
        grid = (m // to_m, n // to_n, k // tr)
        a_spec = pl.BlockSpec((to_m, tr), lambda i, j, l: (i, l))
        if slabs:
            b_spec = pl.BlockSpec((None, tr, to_n), lambda i, j, l: (j // nb, l, j % nb))
        else:
            b_spec = pl.BlockSpec((tr, to_n), lambda i, j, l: (l, j))
        dims, oshape = (1, 0), (m, n)
        if out_slabs:
            o_spec = pl.BlockSpec((None, to_m, to_n), lambda i, j, l: (j // nb, i, j % nb))
            oshape = (N_DEV, m, ns)
    elif mode == "nt":
        m, n = a.shape
        k = b.shape[1] if slabs else b.shape[0]
        ns = n // N_DEV if slabs else n
        to_m, to_n, tr = _pick(m, TM_PREFS), _pick(k, TO_PREFS), _pick(ns, TK_PREFS)
        nb = ns // tr
        grid = (m // to_m, k // to_n, n // tr)
        a_spec = pl.BlockSpec((to_m, tr), lambda i, j, l: (i, l))
        if slabs:
            b_spec = pl.BlockSpec((None, to_n, tr), lambda i, j, l: (l // nb, j, l % nb))
        else:
            b_spec = pl.BlockSpec((to_n, tr), lambda i, j, l: (j, l))
        dims, oshape = (1, 1), (m, k)
    else:
        (t, k), n = a.shape, b.shape[1]
        ns = n // N_DEV if slabs else n
        to_m, to_n, tr = _pick(k, TO_PREFS), _pick(ns, (2048,) + TO_PREFS), _pick(t, TM_PREFS)
        nb = ns // to_n
        grid = (k // to_m, n // to_n, t // tr)
        a_spec = pl.BlockSpec((tr, to_m), lambda i, j, l: (l, i))
        b_spec = pl.BlockSpec((tr, to_n), lambda i, j, l: (l, j))
        dims, oshape = (0, 0), (k, n)
        if slabs:
            o_spec = pl.BlockSpec((None, to_m, to_n), lambda i, j, l: (j // nb, i, j % nb))
            oshape = (N_DEV, k, ns)
    n_red = grid[2]
    if o_spec is None:
        o_spec = pl.BlockSpec((to_m, to_n), lambda i, j, l: (i, j))
    has_add = add is not None

    def body(a_ref, b_ref, *rest):
        add_ref = rest[0] if has_add else None
        o_ref = rest[1] if has_add else rest[0]
        part = lax.dot_general(a_ref[...].astype(BF16), b_ref[...].astype(BF16),
                               (((dims[0],), (dims[1],)), ((), ())), preferred_element_type=F32)

        def finish(v):
            if has_add:
                v = v + add_ref[...]
            o_ref[...] = v.astype(o_ref.dtype)

        if n_red == 1:
            finish(part)
        else:
            acc = rest[-1]
            step = pl.program_id(2)

            @pl.when(step == 0)
            def _():
                acc[...] = part

            @pl.when(step > 0)
            def _():
                acc[...] += part

            @pl.when(step == n_red - 1)
            def _():
                finish(acc[...])

    return pl.pallas_call(
        body, grid=grid, name=name,
        in_specs=[a_spec, b_spec] + ([o_spec] if has_add else []),
        out_specs=o_spec, out_shape=_sds(oshape, out_dtype),
        scratch_shapes=[pltpu.VMEM((to_m, to_n), F32)] if n_red > 1 else [],
        compiler_params=_cparams(3),
    )(a, b, *([add] if has_add else []))


def assemble(name, pieces, out_dtype):
    flat = [s for piece in pieces for s in piece]
    t = flat[0][0].shape[0]
    widths = [piece[0][1] for piece in pieces]

    def body(*refs):
        o_ref, k, off = refs[-1], 0, 0
        for piece, w in zip(pieces, widths):
            v = refs[k][...].astype(F32)
            k += 1
            for _ in piece[1:]:
                v = v + refs[k][...].astype(F32)
                k += 1
            o_ref[:, off:off + w] = v.astype(o_ref.dtype)
            off += w

    tr = ROW_TILE // 2
    return pl.pallas_call(
        body, grid=(t // tr,), name=name,
        in_specs=[pl.BlockSpec((tr, w), functools.partial(lambda i, cb: (i, cb), cb=cb)) for (_, w, cb) in flat],
        out_specs=pl.BlockSpec((tr, sum(widths)), lambda i: (i, 0)),
        out_shape=_sds((t, sum(widths)), out_dtype),
        compiler_params=_cparams(1),
    )(*[s[0] for s in flat])


ELEM_ROWS = (128, 64, 32, 16, 8)


def _rows2d(a, lead=0):
    return a.reshape(a.shape[:lead] + (-1, a.shape[-1]))


def sum_leading(name, arr, out_dtype):
    k, rows, w = arr.shape
    tr = _pick(rows, ELEM_ROWS)

    def body(a_ref, o_ref):
        v = a_ref[0].astype(F32)
        for j in range(1, k):
            v = v + a_ref[j].astype(F32)
        o_ref[...] = v.astype(o_ref.dtype)

    return pl.pallas_call(
        body, grid=(rows // tr,), name=name,
        in_specs=[pl.BlockSpec((k, tr, w), lambda i: (0, i, 0))],
        out_specs=pl.BlockSpec((tr, w), lambda i: (i, 0)),
        out_shape=_sds((rows, w), out_dtype), compiler_params=_cparams(1),
    )(arr)


def adamw(name, g, w, m, v):
    rows, wd = g.shape
    tr = _pick(rows, ELEM_ROWS)

    def body(g_ref, w_ref, m_ref, v_ref, d_ref, nm_ref, nv_ref):
        gv = g_ref[...]
        mn = ADAM_B1 * m_ref[...] + (1.0 - ADAM_B1) * gv
        vn = ADAM_B2 * v_ref[...] + (1.0 - ADAM_B2) * jnp.square(gv)
        m_hat = mn / (1.0 - ADAM_B1 ** ADAM_STEP)
        v_hat = vn / (1.0 - ADAM_B2 ** ADAM_STEP)
        d_ref[...] = -ADAM_LR * (m_hat / (jnp.sqrt(v_hat) + ADAM_EPS) + ADAM_WD * w_ref[...])
        nm_ref[...] = mn
        nv_ref[...] = vn

    spec = pl.BlockSpec((tr, wd), lambda i: (i, 0))
    return pl.pallas_call(
        body, grid=(rows // tr,), name=name, in_specs=[spec] * 4, out_specs=[spec] * 3,
        out_shape=[_sds(g.shape, F32)] * 3, compiler_params=_cparams(1),
    )(g, w, m, v)


def loss_kernel(name, xs, target, n_ctx_tiles):
    t, d = xs.shape
    nt = t // ROW_TILE

    def body(x_ref, t_ref, dx_ref, l_ref):
        i = pl.program_id(0)
        is_lat = i >= n_ctx_tiles
        err = jnp.where(is_lat, x_ref[...] - t_ref[...], 0.0)
        dx_ref[...] = err / d
        part = 0.5 * jnp.sum(jnp.mean(jnp.square(err), axis=-1, keepdims=True), axis=0, keepdims=True)

        @pl.when(i == 0)
        def _():
            l_ref[...] = jnp.zeros_like(l_ref)

        l_ref[...] += jnp.broadcast_to(part, l_ref.shape)

    dx, l = pl.pallas_call(
        body, grid=(nt,), name=name,
        in_specs=[pl.BlockSpec((ROW_TILE, d), lambda i: (i, 0)),
                  pl.BlockSpec((ROW_TILE, d), lambda i: (jnp.maximum(i - n_ctx_tiles, 0), 0))],
        out_specs=[pl.BlockSpec((ROW_TILE, d), lambda i: (i, 0)), pl.BlockSpec((SUBLANES, LANES), lambda i: (0, 0))],
        out_shape=[_sds((t, d), F32), _sds((SUBLANES, LANES), F32)], compiler_params=_cparams(1),
    )(xs, target)
    return l[0, 0], dx


CONV_COLS = 2048
CONV_LEFT = C_CONV // 2


def _conv_halo_specs(t, n_ctx_tiles):
    nt = t // ROW_TILE
    per = ROW_TILE // SUBLANES
    cur = pl.BlockSpec((ROW_TILE, CONV_COLS), lambda j, i: (i, j))
    prev = pl.BlockSpec((SUBLANES, CONV_COLS), lambda j, i: (jnp.maximum(i * per - 1, 0), j))
    nxt = pl.BlockSpec((SUBLANES, CONV_COLS), lambda j, i: (jnp.minimum((i + 1) * per, nt * per - 1), j))
    return cur, prev, nxt


def _fill_ext(ext, prev_ref, cur_ref, next_ref, i, nt, n_ctx_tiles):
    has_prev = jnp.logical_and(i != 0, i != n_ctx_tiles)
    has_next = jnp.logical_and(i != n_ctx_tiles - 1, i != nt - 1)
    ext[0:SUBLANES, :] = jnp.where(has_prev, prev_ref[...], 0.0)
    ext[SUBLANES:SUBLANES + ROW_TILE, :] = cur_ref[...]
    ext[SUBLANES + ROW_TILE:, :] = jnp.where(has_next, next_ref[...], 0.0)


def conv_fwd(name, p, w, width, n_ctx_tiles):
    t = p.shape[0]
    nt = t // ROW_TILE
    cur, prev, nxt = _conv_halo_specs(t, n_ctx_tiles)

    def body(c_ref, p_ref, n_ref, w_ref, o_ref, ext):
        _fill_ext(ext, p_ref, c_ref, n_ref, pl.program_id(1), nt, n_ctx_tiles)
        acc = None
        for j in range(C_CONV):
            term = ext[pl.ds(SUBLANES + j - CONV_LEFT, ROW_TILE), :] * w_ref[j:j + 1, :]
            acc = term if acc is None else acc + term
        o_ref[...] = acc

    return pl.pallas_call(
        body, grid=(width // CONV_COLS, nt), name=name,
        in_specs=[cur, prev, nxt, pl.BlockSpec((C_CONV, CONV_COLS), lambda j, i: (0, j))],
        out_specs=cur, out_shape=_sds((t, width), F32),
        scratch_shapes=[pltpu.VMEM((ROW_TILE + 2 * SUBLANES, CONV_COLS), F32)],
        compiler_params=_cparams(2),
    )(p, p, p, w)


def conv_bwd(name, p, w, dz, width, n_ctx_tiles, into):
    t = p.shape[0]
    nt = t // ROW_TILE
    cur, prev, nxt = _conv_halo_specs(t, n_ctx_tiles)

    def body(c_ref, p_ref, n_ref, dc_ref, dp_ref, dn_ref, w_ref, into_ref, du_ref, dw_ref, ext, dext):
        i = pl.program_id(1)
        _fill_ext(ext, p_ref, c_ref, n_ref, i, nt, n_ctx_tiles)
        _fill_ext(dext, dp_ref, dc_ref, dn_ref, i, nt, n_ctx_tiles)
        dzc = dc_ref[...]
        @pl.when(i == 0)
        def _():
            dw_ref[...] = jnp.zeros_like(dw_ref)

        acc = None
        for j in range(C_CONV):
            term = dext[pl.ds(SUBLANES + CONV_LEFT - j, ROW_TILE), :] * w_ref[j:j + 1, :]
            acc = term if acc is None else acc + term
            dw_ref[j:j + 1, :] += jnp.sum(dzc * ext[pl.ds(SUBLANES + j - CONV_LEFT, ROW_TILE), :], axis=0, keepdims=True)
        du_ref[...] = acc.astype(du_ref.dtype)

    wspec = pl.BlockSpec((C_CONV, CONV_COLS), lambda j, i: (0, j))
    return pl.pallas_call(
        body, grid=(width // CONV_COLS, nt), name=name,
        in_specs=[cur, prev, nxt, cur, prev, nxt, wspec, pl.BlockSpec(memory_space=pl.ANY)],
        out_specs=[cur, wspec], out_shape=[_sds(into.shape, into.dtype), _sds((C_CONV, width), F32)],
        input_output_aliases={7: 0},
        scratch_shapes=[pltpu.VMEM((ROW_TILE + 2 * SUBLANES, CONV_COLS), F32)] * 2,
        compiler_params=_cparams(2),
    )(p, p, p, dz, dz, dz, w, into)


def _rms(x, w):
    return x * lax.rsqrt(jnp.mean(x * x, axis=-1, keepdims=True) + EPS) * w


def _seg_mod(mods, is_ctx):
    return jnp.where(is_ctx, mods[0], mods[1])


def f_norm_mod(shift_i, scale_i, n_ctx_tiles, passthrough=False, transposed=False):
    def f(pids, x, nw, mods):
        m = _seg_mod(mods, pids[0] < n_ctx_tiles)
        h = _rms(x, nw) * (1.0 + m[scale_i:scale_i + 1]) + m[shift_i:shift_i + 1]
        if transposed:
            return (h, h.T)
        return (h, x) if passthrough else (h,)
    return f


def f_gate_res(gate_i, n_ctx_tiles):
    def f(pids, x, y, nw, mods):
        m = _seg_mod(mods, pids[0] < n_ctx_tiles)
        return (x + m[gate_i:gate_i + 1] * _rms(y, nw),)
    return f


def f_res_norm(gate_i, shift_i, scale_i, n_ctx_tiles, transposed=False):
    def f(pids, x, y, nw_g, mods_g, nw_n, mods_n):
        is_ctx = pids[0] < n_ctx_tiles
        mg, mn = _seg_mod(mods_g, is_ctx), _seg_mod(mods_n, is_ctx)
        x1 = x + mg[gate_i:gate_i + 1] * _rms(y, nw_g)
        h = _rms(x1, nw_n) * (1.0 + mn[scale_i:scale_i + 1]) + mn[shift_i:shift_i + 1]
        return (x1, h, h.T) if transposed else (x1, h)
    return f


def f_swiglu(pids, gu):
    half = gu.shape[1] // 2
    return (jax.nn.silu(gu[:, :half]) * gu[:, half:],)


def f_readout(n_heads):
    def f(pids, o_a, o_b, gate, nw):
        cols = [slice(j * HEAD, (j + 1) * HEAD) for j in range(n_heads)]
        outs = _each(lambda cs: _rms(o_a[:, cs] + o_b[:, cs], nw) * jax.nn.silu(gate[:, cs]), cols)
        return (jnp.concatenate(outs, axis=1) if n_heads > 1 else outs[0],)
    return f


def f_pool(pids, u, pmat, pw, scale):
    d = mm(pmat, u, 1, 0, HI_BOTH) - u
    return (mm(d, pw) * scale,)


def f_lb(layer):
    def f(pids, *slots):
        top = slots[0]
        for s in slots[1:]:
            top = jnp.maximum(top, s)
        ex = [jnp.exp(s - top) for s in slots]
        tot = ex[0]
        for e in ex[1:]:
            tot = tot + e
        part = ex[0]
        for e in ex[1:layer + 1]:
            part = part + e
        return (part / tot,)
    return f


def f_ada(pids, c_all, c_ctx, w, b):
    c16 = jnp.concatenate([c_all, jnp.broadcast_to(c_ctx, c_all.shape)], axis=0)
    return (mm(jax.nn.silu(c16), w) + b,)


def _each(fn, *lists):
    return [fn(*xs) for xs in zip(*lists)]


def _hgrn2_heads(sts, qrs, frs, irs, lbs, rev):
    c = A_CHUNK
    mid = c - c // 2 if rev else c // 2 - 1
    ri, ci = _iota2(c, c, 0), _iota2(c, c, 1)
    incl = (ri <= ci) if rev else (ri >= ci)
    incl_f = incl.astype(F32)
    qs = _each(jax.nn.silu, qrs)
    log_fs = _each(lambda lb, fr: jnp.log(lb + (1.0 - lb) * jax.nn.sigmoid(fr)), lbs, frs)
    ks = _each(lambda lb, fr: (1.0 - lb) * jax.nn.sigmoid(-fr), lbs, frs)
    bs = _each(lambda lf: mm(incl_f, lf, 1, 0, HI_BOTH), log_fs)
    b_lasts = _each(lambda lf: jnp.sum(lf, axis=0, keepdims=True), log_fs)
    scores = _each(lambda q, k, b: mm(q * jnp.exp(b - b[mid:mid + 1]), k * jnp.exp(b[mid:mid + 1] - b), 1, 1), qs, ks, bs)
    intra = _each(lambda sc, ir: mm(jnp.where(incl, sc, 0.0), ir), scores, irs)
    inter = _each(lambda q, b, st: mm(q * jnp.exp(b), st, 1, 1), qs, bs, sts)
    upd = _each(lambda ir, k, bl, b: mm(ir, k * jnp.exp(bl - b), 0, 0), irs, ks, b_lasts, bs)
    new = _each(lambda st, bl, u: st * jnp.exp(bl) + u, sts, b_lasts, upd)
    return new, _each(jnp.add, intra, inter)


def f_hgrn2(rev, hb):
    def f(states, qr, fr, ir, lb):
        cols = [slice(j * HEAD, (j + 1) * HEAD) for j in range(hb)]
        new, outs = _hgrn2_heads(list(states), [qr[:, cs] for cs in cols], [fr[:, cs] for cs in cols],
                                 [ir[:, cs] for cs in cols], [lb[j] for j in range(hb)], rev)
        return tuple(new), (jnp.concatenate(outs, axis=1) if hb > 1 else outs[0],)
    return f


def _neumann_inv(a_lows):
    n = a_lows[0].shape[0]
    eye = (_iota2(n, n, 0) == _iota2(n, n, 1)).astype(F32)
    ps = _each(lambda a: -a, a_lows)
    xs = _each(lambda p: eye + p, ps)
    ps = _each(lambda p: mm(p, p, 1, 0, HI_FWD), ps)
    k = 2
    while 2 * k < n:
        ys = _each(lambda p, x: mm(jnp.concatenate([p, x], axis=0), p, 1, 0, HI_FWD), ps, xs)
        ps = _each(lambda y: y[:n], ys)
        xs = _each(lambda x, y: x + y[n:], xs, ys)
        k *= 2
    return tuple(_each(lambda x, p: x + mm(x, p, 1, 0, HI_FWD), xs, ps))


@jax.custom_vjp
def unit_tri_inv(a_lows):
    return _neumann_inv(a_lows)


def _uti_fwd(a_lows):
    xs = _neumann_inv(a_lows)
    return xs, xs


def _uti_bwd(xs, gs):
    ts = _each(lambda x, g: mm(x, g, 0, 0), xs, gs)
    return (tuple(_each(lambda t, x: -mm(t, x, 1, 1), ts, xs)),)


unit_tri_inv.defvjp(_uti_fwd, _uti_bwd)


@jax.custom_vjp
def unit_tri_inv_saved(a_lows, xs):
    return xs


def _utis_fwd(a_lows, xs):
    return xs, xs


def _utis_bwd(xs, gs):
    return _uti_bwd(xs, gs) + (tuple(jnp.zeros_like(x) for x in xs),)


unit_tri_inv_saved.defvjp(_utis_fwd, _utis_bwd)


def _l2n(x):
    return x * lax.rsqrt(jnp.sum(x * x, axis=-1, keepdims=True) + EPS)


def _gdn_heads(ss, qs, ks, vs, a_rows, b_rows, alogs, dtbs, rev, xs_saved=None):
    c = qs[0].shape[0]
    ri, ci = _iota2(c, c, 0), _iota2(c, c, 1)
    causal = (ri <= ci) if rev else (ri >= ci)
    causal_t = (ri >= ci) if rev else (ri <= ci)
    strict = (ri < ci) if rev else (ri > ci)
    eye = ri == ci
    sq = lambda row: jnp.broadcast_to(row, (c, c))
    to_col = lambda row: jnp.sum(jnp.where(eye, sq(row), 0.0), axis=1, keepdims=True)
    g_rows = _each(lambda al, a, dt: -jnp.exp(al) * jax.nn.softplus(a + dt), alogs, a_rows, dtbs)
    beta_cols = _each(lambda b: to_col(jax.nn.sigmoid(b)), b_rows)
    g_cols = _each(to_col, g_rows)
    gc_cols = _each(lambda g: jnp.sum(jnp.where(causal, sq(g), 0.0), axis=1, keepdims=True), g_rows)
    gc_rows = _each(lambda g: jnp.sum(jnp.where(causal_t, sq(g), 0.0), axis=0, keepdims=True), g_cols)
    gc_lasts = _each(lambda g: jnp.sum(g, axis=1, keepdims=True), g_rows)
    decays = _each(lambda gc, gr: jnp.exp(jnp.where(causal, gc - gr, MASKED_EXPONENT)), gc_cols, gc_rows)
    k_betas = _each(jnp.multiply, ks, beta_cols)
    v_betas = _each(jnp.multiply, vs, beta_cols)
    kq_ks = _each(lambda kb, q, k: mm(jnp.concatenate([kb, q], axis=0), k, 1, 1), k_betas, qs, ks)
    a_lows = _each(lambda kk, dec: jnp.where(strict, kk[:c] * dec, 0.0), kq_ks, decays)
    qks = _each(lambda kk, dec: kk[c:] * dec, kq_ks, decays)
    xs = unit_tri_inv(tuple(a_lows)) if xs_saved is None else unit_tri_inv_saved(tuple(a_lows), tuple(xs_saved))
    egcs = _each(jnp.exp, gc_cols)
    uws = _each(lambda x, vb, kb, e: mm(x, jnp.concatenate([vb, kb * e], axis=1), 1, 0, HI_FWD), xs, v_betas, k_betas, egcs)
    dv = vs[0].shape[1]
    wq_ss = _each(lambda uw, q, e, s: mm(jnp.concatenate([uw[:, dv:], q * e], axis=0), s), uws, qs, egcs, ss)
    v_news = _each(lambda uw, wq: uw[:, :dv] - wq[:c], uws, wq_ss)
    o_states = _each(lambda wq: wq[c:], wq_ss)
    o_locals = _each(mm, qks, v_news)
    upds = _each(lambda k, gl, gc, vn: mm(k * jnp.exp(gl - gc), vn, 0, 0), ks, gc_lasts, gc_cols, v_news)
    new = _each(lambda s, gl, u: s * jnp.exp(gl) + u, ss, gc_lasts, upds)
    return new, _each(jnp.add, o_states, o_locals), xs


def f_gdn(rev, khb, saved_inverse):
    def f(states, qr, kr, vr, a3, b3, alog3, dtb3, xcat=None):
        heads = [(j, r) for j in range(khb) for r in range(2)]
        cols = [slice(j * HEAD, (j + 1) * HEAD) for j in range(khb)]
        c = qr.shape[0]
        qk_ = _each(lambda cs: (_l2n(jax.nn.silu(qr[:, cs])) * (HEAD ** -0.5), _l2n(jax.nn.silu(kr[:, cs]))), cols)
        vs = [jax.nn.silu(vr[:, (2 * j + r) * HEAD:(2 * j + r + 1) * HEAD]) for j, r in heads]
        row = lambda arr3: [arr3[j][r:r + 1] for j, r in heads]
        xs_saved = [xcat[n * c:(n + 1) * c] for n in range(len(heads))] if saved_inverse else None
        new, outs, xs = _gdn_heads(list(states), [qk_[j][0] for j, _ in heads], [qk_[j][1] for j, _ in heads], vs,
                                   row(a3), row(b3), row(alog3), row(dtb3), rev, xs_saved)
        o = jnp.concatenate(outs, axis=1)
        return tuple(new), ((o,) if saved_inverse else (o, jnp.concatenate(xs, axis=0)))
    return f


def _hbm_spec():
    return pl.BlockSpec(memory_space=pltpu.HBM)


def all_gather(name, xs):
    nt = len(xs)

    def body(*refs):
        x_refs, out_refs = refs[:nt], refs[nt:2 * nt]
        send_sems, recv_sems, local_sems = refs[2 * nt:]
        x, y, c = lax.axis_index("x"), lax.axis_index("y"), lax.axis_index("c")
        me, sibling = (x, y, c), (x, y, 1 - c)
        chips = [(1 - x, y), (x, 1 - y), (1 - x, 1 - y)]

        def slab(t, px, py, pc):
            return out_refs[t].at[4 * px + 2 * py + pc]

        def copy(t, k, block, to, src=None):
            return pltpu.make_async_remote_copy(
                src_ref=slab(t, *block) if src is None else src, dst_ref=slab(t, *block),
                send_sem=send_sems.at[7 * t + k], recv_sem=recv_sems.at[7 * t + k], device_id=to, device_id_type=MESH)

        mine, first, passed = [], [], []
        for t in range(nt):
            mine.append(pltpu.make_async_copy(x_refs[t], slab(t, *me), local_sems.at[t]))
            mine[-1].start()
            cps = [copy(t, 0, me, sibling, src=x_refs[t])]
            cps += [copy(t, 1 + j, me, (*chip, c), src=x_refs[t]) for j, chip in enumerate(chips)]
            for cp in cps:
                cp.start()
            first += cps
        for j, chip in enumerate(chips):
            for t in range(nt):
                copy(t, 1 + j, (*chip, c), me).wait_recv()
                fw = copy(t, 4 + j, (*chip, c), sibling)
                fw.start()
                passed.append(fw)
        for t in range(nt):
            copy(t, 0, sibling, me).wait_recv()
            for j, chip in enumerate(chips):
                copy(t, 4 + j, (*chip, 1 - c), me).wait_recv()
        for cp in first + passed:
            cp.wait_send()
        for cp in mine:
            cp.wait()

    return pl.pallas_call(
        body, name=name, out_shape=[_sds((N_DEV,) + a.shape, a.dtype) for a in xs],
        in_specs=[_hbm_spec()] * nt, out_specs=[_hbm_spec()] * nt,
        scratch_shapes=[pltpu.SemaphoreType.DMA((7 * nt,)), pltpu.SemaphoreType.DMA((7 * nt,)),
                        pltpu.SemaphoreType.DMA((nt,))],
    )(*xs)


def _peers(x, y, c):
    out = []
    for k in range(1, N_DEV):
        px = 1 - x if k & 4 else x
        py = 1 - y if k & 2 else y
        pc = 1 - c if k & 1 else c
        out.append((px, py, pc))
    return out


def _exchange_copies(src_refs, land_refs, send_sems, recv_sems, scatter):
    x, y, c = lax.axis_index("x"), lax.axis_index("y"), lax.axis_index("c")
    me = 4 * x + 2 * y + c
    sends, recvs = [], []
    for t, (src, land) in enumerate(zip(src_refs, land_refs)):
        for k, (px, py, pc) in enumerate(_peers(x, y, c)):
            peer = 4 * px + 2 * py + pc
            sem = dict(send_sem=send_sems.at[7 * t + k], recv_sem=recv_sems.at[7 * t + k],
                       device_id=(px, py, pc), device_id_type=MESH)
            src_k = src.at[peer] if scatter else src
            sends.append(pltpu.make_async_remote_copy(src_ref=src_k, dst_ref=land.at[me], **sem))
            recvs.append(pltpu.make_async_remote_copy(src_ref=src_k, dst_ref=land.at[peer], **sem))
    return sends, recvs


def exchange_start(name, srcs, scatter, after, carry=None):
    nt = len(srcs)
    lands = [lax.empty(s.shape if scatter else (N_DEV,) + s.shape, s.dtype) for s in srcs]
    thru = list(srcs) + lands + ([carry] if carry is not None else [])
    n_thru = len(thru)

    def body(*refs):
        src_refs, land_refs = refs[:nt], refs[nt:2 * nt]
        send_sems, recv_sems = refs[n_thru + 1], refs[n_thru + 2]
        token = refs[-1]
        sends, _ = _exchange_copies(src_refs, land_refs, send_sems, recv_sems, scatter)
        for cp in sends:
            cp.start()
        token[...] = jnp.zeros_like(token)

    res = pl.pallas_call(
        body, name=name,
        out_shape=(pltpu.SemaphoreType.DMA((7 * nt,)), pltpu.SemaphoreType.DMA((7 * nt,)),
                   *[pltpu.HBM(a.shape, a.dtype) for a in thru], _sds((SUBLANES, LANES), F32)),
        in_specs=[_hbm_spec()] * n_thru + [pl.BlockSpec(memory_space=pl.ANY)],
        out_specs=(pl.BlockSpec(memory_space=pltpu.SEMAPHORE), pl.BlockSpec(memory_space=pltpu.SEMAPHORE),
                   *[_hbm_spec()] * n_thru, pl.BlockSpec(memory_space=pltpu.VMEM)),
        input_output_aliases={i: 2 + i for i in range(n_thru)},
        compiler_params=pltpu.CompilerParams(has_side_effects=pltpu.SideEffectType.DATAFLOW_SIDE_EFFECTING),
    )(*[pltpu.with_memory_space_constraint(a, pltpu.HBM) for a in thru], after)
    return (res[0], res[1], list(res[2:2 + nt]), list(res[2 + nt:2 + 2 * nt]), res[-1],
            res[2 + 2 * nt] if carry is not None else None)


def exchange_wait(name, started, after, scatter):
    send_sems, recv_sems, srcs, lands = started[:4]
    nt = len(srcs)

    def body(*refs):
        src_refs, land_refs = refs[:nt], refs[nt:2 * nt]
        sends, recvs = _exchange_copies(src_refs, land_refs, refs[2 * nt], refs[2 * nt + 1], scatter)
        for cp in sends:
            cp.wait_send()
        for cp in recvs:
            cp.wait_recv()

    hbm = lambda a: pltpu.HBM(a.shape, a.dtype)
    sem = pl.BlockSpec(memory_space=pltpu.SEMAPHORE)
    res = pl.pallas_call(
        body, name=name,
        out_shape=(*[hbm(a) for a in srcs], *[hbm(a) for a in lands]),
        in_specs=[_hbm_spec()] * (2 * nt) + [sem, sem, pl.BlockSpec(memory_space=pl.ANY)],
        out_specs=tuple([_hbm_spec()] * (2 * nt)),
        input_output_aliases={i: i for i in range(2 * nt)},
        compiler_params=pltpu.CompilerParams(has_side_effects=pltpu.SideEffectType.DATAFLOW_SIDE_EFFECTING),
    )(*srcs, *lands, send_sems, recv_sems, after)
    return list(res[:nt]), list(res[nt:])


def _pack(arrs, dtype, width, row_mult, lead=0):
    ld = arrs[0].shape[:lead]
    flat = jnp.concatenate([a.reshape(ld + (-1,)).astype(dtype) for a in arrs], axis=-1)
    n = flat.shape[-1]
    q = width * row_mult
    npad = -(-n // q) * q
    flat = jnp.pad(flat, [(0, 0)] * lead + [(0, npad - n)])
    return flat.reshape(ld + (npad // width, width))


def _unpack(buf, shapes, lead=0):
    ld = buf.shape[:lead]
    flat = buf.reshape(ld + (-1,))
    out, off = [], 0
    for s in shapes:
        n = int(np.prod(s))
        out.append(flat[..., off:off + n].reshape(ld + tuple(s)))
        off += n
    return out


def _pool_mats(seg_len):
    mats = np.zeros((len(POOL_WINDOWS), ROW_TILE, ROW_TILE), np.float32)
    for gi, win in enumerate(POOL_WINDOWS):
        for p in range(ROW_TILE):
            base = (p // seg_len) * seg_len
            q = p - base
            lo = min(max(q - win // 2, 0), seg_len - 1)
            hi = min(max(q + win - 1 - win // 2, 0), seg_len - 1)
            mats[gi, p, base + lo:base + hi + 1] = 1.0 / (hi - lo + 1)
    return mats


def kernel(x, c, ctx, c_ctx, w_ada, b_ada, norm_w, ev_w_in, ev_lb, ev_a_norm, ev_pool_w, ev_pool_scale, ev_w_out, od_w_in, od_conv, od_A_log, od_dt_bias, od_norm, od_w_out, ffn_w13, ffn_w2, loss_target, m_c_ctx, m_w_ada, m_b_ada, m_norm_w, m_ev_w_in, m_ev_lb, m_ev_a_norm, m_ev_pool_w, m_ev_pool_scale, m_ev_w_out, m_od_w_in, m_od_conv, m_od_A_log, m_od_dt_bias, m_od_norm, m_od_w_out, m_ffn_w13, m_ffn_w2, v_c_ctx, v_w_ada, v_b_ada, v_norm_w, v_ev_w_in, v_ev_lb, v_ev_a_norm, v_ev_pool_w, v_ev_pool_scale, v_ev_w_out, v_od_w_in, v_od_conv, v_od_A_log, v_od_dt_bias, v_od_norm, v_od_w_out, v_ffn_w13, v_ffn_w2):
    names = ["c_ctx", "w_ada", "b_ada", "norm_w", "ev_w_in", "ev_lb", "ev_a_norm", "ev_pool_w", "ev_pool_scale",
             "ev_w_out", "od_w_in", "od_conv", "od_A_log", "od_dt_bias", "od_norm", "od_w_out", "ffn_w13", "ffn_w2"]
    wts = dict(zip(names, [c_ctx, w_ada, b_ada, norm_w, ev_w_in, ev_lb, ev_a_norm, ev_pool_w, ev_pool_scale,
                           ev_w_out, od_w_in, od_conv, od_A_log, od_dt_bias, od_norm, od_w_out, ffn_w13, ffn_w2]))
    mom1 = dict(zip(names, [m_c_ctx, m_w_ada, m_b_ada, m_norm_w, m_ev_w_in, m_ev_lb, m_ev_a_norm, m_ev_pool_w,
                            m_ev_pool_scale, m_ev_w_out, m_od_w_in, m_od_conv, m_od_A_log, m_od_dt_bias, m_od_norm,
                            m_od_w_out, m_ffn_w13, m_ffn_w2]))
    mom2 = dict(zip(names, [v_c_ctx, v_w_ada, v_b_ada, v_norm_w, v_ev_w_in, v_ev_lb, v_ev_a_norm, v_ev_pool_w,
                            v_ev_pool_scale, v_ev_w_out, v_od_w_in, v_od_conv, v_od_A_log, v_od_dt_bias, v_od_norm,
                            v_od_w_out, v_ffn_w13, v_ffn_w2]))

    ax, ay, ac = lax.axis_index("x"), lax.axis_index("y"), lax.axis_index("c")
    me = 4 * ax + 2 * ay + ac

    seq, d = x.shape[1], x.shape[2]
    n_ctx = ctx.shape[1]
    t = n_ctx + seq
    nt = t // ROW_TILE
    nct = n_ctx // ROW_TILE
    assert n_ctx == ROW_TILE and seq % ROW_TILE == 0 and ROW_TILE % GRID_W == 0
    depth = w_ada.shape[0]
    aw = d // 2
    n_ah = aw // HEAD
    n_grp = len(POOL_WINDOWS)
    dg = aw // n_grp
    assert dg % LANES == 0
    n_kh = d // HEAD
    kw, vw = n_kh * HEAD, 2 * n_kh * HEAD
    n_gate = 8 * n_kh
    ffn_h = ffn_w2.shape[1] * N_DEV
    ada_loc = w_ada.shape[2]
    assert depth == 2

    bf = lambda w_: w_.astype(BF16)

    def gathered(started, after, name):
        srcs, lands = exchange_wait(name, started, after, False)
        return [lax.dynamic_update_index_in_dim(l_, s_, me, 0) for l_, s_ in zip(lands, srcs)]

    small_shapes = [(d,), norm_w.shape, ev_lb.shape, ev_pool_w.shape[1:], od_conv.shape[1:]]
    (g1,) = all_gather("ag_small", [_pack([c[0], norm_w, ev_lb, ev_pool_w[0], od_conv[0]], F32, LANES, SUBLANES)])
    c_all, nw_g, lb_g, pw_g, cv_g = _unpack(g1, small_shapes, lead=1)
    nw_full = nw_g.transpose(1, 2, 0, 3).reshape(depth, 4, d)
    lb_full = lb_g.transpose(1, 2, 0, 3).reshape(2, depth + 1, aw)
    pw_full = pw_g.transpose(1, 0, 2, 3).reshape(n_grp, dg, dg)
    cv_full = cv_g.transpose(1, 0, 2).reshape(C_CONV, 2 * kw + vw)

    g_ev_in, g_ev_out = all_gather("ag_weights_ev", [bf(ev_w_in[0]), bf(ev_w_out[0])])

    def cols_natural(g):
        return g.transpose(1, 0, 2).reshape(g.shape[1], N_DEV * g.shape[2])

    def rows_natural(g):
        return g.reshape(N_DEV * g.shape[1], g.shape[2])

    def col_weight(g):
        return (g, True) if g.shape[2] % LANES == 0 else (cols_natural(g), False)

    w_ev_in = col_weight(g_ev_in)
    w_ev_out = rows_natural(g_ev_out)
    w13, w2 = [None] * depth, [None] * depth

    b_loc = lax.dynamic_slice_in_dim(b_ada, me * ada_loc, ada_loc, axis=1).reshape(depth, 1, ada_loc)
    ada_cb = _pick(ada_loc, TN_PREFS)
    ada_grid = (depth, ada_loc // ada_cb)
    ada_args = [
        TArg(c_all, (N_DEV, d), lambda l, j: (0, 0), "const"),
        TArg(c_ctx.reshape(1, d), (1, d), lambda l, j: (0, 0), "par", (0, 1)),
        TArg(w_ada, (None, d, ada_cb), lambda l, j: (l, 0, j)),
        TArg(b_loc, (None, 1, ada_cb), lambda l, j: (l, 0, j)),
    ]
    (m_loc,) = tile_fwd("ada_fwd", f_ada, ada_grid, ada_args,
                        [((depth, 2 * N_DEV, ada_loc), F32, (None, 2 * N_DEV, ada_cb), lambda l, j: (l, 0, j))])
    (m_all,) = all_gather("ag_mod", [m_loc])
    mods = []
    for layer in range(depth):
        lat = lax.dynamic_index_in_dim(m_all[:, layer], me, axis=1, keepdims=False).reshape(6, d)
        cxt = lax.dynamic_index_in_dim(m_all[:, layer], N_DEV + me, axis=1, keepdims=False).reshape(6, d)
        mods.append(jnp.stack([cxt, lat]))

    gathers_done = mods[0][0, :1, :SUBLANES] + g_ev_out[0, :1, :SUBLANES].astype(F32)
    ag_ffn0 = exchange_start("ag_start_ffn0", [bf(ffn_w13[0]), bf(ffn_w2[0])], False, gathers_done)
    ag_l1 = exchange_start("ag_start_l1", [bf(od_w_in[0]), bf(od_w_out[0]), bf(ffn_w13[1]), bf(ffn_w2[1])], False,
                           gathers_done)
    mods[0] = mods[0] + (ag_ffn0[4][0, 0] + ag_l1[4][0, 0])

    lb_slots = [TArg(lb_full[:, j], (2, aw), lambda i: (0, 0)) for j in range(depth + 1)]
    (lb0,) = tile_fwd("lb_fwd", f_lb(0), (1,), lb_slots, [((2, aw), F32, (2, aw), lambda i: (0, 0))])
    lb0r = lb0.reshape(2, n_ah, 1, HEAD)

    full_row = lambda i: (i, 0)
    par0 = lambda i: (0, 0)

    def nm_args(xs, layer, slot):
        return [TArg(xs, (ROW_TILE, d), full_row),
                TArg(nw_full[layer, slot].reshape(1, d), (1, d), par0, "par", (0,)),
                TArg(mods[layer], (2, 6, d), lambda i: (0, 0, 0), "par", (0,))]

    def norm_mod(name, xs, layer, slot, si, ci):
        return tile_fwd(name, f_norm_mod(si, ci, nct, transposed=True), (nt,), nm_args(xs, layer, slot),
                        [((t, d), BF16, (ROW_TILE, d), full_row), ((d, t), BF16, (d, ROW_TILE), lambda i: (0, i))])

    def norm_mod_bwd(name, xs, layer, slot, si, ci, dh, carry):
        return tile_bwd(name, f_norm_mod(si, ci, nct, True), (nt,), nm_args(xs, layer, slot),
                        [(dh, (ROW_TILE, d), full_row), (carry, (ROW_TILE, d), full_row)])

    def gr_args(xs, ys, layer, slot):
        return [TArg(xs, (ROW_TILE, d), full_row, grad=False), TArg(ys, (ROW_TILE, d), full_row, gdtype=BF16),
                TArg(nw_full[layer, slot].reshape(1, d), (1, d), par0, "par", (0,)),
                TArg(mods[layer], (2, 6, d), lambda i: (0, 0, 0), "par", (0,))]

    def gate_res(name, xs, ys, layer, slot, gi):
        (o,) = tile_fwd(name, f_gate_res(gi, nct), (nt,), gr_args(xs, ys, layer, slot),
                        [((t, d), F32, (ROW_TILE, d), full_row)])
        return o

    def gate_res_bwd(name, xs, ys, layer, slot, gi, dx):
        return tile_bwd(name, f_gate_res(gi, nct), (nt,), gr_args(xs, ys, layer, slot),
                        [(dx, (ROW_TILE, d), full_row)])

    def rn_args(xs, ys, gate, norm):
        return [TArg(xs, (ROW_TILE, d), full_row), TArg(ys, (ROW_TILE, d), full_row, gdtype=BF16),
                TArg(nw_full[gate[0], gate[1]].reshape(1, d), (1, d), par0, "par", (0,)),
                TArg(mods[gate[0]], (2, 6, d), lambda i: (0, 0, 0), "par", (0,)),
                TArg(nw_full[norm[0], norm[1]].reshape(1, d), (1, d), par0, "par", (0,)),
                TArg(mods[norm[0]], (2, 6, d), lambda i: (0, 0, 0), "par", (0,))]

    def res_norm(name, xs, ys, gate, gi, norm, si, ci):
        return tile_fwd(name, f_res_norm(gi, si, ci, nct, transposed=True), (nt,), rn_args(xs, ys, gate, norm),
                        [((t, d), F32, (ROW_TILE, d), full_row), ((t, d), BF16, (ROW_TILE, d), full_row),
                         ((d, t), BF16, (d, ROW_TILE), lambda i: (0, i))])

    def res_norm_bwd(name, xs, ys, gate, gi, norm, si, ci, carry, dh, acc):
        dx, dy, dnw_g, dmod_g, dnw_n, dmod_n = tile_bwd(
            name, f_res_norm(gi, si, ci, nct), (nt,), rn_args(xs, ys, gate, norm),
            [(carry, (ROW_TILE, d), full_row), (dh, (ROW_TILE, d), full_row)])
        acc["norm_w"][gate[0]][gate[1]] = dnw_g
        acc["norm_w"][norm[0]][norm[1]] = dnw_n
        acc["mods"][gate[0]].append(dmod_g)
        acc["mods"][norm[0]].append(dmod_n)
        return dx, dy

    sw_rows = ROW_TILE // 2

    def sw_args(gu):
        return [TArg(gu, (sw_rows, 2 * ffn_h), full_row, gdtype=BF16)]

    def ffn_fwd(tag, h2, h2t, layer):
        gu = matmul(f"w13_{tag}", h2, w13[layer][0], "nn", slabs=w13[layer][1], out_dtype=BF16)
        (act,) = tile_fwd(f"swiglu_{tag}", f_swiglu, (t // sw_rows,), sw_args(gu),
                          [((t, ffn_h), BF16, (sw_rows, ffn_h), full_row)])
        return matmul(f"w2_{tag}", act, w2[layer], "nn"), (h2t, gu, act)

    def wgrad(name, a, dy, slabs=False, a_t=None):
        if a_t is not None:
            return matmul(name, a_t, dy, "nn", out_slabs=slabs, out_dtype=BF16)
        return matmul(name, a, dy, "tn", slabs=slabs, out_dtype=BF16)

    def col_grad(name, a, dy, slabs, a_t=None):
        g = wgrad(name, a, dy, slabs, a_t)
        return g if slabs else g.reshape(g.shape[0], N_DEV, g.shape[1] // N_DEV).transpose(1, 0, 2)

    def row_grad(name, a, dy):
        g = matmul(name, a, dy, "tn", out_dtype=BF16)
        return g.reshape(N_DEV, g.shape[0] // N_DEV, g.shape[1])

    def ffn_bwd(tag, saved, layer, dfo, acc):
        h2t, gu, act = saved
        dact = matmul(f"w2d_{tag}", dfo, w2[layer], "nt")
        acc["ffn_w2"][layer] = row_grad(f"w2w_{tag}", act, dfo)
        (dgu,) = tile_bwd(f"swiglub_{tag}", f_swiglu, (t // sw_rows,), sw_args(gu), [(dact, (sw_rows, ffn_h), full_row)])
        acc["ffn_w13"][layer] = col_grad(f"w13w_{tag}", None, dgu, w13[layer][1], h2t)
        return matmul(f"w13d_{tag}", dgu, w13[layer][0], "nt", slabs=w13[layer][1])

    n_a = t // A_CHUNK
    nca = n_ctx // A_CHUNK

    def a_tok(rev):
        if not rev:
            return lambda i: i
        return lambda i: jnp.where(i < nca, nca - 1 - i, n_a + nca - 1 - i)

    hb = _pick(n_ah, (HEADS_PER_STEP, 2, 1))
    n_hblk = n_ah // hb

    def hg_args(p, direction):
        tok = a_tok(direction == 1)
        blk = (A_CHUNK, hb * HEAD)
        return [TArg(p, blk, lambda h, i: (tok(i), h)),
                TArg(p, blk, lambda h, i: (tok(i), (1 + direction) * n_hblk + h)),
                TArg(p, blk, lambda h, i: (tok(i), 3 * n_hblk + h)),
                TArg(lb0r, (None, hb, 1, HEAD), lambda h, i: (direction, h, 0, 0), "par", (1,))]

    pmats = jnp.asarray(np.stack([_pool_mats(n_ctx), _pool_mats(GRID_W)]))

    def pool_args(p):
        return [TArg(p, (ROW_TILE, dg), lambda g, i: (i, 5 * n_grp + g)),
                TArg(pmats, (None, None, ROW_TILE, ROW_TILE), lambda g, i: (jnp.where(i < nct, 0, 1), g, 0, 0), "const"),
                TArg(pw_full, (None, dg, dg), lambda g, i: (g, 0, 0), "par", (1,)),
                TArg(ev_pool_scale, (1, dg), lambda g, i: (0, g), "par", (1,))]

    def ro_plan(gate_off, n_heads):
        per = _pick(n_heads, (8, 4, 2, 1))
        assert gate_off % per == 0
        return per, n_heads // per, gate_off // per

    def ro_args(o_f, o_b, gate_arr, gate_off, nw_arr, n_heads):
        per, _, goff = ro_plan(gate_off, n_heads)
        blk = (ROW_TILE, per * HEAD)
        return [TArg(o_f, blk, lambda h, i: (i, h)), TArg(o_b, blk, lambda h, i: (i, h), grad=False),
                TArg(gate_arr, blk, lambda h, i: (i, goff + h), gdtype=BF16),
                TArg(nw_arr, (1, HEAD), lambda h, i: (0, 0), "par", (0, 1))]

    def readout(name, o_f, o_b, gate_arr, gate_off, nw_arr, n_heads):
        per, nblk, _ = ro_plan(gate_off, n_heads)
        (o,) = tile_fwd(name, f_readout(per), (nblk, nt), ro_args(o_f, o_b, gate_arr, gate_off, nw_arr, n_heads),
                        [((t, n_heads * HEAD), BF16, (ROW_TILE, per * HEAD), lambda hh, i: (i, hh))])
        return o

    def readout_bwd(name, o_f, o_b, gate_arr, gate_off, nw_arr, n_heads, dout):
        per, nblk, _ = ro_plan(gate_off, n_heads)
        return tile_bwd(name, f_readout(per), (nblk, nt), ro_args(o_f, o_b, gate_arr, gate_off, nw_arr, n_heads),
                        [(dout, (ROW_TILE, per * HEAD), lambda hh, i: (i, hh))])

    def even_fwd(tag, h, ht):
        p = matmul(f"win_{tag}", h, w_ev_in[0], "nn", slabs=w_ev_in[1])
        outs, saves = [], []
        for direction in (0, 1):
            (o,), sv = scan_fwd(f"hgrn_{tag}_{direction}", f_hgrn2(direction == 1, hb), n_hblk, n_a, hg_args(p, direction),
                                [((t, aw), F32, (A_CHUNK, hb * HEAD), lambda hh, i, tok=a_tok(direction == 1): (tok(i), hh))], hb)
            outs.append(o)
            saves.append(sv)
        a_out = readout(f"ro_{tag}", outs[0], outs[1], p, 4 * n_ah, ev_a_norm, n_ah)
        (pooled,) = tile_fwd(f"pool_{tag}", f_pool, (n_grp, nt), pool_args(p),
                             [((t, aw), BF16, (ROW_TILE, dg), lambda g, i: (i, g))])
        cat = assemble(f"cat_{tag}", [[(a_out, aw, 0)], [(pooled, aw, 0)]], BF16)
        return matmul(f"wout_{tag}", cat, w_ev_out, "nn"), (ht, p, outs, saves, cat)

    def even_bwd(tag, saved, dy, acc):
        ht, p, outs, saves, cat = saved
        dcat = matmul(f"woutd_{tag}", dy, w_ev_out, "nt")
        acc["ev_w_out"] = row_grad(f"woutw_{tag}", cat, dy)
        do, dgate, d_anorm = readout_bwd(f"rob_{tag}", outs[0], outs[1], p, 4 * n_ah, ev_a_norm, n_ah, dcat)
        du, d_pw, d_ps = tile_bwd(f"poolb_{tag}", f_pool, (n_grp, nt), pool_args(p),
                                  [(dcat, (ROW_TILE, dg), lambda g, i: (i, n_grp + g))])
        dq, df, di, dlb = [], [], [], []
        for direction in (0, 1):
            r = scan_bwd(f"hgrnb_{tag}_{direction}", f_hgrn2(direction == 1, hb), n_hblk, n_a, hg_args(p, direction),
                         saves[direction],
                         [(do, (A_CHUNK, hb * HEAD), lambda hh, i, tok=a_tok(direction == 1): (tok(i), hh))])
            dq.append(r[0])
            df.append(r[1])
            di.append(r[2])
            dlb.append(r[3])
        sec = lambda arr, s: (arr, aw, s)
        dp = assemble(f"dp_{tag}", [[sec(dq[0], 0), sec(dq[1], 0)], [sec(df[0], 1)], [sec(df[1], 2)],
                                    [sec(di[0], 3), sec(di[1], 3)], [sec(dgate, 4)], [sec(du, 5)]], BF16)
        acc["ev_w_in"] = col_grad(f"winw_{tag}", None, dp, w_ev_in[1], ht)
        acc["rs_ev"] = exchange_start("rs_start_ev", [acc["ev_w_in"], acc["ev_w_out"]], True, dp, carry=w_ev_in[0])
        acc["ev_a_norm"] = d_anorm
        acc["ev_pool_w"] = d_pw
        acc["ev_pool_scale"] = d_ps
        acc["lb0"] = jnp.stack([dlb[0][0], dlb[1][1]]).reshape(2, aw)
        return matmul(f"wind_{tag}", dp, acc["rs_ev"][5], "nt", slabs=w_ev_in[1])

    n_c = t // C_CHUNK
    ncc = n_ctx // C_CHUNK

    def c_tok(rev):
        if not rev:
            return lambda i: i
        return lambda i: jnp.where(i < ncc, ncc - 1 - i, n_c + ncc - 1 - i)

    alog = od_A_log[0].reshape(2, n_kh, 2, 1)
    dtb = od_dt_bias[0].reshape(2, n_kh, 2, 1)

    khb = _pick(n_kh, (2 * HEADS_PER_STEP, HEADS_PER_STEP, 2, 1))
    n_kblk = n_kh // khb

    def gd_args(z, gates, direction):
        tok = c_tok(direction == 1)
        gblk = (None, khb, None, 2, C_CHUNK)
        sblk = (None, khb, 2, 1)
        return [TArg(z, (C_CHUNK, khb * HEAD), lambda kb, i: (tok(i), kb)),
                TArg(z, (C_CHUNK, khb * HEAD), lambda kb, i: (tok(i), n_kblk + kb)),
                TArg(z, (C_CHUNK, khb * 2 * HEAD), lambda kb, i: (tok(i), n_kblk + kb)),
                TArg(gates, gblk, lambda kb, i: (direction, kb, tok(i), 0, 0)),
                TArg(gates, gblk, lambda kb, i: (2 + direction, kb, tok(i), 0, 0)),
                TArg(alog, sblk, lambda kb, i: (direction, kb, 0, 0), "par", (1,)),
                TArg(dtb, sblk, lambda kb, i: (direction, kb, 0, 0), "par", (1,))]

    def odd_fwd(tag, h, ht):
        pm = matmul(f"win_{tag}", h, w_od_main, "nn")
        pg = matmul(f"wgate_{tag}", h, w_od_gate, "nn")
        z = conv_fwd(f"conv_{tag}", pm, cv_full, 2 * kw + vw, nct)
        gates = pg.reshape(n_c, C_CHUNK, 4, n_kh, 2).transpose(2, 3, 0, 4, 1)
        outs, saves = [], []
        for direction in (0, 1):
            xrows = 2 * khb * C_CHUNK
            (o, xinv), sv = scan_fwd(
                f"gdn_{tag}_{direction}", f_gdn(direction == 1, khb, False), n_kblk, n_c, gd_args(z, gates, direction),
                [((t, vw), F32, (C_CHUNK, khb * 2 * HEAD), lambda kb, i, tok=c_tok(direction == 1): (tok(i), kb)),
                 ((n_kblk, n_c, xrows, C_CHUNK), F32, (None, None, xrows, C_CHUNK), lambda kb, i: (kb, i, 0, 0))],
                2 * khb)
            outs.append(o)
            saves.append((sv, xinv))
        n_vh = 2 * n_kh
        yo = readout(f"ro_{tag}", outs[0], outs[1], pm, 2 * n_kh + n_vh, od_norm, n_vh)
        return matmul(f"wout_{tag}", yo, w_od_out, "nn"), (ht, pm, z, gates, outs, saves, yo)

    def odd_bwd(tag, saved, dy, acc):
        ht, pm, z, gates, outs, saves, yo = saved
        n_vh = 2 * n_kh
        dyo = matmul(f"woutd_{tag}", dy, w_od_out, "nt")
        acc["od_w_out"] = row_grad(f"woutw_{tag}", yo, dy)
        do, dzg, d_onorm = readout_bwd(f"rob_{tag}", outs[0], outs[1], pm, 2 * n_kh + n_vh, od_norm, n_vh, dyo)
        dq, dk, dv, dga, dgb, dal, ddt = [], [], [], [], [], [], []
        for direction in (0, 1):
            sv, xinv = saves[direction]
            xarg = TArg(xinv, (None, None, 2 * khb * C_CHUNK, C_CHUNK), lambda kb, i: (kb, i, 0, 0), "const")
            r = scan_bwd(f"gdnb_{tag}_{direction}", f_gdn(direction == 1, khb, True), n_kblk, n_c,
                         gd_args(z, gates, direction) + [xarg], sv,
                         [(do, (C_CHUNK, khb * 2 * HEAD), lambda kb, i, tok=c_tok(direction == 1): (tok(i), kb))])
            for lst, v_ in zip((dq, dk, dv, dga, dgb, dal, ddt), r):
                lst.append(v_)
        dz = assemble(f"dz_{tag}", [[(dq[0], kw, 0), (dq[1], kw, 0)], [(dk[0], kw, 1), (dk[1], kw, 1)],
                                    [(dv[0], vw, 1), (dv[1], vw, 1)]], F32)
        dpm, d_conv = conv_bwd(f"convb_{tag}", pm, cv_full, dz, 2 * kw + vw, nct, dzg)
        dgates = jnp.stack([dga[0][0], dga[1][1], dgb[0][2], dgb[1][3]])
        dpg = dgates.transpose(2, 4, 0, 1, 3).reshape(t, n_gate).astype(BF16)
        dh = matmul(f"wgated_{tag}", dpg, w_od_gate, "nt")
        dh = matmul(f"wind_{tag}", dpm, w_od_main, "nt", add=dh)
        dw_in = jnp.concatenate([wgrad(f"winw_{tag}", None, dpm, a_t=ht), wgrad(f"wgatew_{tag}", None, dpg, a_t=ht)],
                                axis=1)
        acc["od_w_in"] = dw_in.reshape(d, N_DEV, dw_in.shape[1] // N_DEV).transpose(1, 0, 2)
        acc["od_norm"] = d_onorm
        acc["od_conv"] = d_conv
        acc["od_A_log"] = jnp.stack([dal[0][0], dal[1][1]]).reshape(1, 2, n_vh)
        acc["od_dt_bias"] = jnp.stack([ddt[0][0], ddt[1][1]]).reshape(1, 2, n_vh)
        return dh

    xs0 = jnp.concatenate([ctx[0], x[0]], axis=0)
    h0, h0t = norm_mod("nm1_l0", xs0, 0, 0, 0, 1)
    y0, sv_e = even_fwd("l0", h0, h0t)
    xs1, h1, h1t = res_norm("rn1_l0", xs0, y0, (0, 1), 2, (0, 2), 3, 4)
    g_w13a, g_w2a = gathered(ag_ffn0, xs1, "ag_wait_ffn0")
    w13[0], w2[0] = col_weight(g_w13a), rows_natural(g_w2a)
    y1, sv_f0 = ffn_fwd("l0", h1, h1t, 0)
    xs2, h2, h2t = res_norm("rn2_l0", xs1, y1, (0, 3), 5, (1, 0), 0, 1)
    g_od_in, g_od_out, g_w13b, g_w2b = gathered(ag_l1, xs2, "ag_wait_l1")
    w_od_in = cols_natural(g_od_in)
    w_od_main, w_od_gate = w_od_in[:, :2 * kw + 2 * vw], w_od_in[:, 2 * kw + 2 * vw:]
    w_od_out = rows_natural(g_od_out)
    w13[1], w2[1] = col_weight(g_w13b), rows_natural(g_w2b)
    y2, sv_o = odd_fwd("l1", h2, h2t)
    xs3, h3, h3t = res_norm("rn1_l1", xs2, y2, (1, 1), 2, (1, 2), 3, 4)
    y3, sv_f1 = ffn_fwd("l1", h3, h3t, 1)
    xs4 = gate_res("gr2_l1", xs3, y3, 1, 3, 5)
    loss_loc, dx4 = loss_kernel("loss", xs4, loss_target[0], nct)
    loss = lax.psum(loss_loc, ("x", "y", "c"))

    acc = {"norm_w": [[None] * 4 for _ in range(depth)], "mods": [[] for _ in range(depth)],
           "ffn_w13": [None] * depth, "ffn_w2": [None] * depth}
    dy3, acc["norm_w"][1][3], dmod = gate_res_bwd("gr2b_l1", xs3, y3, 1, 3, 5, dx4)
    acc["mods"][1].append(dmod)
    dh3 = ffn_bwd("l1", sv_f1, 1, dy3, acc)
    rs_ffn1 = exchange_start("rs_start_ffn1", [acc["ffn_w13"][1], acc["ffn_w2"][1]], True, dh3)
    mods[1] = mods[1] + rs_ffn1[4][0, 0]
    dx3, dy2 = res_norm_bwd("rn1b_l1", xs2, y2, (1, 1), 2, (1, 2), 3, 4, dx4, dh3, acc)
    dh2 = odd_bwd("l1", sv_o, dy2, acc)
    rs_od = exchange_start("rs_start_od", [acc["od_w_in"], acc["od_w_out"]], True, dh2)
    mods[0] = mods[0] + rs_od[4][0, 0]
    dx2, dy1 = res_norm_bwd("rn2b_l0", xs1, y1, (0, 3), 5, (1, 0), 0, 1, dx3, dh2, acc)
    dh1 = ffn_bwd("l0", sv_f0, 0, dy1, acc)
    rs_ffn0 = exchange_start("rs_start_ffn0", [acc["ffn_w13"][0], acc["ffn_w2"][0]], True, dh1)
    mods[0] = mods[0] + rs_ffn0[4][0, 0]
    dx1, dy0 = res_norm_bwd("rn1b_l0", xs0, y0, (0, 1), 2, (0, 2), 3, 4, dx2, dh1, acc)
    dh0 = even_bwd("l0", sv_e, dy0, acc)
    rs_ev = acc["rs_ev"]
    dx0, acc["norm_w"][0][0], dmod = norm_mod_bwd("nm1b_l0", xs0, 0, 0, 0, 1, dh0, dx1)
    acc["mods"][0].append(dmod)
    grad_x = dx0[n_ctx:].reshape(1, seq, d)

    (d_lb_slots) = tile_bwd("lb_bwd", f_lb(0), (1,), lb_slots, [(acc["lb0"], (2, aw), lambda i: (0, 0))])
    d_ev_lb = jnp.stack(d_lb_slots, axis=1)

    dmods = jnp.stack([functools.reduce(jnp.add, acc["mods"][layer]) for layer in range(depth)])
    (dm_all,) = all_gather("ag_dmod", [dmods.reshape(depth * 2 * 6, d)])
    dm_all = dm_all.reshape(N_DEV, depth, 2, 6 * d)
    dm_cols = lax.dynamic_slice_in_dim(dm_all, me * ada_loc, ada_loc, axis=3)
    dm_loc = jnp.concatenate([dm_cols[:, :, 1].transpose(1, 0, 2), dm_cols[:, :, 0].transpose(1, 0, 2)], axis=1)
    d_cctx_part, d_w_ada, d_b_loc = tile_bwd("ada_bwd", f_ada, ada_grid, ada_args,
                                             [(dm_loc, (None, 2 * N_DEV, ada_cb), lambda l, j: (l, 0, j))])

    d_b_full = lax.dynamic_update_slice_in_dim(jnp.zeros_like(b_ada), d_b_loc.reshape(depth, ada_loc), me * ada_loc, axis=1)
    d_nw = jnp.stack([jnp.stack([acc["norm_w"][layer][s].reshape(d) for s in range(4)]) for layer in range(depth)])
    small_grads = [d_cctx_part.reshape(d), d_b_full, d_nw, d_ev_lb, acc["ev_a_norm"], acc["ev_pool_w"],
                   acc["ev_pool_scale"], acc["od_conv"], acc["od_A_log"], acc["od_dt_bias"], acc["od_norm"]]
    sg_shapes = [a.shape for a in small_grads]
    (sg,) = all_gather("ag_smallgrads", [_pack(small_grads, F32, FLAT_W, SUBLANES)])
    sg_sum = sum_leading("sum_smallgrads", sg, F32)
    (g_cctx, g_bada, g_nw, g_lb, g_anorm, g_pw, g_ps, g_conv, g_alog, g_dtb, g_onorm) = _unpack(sg_sum, sg_shapes)

    def my_cols(full, axis):
        loc = full.shape[axis] // N_DEV
        return lax.dynamic_slice_in_dim(full, me * loc, loc, axis=axis)

    grads = {
        "c_ctx": g_cctx, "w_ada": d_w_ada, "b_ada": g_bada, "norm_w": my_cols(g_nw, 2), "ev_lb": my_cols(g_lb, 2),
        "ev_a_norm": g_anorm, "ev_pool_w": my_cols(g_pw, 1)[None], "ev_pool_scale": g_ps,
        "od_conv": my_cols(g_conv, 1)[None], "od_A_log": g_alog, "od_dt_bias": g_dtb, "od_norm": g_onorm,
    }

    def reduced(started, tags_, name):
        srcs, lands = exchange_wait(name, started, sg_sum, True)
        out = []
        for tg, s_, l_ in zip(tags_, srcs, lands):
            own = lax.dynamic_index_in_dim(s_, me, 0, keepdims=True)
            out.append(sum_leading(f"rs_sum_{tg}", lax.dynamic_update_slice_in_dim(l_, own, me, 0), F32))
        return out

    g_w13b, g_w2b = reduced(rs_ffn1, ["w13b", "w2b"], "rs_wait_ffn1")
    g_od_in, g_od_out = reduced(rs_od, ["od_in", "od_out"], "rs_wait_od")
    g_w13a, g_w2a = reduced(rs_ffn0, ["w13a", "w2a"], "rs_wait_ffn0")
    g_ev_in, g_ev_out = reduced(rs_ev, ["ev_in", "ev_out"], "rs_wait_ev")
    grads["ev_w_in"], grads["ev_w_out"], grads["od_w_in"], grads["od_w_out"] = (g_ev_in[None], g_ev_out[None],
                                                                                g_od_in[None], g_od_out[None])
    grads["ffn_w13"] = jnp.stack([g_w13a, g_w13b])
    grads["ffn_w2"] = jnp.stack([g_w2a, g_w2b])

    big_names = ["w_ada", "ev_w_in", "ev_w_out", "od_w_in", "od_w_out", "ffn_w13", "ffn_w2"]
    small_names = [n_ for n_ in names if n_ not in big_names]
    gl = {n_: grads[n_].reshape(wts[n_].shape) for n_ in names}
    delta, new_m, new_v = {}, {}, {}
    for n_ in big_names:
        shp = wts[n_].shape
        res = adamw(f"adamw_{n_}", _rows2d(gl[n_]), _rows2d(wts[n_]), _rows2d(mom1[n_]), _rows2d(mom2[n_]))
        delta[n_], new_m[n_], new_v[n_] = (r_.reshape(shp) for r_ in res)
    shapes = [wts[n_].shape for n_ in small_names]
    pk = lambda dct: _pack([dct[n_] for n_ in small_names], F32, FLAT_W, SUBLANES)
    res = adamw("adamw_small", pk(gl), pk(wts), pk(mom1), pk(mom2))
    for dct, r_ in zip((delta, new_m, new_v), res):
        for n_, a_ in zip(small_names, _unpack(r_, shapes)):
            dct[n_] = a_
    return (loss, grad_x, *[gl[n_] for n_ in names], *[delta[n_] for n_ in names], *[new_m[n_] for n_ in names],
            *[new_v[n_] for n_ in names])
```
